```python
import math
import jax, jax.numpy as jnp
from jax import lax
import numpy as np

D_MODEL = 1024
BATCH = 2
SEQ = 8192
DEPTH = 2

N_META = 16
GRID_W = 64
HEAD_DIM = 64
NA_WIDTH = D_MODEL // 2
NA_HEADS = NA_WIDTH // HEAD_DIM
NA_WIN_ROWS = 8
NA_WIN_COLS = 16
RWKV_WIDTH = D_MODEL // 2
RWKV_HEADS = RWKV_WIDTH // HEAD_DIM
RWKV_W_RANK = 64
RWKV_A_RANK = 64
RWKV_G_RANK = 128
MIX_WIDTH = NA_WIDTH + RWKV_WIDTH
SHIFT_COLS = 3 * RWKV_WIDTH + RWKV_W_RANK + RWKV_A_RANK + RWKV_G_RANK
IN_COLS = 3 * NA_WIDTH + SHIFT_COLS
S5_GROUP_CH = 16
S5_GROUPS = D_MODEL // S5_GROUP_CH
S5_STATE = 64
MOE_GROUPS = 4
MOE_PER_GROUP = 8
MOE_EXPERTS = MOE_GROUPS * MOE_PER_GROUP
MOE_TOP_K = 2
MOE_D_FF = D_MODEL // 2
MOE_BLOCK = 128
NORM_EPS = 1e-6
RWKV_GN_EPS = 64e-5
NEG_INF = -1e30

kernel_name = 'hybrid_na_rwkv7_s5_hmoe_encoder'


def rms_norm(x, gain):
    xf = x.astype(jnp.float32)
    y = xf * lax.rsqrt(jnp.mean(xf * xf, axis=-1, keepdims=True) + NORM_EPS)
    return (y * gain.astype(jnp.float32)).astype(x.dtype)


def centred_token_shift(p, mu):
    pad = jnp.pad(p, ((0, 0), (1, 1), (0, 0)))
    nb = 0.5 * (pad[:, :-2] + pad[:, 2:])
    return p + mu.astype(p.dtype) * (nb - p)


def neighbourhood_attention(q, k, v, rpb):
    Bsz, L, H, Dh = q.shape
    T = L - N_META
    rows = T // GRID_W
    wr = min(NA_WIN_ROWS, rows)
    scale = Dh ** -0.5
    f32 = jnp.float32
    grid = lambda t: t[:, N_META:].reshape(Bsz, rows, GRID_W, H, Dh)
    qm, km, vm = q[:, :N_META], k[:, :N_META], v[:, :N_META]
    qg, kg, vg = grid(q), grid(k), grid(v)
    r_ids = jnp.arange(rows)
    row_idx = jnp.clip(r_ids - wr // 2, 0, rows - wr)[:, None] + jnp.arange(wr)[None, :]
    kb = kg[:, row_idx].reshape(Bsz, rows, wr * GRID_W, H, Dh)
    vb = vg[:, row_idx].reshape(Bsz, rows, wr * GRID_W, H, Dh)
    c_ids = jnp.arange(GRID_W)
    c_start = jnp.clip(c_ids - NA_WIN_COLS // 2, 0, GRID_W - NA_WIN_COLS)
    in_band = (c_ids[None, :] >= c_start[:, None]) & (c_ids[None, :] < c_start[:, None] + NA_WIN_COLS)
    dr = row_idx - r_ids[:, None] + NA_WIN_ROWS - 1
    dc = jnp.clip(c_ids[None, :] - c_ids[:, None] + NA_WIN_COLS - 1, 0, 2 * NA_WIN_COLS - 2)
    bias = rpb.astype(f32)[:, dr[:, None, :, None], dc[None, :, None, :]]
    bias = jnp.where(in_band[None, None, :, None, :], bias, NEG_INF).reshape(H, rows, GRID_W, wr * GRID_W)
    s_grid = jnp.einsum('brqhd,brkhd->bhrqk', qg, kb).astype(f32) * scale + bias[None]
    s_meta = jnp.einsum('brqhd,bmhd->bhrqm', qg, km).astype(f32) * scale
    p = jax.nn.softmax(jnp.concatenate([s_meta, s_grid], axis=-1), axis=-1).astype(v.dtype)
    o_grid = (jnp.einsum('bhrqm,bmhd->brqhd', p[..., :N_META], vm)
              + jnp.einsum('bhrqk,brkhd->brqhd', p[..., N_META:], vb))
    s_mm = jnp.einsum('bqhd,bkhd->bhqk', qm, km).astype(f32) * scale
    o_meta = jnp.einsum('bhqk,bkhd->bqhd', jax.nn.softmax(s_mm, axis=-1).astype(v.dtype), vm)
    return jnp.concatenate([o_meta, o_grid.reshape(Bsz, T, H, Dh)], axis=1)


def _rwkv7_step(S, inp):
    w_t, kk_t, a_t, k_t, v_t, r_t = inp
    sa = jnp.einsum('dbhvk,dbhk->dbhv', S, -kk_t)
    S = (S * w_t[..., None, :] + sa[..., :, None] * (kk_t * a_t)[..., None, :]
         + v_t[..., :, None] * k_t[..., None, :])
    y = jnp.einsum('dbhvk,dbhk->dbhv', S, r_t)
    return S, y


def rwkv7_bidir(r, k, v, w_lo, a_lo, g_lo, w0, w_up, a0, a_up, g_up, k_k, k_a, r_k, lnx_g, lnx_b):
    Bsz, L, C = r.shape
    out_dtype = r.dtype
    f = lambda t: t.astype(jnp.float32)
    r, k, v, w_lo, a_lo, g_lo = f(r), f(k), f(v), f(w_lo), f(a_lo), f(g_lo)
    w_log = -jax.nn.softplus(-(f(w0)[:, None, None, :]
                               + jnp.einsum('blr,drc->dblc', jnp.tanh(w_lo), f(w_up)))) - 0.5
    decay = jnp.exp(-jnp.exp(w_log))
    a = jax.nn.sigmoid(f(a0)[:, None, None, :] + jnp.einsum('blr,drc->dblc', a_lo, f(a_up)))
    g = jax.nn.sigmoid(g_lo) @ f(g_up)
    hd = lambda t: t.reshape(t.shape[:-1] + (RWKV_HEADS, HEAD_DIM))
    kk = hd(k * f(k_k))
    kk = kk / jnp.maximum(jnp.sqrt(jnp.sum(kk * kk, axis=-1, keepdims=True)), 1e-12)
    k_dir = hd(k[None] * (1.0 + (a - 1.0) * f(k_a)))
    rh, vh = hd(r), hd(v)

    def both(t):
        return jnp.stack([t, jnp.flip(t, axis=1)])

    def per(t):
        return jnp.stack([t[0], jnp.flip(t[1], axis=1)])

    xs = (per(hd(decay)), both(kk), per(hd(a)), per(k_dir), both(vh), both(rh))
    xs = tuple(jnp.moveaxis(t, 2, 0) for t in xs)
    S0 = jnp.zeros((2, Bsz, RWKV_HEADS, HEAD_DIM, HEAD_DIM), jnp.float32)
    _, ys = lax.scan(_rwkv7_step, S0, xs)
    y = per(jnp.moveaxis(ys, 0, 2)).sum(axis=0)
    mean = jnp.mean(y, axis=-1, keepdims=True)
    var = jnp.mean(jnp.square(y - mean), axis=-1, keepdims=True)
    y = ((y - mean) * lax.rsqrt(var + RWKV_GN_EPS)).reshape(Bsz, L, C) * f(lnx_g) + f(lnx_b)
    bonus = (jnp.sum(rh[None] * k_dir * f(r_k), axis=-1, keepdims=True) * vh[None]).sum(axis=0)
    return ((y + bonus.reshape(Bsz, L, C)) * g).astype(out_dtype)


def na_rwkv_mixer(hn, w_in, w_out, rpb, mu, w0, w_up, a0, a_up, g_up, k_k, k_a, r_k, lnx_g, lnx_b):
    Bsz, L, _ = hn.shape
    proj = hn @ w_in
    q, k, v, rest = jnp.split(proj, [NA_WIDTH, 2 * NA_WIDTH, 3 * NA_WIDTH], axis=-1)
    rest = centred_token_shift(rest, mu)
    C = RWKV_WIDTH
    r, kr, vr, w_lo, a_lo, g_lo = jnp.split(
        rest, [C, 2 * C, 3 * C, 3 * C + RWKV_W_RANK, 3 * C + RWKV_W_RANK + RWKV_A_RANK], axis=-1)
    heads = lambda t: t.reshape(Bsz, L, NA_HEADS, HEAD_DIM)
    na = neighbourhood_attention(heads(q), heads(k), heads(v), rpb).reshape(Bsz, L, NA_WIDTH)
    rw = rwkv7_bidir(r, kr, vr, w_lo, a_lo, g_lo, w0, w_up, a0, a_up, g_up, k_k, k_a, r_k, lnx_g, lnx_b)
    return jnp.concatenate([na, rw], axis=-1) @ w_out


def _complex_linear_combine(e1, e2):
    a1r, a1i, b1r, b1i = e1
    a2r, a2i, b2r, b2i = e2
    return (a2r * a1r - a2i * a1i, a2r * a1i + a2i * a1r,
            a2r * b1r - a2i * b1i + b2r, a2r * b1i + a2i * b1r + b2i)


def s5_mixer(hn, b_re, b_im, lam_re, lam_im, log_step, c_re, c_im, d, w_glu):
    Bsz, L, Dm = hn.shape
    f32 = jnp.float32
    u = hn.astype(f32)
    ug = u.reshape(Bsz, L, S5_GROUPS, S5_GROUP_CH)
    lr, li = lam_re.astype(f32), lam_im.astype(f32)
    step = jnp.exp(log_step.astype(f32))[..., None]
    mag = jnp.exp(lr * step)
    ab_re, ab_im = mag * jnp.cos(li * step), mag * jnp.sin(li * step)
    den = lr * lr + li * li
    z_re = ((ab_re - 1.0) * lr + ab_im * li) / den
    z_im = (ab_im * lr - (ab_re - 1.0) * li) / den
    br, bi = b_re.astype(f32)[None], b_im.astype(f32)[None]
    bb_re = z_re[..., None] * br - z_im[..., None] * bi
    bb_im = z_re[..., None] * bi + z_im[..., None] * br

    def per(t):
        return jnp.stack([t[0], jnp.flip(t[1], axis=1)])

    bu_re = per(jnp.einsum('dgpi,blgi->dblgp', bb_re, ug))
    bu_im = per(jnp.einsum('dgpi,blgi->dblgp', bb_im, ug))
    a_re = jnp.broadcast_to(ab_re[:, None, None], (2, 1, L, S5_GROUPS, S5_STATE))
    a_im = jnp.broadcast_to(ab_im[:, None, None], (2, 1, L, S5_GROUPS, S5_STATE))
    _, _, s_re, s_im = lax.associative_scan(_complex_linear_combine, (a_re, a_im, bu_re, bu_im), axis=2)
    y_dir = (jnp.einsum('dgip,dblgp->dblgi', c_re.astype(f32), s_re)
             - jnp.einsum('dgip,dblgp->dblgi', c_im.astype(f32), s_im))
    y = per(y_dir).sum(axis=0).reshape(Bsz, L, Dm) + d.astype(f32) * u
    hcat = jax.nn.gelu(y).astype(hn.dtype) @ w_glu
    return hcat[..., :Dm] * jax.nn.sigmoid(hcat[..., Dm:])


def hierarchical_moe(h, w_group, b_group, w_expert, b_expert, w1, w3, w2):
    Bsz, L, Dm = h.shape
    n_tok = Bsz * L
    n_assign = n_tok * MOE_TOP_K
    f32 = jnp.float32
    hf = h.reshape(n_tok, Dm)
    pg = jax.nn.softmax((hf @ w_group).astype(f32) + b_group.astype(f32), axis=-1)
    grp = jnp.argmax(pg, axis=-1)
    p_grp = jnp.take_along_axis(pg, grp[:, None], axis=-1)
    le_all = jnp.einsum('nd,gde->nge', hf, w_expert).astype(f32) + b_expert.astype(f32)
    le = jnp.take_along_axis(le_all, grp[:, None, None], axis=1)[:, 0]
    top_p, top_i = lax.top_k(jax.nn.softmax(le, axis=-1), MOE_TOP_K)
    gates = p_grp * top_p / jnp.sum(top_p, axis=-1, keepdims=True)
    flat_e = (grp[:, None] * MOE_PER_GROUP + top_i).reshape(-1)
    flat_g = gates.reshape(-1)
    flat_tok = jnp.repeat(jnp.arange(n_tok), MOE_TOP_K)
    order = jnp.argsort(flat_e)
    se, stok, sg = flat_e[order], flat_tok[order], flat_g[order]
    counts = jnp.bincount(flat_e, length=MOE_EXPERTS)
    padded = (counts + MOE_BLOCK - 1) // MOE_BLOCK * MOE_BLOCK
    pad_end = jnp.cumsum(padded)
    pad_start = pad_end - padded
    start = jnp.cumsum(counts) - counts
    dest = pad_start[se] + jnp.arange(n_assign) - start[se]
    n_blocks = -(-n_assign // MOE_BLOCK) + MOE_EXPERTS
    n_rows = n_blocks * MOE_BLOCK
    xbuf = jnp.zeros((n_rows, Dm), h.dtype).at[dest].set(hf[stok])
    blk_e = jnp.minimum(jnp.searchsorted(pad_end, jnp.arange(n_blocks) * MOE_BLOCK, side='right'),
                        MOE_EXPERTS - 1)

    def expert_block(args):
        xb, e = args
        return (jax.nn.silu(xb @ w1[e]) * (xb @ w3[e])) @ w2[e]

    ybuf = lax.map(expert_block, (xbuf.reshape(n_blocks, MOE_BLOCK, Dm), blk_e)).reshape(n_rows, Dm)
    y = ybuf[dest] * sg[:, None].astype(h.dtype)
    return jax.ops.segment_sum(y, stok, num_segments=n_tok).reshape(Bsz, L, Dm)


def setup_inputs(seed: int = 0) -> dict:
    key = jax.random.key(seed)
    keys = jax.random.split(key, 48)
    counter = [0]

    def nxt():
        kk = keys[counter[0]]
        counter[0] += 1
        return kk

    nrm = lambda shape, s: jax.random.normal(nxt(), shape, jnp.float32) * s
    uni = lambda shape, lo, hi: jax.random.uniform(nxt(), shape, jnp.float32, lo, hi)
    ne, no = (DEPTH + 1) // 2, DEPTH // 2
    C, G, P, I = RWKV_WIDTH, S5_GROUPS, S5_STATE, S5_GROUP_CH
    E, F = MOE_EXPERTS, MOE_D_FF
    return {
        'x': nrm((BATCH, SEQ, D_MODEL), 1.0),
        'meta_tokens': nrm((N_META, D_MODEL), 1.0),
        'norm_mix': 1.0 + nrm((DEPTH, D_MODEL), 0.05),
        'norm_ffn': 1.0 + nrm((DEPTH, D_MODEL), 0.05),
        'norm_final': 1.0 + nrm((D_MODEL,), 0.05),
        'mix_w_in': nrm((ne, D_MODEL, IN_COLS), D_MODEL ** -0.5),
        'mix_w_out': nrm((ne, MIX_WIDTH, D_MODEL), MIX_WIDTH ** -0.5),
        'na_rpb': nrm((ne, NA_HEADS, 2 * NA_WIN_ROWS - 1, 2 * NA_WIN_COLS - 1), 0.5),
        'rwkv_mu': uni((ne, SHIFT_COLS), 0.0, 1.0),
        'rwkv_w0': uni((ne, 2, C), -6.0, -1.0),
        'rwkv_w_up': nrm((ne, 2, RWKV_W_RANK, C), 0.1),
        'rwkv_a0': nrm((ne, 2, C), 0.5),
        'rwkv_a_up': nrm((ne, 2, RWKV_A_RANK, C), 0.1),
        'rwkv_g_up': nrm((ne, RWKV_G_RANK, C), RWKV_G_RANK ** -0.5),
        'rwkv_k_k': 0.85 + nrm((ne, C), 0.05),
        'rwkv_k_a': 1.0 + nrm((ne, C), 0.05),
        'rwkv_r_k': nrm((ne, RWKV_HEADS, HEAD_DIM), 0.1),
        'rwkv_lnx_g': 1.0 + nrm((ne, C), 0.05),
        'rwkv_lnx_b': nrm((ne, C), 0.02),
        's5_b_re': nrm((no, G, P, I), (2 * I) ** -0.5),
        's5_b_im': nrm((no, G, P, I), (2 * I) ** -0.5),
        's5_lambda_re': -0.5 + nrm((no, 2, G, P), 0.01),
        's5_lambda_im': math.pi * jnp.arange(P, dtype=jnp.float32) + nrm((no, 2, G, P), 0.01),
        's5_log_step': uni((no, 2, G), math.log(1e-3), math.log(1e-1)),
        's5_c_re': nrm((no, 2, G, I, P), P ** -0.5),
        's5_c_im': nrm((no, 2, G, I, P), P ** -0.5),
        's5_d': nrm((no, D_MODEL), 1.0),
        's5_w_glu': nrm((no, D_MODEL, 2 * D_MODEL), D_MODEL ** -0.5),
        'moe_w_group': nrm((DEPTH, D_MODEL, MOE_GROUPS), D_MODEL ** -0.5),
        'moe_b_group': nrm((DEPTH, MOE_GROUPS), 0.01),
        'moe_w_expert': nrm((DEPTH, MOE_GROUPS, D_MODEL, MOE_PER_GROUP), D_MODEL ** -0.5),
        'moe_b_expert': nrm((DEPTH, MOE_GROUPS, MOE_PER_GROUP), 0.01),
        'moe_w1': nrm((DEPTH, E, D_MODEL, F), D_MODEL ** -0.5),
        'moe_w3': nrm((DEPTH, E, D_MODEL, F), D_MODEL ** -0.5),
        'moe_w2': nrm((DEPTH, E, F, D_MODEL), F ** -0.5),
    }


def reference(x, meta_tokens, norm_mix, norm_ffn, norm_final, mix_w_in, mix_w_out, na_rpb, rwkv_mu,
              rwkv_w0, rwkv_w_up, rwkv_a0, rwkv_a_up, rwkv_g_up, rwkv_k_k, rwkv_k_a, rwkv_r_k,
              rwkv_lnx_g, rwkv_lnx_b, s5_b_re, s5_b_im, s5_lambda_re, s5_lambda_im, s5_log_step,
              s5_c_re, s5_c_im, s5_d, s5_w_glu, moe_w_group, moe_b_group, moe_w_expert, moe_b_expert,
              moe_w1, moe_w3, moe_w2):
    Bsz = x.shape[0]
    meta = jnp.broadcast_to(meta_tokens.astype(x.dtype)[None], (Bsz, N_META, D_MODEL))
    h = jnp.concatenate([meta, x], axis=1)
    for layer in range(DEPTH):
        i = layer // 2
        hn = rms_norm(h, norm_mix[layer])
        if layer % 2 == 0:
            mix = na_rwkv_mixer(hn, mix_w_in[i], mix_w_out[i], na_rpb[i], rwkv_mu[i], rwkv_w0[i],
                                rwkv_w_up[i], rwkv_a0[i], rwkv_a_up[i], rwkv_g_up[i], rwkv_k_k[i],
                                rwkv_k_a[i], rwkv_r_k[i], rwkv_lnx_g[i], rwkv_lnx_b[i])
        else:
            mix = s5_mixer(hn, s5_b_re[i], s5_b_im[i], s5_lambda_re[i], s5_lambda_im[i], s5_log_step[i],
                           s5_c_re[i], s5_c_im[i], s5_d[i], s5_w_glu[i])
        h = h + mix
        h = h + hierarchical_moe(rms_norm(h, norm_ffn[layer]), moe_w_group[layer], moe_b_group[layer],
                                 moe_w_expert[layer], moe_b_expert[layer], moe_w1[layer],
                                 moe_w3[layer], moe_w2[layer])
    return rms_norm(h, norm_final)[:, N_META:]
```

```python
import functools
import math

import jax
import jax.numpy as jnp
from jax import lax
from jax.experimental import pallas as pl
from jax.experimental.pallas import tpu as pltpu

F32 = jnp.float32
BF16 = jnp.bfloat16
I32 = jnp.int32

N_META = 16
GRID_W = 64
HEAD_DIM = 64
NA_WIN_ROWS = 8
NA_WIN_COLS = 16
S5_GROUP_CH = 16
S5_STATE = 64
MOE_GROUPS = 4
MOE_PER_GROUP = 8
MOE_EXPERTS = MOE_GROUPS * MOE_PER_GROUP
NORM_EPS = 1e-6
RWKV_GN_EPS = 64e-5
NEG_INF = -1e30

LANES = 128
SUBLANES_BF16 = 16
VMEM_LIMIT_BYTES = 56 * 1024 * 1024

MOE_TILE = 256
ROUTER_LANES = 128


def _cparams(*sem):
    return pltpu.CompilerParams(dimension_semantics=sem, vmem_limit_bytes=VMEM_LIMIT_BYTES)


def _row_tile(n, target):
    best = None
    for t in range(SUBLANES_BF16, min(n, target) + 1, SUBLANES_BF16):
        if n % t == 0:
            best = t
    assert best is not None, (n, target)
    return best


def _rms(x, gain):
    ms = jnp.mean(x * x, axis=-1, keepdims=True)
    return (x * lax.rsqrt(ms + NORM_EPS)) * gain


def _split_bf16(x):
    hi = x.astype(BF16)
    lo = (x - hi.astype(F32)).astype(BF16)
    return hi, lo


def _dot(a, b):
    return jnp.dot(a, b, preferred_element_type=F32)


def _dot_nt(a, b):
    return lax.dot_general(a, b, (((1,), (1,)), ((), ())), preferred_element_type=F32)


def _norm_inproj_kernel(h_ref, g_ref, w_ref, qkv_ref, rest_ref, xn_ref, *, n_qkv, chunk):
    xn_ref[...] = _rms(h_ref[...], g_ref[...]).astype(BF16)
    n_all = w_ref.shape[1]
    for c in range(0, n_all, chunk):
        y = _dot(xn_ref[...], w_ref[:, c:c + chunk])
        if c < n_qkv:
            qkv_ref[:, c:c + chunk] = y.astype(BF16)
        else:
            rest_ref[:, c - n_qkv:c - n_qkv + chunk] = y


def norm_inproj(h2, gain, w_bf16, n_qkv):
    n, d = h2.shape
    n_all = w_bf16.shape[1]
    tm = _row_tile(n, 608)
    chunk = 256
    assert n_qkv % chunk == 0 and n_all % chunk == 0
    return pl.pallas_call(
        functools.partial(_norm_inproj_kernel, n_qkv=n_qkv, chunk=chunk),
        grid=(n // tm,),
        in_specs=[
            pl.BlockSpec((tm, d), lambda i: (i, 0)),
            pl.BlockSpec((1, d), lambda i: (0, 0)),
            pl.BlockSpec((d, n_all), lambda i: (0, 0)),
        ],
        out_specs=[
            pl.BlockSpec((tm, n_qkv), lambda i: (i, 0)),
            pl.BlockSpec((tm, n_all - n_qkv), lambda i: (i, 0)),
        ],
        out_shape=[
            jax.ShapeDtypeStruct((n, n_qkv), BF16),
            jax.ShapeDtypeStruct((n, n_all - n_qkv), F32),
        ],
        scratch_shapes=[pltpu.VMEM((tm, d), BF16)],
        compiler_params=_cparams("parallel"),
        name="norm_inproj",
    )(h2, gain.reshape(1, d), w_bf16)


def _outproj_kernel(h_ref, na_ref, rw_ref, wa_ref, wb_ref, o_ref):
    acc = _dot(na_ref[...], wa_ref[...])
    acc = acc + _dot(rw_ref[...], wb_ref[...])
    o_ref[...] = h_ref[...] + acc


def outproj_residual(h2, na, rw, w_out_bf16):
    n, d = h2.shape
    ka, kb = na.shape[1], rw.shape[1]
    tm = _row_tile(n, 608)
    return pl.pallas_call(
        _outproj_kernel,
        grid=(n // tm,),
        in_specs=[
            pl.BlockSpec((tm, d), lambda i: (i, 0)),
            pl.BlockSpec((tm, ka), lambda i: (i, 0)),
            pl.BlockSpec((tm, kb), lambda i: (i, 0)),
            pl.BlockSpec((ka, d), lambda i: (0, 0)),
            pl.BlockSpec((kb, d), lambda i: (0, 0)),
        ],
        out_specs=pl.BlockSpec((tm, d), lambda i: (i, 0)),
        out_shape=jax.ShapeDtypeStruct((n, d), F32),
        compiler_params=_cparams("parallel"),
        name="outproj_residual",
    )(h2, na, rw, w_out_bf16[:ka], w_out_bf16[ka:])


def _router_kernel(h_ref, g_ref, whi_ref, wlo_ref, b_ref, xn_ref, route_ref):
    xn = _rms(h_ref[...], g_ref[...])
    x_hi, x_lo = _split_bf16(xn)
    xn_ref[...] = x_hi
    logits = (_dot(x_hi, whi_ref[...]) + _dot(x_hi, wlo_ref[...]) + _dot(x_lo, whi_ref[...])
              + b_ref[...])
    tm = logits.shape[0]
    lane = lax.broadcasted_iota(I32, (tm, ROUTER_LANES), 1)
    big = jnp.int32(ROUTER_LANES)

    is_g = lane < MOE_GROUPS
    lg = jnp.where(is_g, logits, -jnp.inf)
    eg = jnp.where(is_g, jnp.exp(lg - jnp.max(lg, axis=-1, keepdims=True)), 0.0)
    pg = eg / jnp.sum(eg, axis=-1, keepdims=True)
    p_grp = jnp.max(pg, axis=-1, keepdims=True)
    grp = jnp.min(jnp.where(is_g & (pg == p_grp), lane, big), axis=-1, keepdims=True)

    lo_lane = MOE_GROUPS + MOE_PER_GROUP * grp
    is_e = (lane >= lo_lane) & (lane < lo_lane + MOE_PER_GROUP)
    le = jnp.where(is_e, logits, -jnp.inf)
    ee = jnp.where(is_e, jnp.exp(le - jnp.max(le, axis=-1, keepdims=True)), 0.0)
    pe = jnp.where(is_e, ee / jnp.sum(ee, axis=-1, keepdims=True), -1.0)
    p1 = jnp.max(pe, axis=-1, keepdims=True)
    i1 = jnp.min(jnp.where(pe == p1, lane, big), axis=-1, keepdims=True)
    pe2 = jnp.where(lane == i1, -1.0, pe)
    p2 = jnp.max(pe2, axis=-1, keepdims=True)
    i2 = jnp.min(jnp.where(pe2 == p2, lane, big), axis=-1, keepdims=True)
    denom = p1 + p2
    g1 = p_grp * p1 / denom
    g2 = p_grp * p2 / denom
    e1 = (i1 - MOE_GROUPS).astype(F32)
    e2 = (i2 - MOE_GROUPS).astype(F32)
    route_ref[...] = jnp.where(lane == 0, e1, jnp.where(lane == 1, e2, jnp.where(lane == 2, g1, g2)))


def moe_router(h2, gain, w_group, b_group, w_expert, b_expert):
    n, d = h2.shape
    n_r = MOE_GROUPS + MOE_EXPERTS
    w_r = jnp.concatenate([w_group, jnp.transpose(w_expert, (1, 0, 2)).reshape(d, MOE_EXPERTS)], axis=1)
    w_r = jnp.pad(w_r.astype(F32), ((0, 0), (0, ROUTER_LANES - n_r)))
    w_hi, w_lo = _split_bf16(w_r)
    b_r = jnp.pad(jnp.concatenate([b_group, b_expert.reshape(-1)]).astype(F32), (0, ROUTER_LANES - n_r))
    tm = _row_tile(n, 608)
    return pl.pallas_call(
        _router_kernel,
        grid=(n // tm,),
        in_specs=[
            pl.BlockSpec((tm, d), lambda i: (i, 0)),
            pl.BlockSpec((1, d), lambda i: (0, 0)),
            pl.BlockSpec((d, ROUTER_LANES), lambda i: (0, 0)),
            pl.BlockSpec((d, ROUTER_LANES), lambda i: (0, 0)),
            pl.BlockSpec((1, ROUTER_LANES), lambda i: (0, 0)),
        ],
        out_specs=[
            pl.BlockSpec((tm, d), lambda i: (i, 0)),
            pl.BlockSpec((tm, ROUTER_LANES), lambda i: (i, 0)),
        ],
        out_shape=[
            jax.ShapeDtypeStruct((n, d), BF16),
            jax.ShapeDtypeStruct((n, ROUTER_LANES), F32),
        ],
        compiler_params=_cparams("parallel"),
        name="moe_router",
    )(h2, gain.reshape(1, d), w_hi, w_lo, b_r.reshape(1, ROUTER_LANES))


def _expert_kernel(blk_e_ref, n_used_ref, x_ref, sg_ref, w1_ref, w3_ref, w2_ref, y_ref,
                   w1b_ref, w3b_ref, w2b_ref):
    i = pl.program_id(0)
    used = i < n_used_ref[0]
    prev_e = blk_e_ref[jnp.maximum(i - 1, 0)]
    fresh = (i == 0) | (blk_e_ref[i] != prev_e)

    @pl.when(used & fresh)
    def _():
        w1b_ref[...] = w1_ref[...].astype(BF16)
        w3b_ref[...] = w3_ref[...].astype(BF16)
        w2b_ref[...] = w2_ref[...].astype(BF16)

    @pl.when(used)
    def _():
        x = x_ref[...]
        a = _dot(x, w1b_ref[...])
        b = _dot(x, w3b_ref[...])
        hmid = (a * jax.nn.sigmoid(a) * b).astype(BF16)
        y_ref[...] = _dot(hmid, w2b_ref[...]) * sg_ref[...]

    @pl.when(jnp.logical_not(used))
    def _():
        y_ref[...] = jnp.zeros_like(y_ref)


def moe_experts(xbuf, sg_rows, blk_e, n_used, w1, w3, w2):
    n_rows, d = xbuf.shape
    f = w1.shape[2]
    n_blocks = n_rows // MOE_TILE
    last = n_blocks - 1

    def row_map(i, blk_e_ref, n_used_ref):
        return (jnp.where(i < n_used_ref[0], i, last), 0)

    def w_map(i, blk_e_ref, n_used_ref):
        return (blk_e_ref[i], 0, 0)

    grid_spec = pltpu.PrefetchScalarGridSpec(
        num_scalar_prefetch=2,
        grid=(n_blocks,),
        in_specs=[
            pl.BlockSpec((MOE_TILE, d), row_map),
            pl.BlockSpec((MOE_TILE, 1), row_map),
            pl.BlockSpec((None, d, f), w_map),
            pl.BlockSpec((None, d, f), w_map),
            pl.BlockSpec((None, f, d), w_map),
        ],
        out_specs=pl.BlockSpec((MOE_TILE, d), row_map),
        scratch_shapes=[
            pltpu.VMEM((d, f), BF16),
            pltpu.VMEM((d, f), BF16),
            pltpu.VMEM((f, d), BF16),
        ],
    )
    return pl.pallas_call(
        _expert_kernel,
        grid_spec=grid_spec,
        out_shape=jax.ShapeDtypeStruct((n_rows, d), F32),
        compiler_params=_cparams("arbitrary"),
        name="moe_experts",
    )(blk_e, n_used, xbuf, sg_rows, w1, w3, w2)


def hierarchical_moe_residual(h2, gain, w_group, b_group, w_expert, b_expert, w1, w3, w2):
    n, d = h2.shape
    xn, route = moe_router(h2, gain, w_group, b_group, w_expert, b_expert)
    flat_e = route[:, 0:2].astype(I32).reshape(-1)
    flat_g = route[:, 2:4].reshape(-1)
    n_assign = 2 * n
    onehot = (flat_e[:, None] == jnp.arange(MOE_EXPERTS, dtype=I32)[None, :]).astype(I32)
    csum = jnp.cumsum(onehot, axis=0)
    pos = jnp.sum((csum - onehot) * onehot, axis=1)
    counts = csum[-1]
    padded = (counts + MOE_TILE - 1) // MOE_TILE * MOE_TILE
    pad_end = jnp.cumsum(padded)
    pad_start = pad_end - padded
    dest = pad_start[flat_e] + pos
    n_blocks = -(-n_assign // MOE_TILE) + MOE_EXPERTS
    n_rows = n_blocks * MOE_TILE
    blk_e = jnp.minimum(
        jnp.searchsorted(pad_end, jnp.arange(n_blocks, dtype=I32) * MOE_TILE, side="right"),
        MOE_EXPERTS - 1).astype(I32)
    n_used = (pad_end[-1] // MOE_TILE).astype(I32).reshape(1)
    flat_tok = jnp.repeat(jnp.arange(n, dtype=I32), 2)
    tok_of_row = jnp.zeros((n_rows,), I32).at[dest].set(flat_tok)
    sg_rows = jnp.zeros((n_rows,), F32).at[dest].set(flat_g)
    xbuf = jnp.take(xn, tok_of_row, axis=0)
    ybuf = moe_experts(xbuf, sg_rows.reshape(n_rows, 1), blk_e, n_used, w1, w3, w2)
    y2 = jnp.take(ybuf, dest, axis=0).reshape(n, 2, d)
    return h2 + y2[:, 0] + y2[:, 1]


def _final_norm_kernel(h_ref, g_ref, o_ref):
    o_ref[...] = _rms(h_ref[...], g_ref[...])


def final_norm(h3, gain):
    b, l, d = h3.shape
    t = l - N_META
    tm = _row_tile(t, 512)
    return pl.pallas_call(
        _final_norm_kernel,
        grid=(b, t // tm),
        in_specs=[
            pl.BlockSpec((None, pl.Element(tm), pl.Element(d)),
                         lambda bi, i: (bi, pl.multiple_of(N_META + i * tm, SUBLANES_BF16), 0)),
            pl.BlockSpec((1, d), lambda bi, i: (0, 0)),
        ],
        out_specs=pl.BlockSpec((None, tm, d), lambda bi, i: (bi, i, 0)),
        out_shape=jax.ShapeDtypeStruct((b, t, d), F32),
        compiler_params=_cparams("parallel", "parallel"),
        name="final_norm",
    )(h3, gain.reshape(1, d))


NA_QROWS = 8
NA_KROWS = 3 * NA_QROWS


def _na_kernel(q_ref, kw_ref, vw_ref, qm_ref, km_ref, vm_ref, bias_ref, o_ref, om_ref, *, rows, scale):
    blk = pl.program_id(1)
    tq = GRID_W
    n_pairs = q_ref.shape[1] // LANES
    base = jnp.clip(NA_QROWS * blk - NA_QROWS, 0, rows - NA_KROWS)
    lane = lax.broadcasted_iota(I32, (tq, LANES), 1)
    halves = [lane < HEAD_DIM, lane >= HEAD_DIM]

    def attend(qp, kp, vp, kmp, vmp, bias):
        s_g = _dot_nt(qp, kp) * scale
        if bias is not None:
            s_g = s_g + bias
        s_m = _dot_nt(qp, kmp) * scale
        m = jnp.maximum(jnp.max(s_g, axis=-1, keepdims=True), jnp.max(s_m, axis=-1, keepdims=True))
        p_g = jnp.exp(s_g - m)
        p_m = jnp.exp(s_m - m)
        den = jnp.sum(p_g, axis=-1, keepdims=True) + jnp.sum(p_m, axis=-1, keepdims=True)
        o = _dot(p_g.astype(BF16), vp) + _dot(p_m.astype(BF16), vmp)
        return o / den

    def row_body(j, carry):
        r = NA_QROWS * blk + j
        start = jnp.clip(r - NA_WIN_ROWS // 2, 0, rows - NA_WIN_ROWS)
        s_idx = start - r + (NA_WIN_ROWS - 1)
        koff = pl.multiple_of((start - base) * GRID_W, GRID_W)
        qoff = pl.multiple_of(j * tq, tq)
        for p in range(n_pairs):
            cols = slice(p * LANES, (p + 1) * LANES)
            q_pair = q_ref[pl.ds(qoff, tq), cols]
            kp = kw_ref[pl.ds(koff, NA_WIN_ROWS * GRID_W), cols]
            vp = vw_ref[pl.ds(koff, NA_WIN_ROWS * GRID_W), cols]
            kmp = km_ref[:, cols]
            vmp = vm_ref[:, cols]
            outs = []
            for hh in range(2):
                qp = jnp.where(halves[hh], q_pair, jnp.zeros_like(q_pair))
                outs.append(attend(qp, kp, vp, kmp, vmp, bias_ref[2 * p + hh, s_idx]))
            o_ref[pl.ds(qoff, tq), cols] = jnp.where(halves[0], outs[0], outs[1]).astype(o_ref.dtype)
        return carry

    lax.fori_loop(0, NA_QROWS, row_body, 0)

    @pl.when(blk == 0)
    def _():
        lane_m = lax.broadcasted_iota(I32, (N_META, LANES), 1)
        for p in range(n_pairs):
            cols = slice(p * LANES, (p + 1) * LANES)
            q_pair = qm_ref[:, cols]
            kmp = km_ref[:, cols]
            vmp = vm_ref[:, cols]
            outs = []
            for hh in range(2):
                sel = (lane_m < HEAD_DIM) if hh == 0 else (lane_m >= HEAD_DIM)
                qp = jnp.where(sel, q_pair, jnp.zeros_like(q_pair))
                s_m = _dot_nt(qp, kmp) * scale
                p_m = jnp.exp(s_m - jnp.max(s_m, axis=-1, keepdims=True))
                den = jnp.sum(p_m, axis=-1, keepdims=True)
                outs.append(_dot(p_m.astype(BF16), vmp) / den)
            om_ref[:, cols] = jnp.where(lane_m < HEAD_DIM, outs[0], outs[1]).astype(om_ref.dtype)


def _na_bias_table(rpb):
    h = rpb.shape[0]
    c_ids = jnp.arange(GRID_W)
    c_start = jnp.clip(c_ids - NA_WIN_COLS // 2, 0, GRID_W - NA_WIN_COLS)
    in_band = (c_ids[None, :] >= c_start[:, None]) & (c_ids[None, :] < c_start[:, None] + NA_WIN_COLS)
    dc = jnp.clip(c_ids[None, :] - c_ids[:, None] + NA_WIN_COLS - 1, 0, 2 * NA_WIN_COLS - 2)
    tab = jnp.where(in_band[None, None], rpb.astype(F32)[:, :, dc], NEG_INF)
    win = jnp.stack([tab[:, s:s + NA_WIN_ROWS] for s in range(NA_WIN_ROWS)], axis=1)
    return jnp.transpose(win, (0, 1, 3, 2, 4)).reshape(h, NA_WIN_ROWS, GRID_W, NA_WIN_ROWS * GRID_W)


def na_attention(qkv, rpb):
    b, l, w3 = qkv.shape
    w = w3 // 3
    t = l - N_META
    rows = t // GRID_W
    assert rows * GRID_W == t and rows % NA_QROWS == 0 and rows >= NA_KROWS
    tq = NA_QROWS * GRID_W
    tk = NA_KROWS * GRID_W
    bias = _na_bias_table(rpb)
    al = SUBLANES_BF16

    def q_map(bi, i):
        return (bi, pl.multiple_of(N_META + i * tq, al), 0)

    def kv_map(col):
        def f(bi, i):
            base = jnp.clip(NA_QROWS * i - NA_QROWS, 0, rows - NA_KROWS)
            return (bi, pl.multiple_of(N_META + base * GRID_W, al), col)
        return f

    def meta_map(col):
        return lambda bi, i: (bi, 0, col)

    el = pl.Element
    grid_out, meta_out = pl.pallas_call(
        functools.partial(_na_kernel, rows=rows, scale=HEAD_DIM ** -0.5),
        grid=(b, rows // NA_QROWS),
        in_specs=[
            pl.BlockSpec((None, el(tq), el(w)), q_map),
            pl.BlockSpec((None, el(tk), el(w)), kv_map(w)),
            pl.BlockSpec((None, el(tk), el(w)), kv_map(2 * w)),
            pl.BlockSpec((None, el(N_META), el(w)), meta_map(0)),
            pl.BlockSpec((None, el(N_META), el(w)), meta_map(w)),
            pl.BlockSpec((None, el(N_META), el(w)), meta_map(2 * w)),
            pl.BlockSpec(bias.shape, lambda bi, i: (0, 0, 0, 0)),
        ],
        out_specs=[
            pl.BlockSpec((None, tq, w), lambda bi, i: (bi, i, 0)),
            pl.BlockSpec((None, N_META, w), lambda bi, i: (bi, 0, 0)),
        ],
        out_shape=[
            jax.ShapeDtypeStruct((b, t, w), BF16),
            jax.ShapeDtypeStruct((b, N_META, w), BF16),
        ],
        compiler_params=_cparams("parallel", "arbitrary"),
        name="na_attention",
    )(qkv, qkv, qkv, qkv, qkv, qkv, bias)
    return jnp.concatenate([meta_out, grid_out], axis=1)


RWKV_CHUNK = 64
RWKV_HALO = 8


def _split3_bf16(x):
    p1 = x.astype(BF16)
    r1 = x - p1.astype(F32)
    p2 = r1.astype(BF16)
    p3 = (r1 - p2.astype(F32)).astype(BF16)
    return p1, p2, p3


def _mm1(a, b):
    return _dot(a.astype(BF16), b.astype(BF16))


def _mm3(a, b):
    ah, al = _split_bf16(a)
    bh, bl = _split_bf16(b)
    return _dot(ah, bh) + _dot(ah, bl) + _dot(al, bh)


def _mm1_nt(a, b):
    return _dot_nt(a.astype(BF16), b.astype(BF16))


def _mm3_nt(a, b):
    ah, al = _split_bf16(a)
    bh, bl = _split_bf16(b)
    return _dot_nt(ah, bh) + _dot_nt(ah, bl) + _dot_nt(al, bh)


def _exact_left(mat_bf16, x):
    p1, p2, p3 = _split3_bf16(x)
    return _dot(mat_bf16, p1) + _dot(mat_bf16, p2) + _dot(mat_bf16, p3)


def _exact_right(x, mat_bf16):
    p1, p2, p3 = _split3_bf16(x)
    return _dot(p1, mat_bf16) + _dot(p2, mat_bf16) + _dot(p3, mat_bf16)


def _head_block_ones(width):
    ri = lax.broadcasted_iota(I32, (width, width), 0) // HEAD_DIM
    ci = lax.broadcasted_iota(I32, (width, width), 1) // HEAD_DIM
    return (ri == ci).astype(BF16)


def _stack_heads(x, m0):
    z = jnp.zeros_like(x)
    return jnp.concatenate([jnp.where(m0, x, z), jnp.where(m0, z, x)], axis=0)


def _rwkv_chunk_pair(kkp, rp, ki, bi, kipc, bipc, v, pc, st, sign):
    c = kkp.shape[0]
    c2 = 2 * c
    lane = lax.broadcasted_iota(I32, (c, LANES), 1)
    m0 = lane < HEAD_DIM
    lhs = jnp.concatenate([_stack_heads(kkp, m0), _stack_heads(rp, m0)], axis=0)
    rhs = jnp.concatenate([_stack_heads(ki, m0), _stack_heads(bi, m0)], axis=0)
    l4 = _mm3_nt(lhs, rhs)
    t_i = lax.broadcasted_iota(I32, (c2, c2), 0) % c
    s_i = lax.broadcasted_iota(I32, (c2, c2), 1) % c
    rel = (t_i - s_i) * sign
    strict = rel > 0
    incl = rel >= 0
    m_kk = jnp.where(strict, l4[0:c2, 0:c2], 0.0)
    n1 = jnp.where(strict, l4[0:c2, c2:2 * c2], 0.0)
    m_rk = jnp.where(incl, l4[c2:2 * c2, 0:c2], 0.0)
    m_rb = jnp.where(incl, l4[c2:2 * c2, c2:2 * c2], 0.0)

    vs = _stack_heads(v, m0)
    kt = _mm3_nt(lhs, st)
    x = kt[0:c2] + _mm3(m_kk, vs)
    powers = [n1]
    span = 1
    while span * 2 < c:
        powers.append(_mm3(powers[-1], powers[-1]))
        span *= 2
    for npow in reversed(powers[1:]):
        x = x + _mm3(npow, x)
    u = x - _mm3(n1, x)
    ys = kt[c2:2 * c2] + _mm3(m_rk, vs) - _mm3(m_rb, u)
    y = ys[0:c] + ys[c:c2]
    upd_l = jnp.transpose(jnp.concatenate([vs, -u], axis=0))
    upd_r = jnp.concatenate([_stack_heads(kipc, m0), _stack_heads(bipc, m0)], axis=0)
    st_new = st * pc + _mm3(upd_l, upd_r)
    return y, st_new


def _softplus(z):
    return jnp.maximum(z, 0.0) + jnp.log(1.0 + jnp.exp(-jnp.abs(z)))


def _rwkv_scan_kernel(x_ref, xp_ref, xn_ref, mu_ref, w0_ref, a0_ref, wah_ref, wal_ref, gup_ref,
                      kk_ref, ka_ref, rk_ref, y_ref, bonus_ref, g_ref, st_ref, *, seq_len, width):
    d = pl.program_id(1)
    i = pl.program_id(2)
    n_chunks = pl.num_programs(2)
    c = RWKV_CHUNK
    chunk = i + d * (n_chunks - 1 - 2 * i)
    sign = 1 - 2 * d
    n_pairs = width // LANES

    @pl.when(i == 0)
    def _():
        st_ref[...] = jnp.zeros_like(st_ref)

    valid = jnp.minimum(c, seq_len - chunk * c)
    n_cols = x_ref.shape[1]
    row = lax.broadcasted_iota(I32, (c, n_cols), 0)
    x = jnp.where(row < valid, x_ref[...], 0.0)
    prev_row = jnp.where(chunk > 0, xp_ref[RWKV_HALO - 1:RWKV_HALO, :], 0.0)
    next_row = jnp.where(chunk < n_chunks - 1, xn_ref[0:1, :], 0.0)
    x_prev = jnp.where(row == 0, prev_row, pltpu.roll(x, 1, 0))
    x_next = jnp.where(row == c - 1, next_row, pltpu.roll(x, c - 1, 0))
    xs = x + mu_ref[...] * (0.5 * (x_prev + x_next) - x)
    xs = jnp.where(row < valid, xs, 0.0)

    r = xs[:, 0:width]
    k = xs[:, width:2 * width]
    v = xs[:, 2 * width:3 * width]
    wa = xs[:, 3 * width:3 * width + LANES]
    g_lo = xs[:, 3 * width + LANES:3 * width + 2 * LANES]

    lane_wa = lax.broadcasted_iota(I32, (c, LANES), 1)
    xwa = jnp.where(lane_wa < LANES // 2, jnp.tanh(wa), wa)
    xh, xl = _split_bf16(xwa)
    la = _dot(xh, wah_ref[...]) + _dot(xh, wal_ref[...]) + _dot(xl, wah_ref[...])
    w_log = -_softplus(-(w0_ref[...] + la[:, 0:width])) - 0.5
    rowv = lax.broadcasted_iota(I32, (c, width), 0) < valid
    logw = jnp.where(rowv, -jnp.exp(w_log), 0.0)
    a = jax.nn.sigmoid(a0_ref[...] + la[:, width:2 * width])

    ones_h = _head_block_ones(width)
    kk0 = k * kk_ref[...]
    ss = _exact_right(kk0 * kk0, ones_h)
    kk = kk0 / jnp.maximum(jnp.sqrt(ss), 1e-12)
    kdir = k * (1.0 + (a - 1.0) * ka_ref[...])
    b = kk * a

    t_i = lax.broadcasted_iota(I32, (c, c), 0)
    s_i = lax.broadcasted_iota(I32, (c, c), 1)
    tri = ((t_i - s_i) * sign >= 0).astype(BF16)
    cl = _exact_left(tri, logw)
    last = jnp.where(d == 0, cl[c - 1:c, :], cl[0:1, :])
    e_x = jnp.exp(cl - logw)
    e_i = jnp.exp(cl)
    e_n = jnp.exp(-cl)
    pcr = jnp.exp(last - cl)
    pc = jnp.exp(last)
    kkp = kk * e_x
    rp = r * e_i
    ki = kdir * e_n
    bi = b * e_n
    kipc = kdir * pcr
    bipc = b * pcr

    bonus_ref[...] = _exact_right(r * kdir * rk_ref[...], ones_h) * v
    g_ref[...] = _mm1(jax.nn.sigmoid(g_lo), gup_ref[...]).astype(g_ref.dtype)

    for p in range(n_pairs):
        cols = slice(p * LANES, (p + 1) * LANES)
        y, st_new = _rwkv_chunk_pair(kkp[:, cols], rp[:, cols], ki[:, cols], bi[:, cols], kipc[:, cols],
                                     bipc[:, cols], v[:, cols], pc[:, cols], st_ref[p], sign)
        y_ref[:, cols] = y
        st_ref[p] = st_new


def _rwkv_finish_kernel(y_ref, bonus_ref, g_ref, lg_ref, lb_ref, o_ref):
    width = o_ref.shape[1]
    ones_h = _head_block_ones(width)
    y = y_ref[0] + y_ref[1]
    mean = _exact_right(y, ones_h) * (1.0 / HEAD_DIM)
    yc = y - mean
    var = _exact_right(yc * yc, ones_h) * (1.0 / HEAD_DIM)
    yn = yc * lax.rsqrt(var + RWKV_GN_EPS) * lg_ref[...] + lb_ref[...]
    o_ref[...] = ((yn + bonus_ref[0] + bonus_ref[1]) * g_ref[...].astype(F32)).astype(o_ref.dtype)


def rwkv_mix(rest, mu, w0, w_up, a0, a_up, g_up, k_k, k_a, r_k, lnx_g, lnx_b):
    bsz, l, n_cols = rest.shape
    width = w0.shape[1]
    rank = w_up.shape[1]
    assert n_cols == 3 * width + 2 * LANES and 2 * rank == LANES and l % RWKV_HALO == 0
    c = RWKV_CHUNK
    n_chunks = -(-l // c)
    per = c // RWKV_HALO
    n_halo = l // RWKV_HALO
    zeros = jnp.zeros((2, rank, width), F32)
    w_wa = jnp.concatenate([jnp.concatenate([w_up.astype(F32), zeros], axis=2),
                            jnp.concatenate([zeros, a_up.astype(F32)], axis=2)], axis=1)
    wa_hi, wa_lo = _split_bf16(w_wa)

    def chunk_of(d, i):
        return i + d * (n_chunks - 1 - 2 * i)

    row2 = lambda a: a.astype(F32).reshape(1, -1)
    vec = lambda: pl.BlockSpec((1, width), lambda b, d, i: (0, 0))
    dvec = lambda: pl.BlockSpec((None, 1, width), lambda b, d, i: (d, 0, 0))
    out_spec = lambda: pl.BlockSpec((None, None, c, width), lambda b, d, i: (d, b, chunk_of(d, i), 0))
    y, bonus, g = pl.pallas_call(
        functools.partial(_rwkv_scan_kernel, seq_len=l, width=width),
        grid=(bsz, 2, n_chunks),
        in_specs=[
            pl.BlockSpec((None, c, n_cols), lambda b, d, i: (b, chunk_of(d, i), 0)),
            pl.BlockSpec((None, RWKV_HALO, n_cols),
                         lambda b, d, i: (b, jnp.maximum(chunk_of(d, i) * per - 1, 0), 0)),
            pl.BlockSpec((None, RWKV_HALO, n_cols),
                         lambda b, d, i: (b, jnp.minimum((chunk_of(d, i) + 1) * per, n_halo - 1), 0)),
            pl.BlockSpec((1, n_cols), lambda b, d, i: (0, 0)),
            dvec(), dvec(),
            pl.BlockSpec((None, LANES, 2 * width), lambda b, d, i: (d, 0, 0)),
            pl.BlockSpec((None, LANES, 2 * width), lambda b, d, i: (d, 0, 0)),
            pl.BlockSpec((LANES, width), lambda b, d, i: (0, 0)),
            vec(), vec(), vec(),
        ],
        out_specs=[out_spec(), out_spec(), out_spec()],
        out_shape=[
            jax.ShapeDtypeStruct((2, bsz, l, width), F32),
            jax.ShapeDtypeStruct((2, bsz, l, width), F32),
            jax.ShapeDtypeStruct((2, bsz, l, width), BF16),
        ],
        scratch_shapes=[pltpu.VMEM((width // LANES, LANES, LANES), F32)],
        compiler_params=_cparams("parallel", "parallel", "arbitrary"),
        name="rwkv_scan",
    )(rest, rest, rest, row2(mu), w0.astype(F32).reshape(2, 1, width), a0.astype(F32).reshape(2, 1, width),
      wa_hi, wa_lo, g_up.astype(BF16), row2(k_k), row2(k_a), row2(r_k))

    n = bsz * l
    tm = _row_tile(n, 608)
    both = lambda: pl.BlockSpec((2, tm, width), lambda j: (0, j, 0))
    return pl.pallas_call(
        _rwkv_finish_kernel,
        grid=(n // tm,),
        in_specs=[
            both(), both(),
            pl.BlockSpec((None, tm, width), lambda j: (0, j, 0)),
            pl.BlockSpec((1, width), lambda j: (0, 0)),
            pl.BlockSpec((1, width), lambda j: (0, 0)),
        ],
        out_specs=pl.BlockSpec((tm, width), lambda j: (j, 0)),
        out_shape=jax.ShapeDtypeStruct((n, width), BF16),
        compiler_params=_cparams("parallel"),
        name="rwkv_finish",
    )(y.reshape(2, n, width), bonus.reshape(2, n, width), g.reshape(2, n, width), row2(lnx_g), row2(lnx_b)
      ).reshape(bsz, l, width)


S5_CHUNK = 16


def _cpow(n, lr, li, step):
    mag = jnp.exp(n * (lr * step))
    ang = n * (li * step)
    return mag * jnp.cos(ang), mag * jnp.sin(ang)


def _s5_param_kernel(lamr_ref, lamc_ref, stepr_ref, stepc_ref, bt_ref, ct_ref,
                     kmat_ref, wst_ref, cexp_ref, alpha_ref):
    t_len = S5_CHUNK
    n_i = S5_GROUP_CH
    p2 = 2 * S5_STATE
    ti = t_len * n_i

    lr = lamr_ref[0:1, :]
    li = lamr_ref[1:2, :]
    step = jnp.exp(stepr_ref[...])
    ab_re, ab_im = _cpow(1.0, lr, li, step)
    den = lr * lr + li * li
    z_re = ((ab_re - 1.0) * lr + ab_im * li) / den
    z_im = (ab_im * lr - (ab_re - 1.0) * li) / den
    bt_re = bt_ref[0]
    bt_im = bt_ref[1]
    bb_re = z_re * bt_re - z_im * bt_im
    bb_im = z_re * bt_im + z_im * bt_re
    tau = (lax.broadcasted_iota(I32, (ti, p2), 0) // n_i).astype(F32)
    is_f = lax.broadcasted_iota(I32, (ti, p2), 1) < S5_STATE
    n_w = jnp.where(is_f, (t_len - 1.0) - tau, tau)
    pw_re, pw_im = _cpow(n_w, lr, li, step)
    w_re = pw_re * bb_re - pw_im * bb_im
    w_im = pw_re * bb_im + pw_im * bb_re
    wst_ref[:, 0:p2] = w_re.astype(wst_ref.dtype)
    wst_ref[:, p2:2 * p2] = w_im.astype(wst_ref.dtype)
    al_re, al_im = _cpow(float(t_len), lr, li, step)
    alpha_ref[0:1, :] = al_re
    alpha_ref[1:2, :] = al_im

    lr_c = lamc_ref[:, 0:1]
    li_c = lamc_ref[:, 1:2]
    step_c = jnp.exp(stepc_ref[...])
    t_l = (lax.broadcasted_iota(I32, (p2, ti), 1) // n_i).astype(F32)
    row_f = lax.broadcasted_iota(I32, (p2, ti), 0) < S5_STATE
    ct_re = ct_ref[0]
    ct_im = ct_ref[1]

    def c_times_pow(n):
        q_re, q_im = _cpow(n, lr_c, li_c, step_c)
        return ct_re * q_re - ct_im * q_im, ct_re * q_im + ct_im * q_re

    n_tap = jnp.where(row_f, t_l, jnp.where(t_l == 0.0, 0.0, t_len - t_l))
    ca_re, ca_im = c_times_pow(n_tap)
    lane_p = lax.broadcasted_iota(I32, (n_i, p2), 1)
    bbr = bb_re[0:n_i]
    bbi = bb_im[0:n_i]
    zero = jnp.zeros_like(bbr)
    strips = []
    for sel in (lane_p < S5_STATE, lane_p >= S5_STATE):
        strips.append(_mm3(jnp.where(sel, bbr, zero), ca_re) - _mm3(jnp.where(sel, bbi, zero), ca_im))
    strip_f, strip_b = strips
    t_k = lax.broadcasted_iota(I32, (n_i, ti), 1) // n_i
    for tt in range(t_len):
        sf = strip_f if tt == 0 else pltpu.roll(strip_f, tt * n_i, 1)
        sb = strip_b if tt == 0 else pltpu.roll(strip_b, tt * n_i, 1)
        blk = jnp.where(t_k >= tt, sf, 0.0) + jnp.where(t_k <= tt, sb, 0.0)
        kmat_ref[tt * n_i:(tt + 1) * n_i, :] = blk.astype(kmat_ref.dtype)

    n_out = jnp.where(row_f, t_l + 1.0, t_len - t_l)
    co_re, co_im = c_times_pow(n_out)
    cexp_ref[0:p2, :] = co_re.astype(cexp_ref.dtype)
    cexp_ref[p2:2 * p2, :] = (-co_im).astype(cexp_ref.dtype)


def _s5_main_kernel(u_ref, kmat_ref, wst_ref, cexp_ref, alpha_ref, y_ref, x_ref, sf_ref, sb_ref,
                    *, n_batch, n_chunks):
    p2 = 2 * S5_STATE
    u = u_ref[...]
    x_ref[...] = _dot(u, wst_ref[...])
    a_re = alpha_ref[0:1, :]
    a_im = alpha_ref[1:2, :]
    lane = lax.broadcasted_iota(I32, (1, p2), 1)
    is_f = lane < S5_STATE

    def step(i, carry):
        new = []
        for b in range(n_batch):
            s_re, s_im = carry[b]
            row_f = b * n_chunks + i
            row_b = b * n_chunks + (n_chunks - 1 - i)
            xf = x_ref[pl.ds(row_f, 1), :]
            xb = x_ref[pl.ds(row_b, 1), :]
            s_cat = jnp.concatenate([s_re, s_im], axis=1)
            sf_ref[pl.ds(row_f, 1), :] = s_cat
            sb_ref[pl.ds(row_b, 1), :] = s_cat
            x_re = jnp.where(is_f, xf[:, 0:p2], xb[:, 0:p2])
            x_im = jnp.where(is_f, xf[:, p2:2 * p2], xb[:, p2:2 * p2])
            new.append((a_re * s_re - a_im * s_im + x_re, a_re * s_im + a_im * s_re + x_im))
        return tuple(new)

    zero = jnp.zeros((1, p2), F32)
    lax.fori_loop(0, n_chunks, step, tuple((zero, zero) for _ in range(n_batch)))
    lane2 = lax.broadcasted_iota(I32, sf_ref.shape, 1) % p2
    s_in = jnp.where(lane2 < S5_STATE, sf_ref[...], sb_ref[...])
    s_hi, s_lo = _split_bf16(s_in)
    y_ref[...] = _dot(u, kmat_ref[...]) + _dot(s_hi, cexp_ref[...]) + _dot(s_lo, cexp_ref[...])


def _rms_bf16_kernel(h_ref, g_ref, o_ref):
    o_ref[...] = _rms(h_ref[...], g_ref[...]).astype(o_ref.dtype)


def _gelu_tanh(x):
    return 0.5 * x * (1.0 + jnp.tanh(math.sqrt(2.0 / math.pi) * (x + 0.044715 * (x * x * x))))


def _s5_glu_kernel(h_ref, y_ref, g_ref, d_ref, w_ref, o_ref):
    h = h_ref[...]
    dm = h.shape[1]
    y = y_ref[...] + d_ref[...] * _rms(h, g_ref[...])
    gl = _gelu_tanh(y).astype(BF16)
    a = _dot(gl, w_ref[:, 0:dm])
    b = _dot(gl, w_ref[:, dm:2 * dm])
    o_ref[...] = h + a * jax.nn.sigmoid(b)


def s5_mix(h3, gain, b_re, b_im, lam_re, lam_im, log_step, c_re, c_im, d_skip, w_glu):
    bsz, l, dm = h3.shape
    n_g, n_p, n_i = b_re.shape
    t_len = S5_CHUNK
    assert l % t_len == 0 and n_g * n_i == dm and n_p == S5_STATE and n_i == S5_GROUP_CH
    n_chunks = l // t_len
    m = bsz * n_chunks
    ti = t_len * n_i
    p2 = 2 * n_p
    n = bsz * l
    tm = _row_tile(n, 608)
    h2 = h3.reshape(n, dm)
    gain2 = gain.astype(F32).reshape(1, dm)

    hn = pl.pallas_call(
        _rms_bf16_kernel,
        grid=(n // tm,),
        in_specs=[pl.BlockSpec((tm, dm), lambda i: (i, 0)), pl.BlockSpec((1, dm), lambda i: (0, 0))],
        out_specs=pl.BlockSpec((tm, dm), lambda i: (i, 0)),
        out_shape=jax.ShapeDtypeStruct((n, dm), BF16),
        compiler_params=_cparams("parallel"),
        name="s5_rms",
    )(h2, gain2)
    u = jnp.transpose(hn.reshape(bsz, n_chunks, t_len, n_g, n_i), (3, 0, 1, 2, 4)).reshape(n_g, m, ti)

    f32 = lambda a: a.astype(F32)
    lam_r = jnp.stack([jnp.concatenate([f32(lam_re)[0], f32(lam_re)[1]], axis=-1),
                       jnp.concatenate([f32(lam_im)[0], f32(lam_im)[1]], axis=-1)], axis=1)
    lam_c = jnp.transpose(lam_r, (0, 2, 1))
    step_r = jnp.repeat(jnp.transpose(f32(log_step))[:, None, :], n_p, axis=2)
    step_c = jnp.transpose(step_r, (0, 2, 1))
    bt = jnp.stack([jnp.transpose(f32(b_re), (0, 2, 1)), jnp.transpose(f32(b_im), (0, 2, 1))], axis=1)
    bt = jnp.tile(bt, (1, 1, t_len, 2))
    ct = jnp.stack([f32(c_re), f32(c_im)], axis=0)
    ct = jnp.transpose(ct, (2, 0, 1, 4, 3)).reshape(n_g, 2, p2, n_i)
    ct = jnp.tile(ct, (1, 1, 1, t_len))

    gspec = lambda *shape: pl.BlockSpec((None,) + shape, lambda g: (g,) + (0,) * len(shape))
    kmat, wst, cexp, alpha = pl.pallas_call(
        _s5_param_kernel,
        grid=(n_g,),
        in_specs=[gspec(2, p2), gspec(p2, 2), gspec(1, p2), gspec(p2, 1), gspec(2, ti, p2), gspec(2, p2, ti)],
        out_specs=[gspec(ti, ti), gspec(ti, 2 * p2), gspec(2 * p2, ti), gspec(2, p2)],
        out_shape=[
            jax.ShapeDtypeStruct((n_g, ti, ti), BF16),
            jax.ShapeDtypeStruct((n_g, ti, 2 * p2), BF16),
            jax.ShapeDtypeStruct((n_g, 2 * p2, ti), BF16),
            jax.ShapeDtypeStruct((n_g, 2, p2), F32),
        ],
        compiler_params=_cparams("parallel"),
        name="s5_params",
    )(lam_r, lam_c, step_r, step_c, bt, ct)

    y = pl.pallas_call(
        functools.partial(_s5_main_kernel, n_batch=bsz, n_chunks=n_chunks),
        grid=(n_g,),
        in_specs=[gspec(m, ti), gspec(ti, ti), gspec(ti, 2 * p2), gspec(2 * p2, ti), gspec(2, p2)],
        out_specs=gspec(m, ti),
        out_shape=jax.ShapeDtypeStruct((n_g, m, ti), F32),
        scratch_shapes=[pltpu.VMEM((m, 2 * p2), F32), pltpu.VMEM((m, 2 * p2), F32), pltpu.VMEM((m, 2 * p2), F32)],
        compiler_params=_cparams("parallel"),
        name="s5_main",
    )(u, kmat, wst, cexp, alpha)
    y2 = jnp.transpose(y.reshape(n_g, bsz, n_chunks, t_len, n_i), (1, 2, 3, 0, 4)).reshape(n, dm)

    out = pl.pallas_call(
        _s5_glu_kernel,
        grid=(n // tm,),
        in_specs=[
            pl.BlockSpec((tm, dm), lambda i: (i, 0)),
            pl.BlockSpec((tm, dm), lambda i: (i, 0)),
            pl.BlockSpec((1, dm), lambda i: (0, 0)),
            pl.BlockSpec((1, dm), lambda i: (0, 0)),
            pl.BlockSpec((dm, 2 * dm), lambda i: (0, 0)),
        ],
        out_specs=pl.BlockSpec((tm, dm), lambda i: (i, 0)),
        out_shape=jax.ShapeDtypeStruct((n, dm), F32),
        compiler_params=_cparams("parallel"),
        name="s5_glu",
    )(h2, y2, gain2, f32(d_skip).reshape(1, dm), w_glu.astype(BF16))
    return out.reshape(bsz, l, dm)


def na_rwkv_mix(h3, gain, w_in, w_out, rpb, mu, w0, w_up, a0, a_up, g_up, k_k, k_a, r_k, lnx_g, lnx_b):
    bsz, l, dm = h3.shape
    n = bsz * l
    h2 = h3.reshape(n, dm)
    n_qkv = 3 * (w_out.shape[0] // 2)
    qkv, rest = norm_inproj(h2, gain.astype(F32), w_in.astype(BF16), n_qkv)
    na = na_attention(qkv.reshape(bsz, l, n_qkv), rpb)
    rw = rwkv_mix(rest.reshape(bsz, l, -1), mu, w0, w_up, a0, a_up, g_up, k_k, k_a, r_k, lnx_g, lnx_b)
    out = outproj_residual(h2, na.reshape(n, -1), rw.reshape(n, -1), w_out.astype(BF16))
    return out.reshape(bsz, l, dm)


def kernel(x, meta_tokens, norm_mix, norm_ffn, norm_final, mix_w_in, mix_w_out, na_rpb, rwkv_mu,
           rwkv_w0, rwkv_w_up, rwkv_a0, rwkv_a_up, rwkv_g_up, rwkv_k_k, rwkv_k_a, rwkv_r_k,
           rwkv_lnx_g, rwkv_lnx_b, s5_b_re, s5_b_im, s5_lambda_re, s5_lambda_im, s5_log_step,
           s5_c_re, s5_c_im, s5_d, s5_w_glu, moe_w_group, moe_b_group, moe_w_expert, moe_b_expert,
           moe_w1, moe_w3, moe_w2):
    bsz, _, dm = x.shape
    depth = norm_mix.shape[0]
    meta = jnp.broadcast_to(meta_tokens.astype(x.dtype)[None], (bsz,) + meta_tokens.shape)
    h = jnp.concatenate([meta, x], axis=1)
    l = h.shape[1]
    for layer in range(depth):
        i = layer // 2
        if layer % 2 == 0:
            h = na_rwkv_mix(h, norm_mix[layer], mix_w_in[i], mix_w_out[i], na_rpb[i], rwkv_mu[i], rwkv_w0[i],
                            rwkv_w_up[i], rwkv_a0[i], rwkv_a_up[i], rwkv_g_up[i], rwkv_k_k[i], rwkv_k_a[i],
                            rwkv_r_k[i], rwkv_lnx_g[i], rwkv_lnx_b[i])
        else:
            h = s5_mix(h, norm_mix[layer], s5_b_re[i], s5_b_im[i], s5_lambda_re[i], s5_lambda_im[i],
                       s5_log_step[i], s5_c_re[i], s5_c_im[i], s5_d[i], s5_w_glu[i])
        h = hierarchical_moe_residual(h.reshape(bsz * l, dm), norm_ffn[layer].astype(F32), moe_w_group[layer],
                                      moe_b_group[layer], moe_w_expert[layer], moe_b_expert[layer],
                                      moe_w1[layer], moe_w3[layer], moe_w2[layer]).reshape(bsz, l, dm)
    return final_norm(h, norm_final.astype(F32))
```

```python
import functools
import math

import jax
import jax.numpy as jnp
from jax import lax
from jax.experimental import pallas as pl
from jax.experimental.pallas import tpu as pltpu

F32 = jnp.float32
BF16 = jnp.bfloat16
I32 = jnp.int32

N_META = 16
GRID_W = 64
HEAD_DIM = 64
NA_WIN_ROWS = 8
NA_WIN_COLS = 16
S5_GROUP_CH = 16
S5_STATE = 64
MOE_GROUPS = 4
MOE_PER_GROUP = 8
MOE_EXPERTS = MOE_GROUPS * MOE_PER_GROUP
NORM_EPS = 1e-6
RWKV_GN_EPS = 64e-5
NEG_INF = -1e30

LANES = 128
SUBLANES_BF16 = 16
VMEM_LIMIT_BYTES = 56 * 1024 * 1024

MOE_TILE = 256
ROUTER_LANES = 128


def _cparams(*sem):
    return pltpu.CompilerParams(dimension_semantics=sem, vmem_limit_bytes=VMEM_LIMIT_BYTES)


def _row_tile(n, target):
    best = None
    for t in range(SUBLANES_BF16, min(n, target) + 1, SUBLANES_BF16):
        if n % t == 0:
            best = t
    assert best is not None, (n, target)
    return best


def _rms(x, gain):
    ms = jnp.mean(x * x, axis=-1, keepdims=True)
    return (x * lax.rsqrt(ms + NORM_EPS)) * gain


def _split_bf16(x):
    hi = x.astype(BF16)
    lo = (x - hi.astype(F32)).astype(BF16)
    return hi, lo


def _dot(a, b):
    return jnp.dot(a, b, preferred_element_type=F32)


def _dot_nt(a, b):
    return lax.dot_general(a, b, (((1,), (1,)), ((), ())), preferred_element_type=F32)


def _norm_inproj_kernel(h_ref, g_ref, w_ref, qkv_ref, rest_ref, xn_ref, *, n_qkv, chunk):
    xn_ref[...] = _rms(h_ref[...], g_ref[...]).astype(BF16)
    n_all = w_ref.shape[1]
    for c in range(0, n_all, chunk):
        y = _dot(xn_ref[...], w_ref[:, c:c + chunk])
        if c < n_qkv:
            qkv_ref[:, c:c + chunk] = y.astype(BF16)
        else:
            rest_ref[:, c - n_qkv:c - n_qkv + chunk] = y


def norm_inproj(h2, gain, w_bf16, n_qkv):
    n, d = h2.shape
    n_all = w_bf16.shape[1]
    tm = _row_tile(n, 608)
    chunk = 256
    assert n_qkv % chunk == 0 and n_all % chunk == 0
    return pl.pallas_call(
        functools.partial(_norm_inproj_kernel, n_qkv=n_qkv, chunk=chunk),
        grid=(n // tm,),
        in_specs=[
            pl.BlockSpec((tm, d), lambda i: (i, 0)),
            pl.BlockSpec((1, d), lambda i: (0, 0)),
            pl.BlockSpec((d, n_all), lambda i: (0, 0)),
        ],
        out_specs=[
            pl.BlockSpec((tm, n_qkv), lambda i: (i, 0)),
            pl.BlockSpec((tm, n_all - n_qkv), lambda i: (i, 0)),
        ],
        out_shape=[
            jax.ShapeDtypeStruct((n, n_qkv), BF16),
            jax.ShapeDtypeStruct((n, n_all - n_qkv), F32),
        ],
        scratch_shapes=[pltpu.VMEM((tm, d), BF16)],
        compiler_params=_cparams("parallel"),
        name="norm_inproj",
    )(h2, gain.reshape(1, d), w_bf16)


def _outproj_kernel(h_ref, na_ref, rw_ref, wa_ref, wb_ref, o_ref):
    acc = _dot(na_ref[...], wa_ref[...])
    acc = acc + _dot(rw_ref[...], wb_ref[...])
    o_ref[...] = h_ref[...] + acc


def outproj_residual(h2, na, rw, w_out_bf16):
    n, d = h2.shape
    ka, kb = na.shape[1], rw.shape[1]
    tm = _row_tile(n, 608)
    return pl.pallas_call(
        _outproj_kernel,
        grid=(n // tm,),
        in_specs=[
            pl.BlockSpec((tm, d), lambda i: (i, 0)),
            pl.BlockSpec((tm, ka), lambda i: (i, 0)),
            pl.BlockSpec((tm, kb), lambda i: (i, 0)),
            pl.BlockSpec((ka, d), lambda i: (0, 0)),
            pl.BlockSpec((kb, d), lambda i: (0, 0)),
        ],
        out_specs=pl.BlockSpec((tm, d), lambda i: (i, 0)),
        out_shape=jax.ShapeDtypeStruct((n, d), F32),
        compiler_params=_cparams("parallel"),
        name="outproj_residual",
    )(h2, na, rw, w_out_bf16[:ka], w_out_bf16[ka:])


def _router_kernel(h_ref, g_ref, whi_ref, wlo_ref, b_ref, xn_ref, route_ref):
    xn = _rms(h_ref[...], g_ref[...])
    x_hi, x_lo = _split_bf16(xn)
    xn_ref[...] = x_hi
    logits = (_dot(x_hi, whi_ref[...]) + _dot(x_hi, wlo_ref[...]) + _dot(x_lo, whi_ref[...])
              + b_ref[...])
    tm = logits.shape[0]
    lane = lax.broadcasted_iota(I32, (tm, ROUTER_LANES), 1)
    big = jnp.int32(ROUTER_LANES)

    is_g = lane < MOE_GROUPS
    lg = jnp.where(is_g, logits, -jnp.inf)
    eg = jnp.where(is_g, jnp.exp(lg - jnp.max(lg, axis=-1, keepdims=True)), 0.0)
    pg = eg / jnp.sum(eg, axis=-1, keepdims=True)
    p_grp = jnp.max(pg, axis=-1, keepdims=True)
    grp = jnp.min(jnp.where(is_g & (pg == p_grp), lane, big), axis=-1, keepdims=True)

    lo_lane = MOE_GROUPS + MOE_PER_GROUP * grp
    is_e = (lane >= lo_lane) & (lane < lo_lane + MOE_PER_GROUP)
    le = jnp.where(is_e, logits, -jnp.inf)
    ee = jnp.where(is_e, jnp.exp(le - jnp.max(le, axis=-1, keepdims=True)), 0.0)
    pe = jnp.where(is_e, ee / jnp.sum(ee, axis=-1, keepdims=True), -1.0)
    p1 = jnp.max(pe, axis=-1, keepdims=True)
    i1 = jnp.min(jnp.where(pe == p1, lane, big), axis=-1, keepdims=True)
    pe2 = jnp.where(lane == i1, -1.0, pe)
    p2 = jnp.max(pe2, axis=-1, keepdims=True)
    i2 = jnp.min(jnp.where(pe2 == p2, lane, big), axis=-1, keepdims=True)
    denom = p1 + p2
    g1 = p_grp * p1 / denom
    g2 = p_grp * p2 / denom
    e1 = (i1 - MOE_GROUPS).astype(F32)
    e2 = (i2 - MOE_GROUPS).astype(F32)
    route_ref[...] = jnp.where(lane == 0, e1, jnp.where(lane == 1, e2, jnp.where(lane == 2, g1, g2)))


def moe_router(h2, gain, w_group, b_group, w_expert, b_expert):
    n, d = h2.shape
    n_r = MOE_GROUPS + MOE_EXPERTS
    w_r = jnp.concatenate([w_group, jnp.transpose(w_expert, (1, 0, 2)).reshape(d, MOE_EXPERTS)], axis=1)
    w_r = jnp.pad(w_r.astype(F32), ((0, 0), (0, ROUTER_LANES - n_r)))
    w_hi, w_lo = _split_bf16(w_r)
    b_r = jnp.pad(jnp.concatenate([b_group, b_expert.reshape(-1)]).astype(F32), (0, ROUTER_LANES - n_r))
    tm = _row_tile(n, 608)
    return pl.pallas_call(
        _router_kernel,
        grid=(n // tm,),
        in_specs=[
            pl.BlockSpec((tm, d), lambda i: (i, 0)),
            pl.BlockSpec((1, d), lambda i: (0, 0)),
            pl.BlockSpec((d, ROUTER_LANES), lambda i: (0, 0)),
            pl.BlockSpec((d, ROUTER_LANES), lambda i: (0, 0)),
            pl.BlockSpec((1, ROUTER_LANES), lambda i: (0, 0)),
        ],
        out_specs=[
            pl.BlockSpec((tm, d), lambda i: (i, 0)),
            pl.BlockSpec((tm, ROUTER_LANES), lambda i: (i, 0)),
        ],
        out_shape=[
            jax.ShapeDtypeStruct((n, d), BF16),
            jax.ShapeDtypeStruct((n, ROUTER_LANES), F32),
        ],
        compiler_params=_cparams("parallel"),
        name="moe_router",
    )(h2, gain.reshape(1, d), w_hi, w_lo, b_r.reshape(1, ROUTER_LANES))


def _expert_kernel(blk_e_ref, n_used_ref, x_ref, sg_ref, w1_ref, w3_ref, w2_ref, y_ref,
                   w1b_ref, w3b_ref, w2b_ref):
    i = pl.program_id(0)
    used = i < n_used_ref[0]
    prev_e = blk_e_ref[jnp.maximum(i - 1, 0)]
    fresh = (i == 0) | (blk_e_ref[i] != prev_e)

    @pl.when(used & fresh)
    def _():
        w1b_ref[...] = w1_ref[...].astype(BF16)
        w3b_ref[...] = w3_ref[...].astype(BF16)
        w2b_ref[...] = w2_ref[...].astype(BF16)

    @pl.when(used)
    def _():
        x = x_ref[...]
        a = _dot(x, w1b_ref[...])
        b = _dot(x, w3b_ref[...])
        hmid = (a * jax.nn.sigmoid(a) * b).astype(BF16)
        y_ref[...] = _dot(hmid, w2b_ref[...]) * sg_ref[...]

    @pl.when(jnp.logical_not(used))
    def _():
        y_ref[...] = jnp.zeros_like(y_ref)


def moe_experts(xbuf, sg_rows, blk_e, n_used, w1, w3, w2):
    n_rows, d = xbuf.shape
    f = w1.shape[2]
    n_blocks = n_rows // MOE_TILE
    last = n_blocks - 1

    def row_map(i, blk_e_ref, n_used_ref):
        return (jnp.where(i < n_used_ref[0], i, last), 0)

    def w_map(i, blk_e_ref, n_used_ref):
        return (blk_e_ref[i], 0, 0)

    grid_spec = pltpu.PrefetchScalarGridSpec(
        num_scalar_prefetch=2,
        grid=(n_blocks,),
        in_specs=[
            pl.BlockSpec((MOE_TILE, d), row_map),
            pl.BlockSpec((MOE_TILE, 1), row_map),
            pl.BlockSpec((None, d, f), w_map),
            pl.BlockSpec((None, d, f), w_map),
            pl.BlockSpec((None, f, d), w_map),
        ],
        out_specs=pl.BlockSpec((MOE_TILE, d), row_map),
        scratch_shapes=[
            pltpu.VMEM((d, f), BF16),
            pltpu.VMEM((d, f), BF16),
            pltpu.VMEM((f, d), BF16),
        ],
    )
    return pl.pallas_call(
        _expert_kernel,
        grid_spec=grid_spec,
        out_shape=jax.ShapeDtypeStruct((n_rows, d), F32),
        compiler_params=_cparams("arbitrary"),
        name="moe_experts",
    )(blk_e, n_used, xbuf, sg_rows, w1, w3, w2)


def hierarchical_moe_residual(h2, gain, w_group, b_group, w_expert, b_expert, w1, w3, w2):
    n, d = h2.shape
    xn, route = moe_router(h2, gain, w_group, b_group, w_expert, b_expert)
    flat_e = route[:, 0:2].astype(I32).reshape(-1)
    flat_g = route[:, 2:4].reshape(-1)
    n_assign = 2 * n
    onehot = (flat_e[:, None] == jnp.arange(MOE_EXPERTS, dtype=I32)[None, :]).astype(I32)
    csum = jnp.cumsum(onehot, axis=0)
    pos = jnp.sum((csum - onehot) * onehot, axis=1)
    counts = csum[-1]
    padded = (counts + MOE_TILE - 1) // MOE_TILE * MOE_TILE
    pad_end = jnp.cumsum(padded)
    pad_start = pad_end - padded
    dest = pad_start[flat_e] + pos
    n_blocks = -(-n_assign // MOE_TILE) + MOE_EXPERTS
    n_rows = n_blocks * MOE_TILE
    blk_start = jnp.arange(n_blocks, dtype=I32) * MOE_TILE
    blk_e = jnp.minimum(jnp.sum((pad_end[None, :] <= blk_start[:, None]).astype(I32), axis=1),
                        MOE_EXPERTS - 1).astype(I32)
    n_used = (pad_end[-1] // MOE_TILE).astype(I32).reshape(1)
    flat_tok = jnp.repeat(jnp.arange(n, dtype=I32), 2)
    tok_of_row = jnp.zeros((n_rows,), I32).at[dest].set(flat_tok)
    sg_rows = jnp.zeros((n_rows,), F32).at[dest].set(flat_g)
    xbuf = jnp.take(xn, tok_of_row, axis=0)
    ybuf = moe_experts(xbuf, sg_rows.reshape(n_rows, 1), blk_e, n_used, w1, w3, w2)
    y2 = jnp.take(ybuf, dest, axis=0).reshape(n, 2, d)
    return h2 + y2[:, 0] + y2[:, 1]


def _final_norm_kernel(h_ref, g_ref, o_ref):
    o_ref[...] = _rms(h_ref[...], g_ref[...])


def final_norm(h3, gain):
    b, l, d = h3.shape
    t = l - N_META
    tm = _row_tile(t, 512)
    return pl.pallas_call(
        _final_norm_kernel,
        grid=(b, t // tm),
        in_specs=[
            pl.BlockSpec((None, pl.Element(tm), pl.Element(d)),
                         lambda bi, i: (bi, pl.multiple_of(N_META + i * tm, SUBLANES_BF16), 0)),
            pl.BlockSpec((1, d), lambda bi, i: (0, 0)),
        ],
        out_specs=pl.BlockSpec((None, tm, d), lambda bi, i: (bi, i, 0)),
        out_shape=jax.ShapeDtypeStruct((b, t, d), F32),
        compiler_params=_cparams("parallel", "parallel"),
        name="final_norm",
    )(h3, gain.reshape(1, d))


NA_QROWS = 8
NA_KROWS = 3 * NA_QROWS


def _na_kernel(q_ref, kw_ref, vw_ref, qm_ref, km_ref, vm_ref, bias_ref, o_ref, om_ref, *, rows, scale):
    blk = pl.program_id(1)
    tq = GRID_W
    n_pairs = q_ref.shape[1] // LANES
    base = jnp.clip(NA_QROWS * blk - NA_QROWS, 0, rows - NA_KROWS)
    lane = lax.broadcasted_iota(I32, (tq, LANES), 1)
    halves = [lane < HEAD_DIM, lane >= HEAD_DIM]

    def attend(qp, kp, vp, kmp, vmp, bias):
        s_g = _dot_nt(qp, kp) * scale
        if bias is not None:
            s_g = s_g + bias
        s_m = _dot_nt(qp, kmp) * scale
        m = jnp.maximum(jnp.max(s_g, axis=-1, keepdims=True), jnp.max(s_m, axis=-1, keepdims=True))
        p_g = jnp.exp(s_g - m)
        p_m = jnp.exp(s_m - m)
        den = jnp.sum(p_g, axis=-1, keepdims=True) + jnp.sum(p_m, axis=-1, keepdims=True)
        o = _dot(p_g.astype(BF16), vp) + _dot(p_m.astype(BF16), vmp)
        return o / den

    def row_body(j, carry):
        r = NA_QROWS * blk + j
        start = jnp.clip(r - NA_WIN_ROWS // 2, 0, rows - NA_WIN_ROWS)
        s_idx = start - r + (NA_WIN_ROWS - 1)
        koff = pl.multiple_of((start - base) * GRID_W, GRID_W)
        qoff = pl.multiple_of(j * tq, tq)
        for p in range(n_pairs):
            cols = slice(p * LANES, (p + 1) * LANES)
            q_pair = q_ref[pl.ds(qoff, tq), cols]
            kp = kw_ref[pl.ds(koff, NA_WIN_ROWS * GRID_W), cols]
            vp = vw_ref[pl.ds(koff, NA_WIN_ROWS * GRID_W), cols]
            kmp = km_ref[:, cols]
            vmp = vm_ref[:, cols]
            outs = []
            for hh in range(2):
                qp = jnp.where(halves[hh], q_pair, jnp.zeros_like(q_pair))
                outs.append(attend(qp, kp, vp, kmp, vmp, bias_ref[2 * p + hh, s_idx]))
            o_ref[pl.ds(qoff, tq), cols] = jnp.where(halves[0], outs[0], outs[1]).astype(o_ref.dtype)
        return carry

    lax.fori_loop(0, NA_QROWS, row_body, 0)

    @pl.when(blk == 0)
    def _():
        lane_m = lax.broadcasted_iota(I32, (N_META, LANES), 1)
        for p in range(n_pairs):
            cols = slice(p * LANES, (p + 1) * LANES)
            q_pair = qm_ref[:, cols]
            kmp = km_ref[:, cols]
            vmp = vm_ref[:, cols]
            outs = []
            for hh in range(2):
                sel = (lane_m < HEAD_DIM) if hh == 0 else (lane_m >= HEAD_DIM)
                qp = jnp.where(sel, q_pair, jnp.zeros_like(q_pair))
                s_m = _dot_nt(qp, kmp) * scale
                p_m = jnp.exp(s_m - jnp.max(s_m, axis=-1, keepdims=True))
                den = jnp.sum(p_m, axis=-1, keepdims=True)
                outs.append(_dot(p_m.astype(BF16), vmp) / den)
            om_ref[:, cols] = jnp.where(lane_m < HEAD_DIM, outs[0], outs[1]).astype(om_ref.dtype)


def _na_bias_table(rpb):
    h = rpb.shape[0]
    c_ids = jnp.arange(GRID_W)
    c_start = jnp.clip(c_ids - NA_WIN_COLS // 2, 0, GRID_W - NA_WIN_COLS)
    in_band = (c_ids[None, :] >= c_start[:, None]) & (c_ids[None, :] < c_start[:, None] + NA_WIN_COLS)
    dc = jnp.clip(c_ids[None, :] - c_ids[:, None] + NA_WIN_COLS - 1, 0, 2 * NA_WIN_COLS - 2)
    tab = jnp.where(in_band[None, None], rpb.astype(F32)[:, :, dc], NEG_INF)
    win = jnp.stack([tab[:, s:s + NA_WIN_ROWS] for s in range(NA_WIN_ROWS)], axis=1)
    return jnp.transpose(win, (0, 1, 3, 2, 4)).reshape(h, NA_WIN_ROWS, GRID_W, NA_WIN_ROWS * GRID_W)


def na_attention(qkv, rpb):
    b, l, w3 = qkv.shape
    w = w3 // 3
    t = l - N_META
    rows = t // GRID_W
    assert rows * GRID_W == t and rows % NA_QROWS == 0 and rows >= NA_KROWS
    tq = NA_QROWS * GRID_W
    tk = NA_KROWS * GRID_W
    bias = _na_bias_table(rpb)
    al = SUBLANES_BF16

    def q_map(bi, i):
        return (bi, pl.multiple_of(N_META + i * tq, al), 0)

    def kv_map(col):
        def f(bi, i):
            base = jnp.clip(NA_QROWS * i - NA_QROWS, 0, rows - NA_KROWS)
            return (bi, pl.multiple_of(N_META + base * GRID_W, al), col)
        return f

    def meta_map(col):
        return lambda bi, i: (bi, 0, col)

    el = pl.Element
    grid_out, meta_out = pl.pallas_call(
        functools.partial(_na_kernel, rows=rows, scale=HEAD_DIM ** -0.5),
        grid=(b, rows // NA_QROWS),
        in_specs=[
            pl.BlockSpec((None, el(tq), el(w)), q_map),
            pl.BlockSpec((None, el(tk), el(w)), kv_map(w)),
            pl.BlockSpec((None, el(tk), el(w)), kv_map(2 * w)),
            pl.BlockSpec((None, el(N_META), el(w)), meta_map(0)),
            pl.BlockSpec((None, el(N_META), el(w)), meta_map(w)),
            pl.BlockSpec((None, el(N_META), el(w)), meta_map(2 * w)),
            pl.BlockSpec(bias.shape, lambda bi, i: (0, 0, 0, 0)),
        ],
        out_specs=[
            pl.BlockSpec((None, tq, w), lambda bi, i: (bi, i, 0)),
            pl.BlockSpec((None, N_META, w), lambda bi, i: (bi, 0, 0)),
        ],
        out_shape=[
            jax.ShapeDtypeStruct((b, t, w), BF16),
            jax.ShapeDtypeStruct((b, N_META, w), BF16),
        ],
        compiler_params=_cparams("parallel", "arbitrary"),
        name="na_attention",
    )(qkv, qkv, qkv, qkv, qkv, qkv, bias)
    return jnp.concatenate([meta_out, grid_out], axis=1)


RWKV_CHUNK = 64
RWKV_HALO = 8


def _split3_bf16(x):
    p1 = x.astype(BF16)
    r1 = x - p1.astype(F32)
    p2 = r1.astype(BF16)
    p3 = (r1 - p2.astype(F32)).astype(BF16)
    return p1, p2, p3


def _mm1(a, b):
    return _dot(a.astype(BF16), b.astype(BF16))


def _mm3(a, b):
    ah, al = _split_bf16(a)
    bh, bl = _split_bf16(b)
    return _dot(ah, bh) + _dot(ah, bl) + _dot(al, bh)


def _mm1_nt(a, b):
    return _dot_nt(a.astype(BF16), b.astype(BF16))


def _mm3_nt(a, b):
    ah, al = _split_bf16(a)
    bh, bl = _split_bf16(b)
    return _dot_nt(ah, bh) + _dot_nt(ah, bl) + _dot_nt(al, bh)


def _exact_left(mat_bf16, x):
    p1, p2, p3 = _split3_bf16(x)
    return _dot(mat_bf16, p1) + _dot(mat_bf16, p2) + _dot(mat_bf16, p3)


def _exact_right(x, mat_bf16):
    p1, p2, p3 = _split3_bf16(x)
    return _dot(p1, mat_bf16) + _dot(p2, mat_bf16) + _dot(p3, mat_bf16)


def _head_block_ones(width):
    ri = lax.broadcasted_iota(I32, (width, width), 0) // HEAD_DIM
    ci = lax.broadcasted_iota(I32, (width, width), 1) // HEAD_DIM
    return (ri == ci).astype(BF16)


def _stack_heads(x, m0):
    z = jnp.zeros_like(x)
    return jnp.concatenate([jnp.where(m0, x, z), jnp.where(m0, z, x)], axis=0)


_MM_L4 = _mm1_nt
_MM_KT = _mm3_nt
_MM_SQ = _mm1
_MM_AP = _mm1
_MM_V = _mm1
_MM_Y = _mm1
_MM_UPD = _mm3


def _rwkv_chunk_streams(streams, c):
    c2 = 2 * c
    lane = lax.broadcasted_iota(I32, (c, LANES), 1)
    m0 = lane < HEAD_DIM
    t_i = lax.broadcasted_iota(I32, (c2, c2), 0) % c
    s_i = lax.broadcasted_iota(I32, (c2, c2), 1) % c
    masks = {sg: ((t_i - s_i) * sg > 0, (t_i - s_i) * sg >= 0) for sg in {s["sign"] for s in streams}}

    lhs = [jnp.concatenate([_stack_heads(s["kkp"], m0), _stack_heads(s["rp"], m0)], axis=0) for s in streams]
    rhs = [jnp.concatenate([_stack_heads(s["ki"], m0), _stack_heads(s["bi"], m0)], axis=0) for s in streams]
    vs = [_stack_heads(s["v"], m0) for s in streams]
    l4 = [_MM_L4(a, b) for a, b in zip(lhs, rhs)]
    m_kk, n1, m_rk, m_rb = [], [], [], []
    for s, m in zip(streams, l4):
        strict, incl = masks[s["sign"]]
        m_kk.append(jnp.where(strict, m[0:c2, 0:c2], 0.0))
        n1.append(jnp.where(strict, m[0:c2, c2:2 * c2], 0.0))
        m_rk.append(jnp.where(incl, m[c2:2 * c2, 0:c2], 0.0))
        m_rb.append(jnp.where(incl, m[c2:2 * c2, c2:2 * c2], 0.0))
    powers = [n1]
    span = 1
    while span * 2 < c:
        powers.append([_MM_SQ(q, q) for q in powers[-1]])
        span *= 2
    kt = [_MM_KT(a, s["st"]) for a, s in zip(lhs, streams)]
    x = [k[0:c2] + _MM_V(m, w) for k, m, w in zip(kt, m_kk, vs)]
    for level in reversed(powers[1:]):
        x = [xi + _MM_AP(q, xi) for q, xi in zip(level, x)]
    u = [xi - _MM_AP(q, xi) for q, xi in zip(n1, x)]
    ys = [k[c2:2 * c2] + _MM_Y(a, w) - _MM_Y(b, ui) for k, a, w, b, ui in zip(kt, m_rk, vs, m_rb, u)]
    out = []
    for s, w, ui, yi in zip(streams, vs, u, ys):
        upd_l = jnp.transpose(jnp.concatenate([w, -ui], axis=0))
        upd_r = jnp.concatenate([_stack_heads(s["kipc"], m0), _stack_heads(s["bipc"], m0)], axis=0)
        out.append((yi[0:c] + yi[c:c2], s["st"] * s["pc"] + _MM_UPD(upd_l, upd_r)))
    return out


def _softplus(z):
    return jnp.maximum(z, 0.0) + jnp.log(1.0 + jnp.exp(-jnp.abs(z)))


def _rwkv_tile_prep(x_ref, xp_ref, xn_ref, chunk, n_chunks, seq_len, width, sign, mu, w0, a0, wa_hi, wa_lo,
                    k_k, k_a, r_k, ones_h):
    c = RWKV_CHUNK
    valid = jnp.minimum(c, seq_len - chunk * c)
    n_cols = x_ref.shape[1]
    row = lax.broadcasted_iota(I32, (c, n_cols), 0)
    x = jnp.where(row < valid, x_ref[...], 0.0)
    prev_row = jnp.where(chunk > 0, xp_ref[RWKV_HALO - 1:RWKV_HALO, :], 0.0)
    next_row = jnp.where(chunk < n_chunks - 1, xn_ref[0:1, :], 0.0)
    x_prev = jnp.where(row == 0, prev_row, pltpu.roll(x, 1, 0))
    x_next = jnp.where(row == c - 1, next_row, pltpu.roll(x, c - 1, 0))
    xs = x + mu * (0.5 * (x_prev + x_next) - x)
    xs = jnp.where(row < valid, xs, 0.0)

    r = xs[:, 0:width]
    k = xs[:, width:2 * width]
    v = xs[:, 2 * width:3 * width]
    wa = xs[:, 3 * width:3 * width + LANES]
    g_lo = xs[:, 3 * width + LANES:3 * width + 2 * LANES]

    lane_wa = lax.broadcasted_iota(I32, (c, LANES), 1)
    xwa = jnp.where(lane_wa < LANES // 2, jnp.tanh(wa), wa)
    xh, xl = _split_bf16(xwa)
    la = _dot(xh, wa_hi) + _dot(xh, wa_lo) + _dot(xl, wa_hi)
    w_log = -_softplus(-(w0 + la[:, 0:width])) - 0.5
    rowv = lax.broadcasted_iota(I32, (c, width), 0) < valid
    logw = jnp.where(rowv, -jnp.exp(w_log), 0.0)
    a = jax.nn.sigmoid(a0 + la[:, width:2 * width])

    kk0 = k * k_k
    ss = _exact_right(kk0 * kk0, ones_h)
    kk = kk0 / jnp.maximum(jnp.sqrt(ss), 1e-12)
    kdir = k * (1.0 + (a - 1.0) * k_a)
    b = kk * a

    t_i = lax.broadcasted_iota(I32, (c, c), 0)
    s_i = lax.broadcasted_iota(I32, (c, c), 1)
    tri = ((t_i - s_i) * sign >= 0).astype(BF16)
    cl = _exact_left(tri, logw)
    last = cl[c - 1:c, :] if sign > 0 else cl[0:1, :]
    e_n = jnp.exp(-cl)
    pcr = jnp.exp(last - cl)
    return dict(kkp=kk * jnp.exp(cl - logw), rp=r * jnp.exp(cl), ki=kdir * e_n, bi=b * e_n, kipc=kdir * pcr,
                bipc=b * pcr, v=v, pc=jnp.exp(last), g_lo=g_lo,
                bonus=_exact_right(r * kdir * r_k, ones_h) * v)


def _rwkv_scan_kernel(xf_ref, xfp_ref, xfn_ref, xb_ref, xbp_ref, xbn_ref, mu_ref, w0_ref, a0_ref, wah_ref, wal_ref,
                      gup_ref, kk_ref, ka_ref, rk_ref, yf_ref, yb_ref, bonf_ref, bonb_ref, g_ref, st_ref,
                      *, seq_len, width):
    i = pl.program_id(1)
    n_chunks = pl.num_programs(1)
    n_pairs = width // LANES

    @pl.when(i == 0)
    def _():
        st_ref[...] = jnp.zeros_like(st_ref)

    ones_h = _head_block_ones(width)
    common = (mu_ref[...],)
    tail = (kk_ref[...], ka_ref[...], rk_ref[...], ones_h)
    fwd = _rwkv_tile_prep(xf_ref, xfp_ref, xfn_ref, i, n_chunks, seq_len, width, 1, *common,
                          w0_ref[0], a0_ref[0], wah_ref[0], wal_ref[0], *tail)
    bwd = _rwkv_tile_prep(xb_ref, xbp_ref, xbn_ref, n_chunks - 1 - i, n_chunks, seq_len, width, -1, *common,
                          w0_ref[1], a0_ref[1], wah_ref[1], wal_ref[1], *tail)
    bonf_ref[...] = fwd["bonus"]
    bonb_ref[...] = bwd["bonus"]
    g_ref[...] = _mm1(jax.nn.sigmoid(fwd["g_lo"]), gup_ref[...]).astype(g_ref.dtype)

    names = ("kkp", "rp", "ki", "bi", "kipc", "bipc", "v", "pc")
    streams = []
    for di, (tile, sign) in enumerate(((fwd, 1), (bwd, -1))):
        for p in range(n_pairs):
            cols = slice(p * LANES, (p + 1) * LANES)
            s = {nm: tile[nm][:, cols] for nm in names}
            s["st"] = st_ref[di * n_pairs + p]
            s["sign"] = sign
            streams.append(s)
    res = _rwkv_chunk_streams(streams, RWKV_CHUNK)
    for j, (y, st_new) in enumerate(res):
        di, p = divmod(j, n_pairs)
        cols = slice(p * LANES, (p + 1) * LANES)
        (yf_ref if di == 0 else yb_ref)[:, cols] = y
        st_ref[j] = st_new


def _rwkv_finish_kernel(yf_ref, yb_ref, bonf_ref, bonb_ref, g_ref, lg_ref, lb_ref, o_ref):
    width = o_ref.shape[1]
    ones_h = _head_block_ones(width)
    y = yf_ref[...] + yb_ref[...]
    mean = _exact_right(y, ones_h) * (1.0 / HEAD_DIM)
    yc = y - mean
    var = _exact_right(yc * yc, ones_h) * (1.0 / HEAD_DIM)
    yn = yc * lax.rsqrt(var + RWKV_GN_EPS) * lg_ref[...] + lb_ref[...]
    o_ref[...] = ((yn + bonf_ref[...] + bonb_ref[...]) * g_ref[...].astype(F32)).astype(o_ref.dtype)


def rwkv_mix(rest, mu, w0, w_up, a0, a_up, g_up, k_k, k_a, r_k, lnx_g, lnx_b):
    bsz, l, n_cols = rest.shape
    width = w0.shape[1]
    rank = w_up.shape[1]
    assert n_cols == 3 * width + 2 * LANES and 2 * rank == LANES and l % RWKV_HALO == 0
    c = RWKV_CHUNK
    n_chunks = -(-l // c)
    per = c // RWKV_HALO
    n_halo = l // RWKV_HALO
    zeros = jnp.zeros((2, rank, width), F32)
    w_wa = jnp.concatenate([jnp.concatenate([w_up.astype(F32), zeros], axis=2),
                            jnp.concatenate([zeros, a_up.astype(F32)], axis=2)], axis=1)
    wa_hi, wa_lo = _split_bf16(w_wa)

    fwd_chunk = lambda i: i
    bwd_chunk = lambda i: n_chunks - 1 - i

    def tile_specs(chunk_of):
        return [
            pl.BlockSpec((None, c, n_cols), lambda b, i: (b, chunk_of(i), 0)),
            pl.BlockSpec((None, RWKV_HALO, n_cols), lambda b, i: (b, jnp.maximum(chunk_of(i) * per - 1, 0), 0)),
            pl.BlockSpec((None, RWKV_HALO, n_cols),
                         lambda b, i: (b, jnp.minimum((chunk_of(i) + 1) * per, n_halo - 1), 0)),
        ]

    row2 = lambda a: a.astype(F32).reshape(1, -1)
    whole = lambda *shape: pl.BlockSpec(shape, lambda b, i: (0,) * len(shape))
    out_spec = lambda chunk_of: pl.BlockSpec((None, c, width), lambda b, i: (b, chunk_of(i), 0))
    act = lambda dt: jax.ShapeDtypeStruct((bsz, l, width), dt)
    y_f, y_b, bon_f, bon_b, g = pl.pallas_call(
        functools.partial(_rwkv_scan_kernel, seq_len=l, width=width),
        grid=(bsz, n_chunks),
        in_specs=tile_specs(fwd_chunk) + tile_specs(bwd_chunk) + [
            whole(1, n_cols),
            whole(2, 1, width), whole(2, 1, width),
            whole(2, LANES, 2 * width), whole(2, LANES, 2 * width),
            whole(LANES, width),
            whole(1, width), whole(1, width), whole(1, width),
        ],
        out_specs=[out_spec(fwd_chunk), out_spec(bwd_chunk), out_spec(fwd_chunk), out_spec(bwd_chunk),
                   out_spec(fwd_chunk)],
        out_shape=[act(F32), act(F32), act(F32), act(F32), act(BF16)],
        scratch_shapes=[pltpu.VMEM((2 * (width // LANES), LANES, LANES), F32)],
        compiler_params=_cparams("parallel", "arbitrary"),
        name="rwkv_scan",
    )(rest, rest, rest, rest, rest, rest, row2(mu), w0.astype(F32).reshape(2, 1, width),
      a0.astype(F32).reshape(2, 1, width), wa_hi, wa_lo, g_up.astype(BF16), row2(k_k), row2(k_a), row2(r_k))

    n = bsz * l
    tm = _row_tile(n, 608)
    rows = lambda: pl.BlockSpec((tm, width), lambda j: (j, 0))
    flat = lambda a: a.reshape(n, width)
    return pl.pallas_call(
        _rwkv_finish_kernel,
        grid=(n // tm,),
        in_specs=[rows(), rows(), rows(), rows(), rows(),
                  pl.BlockSpec((1, width), lambda j: (0, 0)),
                  pl.BlockSpec((1, width), lambda j: (0, 0))],
        out_specs=rows(),
        out_shape=jax.ShapeDtypeStruct((n, width), BF16),
        compiler_params=_cparams("parallel"),
        name="rwkv_finish",
    )(flat(y_f), flat(y_b), flat(bon_f), flat(bon_b), flat(g), row2(lnx_g), row2(lnx_b)).reshape(bsz, l, width)


S5_CHUNK = 16


def _cpow(n, lr, li, step):
    mag = jnp.exp(n * (lr * step))
    ang = n * (li * step)
    return mag * jnp.cos(ang), mag * jnp.sin(ang)


def _s5_param_kernel(lamr_ref, lamc_ref, stepr_ref, stepc_ref, bt_ref, ct_ref,
                     kmat_ref, wst_ref, cexp_ref, alpha_ref):
    t_len = S5_CHUNK
    n_i = S5_GROUP_CH
    p2 = 2 * S5_STATE
    ti = t_len * n_i

    lr = lamr_ref[0:1, :]
    li = lamr_ref[1:2, :]
    step = jnp.exp(stepr_ref[...])
    ab_re, ab_im = _cpow(1.0, lr, li, step)
    den = lr * lr + li * li
    z_re = ((ab_re - 1.0) * lr + ab_im * li) / den
    z_im = (ab_im * lr - (ab_re - 1.0) * li) / den
    bt_re = bt_ref[0]
    bt_im = bt_ref[1]
    bb_re = z_re * bt_re - z_im * bt_im
    bb_im = z_re * bt_im + z_im * bt_re
    tau = (lax.broadcasted_iota(I32, (ti, p2), 0) // n_i).astype(F32)
    is_f = lax.broadcasted_iota(I32, (ti, p2), 1) < S5_STATE
    n_w = jnp.where(is_f, (t_len - 1.0) - tau, tau)
    pw_re, pw_im = _cpow(n_w, lr, li, step)
    w_re = pw_re * bb_re - pw_im * bb_im
    w_im = pw_re * bb_im + pw_im * bb_re
    wst_ref[:, 0:p2] = w_re.astype(wst_ref.dtype)
    wst_ref[:, p2:2 * p2] = w_im.astype(wst_ref.dtype)
    al_re, al_im = _cpow(float(t_len), lr, li, step)
    alpha_ref[0:1, :] = al_re
    alpha_ref[1:2, :] = al_im

    lr_c = lamc_ref[:, 0:1]
    li_c = lamc_ref[:, 1:2]
    step_c = jnp.exp(stepc_ref[...])
    t_l = (lax.broadcasted_iota(I32, (p2, ti), 1) // n_i).astype(F32)
    row_f = lax.broadcasted_iota(I32, (p2, ti), 0) < S5_STATE
    ct_re = ct_ref[0]
    ct_im = ct_ref[1]

    def c_times_pow(n):
        q_re, q_im = _cpow(n, lr_c, li_c, step_c)
        return ct_re * q_re - ct_im * q_im, ct_re * q_im + ct_im * q_re

    n_tap = jnp.where(row_f, t_l, jnp.where(t_l == 0.0, 0.0, t_len - t_l))
    ca_re, ca_im = c_times_pow(n_tap)
    lane_p = lax.broadcasted_iota(I32, (n_i, p2), 1)
    bbr = bb_re[0:n_i]
    bbi = bb_im[0:n_i]
    zero = jnp.zeros_like(bbr)
    strips = []
    for sel in (lane_p < S5_STATE, lane_p >= S5_STATE):
        strips.append(_mm3(jnp.where(sel, bbr, zero), ca_re) - _mm3(jnp.where(sel, bbi, zero), ca_im))
    strip_f, strip_b = strips
    t_k = lax.broadcasted_iota(I32, (n_i, ti), 1) // n_i
    for tt in range(t_len):
        sf = strip_f if tt == 0 else pltpu.roll(strip_f, tt * n_i, 1)
        sb = strip_b if tt == 0 else pltpu.roll(strip_b, tt * n_i, 1)
        blk = jnp.where(t_k >= tt, sf, 0.0) + jnp.where(t_k <= tt, sb, 0.0)
        kmat_ref[tt * n_i:(tt + 1) * n_i, :] = blk.astype(kmat_ref.dtype)

    n_out = jnp.where(row_f, t_l + 1.0, t_len - t_l)
    co_re, co_im = c_times_pow(n_out)
    cexp_ref[0:p2, :] = co_re.astype(cexp_ref.dtype)
    cexp_ref[p2:2 * p2, :] = (-co_im).astype(cexp_ref.dtype)


def _s5_main_kernel(u_ref, kmat_ref, wst_ref, cexp_ref, alpha_ref, y_ref, x_ref, sf_ref, sb_ref,
                    *, n_batch, n_chunks):
    p2 = 2 * S5_STATE
    u = u_ref[...].astype(BF16)
    x_ref[...] = _dot(u, wst_ref[...])
    a_re = alpha_ref[0:1, :]
    a_im = alpha_ref[1:2, :]
    lane = lax.broadcasted_iota(I32, (1, p2), 1)
    is_f = lane < S5_STATE

    def step(i, carry):
        new = []
        for b in range(n_batch):
            s_re, s_im = carry[b]
            row_f = b * n_chunks + i
            row_b = b * n_chunks + (n_chunks - 1 - i)
            xf = x_ref[pl.ds(row_f, 1), :]
            xb = x_ref[pl.ds(row_b, 1), :]
            s_cat = jnp.concatenate([s_re, s_im], axis=1)
            sf_ref[pl.ds(row_f, 1), :] = s_cat
            sb_ref[pl.ds(row_b, 1), :] = s_cat
            x_re = jnp.where(is_f, xf[:, 0:p2], xb[:, 0:p2])
            x_im = jnp.where(is_f, xf[:, p2:2 * p2], xb[:, p2:2 * p2])
            new.append((a_re * s_re - a_im * s_im + x_re, a_re * s_im + a_im * s_re + x_im))
        return tuple(new)

    zero = jnp.zeros((1, p2), F32)
    lax.fori_loop(0, n_chunks, step, tuple((zero, zero) for _ in range(n_batch)))
    lane2 = lax.broadcasted_iota(I32, sf_ref.shape, 1) % p2
    s_in = jnp.where(lane2 < S5_STATE, sf_ref[...], sb_ref[...])
    s_hi, s_lo = _split_bf16(s_in)
    y_ref[...] = _dot(u, kmat_ref[...]) + _dot(s_hi, cexp_ref[...]) + _dot(s_lo, cexp_ref[...])


def _rms_kernel(h_ref, g_ref, o_ref):
    o_ref[...] = _rms(h_ref[...], g_ref[...]).astype(o_ref.dtype)


def _gelu_tanh(x):
    return 0.5 * x * (1.0 + jnp.tanh(math.sqrt(2.0 / math.pi) * (x + 0.044715 * (x * x * x))))


def _s5_glu_kernel(h_ref, y_ref, g_ref, d_ref, w_ref, o_ref):
    h = h_ref[...]
    dm = h.shape[1]
    y = y_ref[...] + d_ref[...] * _rms(h, g_ref[...])
    gl = _gelu_tanh(y).astype(BF16)
    a = _dot(gl, w_ref[:, 0:dm])
    b = _dot(gl, w_ref[:, dm:2 * dm])
    o_ref[...] = h + a * jax.nn.sigmoid(b)


def s5_mix(h3, gain, b_re, b_im, lam_re, lam_im, log_step, c_re, c_im, d_skip, w_glu):
    bsz, l, dm = h3.shape
    n_g, n_p, n_i = b_re.shape
    t_len = S5_CHUNK
    assert l % t_len == 0 and n_g * n_i == dm and n_p == S5_STATE and n_i == S5_GROUP_CH
    n_chunks = l // t_len
    m = bsz * n_chunks
    ti = t_len * n_i
    p2 = 2 * n_p
    n = bsz * l
    tm = _row_tile(n, 608)
    h2 = h3.reshape(n, dm)
    gain2 = gain.astype(F32).reshape(1, dm)

    hn = pl.pallas_call(
        _rms_kernel,
        grid=(n // tm,),
        in_specs=[pl.BlockSpec((tm, dm), lambda i: (i, 0)), pl.BlockSpec((1, dm), lambda i: (0, 0))],
        out_specs=pl.BlockSpec((tm, dm), lambda i: (i, 0)),
        out_shape=jax.ShapeDtypeStruct((n, dm), F32),
        compiler_params=_cparams("parallel"),
        name="s5_rms",
    )(h2, gain2)
    u = jnp.transpose(hn.reshape(bsz, n_chunks, t_len, n_g, n_i), (3, 0, 1, 2, 4)).reshape(n_g, m, ti)

    f32 = lambda a: a.astype(F32)
    lam_r = jnp.stack([jnp.concatenate([f32(lam_re)[0], f32(lam_re)[1]], axis=-1),
                       jnp.concatenate([f32(lam_im)[0], f32(lam_im)[1]], axis=-1)], axis=1)
    lam_c = jnp.transpose(lam_r, (0, 2, 1))
    step_r = jnp.repeat(jnp.transpose(f32(log_step))[:, None, :], n_p, axis=2)
    step_c = jnp.transpose(step_r, (0, 2, 1))
    bt = jnp.stack([jnp.transpose(f32(b_re), (0, 2, 1)), jnp.transpose(f32(b_im), (0, 2, 1))], axis=1)
    bt = jnp.tile(bt, (1, 1, t_len, 2))
    ct = jnp.stack([f32(c_re), f32(c_im)], axis=0)
    ct = jnp.transpose(ct, (2, 0, 1, 4, 3)).reshape(n_g, 2, p2, n_i)
    ct = jnp.tile(ct, (1, 1, 1, t_len))

    gspec = lambda *shape: pl.BlockSpec((None,) + shape, lambda g: (g,) + (0,) * len(shape))
    kmat, wst, cexp, alpha = pl.pallas_call(
        _s5_param_kernel,
        grid=(n_g,),
        in_specs=[gspec(2, p2), gspec(p2, 2), gspec(1, p2), gspec(p2, 1), gspec(2, ti, p2), gspec(2, p2, ti)],
        out_specs=[gspec(ti, ti), gspec(ti, 2 * p2), gspec(2 * p2, ti), gspec(2, p2)],
        out_shape=[
            jax.ShapeDtypeStruct((n_g, ti, ti), BF16),
            jax.ShapeDtypeStruct((n_g, ti, 2 * p2), BF16),
            jax.ShapeDtypeStruct((n_g, 2 * p2, ti), BF16),
            jax.ShapeDtypeStruct((n_g, 2, p2), F32),
        ],
        compiler_params=_cparams("parallel"),
        name="s5_params",
    )(lam_r, lam_c, step_r, step_c, bt, ct)

    y = pl.pallas_call(
        functools.partial(_s5_main_kernel, n_batch=bsz, n_chunks=n_chunks),
        grid=(n_g,),
        in_specs=[gspec(m, ti), gspec(ti, ti), gspec(ti, 2 * p2), gspec(2 * p2, ti), gspec(2, p2)],
        out_specs=gspec(m, ti),
        out_shape=jax.ShapeDtypeStruct((n_g, m, ti), F32),
        scratch_shapes=[pltpu.VMEM((m, 2 * p2), F32), pltpu.VMEM((m, 2 * p2), F32), pltpu.VMEM((m, 2 * p2), F32)],
        compiler_params=_cparams("parallel"),
        name="s5_main",
    )(u, kmat, wst, cexp, alpha)
    y2 = jnp.transpose(y.reshape(n_g, bsz, n_chunks, t_len, n_i), (1, 2, 3, 0, 4)).reshape(n, dm)

    out = pl.pallas_call(
        _s5_glu_kernel,
        grid=(n // tm,),
        in_specs=[
            pl.BlockSpec((tm, dm), lambda i: (i, 0)),
            pl.BlockSpec((tm, dm), lambda i: (i, 0)),
            pl.BlockSpec((1, dm), lambda i: (0, 0)),
            pl.BlockSpec((1, dm), lambda i: (0, 0)),
            pl.BlockSpec((dm, 2 * dm), lambda i: (0, 0)),
        ],
        out_specs=pl.BlockSpec((tm, dm), lambda i: (i, 0)),
        out_shape=jax.ShapeDtypeStruct((n, dm), F32),
        compiler_params=_cparams("parallel"),
        name="s5_glu",
    )(h2, y2, gain2, f32(d_skip).reshape(1, dm), w_glu.astype(BF16))
    return out.reshape(bsz, l, dm)


def na_rwkv_mix(h3, gain, w_in, w_out, rpb, mu, w0, w_up, a0, a_up, g_up, k_k, k_a, r_k, lnx_g, lnx_b):
    bsz, l, dm = h3.shape
    n = bsz * l
    h2 = h3.reshape(n, dm)
    n_qkv = 3 * (w_out.shape[0] // 2)
    qkv, rest = norm_inproj(h2, gain.astype(F32), w_in.astype(BF16), n_qkv)
    na = na_attention(qkv.reshape(bsz, l, n_qkv), rpb)
    rw = rwkv_mix(rest.reshape(bsz, l, -1), mu, w0, w_up, a0, a_up, g_up, k_k, k_a, r_k, lnx_g, lnx_b)
    out = outproj_residual(h2, na.reshape(n, -1), rw.reshape(n, -1), w_out.astype(BF16))
    return out.reshape(bsz, l, dm)


def kernel(x, meta_tokens, norm_mix, norm_ffn, norm_final, mix_w_in, mix_w_out, na_rpb, rwkv_mu,
           rwkv_w0, rwkv_w_up, rwkv_a0, rwkv_a_up, rwkv_g_up, rwkv_k_k, rwkv_k_a, rwkv_r_k,
           rwkv_lnx_g, rwkv_lnx_b, s5_b_re, s5_b_im, s5_lambda_re, s5_lambda_im, s5_log_step,
           s5_c_re, s5_c_im, s5_d, s5_w_glu, moe_w_group, moe_b_group, moe_w_expert, moe_b_expert,
           moe_w1, moe_w3, moe_w2):
    bsz, _, dm = x.shape
    depth = norm_mix.shape[0]
    meta = jnp.broadcast_to(meta_tokens.astype(x.dtype)[None], (bsz,) + meta_tokens.shape)
    h = jnp.concatenate([meta, x], axis=1)
    l = h.shape[1]
    for layer in range(depth):
        i = layer // 2
        if layer % 2 == 0:
            h = na_rwkv_mix(h, norm_mix[layer], mix_w_in[i], mix_w_out[i], na_rpb[i], rwkv_mu[i], rwkv_w0[i],
                            rwkv_w_up[i], rwkv_a0[i], rwkv_a_up[i], rwkv_g_up[i], rwkv_k_k[i], rwkv_k_a[i],
                            rwkv_r_k[i], rwkv_lnx_g[i], rwkv_lnx_b[i])
        else:
            h = s5_mix(h, norm_mix[layer], s5_b_re[i], s5_b_im[i], s5_lambda_re[i], s5_lambda_im[i],
                       s5_log_step[i], s5_c_re[i], s5_c_im[i], s5_d[i], s5_w_glu[i])
        h = hierarchical_moe_residual(h.reshape(bsz * l, dm), norm_ffn[layer].astype(F32), moe_w_group[layer],
                                      moe_b_group[layer], moe_w_expert[layer], moe_b_expert[layer],
                                      moe_w1[layer], moe_w3[layer], moe_w2[layer]).reshape(bsz, l, dm)
    return final_norm(h, norm_final.astype(F32))
```

```python
import functools
import math

import jax
import jax.numpy as jnp
from jax import lax
from jax.experimental import pallas as pl
from jax.experimental.pallas import tpu as pltpu

F32 = jnp.float32
BF16 = jnp.bfloat16
I32 = jnp.int32

N_META = 16
GRID_W = 64
HEAD_DIM = 64
NA_WIN_ROWS = 8
NA_WIN_COLS = 16
S5_GROUP_CH = 16
S5_STATE = 64
MOE_GROUPS = 4
MOE_PER_GROUP = 8
MOE_EXPERTS = MOE_GROUPS * MOE_PER_GROUP
NORM_EPS = 1e-6
RWKV_GN_EPS = 64e-5
NEG_INF = -1e30

LANES = 128
SUBLANES_BF16 = 16
VMEM_LIMIT_BYTES = 56 * 1024 * 1024

MOE_TILE = 256
ROUTER_LANES = 128


def _cparams(*sem):
    return pltpu.CompilerParams(dimension_semantics=sem, vmem_limit_bytes=VMEM_LIMIT_BYTES)


def _row_tile(n, target):
    best = None
    for t in range(SUBLANES_BF16, min(n, target) + 1, SUBLANES_BF16):
        if n % t == 0:
            best = t
    assert best is not None, (n, target)
    return best


def _rms(x, gain):
    ms = jnp.mean(x * x, axis=-1, keepdims=True)
    return (x * lax.rsqrt(ms + NORM_EPS)) * gain


def _split_bf16(x):
    hi = x.astype(BF16)
    lo = (x - hi.astype(F32)).astype(BF16)
    return hi, lo


def _dot(a, b):
    return jnp.dot(a, b, preferred_element_type=F32)


def _dot_nt(a, b):
    return lax.dot_general(a, b, (((1,), (1,)), ((), ())), preferred_element_type=F32)


def _norm_inproj_kernel(h_ref, g_ref, w_ref, qkv_ref, rest_ref, xn_ref, *, n_qkv, chunk):
    xn_ref[...] = _rms(h_ref[...], g_ref[...]).astype(BF16)
    n_all = w_ref.shape[1]
    for c in range(0, n_all, chunk):
        y = _dot(xn_ref[...], w_ref[:, c:c + chunk])
        if c < n_qkv:
            qkv_ref[:, c:c + chunk] = y.astype(BF16)
        else:
            rest_ref[:, c - n_qkv:c - n_qkv + chunk] = y


def norm_inproj(h2, gain, w_bf16, n_qkv):
    n, d = h2.shape
    n_all = w_bf16.shape[1]
    tm = _row_tile(n, 608)
    chunk = 256
    assert n_qkv % chunk == 0 and n_all % chunk == 0
    return pl.pallas_call(
        functools.partial(_norm_inproj_kernel, n_qkv=n_qkv, chunk=chunk),
        grid=(n // tm,),
        in_specs=[
            pl.BlockSpec((tm, d), lambda i: (i, 0)),
            pl.BlockSpec((1, d), lambda i: (0, 0)),
            pl.BlockSpec((d, n_all), lambda i: (0, 0)),
        ],
        out_specs=[
            pl.BlockSpec((tm, n_qkv), lambda i: (i, 0)),
            pl.BlockSpec((tm, n_all - n_qkv), lambda i: (i, 0)),
        ],
        out_shape=[
            jax.ShapeDtypeStruct((n, n_qkv), BF16),
            jax.ShapeDtypeStruct((n, n_all - n_qkv), F32),
        ],
        scratch_shapes=[pltpu.VMEM((tm, d), BF16)],
        compiler_params=_cparams("parallel"),
        name="norm_inproj",
    )(h2, gain.reshape(1, d), w_bf16)


def _outproj_kernel(h_ref, na_ref, rw_ref, wa_ref, wb_ref, o_ref):
    acc = _dot(na_ref[...], wa_ref[...])
    acc = acc + _dot(rw_ref[...], wb_ref[...])
    o_ref[...] = h_ref[...] + acc


def outproj_residual(h2, na, rw, w_out_bf16):
    n, d = h2.shape
    ka, kb = na.shape[1], rw.shape[1]
    tm = _row_tile(n, 608)
    return pl.pallas_call(
        _outproj_kernel,
        grid=(n // tm,),
        in_specs=[
            pl.BlockSpec((tm, d), lambda i: (i, 0)),
            pl.BlockSpec((tm, ka), lambda i: (i, 0)),
            pl.BlockSpec((tm, kb), lambda i: (i, 0)),
            pl.BlockSpec((ka, d), lambda i: (0, 0)),
            pl.BlockSpec((kb, d), lambda i: (0, 0)),
        ],
        out_specs=pl.BlockSpec((tm, d), lambda i: (i, 0)),
        out_shape=jax.ShapeDtypeStruct((n, d), F32),
        compiler_params=_cparams("parallel"),
        name="outproj_residual",
    )(h2, na, rw, w_out_bf16[:ka], w_out_bf16[ka:])


def _store_token_tiles(ref, x):
    rows = x.shape[0]
    s_n = x.shape[1] // LANES
    for s in range(s_n):
        ref[pl.ds(s, rows, stride=s_n), :] = x[:, s * LANES:(s + 1) * LANES]


def _load_token_tile_cols(ref, s, rows, s_n):
    return ref[pl.ds(s, rows, stride=s_n), :]


def _router_kernel(h_ref, g_ref, whi_ref, wlo_ref, b_ref, xn_ref, route_ref):
    xn = _rms(h_ref[...], g_ref[...])
    x_hi, x_lo = _split_bf16(xn)
    _store_token_tiles(xn_ref, xn)
    logits = (_dot(x_hi, whi_ref[...]) + _dot(x_hi, wlo_ref[...]) + _dot(x_lo, whi_ref[...])
              + b_ref[...])
    tm = logits.shape[0]
    lane = lax.broadcasted_iota(I32, (tm, ROUTER_LANES), 1)
    big = jnp.int32(ROUTER_LANES)

    is_g = lane < MOE_GROUPS
    lg = jnp.where(is_g, logits, -jnp.inf)
    eg = jnp.where(is_g, jnp.exp(lg - jnp.max(lg, axis=-1, keepdims=True)), 0.0)
    pg = eg / jnp.sum(eg, axis=-1, keepdims=True)
    p_grp = jnp.max(pg, axis=-1, keepdims=True)
    grp = jnp.min(jnp.where(is_g & (pg == p_grp), lane, big), axis=-1, keepdims=True)

    lo_lane = MOE_GROUPS + MOE_PER_GROUP * grp
    is_e = (lane >= lo_lane) & (lane < lo_lane + MOE_PER_GROUP)
    le = jnp.where(is_e, logits, -jnp.inf)
    ee = jnp.where(is_e, jnp.exp(le - jnp.max(le, axis=-1, keepdims=True)), 0.0)
    pe = jnp.where(is_e, ee / jnp.sum(ee, axis=-1, keepdims=True), -1.0)
    p1 = jnp.max(pe, axis=-1, keepdims=True)
    i1 = jnp.min(jnp.where(pe == p1, lane, big), axis=-1, keepdims=True)
    pe2 = jnp.where(lane == i1, -1.0, pe)
    p2 = jnp.max(pe2, axis=-1, keepdims=True)
    i2 = jnp.min(jnp.where(pe2 == p2, lane, big), axis=-1, keepdims=True)
    denom = p1 + p2
    g1 = p_grp * p1 / denom
    g2 = p_grp * p2 / denom
    e1 = (i1 - MOE_GROUPS).astype(F32)
    e2 = (i2 - MOE_GROUPS).astype(F32)
    route_ref[...] = jnp.where(lane == 0, e1, jnp.where(lane == 1, e2, jnp.where(lane == 2, g1, g2)))


def moe_router(h2, gain, w_group, b_group, w_expert, b_expert):
    n, d = h2.shape
    n_r = MOE_GROUPS + MOE_EXPERTS
    w_r = jnp.concatenate([w_group, jnp.transpose(w_expert, (1, 0, 2)).reshape(d, MOE_EXPERTS)], axis=1)
    w_r = jnp.pad(w_r.astype(F32), ((0, 0), (0, ROUTER_LANES - n_r)))
    w_hi, w_lo = _split_bf16(w_r)
    b_r = jnp.pad(jnp.concatenate([b_group, b_expert.reshape(-1)]).astype(F32), (0, ROUTER_LANES - n_r))
    tm = _row_tile(n, 608)
    return pl.pallas_call(
        _router_kernel,
        grid=(n // tm,),
        in_specs=[
            pl.BlockSpec((tm, d), lambda i: (i, 0)),
            pl.BlockSpec((1, d), lambda i: (0, 0)),
            pl.BlockSpec((d, ROUTER_LANES), lambda i: (0, 0)),
            pl.BlockSpec((d, ROUTER_LANES), lambda i: (0, 0)),
            pl.BlockSpec((1, ROUTER_LANES), lambda i: (0, 0)),
        ],
        out_specs=[
            pl.BlockSpec((tm * (d // LANES), LANES), lambda i: (i, 0)),
            pl.BlockSpec((tm, ROUTER_LANES), lambda i: (i, 0)),
        ],
        out_shape=[
            jax.ShapeDtypeStruct((n * (d // LANES), LANES), F32),
            jax.ShapeDtypeStruct((n, ROUTER_LANES), F32),
        ],
        compiler_params=_cparams("parallel"),
        name="moe_router",
    )(h2, gain.reshape(1, d), w_hi, w_lo, b_r.reshape(1, ROUTER_LANES))


def _expert_kernel(blk_e_ref, n_used_ref, src_ref, src_next_ref, dst_ref, xn_hbm, w1_ref, w3_ref, w2_ref, y_hbm,
                   xg_ref, xb_ref, yv_ref, gsem, ssem, w1b_ref, w3b_ref, w2b_ref, *, trash_row):
    i = pl.program_id(0)
    n_steps = pl.num_programs(0)
    n_used = n_used_ref[0]
    used = i < n_used
    slot = i & 1
    tile, d = xb_ref.shape
    s_n = d // LANES

    def start_gather(ids_ref, dst_slot):
        for r in range(tile):
            src = pl.multiple_of(ids_ref[0, r], s_n)
            pltpu.make_async_copy(xn_hbm.at[pl.ds(src, s_n), :], xg_ref.at[dst_slot, pl.ds(r * s_n, s_n), :],
                                  gsem.at[dst_slot]).start()

    def wait_rows_in(s):
        pltpu.make_async_copy(xn_hbm.at[pl.ds(0, tile * s_n), :], xg_ref.at[s], gsem.at[s]).wait()

    def wait_rows_out(s):
        pltpu.make_async_copy(yv_ref.at[s], y_hbm.at[pl.ds(0, tile * s_n), :], ssem.at[s]).wait()

    @pl.when(used & (i == 0))
    def _():
        start_gather(src_ref, 0)

    @pl.when(i + 1 < n_used)
    def _():
        start_gather(src_next_ref, 1 - slot)

    prev_e = blk_e_ref[jnp.maximum(i - 1, 0)]
    fresh = (i == 0) | (blk_e_ref[i] != prev_e)

    @pl.when(used & fresh)
    def _():
        w1b_ref[...] = w1_ref[...].astype(BF16)
        w3b_ref[...] = w3_ref[...].astype(BF16)
        w2b_ref[...] = w2_ref[...].astype(BF16)

    @pl.when(used & (i >= 2))
    def _():
        wait_rows_out(slot)

    @pl.when(used)
    def _():
        wait_rows_in(slot)
        for s in range(s_n):
            xb_ref[:, s * LANES:(s + 1) * LANES] = _load_token_tile_cols(xg_ref.at[slot], s, tile, s_n).astype(BF16)
        x = xb_ref[...]
        a = _dot(x, w1b_ref[...])
        b = _dot(x, w3b_ref[...])
        hmid = (a * jax.nn.sigmoid(a) * b).astype(BF16)
        _store_token_tiles(yv_ref.at[slot], _dot(hmid, w2b_ref[...]))
        for r in range(tile):
            dst = pl.multiple_of(dst_ref[0, r], s_n)
            pltpu.make_async_copy(yv_ref.at[slot, pl.ds(r * s_n, s_n), :], y_hbm.at[pl.ds(dst, s_n), :],
                                  ssem.at[slot]).start()

    @pl.when(i == n_steps - 1)
    def _():
        @pl.when(n_used >= 1)
        def _():
            wait_rows_out((n_used - 1) & 1)

        @pl.when(n_used >= 2)
        def _():
            wait_rows_out((n_used - 2) & 1)

        yv_ref[0] = jnp.zeros(yv_ref.shape[1:], yv_ref.dtype)
        fills = [pltpu.make_async_copy(yv_ref.at[0], y_hbm.at[pl.ds((trash_row + s * tile) * s_n, tile * s_n), :],
                                       ssem.at[s]) for s in range(2)]
        for cp in fills:
            cp.start()
        for cp in fills:
            cp.wait()


def moe_experts(xn_tiles, src_of_row, dst_of_row, blk_e, n_used, w1, w3, w2, layer, n_out_rows):
    d, f = w1.shape[2], w1.shape[3]
    s_n = d // LANES
    tile = MOE_TILE
    n_blocks = src_of_row.shape[0] // tile
    src3 = src_of_row.reshape(n_blocks, 1, tile)
    dst3 = dst_of_row.reshape(n_blocks, 1, tile)

    def w_map(i, blk_e_ref, n_used_ref):
        return (layer, blk_e_ref[i], 0, 0)

    ids = lambda index: pl.BlockSpec((None, 1, tile), lambda i, be, nu: (index(i), 0, 0), memory_space=pltpu.SMEM)
    grid_spec = pltpu.PrefetchScalarGridSpec(
        num_scalar_prefetch=2,
        grid=(n_blocks,),
        in_specs=[
            ids(lambda i: i),
            ids(lambda i: jnp.minimum(i + 1, n_blocks - 1)),
            ids(lambda i: i),
            pl.BlockSpec(memory_space=pl.ANY),
            pl.BlockSpec((None, None, d, f), w_map),
            pl.BlockSpec((None, None, d, f), w_map),
            pl.BlockSpec((None, None, f, d), w_map),
        ],
        out_specs=pl.BlockSpec(memory_space=pl.ANY),
        scratch_shapes=[
            pltpu.VMEM((2, tile * s_n, LANES), F32),
            pltpu.VMEM((tile, d), BF16),
            pltpu.VMEM((2, tile * s_n, LANES), F32),
            pltpu.SemaphoreType.DMA((2,)),
            pltpu.SemaphoreType.DMA((2,)),
            pltpu.VMEM((d, f), BF16),
            pltpu.VMEM((d, f), BF16),
            pltpu.VMEM((f, d), BF16),
        ],
    )
    return pl.pallas_call(
        functools.partial(_expert_kernel, trash_row=n_out_rows),
        grid_spec=grid_spec,
        out_shape=jax.ShapeDtypeStruct(((n_out_rows + 2 * tile) * s_n, LANES), F32),
        compiler_params=_cparams("arbitrary"),
        name="moe_experts",
    )(blk_e, n_used, src3, src3, dst3, xn_tiles, w1, w3, w2)


def _moe_combine_kernel(h_ref, route_ref, y0_ref, y1_ref, o_ref):
    tm, d = h_ref.shape
    s_n = d // LANES
    route = route_ref[...]
    g1 = route[:, 2:3]
    g2 = route[:, 3:4]
    for s in range(s_n):
        cols = slice(s * LANES, (s + 1) * LANES)
        o_ref[:, cols] = (h_ref[:, cols] + g1 * _load_token_tile_cols(y0_ref, s, tm, s_n)
                          + g2 * _load_token_tile_cols(y1_ref, s, tm, s_n))


def moe_combine(h2, route, y_tiles):
    n, d = h2.shape
    s_n = d // LANES
    tm = _row_tile(n, 608)
    nb = n // tm
    return pl.pallas_call(
        _moe_combine_kernel,
        grid=(nb,),
        in_specs=[
            pl.BlockSpec((tm, d), lambda i: (i, 0)),
            pl.BlockSpec((tm, ROUTER_LANES), lambda i: (i, 0)),
            pl.BlockSpec((tm * s_n, LANES), lambda i: (i, 0)),
            pl.BlockSpec((tm * s_n, LANES), lambda i: (i + nb, 0)),
        ],
        out_specs=pl.BlockSpec((tm, d), lambda i: (i, 0)),
        out_shape=jax.ShapeDtypeStruct((n, d), F32),
        compiler_params=_cparams("parallel"),
        name="moe_combine",
    )(h2, route, y_tiles, y_tiles)


def hierarchical_moe_residual(h2, gain, w_group, b_group, w_expert, b_expert, w1, w3, w2, layer):
    n, d = h2.shape
    xn, route = moe_router(h2, gain, w_group, b_group, w_expert, b_expert)
    e_km = jnp.concatenate([route[:, 0], route[:, 1]]).astype(I32)
    n_assign = 2 * n
    onehot = (e_km[:, None] == jnp.arange(MOE_EXPERTS, dtype=I32)[None, :]).astype(I32)
    csum = jnp.cumsum(onehot, axis=0)
    counts = csum[-1]
    padded = (counts + MOE_TILE - 1) // MOE_TILE * MOE_TILE
    pad_end = jnp.cumsum(padded)
    pad_start = pad_end - padded
    dest = jnp.sum((csum - onehot + pad_start[None, :]) * onehot, axis=1)
    n_blocks = -(-n_assign // MOE_TILE) + MOE_EXPERTS
    n_rows = n_blocks * MOE_TILE
    blk_start = jnp.arange(n_blocks, dtype=I32) * MOE_TILE
    blk_e = jnp.minimum(jnp.sum((pad_end[None, :] <= blk_start[:, None]).astype(I32), axis=1),
                        MOE_EXPERTS - 1).astype(I32)
    n_used = (pad_end[-1] // MOE_TILE).astype(I32).reshape(1)
    asg_of_row = jnp.full((n_rows,), n_assign, I32).at[dest].set(jnp.arange(n_assign, dtype=I32))
    s_n = d // LANES
    is_pad = asg_of_row >= n_assign
    row_id = jnp.arange(n_rows, dtype=I32)
    src_of_row = jnp.where(is_pad, n - 1, jnp.where(asg_of_row >= n, asg_of_row - n, asg_of_row)) * s_n
    dst_of_row = jnp.where(is_pad, n_assign + row_id % (2 * MOE_TILE), asg_of_row) * s_n
    y = moe_experts(xn, src_of_row, dst_of_row, blk_e, n_used, w1, w3, w2, layer, n_assign)
    return moe_combine(h2, route, y)


def _final_norm_kernel(h_ref, g_ref, o_ref):
    o_ref[...] = _rms(h_ref[...], g_ref[...])


def final_norm(h3, gain):
    b, l, d = h3.shape
    t = l - N_META
    tm = _row_tile(t, 512)
    return pl.pallas_call(
        _final_norm_kernel,
        grid=(b, t // tm),
        in_specs=[
            pl.BlockSpec((None, pl.Element(tm), pl.Element(d)),
                         lambda bi, i: (bi, pl.multiple_of(N_META + i * tm, SUBLANES_BF16), 0)),
            pl.BlockSpec((1, d), lambda bi, i: (0, 0)),
        ],
        out_specs=pl.BlockSpec((None, tm, d), lambda bi, i: (bi, i, 0)),
        out_shape=jax.ShapeDtypeStruct((b, t, d), F32),
        compiler_params=_cparams("parallel", "parallel"),
        name="final_norm",
    )(h3, gain.reshape(1, d))


NA_QROWS = 8
NA_KROWS = 3 * NA_QROWS


def _na_kernel(q_ref, kw_ref, vw_ref, qm_ref, km_ref, vm_ref, bias_ref, o_ref, om_ref, *, rows, scale):
    blk = pl.program_id(1)
    tq = GRID_W
    n_pairs = q_ref.shape[1] // LANES
    base = jnp.clip(NA_QROWS * blk - NA_QROWS, 0, rows - NA_KROWS)
    lane = lax.broadcasted_iota(I32, (tq, LANES), 1)
    halves = [lane < HEAD_DIM, lane >= HEAD_DIM]

    def attend(qp, kp, vp, kmp, vmp, bias):
        s_g = _dot_nt(qp, kp) * scale
        if bias is not None:
            s_g = s_g + bias
        s_m = _dot_nt(qp, kmp) * scale
        m = jnp.maximum(jnp.max(s_g, axis=-1, keepdims=True), jnp.max(s_m, axis=-1, keepdims=True))
        p_g = jnp.exp(s_g - m)
        p_m = jnp.exp(s_m - m)
        den = jnp.sum(p_g, axis=-1, keepdims=True) + jnp.sum(p_m, axis=-1, keepdims=True)
        o = _dot(p_g.astype(BF16), vp) + _dot(p_m.astype(BF16), vmp)
        return o / den

    def row_body(j, carry):
        r = NA_QROWS * blk + j
        start = jnp.clip(r - NA_WIN_ROWS // 2, 0, rows - NA_WIN_ROWS)
        s_idx = start - r + (NA_WIN_ROWS - 1)
        koff = pl.multiple_of((start - base) * GRID_W, GRID_W)
        qoff = pl.multiple_of(j * tq, tq)
        for p in range(n_pairs):
            cols = slice(p * LANES, (p + 1) * LANES)
            q_pair = q_ref[pl.ds(qoff, tq), cols]
            kp = kw_ref[pl.ds(koff, NA_WIN_ROWS * GRID_W), cols]
            vp = vw_ref[pl.ds(koff, NA_WIN_ROWS * GRID_W), cols]
            kmp = km_ref[:, cols]
            vmp = vm_ref[:, cols]
            outs = []
            for hh in range(2):
                qp = jnp.where(halves[hh], q_pair, jnp.zeros_like(q_pair))
                outs.append(attend(qp, kp, vp, kmp, vmp, bias_ref[2 * p + hh, s_idx]))
            o_ref[pl.ds(qoff, tq), cols] = jnp.where(halves[0], outs[0], outs[1]).astype(o_ref.dtype)
        return carry

    lax.fori_loop(0, NA_QROWS, row_body, 0)

    @pl.when(blk == 0)
    def _():
        lane_m = lax.broadcasted_iota(I32, (N_META, LANES), 1)
        for p in range(n_pairs):
            cols = slice(p * LANES, (p + 1) * LANES)
            q_pair = qm_ref[:, cols]
            kmp = km_ref[:, cols]
            vmp = vm_ref[:, cols]
            outs = []
            for hh in range(2):
                sel = (lane_m < HEAD_DIM) if hh == 0 else (lane_m >= HEAD_DIM)
                qp = jnp.where(sel, q_pair, jnp.zeros_like(q_pair))
                s_m = _dot_nt(qp, kmp) * scale
                p_m = jnp.exp(s_m - jnp.max(s_m, axis=-1, keepdims=True))
                den = jnp.sum(p_m, axis=-1, keepdims=True)
                outs.append(_dot(p_m.astype(BF16), vmp) / den)
            om_ref[:, cols] = jnp.where(lane_m < HEAD_DIM, outs[0], outs[1]).astype(om_ref.dtype)


def _na_bias_table(rpb):
    h = rpb.shape[0]
    c_ids = jnp.arange(GRID_W)
    c_start = jnp.clip(c_ids - NA_WIN_COLS // 2, 0, GRID_W - NA_WIN_COLS)
    in_band = (c_ids[None, :] >= c_start[:, None]) & (c_ids[None, :] < c_start[:, None] + NA_WIN_COLS)
    dc = jnp.clip(c_ids[None, :] - c_ids[:, None] + NA_WIN_COLS - 1, 0, 2 * NA_WIN_COLS - 2)
    tab = jnp.where(in_band[None, None], rpb.astype(F32)[:, :, dc], NEG_INF)
    win = jnp.stack([tab[:, s:s + NA_WIN_ROWS] for s in range(NA_WIN_ROWS)], axis=1)
    return jnp.transpose(win, (0, 1, 3, 2, 4)).reshape(h, NA_WIN_ROWS, GRID_W, NA_WIN_ROWS * GRID_W)


def na_attention(qkv, rpb):
    b, l, w3 = qkv.shape
    w = w3 // 3
    t = l - N_META
    rows = t // GRID_W
    assert rows * GRID_W == t and rows % NA_QROWS == 0 and rows >= NA_KROWS
    tq = NA_QROWS * GRID_W
    tk = NA_KROWS * GRID_W
    bias = _na_bias_table(rpb)
    al = SUBLANES_BF16

    def q_map(bi, i):
        return (bi, pl.multiple_of(N_META + i * tq, al), 0)

    def kv_map(col):
        def f(bi, i):
            base = jnp.clip(NA_QROWS * i - NA_QROWS, 0, rows - NA_KROWS)
            return (bi, pl.multiple_of(N_META + base * GRID_W, al), col)
        return f

    def meta_map(col):
        return lambda bi, i: (bi, 0, col)

    el = pl.Element
    grid_out, meta_out = pl.pallas_call(
        functools.partial(_na_kernel, rows=rows, scale=HEAD_DIM ** -0.5),
        grid=(b, rows // NA_QROWS),
        in_specs=[
            pl.BlockSpec((None, el(tq), el(w)), q_map),
            pl.BlockSpec((None, el(tk), el(w)), kv_map(w)),
            pl.BlockSpec((None, el(tk), el(w)), kv_map(2 * w)),
            pl.BlockSpec((None, el(N_META), el(w)), meta_map(0)),
            pl.BlockSpec((None, el(N_META), el(w)), meta_map(w)),
            pl.BlockSpec((None, el(N_META), el(w)), meta_map(2 * w)),
            pl.BlockSpec(bias.shape, lambda bi, i: (0, 0, 0, 0)),
        ],
        out_specs=[
            pl.BlockSpec((None, tq, w), lambda bi, i: (bi, i, 0)),
            pl.BlockSpec((None, N_META, w), lambda bi, i: (bi, 0, 0)),
        ],
        out_shape=[
            jax.ShapeDtypeStruct((b, t, w), BF16),
            jax.ShapeDtypeStruct((b, N_META, w), BF16),
        ],
        compiler_params=_cparams("parallel", "arbitrary"),
        name="na_attention",
    )(qkv, qkv, qkv, qkv, qkv, qkv, bias)
    return jnp.concatenate([meta_out, grid_out], axis=1)


RWKV_CHUNK = 64
RWKV_HALO = 8


def _split3_bf16(x):
    p1 = x.astype(BF16)
    r1 = x - p1.astype(F32)
    p2 = r1.astype(BF16)
    p3 = (r1 - p2.astype(F32)).astype(BF16)
    return p1, p2, p3


def _mm1(a, b):
    return _dot(a.astype(BF16), b.astype(BF16))


def _mm3(a, b):
    ah, al = _split_bf16(a)
    bh, bl = _split_bf16(b)
    return _dot(ah, bh) + _dot(ah, bl) + _dot(al, bh)


def _mm1_nt(a, b):
    return _dot_nt(a.astype(BF16), b.astype(BF16))


def _mm3_nt(a, b):
    ah, al = _split_bf16(a)
    bh, bl = _split_bf16(b)
    return _dot_nt(ah, bh) + _dot_nt(ah, bl) + _dot_nt(al, bh)


def _exact_left(mat_bf16, x):
    p1, p2, p3 = _split3_bf16(x)
    return _dot(mat_bf16, p1) + _dot(mat_bf16, p2) + _dot(mat_bf16, p3)


def _exact_right(x, mat_bf16):
    p1, p2, p3 = _split3_bf16(x)
    return _dot(p1, mat_bf16) + _dot(p2, mat_bf16) + _dot(p3, mat_bf16)


def _head_block_ones(width):
    ri = lax.broadcasted_iota(I32, (width, width), 0) // HEAD_DIM
    ci = lax.broadcasted_iota(I32, (width, width), 1) // HEAD_DIM
    return (ri == ci).astype(BF16)


def _stack_heads(x, m0):
    z = jnp.zeros_like(x)
    return jnp.concatenate([jnp.where(m0, x, z), jnp.where(m0, z, x)], axis=0)


_MM_L4 = _mm1_nt
_MM_KT = _mm3_nt
_MM_SQ = _mm1
_MM_AP = _mm1
_MM_V = _mm1
_MM_Y = _mm1
_MM_UPD = _mm3


def _rwkv_chunk_streams(streams, c):
    c2 = 2 * c
    lane = lax.broadcasted_iota(I32, (c, LANES), 1)
    m0 = lane < HEAD_DIM
    t_i = lax.broadcasted_iota(I32, (c2, c2), 0) % c
    s_i = lax.broadcasted_iota(I32, (c2, c2), 1) % c
    masks = {sg: ((t_i - s_i) * sg > 0, (t_i - s_i) * sg >= 0) for sg in {s["sign"] for s in streams}}

    lhs = [jnp.concatenate([_stack_heads(s["kkp"], m0), _stack_heads(s["rp"], m0)], axis=0) for s in streams]
    rhs = [jnp.concatenate([_stack_heads(s["ki"], m0), _stack_heads(s["bi"], m0)], axis=0) for s in streams]
    vs = [_stack_heads(s["v"], m0) for s in streams]
    l4 = [_MM_L4(a, b) for a, b in zip(lhs, rhs)]
    m_kk, n1, m_rk, m_rb = [], [], [], []
    for s, m in zip(streams, l4):
        strict, incl = masks[s["sign"]]
        m_kk.append(jnp.where(strict, m[0:c2, 0:c2], 0.0))
        n1.append(jnp.where(strict, m[0:c2, c2:2 * c2], 0.0))
        m_rk.append(jnp.where(incl, m[c2:2 * c2, 0:c2], 0.0))
        m_rb.append(jnp.where(incl, m[c2:2 * c2, c2:2 * c2], 0.0))
    powers = [n1]
    span = 1
    while span * 2 < c:
        powers.append([_MM_SQ(q, q) for q in powers[-1]])
        span *= 2
    kt = [_MM_KT(a, s["st"]) for a, s in zip(lhs, streams)]
    x = [k[0:c2] + _MM_V(m, w) for k, m, w in zip(kt, m_kk, vs)]
    for level in reversed(powers[1:]):
        x = [xi + _MM_AP(q, xi) for q, xi in zip(level, x)]
    u = [xi - _MM_AP(q, xi) for q, xi in zip(n1, x)]
    ys = [k[c2:2 * c2] + _MM_Y(a, w) - _MM_Y(b, ui) for k, a, w, b, ui in zip(kt, m_rk, vs, m_rb, u)]
    out = []
    for s, w, ui, yi in zip(streams, vs, u, ys):
        upd_l = jnp.transpose(jnp.concatenate([w, -ui], axis=0))
        upd_r = jnp.concatenate([_stack_heads(s["kipc"], m0), _stack_heads(s["bipc"], m0)], axis=0)
        out.append((yi[0:c] + yi[c:c2], s["st"] * s["pc"] + _MM_UPD(upd_l, upd_r)))
    return out


def _softplus(z):
    return jnp.maximum(z, 0.0) + jnp.log(1.0 + jnp.exp(-jnp.abs(z)))


def _rwkv_tile_prep(x_ref, xp_ref, xn_ref, chunk, n_chunks, seq_len, width, sign, mu, w0, a0, wa_hi, wa_lo,
                    k_k, k_a, r_k, ones_h):
    c = RWKV_CHUNK
    valid = jnp.minimum(c, seq_len - chunk * c)
    n_cols = x_ref.shape[1]
    row = lax.broadcasted_iota(I32, (c, n_cols), 0)
    x = jnp.where(row < valid, x_ref[...], 0.0)
    prev_row = jnp.where(chunk > 0, xp_ref[RWKV_HALO - 1:RWKV_HALO, :], 0.0)
    next_row = jnp.where(chunk < n_chunks - 1, xn_ref[0:1, :], 0.0)
    x_prev = jnp.where(row == 0, prev_row, pltpu.roll(x, 1, 0))
    x_next = jnp.where(row == c - 1, next_row, pltpu.roll(x, c - 1, 0))
    xs = x + mu * (0.5 * (x_prev + x_next) - x)
    xs = jnp.where(row < valid, xs, 0.0)

    r = xs[:, 0:width]
    k = xs[:, width:2 * width]
    v = xs[:, 2 * width:3 * width]
    wa = xs[:, 3 * width:3 * width + LANES]
    g_lo = xs[:, 3 * width + LANES:3 * width + 2 * LANES]

    lane_wa = lax.broadcasted_iota(I32, (c, LANES), 1)
    xwa = jnp.where(lane_wa < LANES // 2, jnp.tanh(wa), wa)
    xh, xl = _split_bf16(xwa)
    la = _dot(xh, wa_hi) + _dot(xh, wa_lo) + _dot(xl, wa_hi)
    w_log = -_softplus(-(w0 + la[:, 0:width])) - 0.5
    rowv = lax.broadcasted_iota(I32, (c, width), 0) < valid
    logw = jnp.where(rowv, -jnp.exp(w_log), 0.0)
    a = jax.nn.sigmoid(a0 + la[:, width:2 * width])

    kk0 = k * k_k
    ss = _exact_right(kk0 * kk0, ones_h)
    kk = kk0 / jnp.maximum(jnp.sqrt(ss), 1e-12)
    kdir = k * (1.0 + (a - 1.0) * k_a)
    b = kk * a

    t_i = lax.broadcasted_iota(I32, (c, c), 0)
    s_i = lax.broadcasted_iota(I32, (c, c), 1)
    tri = ((t_i - s_i) * sign >= 0).astype(BF16)
    cl = _exact_left(tri, logw)
    last = cl[c - 1:c, :] if sign > 0 else cl[0:1, :]
    e_n = jnp.exp(-cl)
    pcr = jnp.exp(last - cl)
    return dict(kkp=kk * jnp.exp(cl - logw), rp=r * jnp.exp(cl), ki=kdir * e_n, bi=b * e_n, kipc=kdir * pcr,
                bipc=b * pcr, v=v, pc=jnp.exp(last), g_lo=g_lo,
                bonus=_exact_right(r * kdir * r_k, ones_h) * v)


def _rwkv_scan_kernel(xf_ref, xfp_ref, xfn_ref, xb_ref, xbp_ref, xbn_ref, mu_ref, w0_ref, a0_ref, wah_ref, wal_ref,
                      gup_ref, kk_ref, ka_ref, rk_ref, yf_ref, yb_ref, bonf_ref, bonb_ref, g_ref, st_ref,
                      *, seq_len, width):
    i = pl.program_id(1)
    n_chunks = pl.num_programs(1)
    n_pairs = width // LANES

    @pl.when(i == 0)
    def _():
        st_ref[...] = jnp.zeros_like(st_ref)

    ones_h = _head_block_ones(width)
    common = (mu_ref[...],)
    tail = (kk_ref[...], ka_ref[...], rk_ref[...], ones_h)
    fwd = _rwkv_tile_prep(xf_ref, xfp_ref, xfn_ref, i, n_chunks, seq_len, width, 1, *common,
                          w0_ref[0], a0_ref[0], wah_ref[0], wal_ref[0], *tail)
    bwd = _rwkv_tile_prep(xb_ref, xbp_ref, xbn_ref, n_chunks - 1 - i, n_chunks, seq_len, width, -1, *common,
                          w0_ref[1], a0_ref[1], wah_ref[1], wal_ref[1], *tail)
    bonf_ref[...] = fwd["bonus"]
    bonb_ref[...] = bwd["bonus"]
    g_ref[...] = _mm1(jax.nn.sigmoid(fwd["g_lo"]), gup_ref[...]).astype(g_ref.dtype)

    names = ("kkp", "rp", "ki", "bi", "kipc", "bipc", "v", "pc")
    streams = []
    for di, (tile, sign) in enumerate(((fwd, 1), (bwd, -1))):
        for p in range(n_pairs):
            cols = slice(p * LANES, (p + 1) * LANES)
            s = {nm: tile[nm][:, cols] for nm in names}
            s["st"] = st_ref[di * n_pairs + p]
            s["sign"] = sign
            streams.append(s)
    res = _rwkv_chunk_streams(streams, RWKV_CHUNK)
    for j, (y, st_new) in enumerate(res):
        di, p = divmod(j, n_pairs)
        cols = slice(p * LANES, (p + 1) * LANES)
        (yf_ref if di == 0 else yb_ref)[:, cols] = y
        st_ref[j] = st_new


def _rwkv_finish_kernel(yf_ref, yb_ref, bonf_ref, bonb_ref, g_ref, lg_ref, lb_ref, o_ref):
    width = o_ref.shape[1]
    ones_h = _head_block_ones(width)
    y = yf_ref[...] + yb_ref[...]
    mean = _exact_right(y, ones_h) * (1.0 / HEAD_DIM)
    yc = y - mean
    var = _exact_right(yc * yc, ones_h) * (1.0 / HEAD_DIM)
    yn = yc * lax.rsqrt(var + RWKV_GN_EPS) * lg_ref[...] + lb_ref[...]
    o_ref[...] = ((yn + bonf_ref[...] + bonb_ref[...]) * g_ref[...].astype(F32)).astype(o_ref.dtype)


def rwkv_mix(rest, mu, w0, w_up, a0, a_up, g_up, k_k, k_a, r_k, lnx_g, lnx_b):
    bsz, l, n_cols = rest.shape
    width = w0.shape[1]
    rank = w_up.shape[1]
    assert n_cols == 3 * width + 2 * LANES and 2 * rank == LANES and l % RWKV_HALO == 0
    c = RWKV_CHUNK
    n_chunks = -(-l // c)
    per = c // RWKV_HALO
    n_halo = l // RWKV_HALO
    zeros = jnp.zeros((2, rank, width), F32)
    w_wa = jnp.concatenate([jnp.concatenate([w_up.astype(F32), zeros], axis=2),
                            jnp.concatenate([zeros, a_up.astype(F32)], axis=2)], axis=1)
    wa_hi, wa_lo = _split_bf16(w_wa)

    fwd_chunk = lambda i: i
    bwd_chunk = lambda i: n_chunks - 1 - i

    def tile_specs(chunk_of):
        return [
            pl.BlockSpec((None, c, n_cols), lambda b, i: (b, chunk_of(i), 0)),
            pl.BlockSpec((None, RWKV_HALO, n_cols), lambda b, i: (b, jnp.maximum(chunk_of(i) * per - 1, 0), 0)),
            pl.BlockSpec((None, RWKV_HALO, n_cols),
                         lambda b, i: (b, jnp.minimum((chunk_of(i) + 1) * per, n_halo - 1), 0)),
        ]

    row2 = lambda a: a.astype(F32).reshape(1, -1)
    whole = lambda *shape: pl.BlockSpec(shape, lambda b, i: (0,) * len(shape))
    out_spec = lambda chunk_of: pl.BlockSpec((None, c, width), lambda b, i: (b, chunk_of(i), 0))
    act = lambda dt: jax.ShapeDtypeStruct((bsz, l, width), dt)
    y_f, y_b, bon_f, bon_b, g = pl.pallas_call(
        functools.partial(_rwkv_scan_kernel, seq_len=l, width=width),
        grid=(bsz, n_chunks),
        in_specs=tile_specs(fwd_chunk) + tile_specs(bwd_chunk) + [
            whole(1, n_cols),
            whole(2, 1, width), whole(2, 1, width),
            whole(2, LANES, 2 * width), whole(2, LANES, 2 * width),
            whole(LANES, width),
            whole(1, width), whole(1, width), whole(1, width),
        ],
        out_specs=[out_spec(fwd_chunk), out_spec(bwd_chunk), out_spec(fwd_chunk), out_spec(bwd_chunk),
                   out_spec(fwd_chunk)],
        out_shape=[act(F32), act(F32), act(F32), act(F32), act(BF16)],
        scratch_shapes=[pltpu.VMEM((2 * (width // LANES), LANES, LANES), F32)],
        compiler_params=_cparams("parallel", "arbitrary"),
        name="rwkv_scan",
    )(rest, rest, rest, rest, rest, rest, row2(mu), w0.astype(F32).reshape(2, 1, width),
      a0.astype(F32).reshape(2, 1, width), wa_hi, wa_lo, g_up.astype(BF16), row2(k_k), row2(k_a), row2(r_k))

    n = bsz * l
    tm = _row_tile(n, 608)
    rows = lambda: pl.BlockSpec((tm, width), lambda j: (j, 0))
    flat = lambda a: a.reshape(n, width)
    return pl.pallas_call(
        _rwkv_finish_kernel,
        grid=(n // tm,),
        in_specs=[rows(), rows(), rows(), rows(), rows(),
                  pl.BlockSpec((1, width), lambda j: (0, 0)),
                  pl.BlockSpec((1, width), lambda j: (0, 0))],
        out_specs=rows(),
        out_shape=jax.ShapeDtypeStruct((n, width), BF16),
        compiler_params=_cparams("parallel"),
        name="rwkv_finish",
    )(flat(y_f), flat(y_b), flat(bon_f), flat(bon_b), flat(g), row2(lnx_g), row2(lnx_b)).reshape(bsz, l, width)


S5_CHUNK = 16


def _cpow(n, lr, li, step):
    mag = jnp.exp(n * (lr * step))
    ang = n * (li * step)
    return mag * jnp.cos(ang), mag * jnp.sin(ang)


def _s5_param_kernel(lamr_ref, lamc_ref, stepr_ref, stepc_ref, bt_ref, ct_ref,
                     kmat_ref, wst_ref, cexp_ref, alpha_ref):
    t_len = S5_CHUNK
    n_i = S5_GROUP_CH
    p2 = 2 * S5_STATE
    ti = t_len * n_i

    lr = lamr_ref[0:1, :]
    li = lamr_ref[1:2, :]
    step = jnp.exp(stepr_ref[...])
    ab_re, ab_im = _cpow(1.0, lr, li, step)
    den = lr * lr + li * li
    z_re = ((ab_re - 1.0) * lr + ab_im * li) / den
    z_im = (ab_im * lr - (ab_re - 1.0) * li) / den
    bt_re = bt_ref[0]
    bt_im = bt_ref[1]
    bb_re = z_re * bt_re - z_im * bt_im
    bb_im = z_re * bt_im + z_im * bt_re
    tau = (lax.broadcasted_iota(I32, (ti, p2), 0) // n_i).astype(F32)
    is_f = lax.broadcasted_iota(I32, (ti, p2), 1) < S5_STATE
    n_w = jnp.where(is_f, (t_len - 1.0) - tau, tau)
    pw_re, pw_im = _cpow(n_w, lr, li, step)
    w_re = pw_re * bb_re - pw_im * bb_im
    w_im = pw_re * bb_im + pw_im * bb_re
    wst_ref[:, 0:p2] = w_re.astype(wst_ref.dtype)
    wst_ref[:, p2:2 * p2] = w_im.astype(wst_ref.dtype)
    al_re, al_im = _cpow(float(t_len), lr, li, step)
    alpha_ref[0:1, :] = al_re
    alpha_ref[1:2, :] = al_im

    lr_c = lamc_ref[:, 0:1]
    li_c = lamc_ref[:, 1:2]
    step_c = jnp.exp(stepc_ref[...])
    t_l = (lax.broadcasted_iota(I32, (p2, ti), 1) // n_i).astype(F32)
    row_f = lax.broadcasted_iota(I32, (p2, ti), 0) < S5_STATE
    ct_re = ct_ref[0]
    ct_im = ct_ref[1]

    def c_times_pow(n):
        q_re, q_im = _cpow(n, lr_c, li_c, step_c)
        return ct_re * q_re - ct_im * q_im, ct_re * q_im + ct_im * q_re

    n_tap = jnp.where(row_f, t_l, jnp.where(t_l == 0.0, 0.0, t_len - t_l))
    ca_re, ca_im = c_times_pow(n_tap)
    lane_p = lax.broadcasted_iota(I32, (n_i, p2), 1)
    bbr = bb_re[0:n_i]
    bbi = bb_im[0:n_i]
    zero = jnp.zeros_like(bbr)
    strips = []
    for sel in (lane_p < S5_STATE, lane_p >= S5_STATE):
        strips.append(_mm3(jnp.where(sel, bbr, zero), ca_re) - _mm3(jnp.where(sel, bbi, zero), ca_im))
    strip_f, strip_b = strips
    t_k = lax.broadcasted_iota(I32, (n_i, ti), 1) // n_i
    for tt in range(t_len):
        sf = strip_f if tt == 0 else pltpu.roll(strip_f, tt * n_i, 1)
        sb = strip_b if tt == 0 else pltpu.roll(strip_b, tt * n_i, 1)
        blk = jnp.where(t_k >= tt, sf, 0.0) + jnp.where(t_k <= tt, sb, 0.0)
        kmat_ref[tt * n_i:(tt + 1) * n_i, :] = blk.astype(kmat_ref.dtype)

    n_out = jnp.where(row_f, t_l + 1.0, t_len - t_l)
    co_re, co_im = c_times_pow(n_out)
    cexp_ref[0:p2, :] = co_re.astype(cexp_ref.dtype)
    cexp_ref[p2:2 * p2, :] = (-co_im).astype(cexp_ref.dtype)


def _s5_main_kernel(u_ref, kmat_ref, wst_ref, cexp_ref, alpha_ref, y_ref, x_ref, sf_ref, sb_ref,
                    *, n_batch, n_chunks):
    p2 = 2 * S5_STATE
    u = u_ref[...].astype(BF16)
    x_ref[...] = _dot(u, wst_ref[...])
    a_re = alpha_ref[0:1, :]
    a_im = alpha_ref[1:2, :]
    lane = lax.broadcasted_iota(I32, (1, p2), 1)
    is_f = lane < S5_STATE

    def step(i, carry):
        new = []
        for b in range(n_batch):
            s_re, s_im = carry[b]
            row_f = b * n_chunks + i
            row_b = b * n_chunks + (n_chunks - 1 - i)
            xf = x_ref[pl.ds(row_f, 1), :]
            xb = x_ref[pl.ds(row_b, 1), :]
            s_cat = jnp.concatenate([s_re, s_im], axis=1)
            sf_ref[pl.ds(row_f, 1), :] = s_cat
            sb_ref[pl.ds(row_b, 1), :] = s_cat
            x_re = jnp.where(is_f, xf[:, 0:p2], xb[:, 0:p2])
            x_im = jnp.where(is_f, xf[:, p2:2 * p2], xb[:, p2:2 * p2])
            new.append((a_re * s_re - a_im * s_im + x_re, a_re * s_im + a_im * s_re + x_im))
        return tuple(new)

    zero = jnp.zeros((1, p2), F32)
    lax.fori_loop(0, n_chunks, step, tuple((zero, zero) for _ in range(n_batch)))
    lane2 = lax.broadcasted_iota(I32, sf_ref.shape, 1) % p2
    s_in = jnp.where(lane2 < S5_STATE, sf_ref[...], sb_ref[...])
    s_hi, s_lo = _split_bf16(s_in)
    y_ref[...] = _dot(u, kmat_ref[...]) + _dot(s_hi, cexp_ref[...]) + _dot(s_lo, cexp_ref[...])


S5_RELAYOUT_CHUNKS = 128


def _s5_group_major_kernel(h_ref, g_ref, u_ref, hn_ref, ut_ref):
    n_g, mt, ti = u_ref.shape
    t_len = S5_CHUNK
    n_i = ti // t_len
    n_lt = hn_ref.shape[0]
    g_lt = LANES // n_i
    hn = _rms(h_ref[...], g_ref[...])
    for j in range(n_lt):
        hn_ref[j] = hn[:, j * LANES:(j + 1) * LANES]
    for tau in range(t_len):
        for j in range(n_lt):
            xt = jnp.transpose(hn_ref[j, pl.ds(tau, mt, stride=t_len), :])
            ut_ref[j * g_lt:(j + 1) * g_lt, tau * n_i:(tau + 1) * n_i, :] = xt.reshape(g_lt, n_i, mt)
    for g in range(n_g):
        u_ref[g] = jnp.transpose(ut_ref[g]).astype(u_ref.dtype)


def _s5_token_major_kernel(y_ref, o_ref, zt_ref, z_ref):
    n_g, mt, ti = y_ref.shape
    t_len = S5_CHUNK
    n_i = ti // t_len
    n_lt = z_ref.shape[0]
    for g in range(n_g):
        yt = jnp.transpose(y_ref[g])
        zt_ref[:, g * n_i:(g + 1) * n_i, :] = yt.reshape(t_len, n_i, mt)
    for t in range(t_len):
        for j in range(n_lt):
            z_ref[j, pl.ds(t, mt, stride=t_len), :] = jnp.transpose(zt_ref[t, j * LANES:(j + 1) * LANES, :])
    for j in range(n_lt):
        o_ref[:, j * LANES:(j + 1) * LANES] = z_ref[j]


def _gelu_tanh(x):
    return 0.5 * x * (1.0 + jnp.tanh(math.sqrt(2.0 / math.pi) * (x + 0.044715 * (x * x * x))))


def _s5_glu_kernel(h_ref, y_ref, g_ref, d_ref, w_ref, o_ref):
    h = h_ref[...]
    dm = h.shape[1]
    y = y_ref[...] + d_ref[...] * _rms(h, g_ref[...])
    gl = _gelu_tanh(y).astype(BF16)
    a = _dot(gl, w_ref[:, 0:dm])
    b = _dot(gl, w_ref[:, dm:2 * dm])
    o_ref[...] = h + a * jax.nn.sigmoid(b)


def s5_mix(h3, gain, b_re, b_im, lam_re, lam_im, log_step, c_re, c_im, d_skip, w_glu):
    bsz, l, dm = h3.shape
    n_g, n_p, n_i = b_re.shape
    t_len = S5_CHUNK
    assert l % t_len == 0 and n_g * n_i == dm and n_p == S5_STATE and n_i == S5_GROUP_CH
    n_chunks = l // t_len
    m = bsz * n_chunks
    ti = t_len * n_i
    p2 = 2 * n_p
    n = bsz * l
    tm = _row_tile(n, 608)
    h2 = h3.reshape(n, dm)
    gain2 = gain.astype(F32).reshape(1, dm)

    mt = min(S5_RELAYOUT_CHUNKS, m)
    n_tiles = -(-m // mt)
    u = pl.pallas_call(
        _s5_group_major_kernel,
        grid=(n_tiles,),
        in_specs=[pl.BlockSpec((mt * t_len, dm), lambda i: (i, 0)), pl.BlockSpec((1, dm), lambda i: (0, 0))],
        out_specs=pl.BlockSpec((n_g, mt, ti), lambda i: (0, i, 0)),
        out_shape=jax.ShapeDtypeStruct((n_g, m, ti), BF16),
        scratch_shapes=[pltpu.VMEM((dm // LANES, mt * t_len, LANES), F32), pltpu.VMEM((n_g, ti, mt), F32)],
        compiler_params=_cparams("parallel"),
        name="s5_group_major",
    )(h2, gain2)

    f32 = lambda a: a.astype(F32)
    lam_r = jnp.stack([jnp.concatenate([f32(lam_re)[0], f32(lam_re)[1]], axis=-1),
                       jnp.concatenate([f32(lam_im)[0], f32(lam_im)[1]], axis=-1)], axis=1)
    lam_c = jnp.transpose(lam_r, (0, 2, 1))
    step_r = jnp.repeat(jnp.transpose(f32(log_step))[:, None, :], n_p, axis=2)
    step_c = jnp.transpose(step_r, (0, 2, 1))
    bt = jnp.stack([jnp.transpose(f32(b_re), (0, 2, 1)), jnp.transpose(f32(b_im), (0, 2, 1))], axis=1)
    bt = jnp.tile(bt, (1, 1, t_len, 2))
    ct = jnp.stack([f32(c_re), f32(c_im)], axis=0)
    ct = jnp.transpose(ct, (2, 0, 1, 4, 3)).reshape(n_g, 2, p2, n_i)
    ct = jnp.tile(ct, (1, 1, 1, t_len))

    gspec = lambda *shape: pl.BlockSpec((None,) + shape, lambda g: (g,) + (0,) * len(shape))
    kmat, wst, cexp, alpha = pl.pallas_call(
        _s5_param_kernel,
        grid=(n_g,),
        in_specs=[gspec(2, p2), gspec(p2, 2), gspec(1, p2), gspec(p2, 1), gspec(2, ti, p2), gspec(2, p2, ti)],
        out_specs=[gspec(ti, ti), gspec(ti, 2 * p2), gspec(2 * p2, ti), gspec(2, p2)],
        out_shape=[
            jax.ShapeDtypeStruct((n_g, ti, ti), BF16),
            jax.ShapeDtypeStruct((n_g, ti, 2 * p2), BF16),
            jax.ShapeDtypeStruct((n_g, 2 * p2, ti), BF16),
            jax.ShapeDtypeStruct((n_g, 2, p2), F32),
        ],
        compiler_params=_cparams("parallel"),
        name="s5_params",
    )(lam_r, lam_c, step_r, step_c, bt, ct)

    y = pl.pallas_call(
        functools.partial(_s5_main_kernel, n_batch=bsz, n_chunks=n_chunks),
        grid=(n_g,),
        in_specs=[gspec(m, ti), gspec(ti, ti), gspec(ti, 2 * p2), gspec(2 * p2, ti), gspec(2, p2)],
        out_specs=gspec(m, ti),
        out_shape=jax.ShapeDtypeStruct((n_g, m, ti), F32),
        scratch_shapes=[pltpu.VMEM((m, 2 * p2), F32), pltpu.VMEM((m, 2 * p2), F32), pltpu.VMEM((m, 2 * p2), F32)],
        compiler_params=_cparams("parallel"),
        name="s5_main",
    )(u, kmat, wst, cexp, alpha)
    y2 = pl.pallas_call(
        _s5_token_major_kernel,
        grid=(n_tiles,),
        in_specs=[pl.BlockSpec((n_g, mt, ti), lambda i: (0, i, 0))],
        out_specs=pl.BlockSpec((mt * t_len, dm), lambda i: (i, 0)),
        out_shape=jax.ShapeDtypeStruct((n, dm), F32),
        scratch_shapes=[pltpu.VMEM((t_len, dm, mt), F32), pltpu.VMEM((dm // LANES, mt * t_len, LANES), F32)],
        compiler_params=_cparams("parallel"),
        name="s5_token_major",
    )(y)

    out = pl.pallas_call(
        _s5_glu_kernel,
        grid=(n // tm,),
        in_specs=[
            pl.BlockSpec((tm, dm), lambda i: (i, 0)),
            pl.BlockSpec((tm, dm), lambda i: (i, 0)),
            pl.BlockSpec((1, dm), lambda i: (0, 0)),
            pl.BlockSpec((1, dm), lambda i: (0, 0)),
            pl.BlockSpec((dm, 2 * dm), lambda i: (0, 0)),
        ],
        out_specs=pl.BlockSpec((tm, dm), lambda i: (i, 0)),
        out_shape=jax.ShapeDtypeStruct((n, dm), F32),
        compiler_params=_cparams("parallel"),
        name="s5_glu",
    )(h2, y2, gain2, f32(d_skip).reshape(1, dm), w_glu.astype(BF16))
    return out.reshape(bsz, l, dm)


def na_rwkv_mix(h3, gain, w_in, w_out, rpb, mu, w0, w_up, a0, a_up, g_up, k_k, k_a, r_k, lnx_g, lnx_b):
    bsz, l, dm = h3.shape
    n = bsz * l
    h2 = h3.reshape(n, dm)
    n_qkv = 3 * (w_out.shape[0] // 2)
    qkv, rest = norm_inproj(h2, gain.astype(F32), w_in.astype(BF16), n_qkv)
    na = na_attention(qkv.reshape(bsz, l, n_qkv), rpb)
    rw = rwkv_mix(rest.reshape(bsz, l, -1), mu, w0, w_up, a0, a_up, g_up, k_k, k_a, r_k, lnx_g, lnx_b)
    out = outproj_residual(h2, na.reshape(n, -1), rw.reshape(n, -1), w_out.astype(BF16))
    return out.reshape(bsz, l, dm)


def kernel(x, meta_tokens, norm_mix, norm_ffn, norm_final, mix_w_in, mix_w_out, na_rpb, rwkv_mu,
           rwkv_w0, rwkv_w_up, rwkv_a0, rwkv_a_up, rwkv_g_up, rwkv_k_k, rwkv_k_a, rwkv_r_k,
           rwkv_lnx_g, rwkv_lnx_b, s5_b_re, s5_b_im, s5_lambda_re, s5_lambda_im, s5_log_step,
           s5_c_re, s5_c_im, s5_d, s5_w_glu, moe_w_group, moe_b_group, moe_w_expert, moe_b_expert,
           moe_w1, moe_w3, moe_w2):
    bsz, _, dm = x.shape
    depth = norm_mix.shape[0]
    meta = jnp.broadcast_to(meta_tokens.astype(x.dtype)[None], (bsz,) + meta_tokens.shape)
    h = jnp.concatenate([meta, x], axis=1)
    l = h.shape[1]
    for layer in range(depth):
        i = layer // 2
        if layer % 2 == 0:
            h = na_rwkv_mix(h, norm_mix[layer], mix_w_in[i], mix_w_out[i], na_rpb[i], rwkv_mu[i], rwkv_w0[i],
                            rwkv_w_up[i], rwkv_a0[i], rwkv_a_up[i], rwkv_g_up[i], rwkv_k_k[i], rwkv_k_a[i],
                            rwkv_r_k[i], rwkv_lnx_g[i], rwkv_lnx_b[i])
        else:
            h = s5_mix(h, norm_mix[layer], s5_b_re[i], s5_b_im[i], s5_lambda_re[i], s5_lambda_im[i],
                       s5_log_step[i], s5_c_re[i], s5_c_im[i], s5_d[i], s5_w_glu[i])
        h = hierarchical_moe_residual(h.reshape(bsz * l, dm), norm_ffn[layer].astype(F32), moe_w_group[layer],
                                      moe_b_group[layer], moe_w_expert[layer], moe_b_expert[layer],
                                      moe_w1, moe_w3, moe_w2, layer).reshape(bsz, l, dm)
    return final_norm(h, norm_final.astype(F32))
```

```python
import functools
import math

import jax
import jax.numpy as jnp
from jax import lax
from jax.experimental import pallas as pl
from jax.experimental.pallas import tpu as pltpu

F32 = jnp.float32
BF16 = jnp.bfloat16
I32 = jnp.int32

N_META = 16
GRID_W = 64
HEAD_DIM = 64
NA_WIN_ROWS = 8
NA_WIN_COLS = 16
S5_GROUP_CH = 16
S5_STATE = 64
MOE_GROUPS = 4
MOE_PER_GROUP = 8
MOE_EXPERTS = MOE_GROUPS * MOE_PER_GROUP
NORM_EPS = 1e-6
RWKV_GN_EPS = 64e-5
NEG_INF = -1e30

LANES = 128
SUBLANES_BF16 = 16
VMEM_LIMIT_BYTES = 56 * 1024 * 1024

MOE_TILE = 256
ROUTER_LANES = 128


def _cparams(*sem):
    return pltpu.CompilerParams(dimension_semantics=sem, vmem_limit_bytes=VMEM_LIMIT_BYTES)


def _row_tile(n, target):
    best = None
    for t in range(SUBLANES_BF16, min(n, target) + 1, SUBLANES_BF16):
        if n % t == 0:
            best = t
    assert best is not None, (n, target)
    return best


def _rms(x, gain):
    ms = jnp.mean(x * x, axis=-1, keepdims=True)
    return (x * lax.rsqrt(ms + NORM_EPS)) * gain


def _split_bf16(x):
    hi = x.astype(BF16)
    lo = (x - hi.astype(F32)).astype(BF16)
    return hi, lo


def _dot(a, b):
    return jnp.dot(a, b, preferred_element_type=F32)


def _dot_nt(a, b):
    return lax.dot_general(a, b, (((1,), (1,)), ((), ())), preferred_element_type=F32)


def _norm_inproj_kernel(h_ref, g_ref, w_ref, qkv_ref, rest_ref, xn_ref, *, n_qkv, chunk):
    xn_ref[...] = _rms(h_ref[...], g_ref[...]).astype(BF16)
    n_all = w_ref.shape[1]
    for c in range(0, n_all, chunk):
        y = _dot(xn_ref[...], w_ref[:, c:c + chunk])
        if c < n_qkv:
            qkv_ref[:, c:c + chunk] = y.astype(BF16)
        else:
            rest_ref[:, c - n_qkv:c - n_qkv + chunk] = y


def norm_inproj(h2, gain, w_bf16, n_qkv):
    n, d = h2.shape
    n_all = w_bf16.shape[1]
    tm = _row_tile(n, 608)
    chunk = 256
    assert n_qkv % chunk == 0 and n_all % chunk == 0
    return pl.pallas_call(
        functools.partial(_norm_inproj_kernel, n_qkv=n_qkv, chunk=chunk),
        grid=(n // tm,),
        in_specs=[
            pl.BlockSpec((tm, d), lambda i: (i, 0)),
            pl.BlockSpec((1, d), lambda i: (0, 0)),
            pl.BlockSpec((d, n_all), lambda i: (0, 0)),
        ],
        out_specs=[
            pl.BlockSpec((tm, n_qkv), lambda i: (i, 0)),
            pl.BlockSpec((tm, n_all - n_qkv), lambda i: (i, 0)),
        ],
        out_shape=[
            jax.ShapeDtypeStruct((n, n_qkv), BF16),
            jax.ShapeDtypeStruct((n, n_all - n_qkv), F32),
        ],
        scratch_shapes=[pltpu.VMEM((tm, d), BF16)],
        compiler_params=_cparams("parallel"),
        name="norm_inproj",
    )(h2, gain.reshape(1, d), w_bf16)


def _outproj_kernel(h_ref, na_ref, rw_ref, wa_ref, wb_ref, o_ref):
    acc = _dot(na_ref[...], wa_ref[...])
    acc = acc + _dot(rw_ref[...], wb_ref[...])
    o_ref[...] = h_ref[...] + acc


def outproj_residual(h2, na, rw, w_out_bf16):
    n, d = h2.shape
    ka, kb = na.shape[1], rw.shape[1]
    tm = _row_tile(n, 608)
    return pl.pallas_call(
        _outproj_kernel,
        grid=(n // tm,),
        in_specs=[
            pl.BlockSpec((tm, d), lambda i: (i, 0)),
            pl.BlockSpec((tm, ka), lambda i: (i, 0)),
            pl.BlockSpec((tm, kb), lambda i: (i, 0)),
            pl.BlockSpec((ka, d), lambda i: (0, 0)),
            pl.BlockSpec((kb, d), lambda i: (0, 0)),
        ],
        out_specs=pl.BlockSpec((tm, d), lambda i: (i, 0)),
        out_shape=jax.ShapeDtypeStruct((n, d), F32),
        compiler_params=_cparams("parallel"),
        name="outproj_residual",
    )(h2, na, rw, w_out_bf16[:ka], w_out_bf16[ka:])


def _store_token_tiles(ref, x):
    rows = x.shape[0]
    s_n = x.shape[1] // LANES
    for s in range(s_n):
        ref[pl.ds(s, rows, stride=s_n), :] = x[:, s * LANES:(s + 1) * LANES]


def _load_token_tile_cols(ref, s, rows, s_n):
    return ref[pl.ds(s, rows, stride=s_n), :]


def _router_kernel(h_ref, g_ref, whi_ref, wlo_ref, b_ref, xn_ref, route_ref):
    xn = _rms(h_ref[...], g_ref[...])
    x_hi, x_lo = _split_bf16(xn)
    _store_token_tiles(xn_ref, xn)
    logits = (_dot(x_hi, whi_ref[...]) + _dot(x_hi, wlo_ref[...]) + _dot(x_lo, whi_ref[...])
              + b_ref[...])
    tm = logits.shape[0]
    lane = lax.broadcasted_iota(I32, (tm, ROUTER_LANES), 1)
    big = jnp.int32(ROUTER_LANES)

    is_g = lane < MOE_GROUPS
    lg = jnp.where(is_g, logits, -jnp.inf)
    eg = jnp.where(is_g, jnp.exp(lg - jnp.max(lg, axis=-1, keepdims=True)), 0.0)
    pg = eg / jnp.sum(eg, axis=-1, keepdims=True)
    p_grp = jnp.max(pg, axis=-1, keepdims=True)
    grp = jnp.min(jnp.where(is_g & (pg == p_grp), lane, big), axis=-1, keepdims=True)

    lo_lane = MOE_GROUPS + MOE_PER_GROUP * grp
    is_e = (lane >= lo_lane) & (lane < lo_lane + MOE_PER_GROUP)
    le = jnp.where(is_e, logits, -jnp.inf)
    ee = jnp.where(is_e, jnp.exp(le - jnp.max(le, axis=-1, keepdims=True)), 0.0)
    pe = jnp.where(is_e, ee / jnp.sum(ee, axis=-1, keepdims=True), -1.0)
    p1 = jnp.max(pe, axis=-1, keepdims=True)
    i1 = jnp.min(jnp.where(pe == p1, lane, big), axis=-1, keepdims=True)
    pe2 = jnp.where(lane == i1, -1.0, pe)
    p2 = jnp.max(pe2, axis=-1, keepdims=True)
    i2 = jnp.min(jnp.where(pe2 == p2, lane, big), axis=-1, keepdims=True)
    denom = p1 + p2
    g1 = p_grp * p1 / denom
    g2 = p_grp * p2 / denom
    e1 = (i1 - MOE_GROUPS).astype(F32)
    e2 = (i2 - MOE_GROUPS).astype(F32)
    route_ref[...] = jnp.where(lane == 0, e1, jnp.where(lane == 1, e2, jnp.where(lane == 2, g1, g2)))


def moe_router(h2, gain, w_group, b_group, w_expert, b_expert):
    n, d = h2.shape
    n_r = MOE_GROUPS + MOE_EXPERTS
    w_r = jnp.concatenate([w_group, jnp.transpose(w_expert, (1, 0, 2)).reshape(d, MOE_EXPERTS)], axis=1)
    w_r = jnp.pad(w_r.astype(F32), ((0, 0), (0, ROUTER_LANES - n_r)))
    w_hi, w_lo = _split_bf16(w_r)
    b_r = jnp.pad(jnp.concatenate([b_group, b_expert.reshape(-1)]).astype(F32), (0, ROUTER_LANES - n_r))
    tm = _row_tile(n, 608)
    return pl.pallas_call(
        _router_kernel,
        grid=(n // tm,),
        in_specs=[
            pl.BlockSpec((tm, d), lambda i: (i, 0)),
            pl.BlockSpec((1, d), lambda i: (0, 0)),
            pl.BlockSpec((d, ROUTER_LANES), lambda i: (0, 0)),
            pl.BlockSpec((d, ROUTER_LANES), lambda i: (0, 0)),
            pl.BlockSpec((1, ROUTER_LANES), lambda i: (0, 0)),
        ],
        out_specs=[
            pl.BlockSpec((tm * (d // LANES), LANES), lambda i: (i, 0)),
            pl.BlockSpec((tm, ROUTER_LANES), lambda i: (i, 0)),
        ],
        out_shape=[
            jax.ShapeDtypeStruct((n * (d // LANES), LANES), F32),
            jax.ShapeDtypeStruct((n, ROUTER_LANES), F32),
        ],
        compiler_params=_cparams("parallel"),
        name="moe_router",
    )(h2, gain.reshape(1, d), w_hi, w_lo, b_r.reshape(1, ROUTER_LANES))


def _expert_kernel(blk_e_ref, n_used_ref, src_ref, src_next_ref, dst_ref, xn_hbm, w1_ref, w3_ref, w2_ref, y_hbm,
                   xg_ref, xb_ref, yv_ref, gsem, ssem, w1b_ref, w3b_ref, w2b_ref, *, trash_row):
    i = pl.program_id(0)
    n_steps = pl.num_programs(0)
    n_used = n_used_ref[0]
    used = i < n_used
    slot = i & 1
    tile, d = xb_ref.shape
    s_n = d // LANES

    def start_gather(ids_ref, dst_slot):
        for r in range(tile):
            src = pl.multiple_of(ids_ref[0, r], s_n)
            pltpu.make_async_copy(xn_hbm.at[pl.ds(src, s_n), :], xg_ref.at[dst_slot, pl.ds(r * s_n, s_n), :],
                                  gsem.at[dst_slot]).start()

    def wait_rows_in(s):
        pltpu.make_async_copy(xn_hbm.at[pl.ds(0, tile * s_n), :], xg_ref.at[s], gsem.at[s]).wait()

    def wait_rows_out(s):
        pltpu.make_async_copy(yv_ref.at[s], y_hbm.at[pl.ds(0, tile * s_n), :], ssem.at[s]).wait()

    @pl.when(used & (i == 0))
    def _():
        start_gather(src_ref, 0)

    @pl.when(i + 1 < n_used)
    def _():
        start_gather(src_next_ref, 1 - slot)

    prev_e = blk_e_ref[jnp.maximum(i - 1, 0)]
    fresh = (i == 0) | (blk_e_ref[i] != prev_e)

    @pl.when(used & fresh)
    def _():
        w1b_ref[...] = w1_ref[...].astype(BF16)
        w3b_ref[...] = w3_ref[...].astype(BF16)
        w2b_ref[...] = w2_ref[...].astype(BF16)

    @pl.when(used & (i >= 2))
    def _():
        wait_rows_out(slot)

    @pl.when(used)
    def _():
        wait_rows_in(slot)
        for s in range(s_n):
            xb_ref[:, s * LANES:(s + 1) * LANES] = _load_token_tile_cols(xg_ref.at[slot], s, tile, s_n).astype(BF16)
        x = xb_ref[...]
        a = _dot(x, w1b_ref[...])
        b = _dot(x, w3b_ref[...])
        hmid = (a * jax.nn.sigmoid(a) * b).astype(BF16)
        _store_token_tiles(yv_ref.at[slot], _dot(hmid, w2b_ref[...]))
        for r in range(tile):
            dst = pl.multiple_of(dst_ref[0, r], s_n)
            pltpu.make_async_copy(yv_ref.at[slot, pl.ds(r * s_n, s_n), :], y_hbm.at[pl.ds(dst, s_n), :],
                                  ssem.at[slot]).start()

    @pl.when(i == n_steps - 1)
    def _():
        @pl.when(n_used >= 1)
        def _():
            wait_rows_out((n_used - 1) & 1)

        @pl.when(n_used >= 2)
        def _():
            wait_rows_out((n_used - 2) & 1)

        yv_ref[0] = jnp.zeros(yv_ref.shape[1:], yv_ref.dtype)
        fills = [pltpu.make_async_copy(yv_ref.at[0], y_hbm.at[pl.ds((trash_row + s * tile) * s_n, tile * s_n), :],
                                       ssem.at[s]) for s in range(2)]
        for cp in fills:
            cp.start()
        for cp in fills:
            cp.wait()


def moe_experts(xn_tiles, src_of_row, dst_of_row, blk_e, n_used, w1, w3, w2, layer, n_out_rows):
    d, f = w1.shape[2], w1.shape[3]
    s_n = d // LANES
    tile = MOE_TILE
    n_blocks = src_of_row.shape[0] // tile
    src3 = src_of_row.reshape(n_blocks, 1, tile)
    dst3 = dst_of_row.reshape(n_blocks, 1, tile)

    def w_map(i, blk_e_ref, n_used_ref):
        return (layer, blk_e_ref[i], 0, 0)

    ids = lambda index: pl.BlockSpec((None, 1, tile), lambda i, be, nu: (index(i), 0, 0), memory_space=pltpu.SMEM)
    grid_spec = pltpu.PrefetchScalarGridSpec(
        num_scalar_prefetch=2,
        grid=(n_blocks,),
        in_specs=[
            ids(lambda i: i),
            ids(lambda i: jnp.minimum(i + 1, n_blocks - 1)),
            ids(lambda i: i),
            pl.BlockSpec(memory_space=pl.ANY),
            pl.BlockSpec((None, None, d, f), w_map),
            pl.BlockSpec((None, None, d, f), w_map),
            pl.BlockSpec((None, None, f, d), w_map),
        ],
        out_specs=pl.BlockSpec(memory_space=pl.ANY),
        scratch_shapes=[
            pltpu.VMEM((2, tile * s_n, LANES), F32),
            pltpu.VMEM((tile, d), BF16),
            pltpu.VMEM((2, tile * s_n, LANES), F32),
            pltpu.SemaphoreType.DMA((2,)),
            pltpu.SemaphoreType.DMA((2,)),
            pltpu.VMEM((d, f), BF16),
            pltpu.VMEM((d, f), BF16),
            pltpu.VMEM((f, d), BF16),
        ],
    )
    return pl.pallas_call(
        functools.partial(_expert_kernel, trash_row=n_out_rows),
        grid_spec=grid_spec,
        out_shape=jax.ShapeDtypeStruct(((n_out_rows + 2 * tile) * s_n, LANES), F32),
        compiler_params=_cparams("arbitrary"),
        name="moe_experts",
    )(blk_e, n_used, src3, src3, dst3, xn_tiles, w1, w3, w2)


def _moe_combine_kernel(h_ref, route_ref, y0_ref, y1_ref, o_ref):
    tm, d = h_ref.shape
    s_n = d // LANES
    route = route_ref[...]
    g1 = route[:, 2:3]
    g2 = route[:, 3:4]
    for s in range(s_n):
        cols = slice(s * LANES, (s + 1) * LANES)
        o_ref[:, cols] = (h_ref[:, cols] + g1 * _load_token_tile_cols(y0_ref, s, tm, s_n)
                          + g2 * _load_token_tile_cols(y1_ref, s, tm, s_n))


def moe_combine(h2, route, y_tiles):
    n, d = h2.shape
    s_n = d // LANES
    tm = _row_tile(n, 608)
    nb = n // tm
    return pl.pallas_call(
        _moe_combine_kernel,
        grid=(nb,),
        in_specs=[
            pl.BlockSpec((tm, d), lambda i: (i, 0)),
            pl.BlockSpec((tm, ROUTER_LANES), lambda i: (i, 0)),
            pl.BlockSpec((tm * s_n, LANES), lambda i: (i, 0)),
            pl.BlockSpec((tm * s_n, LANES), lambda i: (i + nb, 0)),
        ],
        out_specs=pl.BlockSpec((tm, d), lambda i: (i, 0)),
        out_shape=jax.ShapeDtypeStruct((n, d), F32),
        compiler_params=_cparams("parallel"),
        name="moe_combine",
    )(h2, route, y_tiles, y_tiles)


def hierarchical_moe_residual(h2, gain, w_group, b_group, w_expert, b_expert, w1, w3, w2, layer):
    n, d = h2.shape
    xn, route = moe_router(h2, gain, w_group, b_group, w_expert, b_expert)
    e_km = jnp.concatenate([route[:, 0], route[:, 1]]).astype(I32)
    n_assign = 2 * n
    onehot = (e_km[:, None] == jnp.arange(MOE_EXPERTS, dtype=I32)[None, :]).astype(I32)
    csum = jnp.cumsum(onehot, axis=0)
    counts = csum[-1]
    padded = (counts + MOE_TILE - 1) // MOE_TILE * MOE_TILE
    pad_end = jnp.cumsum(padded)
    pad_start = pad_end - padded
    dest = jnp.sum((csum - onehot + pad_start[None, :]) * onehot, axis=1)
    n_blocks = -(-n_assign // MOE_TILE) + MOE_EXPERTS
    n_rows = n_blocks * MOE_TILE
    blk_start = jnp.arange(n_blocks, dtype=I32) * MOE_TILE
    blk_e = jnp.minimum(jnp.sum((pad_end[None, :] <= blk_start[:, None]).astype(I32), axis=1),
                        MOE_EXPERTS - 1).astype(I32)
    n_used = (pad_end[-1] // MOE_TILE).astype(I32).reshape(1)
    asg_of_row = jnp.full((n_rows,), n_assign, I32).at[dest].set(jnp.arange(n_assign, dtype=I32))
    s_n = d // LANES
    is_pad = asg_of_row >= n_assign
    row_id = jnp.arange(n_rows, dtype=I32)
    src_of_row = jnp.where(is_pad, n - 1, jnp.where(asg_of_row >= n, asg_of_row - n, asg_of_row)) * s_n
    dst_of_row = jnp.where(is_pad, n_assign + row_id % (2 * MOE_TILE), asg_of_row) * s_n
    y = moe_experts(xn, src_of_row, dst_of_row, blk_e, n_used, w1, w3, w2, layer, n_assign)
    return moe_combine(h2, route, y)


def _final_norm_kernel(h_ref, g_ref, o_ref):
    o_ref[...] = _rms(h_ref[...], g_ref[...])


def final_norm(h3, gain):
    b, l, d = h3.shape
    t = l - N_META
    tm = _row_tile(t, 512)
    return pl.pallas_call(
        _final_norm_kernel,
        grid=(b, t // tm),
        in_specs=[
            pl.BlockSpec((None, pl.Element(tm), pl.Element(d)),
                         lambda bi, i: (bi, pl.multiple_of(N_META + i * tm, SUBLANES_BF16), 0)),
            pl.BlockSpec((1, d), lambda bi, i: (0, 0)),
        ],
        out_specs=pl.BlockSpec((None, tm, d), lambda bi, i: (bi, i, 0)),
        out_shape=jax.ShapeDtypeStruct((b, t, d), F32),
        compiler_params=_cparams("parallel", "parallel"),
        name="final_norm",
    )(h3, gain.reshape(1, d))


NA_QROWS = 8
NA_KROWS = 3 * NA_QROWS
NA_ROWS_PER_ITER = 2


def _na_kernel(q_ref, kw_ref, vw_ref, qm_ref, km_ref, vm_ref, bias_ref, o_ref, om_ref, *, rows, scale):
    blk = pl.program_id(1)
    tq = GRID_W
    n_pairs = q_ref.shape[1] // LANES
    base = jnp.clip(NA_QROWS * blk - NA_QROWS, 0, rows - NA_KROWS)
    lane = lax.broadcasted_iota(I32, (tq, LANES), 1)
    halves = [lane < HEAD_DIM, lane >= HEAD_DIM]

    pad = jnp.zeros((LANES - N_META, LANES), km_ref.dtype)
    k_meta = [jnp.concatenate([km_ref[:, p * LANES:(p + 1) * LANES], pad], axis=0) for p in range(n_pairs)]
    v_meta = [jnp.concatenate([vm_ref[:, p * LANES:(p + 1) * LANES], pad], axis=0) for p in range(n_pairs)]
    meta_bias = jnp.where(lane < N_META, 0.0, NEG_INF)

    n_win = NA_WIN_ROWS * GRID_W

    def row_body(jb, carry):
        colsl = [slice(p * LANES, (p + 1) * LANES) for p in range(n_pairs)]
        units = [(jj, p, hh) for jj in range(NA_ROWS_PER_ITER) for p in range(n_pairs) for hh in range(2)]
        s_idx, koff, qoff = [], [], []
        for jj in range(NA_ROWS_PER_ITER):
            j = jb * NA_ROWS_PER_ITER + jj
            r = NA_QROWS * blk + j
            start = jnp.clip(r - NA_WIN_ROWS // 2, 0, rows - NA_WIN_ROWS)
            s_idx.append(start - r + (NA_WIN_ROWS - 1))
            koff.append(pl.multiple_of((start - base) * GRID_W, GRID_W))
            qoff.append(pl.multiple_of(j * tq, tq))
        q_pair = {(jj, p): q_ref[pl.ds(qoff[jj], tq), c]
                  for jj in range(NA_ROWS_PER_ITER) for p, c in enumerate(colsl)}
        k_ext = {(jj, p): jnp.concatenate([kw_ref[pl.ds(koff[jj], n_win), c], k_meta[p]], axis=0)
                 for jj in range(NA_ROWS_PER_ITER) for p, c in enumerate(colsl)}
        v_ext = {(jj, p): jnp.concatenate([vw_ref[pl.ds(koff[jj], n_win), c], v_meta[p]], axis=0)
                 for jj in range(NA_ROWS_PER_ITER) for p, c in enumerate(colsl)}
        qh = [jnp.where(halves[hh], q_pair[jj, p], jnp.zeros_like(q_pair[jj, p])) for jj, p, hh in units]
        s = [_dot_nt(qh[u], k_ext[jj, p]) * scale
             + jnp.concatenate([bias_ref[2 * p + hh, s_idx[jj]], meta_bias], axis=1)
             for u, (jj, p, hh) in enumerate(units)]
        m = [jnp.max(x, axis=-1, keepdims=True) for x in s]
        e = [jnp.exp(x - mx) for x, mx in zip(s, m)]
        den = [jnp.sum(x, axis=-1, keepdims=True) for x in e]
        o = [_dot(e[u].astype(BF16), v_ext[jj, p]) / den[u] for u, (jj, p, hh) in enumerate(units)]
        for u in range(0, len(units), 2):
            jj, p, _ = units[u]
            o_ref[pl.ds(qoff[jj], tq), colsl[p]] = jnp.where(halves[0], o[u], o[u + 1]).astype(o_ref.dtype)
        return carry

    lax.fori_loop(0, NA_QROWS // NA_ROWS_PER_ITER, row_body, 0)

    @pl.when(blk == 0)
    def _():
        lane_m = lax.broadcasted_iota(I32, (N_META, LANES), 1)
        for p in range(n_pairs):
            cols = slice(p * LANES, (p + 1) * LANES)
            q_pair = qm_ref[:, cols]
            kmp = km_ref[:, cols]
            vmp = vm_ref[:, cols]
            outs = []
            for hh in range(2):
                sel = (lane_m < HEAD_DIM) if hh == 0 else (lane_m >= HEAD_DIM)
                qp = jnp.where(sel, q_pair, jnp.zeros_like(q_pair))
                s_m = _dot_nt(qp, kmp) * scale
                p_m = jnp.exp(s_m - jnp.max(s_m, axis=-1, keepdims=True))
                den = jnp.sum(p_m, axis=-1, keepdims=True)
                outs.append(_dot(p_m.astype(BF16), vmp) / den)
            om_ref[:, cols] = jnp.where(lane_m < HEAD_DIM, outs[0], outs[1]).astype(om_ref.dtype)


def _na_bias_table(rpb):
    h = rpb.shape[0]
    c_ids = jnp.arange(GRID_W)
    c_start = jnp.clip(c_ids - NA_WIN_COLS // 2, 0, GRID_W - NA_WIN_COLS)
    in_band = (c_ids[None, :] >= c_start[:, None]) & (c_ids[None, :] < c_start[:, None] + NA_WIN_COLS)
    dc = jnp.clip(c_ids[None, :] - c_ids[:, None] + NA_WIN_COLS - 1, 0, 2 * NA_WIN_COLS - 2)
    tab = jnp.where(in_band[None, None], rpb.astype(F32)[:, :, dc], NEG_INF)
    win = jnp.stack([tab[:, s:s + NA_WIN_ROWS] for s in range(NA_WIN_ROWS)], axis=1)
    return jnp.transpose(win, (0, 1, 3, 2, 4)).reshape(h, NA_WIN_ROWS, GRID_W, NA_WIN_ROWS * GRID_W)


def na_attention(qkv, rpb):
    b, l, w3 = qkv.shape
    w = w3 // 3
    t = l - N_META
    rows = t // GRID_W
    assert rows * GRID_W == t and rows % NA_QROWS == 0 and rows >= NA_KROWS
    tq = NA_QROWS * GRID_W
    tk = NA_KROWS * GRID_W
    bias = _na_bias_table(rpb)
    al = SUBLANES_BF16

    def q_map(bi, i):
        return (bi, pl.multiple_of(N_META + i * tq, al), 0)

    def kv_map(col):
        def f(bi, i):
            base = jnp.clip(NA_QROWS * i - NA_QROWS, 0, rows - NA_KROWS)
            return (bi, pl.multiple_of(N_META + base * GRID_W, al), col)
        return f

    def meta_map(col):
        return lambda bi, i: (bi, 0, col)

    el = pl.Element
    grid_out, meta_out = pl.pallas_call(
        functools.partial(_na_kernel, rows=rows, scale=HEAD_DIM ** -0.5),
        grid=(b, rows // NA_QROWS),
        in_specs=[
            pl.BlockSpec((None, el(tq), el(w)), q_map),
            pl.BlockSpec((None, el(tk), el(w)), kv_map(w)),
            pl.BlockSpec((None, el(tk), el(w)), kv_map(2 * w)),
            pl.BlockSpec((None, el(N_META), el(w)), meta_map(0)),
            pl.BlockSpec((None, el(N_META), el(w)), meta_map(w)),
            pl.BlockSpec((None, el(N_META), el(w)), meta_map(2 * w)),
            pl.BlockSpec(bias.shape, lambda bi, i: (0, 0, 0, 0)),
        ],
        out_specs=[
            pl.BlockSpec((None, tq, w), lambda bi, i: (bi, i, 0)),
            pl.BlockSpec((None, N_META, w), lambda bi, i: (bi, 0, 0)),
        ],
        out_shape=[
            jax.ShapeDtypeStruct((b, t, w), BF16),
            jax.ShapeDtypeStruct((b, N_META, w), BF16),
        ],
        compiler_params=_cparams("parallel", "arbitrary"),
        name="na_attention",
    )(qkv, qkv, qkv, qkv, qkv, qkv, bias)
    return jnp.concatenate([meta_out, grid_out], axis=1)


RWKV_CHUNK = 64
RWKV_HALO = 8


def _split3_bf16(x):
    p1 = x.astype(BF16)
    r1 = x - p1.astype(F32)
    p2 = r1.astype(BF16)
    p3 = (r1 - p2.astype(F32)).astype(BF16)
    return p1, p2, p3


def _mm1(a, b):
    return _dot(a.astype(BF16), b.astype(BF16))


def _mm3(a, b):
    ah, al = _split_bf16(a)
    bh, bl = _split_bf16(b)
    return _dot(ah, bh) + _dot(ah, bl) + _dot(al, bh)


def _mm1_nt(a, b):
    return _dot_nt(a.astype(BF16), b.astype(BF16))


def _mm3_nt(a, b):
    ah, al = _split_bf16(a)
    bh, bl = _split_bf16(b)
    return _dot_nt(ah, bh) + _dot_nt(ah, bl) + _dot_nt(al, bh)


def _exact_left(mat_bf16, x):
    p1, p2, p3 = _split3_bf16(x)
    return _dot(mat_bf16, p1) + _dot(mat_bf16, p2) + _dot(mat_bf16, p3)


def _exact_right(x, mat_bf16):
    p1, p2, p3 = _split3_bf16(x)
    return _dot(p1, mat_bf16) + _dot(p2, mat_bf16) + _dot(p3, mat_bf16)


def _head_block_ones(width):
    ri = lax.broadcasted_iota(I32, (width, width), 0) // HEAD_DIM
    ci = lax.broadcasted_iota(I32, (width, width), 1) // HEAD_DIM
    return (ri == ci).astype(BF16)


def _head_sums(x, exact):
    ones_pair = _head_block_ones(LANES)
    tiles = []
    for p in range(x.shape[1] // LANES):
        xt = x[:, p * LANES:(p + 1) * LANES]
        tiles.append(_exact_right(xt, ones_pair) if exact else _dot(xt.astype(BF16), ones_pair))
    return jnp.concatenate(tiles, axis=1)


def _stack_heads(x, m0):
    z = jnp.zeros_like(x)
    return jnp.concatenate([jnp.where(m0, x, z), jnp.where(m0, z, x)], axis=0)


_MM_L4 = _mm1_nt
_MM_KT = _mm1_nt
_MM_SQ = _mm1
_MM_AP = _mm1
_MM_V = _mm1
_MM_Y = _mm1
_MM_UPD = _mm1


def _rwkv_chunk_streams(streams, c):
    c2 = 2 * c
    lane = lax.broadcasted_iota(I32, (c, LANES), 1)
    m0 = lane < HEAD_DIM
    t_i = lax.broadcasted_iota(I32, (c2, c2), 0) % c
    s_i = lax.broadcasted_iota(I32, (c2, c2), 1) % c
    masks = {sg: ((t_i - s_i) * sg > 0, (t_i - s_i) * sg >= 0) for sg in {s["sign"] for s in streams}}

    lhs = [jnp.concatenate([_stack_heads(s["kkp"], m0), _stack_heads(s["rp"], m0)], axis=0) for s in streams]
    rhs = [jnp.concatenate([_stack_heads(s["ki"], m0), _stack_heads(s["bi"], m0)], axis=0) for s in streams]
    vs = [_stack_heads(s["v"], m0) for s in streams]
    l4 = [_MM_L4(a, b) for a, b in zip(lhs, rhs)]
    m_kk, n1, m_rk, m_rb = [], [], [], []
    for s, m in zip(streams, l4):
        strict, incl = masks[s["sign"]]
        m_kk.append(jnp.where(strict, m[0:c2, 0:c2], 0.0))
        n1.append(jnp.where(strict, m[0:c2, c2:2 * c2], 0.0))
        m_rk.append(jnp.where(incl, m[c2:2 * c2, 0:c2], 0.0))
        m_rb.append(jnp.where(incl, m[c2:2 * c2, c2:2 * c2], 0.0))
    powers = [n1]
    span = 1
    while span * 2 < c:
        powers.append([_MM_SQ(q, q) for q in powers[-1]])
        span *= 2
    kt = [_MM_KT(a, s["st"]) for a, s in zip(lhs, streams)]
    x = [k[0:c2] + _MM_V(m, w) for k, m, w in zip(kt, m_kk, vs)]
    for level in reversed(powers[1:]):
        x = [xi + _MM_AP(q, xi) for q, xi in zip(level, x)]
    u = [xi - _MM_AP(q, xi) for q, xi in zip(n1, x)]
    ys = [k[c2:2 * c2] + _MM_Y(a, w) - _MM_Y(b, ui) for k, a, w, b, ui in zip(kt, m_rk, vs, m_rb, u)]
    out = []
    for s, w, ui, yi in zip(streams, vs, u, ys):
        upd_l = jnp.transpose(jnp.concatenate([w, -ui], axis=0))
        upd_r = jnp.concatenate([_stack_heads(s["kipc"], m0), _stack_heads(s["bipc"], m0)], axis=0)
        out.append((yi[0:c] + yi[c:c2], s["st"] * s["pc"] + _MM_UPD(upd_l, upd_r)))
    return out


def _softplus(z):
    return jnp.maximum(z, 0.0) + jnp.log(1.0 + jnp.exp(-jnp.abs(z)))


def _rwkv_tile_prep(x_ref, xp_ref, xn_ref, chunk, n_chunks, seq_len, width, sign, mu, w0, a0, w_wa,
                    k_k, k_a, r_k):
    c = RWKV_CHUNK
    valid = jnp.minimum(c, seq_len - chunk * c)
    n_cols = x_ref.shape[1]
    row = lax.broadcasted_iota(I32, (c, n_cols), 0)
    x = jnp.where(row < valid, x_ref[...], 0.0)
    prev_row = jnp.where(chunk > 0, xp_ref[RWKV_HALO - 1:RWKV_HALO, :], 0.0)
    next_row = jnp.where(chunk < n_chunks - 1, xn_ref[0:1, :], 0.0)
    x_prev = jnp.where(row == 0, prev_row, pltpu.roll(x, 1, 0))
    x_next = jnp.where(row == c - 1, next_row, pltpu.roll(x, c - 1, 0))
    xs = x + mu * (0.5 * (x_prev + x_next) - x)
    xs = jnp.where(row < valid, xs, 0.0)

    r = xs[:, 0:width]
    k = xs[:, width:2 * width]
    v = xs[:, 2 * width:3 * width]
    wa = xs[:, 3 * width:3 * width + LANES]
    g_lo = xs[:, 3 * width + LANES:3 * width + 2 * LANES]

    lane_wa = lax.broadcasted_iota(I32, (c, LANES), 1)
    xwa = jnp.where(lane_wa < LANES // 2, jnp.tanh(wa), wa)
    la = _dot(xwa.astype(BF16), w_wa)
    w_log = -_softplus(-(w0 + la[:, 0:width])) - 0.5
    rowv = lax.broadcasted_iota(I32, (c, width), 0) < valid
    logw = jnp.where(rowv, -jnp.exp(w_log), 0.0)
    a = jax.nn.sigmoid(a0 + la[:, width:2 * width])

    kk0 = k * k_k
    ss = _head_sums(kk0 * kk0, exact=False)
    kk = kk0 / jnp.maximum(jnp.sqrt(ss), 1e-12)
    kdir = k * (1.0 + (a - 1.0) * k_a)
    b = kk * a

    t_i = lax.broadcasted_iota(I32, (c, c), 0)
    s_i = lax.broadcasted_iota(I32, (c, c), 1)
    tri = ((t_i - s_i) * sign >= 0).astype(BF16)
    cl = _exact_left(tri, logw)
    last = cl[c - 1:c, :] if sign > 0 else cl[0:1, :]
    e_n = jnp.exp(-cl)
    pcr = jnp.exp(last - cl)
    return dict(kkp=kk * jnp.exp(cl - logw), rp=r * jnp.exp(cl), ki=kdir * e_n, bi=b * e_n, kipc=kdir * pcr,
                bipc=b * pcr, v=v, pc=jnp.exp(last), g_lo=g_lo,
                bonus=_head_sums(r * kdir * r_k, exact=False) * v)


def _rwkv_scan_kernel(xf_ref, xfp_ref, xfn_ref, xb_ref, xbp_ref, xbn_ref, mu_ref, w0_ref, a0_ref, wwa_ref,
                      gup_ref, kk_ref, ka_ref, rk_ref, yf_ref, yb_ref, bonf_ref, bonb_ref, g_ref, st_ref,
                      *, seq_len, width):
    i = pl.program_id(1)
    n_chunks = pl.num_programs(1)
    n_pairs = width // LANES

    @pl.when(i == 0)
    def _():
        st_ref[...] = jnp.zeros_like(st_ref)

    common = (mu_ref[...],)
    tail = (kk_ref[...], ka_ref[...], rk_ref[...])
    fwd = _rwkv_tile_prep(xf_ref, xfp_ref, xfn_ref, i, n_chunks, seq_len, width, 1, *common,
                          w0_ref[0], a0_ref[0], wwa_ref[0], *tail)
    bwd = _rwkv_tile_prep(xb_ref, xbp_ref, xbn_ref, n_chunks - 1 - i, n_chunks, seq_len, width, -1, *common,
                          w0_ref[1], a0_ref[1], wwa_ref[1], *tail)
    bonf_ref[...] = fwd["bonus"]
    bonb_ref[...] = bwd["bonus"]
    g_ref[...] = _mm1(jax.nn.sigmoid(fwd["g_lo"]), gup_ref[...]).astype(g_ref.dtype)

    names = ("kkp", "rp", "ki", "bi", "kipc", "bipc", "v", "pc")
    streams = []
    for di, (tile, sign) in enumerate(((fwd, 1), (bwd, -1))):
        for p in range(n_pairs):
            cols = slice(p * LANES, (p + 1) * LANES)
            s = {nm: tile[nm][:, cols] for nm in names}
            s["st"] = st_ref[di * n_pairs + p]
            s["sign"] = sign
            streams.append(s)
    res = _rwkv_chunk_streams(streams, RWKV_CHUNK)
    for j, (y, st_new) in enumerate(res):
        di, p = divmod(j, n_pairs)
        cols = slice(p * LANES, (p + 1) * LANES)
        (yf_ref if di == 0 else yb_ref)[:, cols] = y
        st_ref[j] = st_new


def _rwkv_finish_kernel(yf_ref, yb_ref, bonf_ref, bonb_ref, g_ref, lg_ref, lb_ref, o_ref):
    y = yf_ref[...] + yb_ref[...]
    mean = _head_sums(y, exact=True) * (1.0 / HEAD_DIM)
    yc = y - mean
    var = _head_sums(yc * yc, exact=True) * (1.0 / HEAD_DIM)
    yn = yc * lax.rsqrt(var + RWKV_GN_EPS) * lg_ref[...] + lb_ref[...]
    o_ref[...] = ((yn + bonf_ref[...] + bonb_ref[...]) * g_ref[...].astype(F32)).astype(o_ref.dtype)


def rwkv_mix(rest, mu, w0, w_up, a0, a_up, g_up, k_k, k_a, r_k, lnx_g, lnx_b):
    bsz, l, n_cols = rest.shape
    width = w0.shape[1]
    rank = w_up.shape[1]
    assert n_cols == 3 * width + 2 * LANES and 2 * rank == LANES and l % RWKV_HALO == 0
    c = RWKV_CHUNK
    n_chunks = -(-l // c)
    per = c // RWKV_HALO
    n_halo = l // RWKV_HALO
    zeros = jnp.zeros((2, rank, width), F32)
    w_wa = jnp.concatenate([jnp.concatenate([w_up.astype(F32), zeros], axis=2),
                            jnp.concatenate([zeros, a_up.astype(F32)], axis=2)], axis=1)
    w_wa = w_wa.astype(BF16)

    fwd_chunk = lambda i: i
    bwd_chunk = lambda i: n_chunks - 1 - i

    def tile_specs(chunk_of):
        return [
            pl.BlockSpec((None, c, n_cols), lambda b, i: (b, chunk_of(i), 0)),
            pl.BlockSpec((None, RWKV_HALO, n_cols), lambda b, i: (b, jnp.maximum(chunk_of(i) * per - 1, 0), 0)),
            pl.BlockSpec((None, RWKV_HALO, n_cols),
                         lambda b, i: (b, jnp.minimum((chunk_of(i) + 1) * per, n_halo - 1), 0)),
        ]

    row2 = lambda a: a.astype(F32).reshape(1, -1)
    whole = lambda *shape: pl.BlockSpec(shape, lambda b, i: (0,) * len(shape))
    out_spec = lambda chunk_of: pl.BlockSpec((None, c, width), lambda b, i: (b, chunk_of(i), 0))
    act = lambda dt: jax.ShapeDtypeStruct((bsz, l, width), dt)
    y_f, y_b, bon_f, bon_b, g = pl.pallas_call(
        functools.partial(_rwkv_scan_kernel, seq_len=l, width=width),
        grid=(bsz, n_chunks),
        in_specs=tile_specs(fwd_chunk) + tile_specs(bwd_chunk) + [
            whole(1, n_cols),
            whole(2, 1, width), whole(2, 1, width),
            whole(2, LANES, 2 * width),
            whole(LANES, width),
            whole(1, width), whole(1, width), whole(1, width),
        ],
        out_specs=[out_spec(fwd_chunk), out_spec(bwd_chunk), out_spec(fwd_chunk), out_spec(bwd_chunk),
                   out_spec(fwd_chunk)],
        out_shape=[act(F32), act(F32), act(F32), act(F32), act(BF16)],
        scratch_shapes=[pltpu.VMEM((2 * (width // LANES), LANES, LANES), F32)],
        compiler_params=_cparams("parallel", "arbitrary"),
        name="rwkv_scan",
    )(rest, rest, rest, rest, rest, rest, row2(mu), w0.astype(F32).reshape(2, 1, width),
      a0.astype(F32).reshape(2, 1, width), w_wa, g_up.astype(BF16), row2(k_k), row2(k_a), row2(r_k))

    n = bsz * l
    tm = _row_tile(n, 608)
    rows = lambda: pl.BlockSpec((tm, width), lambda j: (j, 0))
    flat = lambda a: a.reshape(n, width)
    return pl.pallas_call(
        _rwkv_finish_kernel,
        grid=(n // tm,),
        in_specs=[rows(), rows(), rows(), rows(), rows(),
                  pl.BlockSpec((1, width), lambda j: (0, 0)),
                  pl.BlockSpec((1, width), lambda j: (0, 0))],
        out_specs=rows(),
        out_shape=jax.ShapeDtypeStruct((n, width), BF16),
        compiler_params=_cparams("parallel"),
        name="rwkv_finish",
    )(flat(y_f), flat(y_b), flat(bon_f), flat(bon_b), flat(g), row2(lnx_g), row2(lnx_b)).reshape(bsz, l, width)


S5_CHUNK = 16


def _cpow(n, lr, li, step):
    mag = jnp.exp(n * (lr * step))
    ang = n * (li * step)
    return mag * jnp.cos(ang), mag * jnp.sin(ang)


def _s5_param_kernel(lamr_ref, lamc_ref, stepr_ref, stepc_ref, bt_ref, ct_ref,
                     kmat_ref, wst_ref, cexp_ref, alpha_ref):
    t_len = S5_CHUNK
    n_i = S5_GROUP_CH
    p2 = 2 * S5_STATE
    ti = t_len * n_i

    lr = lamr_ref[0:1, :]
    li = lamr_ref[1:2, :]
    step = jnp.exp(stepr_ref[...])
    ab_re, ab_im = _cpow(1.0, lr, li, step)
    den = lr * lr + li * li
    z_re = ((ab_re - 1.0) * lr + ab_im * li) / den
    z_im = (ab_im * lr - (ab_re - 1.0) * li) / den
    bt_re = bt_ref[0]
    bt_im = bt_ref[1]
    bb_re = z_re * bt_re - z_im * bt_im
    bb_im = z_re * bt_im + z_im * bt_re
    tau = (lax.broadcasted_iota(I32, (ti, p2), 0) // n_i).astype(F32)
    is_f = lax.broadcasted_iota(I32, (ti, p2), 1) < S5_STATE
    n_w = jnp.where(is_f, (t_len - 1.0) - tau, tau)
    pw_re, pw_im = _cpow(n_w, lr, li, step)
    w_re = pw_re * bb_re - pw_im * bb_im
    w_im = pw_re * bb_im + pw_im * bb_re
    wst_ref[:, 0:p2] = w_re.astype(wst_ref.dtype)
    wst_ref[:, p2:2 * p2] = w_im.astype(wst_ref.dtype)
    al_re, al_im = _cpow(float(t_len), lr, li, step)
    alpha_ref[0:1, :] = al_re
    alpha_ref[1:2, :] = al_im

    lr_c = lamc_ref[:, 0:1]
    li_c = lamc_ref[:, 1:2]
    step_c = jnp.exp(stepc_ref[...])
    t_l = (lax.broadcasted_iota(I32, (p2, ti), 1) // n_i).astype(F32)
    row_f = lax.broadcasted_iota(I32, (p2, ti), 0) < S5_STATE
    ct_re = ct_ref[0]
    ct_im = ct_ref[1]

    def c_times_pow(n):
        q_re, q_im = _cpow(n, lr_c, li_c, step_c)
        return ct_re * q_re - ct_im * q_im, ct_re * q_im + ct_im * q_re

    n_tap = jnp.where(row_f, t_l, jnp.where(t_l == 0.0, 0.0, t_len - t_l))
    ca_re, ca_im = c_times_pow(n_tap)
    lane_p = lax.broadcasted_iota(I32, (n_i, p2), 1)
    bbr = bb_re[0:n_i]
    bbi = bb_im[0:n_i]
    zero = jnp.zeros_like(bbr)
    strips = []
    for sel in (lane_p < S5_STATE, lane_p >= S5_STATE):
        strips.append(_mm3(jnp.where(sel, bbr, zero), ca_re) - _mm3(jnp.where(sel, bbi, zero), ca_im))
    strip_f, strip_b = strips
    t_k = lax.broadcasted_iota(I32, (n_i, ti), 1) // n_i
    for tt in range(t_len):
        sf = strip_f if tt == 0 else pltpu.roll(strip_f, tt * n_i, 1)
        sb = strip_b if tt == 0 else pltpu.roll(strip_b, tt * n_i, 1)
        blk = jnp.where(t_k >= tt, sf, 0.0) + jnp.where(t_k <= tt, sb, 0.0)
        kmat_ref[tt * n_i:(tt + 1) * n_i, :] = blk.astype(kmat_ref.dtype)

    n_out = jnp.where(row_f, t_l + 1.0, t_len - t_l)
    co_re, co_im = c_times_pow(n_out)
    cexp_ref[0:p2, :] = co_re.astype(cexp_ref.dtype)
    cexp_ref[p2:2 * p2, :] = (-co_im).astype(cexp_ref.dtype)


def _s5_main_kernel(u_ref, kmat_ref, wst_ref, cexp_ref, alpha_ref, y_ref, x_ref, sf_ref, sb_ref,
                    *, n_batch, n_chunks):
    p2 = 2 * S5_STATE
    u = u_ref[...].astype(BF16)
    x_ref[...] = _dot(u, wst_ref[...])
    a_re = alpha_ref[0:1, :]
    a_im = alpha_ref[1:2, :]
    lane = lax.broadcasted_iota(I32, (1, p2), 1)
    is_f = lane < S5_STATE

    sub = S5_SCAN_ROWS
    assert n_chunks % sub == 0

    def step(k, carry):
        new = []
        for b in range(n_batch):
            s_re, s_im = carry[b]
            row_f = pl.multiple_of(b * n_chunks + sub * k, sub)
            row_b = pl.multiple_of(b * n_chunks + (n_chunks - sub) - sub * k, sub)
            xf = x_ref[pl.ds(row_f, sub), :]
            xb = x_ref[pl.ds(row_b, sub), :]
            seen = []
            for r in range(sub):
                seen.append(jnp.concatenate([s_re, s_im], axis=1))
                rb = sub - 1 - r
                x_re = jnp.where(is_f, xf[r:r + 1, 0:p2], xb[rb:rb + 1, 0:p2])
                x_im = jnp.where(is_f, xf[r:r + 1, p2:2 * p2], xb[rb:rb + 1, p2:2 * p2])
                s_re, s_im = a_re * s_re - a_im * s_im + x_re, a_re * s_im + a_im * s_re + x_im
            sf_ref[pl.ds(row_f, sub), :] = jnp.concatenate(seen, axis=0)
            sb_ref[pl.ds(row_b, sub), :] = jnp.concatenate(seen[::-1], axis=0)
            new.append((s_re, s_im))
        return tuple(new)

    zero = jnp.zeros((1, p2), F32)
    lax.fori_loop(0, n_chunks // sub, step, tuple((zero, zero) for _ in range(n_batch)))
    lane2 = lax.broadcasted_iota(I32, sf_ref.shape, 1) % p2
    s_in = jnp.where(lane2 < S5_STATE, sf_ref[...], sb_ref[...])
    s_hi, s_lo = _split_bf16(s_in)
    y_ref[...] = _dot(u, kmat_ref[...]) + _dot(s_hi, cexp_ref[...]) + _dot(s_lo, cexp_ref[...])


S5_RELAYOUT_CHUNKS = 128
S5_SCAN_ROWS = 8


def _s5_group_major_kernel(h_ref, g_ref, u_ref, hn_ref, ut_ref, *, seq_len):
    n_g, mt, ti = u_ref.shape
    t_len = S5_CHUNK
    n_i = ti // t_len
    n_lt = hn_ref.shape[0]
    g_lt = LANES // n_i
    rows = h_ref.shape[0]
    valid = seq_len - pl.program_id(1) * rows
    row = lax.broadcasted_iota(I32, h_ref.shape, 0)
    hn = jnp.where(row < valid, _rms(h_ref[...], g_ref[...]), 0.0)
    for j in range(n_lt):
        hn_ref[j] = hn[:, j * LANES:(j + 1) * LANES]
    for tau in range(t_len):
        for j in range(n_lt):
            xt = jnp.transpose(hn_ref[j, pl.ds(tau, mt, stride=t_len), :])
            ut_ref[j * g_lt:(j + 1) * g_lt, tau * n_i:(tau + 1) * n_i, :] = xt.reshape(g_lt, n_i, mt)
    for g in range(n_g):
        u_ref[g] = jnp.transpose(ut_ref[g]).astype(u_ref.dtype)


def _s5_token_major_kernel(y_ref, o_ref, zt_ref, z_ref):
    n_g, mt, ti = y_ref.shape
    t_len = S5_CHUNK
    n_i = ti // t_len
    n_lt = z_ref.shape[0]
    for g in range(n_g):
        yt = jnp.transpose(y_ref[g])
        zt_ref[:, g * n_i:(g + 1) * n_i, :] = yt.reshape(t_len, n_i, mt)
    for t in range(t_len):
        for j in range(n_lt):
            z_ref[j, pl.ds(t, mt, stride=t_len), :] = jnp.transpose(zt_ref[t, j * LANES:(j + 1) * LANES, :])
    for j in range(n_lt):
        o_ref[:, j * LANES:(j + 1) * LANES] = z_ref[j]


def _gelu_tanh(x):
    return 0.5 * x * (1.0 + jnp.tanh(math.sqrt(2.0 / math.pi) * (x + 0.044715 * (x * x * x))))


def _s5_glu_kernel(h_ref, y_ref, g_ref, d_ref, w_ref, o_ref):
    h = h_ref[...]
    dm = h.shape[1]
    y = y_ref[...] + d_ref[...] * _rms(h, g_ref[...])
    gl = _gelu_tanh(y).astype(BF16)
    a = _dot(gl, w_ref[:, 0:dm])
    b = _dot(gl, w_ref[:, dm:2 * dm])
    o_ref[...] = h + a * jax.nn.sigmoid(b)


def s5_mix(h3, gain, b_re, b_im, lam_re, lam_im, log_step, c_re, c_im, d_skip, w_glu):
    bsz, l, dm = h3.shape
    n_g, n_p, n_i = b_re.shape
    t_len = S5_CHUNK
    assert l % t_len == 0 and n_g * n_i == dm and n_p == S5_STATE and n_i == S5_GROUP_CH
    n_chunks = -(-(l // t_len) // S5_SCAN_ROWS) * S5_SCAN_ROWS
    m = bsz * n_chunks
    ti = t_len * n_i
    p2 = 2 * n_p
    n = bsz * l
    tm = _row_tile(n, 608)
    h2 = h3.reshape(n, dm)
    gain2 = gain.astype(F32).reshape(1, dm)

    mt = min(S5_RELAYOUT_CHUNKS, n_chunks)
    n_tiles = -(-n_chunks // mt)
    u = pl.pallas_call(
        functools.partial(_s5_group_major_kernel, seq_len=l),
        grid=(bsz, n_tiles),
        in_specs=[pl.BlockSpec((None, mt * t_len, dm), lambda b, i: (b, i, 0)),
                  pl.BlockSpec((1, dm), lambda b, i: (0, 0))],
        out_specs=pl.BlockSpec((n_g, None, mt, ti), lambda b, i: (0, b, i, 0)),
        out_shape=jax.ShapeDtypeStruct((n_g, bsz, n_chunks, ti), BF16),
        scratch_shapes=[pltpu.VMEM((dm // LANES, mt * t_len, LANES), F32), pltpu.VMEM((n_g, ti, mt), F32)],
        compiler_params=_cparams("parallel", "parallel"),
        name="s5_group_major",
    )(h3, gain2).reshape(n_g, m, ti)

    f32 = lambda a: a.astype(F32)
    lam_r = jnp.stack([jnp.concatenate([f32(lam_re)[0], f32(lam_re)[1]], axis=-1),
                       jnp.concatenate([f32(lam_im)[0], f32(lam_im)[1]], axis=-1)], axis=1)
    lam_c = jnp.transpose(lam_r, (0, 2, 1))
    step_r = jnp.repeat(jnp.transpose(f32(log_step))[:, None, :], n_p, axis=2)
    step_c = jnp.transpose(step_r, (0, 2, 1))
    bt = jnp.stack([jnp.transpose(f32(b_re), (0, 2, 1)), jnp.transpose(f32(b_im), (0, 2, 1))], axis=1)
    bt = jnp.tile(bt, (1, 1, t_len, 2))
    ct = jnp.stack([f32(c_re), f32(c_im)], axis=0)
    ct = jnp.transpose(ct, (2, 0, 1, 4, 3)).reshape(n_g, 2, p2, n_i)
    ct = jnp.tile(ct, (1, 1, 1, t_len))

    gspec = lambda *shape: pl.BlockSpec((None,) + shape, lambda g: (g,) + (0,) * len(shape))
    kmat, wst, cexp, alpha = pl.pallas_call(
        _s5_param_kernel,
        grid=(n_g,),
        in_specs=[gspec(2, p2), gspec(p2, 2), gspec(1, p2), gspec(p2, 1), gspec(2, ti, p2), gspec(2, p2, ti)],
        out_specs=[gspec(ti, ti), gspec(ti, 2 * p2), gspec(2 * p2, ti), gspec(2, p2)],
        out_shape=[
            jax.ShapeDtypeStruct((n_g, ti, ti), BF16),
            jax.ShapeDtypeStruct((n_g, ti, 2 * p2), BF16),
            jax.ShapeDtypeStruct((n_g, 2 * p2, ti), BF16),
            jax.ShapeDtypeStruct((n_g, 2, p2), F32),
        ],
        compiler_params=_cparams("parallel"),
        name="s5_params",
    )(lam_r, lam_c, step_r, step_c, bt, ct)

    y = pl.pallas_call(
        functools.partial(_s5_main_kernel, n_batch=bsz, n_chunks=n_chunks),
        grid=(n_g,),
        in_specs=[gspec(m, ti), gspec(ti, ti), gspec(ti, 2 * p2), gspec(2 * p2, ti), gspec(2, p2)],
        out_specs=gspec(m, ti),
        out_shape=jax.ShapeDtypeStruct((n_g, m, ti), F32),
        scratch_shapes=[pltpu.VMEM((m, 2 * p2), F32), pltpu.VMEM((m, 2 * p2), F32), pltpu.VMEM((m, 2 * p2), F32)],
        compiler_params=_cparams("parallel"),
        name="s5_main",
    )(u, kmat, wst, cexp, alpha)
    y2 = pl.pallas_call(
        _s5_token_major_kernel,
        grid=(bsz, n_tiles),
        in_specs=[pl.BlockSpec((n_g, None, mt, ti), lambda b, i: (0, b, i, 0))],
        out_specs=pl.BlockSpec((None, mt * t_len, dm), lambda b, i: (b, i, 0)),
        out_shape=jax.ShapeDtypeStruct((bsz, l, dm), F32),
        scratch_shapes=[pltpu.VMEM((t_len, dm, mt), F32), pltpu.VMEM((dm // LANES, mt * t_len, LANES), F32)],
        compiler_params=_cparams("parallel", "parallel"),
        name="s5_token_major",
    )(y.reshape(n_g, bsz, n_chunks, ti)).reshape(n, dm)

    out = pl.pallas_call(
        _s5_glu_kernel,
        grid=(n // tm,),
        in_specs=[
            pl.BlockSpec((tm, dm), lambda i: (i, 0)),
            pl.BlockSpec((tm, dm), lambda i: (i, 0)),
            pl.BlockSpec((1, dm), lambda i: (0, 0)),
            pl.BlockSpec((1, dm), lambda i: (0, 0)),
            pl.BlockSpec((dm, 2 * dm), lambda i: (0, 0)),
        ],
        out_specs=pl.BlockSpec((tm, dm), lambda i: (i, 0)),
        out_shape=jax.ShapeDtypeStruct((n, dm), F32),
        compiler_params=_cparams("parallel"),
        name="s5_glu",
    )(h2, y2, gain2, f32(d_skip).reshape(1, dm), w_glu.astype(BF16))
    return out.reshape(bsz, l, dm)


def na_rwkv_mix(h3, gain, w_in, w_out, rpb, mu, w0, w_up, a0, a_up, g_up, k_k, k_a, r_k, lnx_g, lnx_b):
    bsz, l, dm = h3.shape
    n = bsz * l
    h2 = h3.reshape(n, dm)
    n_qkv = 3 * (w_out.shape[0] // 2)
    qkv, rest = norm_inproj(h2, gain.astype(F32), w_in.astype(BF16), n_qkv)
    na = na_attention(qkv.reshape(bsz, l, n_qkv), rpb)
    rw = rwkv_mix(rest.reshape(bsz, l, -1), mu, w0, w_up, a0, a_up, g_up, k_k, k_a, r_k, lnx_g, lnx_b)
    out = outproj_residual(h2, na.reshape(n, -1), rw.reshape(n, -1), w_out.astype(BF16))
    return out.reshape(bsz, l, dm)


def kernel(x, meta_tokens, norm_mix, norm_ffn, norm_final, mix_w_in, mix_w_out, na_rpb, rwkv_mu,
           rwkv_w0, rwkv_w_up, rwkv_a0, rwkv_a_up, rwkv_g_up, rwkv_k_k, rwkv_k_a, rwkv_r_k,
           rwkv_lnx_g, rwkv_lnx_b, s5_b_re, s5_b_im, s5_lambda_re, s5_lambda_im, s5_log_step,
           s5_c_re, s5_c_im, s5_d, s5_w_glu, moe_w_group, moe_b_group, moe_w_expert, moe_b_expert,
           moe_w1, moe_w3, moe_w2):
    bsz, _, dm = x.shape
    depth = norm_mix.shape[0]
    meta = jnp.broadcast_to(meta_tokens.astype(x.dtype)[None], (bsz,) + meta_tokens.shape)
    h = jnp.concatenate([meta, x], axis=1)
    l = h.shape[1]
    for layer in range(depth):
        i = layer // 2
        if layer % 2 == 0:
            h = na_rwkv_mix(h, norm_mix[layer], mix_w_in[i], mix_w_out[i], na_rpb[i], rwkv_mu[i], rwkv_w0[i],
                            rwkv_w_up[i], rwkv_a0[i], rwkv_a_up[i], rwkv_g_up[i], rwkv_k_k[i], rwkv_k_a[i],
                            rwkv_r_k[i], rwkv_lnx_g[i], rwkv_lnx_b[i])
        else:
            h = s5_mix(h, norm_mix[layer], s5_b_re[i], s5_b_im[i], s5_lambda_re[i], s5_lambda_im[i],
                       s5_log_step[i], s5_c_re[i], s5_c_im[i], s5_d[i], s5_w_glu[i])
        h = hierarchical_moe_residual(h.reshape(bsz * l, dm), norm_ffn[layer].astype(F32), moe_w_group[layer],
                                      moe_b_group[layer], moe_w_expert[layer], moe_b_expert[layer],
                                      moe_w1, moe_w3, moe_w2, layer).reshape(bsz, l, dm)
    return final_norm(h, norm_final.astype(F32))
```

```python
import functools
import math

import jax
import jax.numpy as jnp
from jax import lax
from jax.experimental import pallas as pl
from jax.experimental.pallas import tpu as pltpu

F32 = jnp.float32
BF16 = jnp.bfloat16
I32 = jnp.int32

N_META = 16
GRID_W = 64
HEAD_DIM = 64
NA_WIN_ROWS = 8
NA_WIN_COLS = 16
S5_GROUP_CH = 16
S5_STATE = 64
MOE_GROUPS = 4
MOE_PER_GROUP = 8
MOE_EXPERTS = MOE_GROUPS * MOE_PER_GROUP
NORM_EPS = 1e-6
RWKV_GN_EPS = 64e-5
NEG_INF = -1e30

LANES = 128
SUBLANES_BF16 = 16
VMEM_LIMIT_BYTES = 56 * 1024 * 1024

MOE_TILE = 256
ROUTER_LANES = 128


def _cparams(*sem):
    return pltpu.CompilerParams(dimension_semantics=sem, vmem_limit_bytes=VMEM_LIMIT_BYTES)


def _row_tile(n, target):
    best = None
    for t in range(SUBLANES_BF16, min(n, target) + 1, SUBLANES_BF16):
        if n % t == 0:
            best = t
    assert best is not None, (n, target)
    return best


def _rms(x, gain):
    ms = jnp.mean(x * x, axis=-1, keepdims=True)
    return (x * lax.rsqrt(ms + NORM_EPS)) * gain


def _split_bf16(x):
    hi = x.astype(BF16)
    lo = (x - hi.astype(F32)).astype(BF16)
    return hi, lo


def _dot(a, b):
    return jnp.dot(a, b, preferred_element_type=F32)


def _dot_nt(a, b):
    return lax.dot_general(a, b, (((1,), (1,)), ((), ())), preferred_element_type=F32)


def _norm_inproj_kernel(h_ref, g_ref, w_ref, qkv_ref, rest_ref, xn_ref, *, n_qkv, chunk):
    xn_ref[...] = _rms(h_ref[...], g_ref[...]).astype(BF16)
    n_all = w_ref.shape[1]
    for c in range(0, n_all, chunk):
        y = _dot(xn_ref[...], w_ref[:, c:c + chunk])
        if c < n_qkv:
            qkv_ref[:, c:c + chunk] = y.astype(BF16)
        else:
            rest_ref[:, c - n_qkv:c - n_qkv + chunk] = y


def norm_inproj(h2, gain, w_bf16, n_qkv):
    n, d = h2.shape
    n_all = w_bf16.shape[1]
    tm = _row_tile(n, 608)
    chunk = 256
    assert n_qkv % chunk == 0 and n_all % chunk == 0
    return pl.pallas_call(
        functools.partial(_norm_inproj_kernel, n_qkv=n_qkv, chunk=chunk),
        grid=(n // tm,),
        in_specs=[
            pl.BlockSpec((tm, d), lambda i: (i, 0)),
            pl.BlockSpec((1, d), lambda i: (0, 0)),
            pl.BlockSpec((d, n_all), lambda i: (0, 0)),
        ],
        out_specs=[
            pl.BlockSpec((tm, n_qkv), lambda i: (i, 0)),
            pl.BlockSpec((tm, n_all - n_qkv), lambda i: (i, 0)),
        ],
        out_shape=[
            jax.ShapeDtypeStruct((n, n_qkv), BF16),
            jax.ShapeDtypeStruct((n, n_all - n_qkv), F32),
        ],
        scratch_shapes=[pltpu.VMEM((tm, d), BF16)],
        compiler_params=_cparams("parallel"),
        name="norm_inproj",
    )(h2, gain.reshape(1, d), w_bf16)


def _outproj_kernel(h_ref, na_ref, rw_ref, wa_ref, wb_ref, o_ref):
    acc = _dot(na_ref[...], wa_ref[...])
    acc = acc + _dot(rw_ref[...], wb_ref[...])
    o_ref[...] = h_ref[...] + acc


def outproj_residual(h2, na, rw, w_out_bf16):
    n, d = h2.shape
    ka, kb = na.shape[1], rw.shape[1]
    tm = _row_tile(n, 608)
    return pl.pallas_call(
        _outproj_kernel,
        grid=(n // tm,),
        in_specs=[
            pl.BlockSpec((tm, d), lambda i: (i, 0)),
            pl.BlockSpec((tm, ka), lambda i: (i, 0)),
            pl.BlockSpec((tm, kb), lambda i: (i, 0)),
            pl.BlockSpec((ka, d), lambda i: (0, 0)),
            pl.BlockSpec((kb, d), lambda i: (0, 0)),
        ],
        out_specs=pl.BlockSpec((tm, d), lambda i: (i, 0)),
        out_shape=jax.ShapeDtypeStruct((n, d), F32),
        compiler_params=_cparams("parallel"),
        name="outproj_residual",
    )(h2, na, rw, w_out_bf16[:ka], w_out_bf16[ka:])


def _store_token_tiles(ref, x):
    rows = x.shape[0]
    s_n = x.shape[1] // LANES
    for s in range(s_n):
        ref[pl.ds(s, rows, stride=s_n), :] = x[:, s * LANES:(s + 1) * LANES]


def _load_token_tile_cols(ref, s, rows, s_n):
    return ref[pl.ds(s, rows, stride=s_n), :]


def _router_kernel(h_ref, g_ref, whi_ref, wlo_ref, b_ref, xn_ref, route_ref):
    xn = _rms(h_ref[...], g_ref[...])
    x_hi, x_lo = _split_bf16(xn)
    _store_token_tiles(xn_ref, xn)
    logits = (_dot(x_hi, whi_ref[...]) + _dot(x_hi, wlo_ref[...]) + _dot(x_lo, whi_ref[...])
              + b_ref[...])
    tm = logits.shape[0]
    lane = lax.broadcasted_iota(I32, (tm, ROUTER_LANES), 1)
    big = jnp.int32(ROUTER_LANES)

    is_g = lane < MOE_GROUPS
    lg = jnp.where(is_g, logits, -jnp.inf)
    eg = jnp.where(is_g, jnp.exp(lg - jnp.max(lg, axis=-1, keepdims=True)), 0.0)
    pg = eg / jnp.sum(eg, axis=-1, keepdims=True)
    p_grp = jnp.max(pg, axis=-1, keepdims=True)
    grp = jnp.min(jnp.where(is_g & (pg == p_grp), lane, big), axis=-1, keepdims=True)

    lo_lane = MOE_GROUPS + MOE_PER_GROUP * grp
    is_e = (lane >= lo_lane) & (lane < lo_lane + MOE_PER_GROUP)
    le = jnp.where(is_e, logits, -jnp.inf)
    ee = jnp.where(is_e, jnp.exp(le - jnp.max(le, axis=-1, keepdims=True)), 0.0)
    pe = jnp.where(is_e, ee / jnp.sum(ee, axis=-1, keepdims=True), -1.0)
    p1 = jnp.max(pe, axis=-1, keepdims=True)
    i1 = jnp.min(jnp.where(pe == p1, lane, big), axis=-1, keepdims=True)
    pe2 = jnp.where(lane == i1, -1.0, pe)
    p2 = jnp.max(pe2, axis=-1, keepdims=True)
    i2 = jnp.min(jnp.where(pe2 == p2, lane, big), axis=-1, keepdims=True)
    denom = p1 + p2
    g1 = p_grp * p1 / denom
    g2 = p_grp * p2 / denom
    e1 = (i1 - MOE_GROUPS).astype(F32)
    e2 = (i2 - MOE_GROUPS).astype(F32)
    route_ref[...] = jnp.where(lane == 0, e1, jnp.where(lane == 1, e2, jnp.where(lane == 2, g1, g2)))


def moe_router(h2, gain, w_group, b_group, w_expert, b_expert):
    n, d = h2.shape
    n_r = MOE_GROUPS + MOE_EXPERTS
    w_r = jnp.concatenate([w_group, jnp.transpose(w_expert, (1, 0, 2)).reshape(d, MOE_EXPERTS)], axis=1)
    w_r = jnp.pad(w_r.astype(F32), ((0, 0), (0, ROUTER_LANES - n_r)))
    w_hi, w_lo = _split_bf16(w_r)
    b_r = jnp.pad(jnp.concatenate([b_group, b_expert.reshape(-1)]).astype(F32), (0, ROUTER_LANES - n_r))
    tm = _row_tile(n, 608)
    return pl.pallas_call(
        _router_kernel,
        grid=(n // tm,),
        in_specs=[
            pl.BlockSpec((tm, d), lambda i: (i, 0)),
            pl.BlockSpec((1, d), lambda i: (0, 0)),
            pl.BlockSpec((d, ROUTER_LANES), lambda i: (0, 0)),
            pl.BlockSpec((d, ROUTER_LANES), lambda i: (0, 0)),
            pl.BlockSpec((1, ROUTER_LANES), lambda i: (0, 0)),
        ],
        out_specs=[
            pl.BlockSpec((tm * (d // LANES), LANES), lambda i: (i, 0)),
            pl.BlockSpec((tm, ROUTER_LANES), lambda i: (i, 0)),
        ],
        out_shape=[
            jax.ShapeDtypeStruct((n * (d // LANES), LANES), F32),
            jax.ShapeDtypeStruct((n, ROUTER_LANES), F32),
        ],
        compiler_params=_cparams("parallel"),
        name="moe_router",
    )(h2, gain.reshape(1, d), w_hi, w_lo, b_r.reshape(1, ROUTER_LANES))


def _expert_kernel(blk_e_ref, n_used_ref, src_ref, src_next_ref, dst_ref, xn_hbm, w1_ref, w3_ref, w2_ref, y_hbm,
                   xg_ref, xb_ref, yv_ref, gsem, ssem, w1b_ref, w3b_ref, w2b_ref, *, trash_row):
    i = pl.program_id(0)
    n_steps = pl.num_programs(0)
    n_used = n_used_ref[0]
    used = i < n_used
    slot = i & 1
    tile, d = xb_ref.shape
    s_n = d // LANES

    def start_gather(ids_ref, dst_slot):
        for r in range(tile):
            src = pl.multiple_of(ids_ref[0, r], s_n)
            pltpu.make_async_copy(xn_hbm.at[pl.ds(src, s_n), :], xg_ref.at[dst_slot, pl.ds(r * s_n, s_n), :],
                                  gsem.at[dst_slot]).start()

    def wait_rows_in(s):
        pltpu.make_async_copy(xn_hbm.at[pl.ds(0, tile * s_n), :], xg_ref.at[s], gsem.at[s]).wait()

    def wait_rows_out(s):
        pltpu.make_async_copy(yv_ref.at[s], y_hbm.at[pl.ds(0, tile * s_n), :], ssem.at[s]).wait()

    @pl.when(used & (i == 0))
    def _():
        start_gather(src_ref, 0)

    prev_e = blk_e_ref[jnp.maximum(i - 1, 0)]
    fresh = (i == 0) | (blk_e_ref[i] != prev_e)

    @pl.when(used & fresh)
    def _():
        w1b_ref[...] = w1_ref[...].astype(BF16)
        w3b_ref[...] = w3_ref[...].astype(BF16)
        w2b_ref[...] = w2_ref[...].astype(BF16)

    @pl.when(used & (i >= 2))
    def _():
        wait_rows_out(slot)

    @pl.when(used)
    def _():
        wait_rows_in(slot)
        for s in range(s_n):
            xb_ref[:, s * LANES:(s + 1) * LANES] = _load_token_tile_cols(xg_ref.at[slot], s, tile, s_n).astype(BF16)
        x = xb_ref[...]
        a = _dot(x, w1b_ref[...])
        b = _dot(x, w3b_ref[...])
        hmid = (a * jax.nn.sigmoid(a) * b).astype(BF16)
        y = _dot(hmid, w2b_ref[...])
        start_gather(src_next_ref, 1 - slot)
        _store_token_tiles(yv_ref.at[slot], y)
        for r in range(tile):
            dst = pl.multiple_of(dst_ref[0, r], s_n)
            pltpu.make_async_copy(yv_ref.at[slot, pl.ds(r * s_n, s_n), :], y_hbm.at[pl.ds(dst, s_n), :],
                                  ssem.at[slot]).start()

    @pl.when(i == n_steps - 1)
    def _():
        @pl.when(n_used >= 1)
        def _():
            wait_rows_in(n_used & 1)
            wait_rows_out((n_used - 1) & 1)

        @pl.when(n_used >= 2)
        def _():
            wait_rows_out((n_used - 2) & 1)

        yv_ref[0] = jnp.zeros(yv_ref.shape[1:], yv_ref.dtype)
        fills = [pltpu.make_async_copy(yv_ref.at[0], y_hbm.at[pl.ds((trash_row + s * tile) * s_n, tile * s_n), :],
                                       ssem.at[s]) for s in range(2)]
        for cp in fills:
            cp.start()
        for cp in fills:
            cp.wait()


def moe_experts(xn_tiles, src_of_row, dst_of_row, blk_e, n_used, w1, w3, w2, layer, n_out_rows):
    d, f = w1.shape[2], w1.shape[3]
    s_n = d // LANES
    tile = MOE_TILE
    n_blocks = src_of_row.shape[0] // tile
    src3 = src_of_row.reshape(n_blocks, 1, tile)
    dst3 = dst_of_row.reshape(n_blocks, 1, tile)

    def w_map(i, blk_e_ref, n_used_ref):
        return (layer, blk_e_ref[i], 0, 0)

    ids = lambda index: pl.BlockSpec((None, 1, tile), lambda i, be, nu: (index(i), 0, 0), memory_space=pltpu.SMEM)
    grid_spec = pltpu.PrefetchScalarGridSpec(
        num_scalar_prefetch=2,
        grid=(n_blocks,),
        in_specs=[
            ids(lambda i: i),
            ids(lambda i: jnp.minimum(i + 1, n_blocks - 1)),
            ids(lambda i: i),
            pl.BlockSpec(memory_space=pl.ANY),
            pl.BlockSpec((None, None, d, f), w_map),
            pl.BlockSpec((None, None, d, f), w_map),
            pl.BlockSpec((None, None, f, d), w_map),
        ],
        out_specs=pl.BlockSpec(memory_space=pl.ANY),
        scratch_shapes=[
            pltpu.VMEM((2, tile * s_n, LANES), F32),
            pltpu.VMEM((tile, d), BF16),
            pltpu.VMEM((2, tile * s_n, LANES), F32),
            pltpu.SemaphoreType.DMA((2,)),
            pltpu.SemaphoreType.DMA((2,)),
            pltpu.VMEM((d, f), BF16),
            pltpu.VMEM((d, f), BF16),
            pltpu.VMEM((f, d), BF16),
        ],
    )
    return pl.pallas_call(
        functools.partial(_expert_kernel, trash_row=n_out_rows),
        grid_spec=grid_spec,
        out_shape=jax.ShapeDtypeStruct(((n_out_rows + 2 * tile) * s_n, LANES), F32),
        compiler_params=_cparams("arbitrary"),
        name="moe_experts",
    )(blk_e, n_used, src3, src3, dst3, xn_tiles, w1, w3, w2)


def _moe_combine_kernel(h_ref, route_ref, y0_ref, y1_ref, o_ref):
    tm, d = h_ref.shape
    s_n = d // LANES
    route = route_ref[...]
    g1 = route[:, 2:3]
    g2 = route[:, 3:4]
    for s in range(s_n):
        cols = slice(s * LANES, (s + 1) * LANES)
        o_ref[:, cols] = (h_ref[:, cols] + g1 * _load_token_tile_cols(y0_ref, s, tm, s_n)
                          + g2 * _load_token_tile_cols(y1_ref, s, tm, s_n))


def moe_combine(h2, route, y_tiles):
    n, d = h2.shape
    s_n = d // LANES
    tm = _row_tile(n, 608)
    nb = n // tm
    return pl.pallas_call(
        _moe_combine_kernel,
        grid=(nb,),
        in_specs=[
            pl.BlockSpec((tm, d), lambda i: (i, 0)),
            pl.BlockSpec((tm, ROUTER_LANES), lambda i: (i, 0)),
            pl.BlockSpec((tm * s_n, LANES), lambda i: (i, 0)),
            pl.BlockSpec((tm * s_n, LANES), lambda i: (i + nb, 0)),
        ],
        out_specs=pl.BlockSpec((tm, d), lambda i: (i, 0)),
        out_shape=jax.ShapeDtypeStruct((n, d), F32),
        compiler_params=_cparams("parallel"),
        name="moe_combine",
    )(h2, route, y_tiles, y_tiles)


def hierarchical_moe_residual(h2, gain, w_group, b_group, w_expert, b_expert, w1, w3, w2, layer):
    n, d = h2.shape
    xn, route = moe_router(h2, gain, w_group, b_group, w_expert, b_expert)
    e_km = jnp.concatenate([route[:, 0], route[:, 1]]).astype(I32)
    n_assign = 2 * n
    onehot = (e_km[:, None] == jnp.arange(MOE_EXPERTS, dtype=I32)[None, :]).astype(I32)
    csum = jnp.cumsum(onehot, axis=0)
    counts = csum[-1]
    padded = (counts + MOE_TILE - 1) // MOE_TILE * MOE_TILE
    pad_end = jnp.cumsum(padded)
    pad_start = pad_end - padded
    dest = jnp.sum((csum - onehot + pad_start[None, :]) * onehot, axis=1)
    n_blocks = -(-n_assign // MOE_TILE) + MOE_EXPERTS
    n_rows = n_blocks * MOE_TILE
    blk_start = jnp.arange(n_blocks, dtype=I32) * MOE_TILE
    blk_e = jnp.minimum(jnp.sum((pad_end[None, :] <= blk_start[:, None]).astype(I32), axis=1),
                        MOE_EXPERTS - 1).astype(I32)
    n_used = (pad_end[-1] // MOE_TILE).astype(I32).reshape(1)
    asg_of_row = jnp.full((n_rows,), n_assign, I32).at[dest].set(jnp.arange(n_assign, dtype=I32))
    s_n = d // LANES
    is_pad = asg_of_row >= n_assign
    row_id = jnp.arange(n_rows, dtype=I32)
    src_of_row = jnp.where(is_pad, n - 1, jnp.where(asg_of_row >= n, asg_of_row - n, asg_of_row)) * s_n
    dst_of_row = jnp.where(is_pad, n_assign + row_id % (2 * MOE_TILE), asg_of_row) * s_n
    y = moe_experts(xn, src_of_row, dst_of_row, blk_e, n_used, w1, w3, w2, layer, n_assign)
    return moe_combine(h2, route, y)


def _final_norm_kernel(h_ref, g_ref, o_ref):
    o_ref[...] = _rms(h_ref[...], g_ref[...])


def final_norm(h3, gain):
    b, l, d = h3.shape
    t = l - N_META
    tm = _row_tile(t, 512)
    return pl.pallas_call(
        _final_norm_kernel,
        grid=(b, t // tm),
        in_specs=[
            pl.BlockSpec((None, pl.Element(tm), pl.Element(d)),
                         lambda bi, i: (bi, pl.multiple_of(N_META + i * tm, SUBLANES_BF16), 0)),
            pl.BlockSpec((1, d), lambda bi, i: (0, 0)),
        ],
        out_specs=pl.BlockSpec((None, tm, d), lambda bi, i: (bi, i, 0)),
        out_shape=jax.ShapeDtypeStruct((b, t, d), F32),
        compiler_params=_cparams("parallel", "parallel"),
        name="final_norm",
    )(h3, gain.reshape(1, d))


NA_QROWS = 8
NA_KROWS = 3 * NA_QROWS
NA_ROWS_PER_ITER = 2


def _na_kernel(q_ref, kw_ref, vw_ref, qm_ref, km_ref, vm_ref, bias_ref, o_ref, om_ref, *, rows, scale):
    blk = pl.program_id(1)
    tq = GRID_W
    n_pairs = q_ref.shape[1] // LANES
    base = jnp.clip(NA_QROWS * blk - NA_QROWS, 0, rows - NA_KROWS)
    lane = lax.broadcasted_iota(I32, (tq, LANES), 1)
    halves = [lane < HEAD_DIM, lane >= HEAD_DIM]

    pad = jnp.zeros((LANES - N_META, LANES), km_ref.dtype)
    k_meta = [jnp.concatenate([km_ref[:, p * LANES:(p + 1) * LANES], pad], axis=0) for p in range(n_pairs)]
    v_meta = [jnp.concatenate([vm_ref[:, p * LANES:(p + 1) * LANES], pad], axis=0) for p in range(n_pairs)]
    meta_bias = jnp.where(lane < N_META, 0.0, NEG_INF)

    n_win = NA_WIN_ROWS * GRID_W

    def row_body(jb, carry):
        colsl = [slice(p * LANES, (p + 1) * LANES) for p in range(n_pairs)]
        units = [(jj, p, hh) for jj in range(NA_ROWS_PER_ITER) for p in range(n_pairs) for hh in range(2)]
        s_idx, koff, qoff = [], [], []
        for jj in range(NA_ROWS_PER_ITER):
            j = jb * NA_ROWS_PER_ITER + jj
            r = NA_QROWS * blk + j
            start = jnp.clip(r - NA_WIN_ROWS // 2, 0, rows - NA_WIN_ROWS)
            s_idx.append(start - r + (NA_WIN_ROWS - 1))
            koff.append(pl.multiple_of((start - base) * GRID_W, GRID_W))
            qoff.append(pl.multiple_of(j * tq, tq))
        q_pair = {(jj, p): q_ref[pl.ds(qoff[jj], tq), c]
                  for jj in range(NA_ROWS_PER_ITER) for p, c in enumerate(colsl)}
        k_ext = {(jj, p): jnp.concatenate([kw_ref[pl.ds(koff[jj], n_win), c], k_meta[p]], axis=0)
                 for jj in range(NA_ROWS_PER_ITER) for p, c in enumerate(colsl)}
        v_ext = {(jj, p): jnp.concatenate([vw_ref[pl.ds(koff[jj], n_win), c], v_meta[p]], axis=0)
                 for jj in range(NA_ROWS_PER_ITER) for p, c in enumerate(colsl)}
        qh = [jnp.where(halves[hh], q_pair[jj, p], jnp.zeros_like(q_pair[jj, p])) for jj, p, hh in units]
        s = [_dot_nt(qh[u], k_ext[jj, p]) * scale
             + jnp.concatenate([bias_ref[2 * p + hh, s_idx[jj]], meta_bias], axis=1)
             for u, (jj, p, hh) in enumerate(units)]
        m = [jnp.max(x, axis=-1, keepdims=True) for x in s]
        e = [jnp.exp(x - mx) for x, mx in zip(s, m)]
        den = [jnp.sum(x, axis=-1, keepdims=True) for x in e]
        o = [_dot(e[u].astype(BF16), v_ext[jj, p]) / den[u] for u, (jj, p, hh) in enumerate(units)]
        for u in range(0, len(units), 2):
            jj, p, _ = units[u]
            o_ref[pl.ds(qoff[jj], tq), colsl[p]] = jnp.where(halves[0], o[u], o[u + 1]).astype(o_ref.dtype)
        return carry

    lax.fori_loop(0, NA_QROWS // NA_ROWS_PER_ITER, row_body, 0)

    @pl.when(blk == 0)
    def _():
        lane_m = lax.broadcasted_iota(I32, (N_META, LANES), 1)
        for p in range(n_pairs):
            cols = slice(p * LANES, (p + 1) * LANES)
            q_pair = qm_ref[:, cols]
            kmp = km_ref[:, cols]
            vmp = vm_ref[:, cols]
            outs = []
            for hh in range(2):
                sel = (lane_m < HEAD_DIM) if hh == 0 else (lane_m >= HEAD_DIM)
                qp = jnp.where(sel, q_pair, jnp.zeros_like(q_pair))
                s_m = _dot_nt(qp, kmp) * scale
                p_m = jnp.exp(s_m - jnp.max(s_m, axis=-1, keepdims=True))
                den = jnp.sum(p_m, axis=-1, keepdims=True)
                outs.append(_dot(p_m.astype(BF16), vmp) / den)
            om_ref[:, cols] = jnp.where(lane_m < HEAD_DIM, outs[0], outs[1]).astype(om_ref.dtype)


def _na_bias_table(rpb):
    h = rpb.shape[0]
    c_ids = jnp.arange(GRID_W)
    c_start = jnp.clip(c_ids - NA_WIN_COLS // 2, 0, GRID_W - NA_WIN_COLS)
    in_band = (c_ids[None, :] >= c_start[:, None]) & (c_ids[None, :] < c_start[:, None] + NA_WIN_COLS)
    dc = jnp.clip(c_ids[None, :] - c_ids[:, None] + NA_WIN_COLS - 1, 0, 2 * NA_WIN_COLS - 2)
    tab = jnp.where(in_band[None, None], rpb.astype(F32)[:, :, dc], NEG_INF)
    win = jnp.stack([tab[:, s:s + NA_WIN_ROWS] for s in range(NA_WIN_ROWS)], axis=1)
    return jnp.transpose(win, (0, 1, 3, 2, 4)).reshape(h, NA_WIN_ROWS, GRID_W, NA_WIN_ROWS * GRID_W)


def na_attention(qkv, rpb):
    b, l, w3 = qkv.shape
    w = w3 // 3
    t = l - N_META
    rows = t // GRID_W
    assert rows * GRID_W == t and rows % NA_QROWS == 0 and rows >= NA_KROWS
    tq = NA_QROWS * GRID_W
    tk = NA_KROWS * GRID_W
    bias = _na_bias_table(rpb)
    al = SUBLANES_BF16

    def q_map(bi, i):
        return (bi, pl.multiple_of(N_META + i * tq, al), 0)

    def kv_map(col):
        def f(bi, i):
            base = jnp.clip(NA_QROWS * i - NA_QROWS, 0, rows - NA_KROWS)
            return (bi, pl.multiple_of(N_META + base * GRID_W, al), col)
        return f

    def meta_map(col):
        return lambda bi, i: (bi, 0, col)

    el = pl.Element
    grid_out, meta_out = pl.pallas_call(
        functools.partial(_na_kernel, rows=rows, scale=HEAD_DIM ** -0.5),
        grid=(b, rows // NA_QROWS),
        in_specs=[
            pl.BlockSpec((None, el(tq), el(w)), q_map),
            pl.BlockSpec((None, el(tk), el(w)), kv_map(w)),
            pl.BlockSpec((None, el(tk), el(w)), kv_map(2 * w)),
            pl.BlockSpec((None, el(N_META), el(w)), meta_map(0)),
            pl.BlockSpec((None, el(N_META), el(w)), meta_map(w)),
            pl.BlockSpec((None, el(N_META), el(w)), meta_map(2 * w)),
            pl.BlockSpec(bias.shape, lambda bi, i: (0, 0, 0, 0)),
        ],
        out_specs=[
            pl.BlockSpec((None, tq, w), lambda bi, i: (bi, i, 0)),
            pl.BlockSpec((None, N_META, w), lambda bi, i: (bi, 0, 0)),
        ],
        out_shape=[
            jax.ShapeDtypeStruct((b, t, w), BF16),
            jax.ShapeDtypeStruct((b, N_META, w), BF16),
        ],
        compiler_params=_cparams("parallel", "arbitrary"),
        name="na_attention",
    )(qkv, qkv, qkv, qkv, qkv, qkv, bias)
    return jnp.concatenate([meta_out, grid_out], axis=1)


RWKV_CHUNK = 64
RWKV_TILE_CHUNKS = 2
RWKV_HALO = 8


def _split3_bf16(x):
    p1 = x.astype(BF16)
    r1 = x - p1.astype(F32)
    p2 = r1.astype(BF16)
    p3 = (r1 - p2.astype(F32)).astype(BF16)
    return p1, p2, p3


def _mm1(a, b):
    return _dot(a.astype(BF16), b.astype(BF16))


def _mm3(a, b):
    ah, al = _split_bf16(a)
    bh, bl = _split_bf16(b)
    return _dot(ah, bh) + _dot(ah, bl) + _dot(al, bh)


def _mm1_nt(a, b):
    return _dot_nt(a.astype(BF16), b.astype(BF16))


def _mm3_nt(a, b):
    ah, al = _split_bf16(a)
    bh, bl = _split_bf16(b)
    return _dot_nt(ah, bh) + _dot_nt(ah, bl) + _dot_nt(al, bh)


def _exact_left(mat_bf16, x):
    p1, p2, p3 = _split3_bf16(x)
    return _dot(mat_bf16, p1) + _dot(mat_bf16, p2) + _dot(mat_bf16, p3)


def _exact_right(x, mat_bf16):
    p1, p2, p3 = _split3_bf16(x)
    return _dot(p1, mat_bf16) + _dot(p2, mat_bf16) + _dot(p3, mat_bf16)


def _head_block_ones(width):
    ri = lax.broadcasted_iota(I32, (width, width), 0) // HEAD_DIM
    ci = lax.broadcasted_iota(I32, (width, width), 1) // HEAD_DIM
    return (ri == ci).astype(BF16)


def _head_sums(x, exact):
    ones_pair = _head_block_ones(LANES)
    tiles = []
    for p in range(x.shape[1] // LANES):
        xt = x[:, p * LANES:(p + 1) * LANES]
        tiles.append(_exact_right(xt, ones_pair) if exact else _dot(xt.astype(BF16), ones_pair))
    return jnp.concatenate(tiles, axis=1)


def _stack_heads(x, m0):
    z = jnp.zeros_like(x)
    return jnp.concatenate([jnp.where(m0, x, z), jnp.where(m0, z, x)], axis=0)


_MM_L4 = _mm1_nt
_MM_KT = _mm1_nt
_MM_SQ = _mm1
_MM_AP = _mm1
_MM_V = _mm1
_MM_Y = _mm1
_MM_UPD = _mm1


def _rwkv_chunk_maps(streams, c, n_sub):
    assert c == 64
    c2 = 2 * c
    lane = lax.broadcasted_iota(I32, (c, LANES), 1)
    m0 = lane < HEAD_DIM
    r_i = lax.broadcasted_iota(I32, (c2, c2), 0)
    c_i = lax.broadcasted_iota(I32, (c2, c2), 1)
    eye = (r_i == c_i).astype(F32)
    rel = r_i % c - c_i % c
    masks = {sg: (rel * sg > 0, rel * sg >= 0) for sg in {s["sign"] for s in streams}}
    items = [(j, q) for j in range(len(streams)) for q in range(n_sub)]

    def part(j, q, name):
        return _stack_heads(streams[j][name][q * c:(q + 1) * c], m0)

    lhs = {it: jnp.concatenate([part(*it, "kkp"), part(*it, "rp")], axis=0) for it in items}
    rhs = {it: jnp.concatenate([part(*it, "ki"), part(*it, "bi")], axis=0) for it in items}
    vs = {it: part(*it, "v") for it in items}
    kipcs = {it: part(*it, "kipc") for it in items}
    bipcs = {it: part(*it, "bipc") for it in items}
    l4 = {it: _MM_L4(lhs[it], rhs[it]) for it in items}
    m_kk, n1, m_rk, m_rb = {}, {}, {}, {}
    for it in items:
        strict, incl = masks[streams[it[0]]["sign"]]
        m = l4[it]
        m_kk[it] = jnp.where(strict, m[0:c2, 0:c2], 0.0)
        n1[it] = jnp.where(strict, m[0:c2, c2:2 * c2], 0.0)
        m_rk[it] = jnp.where(incl, m[c2:2 * c2, 0:c2], 0.0)
        m_rb[it] = jnp.where(incl, m[c2:2 * c2, c2:2 * c2], 0.0)
    n2 = {it: _MM_SQ(n1[it], n1[it]) for it in items}
    n4 = {it: _MM_SQ(n2[it], n2[it]) for it in items}
    n8 = {it: _MM_SQ(n4[it], n4[it]) for it in items}
    n16 = {it: _MM_SQ(n8[it], n8[it]) for it in items}
    n32 = {it: _MM_SQ(n16[it], n16[it]) for it in items}
    p1 = {it: (eye - n1[it]) + _MM_AP(eye - n1[it], n2[it]) for it in items}
    p2 = {it: eye + n4[it] + n8[it] + _MM_AP(n4[it], n8[it]) for it in items}
    p3 = {it: eye + n16[it] + n32[it] + _MM_AP(n16[it], n32[it]) for it in items}
    p23 = {it: _MM_AP(p2[it], p3[it]) for it in items}
    winv = {it: _MM_AP(p1[it], p23[it]) for it in items}
    mv = {it: _MM_V(m_kk[it], vs[it]) for it in items}
    mrv = {it: _MM_Y(m_rk[it], vs[it]) for it in items}
    wl = {it: _MM_AP(winv[it], lhs[it][0:c2]) for it in items}
    wmv = {it: _MM_AP(winv[it], mv[it]) for it in items}
    yl = {it: lhs[it][c2:2 * c2] - _MM_Y(m_rb[it], wl[it]) for it in items}
    y0 = {it: mrv[it] - _MM_Y(m_rb[it], wmv[it]) for it in items}
    g2 = {it: _MM_UPD(jnp.transpose(wl[it]), bipcs[it]) for it in items}
    hh = {it: _MM_UPD(jnp.transpose(jnp.concatenate([vs[it], -wmv[it]], axis=0)),
                      jnp.concatenate([kipcs[it], bipcs[it]], axis=0)) for it in items}
    return {it: (yl[it], y0[it], g2[it], hh[it]) for it in items}


def _rwkv_apply_maps(streams, maps, c, n_sub):
    c2 = 2 * c
    st = [s["st"] for s in streams]
    ys = {}
    for k in range(n_sub):
        cur = [(j, k if s["sign"] > 0 else n_sub - 1 - k) for j, s in enumerate(streams)]
        sg = [_MM_UPD(st[j], maps[j][q][2]) for j, q in cur]
        yk = [_MM_KT(maps[j][q][0], st[j]) + maps[j][q][1] for j, q in cur]
        for it, yi in zip(cur, yk):
            ys[it] = yi[0:c] + yi[c:c2]
        st = [st[j] * streams[j]["pc"][q] - sgi + maps[j][q][3] for (j, q), sgi in zip(cur, sg)]
    return [(jnp.concatenate([ys[j, q] for q in range(n_sub)], axis=0), st[j]) for j in range(len(streams))]


def _softplus(z):
    return jnp.maximum(z, 0.0) + jnp.log(1.0 + jnp.exp(-jnp.abs(z)))


def _rwkv_tile_prep(x_ref, xp_ref, xn_ref, tile, n_tiles, seq_len, width, sign, mu, w0, a0, w_wa,
                    k_k, k_a, r_k):
    cs = RWKV_CHUNK
    c = x_ref.shape[0]
    valid = jnp.minimum(c, seq_len - tile * c)
    row = lax.broadcasted_iota(I32, (c, LANES), 0)
    rowv = row < valid
    lane = lax.broadcasted_iota(I32, (c, LANES), 1)
    ones_pair = _head_block_ones(LANES)

    def shifted(lo):
        cols = slice(lo, lo + LANES)
        x = jnp.where(rowv, x_ref[:, cols], 0.0)
        prev_row = jnp.where(tile > 0, xp_ref[RWKV_HALO - 1:RWKV_HALO, cols], 0.0)
        next_row = jnp.where(tile < n_tiles - 1, xn_ref[0:1, cols], 0.0)
        x_prev = jnp.where(row == 0, prev_row, pltpu.roll(x, 1, 0))
        x_next = jnp.where(row == c - 1, next_row, pltpu.roll(x, c - 1, 0))
        xs = x + mu[:, cols] * (0.5 * (x_prev + x_next) - x)
        return jnp.where(rowv, xs, 0.0)

    wa = shifted(3 * width)
    g_lo = shifted(3 * width + LANES)
    xwa = jnp.where(lane < LANES // 2, jnp.tanh(wa), wa)
    la = _dot(xwa.astype(BF16), w_wa)

    t_i = lax.broadcasted_iota(I32, (c, c), 0)
    s_i = lax.broadcasted_iota(I32, (c, c), 1)
    tri = ((t_i // cs == s_i // cs) & ((t_i - s_i) * sign >= 0)).astype(BF16)

    pairs = []
    for p in range(width // LANES):
        lo = p * LANES
        cols = slice(lo, lo + LANES)
        r = shifted(lo)
        k = shifted(width + lo)
        v = shifted(2 * width + lo)
        w_log = -_softplus(-(w0[:, cols] + la[:, cols])) - 0.5
        logw = jnp.where(rowv, -jnp.exp(w_log), 0.0)
        a = jax.nn.sigmoid(a0[:, cols] + la[:, width + lo:width + lo + LANES])
        kk0 = k * k_k[:, cols]
        ss = _dot((kk0 * kk0).astype(BF16), ones_pair)
        kk = kk0 / jnp.maximum(jnp.sqrt(ss), 1e-12)
        kdir = k * (1.0 + (a - 1.0) * k_a[:, cols])
        b = kk * a
        cl = _exact_left(tri, logw)
        lasts = [cl[q * cs + cs - 1:q * cs + cs, :] if sign > 0 else cl[q * cs:q * cs + 1, :]
                 for q in range(c // cs)]
        last = jnp.concatenate([jnp.broadcast_to(lq, (cs, LANES)) for lq in lasts], axis=0)
        e_n = jnp.exp(-cl)
        pcr = jnp.exp(last - cl)
        pairs.append(dict(kkp=kk * jnp.exp(cl - logw), rp=r * jnp.exp(cl), ki=kdir * e_n, bi=b * e_n,
                          kipc=kdir * pcr, bipc=b * pcr, v=v, pc=[jnp.exp(lq) for lq in lasts],
                          bonus=_dot((r * kdir * r_k[:, cols]).astype(BF16), ones_pair) * v))
    return pairs, g_lo


def _rwkv_scan_kernel(xf_ref, xfp_ref, xfn_ref, xb_ref, xbp_ref, xbn_ref, mu_ref, w0_ref, a0_ref, wwa_ref,
                      gup_ref, kk_ref, ka_ref, rk_ref, yf_ref, yb_ref, bonf_ref, bonb_ref, g_ref, st_ref,
                      *, seq_len, width):
    i = pl.program_id(1)
    n_chunks = pl.num_programs(1)
    n_pairs = width // LANES

    @pl.when(i == 0)
    def _():
        st_ref[...] = jnp.zeros_like(st_ref)

    common = (mu_ref[...],)
    tail = (kk_ref[...], ka_ref[...], rk_ref[...])
    n_sub = xf_ref.shape[0] // RWKV_CHUNK
    fwd, g_lo = _rwkv_tile_prep(xf_ref, xfp_ref, xfn_ref, i, n_chunks, seq_len, width, 1, *common,
                                w0_ref[0], a0_ref[0], wwa_ref[0], *tail)
    bwd, _ = _rwkv_tile_prep(xb_ref, xbp_ref, xbn_ref, n_chunks - 1 - i, n_chunks, seq_len, width, -1, *common,
                             w0_ref[1], a0_ref[1], wwa_ref[1], *tail)
    g_ref[...] = _mm1(jax.nn.sigmoid(g_lo), gup_ref[...]).astype(g_ref.dtype)
    streams = []
    for di, (pairs, sign, bon_ref) in enumerate(((fwd, 1, bonf_ref), (bwd, -1, bonb_ref))):
        for p, s in enumerate(pairs):
            bon_ref[:, p * LANES:(p + 1) * LANES] = s.pop("bonus")
            s["st"] = st_ref[di * n_pairs + p]
            s["sign"] = sign
            streams.append(s)
    m = _rwkv_chunk_maps(streams, RWKV_CHUNK, n_sub)
    maps = [{q: m[j, q] for q in range(n_sub)} for j in range(len(streams))]
    res = _rwkv_apply_maps(streams, maps, RWKV_CHUNK, n_sub)
    for j, (y, st_new) in enumerate(res):
        di, p = divmod(j, n_pairs)
        cols = slice(p * LANES, (p + 1) * LANES)
        (yf_ref if di == 0 else yb_ref)[:, cols] = y
        st_ref[j] = st_new


def _rwkv_finish_kernel(yf_ref, yb_ref, bonf_ref, bonb_ref, g_ref, lg_ref, lb_ref, o_ref):
    y = yf_ref[...] + yb_ref[...]
    mean = _head_sums(y, exact=True) * (1.0 / HEAD_DIM)
    yc = y - mean
    var = _head_sums(yc * yc, exact=True) * (1.0 / HEAD_DIM)
    yn = yc * lax.rsqrt(var + RWKV_GN_EPS) * lg_ref[...] + lb_ref[...]
    o_ref[...] = ((yn + bonf_ref[...] + bonb_ref[...]) * g_ref[...].astype(F32)).astype(o_ref.dtype)


def rwkv_mix(rest, mu, w0, w_up, a0, a_up, g_up, k_k, k_a, r_k, lnx_g, lnx_b):
    bsz, l, n_cols = rest.shape
    width = w0.shape[1]
    rank = w_up.shape[1]
    assert n_cols == 3 * width + 2 * LANES and 2 * rank == LANES and l % RWKV_HALO == 0
    c = RWKV_CHUNK * RWKV_TILE_CHUNKS
    n_chunks = -(-l // c)
    per = c // RWKV_HALO
    n_halo = l // RWKV_HALO
    zeros = jnp.zeros((2, rank, width), F32)
    w_wa = jnp.concatenate([jnp.concatenate([w_up.astype(F32), zeros], axis=2),
                            jnp.concatenate([zeros, a_up.astype(F32)], axis=2)], axis=1)
    w_wa = w_wa.astype(BF16)

    fwd_chunk = lambda i: i
    bwd_chunk = lambda i: n_chunks - 1 - i

    def tile_specs(chunk_of):
        return [
            pl.BlockSpec((None, c, n_cols), lambda b, i: (b, chunk_of(i), 0)),
            pl.BlockSpec((None, RWKV_HALO, n_cols), lambda b, i: (b, jnp.maximum(chunk_of(i) * per - 1, 0), 0)),
            pl.BlockSpec((None, RWKV_HALO, n_cols),
                         lambda b, i: (b, jnp.minimum((chunk_of(i) + 1) * per, n_halo - 1), 0)),
        ]

    row2 = lambda a: a.astype(F32).reshape(1, -1)
    whole = lambda *shape: pl.BlockSpec(shape, lambda b, i: (0,) * len(shape))
    out_spec = lambda chunk_of: pl.BlockSpec((None, c, width), lambda b, i: (b, chunk_of(i), 0))
    act = lambda dt: jax.ShapeDtypeStruct((bsz, l, width), dt)
    y_f, y_b, bon_f, bon_b, g = pl.pallas_call(
        functools.partial(_rwkv_scan_kernel, seq_len=l, width=width),
        grid=(bsz, n_chunks),
        in_specs=tile_specs(fwd_chunk) + tile_specs(bwd_chunk) + [
            whole(1, n_cols),
            whole(2, 1, width), whole(2, 1, width),
            whole(2, LANES, 2 * width),
            whole(LANES, width),
            whole(1, width), whole(1, width), whole(1, width),
        ],
        out_specs=[out_spec(fwd_chunk), out_spec(bwd_chunk), out_spec(fwd_chunk), out_spec(bwd_chunk),
                   out_spec(fwd_chunk)],
        out_shape=[act(F32), act(F32), act(F32), act(F32), act(BF16)],
        scratch_shapes=[pltpu.VMEM((2 * (width // LANES), LANES, LANES), F32)],
        compiler_params=_cparams("parallel", "arbitrary"),
        name="rwkv_scan",
    )(rest, rest, rest, rest, rest, rest, row2(mu), w0.astype(F32).reshape(2, 1, width),
      a0.astype(F32).reshape(2, 1, width), w_wa, g_up.astype(BF16), row2(k_k), row2(k_a), row2(r_k))

    n = bsz * l
    tm = _row_tile(n, 608)
    rows = lambda: pl.BlockSpec((tm, width), lambda j: (j, 0))
    flat = lambda a: a.reshape(n, width)
    return pl.pallas_call(
        _rwkv_finish_kernel,
        grid=(n // tm,),
        in_specs=[rows(), rows(), rows(), rows(), rows(),
                  pl.BlockSpec((1, width), lambda j: (0, 0)),
                  pl.BlockSpec((1, width), lambda j: (0, 0))],
        out_specs=rows(),
        out_shape=jax.ShapeDtypeStruct((n, width), BF16),
        compiler_params=_cparams("parallel"),
        name="rwkv_finish",
    )(flat(y_f), flat(y_b), flat(bon_f), flat(bon_b), flat(g), row2(lnx_g), row2(lnx_b)).reshape(bsz, l, width)


S5_CHUNK = 16


def _cpow(n, lr, li, step):
    mag = jnp.exp(n * (lr * step))
    ang = n * (li * step)
    return mag * jnp.cos(ang), mag * jnp.sin(ang)


def _s5_param_kernel(lamr_ref, stepr_ref, bt_ref, ct_ref, kmat_ref, wst_ref, cexp_ref, alpha_ref):
    t_len = S5_CHUNK
    n_i = S5_GROUP_CH
    p2 = 2 * S5_STATE
    ti = t_len * n_i

    lr = lamr_ref[0:1, :]
    li = lamr_ref[1:2, :]
    step = jnp.exp(stepr_ref[...])
    ab_re, ab_im = _cpow(1.0, lr, li, step)
    den = lr * lr + li * li
    z_re = ((ab_re - 1.0) * lr + ab_im * li) / den
    z_im = (ab_im * lr - (ab_re - 1.0) * li) / den
    t16 = lax.broadcasted_iota(I32, (t_len, p2), 0).astype(F32)
    is_f = lax.broadcasted_iota(I32, (t_len, p2), 1) < S5_STATE

    def rows_by_t(x):
        return jnp.concatenate([jnp.broadcast_to(x[t:t + 1], (n_i, p2)) for t in range(t_len)], axis=0)

    def tiled_rows(x):
        return jnp.concatenate([x] * t_len, axis=0)

    def pow_rows(n):
        q_re, q_im = _cpow(n, lr, li, step)
        return rows_by_t(q_re), rows_by_t(q_im)

    bt_re = tiled_rows(bt_ref[0])
    bt_im = tiled_rows(bt_ref[1])
    bb_re = z_re * bt_re - z_im * bt_im
    bb_im = z_re * bt_im + z_im * bt_re
    pw_re, pw_im = pow_rows(jnp.where(is_f, (t_len - 1.0) - t16, t16))
    wst_ref[:, 0:p2] = (pw_re * bb_re - pw_im * bb_im).astype(wst_ref.dtype)
    wst_ref[:, p2:2 * p2] = (pw_re * bb_im + pw_im * bb_re).astype(wst_ref.dtype)
    al_re, al_im = _cpow(float(t_len), lr, li, step)
    alpha_ref[0:1, :] = al_re
    alpha_ref[1:2, :] = al_im

    ct_re = tiled_rows(ct_ref[0])
    ct_im = tiled_rows(ct_ref[1])

    def c_times_pow(n):
        q_re, q_im = pow_rows(n)
        return jnp.transpose(ct_re * q_re - ct_im * q_im), jnp.transpose(ct_re * q_im + ct_im * q_re)

    ca_re, ca_im = c_times_pow(jnp.where(is_f, t16, jnp.where(t16 == 0.0, 0.0, t_len - t16)))
    lane_p = lax.broadcasted_iota(I32, (n_i, p2), 1)
    bbr = bb_re[0:n_i]
    bbi = bb_im[0:n_i]
    zero = jnp.zeros_like(bbr)
    strips = []
    for sel in (lane_p < S5_STATE, lane_p >= S5_STATE):
        strips.append(_mm3(jnp.where(sel, bbr, zero), ca_re) - _mm3(jnp.where(sel, bbi, zero), ca_im))
    strip_f, strip_b = strips
    t_k = lax.broadcasted_iota(I32, (n_i, ti), 1) // n_i
    for tt in range(t_len):
        sf = strip_f if tt == 0 else pltpu.roll(strip_f, tt * n_i, 1)
        sb = strip_b if tt == 0 else pltpu.roll(strip_b, tt * n_i, 1)
        blk = jnp.where(t_k >= tt, sf, 0.0) + jnp.where(t_k <= tt, sb, 0.0)
        kmat_ref[tt * n_i:(tt + 1) * n_i, :] = blk.astype(kmat_ref.dtype)

    co_re, co_im = c_times_pow(jnp.where(is_f, t16 + 1.0, t_len - t16))
    cexp_ref[0:p2, :] = co_re.astype(cexp_ref.dtype)
    cexp_ref[p2:2 * p2, :] = (-co_im).astype(cexp_ref.dtype)


def _s5_main_kernel(u_ref, kmat_ref, wst_ref, cexp_ref, alpha_ref, y_ref, x_ref, sf_ref, sb_ref,
                    *, n_batch, n_chunks):
    p2 = 2 * S5_STATE
    u = u_ref[...].astype(BF16)
    x_ref[...] = _dot(u, wst_ref[...])
    a_re = alpha_ref[0:1, :]
    a_im = alpha_ref[1:2, :]
    lane = lax.broadcasted_iota(I32, (1, p2), 1)
    is_f = lane < S5_STATE

    sub = S5_SCAN_ROWS
    assert n_chunks % sub == 0

    def step(k, carry):
        new = []
        for b in range(n_batch):
            s_re, s_im = carry[b]
            row_f = pl.multiple_of(b * n_chunks + sub * k, sub)
            row_b = pl.multiple_of(b * n_chunks + (n_chunks - sub) - sub * k, sub)
            xf = x_ref[pl.ds(row_f, sub), :]
            xb = x_ref[pl.ds(row_b, sub), :]
            seen = []
            for r in range(sub):
                seen.append(jnp.concatenate([s_re, s_im], axis=1))
                rb = sub - 1 - r
                x_re = jnp.where(is_f, xf[r:r + 1, 0:p2], xb[rb:rb + 1, 0:p2])
                x_im = jnp.where(is_f, xf[r:r + 1, p2:2 * p2], xb[rb:rb + 1, p2:2 * p2])
                s_re, s_im = a_re * s_re - a_im * s_im + x_re, a_re * s_im + a_im * s_re + x_im
            sf_ref[pl.ds(row_f, sub), :] = jnp.concatenate(seen, axis=0)
            sb_ref[pl.ds(row_b, sub), :] = jnp.concatenate(seen[::-1], axis=0)
            new.append((s_re, s_im))
        return tuple(new)

    zero = jnp.zeros((1, p2), F32)
    lax.fori_loop(0, n_chunks // sub, step, tuple((zero, zero) for _ in range(n_batch)))
    lane2 = lax.broadcasted_iota(I32, sf_ref.shape, 1) % p2
    s_in = jnp.where(lane2 < S5_STATE, sf_ref[...], sb_ref[...])
    s_hi, s_lo = _split_bf16(s_in)
    y_ref[...] = _dot(u, kmat_ref[...]) + _dot(s_hi, cexp_ref[...]) + _dot(s_lo, cexp_ref[...])


S5_RELAYOUT_CHUNKS = 128
S5_SCAN_ROWS = 8


def _s5_group_major_kernel(h_ref, g_ref, u_ref, hn_ref, ut_ref, *, seq_len):
    n_g, mt, ti = u_ref.shape
    t_len = S5_CHUNK
    n_i = ti // t_len
    n_lt = hn_ref.shape[0]
    g_lt = LANES // n_i
    rows = h_ref.shape[0]
    valid = seq_len - pl.program_id(1) * rows
    row = lax.broadcasted_iota(I32, h_ref.shape, 0)
    hn = jnp.where(row < valid, _rms(h_ref[...], g_ref[...]), 0.0)
    for j in range(n_lt):
        hn_ref[j] = hn[:, j * LANES:(j + 1) * LANES]
    for tau in range(t_len):
        for j in range(n_lt):
            xt = jnp.transpose(hn_ref[j, pl.ds(tau, mt, stride=t_len), :])
            ut_ref[j * g_lt:(j + 1) * g_lt, tau * n_i:(tau + 1) * n_i, :] = xt.reshape(g_lt, n_i, mt)
    for g in range(n_g):
        u_ref[g] = jnp.transpose(ut_ref[g]).astype(u_ref.dtype)


def _s5_token_major_kernel(y_ref, o_ref, zt_ref, z_ref):
    n_g, mt, ti = y_ref.shape
    t_len = S5_CHUNK
    n_i = ti // t_len
    n_lt = z_ref.shape[0]
    for g in range(n_g):
        yt = jnp.transpose(y_ref[g])
        zt_ref[:, g * n_i:(g + 1) * n_i, :] = yt.reshape(t_len, n_i, mt)
    for t in range(t_len):
        for j in range(n_lt):
            z_ref[j, pl.ds(t, mt, stride=t_len), :] = jnp.transpose(zt_ref[t, j * LANES:(j + 1) * LANES, :])
    for j in range(n_lt):
        o_ref[:, j * LANES:(j + 1) * LANES] = z_ref[j]


def _gelu_tanh(x):
    return 0.5 * x * (1.0 + jnp.tanh(math.sqrt(2.0 / math.pi) * (x + 0.044715 * (x * x * x))))


def _s5_glu_kernel(h_ref, y_ref, g_ref, d_ref, w_ref, o_ref):
    h = h_ref[...]
    dm = h.shape[1]
    y = y_ref[...] + d_ref[...] * _rms(h, g_ref[...])
    gl = _gelu_tanh(y).astype(BF16)
    a = _dot(gl, w_ref[:, 0:dm])
    b = _dot(gl, w_ref[:, dm:2 * dm])
    o_ref[...] = h + a * jax.nn.sigmoid(b)


def s5_mix(h3, gain, b_re, b_im, lam_re, lam_im, log_step, c_re, c_im, d_skip, w_glu):
    bsz, l, dm = h3.shape
    n_g, n_p, n_i = b_re.shape
    t_len = S5_CHUNK
    assert l % t_len == 0 and n_g * n_i == dm and n_p == S5_STATE and n_i == S5_GROUP_CH
    n_chunks = -(-(l // t_len) // S5_SCAN_ROWS) * S5_SCAN_ROWS
    m = bsz * n_chunks
    ti = t_len * n_i
    p2 = 2 * n_p
    n = bsz * l
    tm = _row_tile(n, 608)
    h2 = h3.reshape(n, dm)
    gain2 = gain.astype(F32).reshape(1, dm)

    mt = min(S5_RELAYOUT_CHUNKS, n_chunks)
    n_tiles = -(-n_chunks // mt)
    u = pl.pallas_call(
        functools.partial(_s5_group_major_kernel, seq_len=l),
        grid=(bsz, n_tiles),
        in_specs=[pl.BlockSpec((None, mt * t_len, dm), lambda b, i: (b, i, 0)),
                  pl.BlockSpec((1, dm), lambda b, i: (0, 0))],
        out_specs=pl.BlockSpec((n_g, None, mt, ti), lambda b, i: (0, b, i, 0)),
        out_shape=jax.ShapeDtypeStruct((n_g, bsz, n_chunks, ti), BF16),
        scratch_shapes=[pltpu.VMEM((dm // LANES, mt * t_len, LANES), F32), pltpu.VMEM((n_g, ti, mt), F32)],
        compiler_params=_cparams("parallel", "parallel"),
        name="s5_group_major",
    )(h3, gain2).reshape(n_g, m, ti)

    f32 = lambda a: a.astype(F32)
    lam_r = jnp.stack([jnp.concatenate([f32(lam_re)[0], f32(lam_re)[1]], axis=-1),
                       jnp.concatenate([f32(lam_im)[0], f32(lam_im)[1]], axis=-1)], axis=1)
    step_r = jnp.repeat(jnp.transpose(f32(log_step))[:, None, :], n_p, axis=2)
    bt = jnp.stack([jnp.transpose(f32(b_re), (0, 2, 1)), jnp.transpose(f32(b_im), (0, 2, 1))], axis=1)
    bt = jnp.tile(bt, (1, 1, 1, 2))
    ct = jnp.stack([f32(c_re), f32(c_im)], axis=0)
    ct = jnp.transpose(ct, (2, 0, 3, 1, 4)).reshape(n_g, 2, n_i, p2)

    gspec = lambda *shape: pl.BlockSpec((None,) + shape, lambda g: (g,) + (0,) * len(shape))
    kmat, wst, cexp, alpha = pl.pallas_call(
        _s5_param_kernel,
        grid=(n_g,),
        in_specs=[gspec(2, p2), gspec(1, p2), gspec(2, n_i, p2), gspec(2, n_i, p2)],
        out_specs=[gspec(ti, ti), gspec(ti, 2 * p2), gspec(2 * p2, ti), gspec(2, p2)],
        out_shape=[
            jax.ShapeDtypeStruct((n_g, ti, ti), BF16),
            jax.ShapeDtypeStruct((n_g, ti, 2 * p2), BF16),
            jax.ShapeDtypeStruct((n_g, 2 * p2, ti), BF16),
            jax.ShapeDtypeStruct((n_g, 2, p2), F32),
        ],
        compiler_params=_cparams("parallel"),
        name="s5_params",
    )(lam_r, step_r, bt, ct)

    y = pl.pallas_call(
        functools.partial(_s5_main_kernel, n_batch=bsz, n_chunks=n_chunks),
        grid=(n_g,),
        in_specs=[gspec(m, ti), gspec(ti, ti), gspec(ti, 2 * p2), gspec(2 * p2, ti), gspec(2, p2)],
        out_specs=gspec(m, ti),
        out_shape=jax.ShapeDtypeStruct((n_g, m, ti), F32),
        scratch_shapes=[pltpu.VMEM((m, 2 * p2), F32), pltpu.VMEM((m, 2 * p2), F32), pltpu.VMEM((m, 2 * p2), F32)],
        compiler_params=_cparams("parallel"),
        name="s5_main",
    )(u, kmat, wst, cexp, alpha)
    y2 = pl.pallas_call(
        _s5_token_major_kernel,
        grid=(bsz, n_tiles),
        in_specs=[pl.BlockSpec((n_g, None, mt, ti), lambda b, i: (0, b, i, 0))],
        out_specs=pl.BlockSpec((None, mt * t_len, dm), lambda b, i: (b, i, 0)),
        out_shape=jax.ShapeDtypeStruct((bsz, l, dm), F32),
        scratch_shapes=[pltpu.VMEM((t_len, dm, mt), F32), pltpu.VMEM((dm // LANES, mt * t_len, LANES), F32)],
        compiler_params=_cparams("parallel", "parallel"),
        name="s5_token_major",
    )(y.reshape(n_g, bsz, n_chunks, ti)).reshape(n, dm)

    out = pl.pallas_call(
        _s5_glu_kernel,
        grid=(n // tm,),
        in_specs=[
            pl.BlockSpec((tm, dm), lambda i: (i, 0)),
            pl.BlockSpec((tm, dm), lambda i: (i, 0)),
            pl.BlockSpec((1, dm), lambda i: (0, 0)),
            pl.BlockSpec((1, dm), lambda i: (0, 0)),
            pl.BlockSpec((dm, 2 * dm), lambda i: (0, 0)),
        ],
        out_specs=pl.BlockSpec((tm, dm), lambda i: (i, 0)),
        out_shape=jax.ShapeDtypeStruct((n, dm), F32),
        compiler_params=_cparams("parallel"),
        name="s5_glu",
    )(h2, y2, gain2, f32(d_skip).reshape(1, dm), w_glu.astype(BF16))
    return out.reshape(bsz, l, dm)


def na_rwkv_mix(h3, gain, w_in, w_out, rpb, mu, w0, w_up, a0, a_up, g_up, k_k, k_a, r_k, lnx_g, lnx_b):
    bsz, l, dm = h3.shape
    n = bsz * l
    h2 = h3.reshape(n, dm)
    n_qkv = 3 * (w_out.shape[0] // 2)
    qkv, rest = norm_inproj(h2, gain.astype(F32), w_in.astype(BF16), n_qkv)
    na = na_attention(qkv.reshape(bsz, l, n_qkv), rpb)
    rw = rwkv_mix(rest.reshape(bsz, l, -1), mu, w0, w_up, a0, a_up, g_up, k_k, k_a, r_k, lnx_g, lnx_b)
    out = outproj_residual(h2, na.reshape(n, -1), rw.reshape(n, -1), w_out.astype(BF16))
    return out.reshape(bsz, l, dm)


def kernel(x, meta_tokens, norm_mix, norm_ffn, norm_final, mix_w_in, mix_w_out, na_rpb, rwkv_mu,
           rwkv_w0, rwkv_w_up, rwkv_a0, rwkv_a_up, rwkv_g_up, rwkv_k_k, rwkv_k_a, rwkv_r_k,
           rwkv_lnx_g, rwkv_lnx_b, s5_b_re, s5_b_im, s5_lambda_re, s5_lambda_im, s5_log_step,
           s5_c_re, s5_c_im, s5_d, s5_w_glu, moe_w_group, moe_b_group, moe_w_expert, moe_b_expert,
           moe_w1, moe_w3, moe_w2):
    bsz, _, dm = x.shape
    depth = norm_mix.shape[0]
    meta = jnp.broadcast_to(meta_tokens.astype(x.dtype)[None], (bsz,) + meta_tokens.shape)
    h = jnp.concatenate([meta, x], axis=1)
    l = h.shape[1]
    for layer in range(depth):
        i = layer // 2
        if layer % 2 == 0:
            h = na_rwkv_mix(h, norm_mix[layer], mix_w_in[i], mix_w_out[i], na_rpb[i], rwkv_mu[i], rwkv_w0[i],
                            rwkv_w_up[i], rwkv_a0[i], rwkv_a_up[i], rwkv_g_up[i], rwkv_k_k[i], rwkv_k_a[i],
                            rwkv_r_k[i], rwkv_lnx_g[i], rwkv_lnx_b[i])
        else:
            h = s5_mix(h, norm_mix[layer], s5_b_re[i], s5_b_im[i], s5_lambda_re[i], s5_lambda_im[i],
                       s5_log_step[i], s5_c_re[i], s5_c_im[i], s5_d[i], s5_w_glu[i])
        h = hierarchical_moe_residual(h.reshape(bsz * l, dm), norm_ffn[layer].astype(F32), moe_w_group[layer],
                                      moe_b_group[layer], moe_w_expert[layer], moe_b_expert[layer],
                                      moe_w1, moe_w3, moe_w2, layer).reshape(bsz, l, dm)
    return final_norm(h, norm_final.astype(F32))
```

```python
import functools
import math

import jax
import jax.numpy as jnp
from jax import lax
from jax.experimental import pallas as pl
from jax.experimental.pallas import tpu as pltpu

F32 = jnp.float32
BF16 = jnp.bfloat16
I32 = jnp.int32

N_META = 16
GRID_W = 64
HEAD_DIM = 64
NA_WIN_ROWS = 8
NA_WIN_COLS = 16
S5_GROUP_CH = 16
S5_STATE = 64
MOE_GROUPS = 4
MOE_PER_GROUP = 8
MOE_EXPERTS = MOE_GROUPS * MOE_PER_GROUP
NORM_EPS = 1e-6
RWKV_GN_EPS = 64e-5
NEG_INF = -1e30

LANES = 128
SUBLANES_BF16 = 16
VMEM_LIMIT_BYTES = 56 * 1024 * 1024

MOE_TILE = 256
MOE_TOKEN_TILE = 320
ROUTER_LANES = 128


def _cparams(*sem):
    return pltpu.CompilerParams(dimension_semantics=sem, vmem_limit_bytes=VMEM_LIMIT_BYTES)


def _row_tile(n, target):
    best = None
    for t in range(SUBLANES_BF16, min(n, target) + 1, SUBLANES_BF16):
        if n % t == 0:
            best = t
    assert best is not None, (n, target)
    return best


def _rms(x, gain):
    ms = jnp.mean(x * x, axis=-1, keepdims=True)
    return (x * lax.rsqrt(ms + NORM_EPS)) * gain


def _split_bf16(x):
    hi = x.astype(BF16)
    lo = (x - hi.astype(F32)).astype(BF16)
    return hi, lo


def _dot(a, b):
    return jnp.dot(a, b, preferred_element_type=F32)


def _dot_nt(a, b):
    return lax.dot_general(a, b, (((1,), (1,)), ((), ())), preferred_element_type=F32)


def _norm_inproj_kernel(h_ref, g_ref, w_ref, qkv_ref, rest_ref, xn_ref, *, n_qkv, chunk):
    xn_ref[...] = _rms(h_ref[...], g_ref[...]).astype(BF16)
    n_all = w_ref.shape[1]
    for c in range(0, n_all, chunk):
        y = _dot(xn_ref[...], w_ref[:, c:c + chunk])
        if c < n_qkv:
            qkv_ref[:, c:c + chunk] = y.astype(BF16)
        else:
            rest_ref[:, c - n_qkv:c - n_qkv + chunk] = y


def norm_inproj(h2, gain, w_bf16, n_qkv):
    n, d = h2.shape
    n_all = w_bf16.shape[1]
    tm = _row_tile(n, 608)
    chunk = 256
    assert n_qkv % chunk == 0 and n_all % chunk == 0
    return pl.pallas_call(
        functools.partial(_norm_inproj_kernel, n_qkv=n_qkv, chunk=chunk),
        grid=(n // tm,),
        in_specs=[
            pl.BlockSpec((tm, d), lambda i: (i, 0)),
            pl.BlockSpec((1, d), lambda i: (0, 0)),
            pl.BlockSpec((d, n_all), lambda i: (0, 0)),
        ],
        out_specs=[
            pl.BlockSpec((tm, n_qkv), lambda i: (i, 0)),
            pl.BlockSpec((tm, n_all - n_qkv), lambda i: (i, 0)),
        ],
        out_shape=[
            jax.ShapeDtypeStruct((n, n_qkv), BF16),
            jax.ShapeDtypeStruct((n, n_all - n_qkv), F32),
        ],
        scratch_shapes=[pltpu.VMEM((tm, d), BF16)],
        compiler_params=_cparams("parallel"),
        name="norm_inproj",
    )(h2, gain.reshape(1, d), w_bf16)


def _outproj_kernel(h_ref, na_ref, rw_ref, wa_ref, wb_ref, o_ref):
    acc = _dot(na_ref[...], wa_ref[...])
    acc = acc + _dot(rw_ref[...], wb_ref[...])
    o_ref[...] = h_ref[...] + acc


def outproj_residual(h2, na, rw, w_out_bf16):
    n, d = h2.shape
    ka, kb = na.shape[1], rw.shape[1]
    tm = _row_tile(n, 608)
    return pl.pallas_call(
        _outproj_kernel,
        grid=(n // tm,),
        in_specs=[
            pl.BlockSpec((tm, d), lambda i: (i, 0)),
            pl.BlockSpec((tm, ka), lambda i: (i, 0)),
            pl.BlockSpec((tm, kb), lambda i: (i, 0)),
            pl.BlockSpec((ka, d), lambda i: (0, 0)),
            pl.BlockSpec((kb, d), lambda i: (0, 0)),
        ],
        out_specs=pl.BlockSpec((tm, d), lambda i: (i, 0)),
        out_shape=jax.ShapeDtypeStruct((n, d), F32),
        compiler_params=_cparams("parallel"),
        name="outproj_residual",
    )(h2, na, rw, w_out_bf16[:ka], w_out_bf16[ka:])


def _store_token_tiles(ref, x):
    rows = x.shape[0]
    s_n = x.shape[1] // LANES
    for s in range(s_n):
        ref[pl.ds(s, rows, stride=s_n), :] = x[:, s * LANES:(s + 1) * LANES]


def _load_token_tile_cols(ref, s, rows, s_n):
    return ref[pl.ds(s, rows, stride=s_n), :]


def _router_kernel(h_ref, g_ref, whi_ref, wlo_ref, b_ref, xn_ref, route_ref):
    xn = _rms(h_ref[...], g_ref[...])
    x_hi, x_lo = _split_bf16(xn)
    _store_token_tiles(xn_ref, xn)
    logits = (_dot(x_hi, whi_ref[...]) + _dot(x_hi, wlo_ref[...]) + _dot(x_lo, whi_ref[...])
              + b_ref[...])
    tm = logits.shape[0]
    lane = lax.broadcasted_iota(I32, (tm, ROUTER_LANES), 1)
    big = jnp.int32(ROUTER_LANES)

    is_g = lane < MOE_GROUPS
    lg = jnp.where(is_g, logits, -jnp.inf)
    eg = jnp.where(is_g, jnp.exp(lg - jnp.max(lg, axis=-1, keepdims=True)), 0.0)
    pg = eg / jnp.sum(eg, axis=-1, keepdims=True)
    p_grp = jnp.max(pg, axis=-1, keepdims=True)
    grp = jnp.min(jnp.where(is_g & (pg == p_grp), lane, big), axis=-1, keepdims=True)

    lo_lane = MOE_GROUPS + MOE_PER_GROUP * grp
    is_e = (lane >= lo_lane) & (lane < lo_lane + MOE_PER_GROUP)
    le = jnp.where(is_e, logits, -jnp.inf)
    ee = jnp.where(is_e, jnp.exp(le - jnp.max(le, axis=-1, keepdims=True)), 0.0)
    pe = jnp.where(is_e, ee / jnp.sum(ee, axis=-1, keepdims=True), -1.0)
    p1 = jnp.max(pe, axis=-1, keepdims=True)
    i1 = jnp.min(jnp.where(pe == p1, lane, big), axis=-1, keepdims=True)
    pe2 = jnp.where(lane == i1, -1.0, pe)
    p2 = jnp.max(pe2, axis=-1, keepdims=True)
    i2 = jnp.min(jnp.where(pe2 == p2, lane, big), axis=-1, keepdims=True)
    denom = p1 + p2
    g1 = p_grp * p1 / denom
    g2 = p_grp * p2 / denom
    e1 = (i1 - MOE_GROUPS).astype(F32)
    e2 = (i2 - MOE_GROUPS).astype(F32)
    route_ref[...] = jnp.where(lane == 0, e1, jnp.where(lane == 1, e2, jnp.where(lane == 2, g1, g2)))


def moe_router(h2, gain, w_group, b_group, w_expert, b_expert):
    n, d = h2.shape
    n_r = MOE_GROUPS + MOE_EXPERTS
    w_r = jnp.concatenate([w_group, jnp.transpose(w_expert, (1, 0, 2)).reshape(d, MOE_EXPERTS)], axis=1)
    w_r = jnp.pad(w_r.astype(F32), ((0, 0), (0, ROUTER_LANES - n_r)))
    w_hi, w_lo = _split_bf16(w_r)
    b_r = jnp.pad(jnp.concatenate([b_group, b_expert.reshape(-1)]).astype(F32), (0, ROUTER_LANES - n_r))
    tm = _row_tile(n, 608)
    return pl.pallas_call(
        _router_kernel,
        grid=(n // tm,),
        in_specs=[
            pl.BlockSpec((tm, d), lambda i: (i, 0)),
            pl.BlockSpec((1, d), lambda i: (0, 0)),
            pl.BlockSpec((d, ROUTER_LANES), lambda i: (0, 0)),
            pl.BlockSpec((d, ROUTER_LANES), lambda i: (0, 0)),
            pl.BlockSpec((1, ROUTER_LANES), lambda i: (0, 0)),
        ],
        out_specs=[
            pl.BlockSpec((tm * (d // LANES), LANES), lambda i: (i, 0)),
            pl.BlockSpec((tm, ROUTER_LANES), lambda i: (i, 0)),
        ],
        out_shape=[
            jax.ShapeDtypeStruct((n * (d // LANES), LANES), F32),
            jax.ShapeDtypeStruct((n, ROUTER_LANES), F32),
        ],
        compiler_params=_cparams("parallel"),
        name="moe_router",
    )(h2, gain.reshape(1, d), w_hi, w_lo, b_r.reshape(1, ROUTER_LANES))


def _moe_dispatch_kernel(tail_ref, n_used_ref, dst_ref, xn_hbm, xbuf_hbm, zero_ref, sem, zsem, *, tm, s_n, n_blocks):
    i = pl.program_id(0)
    n_steps = pl.num_programs(0)
    tile_rows = zero_ref.shape[0]

    def zero_block(b, carry):
        pltpu.make_async_copy(zero_ref, xbuf_hbm.at[pl.ds(pl.multiple_of(b * tile_rows, tile_rows), tile_rows), :],
                              zsem).start()
        return carry

    def wait_zero_block(b, carry):
        pltpu.make_async_copy(zero_ref, xbuf_hbm.at[pl.ds(0, tile_rows), :], zsem).wait()
        return carry

    def wait_tile():
        pltpu.make_async_copy(xn_hbm.at[pl.ds(0, 2 * tm * s_n), :], xbuf_hbm.at[pl.ds(0, 2 * tm * s_n), :], sem).wait()

    @pl.when(i == 0)
    def _():
        zero_ref[...] = jnp.zeros_like(zero_ref)
        for e in range(tail_ref.shape[0]):
            @pl.when(tail_ref[e] >= 0)
            def _():
                pltpu.make_async_copy(zero_ref, xbuf_hbm.at[pl.ds(pl.multiple_of(tail_ref[e], s_n), tile_rows), :],
                                      zsem).start()
        lax.fori_loop(n_used_ref[0], n_blocks, zero_block, 0)
        for e in range(tail_ref.shape[0]):
            @pl.when(tail_ref[e] >= 0)
            def _():
                wait_zero_block(0, 0)
        lax.fori_loop(n_used_ref[0], n_blocks, wait_zero_block, 0)

    @pl.when(i > 0)
    def _():
        wait_tile()

    for r in range(tm):
        src = xn_hbm.at[pl.ds(pl.multiple_of((i * tm + r) * s_n, s_n), s_n), :]
        for k in range(2):
            dst = pl.multiple_of(dst_ref[k, r], s_n)
            pltpu.make_async_copy(src, xbuf_hbm.at[pl.ds(dst, s_n), :], sem).start()

    @pl.when(i == n_steps - 1)
    def _():
        wait_tile()


def moe_dispatch(xn_tiles, dst_tiles, tail_start, n_used, n_blocks, tm, s_n):
    n_steps = dst_tiles.shape[0]
    n_rows = n_blocks * MOE_TILE
    assert n_steps * tm * s_n == xn_tiles.shape[0]
    grid_spec = pltpu.PrefetchScalarGridSpec(
        num_scalar_prefetch=2,
        grid=(n_steps,),
        in_specs=[
            pl.BlockSpec((None, 2, tm), lambda i, tail, nu: (i, 0, 0), memory_space=pltpu.SMEM),
            pl.BlockSpec(memory_space=pl.ANY),
        ],
        out_specs=pl.BlockSpec(memory_space=pl.ANY),
        scratch_shapes=[
            pltpu.VMEM((MOE_TILE * s_n, LANES), F32),
            pltpu.SemaphoreType.DMA(()),
            pltpu.SemaphoreType.DMA(()),
        ],
    )
    return pl.pallas_call(
        functools.partial(_moe_dispatch_kernel, tm=tm, s_n=s_n, n_blocks=n_blocks),
        grid_spec=grid_spec,
        out_shape=jax.ShapeDtypeStruct((n_rows * s_n, LANES), F32),
        compiler_params=_cparams("arbitrary"),
        name="moe_dispatch",
    )(tail_start, n_used, dst_tiles, xn_tiles)


def _expert_kernel(blk_e_ref, n_used_ref, x_ref, w1_ref, w3_ref, w2_ref, y_ref, xb_ref, w1b_ref, w3b_ref, w2b_ref):
    i = pl.program_id(0)
    used = i < n_used_ref[0]
    tile, d = xb_ref.shape
    s_n = d // LANES
    prev_e = blk_e_ref[jnp.maximum(i - 1, 0)]
    fresh = (i == 0) | (blk_e_ref[i] != prev_e)

    @pl.when(used & fresh)
    def _():
        w1b_ref[...] = w1_ref[...].astype(BF16)
        w3b_ref[...] = w3_ref[...].astype(BF16)
        w2b_ref[...] = w2_ref[...].astype(BF16)

    @pl.when(used)
    def _():
        for s in range(s_n):
            xb_ref[:, s * LANES:(s + 1) * LANES] = _load_token_tile_cols(x_ref, s, tile, s_n).astype(BF16)
        x = xb_ref[...]
        a = _dot(x, w1b_ref[...])
        b = _dot(x, w3b_ref[...])
        hmid = (a * jax.nn.sigmoid(a) * b).astype(BF16)
        _store_token_tiles(y_ref, _dot(hmid, w2b_ref[...]))

    @pl.when(jnp.logical_not(used))
    def _():
        y_ref[...] = jnp.zeros_like(y_ref)


def moe_experts(xbuf, blk_e, n_used, w1, w3, w2, layer, n_blocks):
    d, f = w1.shape[2], w1.shape[3]
    s_n = d // LANES
    tile = MOE_TILE

    def w_map(i, blk_e_ref, n_used_ref):
        return (layer, blk_e_ref[i], 0, 0)

    def x_map(i, blk_e_ref, n_used_ref):
        return (jnp.minimum(i, jnp.maximum(n_used_ref[0] - 1, 0)), 0)

    grid_spec = pltpu.PrefetchScalarGridSpec(
        num_scalar_prefetch=2,
        grid=(n_blocks,),
        in_specs=[
            pl.BlockSpec((tile * s_n, LANES), x_map),
            pl.BlockSpec((None, None, d, f), w_map),
            pl.BlockSpec((None, None, d, f), w_map),
            pl.BlockSpec((None, None, f, d), w_map),
        ],
        out_specs=pl.BlockSpec((tile * s_n, LANES), lambda i, be, nu: (i, 0)),
        scratch_shapes=[
            pltpu.VMEM((tile, d), BF16),
            pltpu.VMEM((d, f), BF16),
            pltpu.VMEM((d, f), BF16),
            pltpu.VMEM((f, d), BF16),
        ],
    )
    return pl.pallas_call(
        _expert_kernel,
        grid_spec=grid_spec,
        out_shape=jax.ShapeDtypeStruct((n_blocks * tile * s_n, LANES), F32),
        compiler_params=_cparams("arbitrary"),
        name="moe_experts",
    )(blk_e, n_used, xbuf, w1, w3, w2)


def _moe_combine_kernel(src_ref, src_next_ref, h_ref, route_ref, y_hbm, o_ref, yg_ref, sem, *, n_steps):
    i = pl.program_id(0)
    slot = i & 1
    tm, d = h_ref.shape
    s_n = d // LANES

    def start_gather(ids_ref, dst_slot):
        for k in range(2):
            for r in range(tm):
                src = pl.multiple_of(ids_ref[k, r], s_n)
                pltpu.make_async_copy(y_hbm.at[pl.ds(src, s_n), :],
                                      yg_ref.at[dst_slot, k, pl.ds(r * s_n, s_n), :], sem.at[dst_slot]).start()

    @pl.when(i == 0)
    def _():
        start_gather(src_ref, 0)

    @pl.when(i + 1 < n_steps)
    def _():
        start_gather(src_next_ref, 1 - slot)

    for k in range(2):
        pltpu.make_async_copy(y_hbm.at[pl.ds(0, tm * s_n), :], yg_ref.at[slot, k], sem.at[slot]).wait()
    route = route_ref[...]
    g1 = route[:, 2:3]
    g2 = route[:, 3:4]
    for s in range(s_n):
        cols = slice(s * LANES, (s + 1) * LANES)
        o_ref[:, cols] = (h_ref[:, cols] + g1 * _load_token_tile_cols(yg_ref.at[slot, 0], s, tm, s_n)
                          + g2 * _load_token_tile_cols(yg_ref.at[slot, 1], s, tm, s_n))


def moe_combine(h2, route, y_tiles, src_tiles, tm):
    n, d = h2.shape
    s_n = d // LANES
    n_steps = src_tiles.shape[0]
    ids = lambda index: pl.BlockSpec((None, 2, tm), lambda i: (index(i), 0, 0), memory_space=pltpu.SMEM)
    return pl.pallas_call(
        functools.partial(_moe_combine_kernel, n_steps=n_steps),
        grid=(n_steps,),
        in_specs=[
            ids(lambda i: i),
            ids(lambda i: jnp.minimum(i + 1, n_steps - 1)),
            pl.BlockSpec((tm, d), lambda i: (i, 0)),
            pl.BlockSpec((tm, ROUTER_LANES), lambda i: (i, 0)),
            pl.BlockSpec(memory_space=pl.ANY),
        ],
        out_specs=pl.BlockSpec((tm, d), lambda i: (i, 0)),
        out_shape=jax.ShapeDtypeStruct((n, d), F32),
        scratch_shapes=[pltpu.VMEM((2, 2, tm * s_n, LANES), F32), pltpu.SemaphoreType.DMA((2,))],
        compiler_params=_cparams("arbitrary"),
        name="moe_combine",
    )(src_tiles, src_tiles, h2, route, y_tiles)


def hierarchical_moe_residual(h2, gain, w_group, b_group, w_expert, b_expert, w1, w3, w2, layer):
    n, d = h2.shape
    xn, route = moe_router(h2, gain, w_group, b_group, w_expert, b_expert)
    e_km = jnp.concatenate([route[:, 0], route[:, 1]]).astype(I32)
    n_assign = 2 * n
    onehot = (e_km[:, None] == jnp.arange(MOE_EXPERTS, dtype=I32)[None, :]).astype(I32)
    csum = jnp.cumsum(onehot, axis=0)
    counts = csum[-1]
    padded = (counts + MOE_TILE - 1) // MOE_TILE * MOE_TILE
    pad_end = jnp.cumsum(padded)
    pad_start = pad_end - padded
    dest = jnp.sum((csum - onehot + pad_start[None, :]) * onehot, axis=1)
    n_blocks = -(-n_assign // MOE_TILE) + MOE_EXPERTS
    n_rows = n_blocks * MOE_TILE
    blk_start = jnp.arange(n_blocks, dtype=I32) * MOE_TILE
    blk_e = jnp.minimum(jnp.sum((pad_end[None, :] <= blk_start[:, None]).astype(I32), axis=1),
                        MOE_EXPERTS - 1).astype(I32)
    n_used = (pad_end[-1] // MOE_TILE).astype(I32).reshape(1)
    s_n = d // LANES
    tm = _row_tile(n, MOE_TOKEN_TILE)
    dest_tiles = jnp.transpose((dest * s_n).astype(I32).reshape(2, n // tm, tm), (1, 0, 2))
    tail_start = jnp.where(counts > 0, (pad_end - MOE_TILE) * s_n, -1).astype(I32)
    xbuf = moe_dispatch(xn, dest_tiles, tail_start, n_used, n_blocks, tm, s_n)
    y = moe_experts(xbuf, blk_e, n_used, w1, w3, w2, layer, n_blocks)
    return moe_combine(h2, route, y, dest_tiles, tm)


def _final_norm_kernel(h_ref, g_ref, o_ref):
    o_ref[...] = _rms(h_ref[...], g_ref[...])


def final_norm(h3, gain):
    b, l, d = h3.shape
    t = l - N_META
    tm = _row_tile(t, 512)
    return pl.pallas_call(
        _final_norm_kernel,
        grid=(b, t // tm),
        in_specs=[
            pl.BlockSpec((None, pl.Element(tm), pl.Element(d)),
                         lambda bi, i: (bi, pl.multiple_of(N_META + i * tm, SUBLANES_BF16), 0)),
            pl.BlockSpec((1, d), lambda bi, i: (0, 0)),
        ],
        out_specs=pl.BlockSpec((None, tm, d), lambda bi, i: (bi, i, 0)),
        out_shape=jax.ShapeDtypeStruct((b, t, d), F32),
        compiler_params=_cparams("parallel", "parallel"),
        name="final_norm",
    )(h3, gain.reshape(1, d))


NA_QROWS = 8
NA_KROWS = 3 * NA_QROWS
NA_ROWS_PER_ITER = 4


def _na_kernel(q_ref, kw_ref, vw_ref, qm_ref, km_ref, vm_ref, bias_ref, o_ref, om_ref, *, rows, scale):
    blk = pl.program_id(1)
    tq = GRID_W
    n_pairs = q_ref.shape[1] // LANES
    base = jnp.clip(NA_QROWS * blk - NA_QROWS, 0, rows - NA_KROWS)
    lane = lax.broadcasted_iota(I32, (tq, LANES), 1)
    halves = [lane < HEAD_DIM, lane >= HEAD_DIM]

    pad = jnp.zeros((LANES - N_META, LANES), km_ref.dtype)
    k_meta = [jnp.concatenate([km_ref[:, p * LANES:(p + 1) * LANES], pad], axis=0) for p in range(n_pairs)]
    v_meta = [jnp.concatenate([vm_ref[:, p * LANES:(p + 1) * LANES], pad], axis=0) for p in range(n_pairs)]
    meta_bias = jnp.where(lane < N_META, 0.0, NEG_INF)

    n_win = NA_WIN_ROWS * GRID_W

    def row_body(jb, carry):
        colsl = [slice(p * LANES, (p + 1) * LANES) for p in range(n_pairs)]
        units = [(jj, p, hh) for jj in range(NA_ROWS_PER_ITER) for p in range(n_pairs) for hh in range(2)]
        s_idx, koff, qoff = [], [], []
        for jj in range(NA_ROWS_PER_ITER):
            j = jb * NA_ROWS_PER_ITER + jj
            r = NA_QROWS * blk + j
            start = jnp.clip(r - NA_WIN_ROWS // 2, 0, rows - NA_WIN_ROWS)
            s_idx.append(start - r + (NA_WIN_ROWS - 1))
            koff.append(pl.multiple_of((start - base) * GRID_W, GRID_W))
            qoff.append(pl.multiple_of(j * tq, tq))
        q_pair = {(jj, p): q_ref[pl.ds(qoff[jj], tq), c]
                  for jj in range(NA_ROWS_PER_ITER) for p, c in enumerate(colsl)}
        k_ext = {(jj, p): jnp.concatenate([kw_ref[pl.ds(koff[jj], n_win), c], k_meta[p]], axis=0)
                 for jj in range(NA_ROWS_PER_ITER) for p, c in enumerate(colsl)}
        v_ext = {(jj, p): jnp.concatenate([vw_ref[pl.ds(koff[jj], n_win), c], v_meta[p]], axis=0)
                 for jj in range(NA_ROWS_PER_ITER) for p, c in enumerate(colsl)}
        qh = [jnp.where(halves[hh], q_pair[jj, p], jnp.zeros_like(q_pair[jj, p])) for jj, p, hh in units]
        s = [_dot_nt(qh[u], k_ext[jj, p]) * scale
             + jnp.concatenate([bias_ref[2 * p + hh, s_idx[jj]], meta_bias], axis=1)
             for u, (jj, p, hh) in enumerate(units)]
        m = [jnp.max(x, axis=-1, keepdims=True) for x in s]
        e = [jnp.exp(x - mx) for x, mx in zip(s, m)]
        den = [jnp.sum(x, axis=-1, keepdims=True) for x in e]
        o = [_dot(e[u].astype(BF16), v_ext[jj, p]) / den[u] for u, (jj, p, hh) in enumerate(units)]
        for u in range(0, len(units), 2):
            jj, p, _ = units[u]
            o_ref[pl.ds(qoff[jj], tq), colsl[p]] = jnp.where(halves[0], o[u], o[u + 1]).astype(o_ref.dtype)
        return carry

    lax.fori_loop(0, NA_QROWS // NA_ROWS_PER_ITER, row_body, 0)

    @pl.when(blk == 0)
    def _():
        lane_m = lax.broadcasted_iota(I32, (N_META, LANES), 1)
        for p in range(n_pairs):
            cols = slice(p * LANES, (p + 1) * LANES)
            q_pair = qm_ref[:, cols]
            kmp = km_ref[:, cols]
            vmp = vm_ref[:, cols]
            outs = []
            for hh in range(2):
                sel = (lane_m < HEAD_DIM) if hh == 0 else (lane_m >= HEAD_DIM)
                qp = jnp.where(sel, q_pair, jnp.zeros_like(q_pair))
                s_m = _dot_nt(qp, kmp) * scale
                p_m = jnp.exp(s_m - jnp.max(s_m, axis=-1, keepdims=True))
                den = jnp.sum(p_m, axis=-1, keepdims=True)
                outs.append(_dot(p_m.astype(BF16), vmp) / den)
            om_ref[:, cols] = jnp.where(lane_m < HEAD_DIM, outs[0], outs[1]).astype(om_ref.dtype)


def _na_bias_table(rpb):
    h = rpb.shape[0]
    c_ids = jnp.arange(GRID_W)
    c_start = jnp.clip(c_ids - NA_WIN_COLS // 2, 0, GRID_W - NA_WIN_COLS)
    in_band = (c_ids[None, :] >= c_start[:, None]) & (c_ids[None, :] < c_start[:, None] + NA_WIN_COLS)
    dc = jnp.clip(c_ids[None, :] - c_ids[:, None] + NA_WIN_COLS - 1, 0, 2 * NA_WIN_COLS - 2)
    tab = jnp.where(in_band[None, None], rpb.astype(F32)[:, :, dc], NEG_INF)
    win = jnp.stack([tab[:, s:s + NA_WIN_ROWS] for s in range(NA_WIN_ROWS)], axis=1)
    return jnp.transpose(win, (0, 1, 3, 2, 4)).reshape(h, NA_WIN_ROWS, GRID_W, NA_WIN_ROWS * GRID_W)


def na_attention(qkv, rpb):
    b, l, w3 = qkv.shape
    w = w3 // 3
    t = l - N_META
    rows = t // GRID_W
    assert rows * GRID_W == t and rows % NA_QROWS == 0 and rows >= NA_KROWS
    tq = NA_QROWS * GRID_W
    tk = NA_KROWS * GRID_W
    bias = _na_bias_table(rpb)
    al = SUBLANES_BF16

    def q_map(bi, i):
        return (bi, pl.multiple_of(N_META + i * tq, al), 0)

    def kv_map(col):
        def f(bi, i):
            base = jnp.clip(NA_QROWS * i - NA_QROWS, 0, rows - NA_KROWS)
            return (bi, pl.multiple_of(N_META + base * GRID_W, al), col)
        return f

    def meta_map(col):
        return lambda bi, i: (bi, 0, col)

    el = pl.Element
    grid_out, meta_out = pl.pallas_call(
        functools.partial(_na_kernel, rows=rows, scale=HEAD_DIM ** -0.5),
        grid=(b, rows // NA_QROWS),
        in_specs=[
            pl.BlockSpec((None, el(tq), el(w)), q_map),
            pl.BlockSpec((None, el(tk), el(w)), kv_map(w)),
            pl.BlockSpec((None, el(tk), el(w)), kv_map(2 * w)),
            pl.BlockSpec((None, el(N_META), el(w)), meta_map(0)),
            pl.BlockSpec((None, el(N_META), el(w)), meta_map(w)),
            pl.BlockSpec((None, el(N_META), el(w)), meta_map(2 * w)),
            pl.BlockSpec(bias.shape, lambda bi, i: (0, 0, 0, 0)),
        ],
        out_specs=[
            pl.BlockSpec((None, tq, w), lambda bi, i: (bi, i, 0)),
            pl.BlockSpec((None, N_META, w), lambda bi, i: (bi, 0, 0)),
        ],
        out_shape=[
            jax.ShapeDtypeStruct((b, t, w), BF16),
            jax.ShapeDtypeStruct((b, N_META, w), BF16),
        ],
        compiler_params=_cparams("parallel", "arbitrary"),
        name="na_attention",
    )(qkv, qkv, qkv, qkv, qkv, qkv, bias)
    return jnp.concatenate([meta_out, grid_out], axis=1)


RWKV_CHUNK = 64
RWKV_TILE_CHUNKS = 2
RWKV_HALO = 8


def _split3_bf16(x):
    p1 = x.astype(BF16)
    r1 = x - p1.astype(F32)
    p2 = r1.astype(BF16)
    p3 = (r1 - p2.astype(F32)).astype(BF16)
    return p1, p2, p3


def _mm1(a, b):
    return _dot(a.astype(BF16), b.astype(BF16))


def _mm3(a, b):
    ah, al = _split_bf16(a)
    bh, bl = _split_bf16(b)
    return _dot(ah, bh) + _dot(ah, bl) + _dot(al, bh)


def _mm1_nt(a, b):
    return _dot_nt(a.astype(BF16), b.astype(BF16))


def _mm3_nt(a, b):
    ah, al = _split_bf16(a)
    bh, bl = _split_bf16(b)
    return _dot_nt(ah, bh) + _dot_nt(ah, bl) + _dot_nt(al, bh)


def _exact_left(mat_bf16, x):
    p1, p2, p3 = _split3_bf16(x)
    return _dot(mat_bf16, p1) + _dot(mat_bf16, p2) + _dot(mat_bf16, p3)


def _exact_right(x, mat_bf16):
    p1, p2, p3 = _split3_bf16(x)
    return _dot(p1, mat_bf16) + _dot(p2, mat_bf16) + _dot(p3, mat_bf16)


def _head_block_ones(width):
    ri = lax.broadcasted_iota(I32, (width, width), 0) // HEAD_DIM
    ci = lax.broadcasted_iota(I32, (width, width), 1) // HEAD_DIM
    return (ri == ci).astype(BF16)


def _head_sums(x, exact):
    ones_pair = _head_block_ones(LANES)
    tiles = []
    for p in range(x.shape[1] // LANES):
        xt = x[:, p * LANES:(p + 1) * LANES]
        tiles.append(_exact_right(xt, ones_pair) if exact else _dot(xt.astype(BF16), ones_pair))
    return jnp.concatenate(tiles, axis=1)


def _stack_heads(x, m0):
    z = jnp.zeros_like(x)
    return jnp.concatenate([jnp.where(m0, x, z), jnp.where(m0, z, x)], axis=0)


_MM_L4 = _mm1_nt
_MM_KT = _mm1_nt
_MM_SQ = _mm1
_MM_AP = _mm1
_MM_V = _mm1
_MM_Y = _mm1
_MM_UPD = _mm1


def _rwkv_chunk_maps(streams, c, n_sub):
    assert c == 64
    c2 = 2 * c
    lane = lax.broadcasted_iota(I32, (c, LANES), 1)
    m0 = lane < HEAD_DIM
    r_i = lax.broadcasted_iota(I32, (c2, c2), 0)
    c_i = lax.broadcasted_iota(I32, (c2, c2), 1)
    eye = (r_i == c_i).astype(F32)
    rel = r_i % c - c_i % c
    masks = {sg: (rel * sg > 0, rel * sg >= 0) for sg in {s["sign"] for s in streams}}
    items = [(j, q) for j in range(len(streams)) for q in range(n_sub)]

    def part(j, q, name):
        return _stack_heads(streams[j][name][q * c:(q + 1) * c], m0)

    lhs = {it: jnp.concatenate([part(*it, "kkp"), part(*it, "rp")], axis=0) for it in items}
    rhs = {it: jnp.concatenate([part(*it, "ki"), part(*it, "bi")], axis=0) for it in items}
    vs = {it: part(*it, "v") for it in items}
    kipcs = {it: part(*it, "kipc") for it in items}
    bipcs = {it: part(*it, "bipc") for it in items}
    l4 = {it: _MM_L4(lhs[it], rhs[it]) for it in items}
    m_kk, n1, m_rk, m_rb = {}, {}, {}, {}
    for it in items:
        strict, incl = masks[streams[it[0]]["sign"]]
        m = l4[it]
        m_kk[it] = jnp.where(strict, m[0:c2, 0:c2], 0.0)
        n1[it] = jnp.where(strict, m[0:c2, c2:2 * c2], 0.0)
        m_rk[it] = jnp.where(incl, m[c2:2 * c2, 0:c2], 0.0)
        m_rb[it] = jnp.where(incl, m[c2:2 * c2, c2:2 * c2], 0.0)
    n2 = {it: _MM_SQ(n1[it], n1[it]) for it in items}
    n4 = {it: _MM_SQ(n2[it], n2[it]) for it in items}
    n8 = {it: _MM_SQ(n4[it], n4[it]) for it in items}
    n16 = {it: _MM_SQ(n8[it], n8[it]) for it in items}
    n32 = {it: _MM_SQ(n16[it], n16[it]) for it in items}
    p1 = {it: (eye - n1[it]) + _MM_AP(eye - n1[it], n2[it]) for it in items}
    p2 = {it: eye + n4[it] + n8[it] + _MM_AP(n4[it], n8[it]) for it in items}
    p3 = {it: eye + n16[it] + n32[it] + _MM_AP(n16[it], n32[it]) for it in items}
    p23 = {it: _MM_AP(p2[it], p3[it]) for it in items}
    winv = {it: _MM_AP(p1[it], p23[it]) for it in items}
    mv = {it: _MM_V(m_kk[it], vs[it]) for it in items}
    mrv = {it: _MM_Y(m_rk[it], vs[it]) for it in items}
    wl = {it: _MM_AP(winv[it], lhs[it][0:c2]) for it in items}
    wmv = {it: _MM_AP(winv[it], mv[it]) for it in items}
    yl = {it: lhs[it][c2:2 * c2] - _MM_Y(m_rb[it], wl[it]) for it in items}
    y0 = {it: mrv[it] - _MM_Y(m_rb[it], wmv[it]) for it in items}
    g2 = {it: _MM_UPD(jnp.transpose(wl[it]), bipcs[it]) for it in items}
    hh = {it: _MM_UPD(jnp.transpose(jnp.concatenate([vs[it], -wmv[it]], axis=0)),
                      jnp.concatenate([kipcs[it], bipcs[it]], axis=0)) for it in items}
    return {it: (yl[it], y0[it], g2[it], hh[it]) for it in items}


def _rwkv_apply_maps(streams, maps, c, n_sub):
    c2 = 2 * c
    st = [s["st"] for s in streams]
    ys = {}
    for k in range(n_sub):
        cur = [(j, k if s["sign"] > 0 else n_sub - 1 - k) for j, s in enumerate(streams)]
        sg = [_MM_UPD(st[j], maps[j][q][2]) for j, q in cur]
        yk = [_MM_KT(maps[j][q][0], st[j]) + maps[j][q][1] for j, q in cur]
        for it, yi in zip(cur, yk):
            ys[it] = yi[0:c] + yi[c:c2]
        st = [st[j] * streams[j]["pc"][q] - sgi + maps[j][q][3] for (j, q), sgi in zip(cur, sg)]
    return [(jnp.concatenate([ys[j, q] for q in range(n_sub)], axis=0), st[j]) for j in range(len(streams))]


def _softplus(z):
    return jnp.maximum(z, 0.0) + jnp.log(1.0 + jnp.exp(-jnp.abs(z)))


def _rwkv_tile_prep(x_ref, xp_ref, xn_ref, tile, n_tiles, seq_len, width, sign, mu, w0, a0, w_wa,
                    k_k, k_a, r_k):
    cs = RWKV_CHUNK
    c = x_ref.shape[0]
    valid = jnp.minimum(c, seq_len - tile * c)
    row = lax.broadcasted_iota(I32, (c, LANES), 0)
    rowv = row < valid
    lane = lax.broadcasted_iota(I32, (c, LANES), 1)
    ones_pair = _head_block_ones(LANES)

    def shifted(lo):
        cols = slice(lo, lo + LANES)
        x = jnp.where(rowv, x_ref[:, cols], 0.0)
        prev_row = jnp.where(tile > 0, xp_ref[RWKV_HALO - 1:RWKV_HALO, cols], 0.0)
        next_row = jnp.where(tile < n_tiles - 1, xn_ref[0:1, cols], 0.0)
        x_prev = jnp.where(row == 0, prev_row, pltpu.roll(x, 1, 0))
        x_next = jnp.where(row == c - 1, next_row, pltpu.roll(x, c - 1, 0))
        xs = x + mu[:, cols] * (0.5 * (x_prev + x_next) - x)
        return jnp.where(rowv, xs, 0.0)

    wa = shifted(3 * width)
    g_lo = shifted(3 * width + LANES)
    xwa = jnp.where(lane < LANES // 2, jnp.tanh(wa), wa)
    la = _dot(xwa.astype(BF16), w_wa)

    t_i = lax.broadcasted_iota(I32, (c, c), 0)
    s_i = lax.broadcasted_iota(I32, (c, c), 1)
    tri = ((t_i // cs == s_i // cs) & ((t_i - s_i) * sign >= 0)).astype(BF16)

    pairs = []
    for p in range(width // LANES):
        lo = p * LANES
        cols = slice(lo, lo + LANES)
        r = shifted(lo)
        k = shifted(width + lo)
        v = shifted(2 * width + lo)
        w_log = -_softplus(-(w0[:, cols] + la[:, cols])) - 0.5
        logw = jnp.where(rowv, -jnp.exp(w_log), 0.0)
        a = jax.nn.sigmoid(a0[:, cols] + la[:, width + lo:width + lo + LANES])
        kk0 = k * k_k[:, cols]
        ss = _dot((kk0 * kk0).astype(BF16), ones_pair)
        kk = kk0 / jnp.maximum(jnp.sqrt(ss), 1e-12)
        kdir = k * (1.0 + (a - 1.0) * k_a[:, cols])
        b = kk * a
        cl = _exact_left(tri, logw)
        lasts = [cl[q * cs + cs - 1:q * cs + cs, :] if sign > 0 else cl[q * cs:q * cs + 1, :]
                 for q in range(c // cs)]
        last = jnp.concatenate([jnp.broadcast_to(lq, (cs, LANES)) for lq in lasts], axis=0)
        e_n = jnp.exp(-cl)
        pcr = jnp.exp(last - cl)
        pairs.append(dict(kkp=kk * jnp.exp(cl - logw), rp=r * jnp.exp(cl), ki=kdir * e_n, bi=b * e_n,
                          kipc=kdir * pcr, bipc=b * pcr, v=v, pc=[jnp.exp(lq) for lq in lasts],
                          bonus=_dot((r * kdir * r_k[:, cols]).astype(BF16), ones_pair) * v))
    return pairs, g_lo


def _rwkv_scan_kernel(xf_ref, xfp_ref, xfn_ref, xb_ref, xbp_ref, xbn_ref, mu_ref, w0_ref, a0_ref, wwa_ref,
                      gup_ref, kk_ref, ka_ref, rk_ref, yf_ref, yb_ref, bonf_ref, bonb_ref, g_ref, st_ref,
                      *, seq_len, width):
    i = pl.program_id(1)
    n_chunks = pl.num_programs(1)
    n_pairs = width // LANES

    @pl.when(i == 0)
    def _():
        st_ref[...] = jnp.zeros_like(st_ref)

    common = (mu_ref[...],)
    tail = (kk_ref[...], ka_ref[...], rk_ref[...])
    n_sub = xf_ref.shape[0] // RWKV_CHUNK
    fwd, g_lo = _rwkv_tile_prep(xf_ref, xfp_ref, xfn_ref, i, n_chunks, seq_len, width, 1, *common,
                                w0_ref[0], a0_ref[0], wwa_ref[0], *tail)
    bwd, _ = _rwkv_tile_prep(xb_ref, xbp_ref, xbn_ref, n_chunks - 1 - i, n_chunks, seq_len, width, -1, *common,
                             w0_ref[1], a0_ref[1], wwa_ref[1], *tail)
    g_ref[...] = _mm1(jax.nn.sigmoid(g_lo), gup_ref[...]).astype(g_ref.dtype)
    streams = []
    for di, (pairs, sign, bon_ref) in enumerate(((fwd, 1, bonf_ref), (bwd, -1, bonb_ref))):
        for p, s in enumerate(pairs):
            bon_ref[:, p * LANES:(p + 1) * LANES] = s.pop("bonus")
            s["st"] = st_ref[di * n_pairs + p]
            s["sign"] = sign
            streams.append(s)
    m = _rwkv_chunk_maps(streams, RWKV_CHUNK, n_sub)
    maps = [{q: m[j, q] for q in range(n_sub)} for j in range(len(streams))]
    res = _rwkv_apply_maps(streams, maps, RWKV_CHUNK, n_sub)
    for j, (y, st_new) in enumerate(res):
        di, p = divmod(j, n_pairs)
        cols = slice(p * LANES, (p + 1) * LANES)
        (yf_ref if di == 0 else yb_ref)[:, cols] = y
        st_ref[j] = st_new


def _rwkv_finish_kernel(yf_ref, yb_ref, bonf_ref, bonb_ref, g_ref, lg_ref, lb_ref, o_ref):
    y = yf_ref[...] + yb_ref[...]
    mean = _head_sums(y, exact=True) * (1.0 / HEAD_DIM)
    yc = y - mean
    var = _head_sums(yc * yc, exact=True) * (1.0 / HEAD_DIM)
    yn = yc * lax.rsqrt(var + RWKV_GN_EPS) * lg_ref[...] + lb_ref[...]
    o_ref[...] = ((yn + bonf_ref[...] + bonb_ref[...]) * g_ref[...].astype(F32)).astype(o_ref.dtype)


def rwkv_mix(rest, mu, w0, w_up, a0, a_up, g_up, k_k, k_a, r_k, lnx_g, lnx_b):
    bsz, l, n_cols = rest.shape
    width = w0.shape[1]
    rank = w_up.shape[1]
    assert n_cols == 3 * width + 2 * LANES and 2 * rank == LANES and l % RWKV_HALO == 0
    c = RWKV_CHUNK * RWKV_TILE_CHUNKS
    n_chunks = -(-l // c)
    per = c // RWKV_HALO
    n_halo = l // RWKV_HALO
    zeros = jnp.zeros((2, rank, width), F32)
    w_wa = jnp.concatenate([jnp.concatenate([w_up.astype(F32), zeros], axis=2),
                            jnp.concatenate([zeros, a_up.astype(F32)], axis=2)], axis=1)
    w_wa = w_wa.astype(BF16)

    fwd_chunk = lambda i: i
    bwd_chunk = lambda i: n_chunks - 1 - i

    def tile_specs(chunk_of):
        return [
            pl.BlockSpec((None, c, n_cols), lambda b, i: (b, chunk_of(i), 0)),
            pl.BlockSpec((None, RWKV_HALO, n_cols), lambda b, i: (b, jnp.maximum(chunk_of(i) * per - 1, 0), 0)),
            pl.BlockSpec((None, RWKV_HALO, n_cols),
                         lambda b, i: (b, jnp.minimum((chunk_of(i) + 1) * per, n_halo - 1), 0)),
        ]

    row2 = lambda a: a.astype(F32).reshape(1, -1)
    whole = lambda *shape: pl.BlockSpec(shape, lambda b, i: (0,) * len(shape))
    out_spec = lambda chunk_of: pl.BlockSpec((None, c, width), lambda b, i: (b, chunk_of(i), 0))
    act = lambda dt: jax.ShapeDtypeStruct((bsz, l, width), dt)
    y_f, y_b, bon_f, bon_b, g = pl.pallas_call(
        functools.partial(_rwkv_scan_kernel, seq_len=l, width=width),
        grid=(bsz, n_chunks),
        in_specs=tile_specs(fwd_chunk) + tile_specs(bwd_chunk) + [
            whole(1, n_cols),
            whole(2, 1, width), whole(2, 1, width),
            whole(2, LANES, 2 * width),
            whole(LANES, width),
            whole(1, width), whole(1, width), whole(1, width),
        ],
        out_specs=[out_spec(fwd_chunk), out_spec(bwd_chunk), out_spec(fwd_chunk), out_spec(bwd_chunk),
                   out_spec(fwd_chunk)],
        out_shape=[act(F32), act(F32), act(F32), act(F32), act(BF16)],
        scratch_shapes=[pltpu.VMEM((2 * (width // LANES), LANES, LANES), F32)],
        compiler_params=_cparams("parallel", "arbitrary"),
        name="rwkv_scan",
    )(rest, rest, rest, rest, rest, rest, row2(mu), w0.astype(F32).reshape(2, 1, width),
      a0.astype(F32).reshape(2, 1, width), w_wa, g_up.astype(BF16), row2(k_k), row2(k_a), row2(r_k))

    n = bsz * l
    tm = _row_tile(n, 608)
    rows = lambda: pl.BlockSpec((tm, width), lambda j: (j, 0))
    flat = lambda a: a.reshape(n, width)
    return pl.pallas_call(
        _rwkv_finish_kernel,
        grid=(n // tm,),
        in_specs=[rows(), rows(), rows(), rows(), rows(),
                  pl.BlockSpec((1, width), lambda j: (0, 0)),
                  pl.BlockSpec((1, width), lambda j: (0, 0))],
        out_specs=rows(),
        out_shape=jax.ShapeDtypeStruct((n, width), BF16),
        compiler_params=_cparams("parallel"),
        name="rwkv_finish",
    )(flat(y_f), flat(y_b), flat(bon_f), flat(bon_b), flat(g), row2(lnx_g), row2(lnx_b)).reshape(bsz, l, width)


S5_CHUNK = 16


def _cpow(n, lr, li, step):
    mag = jnp.exp(n * (lr * step))
    ang = n * (li * step)
    return mag * jnp.cos(ang), mag * jnp.sin(ang)


def _s5_param_kernel(lamr_ref, stepr_ref, bt_ref, ct_ref, kmat_ref, wst_ref, cexp_ref, alpha_ref):
    t_len = S5_CHUNK
    n_i = S5_GROUP_CH
    p2 = 2 * S5_STATE
    ti = t_len * n_i

    lr = lamr_ref[0:1, :]
    li = lamr_ref[1:2, :]
    step = jnp.exp(stepr_ref[...])
    ab_re, ab_im = _cpow(1.0, lr, li, step)
    den = lr * lr + li * li
    z_re = ((ab_re - 1.0) * lr + ab_im * li) / den
    z_im = (ab_im * lr - (ab_re - 1.0) * li) / den
    t16 = lax.broadcasted_iota(I32, (t_len, p2), 0).astype(F32)
    is_f = lax.broadcasted_iota(I32, (t_len, p2), 1) < S5_STATE

    def rows_by_t(x):
        return jnp.concatenate([jnp.broadcast_to(x[t:t + 1], (n_i, p2)) for t in range(t_len)], axis=0)

    def tiled_rows(x):
        return jnp.concatenate([x] * t_len, axis=0)

    def pow_rows(n):
        q_re, q_im = _cpow(n, lr, li, step)
        return rows_by_t(q_re), rows_by_t(q_im)

    bt_re = tiled_rows(bt_ref[0])
    bt_im = tiled_rows(bt_ref[1])
    bb_re = z_re * bt_re - z_im * bt_im
    bb_im = z_re * bt_im + z_im * bt_re
    pw_re, pw_im = pow_rows(jnp.where(is_f, (t_len - 1.0) - t16, t16))
    wst_ref[:, 0:p2] = (pw_re * bb_re - pw_im * bb_im).astype(wst_ref.dtype)
    wst_ref[:, p2:2 * p2] = (pw_re * bb_im + pw_im * bb_re).astype(wst_ref.dtype)
    al_re, al_im = _cpow(float(t_len), lr, li, step)
    alpha_ref[0:1, :] = al_re
    alpha_ref[1:2, :] = al_im

    ct_re = tiled_rows(ct_ref[0])
    ct_im = tiled_rows(ct_ref[1])

    def c_times_pow(n):
        q_re, q_im = pow_rows(n)
        return jnp.transpose(ct_re * q_re - ct_im * q_im), jnp.transpose(ct_re * q_im + ct_im * q_re)

    ca_re, ca_im = c_times_pow(jnp.where(is_f, t16, jnp.where(t16 == 0.0, 0.0, t_len - t16)))
    lane_p = lax.broadcasted_iota(I32, (n_i, p2), 1)
    bbr = bb_re[0:n_i]
    bbi = bb_im[0:n_i]
    zero = jnp.zeros_like(bbr)
    strips = []
    for sel in (lane_p < S5_STATE, lane_p >= S5_STATE):
        strips.append(_mm3(jnp.where(sel, bbr, zero), ca_re) - _mm3(jnp.where(sel, bbi, zero), ca_im))
    strip_f, strip_b = strips
    t_k = lax.broadcasted_iota(I32, (n_i, ti), 1) // n_i
    for tt in range(t_len):
        sf = strip_f if tt == 0 else pltpu.roll(strip_f, tt * n_i, 1)
        sb = strip_b if tt == 0 else pltpu.roll(strip_b, tt * n_i, 1)
        blk = jnp.where(t_k >= tt, sf, 0.0) + jnp.where(t_k <= tt, sb, 0.0)
        kmat_ref[tt * n_i:(tt + 1) * n_i, :] = blk.astype(kmat_ref.dtype)

    co_re, co_im = c_times_pow(jnp.where(is_f, t16 + 1.0, t_len - t16))
    cexp_ref[0:p2, :] = co_re.astype(cexp_ref.dtype)
    cexp_ref[p2:2 * p2, :] = (-co_im).astype(cexp_ref.dtype)


def _s5_main_kernel(u_ref, kmat_ref, wst_ref, cexp_ref, alpha_ref, y_ref, x_ref, sf_ref, sb_ref,
                    *, n_batch, n_chunks):
    p2 = 2 * S5_STATE
    u = u_ref[...].astype(BF16)
    x_ref[...] = _dot(u, wst_ref[...])
    a_re = alpha_ref[0:1, :]
    a_im = alpha_ref[1:2, :]
    lane = lax.broadcasted_iota(I32, (1, p2), 1)
    is_f = lane < S5_STATE

    sub = S5_SCAN_ROWS
    assert n_chunks % sub == 0

    def step(k, carry):
        new = []
        for b in range(n_batch):
            s_re, s_im = carry[b]
            row_f = pl.multiple_of(b * n_chunks + sub * k, sub)
            row_b = pl.multiple_of(b * n_chunks + (n_chunks - sub) - sub * k, sub)
            xf = x_ref[pl.ds(row_f, sub), :]
            xb = x_ref[pl.ds(row_b, sub), :]
            seen = []
            for r in range(sub):
                seen.append(jnp.concatenate([s_re, s_im], axis=1))
                rb = sub - 1 - r
                x_re = jnp.where(is_f, xf[r:r + 1, 0:p2], xb[rb:rb + 1, 0:p2])
                x_im = jnp.where(is_f, xf[r:r + 1, p2:2 * p2], xb[rb:rb + 1, p2:2 * p2])
                s_re, s_im = a_re * s_re - a_im * s_im + x_re, a_re * s_im + a_im * s_re + x_im
            sf_ref[pl.ds(row_f, sub), :] = jnp.concatenate(seen, axis=0)
            sb_ref[pl.ds(row_b, sub), :] = jnp.concatenate(seen[::-1], axis=0)
            new.append((s_re, s_im))
        return tuple(new)

    zero = jnp.zeros((1, p2), F32)
    lax.fori_loop(0, n_chunks // sub, step, tuple((zero, zero) for _ in range(n_batch)))
    lane2 = lax.broadcasted_iota(I32, sf_ref.shape, 1) % p2
    s_in = jnp.where(lane2 < S5_STATE, sf_ref[...], sb_ref[...])
    s_hi, s_lo = _split_bf16(s_in)
    y_ref[...] = _dot(u, kmat_ref[...]) + _dot(s_hi, cexp_ref[...]) + _dot(s_lo, cexp_ref[...])


S5_RELAYOUT_CHUNKS = 128
S5_SCAN_ROWS = 8


def _s5_group_major_kernel(h_ref, g_ref, u_ref, hn_ref, ut_ref, *, seq_len):
    n_g, mt, ti = u_ref.shape
    t_len = S5_CHUNK
    n_i = ti // t_len
    n_lt = hn_ref.shape[0]
    g_lt = LANES // n_i
    rows = h_ref.shape[0]
    valid = seq_len - pl.program_id(1) * rows
    row = lax.broadcasted_iota(I32, h_ref.shape, 0)
    hn = jnp.where(row < valid, _rms(h_ref[...], g_ref[...]), 0.0)
    for j in range(n_lt):
        hn_ref[j] = hn[:, j * LANES:(j + 1) * LANES]
    for tau in range(t_len):
        for j in range(n_lt):
            xt = jnp.transpose(hn_ref[j, pl.ds(tau, mt, stride=t_len), :])
            ut_ref[j * g_lt:(j + 1) * g_lt, tau * n_i:(tau + 1) * n_i, :] = xt.reshape(g_lt, n_i, mt)
    for g in range(n_g):
        u_ref[g] = jnp.transpose(ut_ref[g]).astype(u_ref.dtype)


def _s5_token_major_kernel(y_ref, o_ref, zt_ref, z_ref):
    n_g, mt, ti = y_ref.shape
    t_len = S5_CHUNK
    n_i = ti // t_len
    n_lt = z_ref.shape[0]
    for g in range(n_g):
        yt = jnp.transpose(y_ref[g])
        zt_ref[:, g * n_i:(g + 1) * n_i, :] = yt.reshape(t_len, n_i, mt)
    for t in range(t_len):
        for j in range(n_lt):
            z_ref[j, pl.ds(t, mt, stride=t_len), :] = jnp.transpose(zt_ref[t, j * LANES:(j + 1) * LANES, :])
    for j in range(n_lt):
        o_ref[:, j * LANES:(j + 1) * LANES] = z_ref[j]


def _gelu_tanh(x):
    return 0.5 * x * (1.0 + jnp.tanh(math.sqrt(2.0 / math.pi) * (x + 0.044715 * (x * x * x))))


def _s5_glu_kernel(h_ref, y_ref, g_ref, d_ref, w_ref, o_ref):
    h = h_ref[...]
    dm = h.shape[1]
    y = y_ref[...] + d_ref[...] * _rms(h, g_ref[...])
    gl = _gelu_tanh(y).astype(BF16)
    a = _dot(gl, w_ref[:, 0:dm])
    b = _dot(gl, w_ref[:, dm:2 * dm])
    o_ref[...] = h + a * jax.nn.sigmoid(b)


def s5_mix(h3, gain, b_re, b_im, lam_re, lam_im, log_step, c_re, c_im, d_skip, w_glu):
    bsz, l, dm = h3.shape
    n_g, n_p, n_i = b_re.shape
    t_len = S5_CHUNK
    assert l % t_len == 0 and n_g * n_i == dm and n_p == S5_STATE and n_i == S5_GROUP_CH
    n_chunks = -(-(l // t_len) // S5_SCAN_ROWS) * S5_SCAN_ROWS
    m = bsz * n_chunks
    ti = t_len * n_i
    p2 = 2 * n_p
    n = bsz * l
    tm = _row_tile(n, 608)
    h2 = h3.reshape(n, dm)
    gain2 = gain.astype(F32).reshape(1, dm)

    mt = min(S5_RELAYOUT_CHUNKS, n_chunks)
    n_tiles = -(-n_chunks // mt)
    u = pl.pallas_call(
        functools.partial(_s5_group_major_kernel, seq_len=l),
        grid=(bsz, n_tiles),
        in_specs=[pl.BlockSpec((None, mt * t_len, dm), lambda b, i: (b, i, 0)),
                  pl.BlockSpec((1, dm), lambda b, i: (0, 0))],
        out_specs=pl.BlockSpec((n_g, None, mt, ti), lambda b, i: (0, b, i, 0)),
        out_shape=jax.ShapeDtypeStruct((n_g, bsz, n_chunks, ti), BF16),
        scratch_shapes=[pltpu.VMEM((dm // LANES, mt * t_len, LANES), F32), pltpu.VMEM((n_g, ti, mt), F32)],
        compiler_params=_cparams("parallel", "parallel"),
        name="s5_group_major",
    )(h3, gain2).reshape(n_g, m, ti)

    f32 = lambda a: a.astype(F32)
    lam_r = jnp.stack([jnp.concatenate([f32(lam_re)[0], f32(lam_re)[1]], axis=-1),
                       jnp.concatenate([f32(lam_im)[0], f32(lam_im)[1]], axis=-1)], axis=1)
    step_r = jnp.repeat(jnp.transpose(f32(log_step))[:, None, :], n_p, axis=2)
    bt = jnp.stack([jnp.transpose(f32(b_re), (0, 2, 1)), jnp.transpose(f32(b_im), (0, 2, 1))], axis=1)
    bt = jnp.tile(bt, (1, 1, 1, 2))
    ct = jnp.stack([f32(c_re), f32(c_im)], axis=0)
    ct = jnp.transpose(ct, (2, 0, 3, 1, 4)).reshape(n_g, 2, n_i, p2)

    gspec = lambda *shape: pl.BlockSpec((None,) + shape, lambda g: (g,) + (0,) * len(shape))
    kmat, wst, cexp, alpha = pl.pallas_call(
        _s5_param_kernel,
        grid=(n_g,),
        in_specs=[gspec(2, p2), gspec(1, p2), gspec(2, n_i, p2), gspec(2, n_i, p2)],
        out_specs=[gspec(ti, ti), gspec(ti, 2 * p2), gspec(2 * p2, ti), gspec(2, p2)],
        out_shape=[
            jax.ShapeDtypeStruct((n_g, ti, ti), BF16),
            jax.ShapeDtypeStruct((n_g, ti, 2 * p2), BF16),
            jax.ShapeDtypeStruct((n_g, 2 * p2, ti), BF16),
            jax.ShapeDtypeStruct((n_g, 2, p2), F32),
        ],
        compiler_params=_cparams("parallel"),
        name="s5_params",
    )(lam_r, step_r, bt, ct)

    y = pl.pallas_call(
        functools.partial(_s5_main_kernel, n_batch=bsz, n_chunks=n_chunks),
        grid=(n_g,),
        in_specs=[gspec(m, ti), gspec(ti, ti), gspec(ti, 2 * p2), gspec(2 * p2, ti), gspec(2, p2)],
        out_specs=gspec(m, ti),
        out_shape=jax.ShapeDtypeStruct((n_g, m, ti), F32),
        scratch_shapes=[pltpu.VMEM((m, 2 * p2), F32), pltpu.VMEM((m, 2 * p2), F32), pltpu.VMEM((m, 2 * p2), F32)],
        compiler_params=_cparams("parallel"),
        name="s5_main",
    )(u, kmat, wst, cexp, alpha)
    y2 = pl.pallas_call(
        _s5_token_major_kernel,
        grid=(bsz, n_tiles),
        in_specs=[pl.BlockSpec((n_g, None, mt, ti), lambda b, i: (0, b, i, 0))],
        out_specs=pl.BlockSpec((None, mt * t_len, dm), lambda b, i: (b, i, 0)),
        out_shape=jax.ShapeDtypeStruct((bsz, l, dm), F32),
        scratch_shapes=[pltpu.VMEM((t_len, dm, mt), F32), pltpu.VMEM((dm // LANES, mt * t_len, LANES), F32)],
        compiler_params=_cparams("parallel", "parallel"),
        name="s5_token_major",
    )(y.reshape(n_g, bsz, n_chunks, ti)).reshape(n, dm)

    out = pl.pallas_call(
        _s5_glu_kernel,
        grid=(n // tm,),
        in_specs=[
            pl.BlockSpec((tm, dm), lambda i: (i, 0)),
            pl.BlockSpec((tm, dm), lambda i: (i, 0)),
            pl.BlockSpec((1, dm), lambda i: (0, 0)),
            pl.BlockSpec((1, dm), lambda i: (0, 0)),
            pl.BlockSpec((dm, 2 * dm), lambda i: (0, 0)),
        ],
        out_specs=pl.BlockSpec((tm, dm), lambda i: (i, 0)),
        out_shape=jax.ShapeDtypeStruct((n, dm), F32),
        compiler_params=_cparams("parallel"),
        name="s5_glu",
    )(h2, y2, gain2, f32(d_skip).reshape(1, dm), w_glu.astype(BF16))
    return out.reshape(bsz, l, dm)


def na_rwkv_mix(h3, gain, w_in, w_out, rpb, mu, w0, w_up, a0, a_up, g_up, k_k, k_a, r_k, lnx_g, lnx_b):
    bsz, l, dm = h3.shape
    n = bsz * l
    h2 = h3.reshape(n, dm)
    n_qkv = 3 * (w_out.shape[0] // 2)
    qkv, rest = norm_inproj(h2, gain.astype(F32), w_in.astype(BF16), n_qkv)
    na = na_attention(qkv.reshape(bsz, l, n_qkv), rpb)
    rw = rwkv_mix(rest.reshape(bsz, l, -1), mu, w0, w_up, a0, a_up, g_up, k_k, k_a, r_k, lnx_g, lnx_b)
    out = outproj_residual(h2, na.reshape(n, -1), rw.reshape(n, -1), w_out.astype(BF16))
    return out.reshape(bsz, l, dm)


def kernel(x, meta_tokens, norm_mix, norm_ffn, norm_final, mix_w_in, mix_w_out, na_rpb, rwkv_mu,
           rwkv_w0, rwkv_w_up, rwkv_a0, rwkv_a_up, rwkv_g_up, rwkv_k_k, rwkv_k_a, rwkv_r_k,
           rwkv_lnx_g, rwkv_lnx_b, s5_b_re, s5_b_im, s5_lambda_re, s5_lambda_im, s5_log_step,
           s5_c_re, s5_c_im, s5_d, s5_w_glu, moe_w_group, moe_b_group, moe_w_expert, moe_b_expert,
           moe_w1, moe_w3, moe_w2):
    bsz, _, dm = x.shape
    depth = norm_mix.shape[0]
    meta = jnp.broadcast_to(meta_tokens.astype(x.dtype)[None], (bsz,) + meta_tokens.shape)
    h = jnp.concatenate([meta, x], axis=1)
    l = h.shape[1]
    for layer in range(depth):
        i = layer // 2
        if layer % 2 == 0:
            h = na_rwkv_mix(h, norm_mix[layer], mix_w_in[i], mix_w_out[i], na_rpb[i], rwkv_mu[i], rwkv_w0[i],
                            rwkv_w_up[i], rwkv_a0[i], rwkv_a_up[i], rwkv_g_up[i], rwkv_k_k[i], rwkv_k_a[i],
                            rwkv_r_k[i], rwkv_lnx_g[i], rwkv_lnx_b[i])
        else:
            h = s5_mix(h, norm_mix[layer], s5_b_re[i], s5_b_im[i], s5_lambda_re[i], s5_lambda_im[i],
                       s5_log_step[i], s5_c_re[i], s5_c_im[i], s5_d[i], s5_w_glu[i])
        h = hierarchical_moe_residual(h.reshape(bsz * l, dm), norm_ffn[layer].astype(F32), moe_w_group[layer],
                                      moe_b_group[layer], moe_w_expert[layer], moe_b_expert[layer],
                                      moe_w1, moe_w3, moe_w2, layer).reshape(bsz, l, dm)
    return final_norm(h, norm_final.astype(F32))
```

```python
import functools
import math

import jax
import jax.numpy as jnp
from jax import lax
from jax.experimental import pallas as pl
from jax.experimental.pallas import tpu as pltpu

F32 = jnp.float32
BF16 = jnp.bfloat16
I32 = jnp.int32

N_META = 16
GRID_W = 64
HEAD_DIM = 64
NA_WIN_ROWS = 8
NA_WIN_COLS = 16
S5_GROUP_CH = 16
S5_STATE = 64
MOE_GROUPS = 4
MOE_PER_GROUP = 8
MOE_EXPERTS = MOE_GROUPS * MOE_PER_GROUP
NORM_EPS = 1e-6
RWKV_GN_EPS = 64e-5
NEG_INF = -1e30

LANES = 128
SUBLANES_BF16 = 16
VMEM_LIMIT_BYTES = 56 * 1024 * 1024

MOE_TILE = 256
MOE_TOKEN_TILE = 320
ROUTER_LANES = 128


def _cparams(*sem):
    return pltpu.CompilerParams(dimension_semantics=sem, vmem_limit_bytes=VMEM_LIMIT_BYTES)


def _row_tile(n, target):
    best = None
    for t in range(SUBLANES_BF16, min(n, target) + 1, SUBLANES_BF16):
        if n % t == 0:
            best = t
    assert best is not None, (n, target)
    return best


def _rms(x, gain):
    ms = jnp.mean(x * x, axis=-1, keepdims=True)
    return (x * lax.rsqrt(ms + NORM_EPS)) * gain


def _split_bf16(x):
    hi = x.astype(BF16)
    lo = (x - hi.astype(F32)).astype(BF16)
    return hi, lo


def _dot(a, b):
    return jnp.dot(a, b, preferred_element_type=F32)


def _dot_nt(a, b):
    return lax.dot_general(a, b, (((1,), (1,)), ((), ())), preferred_element_type=F32)


def _norm_inproj_kernel(h_ref, g_ref, w_ref, qkv_ref, rest_ref, xn_ref, *, n_qkv, chunk):
    xn_ref[...] = _rms(h_ref[...], g_ref[...]).astype(BF16)
    n_all = w_ref.shape[1]
    for c in range(0, n_all, chunk):
        y = _dot(xn_ref[...], w_ref[:, c:c + chunk])
        if c < n_qkv:
            qkv_ref[:, c:c + chunk] = y.astype(BF16)
        else:
            rest_ref[:, c - n_qkv:c - n_qkv + chunk] = y


def norm_inproj(h2, gain, w_bf16, n_qkv):
    n, d = h2.shape
    n_all = w_bf16.shape[1]
    tm = _row_tile(n, 608)
    chunk = 256
    assert n_qkv % chunk == 0 and n_all % chunk == 0
    return pl.pallas_call(
        functools.partial(_norm_inproj_kernel, n_qkv=n_qkv, chunk=chunk),
        grid=(n // tm,),
        in_specs=[
            pl.BlockSpec((tm, d), lambda i: (i, 0)),
            pl.BlockSpec((1, d), lambda i: (0, 0)),
            pl.BlockSpec((d, n_all), lambda i: (0, 0)),
        ],
        out_specs=[
            pl.BlockSpec((tm, n_qkv), lambda i: (i, 0)),
            pl.BlockSpec((tm, n_all - n_qkv), lambda i: (i, 0)),
        ],
        out_shape=[
            jax.ShapeDtypeStruct((n, n_qkv), BF16),
            jax.ShapeDtypeStruct((n, n_all - n_qkv), F32),
        ],
        scratch_shapes=[pltpu.VMEM((tm, d), BF16)],
        compiler_params=_cparams("parallel"),
        name="norm_inproj",
    )(h2, gain.reshape(1, d), w_bf16)


def _outproj_kernel(h_ref, na_ref, rw_ref, wa_ref, wb_ref, o_ref):
    acc = _dot(na_ref[...], wa_ref[...])
    acc = acc + _dot(rw_ref[...], wb_ref[...])
    o_ref[...] = h_ref[...] + acc


def outproj_residual(h2, na, rw, w_out_bf16):
    n, d = h2.shape
    ka, kb = na.shape[1], rw.shape[1]
    tm = _row_tile(n, 608)
    return pl.pallas_call(
        _outproj_kernel,
        grid=(n // tm,),
        in_specs=[
            pl.BlockSpec((tm, d), lambda i: (i, 0)),
            pl.BlockSpec((tm, ka), lambda i: (i, 0)),
            pl.BlockSpec((tm, kb), lambda i: (i, 0)),
            pl.BlockSpec((ka, d), lambda i: (0, 0)),
            pl.BlockSpec((kb, d), lambda i: (0, 0)),
        ],
        out_specs=pl.BlockSpec((tm, d), lambda i: (i, 0)),
        out_shape=jax.ShapeDtypeStruct((n, d), F32),
        compiler_params=_cparams("parallel"),
        name="outproj_residual",
    )(h2, na, rw, w_out_bf16[:ka], w_out_bf16[ka:])


def _store_token_tiles(ref, x):
    rows = x.shape[0]
    s_n = x.shape[1] // LANES
    for s in range(s_n):
        ref[pl.ds(s, rows, stride=s_n), :] = x[:, s * LANES:(s + 1) * LANES]


def _load_token_tile_cols(ref, s, rows, s_n):
    return ref[pl.ds(s, rows, stride=s_n), :]


def _router_kernel(h_ref, g_ref, whi_ref, wlo_ref, b_ref, xn_ref, route_ref):
    xn = _rms(h_ref[...], g_ref[...])
    x_hi, x_lo = _split_bf16(xn)
    _store_token_tiles(xn_ref, xn)
    logits = (_dot(x_hi, whi_ref[...]) + _dot(x_hi, wlo_ref[...]) + _dot(x_lo, whi_ref[...])
              + b_ref[...])
    tm = logits.shape[0]
    lane = lax.broadcasted_iota(I32, (tm, ROUTER_LANES), 1)
    big = jnp.int32(ROUTER_LANES)

    is_g = lane < MOE_GROUPS
    lg = jnp.where(is_g, logits, -jnp.inf)
    eg = jnp.where(is_g, jnp.exp(lg - jnp.max(lg, axis=-1, keepdims=True)), 0.0)
    pg = eg / jnp.sum(eg, axis=-1, keepdims=True)
    p_grp = jnp.max(pg, axis=-1, keepdims=True)
    grp = jnp.min(jnp.where(is_g & (pg == p_grp), lane, big), axis=-1, keepdims=True)

    lo_lane = MOE_GROUPS + MOE_PER_GROUP * grp
    is_e = (lane >= lo_lane) & (lane < lo_lane + MOE_PER_GROUP)
    le = jnp.where(is_e, logits, -jnp.inf)
    ee = jnp.where(is_e, jnp.exp(le - jnp.max(le, axis=-1, keepdims=True)), 0.0)
    pe = jnp.where(is_e, ee / jnp.sum(ee, axis=-1, keepdims=True), -1.0)
    p1 = jnp.max(pe, axis=-1, keepdims=True)
    i1 = jnp.min(jnp.where(pe == p1, lane, big), axis=-1, keepdims=True)
    pe2 = jnp.where(lane == i1, -1.0, pe)
    p2 = jnp.max(pe2, axis=-1, keepdims=True)
    i2 = jnp.min(jnp.where(pe2 == p2, lane, big), axis=-1, keepdims=True)
    denom = p1 + p2
    g1 = p_grp * p1 / denom
    g2 = p_grp * p2 / denom
    e1 = (i1 - MOE_GROUPS).astype(F32)
    e2 = (i2 - MOE_GROUPS).astype(F32)
    route_ref[...] = jnp.where(lane == 0, e1, jnp.where(lane == 1, e2, jnp.where(lane == 2, g1, g2)))


def moe_router(h2, gain, w_group, b_group, w_expert, b_expert):
    n, d = h2.shape
    n_r = MOE_GROUPS + MOE_EXPERTS
    w_r = jnp.concatenate([w_group, jnp.transpose(w_expert, (1, 0, 2)).reshape(d, MOE_EXPERTS)], axis=1)
    w_r = jnp.pad(w_r.astype(F32), ((0, 0), (0, ROUTER_LANES - n_r)))
    w_hi, w_lo = _split_bf16(w_r)
    b_r = jnp.pad(jnp.concatenate([b_group, b_expert.reshape(-1)]).astype(F32), (0, ROUTER_LANES - n_r))
    tm = _row_tile(n, 608)
    return pl.pallas_call(
        _router_kernel,
        grid=(n // tm,),
        in_specs=[
            pl.BlockSpec((tm, d), lambda i: (i, 0)),
            pl.BlockSpec((1, d), lambda i: (0, 0)),
            pl.BlockSpec((d, ROUTER_LANES), lambda i: (0, 0)),
            pl.BlockSpec((d, ROUTER_LANES), lambda i: (0, 0)),
            pl.BlockSpec((1, ROUTER_LANES), lambda i: (0, 0)),
        ],
        out_specs=[
            pl.BlockSpec((tm * (d // LANES), LANES), lambda i: (i, 0)),
            pl.BlockSpec((tm, ROUTER_LANES), lambda i: (i, 0)),
        ],
        out_shape=[
            jax.ShapeDtypeStruct((n * (d // LANES), LANES), F32),
            jax.ShapeDtypeStruct((n, ROUTER_LANES), F32),
        ],
        compiler_params=_cparams("parallel"),
        name="moe_router",
    )(h2, gain.reshape(1, d), w_hi, w_lo, b_r.reshape(1, ROUTER_LANES))


def _moe_dispatch_kernel(tail_ref, n_used_ref, dst_ref, xn_hbm, xbuf_hbm, zero_ref, stage_ref, sem, lsem, zsem,
                         *, tm, s_n, n_blocks):
    i = pl.program_id(0)
    n_steps = pl.num_programs(0)
    tile_rows = zero_ref.shape[0]

    def zero_block(b, carry):
        pltpu.make_async_copy(zero_ref, xbuf_hbm.at[pl.ds(pl.multiple_of(b * tile_rows, tile_rows), tile_rows), :],
                              zsem).start()
        return carry

    def wait_zero_block(b, carry):
        pltpu.make_async_copy(zero_ref, xbuf_hbm.at[pl.ds(0, tile_rows), :], zsem).wait()
        return carry

    slot = i & 1

    def load(step, s):
        return pltpu.make_async_copy(xn_hbm.at[pl.ds(pl.multiple_of(step * tm * s_n, s_n), tm * s_n), :],
                                     stage_ref.at[s], lsem.at[s])

    def wait_rows_out(s):
        for _ in range(2):
            pltpu.make_async_copy(stage_ref.at[s], xbuf_hbm.at[pl.ds(0, tm * s_n), :], sem.at[s]).wait()

    @pl.when(i == 0)
    def _():
        load(0, 0).start()
        zero_ref[...] = jnp.zeros_like(zero_ref)
        for e in range(tail_ref.shape[0]):
            @pl.when(tail_ref[e] >= 0)
            def _():
                pltpu.make_async_copy(zero_ref, xbuf_hbm.at[pl.ds(pl.multiple_of(tail_ref[e], s_n), tile_rows), :],
                                      zsem).start()
        lax.fori_loop(n_used_ref[0], n_blocks, zero_block, 0)
        for e in range(tail_ref.shape[0]):
            @pl.when(tail_ref[e] >= 0)
            def _():
                wait_zero_block(0, 0)
        lax.fori_loop(n_used_ref[0], n_blocks, wait_zero_block, 0)

    @pl.when(i > 0)
    def _():
        wait_rows_out(1 - slot)

    @pl.when(i + 1 < n_steps)
    def _():
        load(i + 1, 1 - slot).start()

    load(i, slot).wait()
    for r in range(tm):
        src = stage_ref.at[slot, pl.ds(r * s_n, s_n), :]
        for k in range(2):
            dst = pl.multiple_of(dst_ref[k, r], s_n)
            pltpu.make_async_copy(src, xbuf_hbm.at[pl.ds(dst, s_n), :], sem.at[slot]).start()

    @pl.when(i == n_steps - 1)
    def _():
        wait_rows_out(slot)


def moe_dispatch(xn_tiles, dst_tiles, tail_start, n_used, n_blocks, tm, s_n):
    n_steps = dst_tiles.shape[0]
    n_rows = n_blocks * MOE_TILE
    assert n_steps * tm * s_n == xn_tiles.shape[0]
    grid_spec = pltpu.PrefetchScalarGridSpec(
        num_scalar_prefetch=2,
        grid=(n_steps,),
        in_specs=[
            pl.BlockSpec((None, 2, tm), lambda i, tail, nu: (i, 0, 0), memory_space=pltpu.SMEM),
            pl.BlockSpec(memory_space=pl.ANY),
        ],
        out_specs=pl.BlockSpec(memory_space=pl.ANY),
        scratch_shapes=[
            pltpu.VMEM((MOE_TILE * s_n, LANES), F32),
            pltpu.VMEM((2, tm * s_n, LANES), F32),
            pltpu.SemaphoreType.DMA((2,)),
            pltpu.SemaphoreType.DMA((2,)),
            pltpu.SemaphoreType.DMA(()),
        ],
    )
    return pl.pallas_call(
        functools.partial(_moe_dispatch_kernel, tm=tm, s_n=s_n, n_blocks=n_blocks),
        grid_spec=grid_spec,
        out_shape=jax.ShapeDtypeStruct((n_rows * s_n, LANES), F32),
        compiler_params=_cparams("arbitrary"),
        name="moe_dispatch",
    )(tail_start, n_used, dst_tiles, xn_tiles)


def _expert_kernel(blk_e_ref, n_used_ref, x_ref, w1_ref, w3_ref, w2_ref, y_ref, xb_ref, w1b_ref, w3b_ref, w2b_ref):
    i = pl.program_id(0)
    used = i < n_used_ref[0]
    tile, d = xb_ref.shape
    s_n = d // LANES
    prev_e = blk_e_ref[jnp.maximum(i - 1, 0)]
    fresh = (i == 0) | (blk_e_ref[i] != prev_e)

    @pl.when(used & fresh)
    def _():
        w1b_ref[...] = w1_ref[...].astype(BF16)
        w3b_ref[...] = w3_ref[...].astype(BF16)
        w2b_ref[...] = w2_ref[...].astype(BF16)

    @pl.when(used)
    def _():
        for s in range(s_n):
            xb_ref[:, s * LANES:(s + 1) * LANES] = _load_token_tile_cols(x_ref, s, tile, s_n).astype(BF16)
        x = xb_ref[...]
        a = _dot(x, w1b_ref[...])
        b = _dot(x, w3b_ref[...])
        hmid = (a * jax.nn.sigmoid(a) * b).astype(BF16)
        _store_token_tiles(y_ref, _dot(hmid, w2b_ref[...]))

    @pl.when(jnp.logical_not(used))
    def _():
        y_ref[...] = jnp.zeros_like(y_ref)


def moe_experts(xbuf, blk_e, n_used, w1, w3, w2, layer, n_blocks):
    d, f = w1.shape[2], w1.shape[3]
    s_n = d // LANES
    tile = MOE_TILE

    def w_map(i, blk_e_ref, n_used_ref):
        return (layer, blk_e_ref[i], 0, 0)

    def x_map(i, blk_e_ref, n_used_ref):
        return (jnp.minimum(i, jnp.maximum(n_used_ref[0] - 1, 0)), 0)

    grid_spec = pltpu.PrefetchScalarGridSpec(
        num_scalar_prefetch=2,
        grid=(n_blocks,),
        in_specs=[
            pl.BlockSpec((tile * s_n, LANES), x_map),
            pl.BlockSpec((None, None, d, f), w_map),
            pl.BlockSpec((None, None, d, f), w_map),
            pl.BlockSpec((None, None, f, d), w_map),
        ],
        out_specs=pl.BlockSpec((tile * s_n, LANES), lambda i, be, nu: (i, 0)),
        scratch_shapes=[
            pltpu.VMEM((tile, d), BF16),
            pltpu.VMEM((d, f), BF16),
            pltpu.VMEM((d, f), BF16),
            pltpu.VMEM((f, d), BF16),
        ],
    )
    return pl.pallas_call(
        _expert_kernel,
        grid_spec=grid_spec,
        out_shape=jax.ShapeDtypeStruct((n_blocks * tile * s_n, LANES), F32),
        compiler_params=_cparams("arbitrary"),
        name="moe_experts",
    )(blk_e, n_used, xbuf, w1, w3, w2)


def _moe_combine_kernel(src_ref, src_next_ref, h_ref, route_ref, y_hbm, o_ref, yg_ref, sem, *, n_steps):
    i = pl.program_id(0)
    slot = i & 1
    tm, d = h_ref.shape
    s_n = d // LANES

    def start_gather(ids_ref, dst_slot):
        for k in range(2):
            for r in range(tm):
                src = pl.multiple_of(ids_ref[k, r], s_n)
                pltpu.make_async_copy(y_hbm.at[pl.ds(src, s_n), :],
                                      yg_ref.at[dst_slot, k, pl.ds(r * s_n, s_n), :], sem.at[dst_slot]).start()

    @pl.when(i == 0)
    def _():
        start_gather(src_ref, 0)

    @pl.when(i + 1 < n_steps)
    def _():
        start_gather(src_next_ref, 1 - slot)

    for k in range(2):
        pltpu.make_async_copy(y_hbm.at[pl.ds(0, tm * s_n), :], yg_ref.at[slot, k], sem.at[slot]).wait()
    route = route_ref[...]
    g1 = route[:, 2:3]
    g2 = route[:, 3:4]
    for s in range(s_n):
        cols = slice(s * LANES, (s + 1) * LANES)
        o_ref[:, cols] = (h_ref[:, cols] + g1 * _load_token_tile_cols(yg_ref.at[slot, 0], s, tm, s_n)
                          + g2 * _load_token_tile_cols(yg_ref.at[slot, 1], s, tm, s_n))


def moe_combine(h2, route, y_tiles, src_tiles, tm):
    n, d = h2.shape
    s_n = d // LANES
    n_steps = src_tiles.shape[0]
    ids = lambda index: pl.BlockSpec((None, 2, tm), lambda i: (index(i), 0, 0), memory_space=pltpu.SMEM)
    return pl.pallas_call(
        functools.partial(_moe_combine_kernel, n_steps=n_steps),
        grid=(n_steps,),
        in_specs=[
            ids(lambda i: i),
            ids(lambda i: jnp.minimum(i + 1, n_steps - 1)),
            pl.BlockSpec((tm, d), lambda i: (i, 0)),
            pl.BlockSpec((tm, ROUTER_LANES), lambda i: (i, 0)),
            pl.BlockSpec(memory_space=pl.ANY),
        ],
        out_specs=pl.BlockSpec((tm, d), lambda i: (i, 0)),
        out_shape=jax.ShapeDtypeStruct((n, d), F32),
        scratch_shapes=[pltpu.VMEM((2, 2, tm * s_n, LANES), F32), pltpu.SemaphoreType.DMA((2,))],
        compiler_params=_cparams("arbitrary"),
        name="moe_combine",
    )(src_tiles, src_tiles, h2, route, y_tiles)


def hierarchical_moe_residual(h2, gain, w_group, b_group, w_expert, b_expert, w1, w3, w2, layer):
    n, d = h2.shape
    xn, route = moe_router(h2, gain, w_group, b_group, w_expert, b_expert)
    e_km = jnp.concatenate([route[:, 0], route[:, 1]]).astype(I32)
    n_assign = 2 * n
    onehot = (e_km[:, None] == jnp.arange(MOE_EXPERTS, dtype=I32)[None, :]).astype(I32)
    csum = jnp.cumsum(onehot, axis=0)
    counts = csum[-1]
    padded = (counts + MOE_TILE - 1) // MOE_TILE * MOE_TILE
    pad_end = jnp.cumsum(padded)
    pad_start = pad_end - padded
    dest = jnp.sum((csum - onehot + pad_start[None, :]) * onehot, axis=1)
    n_blocks = -(-n_assign // MOE_TILE) + MOE_EXPERTS
    n_rows = n_blocks * MOE_TILE
    blk_start = jnp.arange(n_blocks, dtype=I32) * MOE_TILE
    blk_e = jnp.minimum(jnp.sum((pad_end[None, :] <= blk_start[:, None]).astype(I32), axis=1),
                        MOE_EXPERTS - 1).astype(I32)
    n_used = (pad_end[-1] // MOE_TILE).astype(I32).reshape(1)
    s_n = d // LANES
    tm = _row_tile(n, MOE_TOKEN_TILE)
    dest_tiles = jnp.transpose((dest * s_n).astype(I32).reshape(2, n // tm, tm), (1, 0, 2))
    tail_start = jnp.where(counts > 0, (pad_end - MOE_TILE) * s_n, -1).astype(I32)
    xbuf = moe_dispatch(xn, dest_tiles, tail_start, n_used, n_blocks, tm, s_n)
    y = moe_experts(xbuf, blk_e, n_used, w1, w3, w2, layer, n_blocks)
    return moe_combine(h2, route, y, dest_tiles, tm)


def _final_norm_kernel(h_ref, g_ref, o_ref):
    o_ref[...] = _rms(h_ref[...], g_ref[...])


def final_norm(h3, gain):
    b, l, d = h3.shape
    t = l - N_META
    tm = _row_tile(t, 512)
    return pl.pallas_call(
        _final_norm_kernel,
        grid=(b, t // tm),
        in_specs=[
            pl.BlockSpec((None, pl.Element(tm), pl.Element(d)),
                         lambda bi, i: (bi, pl.multiple_of(N_META + i * tm, SUBLANES_BF16), 0)),
            pl.BlockSpec((1, d), lambda bi, i: (0, 0)),
        ],
        out_specs=pl.BlockSpec((None, tm, d), lambda bi, i: (bi, i, 0)),
        out_shape=jax.ShapeDtypeStruct((b, t, d), F32),
        compiler_params=_cparams("parallel", "parallel"),
        name="final_norm",
    )(h3, gain.reshape(1, d))


NA_QROWS = 8
NA_KROWS = 3 * NA_QROWS
NA_ROWS_PER_ITER = 4


def _na_kernel(q_ref, kw_ref, vw_ref, qm_ref, km_ref, vm_ref, bias_ref, o_ref, om_ref, *, rows, scale):
    blk = pl.program_id(1)
    tq = GRID_W
    n_pairs = q_ref.shape[1] // LANES
    base = jnp.clip(NA_QROWS * blk - NA_QROWS, 0, rows - NA_KROWS)
    lane = lax.broadcasted_iota(I32, (tq, LANES), 1)
    halves = [lane < HEAD_DIM, lane >= HEAD_DIM]

    pad = jnp.zeros((LANES - N_META, LANES), km_ref.dtype)
    k_meta = [jnp.concatenate([km_ref[:, p * LANES:(p + 1) * LANES], pad], axis=0) for p in range(n_pairs)]
    v_meta = [jnp.concatenate([vm_ref[:, p * LANES:(p + 1) * LANES], pad], axis=0) for p in range(n_pairs)]
    meta_bias = jnp.where(lane < N_META, 0.0, NEG_INF)

    n_win = NA_WIN_ROWS * GRID_W

    def row_body(jb, carry):
        colsl = [slice(p * LANES, (p + 1) * LANES) for p in range(n_pairs)]
        units = [(jj, p, hh) for jj in range(NA_ROWS_PER_ITER) for p in range(n_pairs) for hh in range(2)]
        s_idx, koff, qoff = [], [], []
        for jj in range(NA_ROWS_PER_ITER):
            j = jb * NA_ROWS_PER_ITER + jj
            r = NA_QROWS * blk + j
            start = jnp.clip(r - NA_WIN_ROWS // 2, 0, rows - NA_WIN_ROWS)
            s_idx.append(start - r + (NA_WIN_ROWS - 1))
            koff.append(pl.multiple_of((start - base) * GRID_W, GRID_W))
            qoff.append(pl.multiple_of(j * tq, tq))
        q_pair = {(jj, p): q_ref[pl.ds(qoff[jj], tq), c]
                  for jj in range(NA_ROWS_PER_ITER) for p, c in enumerate(colsl)}
        k_ext = {(jj, p): jnp.concatenate([kw_ref[pl.ds(koff[jj], n_win), c], k_meta[p]], axis=0)
                 for jj in range(NA_ROWS_PER_ITER) for p, c in enumerate(colsl)}
        v_ext = {(jj, p): jnp.concatenate([vw_ref[pl.ds(koff[jj], n_win), c], v_meta[p]], axis=0)
                 for jj in range(NA_ROWS_PER_ITER) for p, c in enumerate(colsl)}
        qh = [jnp.where(halves[hh], q_pair[jj, p], jnp.zeros_like(q_pair[jj, p])) for jj, p, hh in units]
        s = [_dot_nt(qh[u], k_ext[jj, p]) * scale
             + jnp.concatenate([bias_ref[2 * p + hh, s_idx[jj]], meta_bias], axis=1)
             for u, (jj, p, hh) in enumerate(units)]
        m = [jnp.max(x, axis=-1, keepdims=True) for x in s]
        e = [jnp.exp(x - mx) for x, mx in zip(s, m)]
        den = [jnp.sum(x, axis=-1, keepdims=True) for x in e]
        o = [_dot(e[u].astype(BF16), v_ext[jj, p]) / den[u] for u, (jj, p, hh) in enumerate(units)]
        for u in range(0, len(units), 2):
            jj, p, _ = units[u]
            o_ref[pl.ds(qoff[jj], tq), colsl[p]] = jnp.where(halves[0], o[u], o[u + 1]).astype(o_ref.dtype)
        return carry

    lax.fori_loop(0, NA_QROWS // NA_ROWS_PER_ITER, row_body, 0)

    @pl.when(blk == 0)
    def _():
        lane_m = lax.broadcasted_iota(I32, (N_META, LANES), 1)
        for p in range(n_pairs):
            cols = slice(p * LANES, (p + 1) * LANES)
            q_pair = qm_ref[:, cols]
            kmp = km_ref[:, cols]
            vmp = vm_ref[:, cols]
            outs = []
            for hh in range(2):
                sel = (lane_m < HEAD_DIM) if hh == 0 else (lane_m >= HEAD_DIM)
                qp = jnp.where(sel, q_pair, jnp.zeros_like(q_pair))
                s_m = _dot_nt(qp, kmp) * scale
                p_m = jnp.exp(s_m - jnp.max(s_m, axis=-1, keepdims=True))
                den = jnp.sum(p_m, axis=-1, keepdims=True)
                outs.append(_dot(p_m.astype(BF16), vmp) / den)
            om_ref[:, cols] = jnp.where(lane_m < HEAD_DIM, outs[0], outs[1]).astype(om_ref.dtype)


def _na_bias_table(rpb):
    h = rpb.shape[0]
    c_ids = jnp.arange(GRID_W)
    c_start = jnp.clip(c_ids - NA_WIN_COLS // 2, 0, GRID_W - NA_WIN_COLS)
    in_band = (c_ids[None, :] >= c_start[:, None]) & (c_ids[None, :] < c_start[:, None] + NA_WIN_COLS)
    dc = jnp.clip(c_ids[None, :] - c_ids[:, None] + NA_WIN_COLS - 1, 0, 2 * NA_WIN_COLS - 2)
    tab = jnp.where(in_band[None, None], rpb.astype(F32)[:, :, dc], NEG_INF)
    win = jnp.stack([tab[:, s:s + NA_WIN_ROWS] for s in range(NA_WIN_ROWS)], axis=1)
    return jnp.transpose(win, (0, 1, 3, 2, 4)).reshape(h, NA_WIN_ROWS, GRID_W, NA_WIN_ROWS * GRID_W)


def na_attention(qkv, rpb):
    b, l, w3 = qkv.shape
    w = w3 // 3
    t = l - N_META
    rows = t // GRID_W
    assert rows * GRID_W == t and rows % NA_QROWS == 0 and rows >= NA_KROWS
    tq = NA_QROWS * GRID_W
    tk = NA_KROWS * GRID_W
    bias = _na_bias_table(rpb)
    al = SUBLANES_BF16

    def q_map(bi, i):
        return (bi, pl.multiple_of(N_META + i * tq, al), 0)

    def kv_map(col):
        def f(bi, i):
            base = jnp.clip(NA_QROWS * i - NA_QROWS, 0, rows - NA_KROWS)
            return (bi, pl.multiple_of(N_META + base * GRID_W, al), col)
        return f

    def meta_map(col):
        return lambda bi, i: (bi, 0, col)

    el = pl.Element
    grid_out, meta_out = pl.pallas_call(
        functools.partial(_na_kernel, rows=rows, scale=HEAD_DIM ** -0.5),
        grid=(b, rows // NA_QROWS),
        in_specs=[
            pl.BlockSpec((None, el(tq), el(w)), q_map),
            pl.BlockSpec((None, el(tk), el(w)), kv_map(w)),
            pl.BlockSpec((None, el(tk), el(w)), kv_map(2 * w)),
            pl.BlockSpec((None, el(N_META), el(w)), meta_map(0)),
            pl.BlockSpec((None, el(N_META), el(w)), meta_map(w)),
            pl.BlockSpec((None, el(N_META), el(w)), meta_map(2 * w)),
            pl.BlockSpec(bias.shape, lambda bi, i: (0, 0, 0, 0)),
        ],
        out_specs=[
            pl.BlockSpec((None, tq, w), lambda bi, i: (bi, i, 0)),
            pl.BlockSpec((None, N_META, w), lambda bi, i: (bi, 0, 0)),
        ],
        out_shape=[
            jax.ShapeDtypeStruct((b, t, w), BF16),
            jax.ShapeDtypeStruct((b, N_META, w), BF16),
        ],
        compiler_params=_cparams("parallel", "arbitrary"),
        name="na_attention",
    )(qkv, qkv, qkv, qkv, qkv, qkv, bias)
    return jnp.concatenate([meta_out, grid_out], axis=1)


RWKV_CHUNK = 64
RWKV_TILE_CHUNKS = 2
RWKV_HALO = 8


def _split3_bf16(x):
    p1 = x.astype(BF16)
    r1 = x - p1.astype(F32)
    p2 = r1.astype(BF16)
    p3 = (r1 - p2.astype(F32)).astype(BF16)
    return p1, p2, p3


def _mm1(a, b):
    return _dot(a.astype(BF16), b.astype(BF16))


def _mm3(a, b):
    ah, al = _split_bf16(a)
    bh, bl = _split_bf16(b)
    return _dot(ah, bh) + _dot(ah, bl) + _dot(al, bh)


def _mm1_nt(a, b):
    return _dot_nt(a.astype(BF16), b.astype(BF16))


def _mm3_nt(a, b):
    ah, al = _split_bf16(a)
    bh, bl = _split_bf16(b)
    return _dot_nt(ah, bh) + _dot_nt(ah, bl) + _dot_nt(al, bh)


def _exact_left(mat_bf16, x):
    p1, p2, p3 = _split3_bf16(x)
    return _dot(mat_bf16, p1) + _dot(mat_bf16, p2) + _dot(mat_bf16, p3)


def _exact_right(x, mat_bf16):
    p1, p2, p3 = _split3_bf16(x)
    return _dot(p1, mat_bf16) + _dot(p2, mat_bf16) + _dot(p3, mat_bf16)


def _head_block_ones(width):
    ri = lax.broadcasted_iota(I32, (width, width), 0) // HEAD_DIM
    ci = lax.broadcasted_iota(I32, (width, width), 1) // HEAD_DIM
    return (ri == ci).astype(BF16)


def _head_sums(x, exact):
    ones_pair = _head_block_ones(LANES)
    tiles = []
    for p in range(x.shape[1] // LANES):
        xt = x[:, p * LANES:(p + 1) * LANES]
        tiles.append(_exact_right(xt, ones_pair) if exact else _dot(xt.astype(BF16), ones_pair))
    return jnp.concatenate(tiles, axis=1)


def _stack_heads(x, m0):
    z = jnp.zeros_like(x)
    return jnp.concatenate([jnp.where(m0, x, z), jnp.where(m0, z, x)], axis=0)


_MM_L4 = _mm1_nt
_MM_KT = _mm1_nt
_MM_SQ = _mm1
_MM_AP = _mm1
_MM_V = _mm1
_MM_Y = _mm1
_MM_UPD = _mm1


def _rwkv_chunk_maps(streams, c, n_sub):
    assert c == 64
    c2 = 2 * c
    lane = lax.broadcasted_iota(I32, (c, LANES), 1)
    m0 = lane < HEAD_DIM
    r_i = lax.broadcasted_iota(I32, (c2, c2), 0)
    c_i = lax.broadcasted_iota(I32, (c2, c2), 1)
    eye = (r_i == c_i).astype(F32)
    rel = r_i % c - c_i % c
    masks = {sg: (rel * sg > 0, rel * sg >= 0) for sg in {s["sign"] for s in streams}}
    items = [(j, q) for j in range(len(streams)) for q in range(n_sub)]

    def part(j, q, name):
        return _stack_heads(streams[j][name][q * c:(q + 1) * c], m0)

    lhs = {it: jnp.concatenate([part(*it, "kkp"), part(*it, "rp")], axis=0) for it in items}
    rhs = {it: jnp.concatenate([part(*it, "ki"), part(*it, "bi")], axis=0) for it in items}
    vs = {it: part(*it, "v") for it in items}
    kipcs = {it: part(*it, "kipc") for it in items}
    bipcs = {it: part(*it, "bipc") for it in items}
    l4 = {it: _MM_L4(lhs[it], rhs[it]) for it in items}
    m_kk, n1, m_rk, m_rb = {}, {}, {}, {}
    for it in items:
        strict, incl = masks[streams[it[0]]["sign"]]
        m = l4[it]
        m_kk[it] = jnp.where(strict, m[0:c2, 0:c2], 0.0)
        n1[it] = jnp.where(strict, m[0:c2, c2:2 * c2], 0.0)
        m_rk[it] = jnp.where(incl, m[c2:2 * c2, 0:c2], 0.0)
        m_rb[it] = jnp.where(incl, m[c2:2 * c2, c2:2 * c2], 0.0)
    n2 = {it: _MM_SQ(n1[it], n1[it]) for it in items}
    n4 = {it: _MM_SQ(n2[it], n2[it]) for it in items}
    n8 = {it: _MM_SQ(n4[it], n4[it]) for it in items}
    n16 = {it: _MM_SQ(n8[it], n8[it]) for it in items}
    n32 = {it: _MM_SQ(n16[it], n16[it]) for it in items}
    p1 = {it: (eye - n1[it]) + _MM_AP(eye - n1[it], n2[it]) for it in items}
    p2 = {it: eye + n4[it] + n8[it] + _MM_AP(n4[it], n8[it]) for it in items}
    p3 = {it: eye + n16[it] + n32[it] + _MM_AP(n16[it], n32[it]) for it in items}
    p23 = {it: _MM_AP(p2[it], p3[it]) for it in items}
    winv = {it: _MM_AP(p1[it], p23[it]) for it in items}
    mv = {it: _MM_V(m_kk[it], vs[it]) for it in items}
    mrv = {it: _MM_Y(m_rk[it], vs[it]) for it in items}
    wl = {it: _MM_AP(winv[it], lhs[it][0:c2]) for it in items}
    wmv = {it: _MM_AP(winv[it], mv[it]) for it in items}
    yl = {it: lhs[it][c2:2 * c2] - _MM_Y(m_rb[it], wl[it]) for it in items}
    y0 = {it: mrv[it] - _MM_Y(m_rb[it], wmv[it]) for it in items}
    g2 = {it: _MM_UPD(jnp.transpose(wl[it]), bipcs[it]) for it in items}
    hh = {it: _MM_UPD(jnp.transpose(jnp.concatenate([vs[it], -wmv[it]], axis=0)),
                      jnp.concatenate([kipcs[it], bipcs[it]], axis=0)) for it in items}
    return {it: (yl[it], y0[it], g2[it], hh[it]) for it in items}


def _rwkv_apply_maps(streams, maps, c, n_sub):
    c2 = 2 * c
    st = [s["st"] for s in streams]
    ys = {}
    for k in range(n_sub):
        cur = [(j, k if s["sign"] > 0 else n_sub - 1 - k) for j, s in enumerate(streams)]
        sg = [_MM_UPD(st[j], maps[j][q][2]) for j, q in cur]
        yk = [_MM_KT(maps[j][q][0], st[j]) + maps[j][q][1] for j, q in cur]
        for it, yi in zip(cur, yk):
            ys[it] = yi[0:c] + yi[c:c2]
        st = [st[j] * streams[j]["pc"][q] - sgi + maps[j][q][3] for (j, q), sgi in zip(cur, sg)]
    return [(jnp.concatenate([ys[j, q] for q in range(n_sub)], axis=0), st[j]) for j in range(len(streams))]


def _softplus(z):
    return jnp.maximum(z, 0.0) + jnp.log(1.0 + jnp.exp(-jnp.abs(z)))


def _rwkv_tile_prep(x_ref, xp_ref, xn_ref, tile, n_tiles, seq_len, width, sign, mu, w0, a0, w_wa,
                    k_k, k_a, r_k):
    cs = RWKV_CHUNK
    c = x_ref.shape[0]
    valid = jnp.minimum(c, seq_len - tile * c)
    row = lax.broadcasted_iota(I32, (c, LANES), 0)
    rowv = row < valid
    lane = lax.broadcasted_iota(I32, (c, LANES), 1)
    ones_pair = _head_block_ones(LANES)

    def shifted(lo):
        cols = slice(lo, lo + LANES)
        x = jnp.where(rowv, x_ref[:, cols], 0.0)
        prev_row = jnp.where(tile > 0, xp_ref[RWKV_HALO - 1:RWKV_HALO, cols], 0.0)
        next_row = jnp.where(tile < n_tiles - 1, xn_ref[0:1, cols], 0.0)
        x_prev = jnp.where(row == 0, prev_row, pltpu.roll(x, 1, 0))
        x_next = jnp.where(row == c - 1, next_row, pltpu.roll(x, c - 1, 0))
        xs = x + mu[:, cols] * (0.5 * (x_prev + x_next) - x)
        return jnp.where(rowv, xs, 0.0)

    wa = shifted(3 * width)
    g_lo = shifted(3 * width + LANES)
    xwa = jnp.where(lane < LANES // 2, jnp.tanh(wa), wa)
    la = _dot(xwa.astype(BF16), w_wa)

    t_i = lax.broadcasted_iota(I32, (c, c), 0)
    s_i = lax.broadcasted_iota(I32, (c, c), 1)
    tri = ((t_i // cs == s_i // cs) & ((t_i - s_i) * sign >= 0)).astype(BF16)

    pairs = []
    for p in range(width // LANES):
        lo = p * LANES
        cols = slice(lo, lo + LANES)
        r = shifted(lo)
        k = shifted(width + lo)
        v = shifted(2 * width + lo)
        w_log = -_softplus(-(w0[:, cols] + la[:, cols])) - 0.5
        logw = jnp.where(rowv, -jnp.exp(w_log), 0.0)
        a = jax.nn.sigmoid(a0[:, cols] + la[:, width + lo:width + lo + LANES])
        kk0 = k * k_k[:, cols]
        ss = _dot((kk0 * kk0).astype(BF16), ones_pair)
        kk = kk0 / jnp.maximum(jnp.sqrt(ss), 1e-12)
        kdir = k * (1.0 + (a - 1.0) * k_a[:, cols])
        b = kk * a
        cl = _exact_left(tri, logw)
        lasts = [cl[q * cs + cs - 1:q * cs + cs, :] if sign > 0 else cl[q * cs:q * cs + 1, :]
                 for q in range(c // cs)]
        last = jnp.concatenate([jnp.broadcast_to(lq, (cs, LANES)) for lq in lasts], axis=0)
        e_n = jnp.exp(-cl)
        pcr = jnp.exp(last - cl)
        pairs.append(dict(kkp=kk * jnp.exp(cl - logw), rp=r * jnp.exp(cl), ki=kdir * e_n, bi=b * e_n,
                          kipc=kdir * pcr, bipc=b * pcr, v=v, pc=[jnp.exp(lq) for lq in lasts],
                          bonus=_dot((r * kdir * r_k[:, cols]).astype(BF16), ones_pair) * v))
    return pairs, g_lo


def _rwkv_scan_kernel(xf_ref, xfp_ref, xfn_ref, xb_ref, xbp_ref, xbn_ref, mu_ref, w0_ref, a0_ref, wwa_ref,
                      gup_ref, kk_ref, ka_ref, rk_ref, yf_ref, yb_ref, bonf_ref, bonb_ref, g_ref, st_ref,
                      *, seq_len, width):
    i = pl.program_id(1)
    n_chunks = pl.num_programs(1)
    n_pairs = width // LANES

    @pl.when(i == 0)
    def _():
        st_ref[...] = jnp.zeros_like(st_ref)

    common = (mu_ref[...],)
    tail = (kk_ref[...], ka_ref[...], rk_ref[...])
    n_sub = xf_ref.shape[0] // RWKV_CHUNK
    fwd, g_lo = _rwkv_tile_prep(xf_ref, xfp_ref, xfn_ref, i, n_chunks, seq_len, width, 1, *common,
                                w0_ref[0], a0_ref[0], wwa_ref[0], *tail)
    bwd, _ = _rwkv_tile_prep(xb_ref, xbp_ref, xbn_ref, n_chunks - 1 - i, n_chunks, seq_len, width, -1, *common,
                             w0_ref[1], a0_ref[1], wwa_ref[1], *tail)
    g_ref[...] = _mm1(jax.nn.sigmoid(g_lo), gup_ref[...]).astype(g_ref.dtype)
    streams = []
    for di, (pairs, sign, bon_ref) in enumerate(((fwd, 1, bonf_ref), (bwd, -1, bonb_ref))):
        for p, s in enumerate(pairs):
            bon_ref[:, p * LANES:(p + 1) * LANES] = s.pop("bonus")
            s["st"] = st_ref[di * n_pairs + p]
            s["sign"] = sign
            streams.append(s)
    m = _rwkv_chunk_maps(streams, RWKV_CHUNK, n_sub)
    maps = [{q: m[j, q] for q in range(n_sub)} for j in range(len(streams))]
    res = _rwkv_apply_maps(streams, maps, RWKV_CHUNK, n_sub)
    for j, (y, st_new) in enumerate(res):
        di, p = divmod(j, n_pairs)
        cols = slice(p * LANES, (p + 1) * LANES)
        (yf_ref if di == 0 else yb_ref)[:, cols] = y
        st_ref[j] = st_new


def _rwkv_finish_kernel(yf_ref, yb_ref, bonf_ref, bonb_ref, g_ref, lg_ref, lb_ref, o_ref):
    y = yf_ref[...] + yb_ref[...]
    mean = _head_sums(y, exact=True) * (1.0 / HEAD_DIM)
    yc = y - mean
    var = _head_sums(yc * yc, exact=True) * (1.0 / HEAD_DIM)
    yn = yc * lax.rsqrt(var + RWKV_GN_EPS) * lg_ref[...] + lb_ref[...]
    o_ref[...] = ((yn + bonf_ref[...] + bonb_ref[...]) * g_ref[...].astype(F32)).astype(o_ref.dtype)


def rwkv_mix(rest, mu, w0, w_up, a0, a_up, g_up, k_k, k_a, r_k, lnx_g, lnx_b):
    bsz, l, n_cols = rest.shape
    width = w0.shape[1]
    rank = w_up.shape[1]
    assert n_cols == 3 * width + 2 * LANES and 2 * rank == LANES and l % RWKV_HALO == 0
    c = RWKV_CHUNK * RWKV_TILE_CHUNKS
    n_chunks = -(-l // c)
    per = c // RWKV_HALO
    n_halo = l // RWKV_HALO
    zeros = jnp.zeros((2, rank, width), F32)
    w_wa = jnp.concatenate([jnp.concatenate([w_up.astype(F32), zeros], axis=2),
                            jnp.concatenate([zeros, a_up.astype(F32)], axis=2)], axis=1)
    w_wa = w_wa.astype(BF16)

    fwd_chunk = lambda i: i
    bwd_chunk = lambda i: n_chunks - 1 - i

    def tile_specs(chunk_of):
        return [
            pl.BlockSpec((None, c, n_cols), lambda b, i: (b, chunk_of(i), 0)),
            pl.BlockSpec((None, RWKV_HALO, n_cols), lambda b, i: (b, jnp.maximum(chunk_of(i) * per - 1, 0), 0)),
            pl.BlockSpec((None, RWKV_HALO, n_cols),
                         lambda b, i: (b, jnp.minimum((chunk_of(i) + 1) * per, n_halo - 1), 0)),
        ]

    row2 = lambda a: a.astype(F32).reshape(1, -1)
    whole = lambda *shape: pl.BlockSpec(shape, lambda b, i: (0,) * len(shape))
    out_spec = lambda chunk_of: pl.BlockSpec((None, c, width), lambda b, i: (b, chunk_of(i), 0))
    act = lambda dt: jax.ShapeDtypeStruct((bsz, l, width), dt)
    y_f, y_b, bon_f, bon_b, g = pl.pallas_call(
        functools.partial(_rwkv_scan_kernel, seq_len=l, width=width),
        grid=(bsz, n_chunks),
        in_specs=tile_specs(fwd_chunk) + tile_specs(bwd_chunk) + [
            whole(1, n_cols),
            whole(2, 1, width), whole(2, 1, width),
            whole(2, LANES, 2 * width),
            whole(LANES, width),
            whole(1, width), whole(1, width), whole(1, width),
        ],
        out_specs=[out_spec(fwd_chunk), out_spec(bwd_chunk), out_spec(fwd_chunk), out_spec(bwd_chunk),
                   out_spec(fwd_chunk)],
        out_shape=[act(F32), act(F32), act(F32), act(F32), act(BF16)],
        scratch_shapes=[pltpu.VMEM((2 * (width // LANES), LANES, LANES), F32)],
        compiler_params=_cparams("parallel", "arbitrary"),
        name="rwkv_scan",
    )(rest, rest, rest, rest, rest, rest, row2(mu), w0.astype(F32).reshape(2, 1, width),
      a0.astype(F32).reshape(2, 1, width), w_wa, g_up.astype(BF16), row2(k_k), row2(k_a), row2(r_k))

    n = bsz * l
    tm = _row_tile(n, 608)
    rows = lambda: pl.BlockSpec((tm, width), lambda j: (j, 0))
    flat = lambda a: a.reshape(n, width)
    return pl.pallas_call(
        _rwkv_finish_kernel,
        grid=(n // tm,),
        in_specs=[rows(), rows(), rows(), rows(), rows(),
                  pl.BlockSpec((1, width), lambda j: (0, 0)),
                  pl.BlockSpec((1, width), lambda j: (0, 0))],
        out_specs=rows(),
        out_shape=jax.ShapeDtypeStruct((n, width), BF16),
        compiler_params=_cparams("parallel"),
        name="rwkv_finish",
    )(flat(y_f), flat(y_b), flat(bon_f), flat(bon_b), flat(g), row2(lnx_g), row2(lnx_b)).reshape(bsz, l, width)


S5_CHUNK = 16


def _cpow(n, lr, li, step):
    mag = jnp.exp(n * (lr * step))
    ang = n * (li * step)
    return mag * jnp.cos(ang), mag * jnp.sin(ang)


def _s5_param_kernel(lamr_ref, stepr_ref, bt_ref, ct_ref, kmat_ref, wst_ref, cexp_ref, alpha_ref):
    t_len = S5_CHUNK
    n_i = S5_GROUP_CH
    p2 = 2 * S5_STATE
    ti = t_len * n_i

    lr = lamr_ref[0:1, :]
    li = lamr_ref[1:2, :]
    step = jnp.exp(stepr_ref[...])
    ab_re, ab_im = _cpow(1.0, lr, li, step)
    den = lr * lr + li * li
    z_re = ((ab_re - 1.0) * lr + ab_im * li) / den
    z_im = (ab_im * lr - (ab_re - 1.0) * li) / den
    t16 = lax.broadcasted_iota(I32, (t_len, p2), 0).astype(F32)
    is_f = lax.broadcasted_iota(I32, (t_len, p2), 1) < S5_STATE

    def rows_by_t(x):
        return jnp.concatenate([jnp.broadcast_to(x[t:t + 1], (n_i, p2)) for t in range(t_len)], axis=0)

    def tiled_rows(x):
        return jnp.concatenate([x] * t_len, axis=0)

    def pow_rows(n):
        q_re, q_im = _cpow(n, lr, li, step)
        return rows_by_t(q_re), rows_by_t(q_im)

    bt_re = tiled_rows(bt_ref[0])
    bt_im = tiled_rows(bt_ref[1])
    bb_re = z_re * bt_re - z_im * bt_im
    bb_im = z_re * bt_im + z_im * bt_re
    pw_re, pw_im = pow_rows(jnp.where(is_f, (t_len - 1.0) - t16, t16))
    wst_ref[:, 0:p2] = (pw_re * bb_re - pw_im * bb_im).astype(wst_ref.dtype)
    wst_ref[:, p2:2 * p2] = (pw_re * bb_im + pw_im * bb_re).astype(wst_ref.dtype)
    al_re, al_im = _cpow(float(t_len), lr, li, step)
    alpha_ref[0:1, :] = al_re
    alpha_ref[1:2, :] = al_im

    ct_re = tiled_rows(ct_ref[0])
    ct_im = tiled_rows(ct_ref[1])

    def c_times_pow(n):
        q_re, q_im = pow_rows(n)
        return jnp.transpose(ct_re * q_re - ct_im * q_im), jnp.transpose(ct_re * q_im + ct_im * q_re)

    ca_re, ca_im = c_times_pow(jnp.where(is_f, t16, jnp.where(t16 == 0.0, 0.0, t_len - t16)))
    lane_p = lax.broadcasted_iota(I32, (n_i, p2), 1)
    bbr = bb_re[0:n_i]
    bbi = bb_im[0:n_i]
    zero = jnp.zeros_like(bbr)
    strips = []
    for sel in (lane_p < S5_STATE, lane_p >= S5_STATE):
        strips.append(_mm3(jnp.where(sel, bbr, zero), ca_re) - _mm3(jnp.where(sel, bbi, zero), ca_im))
    strip_f, strip_b = strips
    t_k = lax.broadcasted_iota(I32, (n_i, ti), 1) // n_i
    for tt in range(t_len):
        sf = strip_f if tt == 0 else pltpu.roll(strip_f, tt * n_i, 1)
        sb = strip_b if tt == 0 else pltpu.roll(strip_b, tt * n_i, 1)
        blk = jnp.where(t_k >= tt, sf, 0.0) + jnp.where(t_k <= tt, sb, 0.0)
        kmat_ref[tt * n_i:(tt + 1) * n_i, :] = blk.astype(kmat_ref.dtype)

    co_re, co_im = c_times_pow(jnp.where(is_f, t16 + 1.0, t_len - t16))
    cexp_ref[0:p2, :] = co_re.astype(cexp_ref.dtype)
    cexp_ref[p2:2 * p2, :] = (-co_im).astype(cexp_ref.dtype)


def _s5_main_kernel(u_ref, kmat_ref, wst_ref, cexp_ref, alpha_ref, y_ref, x_ref, sf_ref, sb_ref,
                    *, n_batch, n_chunks):
    p2 = 2 * S5_STATE
    u = u_ref[...].astype(BF16)
    x_ref[...] = _dot(u, wst_ref[...])
    a_re = alpha_ref[0:1, :]
    a_im = alpha_ref[1:2, :]
    lane = lax.broadcasted_iota(I32, (1, p2), 1)
    is_f = lane < S5_STATE

    sub = S5_SCAN_ROWS
    assert n_chunks % sub == 0

    def step(k, carry):
        new = []
        for b in range(n_batch):
            s_re, s_im = carry[b]
            row_f = pl.multiple_of(b * n_chunks + sub * k, sub)
            row_b = pl.multiple_of(b * n_chunks + (n_chunks - sub) - sub * k, sub)
            xf = x_ref[pl.ds(row_f, sub), :]
            xb = x_ref[pl.ds(row_b, sub), :]
            seen = []
            for r in range(sub):
                seen.append(jnp.concatenate([s_re, s_im], axis=1))
                rb = sub - 1 - r
                x_re = jnp.where(is_f, xf[r:r + 1, 0:p2], xb[rb:rb + 1, 0:p2])
                x_im = jnp.where(is_f, xf[r:r + 1, p2:2 * p2], xb[rb:rb + 1, p2:2 * p2])
                s_re, s_im = a_re * s_re - a_im * s_im + x_re, a_re * s_im + a_im * s_re + x_im
            sf_ref[pl.ds(row_f, sub), :] = jnp.concatenate(seen, axis=0)
            sb_ref[pl.ds(row_b, sub), :] = jnp.concatenate(seen[::-1], axis=0)
            new.append((s_re, s_im))
        return tuple(new)

    zero = jnp.zeros((1, p2), F32)
    lax.fori_loop(0, n_chunks // sub, step, tuple((zero, zero) for _ in range(n_batch)))
    lane2 = lax.broadcasted_iota(I32, sf_ref.shape, 1) % p2
    s_in = jnp.where(lane2 < S5_STATE, sf_ref[...], sb_ref[...])
    s_hi, s_lo = _split_bf16(s_in)
    y_ref[...] = _dot(u, kmat_ref[...]) + _dot(s_hi, cexp_ref[...]) + _dot(s_lo, cexp_ref[...])


S5_RELAYOUT_CHUNKS = 128
S5_SCAN_ROWS = 8


def _s5_group_major_kernel(h_ref, g_ref, u_ref, hn_ref, ut_ref, *, seq_len):
    n_g, mt, ti = u_ref.shape
    t_len = S5_CHUNK
    n_i = ti // t_len
    n_lt = hn_ref.shape[0]
    g_lt = LANES // n_i
    rows = h_ref.shape[0]
    valid = seq_len - pl.program_id(1) * rows
    row = lax.broadcasted_iota(I32, h_ref.shape, 0)
    hn = jnp.where(row < valid, _rms(h_ref[...], g_ref[...]), 0.0)
    for j in range(n_lt):
        hn_ref[j] = hn[:, j * LANES:(j + 1) * LANES]
    for tau in range(t_len):
        for j in range(n_lt):
            xt = jnp.transpose(hn_ref[j, pl.ds(tau, mt, stride=t_len), :])
            ut_ref[j * g_lt:(j + 1) * g_lt, tau * n_i:(tau + 1) * n_i, :] = xt.reshape(g_lt, n_i, mt)
    for g in range(n_g):
        u_ref[g] = jnp.transpose(ut_ref[g]).astype(u_ref.dtype)


def _s5_token_major_kernel(y_ref, o_ref, zt_ref, z_ref):
    n_g, mt, ti = y_ref.shape
    t_len = S5_CHUNK
    n_i = ti // t_len
    n_lt = z_ref.shape[0]
    for g in range(n_g):
        yt = jnp.transpose(y_ref[g])
        zt_ref[:, g * n_i:(g + 1) * n_i, :] = yt.reshape(t_len, n_i, mt)
    for t in range(t_len):
        for j in range(n_lt):
            z_ref[j, pl.ds(t, mt, stride=t_len), :] = jnp.transpose(zt_ref[t, j * LANES:(j + 1) * LANES, :])
    for j in range(n_lt):
        o_ref[:, j * LANES:(j + 1) * LANES] = z_ref[j]


def _gelu_tanh(x):
    return 0.5 * x * (1.0 + jnp.tanh(math.sqrt(2.0 / math.pi) * (x + 0.044715 * (x * x * x))))


def _s5_glu_kernel(h_ref, y_ref, g_ref, d_ref, w_ref, o_ref):
    h = h_ref[...]
    dm = h.shape[1]
    y = y_ref[...] + d_ref[...] * _rms(h, g_ref[...])
    gl = _gelu_tanh(y).astype(BF16)
    a = _dot(gl, w_ref[:, 0:dm])
    b = _dot(gl, w_ref[:, dm:2 * dm])
    o_ref[...] = h + a * jax.nn.sigmoid(b)


def s5_mix(h3, gain, b_re, b_im, lam_re, lam_im, log_step, c_re, c_im, d_skip, w_glu):
    bsz, l, dm = h3.shape
    n_g, n_p, n_i = b_re.shape
    t_len = S5_CHUNK
    assert l % t_len == 0 and n_g * n_i == dm and n_p == S5_STATE and n_i == S5_GROUP_CH
    n_chunks = -(-(l // t_len) // S5_SCAN_ROWS) * S5_SCAN_ROWS
    m = bsz * n_chunks
    ti = t_len * n_i
    p2 = 2 * n_p
    n = bsz * l
    tm = _row_tile(n, 608)
    h2 = h3.reshape(n, dm)
    gain2 = gain.astype(F32).reshape(1, dm)

    mt = min(S5_RELAYOUT_CHUNKS, n_chunks)
    n_tiles = -(-n_chunks // mt)
    u = pl.pallas_call(
        functools.partial(_s5_group_major_kernel, seq_len=l),
        grid=(bsz, n_tiles),
        in_specs=[pl.BlockSpec((None, mt * t_len, dm), lambda b, i: (b, i, 0)),
                  pl.BlockSpec((1, dm), lambda b, i: (0, 0))],
        out_specs=pl.BlockSpec((n_g, None, mt, ti), lambda b, i: (0, b, i, 0)),
        out_shape=jax.ShapeDtypeStruct((n_g, bsz, n_chunks, ti), BF16),
        scratch_shapes=[pltpu.VMEM((dm // LANES, mt * t_len, LANES), F32), pltpu.VMEM((n_g, ti, mt), F32)],
        compiler_params=_cparams("parallel", "parallel"),
        name="s5_group_major",
    )(h3, gain2).reshape(n_g, m, ti)

    f32 = lambda a: a.astype(F32)
    lam_r = jnp.stack([jnp.concatenate([f32(lam_re)[0], f32(lam_re)[1]], axis=-1),
                       jnp.concatenate([f32(lam_im)[0], f32(lam_im)[1]], axis=-1)], axis=1)
    step_r = jnp.repeat(jnp.transpose(f32(log_step))[:, None, :], n_p, axis=2)
    bt = jnp.stack([jnp.transpose(f32(b_re), (0, 2, 1)), jnp.transpose(f32(b_im), (0, 2, 1))], axis=1)
    bt = jnp.tile(bt, (1, 1, 1, 2))
    ct = jnp.stack([f32(c_re), f32(c_im)], axis=0)
    ct = jnp.transpose(ct, (2, 0, 3, 1, 4)).reshape(n_g, 2, n_i, p2)

    gspec = lambda *shape: pl.BlockSpec((None,) + shape, lambda g: (g,) + (0,) * len(shape))
    kmat, wst, cexp, alpha = pl.pallas_call(
        _s5_param_kernel,
        grid=(n_g,),
        in_specs=[gspec(2, p2), gspec(1, p2), gspec(2, n_i, p2), gspec(2, n_i, p2)],
        out_specs=[gspec(ti, ti), gspec(ti, 2 * p2), gspec(2 * p2, ti), gspec(2, p2)],
        out_shape=[
            jax.ShapeDtypeStruct((n_g, ti, ti), BF16),
            jax.ShapeDtypeStruct((n_g, ti, 2 * p2), BF16),
            jax.ShapeDtypeStruct((n_g, 2 * p2, ti), BF16),
            jax.ShapeDtypeStruct((n_g, 2, p2), F32),
        ],
        compiler_params=_cparams("parallel"),
        name="s5_params",
    )(lam_r, step_r, bt, ct)

    y = pl.pallas_call(
        functools.partial(_s5_main_kernel, n_batch=bsz, n_chunks=n_chunks),
        grid=(n_g,),
        in_specs=[gspec(m, ti), gspec(ti, ti), gspec(ti, 2 * p2), gspec(2 * p2, ti), gspec(2, p2)],
        out_specs=gspec(m, ti),
        out_shape=jax.ShapeDtypeStruct((n_g, m, ti), F32),
        scratch_shapes=[pltpu.VMEM((m, 2 * p2), F32), pltpu.VMEM((m, 2 * p2), F32), pltpu.VMEM((m, 2 * p2), F32)],
        compiler_params=_cparams("parallel"),
        name="s5_main",
    )(u, kmat, wst, cexp, alpha)
    y2 = pl.pallas_call(
        _s5_token_major_kernel,
        grid=(bsz, n_tiles),
        in_specs=[pl.BlockSpec((n_g, None, mt, ti), lambda b, i: (0, b, i, 0))],
        out_specs=pl.BlockSpec((None, mt * t_len, dm), lambda b, i: (b, i, 0)),
        out_shape=jax.ShapeDtypeStruct((bsz, l, dm), F32),
        scratch_shapes=[pltpu.VMEM((t_len, dm, mt), F32), pltpu.VMEM((dm // LANES, mt * t_len, LANES), F32)],
        compiler_params=_cparams("parallel", "parallel"),
        name="s5_token_major",
    )(y.reshape(n_g, bsz, n_chunks, ti)).reshape(n, dm)

    out = pl.pallas_call(
        _s5_glu_kernel,
        grid=(n // tm,),
        in_specs=[
            pl.BlockSpec((tm, dm), lambda i: (i, 0)),
            pl.BlockSpec((tm, dm), lambda i: (i, 0)),
            pl.BlockSpec((1, dm), lambda i: (0, 0)),
            pl.BlockSpec((1, dm), lambda i: (0, 0)),
            pl.BlockSpec((dm, 2 * dm), lambda i: (0, 0)),
        ],
        out_specs=pl.BlockSpec((tm, dm), lambda i: (i, 0)),
        out_shape=jax.ShapeDtypeStruct((n, dm), F32),
        compiler_params=_cparams("parallel"),
        name="s5_glu",
    )(h2, y2, gain2, f32(d_skip).reshape(1, dm), w_glu.astype(BF16))
    return out.reshape(bsz, l, dm)


def na_rwkv_mix(h3, gain, w_in, w_out, rpb, mu, w0, w_up, a0, a_up, g_up, k_k, k_a, r_k, lnx_g, lnx_b):
    bsz, l, dm = h3.shape
    n = bsz * l
    h2 = h3.reshape(n, dm)
    n_qkv = 3 * (w_out.shape[0] // 2)
    qkv, rest = norm_inproj(h2, gain.astype(F32), w_in.astype(BF16), n_qkv)
    na = na_attention(qkv.reshape(bsz, l, n_qkv), rpb)
    rw = rwkv_mix(rest.reshape(bsz, l, -1), mu, w0, w_up, a0, a_up, g_up, k_k, k_a, r_k, lnx_g, lnx_b)
    out = outproj_residual(h2, na.reshape(n, -1), rw.reshape(n, -1), w_out.astype(BF16))
    return out.reshape(bsz, l, dm)


def kernel(x, meta_tokens, norm_mix, norm_ffn, norm_final, mix_w_in, mix_w_out, na_rpb, rwkv_mu,
           rwkv_w0, rwkv_w_up, rwkv_a0, rwkv_a_up, rwkv_g_up, rwkv_k_k, rwkv_k_a, rwkv_r_k,
           rwkv_lnx_g, rwkv_lnx_b, s5_b_re, s5_b_im, s5_lambda_re, s5_lambda_im, s5_log_step,
           s5_c_re, s5_c_im, s5_d, s5_w_glu, moe_w_group, moe_b_group, moe_w_expert, moe_b_expert,
           moe_w1, moe_w3, moe_w2):
    bsz, _, dm = x.shape
    depth = norm_mix.shape[0]
    meta = jnp.broadcast_to(meta_tokens.astype(x.dtype)[None], (bsz,) + meta_tokens.shape)
    h = jnp.concatenate([meta, x], axis=1)
    l = h.shape[1]
    for layer in range(depth):
        i = layer // 2
        if layer % 2 == 0:
            h = na_rwkv_mix(h, norm_mix[layer], mix_w_in[i], mix_w_out[i], na_rpb[i], rwkv_mu[i], rwkv_w0[i],
                            rwkv_w_up[i], rwkv_a0[i], rwkv_a_up[i], rwkv_g_up[i], rwkv_k_k[i], rwkv_k_a[i],
                            rwkv_r_k[i], rwkv_lnx_g[i], rwkv_lnx_b[i])
        else:
            h = s5_mix(h, norm_mix[layer], s5_b_re[i], s5_b_im[i], s5_lambda_re[i], s5_lambda_im[i],
                       s5_log_step[i], s5_c_re[i], s5_c_im[i], s5_d[i], s5_w_glu[i])
        h = hierarchical_moe_residual(h.reshape(bsz * l, dm), norm_ffn[layer].astype(F32), moe_w_group[layer],
                                      moe_b_group[layer], moe_w_expert[layer], moe_b_expert[layer],
                                      moe_w1, moe_w3, moe_w2, layer).reshape(bsz, l, dm)
    return final_norm(h, norm_final.astype(F32))
```

```python
import functools
import math

import jax
import jax.numpy as jnp
from jax import lax
from jax.experimental import pallas as pl
from jax.experimental.pallas import tpu as pltpu

F32 = jnp.float32
BF16 = jnp.bfloat16
I32 = jnp.int32

N_META = 16
GRID_W = 64
HEAD_DIM = 64
NA_WIN_ROWS = 8
NA_WIN_COLS = 16
S5_GROUP_CH = 16
S5_STATE = 64
MOE_GROUPS = 4
MOE_PER_GROUP = 8
MOE_EXPERTS = MOE_GROUPS * MOE_PER_GROUP
NORM_EPS = 1e-6
RWKV_GN_EPS = 64e-5
NEG_INF = -1e30

LANES = 128
SUBLANES_BF16 = 16
VMEM_LIMIT_BYTES = 56 * 1024 * 1024

MOE_TILE = 256
MOE_TOKEN_TILE = 320
MOE_STAGE_SLOTS = 3
ROUTER_LANES = 128


def _cparams(*sem):
    return pltpu.CompilerParams(dimension_semantics=sem, vmem_limit_bytes=VMEM_LIMIT_BYTES)


def _row_tile(n, target):
    best = None
    for t in range(SUBLANES_BF16, min(n, target) + 1, SUBLANES_BF16):
        if n % t == 0:
            best = t
    assert best is not None, (n, target)
    return best


def _rms(x, gain):
    ms = jnp.mean(x * x, axis=-1, keepdims=True)
    return (x * lax.rsqrt(ms + NORM_EPS)) * gain


def _split_bf16(x):
    hi = x.astype(BF16)
    lo = (x - hi.astype(F32)).astype(BF16)
    return hi, lo


def _dot(a, b):
    return jnp.dot(a, b, preferred_element_type=F32)


def _dot_nt(a, b):
    return lax.dot_general(a, b, (((1,), (1,)), ((), ())), preferred_element_type=F32)


def _norm_inproj_kernel(h_ref, g_ref, w_ref, qkv_ref, rest_ref, xn_ref, *, n_qkv, chunk):
    xn_ref[...] = _rms(h_ref[...], g_ref[...]).astype(BF16)
    n_all = w_ref.shape[1]
    for c in range(0, n_all, chunk):
        y = _dot(xn_ref[...], w_ref[:, c:c + chunk])
        if c < n_qkv:
            qkv_ref[:, c:c + chunk] = y.astype(BF16)
        else:
            rest_ref[:, c - n_qkv:c - n_qkv + chunk] = y


def norm_inproj(h2, gain, w_bf16, n_qkv):
    n, d = h2.shape
    n_all = w_bf16.shape[1]
    tm = _row_tile(n, 608)
    chunk = 256
    assert n_qkv % chunk == 0 and n_all % chunk == 0
    return pl.pallas_call(
        functools.partial(_norm_inproj_kernel, n_qkv=n_qkv, chunk=chunk),
        grid=(n // tm,),
        in_specs=[
            pl.BlockSpec((tm, d), lambda i: (i, 0)),
            pl.BlockSpec((1, d), lambda i: (0, 0)),
            pl.BlockSpec((d, n_all), lambda i: (0, 0)),
        ],
        out_specs=[
            pl.BlockSpec((tm, n_qkv), lambda i: (i, 0)),
            pl.BlockSpec((tm, n_all - n_qkv), lambda i: (i, 0)),
        ],
        out_shape=[
            jax.ShapeDtypeStruct((n, n_qkv), BF16),
            jax.ShapeDtypeStruct((n, n_all - n_qkv), F32),
        ],
        scratch_shapes=[pltpu.VMEM((tm, d), BF16)],
        compiler_params=_cparams("parallel"),
        name="norm_inproj",
    )(h2, gain.reshape(1, d), w_bf16)


def _outproj_kernel(h_ref, na_ref, rw_ref, wa_ref, wb_ref, o_ref):
    acc = _dot(na_ref[...], wa_ref[...])
    acc = acc + _dot(rw_ref[...], wb_ref[...])
    o_ref[...] = h_ref[...] + acc


def outproj_residual(h2, na, rw, w_out_bf16):
    n, d = h2.shape
    ka, kb = na.shape[1], rw.shape[1]
    tm = _row_tile(n, 608)
    return pl.pallas_call(
        _outproj_kernel,
        grid=(n // tm,),
        in_specs=[
            pl.BlockSpec((tm, d), lambda i: (i, 0)),
            pl.BlockSpec((tm, ka), lambda i: (i, 0)),
            pl.BlockSpec((tm, kb), lambda i: (i, 0)),
            pl.BlockSpec((ka, d), lambda i: (0, 0)),
            pl.BlockSpec((kb, d), lambda i: (0, 0)),
        ],
        out_specs=pl.BlockSpec((tm, d), lambda i: (i, 0)),
        out_shape=jax.ShapeDtypeStruct((n, d), F32),
        compiler_params=_cparams("parallel"),
        name="outproj_residual",
    )(h2, na, rw, w_out_bf16[:ka], w_out_bf16[ka:])


def _store_token_tiles(ref, x):
    rows = x.shape[0]
    s_n = x.shape[1] // LANES
    for s in range(s_n):
        ref[pl.ds(s, rows, stride=s_n), :] = x[:, s * LANES:(s + 1) * LANES]


def _load_token_tile_cols(ref, s, rows, s_n):
    return ref[pl.ds(s, rows, stride=s_n), :]


def _router_kernel(h_ref, g_ref, whi_ref, wlo_ref, b_ref, xn_ref, route_ref):
    xn = _rms(h_ref[...], g_ref[...])
    x_hi, x_lo = _split_bf16(xn)
    _store_token_tiles(xn_ref, xn)
    logits = (_dot(x_hi, whi_ref[...]) + _dot(x_hi, wlo_ref[...]) + _dot(x_lo, whi_ref[...])
              + b_ref[...])
    tm = logits.shape[0]
    lane = lax.broadcasted_iota(I32, (tm, ROUTER_LANES), 1)
    big = jnp.int32(ROUTER_LANES)

    is_g = lane < MOE_GROUPS
    lg = jnp.where(is_g, logits, -jnp.inf)
    eg = jnp.where(is_g, jnp.exp(lg - jnp.max(lg, axis=-1, keepdims=True)), 0.0)
    pg = eg / jnp.sum(eg, axis=-1, keepdims=True)
    p_grp = jnp.max(pg, axis=-1, keepdims=True)
    grp = jnp.min(jnp.where(is_g & (pg == p_grp), lane, big), axis=-1, keepdims=True)

    lo_lane = MOE_GROUPS + MOE_PER_GROUP * grp
    is_e = (lane >= lo_lane) & (lane < lo_lane + MOE_PER_GROUP)
    le = jnp.where(is_e, logits, -jnp.inf)
    ee = jnp.where(is_e, jnp.exp(le - jnp.max(le, axis=-1, keepdims=True)), 0.0)
    pe = jnp.where(is_e, ee / jnp.sum(ee, axis=-1, keepdims=True), -1.0)
    p1 = jnp.max(pe, axis=-1, keepdims=True)
    i1 = jnp.min(jnp.where(pe == p1, lane, big), axis=-1, keepdims=True)
    pe2 = jnp.where(lane == i1, -1.0, pe)
    p2 = jnp.max(pe2, axis=-1, keepdims=True)
    i2 = jnp.min(jnp.where(pe2 == p2, lane, big), axis=-1, keepdims=True)
    denom = p1 + p2
    g1 = p_grp * p1 / denom
    g2 = p_grp * p2 / denom
    e1 = (i1 - MOE_GROUPS).astype(F32)
    e2 = (i2 - MOE_GROUPS).astype(F32)
    route_ref[...] = jnp.where(lane == 0, e1, jnp.where(lane == 1, e2, jnp.where(lane == 2, g1, g2)))


def moe_router(h2, gain, w_group, b_group, w_expert, b_expert):
    n, d = h2.shape
    n_r = MOE_GROUPS + MOE_EXPERTS
    w_r = jnp.concatenate([w_group, jnp.transpose(w_expert, (1, 0, 2)).reshape(d, MOE_EXPERTS)], axis=1)
    w_r = jnp.pad(w_r.astype(F32), ((0, 0), (0, ROUTER_LANES - n_r)))
    w_hi, w_lo = _split_bf16(w_r)
    b_r = jnp.pad(jnp.concatenate([b_group, b_expert.reshape(-1)]).astype(F32), (0, ROUTER_LANES - n_r))
    tm = _row_tile(n, 608)
    return pl.pallas_call(
        _router_kernel,
        grid=(n // tm,),
        in_specs=[
            pl.BlockSpec((tm, d), lambda i: (i, 0)),
            pl.BlockSpec((1, d), lambda i: (0, 0)),
            pl.BlockSpec((d, ROUTER_LANES), lambda i: (0, 0)),
            pl.BlockSpec((d, ROUTER_LANES), lambda i: (0, 0)),
            pl.BlockSpec((1, ROUTER_LANES), lambda i: (0, 0)),
        ],
        out_specs=[
            pl.BlockSpec((tm * (d // LANES), LANES), lambda i: (i, 0)),
            pl.BlockSpec((tm, ROUTER_LANES), lambda i: (i, 0)),
        ],
        out_shape=[
            jax.ShapeDtypeStruct((n * (d // LANES), LANES), F32),
            jax.ShapeDtypeStruct((n, ROUTER_LANES), F32),
        ],
        compiler_params=_cparams("parallel"),
        name="moe_router",
    )(h2, gain.reshape(1, d), w_hi, w_lo, b_r.reshape(1, ROUTER_LANES))


def _moe_dispatch_kernel(tail_ref, n_used_ref, dst_ref, xn_hbm, xbuf_hbm, zero_ref, stage_ref, sem, lsem, zsem,
                         *, tm, s_n, n_blocks):
    i = pl.program_id(0)
    n_steps = pl.num_programs(0)
    tile_rows = zero_ref.shape[0]

    def zero_block(b, carry):
        pltpu.make_async_copy(zero_ref, xbuf_hbm.at[pl.ds(pl.multiple_of(b * tile_rows, tile_rows), tile_rows), :],
                              zsem).start()
        return carry

    def wait_zero_block(b, carry):
        pltpu.make_async_copy(zero_ref, xbuf_hbm.at[pl.ds(0, tile_rows), :], zsem).wait()
        return carry

    n_slots = stage_ref.shape[0]
    slot = i % n_slots
    nxt = (i + 1) % n_slots

    def load(step, s):
        return pltpu.make_async_copy(xn_hbm.at[pl.ds(pl.multiple_of(step * tm * s_n, s_n), tm * s_n), :],
                                     stage_ref.at[s], lsem.at[s])

    def wait_rows_out(s):
        for _ in range(2):
            pltpu.make_async_copy(stage_ref.at[s], xbuf_hbm.at[pl.ds(0, tm * s_n), :], sem.at[s]).wait()

    @pl.when(i == 0)
    def _():
        load(0, 0).start()
        zero_ref[...] = jnp.zeros_like(zero_ref)
        for e in range(tail_ref.shape[0]):
            @pl.when(tail_ref[e] >= 0)
            def _():
                pltpu.make_async_copy(zero_ref, xbuf_hbm.at[pl.ds(pl.multiple_of(tail_ref[e], s_n), tile_rows), :],
                                      zsem).start()
        lax.fori_loop(n_used_ref[0], n_blocks, zero_block, 0)
        for e in range(tail_ref.shape[0]):
            @pl.when(tail_ref[e] >= 0)
            def _():
                wait_zero_block(0, 0)
        lax.fori_loop(n_used_ref[0], n_blocks, wait_zero_block, 0)

    @pl.when(i >= n_slots - 1)
    def _():
        wait_rows_out(nxt)

    @pl.when(i + 1 < n_steps)
    def _():
        load(i + 1, nxt).start()

    load(i, slot).wait()
    for r in range(tm):
        src = stage_ref.at[slot, pl.ds(r * s_n, s_n), :]
        for k in range(2):
            dst = pl.multiple_of(dst_ref[k, r], s_n)
            pltpu.make_async_copy(src, xbuf_hbm.at[pl.ds(dst, s_n), :], sem.at[slot]).start()

    @pl.when(i == n_steps - 1)
    def _():
        for back in range(n_slots - 1):
            @pl.when(i >= back)
            def _():
                wait_rows_out((i - back) % n_slots)


def moe_dispatch(xn_tiles, dst_tiles, tail_start, n_used, n_blocks, tm, s_n):
    n_steps = dst_tiles.shape[0]
    n_rows = n_blocks * MOE_TILE
    assert n_steps * tm * s_n == xn_tiles.shape[0]
    grid_spec = pltpu.PrefetchScalarGridSpec(
        num_scalar_prefetch=2,
        grid=(n_steps,),
        in_specs=[
            pl.BlockSpec((None, 2, tm), lambda i, tail, nu: (i, 0, 0), memory_space=pltpu.SMEM),
            pl.BlockSpec(memory_space=pl.ANY),
        ],
        out_specs=pl.BlockSpec(memory_space=pl.ANY),
        scratch_shapes=[
            pltpu.VMEM((MOE_TILE * s_n, LANES), F32),
            pltpu.VMEM((MOE_STAGE_SLOTS, tm * s_n, LANES), F32),
            pltpu.SemaphoreType.DMA((MOE_STAGE_SLOTS,)),
            pltpu.SemaphoreType.DMA((MOE_STAGE_SLOTS,)),
            pltpu.SemaphoreType.DMA(()),
        ],
    )
    return pl.pallas_call(
        functools.partial(_moe_dispatch_kernel, tm=tm, s_n=s_n, n_blocks=n_blocks),
        grid_spec=grid_spec,
        out_shape=jax.ShapeDtypeStruct((n_rows * s_n, LANES), F32),
        compiler_params=_cparams("arbitrary"),
        name="moe_dispatch",
    )(tail_start, n_used, dst_tiles, xn_tiles)


def _expert_kernel(blk_e_ref, n_used_ref, x_ref, w1_ref, w3_ref, w2_ref, y_ref, xb_ref, w1b_ref, w3b_ref, w2b_ref):
    i = pl.program_id(0)
    used = i < n_used_ref[0]
    tile, d = xb_ref.shape
    s_n = d // LANES
    prev_e = blk_e_ref[jnp.maximum(i - 1, 0)]
    fresh = (i == 0) | (blk_e_ref[i] != prev_e)

    @pl.when(used & fresh)
    def _():
        w1b_ref[...] = w1_ref[...].astype(BF16)
        w3b_ref[...] = w3_ref[...].astype(BF16)
        w2b_ref[...] = w2_ref[...].astype(BF16)

    @pl.when(used)
    def _():
        for s in range(s_n):
            xb_ref[:, s * LANES:(s + 1) * LANES] = _load_token_tile_cols(x_ref, s, tile, s_n).astype(BF16)
        x = xb_ref[...]
        a = _dot(x, w1b_ref[...])
        b = _dot(x, w3b_ref[...])
        hmid = (a * jax.nn.sigmoid(a) * b).astype(BF16)
        _store_token_tiles(y_ref, _dot(hmid, w2b_ref[...]))

    @pl.when(jnp.logical_not(used))
    def _():
        y_ref[...] = jnp.zeros_like(y_ref)


def moe_experts(xbuf, blk_e, n_used, w1, w3, w2, layer, n_blocks):
    d, f = w1.shape[2], w1.shape[3]
    s_n = d // LANES
    tile = MOE_TILE

    def w_map(i, blk_e_ref, n_used_ref):
        return (layer, blk_e_ref[i], 0, 0)

    def x_map(i, blk_e_ref, n_used_ref):
        return (jnp.minimum(i, jnp.maximum(n_used_ref[0] - 1, 0)), 0)

    grid_spec = pltpu.PrefetchScalarGridSpec(
        num_scalar_prefetch=2,
        grid=(n_blocks,),
        in_specs=[
            pl.BlockSpec((tile * s_n, LANES), x_map),
            pl.BlockSpec((None, None, d, f), w_map),
            pl.BlockSpec((None, None, d, f), w_map),
            pl.BlockSpec((None, None, f, d), w_map),
        ],
        out_specs=pl.BlockSpec((tile * s_n, LANES), lambda i, be, nu: (i, 0)),
        scratch_shapes=[
            pltpu.VMEM((tile, d), BF16),
            pltpu.VMEM((d, f), BF16),
            pltpu.VMEM((d, f), BF16),
            pltpu.VMEM((f, d), BF16),
        ],
    )
    return pl.pallas_call(
        _expert_kernel,
        grid_spec=grid_spec,
        out_shape=jax.ShapeDtypeStruct((n_blocks * tile * s_n, LANES), F32),
        compiler_params=_cparams("arbitrary"),
        name="moe_experts",
    )(blk_e, n_used, xbuf, w1, w3, w2)


def _moe_combine_kernel(src_ref, src_next_ref, h_ref, route_ref, y_hbm, o_ref, yg_ref, sem, *, n_steps):
    i = pl.program_id(0)
    slot = i & 1
    tm, d = h_ref.shape
    s_n = d // LANES

    def start_gather(ids_ref, dst_slot):
        for k in range(2):
            for r in range(tm):
                src = pl.multiple_of(ids_ref[k, r], s_n)
                pltpu.make_async_copy(y_hbm.at[pl.ds(src, s_n), :],
                                      yg_ref.at[dst_slot, k, pl.ds(r * s_n, s_n), :], sem.at[dst_slot]).start()

    @pl.when(i == 0)
    def _():
        start_gather(src_ref, 0)

    @pl.when(i + 1 < n_steps)
    def _():
        start_gather(src_next_ref, 1 - slot)

    for k in range(2):
        pltpu.make_async_copy(y_hbm.at[pl.ds(0, tm * s_n), :], yg_ref.at[slot, k], sem.at[slot]).wait()
    route = route_ref[...]
    g1 = route[:, 2:3]
    g2 = route[:, 3:4]
    for s in range(s_n):
        cols = slice(s * LANES, (s + 1) * LANES)
        o_ref[:, cols] = (h_ref[:, cols] + g1 * _load_token_tile_cols(yg_ref.at[slot, 0], s, tm, s_n)
                          + g2 * _load_token_tile_cols(yg_ref.at[slot, 1], s, tm, s_n))


def moe_combine(h2, route, y_tiles, src_tiles, tm):
    n, d = h2.shape
    s_n = d // LANES
    n_steps = src_tiles.shape[0]
    ids = lambda index: pl.BlockSpec((None, 2, tm), lambda i: (index(i), 0, 0), memory_space=pltpu.SMEM)
    return pl.pallas_call(
        functools.partial(_moe_combine_kernel, n_steps=n_steps),
        grid=(n_steps,),
        in_specs=[
            ids(lambda i: i),
            ids(lambda i: jnp.minimum(i + 1, n_steps - 1)),
            pl.BlockSpec((tm, d), lambda i: (i, 0)),
            pl.BlockSpec((tm, ROUTER_LANES), lambda i: (i, 0)),
            pl.BlockSpec(memory_space=pl.ANY),
        ],
        out_specs=pl.BlockSpec((tm, d), lambda i: (i, 0)),
        out_shape=jax.ShapeDtypeStruct((n, d), F32),
        scratch_shapes=[pltpu.VMEM((2, 2, tm * s_n, LANES), F32), pltpu.SemaphoreType.DMA((2,))],
        compiler_params=_cparams("arbitrary"),
        name="moe_combine",
    )(src_tiles, src_tiles, h2, route, y_tiles)


def hierarchical_moe_residual(h2, gain, w_group, b_group, w_expert, b_expert, w1, w3, w2, layer):
    n, d = h2.shape
    xn, route = moe_router(h2, gain, w_group, b_group, w_expert, b_expert)
    e_km = jnp.concatenate([route[:, 0], route[:, 1]]).astype(I32)
    n_assign = 2 * n
    onehot = (e_km[:, None] == jnp.arange(MOE_EXPERTS, dtype=I32)[None, :]).astype(I32)
    csum = jnp.cumsum(onehot, axis=0)
    counts = csum[-1]
    padded = (counts + MOE_TILE - 1) // MOE_TILE * MOE_TILE
    pad_end = jnp.cumsum(padded)
    pad_start = pad_end - padded
    dest = jnp.sum((csum - onehot + pad_start[None, :]) * onehot, axis=1)
    n_blocks = -(-n_assign // MOE_TILE) + MOE_EXPERTS
    blk_start = jnp.arange(n_blocks, dtype=I32) * MOE_TILE
    blk_e = jnp.minimum(jnp.sum((pad_end[None, :] <= blk_start[:, None]).astype(I32), axis=1),
                        MOE_EXPERTS - 1).astype(I32)
    n_used = (pad_end[-1] // MOE_TILE).astype(I32).reshape(1)
    s_n = d // LANES
    tm = _row_tile(n, MOE_TOKEN_TILE)
    dest_tiles = jnp.transpose((dest * s_n).astype(I32).reshape(2, n // tm, tm), (1, 0, 2))
    tail_start = jnp.where(counts > 0, (pad_end - MOE_TILE) * s_n, -1).astype(I32)
    xbuf = moe_dispatch(xn, dest_tiles, tail_start, n_used, n_blocks, tm, s_n)
    y = moe_experts(xbuf, blk_e, n_used, w1, w3, w2, layer, n_blocks)
    return moe_combine(h2, route, y, dest_tiles, tm)


def _final_norm_kernel(h_ref, g_ref, o_ref):
    o_ref[...] = _rms(h_ref[...], g_ref[...])


def final_norm(h3, gain):
    b, l, d = h3.shape
    t = l - N_META
    tm = _row_tile(t, 512)
    return pl.pallas_call(
        _final_norm_kernel,
        grid=(b, t // tm),
        in_specs=[
            pl.BlockSpec((None, pl.Element(tm), pl.Element(d)),
                         lambda bi, i: (bi, pl.multiple_of(N_META + i * tm, SUBLANES_BF16), 0)),
            pl.BlockSpec((1, d), lambda bi, i: (0, 0)),
        ],
        out_specs=pl.BlockSpec((None, tm, d), lambda bi, i: (bi, i, 0)),
        out_shape=jax.ShapeDtypeStruct((b, t, d), F32),
        compiler_params=_cparams("parallel", "parallel"),
        name="final_norm",
    )(h3, gain.reshape(1, d))


NA_QROWS = 8
NA_KROWS = 3 * NA_QROWS
NA_ROWS_PER_ITER = 4


def _na_kernel(q_ref, kw_ref, vw_ref, qm_ref, km_ref, vm_ref, bias_ref, o_ref, om_ref, *, rows, scale):
    blk = pl.program_id(1)
    tq = GRID_W
    n_pairs = q_ref.shape[1] // LANES
    base = jnp.clip(NA_QROWS * blk - NA_QROWS, 0, rows - NA_KROWS)
    lane = lax.broadcasted_iota(I32, (tq, LANES), 1)
    halves = [lane < HEAD_DIM, lane >= HEAD_DIM]

    pad = jnp.zeros((LANES - N_META, LANES), km_ref.dtype)
    k_meta = [jnp.concatenate([km_ref[:, p * LANES:(p + 1) * LANES], pad], axis=0) for p in range(n_pairs)]
    v_meta = [jnp.concatenate([vm_ref[:, p * LANES:(p + 1) * LANES], pad], axis=0) for p in range(n_pairs)]
    meta_bias = jnp.where(lane < N_META, 0.0, NEG_INF)

    n_win = NA_WIN_ROWS * GRID_W

    def row_body(jb, carry):
        colsl = [slice(p * LANES, (p + 1) * LANES) for p in range(n_pairs)]
        units = [(jj, p, hh) for jj in range(NA_ROWS_PER_ITER) for p in range(n_pairs) for hh in range(2)]
        s_idx, koff, qoff = [], [], []
        for jj in range(NA_ROWS_PER_ITER):
            j = jb * NA_ROWS_PER_ITER + jj
            r = NA_QROWS * blk + j
            start = jnp.clip(r - NA_WIN_ROWS // 2, 0, rows - NA_WIN_ROWS)
            s_idx.append(start - r + (NA_WIN_ROWS - 1))
            koff.append(pl.multiple_of((start - base) * GRID_W, GRID_W))
            qoff.append(pl.multiple_of(j * tq, tq))
        q_pair = {(jj, p): q_ref[pl.ds(qoff[jj], tq), c]
                  for jj in range(NA_ROWS_PER_ITER) for p, c in enumerate(colsl)}
        k_ext = {(jj, p): jnp.concatenate([kw_ref[pl.ds(koff[jj], n_win), c], k_meta[p]], axis=0)
                 for jj in range(NA_ROWS_PER_ITER) for p, c in enumerate(colsl)}
        v_ext = {(jj, p): jnp.concatenate([vw_ref[pl.ds(koff[jj], n_win), c], v_meta[p]], axis=0)
                 for jj in range(NA_ROWS_PER_ITER) for p, c in enumerate(colsl)}
        qh = [jnp.where(halves[hh], q_pair[jj, p], jnp.zeros_like(q_pair[jj, p])) for jj, p, hh in units]
        s = [_dot_nt(qh[u], k_ext[jj, p]) * scale
             + jnp.concatenate([bias_ref[2 * p + hh, s_idx[jj]], meta_bias], axis=1)
             for u, (jj, p, hh) in enumerate(units)]
        m = [jnp.max(x, axis=-1, keepdims=True) for x in s]
        e = [jnp.exp(x - mx) for x, mx in zip(s, m)]
        den = [jnp.sum(x, axis=-1, keepdims=True) for x in e]
        o = [_dot(e[u].astype(BF16), v_ext[jj, p]) / den[u] for u, (jj, p, hh) in enumerate(units)]
        for u in range(0, len(units), 2):
            jj, p, _ = units[u]
            o_ref[pl.ds(qoff[jj], tq), colsl[p]] = jnp.where(halves[0], o[u], o[u + 1]).astype(o_ref.dtype)
        return carry

    lax.fori_loop(0, NA_QROWS // NA_ROWS_PER_ITER, row_body, 0)

    @pl.when(blk == 0)
    def _():
        lane_m = lax.broadcasted_iota(I32, (N_META, LANES), 1)
        for p in range(n_pairs):
            cols = slice(p * LANES, (p + 1) * LANES)
            q_pair = qm_ref[:, cols]
            kmp = km_ref[:, cols]
            vmp = vm_ref[:, cols]
            outs = []
            for hh in range(2):
                sel = (lane_m < HEAD_DIM) if hh == 0 else (lane_m >= HEAD_DIM)
                qp = jnp.where(sel, q_pair, jnp.zeros_like(q_pair))
                s_m = _dot_nt(qp, kmp) * scale
                p_m = jnp.exp(s_m - jnp.max(s_m, axis=-1, keepdims=True))
                den = jnp.sum(p_m, axis=-1, keepdims=True)
                outs.append(_dot(p_m.astype(BF16), vmp) / den)
            om_ref[:, cols] = jnp.where(lane_m < HEAD_DIM, outs[0], outs[1]).astype(om_ref.dtype)


def _na_bias_table(rpb):
    h = rpb.shape[0]
    c_ids = jnp.arange(GRID_W)
    c_start = jnp.clip(c_ids - NA_WIN_COLS // 2, 0, GRID_W - NA_WIN_COLS)
    in_band = (c_ids[None, :] >= c_start[:, None]) & (c_ids[None, :] < c_start[:, None] + NA_WIN_COLS)
    dc = jnp.clip(c_ids[None, :] - c_ids[:, None] + NA_WIN_COLS - 1, 0, 2 * NA_WIN_COLS - 2)
    tab = jnp.where(in_band[None, None], rpb.astype(F32)[:, :, dc], NEG_INF)
    win = jnp.stack([tab[:, s:s + NA_WIN_ROWS] for s in range(NA_WIN_ROWS)], axis=1)
    return jnp.transpose(win, (0, 1, 3, 2, 4)).reshape(h, NA_WIN_ROWS, GRID_W, NA_WIN_ROWS * GRID_W)


def na_attention(qkv, rpb):
    b, l, w3 = qkv.shape
    w = w3 // 3
    t = l - N_META
    rows = t // GRID_W
    assert rows * GRID_W == t and rows % NA_QROWS == 0 and rows >= NA_KROWS
    tq = NA_QROWS * GRID_W
    tk = NA_KROWS * GRID_W
    bias = _na_bias_table(rpb)
    al = SUBLANES_BF16

    def q_map(bi, i):
        return (bi, pl.multiple_of(N_META + i * tq, al), 0)

    def kv_map(col):
        def f(bi, i):
            base = jnp.clip(NA_QROWS * i - NA_QROWS, 0, rows - NA_KROWS)
            return (bi, pl.multiple_of(N_META + base * GRID_W, al), col)
        return f

    def meta_map(col):
        return lambda bi, i: (bi, 0, col)

    el = pl.Element
    grid_out, meta_out = pl.pallas_call(
        functools.partial(_na_kernel, rows=rows, scale=HEAD_DIM ** -0.5),
        grid=(b, rows // NA_QROWS),
        in_specs=[
            pl.BlockSpec((None, el(tq), el(w)), q_map),
            pl.BlockSpec((None, el(tk), el(w)), kv_map(w)),
            pl.BlockSpec((None, el(tk), el(w)), kv_map(2 * w)),
            pl.BlockSpec((None, el(N_META), el(w)), meta_map(0)),
            pl.BlockSpec((None, el(N_META), el(w)), meta_map(w)),
            pl.BlockSpec((None, el(N_META), el(w)), meta_map(2 * w)),
            pl.BlockSpec(bias.shape, lambda bi, i: (0, 0, 0, 0)),
        ],
        out_specs=[
            pl.BlockSpec((None, tq, w), lambda bi, i: (bi, i, 0)),
            pl.BlockSpec((None, N_META, w), lambda bi, i: (bi, 0, 0)),
        ],
        out_shape=[
            jax.ShapeDtypeStruct((b, t, w), BF16),
            jax.ShapeDtypeStruct((b, N_META, w), BF16),
        ],
        compiler_params=_cparams("parallel", "arbitrary"),
        name="na_attention",
    )(qkv, qkv, qkv, qkv, qkv, qkv, bias)
    return jnp.concatenate([meta_out, grid_out], axis=1)


RWKV_CHUNK = 64
RWKV_TILE_CHUNKS = 2
RWKV_HALO = 8


def _split3_bf16(x):
    p1 = x.astype(BF16)
    r1 = x - p1.astype(F32)
    p2 = r1.astype(BF16)
    p3 = (r1 - p2.astype(F32)).astype(BF16)
    return p1, p2, p3


def _mm1(a, b):
    return _dot(a.astype(BF16), b.astype(BF16))


def _mm3(a, b):
    ah, al = _split_bf16(a)
    bh, bl = _split_bf16(b)
    return _dot(ah, bh) + _dot(ah, bl) + _dot(al, bh)


def _mm1_nt(a, b):
    return _dot_nt(a.astype(BF16), b.astype(BF16))


def _mm3_nt(a, b):
    ah, al = _split_bf16(a)
    bh, bl = _split_bf16(b)
    return _dot_nt(ah, bh) + _dot_nt(ah, bl) + _dot_nt(al, bh)


def _exact_left(mat_bf16, x):
    p1, p2, p3 = _split3_bf16(x)
    return _dot(mat_bf16, p1) + _dot(mat_bf16, p2) + _dot(mat_bf16, p3)


def _exact_right(x, mat_bf16):
    p1, p2, p3 = _split3_bf16(x)
    return _dot(p1, mat_bf16) + _dot(p2, mat_bf16) + _dot(p3, mat_bf16)


def _head_block_ones(width):
    ri = lax.broadcasted_iota(I32, (width, width), 0) // HEAD_DIM
    ci = lax.broadcasted_iota(I32, (width, width), 1) // HEAD_DIM
    return (ri == ci).astype(BF16)


def _head_sums(x, exact):
    ones_pair = _head_block_ones(LANES)
    tiles = []
    for p in range(x.shape[1] // LANES):
        xt = x[:, p * LANES:(p + 1) * LANES]
        tiles.append(_exact_right(xt, ones_pair) if exact else _dot(xt.astype(BF16), ones_pair))
    return jnp.concatenate(tiles, axis=1)


def _stack_heads(x, m0):
    z = jnp.zeros_like(x)
    return jnp.concatenate([jnp.where(m0, x, z), jnp.where(m0, z, x)], axis=0)


_MM_L4 = _mm1_nt
_MM_KT = _mm1_nt
_MM_SQ = _mm1
_MM_AP = _mm1
_MM_V = _mm1
_MM_Y = _mm1
_MM_UPD = _mm1


def _rwkv_chunk_maps(streams, c, n_sub):
    assert c == 64
    c2 = 2 * c
    lane = lax.broadcasted_iota(I32, (c, LANES), 1)
    m0 = lane < HEAD_DIM
    r_i = lax.broadcasted_iota(I32, (c2, c2), 0)
    c_i = lax.broadcasted_iota(I32, (c2, c2), 1)
    eye = (r_i == c_i).astype(F32)
    rel = r_i % c - c_i % c
    masks = {sg: (rel * sg > 0, rel * sg >= 0) for sg in {s["sign"] for s in streams}}
    items = [(j, q) for j in range(len(streams)) for q in range(n_sub)]

    def part(j, q, name):
        return _stack_heads(streams[j][name][q * c:(q + 1) * c], m0)

    lhs = {it: jnp.concatenate([part(*it, "kkp"), part(*it, "rp")], axis=0) for it in items}
    rhs = {it: jnp.concatenate([part(*it, "ki"), part(*it, "bi")], axis=0) for it in items}
    vs = {it: part(*it, "v") for it in items}
    kipcs = {it: part(*it, "kipc") for it in items}
    bipcs = {it: part(*it, "bipc") for it in items}
    l4 = {it: _MM_L4(lhs[it], rhs[it]) for it in items}
    m_kk, n1, m_rk, m_rb = {}, {}, {}, {}
    for it in items:
        strict, incl = masks[streams[it[0]]["sign"]]
        m = l4[it]
        m_kk[it] = jnp.where(strict, m[0:c2, 0:c2], 0.0)
        n1[it] = jnp.where(strict, m[0:c2, c2:2 * c2], 0.0)
        m_rk[it] = jnp.where(incl, m[c2:2 * c2, 0:c2], 0.0)
        m_rb[it] = jnp.where(incl, m[c2:2 * c2, c2:2 * c2], 0.0)
    n2 = {it: _MM_SQ(n1[it], n1[it]) for it in items}
    n4 = {it: _MM_SQ(n2[it], n2[it]) for it in items}
    n8 = {it: _MM_SQ(n4[it], n4[it]) for it in items}
    n16 = {it: _MM_SQ(n8[it], n8[it]) for it in items}
    n32 = {it: _MM_SQ(n16[it], n16[it]) for it in items}
    p1 = {it: (eye - n1[it]) + _MM_AP(eye - n1[it], n2[it]) for it in items}
    p2 = {it: eye + n4[it] + n8[it] + _MM_AP(n4[it], n8[it]) for it in items}
    p3 = {it: eye + n16[it] + n32[it] + _MM_AP(n16[it], n32[it]) for it in items}
    p23 = {it: _MM_AP(p2[it], p3[it]) for it in items}
    winv = {it: _MM_AP(p1[it], p23[it]) for it in items}
    mv = {it: _MM_V(m_kk[it], vs[it]) for it in items}
    mrv = {it: _MM_Y(m_rk[it], vs[it]) for it in items}
    wl = {it: _MM_AP(winv[it], lhs[it][0:c2]) for it in items}
    wmv = {it: _MM_AP(winv[it], mv[it]) for it in items}
    yl = {it: lhs[it][c2:2 * c2] - _MM_Y(m_rb[it], wl[it]) for it in items}
    y0 = {it: mrv[it] - _MM_Y(m_rb[it], wmv[it]) for it in items}
    g2 = {it: _MM_UPD(jnp.transpose(wl[it]), bipcs[it]) for it in items}
    hh = {it: _MM_UPD(jnp.transpose(jnp.concatenate([vs[it], -wmv[it]], axis=0)),
                      jnp.concatenate([kipcs[it], bipcs[it]], axis=0)) for it in items}
    return {it: (yl[it], y0[it], g2[it], hh[it]) for it in items}


def _rwkv_apply_maps(streams, maps, c, n_sub):
    c2 = 2 * c
    st = [s["st"] for s in streams]
    ys = {}
    for k in range(n_sub):
        cur = [(j, k if s["sign"] > 0 else n_sub - 1 - k) for j, s in enumerate(streams)]
        sg = [_MM_UPD(st[j], maps[j][q][2]) for j, q in cur]
        yk = [_MM_KT(maps[j][q][0], st[j]) + maps[j][q][1] for j, q in cur]
        for it, yi in zip(cur, yk):
            ys[it] = yi[0:c] + yi[c:c2]
        st = [st[j] * streams[j]["pc"][q] - sgi + maps[j][q][3] for (j, q), sgi in zip(cur, sg)]
    return [(jnp.concatenate([ys[j, q] for q in range(n_sub)], axis=0), st[j]) for j in range(len(streams))]


def _softplus(z):
    return jnp.maximum(z, 0.0) + jnp.log(1.0 + jnp.exp(-jnp.abs(z)))


def _rwkv_tile_prep(x_ref, xp_ref, xn_ref, tile, n_tiles, seq_len, width, sign, mu, w0, a0, w_wa,
                    k_k, k_a, r_k):
    cs = RWKV_CHUNK
    c = x_ref.shape[0]
    valid = jnp.minimum(c, seq_len - tile * c)
    row = lax.broadcasted_iota(I32, (c, LANES), 0)
    rowv = row < valid
    lane = lax.broadcasted_iota(I32, (c, LANES), 1)
    ones_pair = _head_block_ones(LANES)

    def shifted(lo):
        cols = slice(lo, lo + LANES)
        x = jnp.where(rowv, x_ref[:, cols], 0.0)
        prev_row = jnp.where(tile > 0, xp_ref[RWKV_HALO - 1:RWKV_HALO, cols], 0.0)
        next_row = jnp.where(tile < n_tiles - 1, xn_ref[0:1, cols], 0.0)
        x_prev = jnp.where(row == 0, prev_row, pltpu.roll(x, 1, 0))
        x_next = jnp.where(row == c - 1, next_row, pltpu.roll(x, c - 1, 0))
        xs = x + mu[:, cols] * (0.5 * (x_prev + x_next) - x)
        return jnp.where(rowv, xs, 0.0)

    wa = shifted(3 * width)
    g_lo = shifted(3 * width + LANES)
    xwa = jnp.where(lane < LANES // 2, jnp.tanh(wa), wa)
    la = _dot(xwa.astype(BF16), w_wa)

    t_i = lax.broadcasted_iota(I32, (c, c), 0)
    s_i = lax.broadcasted_iota(I32, (c, c), 1)
    tri = ((t_i // cs == s_i // cs) & ((t_i - s_i) * sign >= 0)).astype(BF16)

    pairs = []
    for p in range(width // LANES):
        lo = p * LANES
        cols = slice(lo, lo + LANES)
        r = shifted(lo)
        k = shifted(width + lo)
        v = shifted(2 * width + lo)
        w_log = -_softplus(-(w0[:, cols] + la[:, cols])) - 0.5
        logw = jnp.where(rowv, -jnp.exp(w_log), 0.0)
        a = jax.nn.sigmoid(a0[:, cols] + la[:, width + lo:width + lo + LANES])
        kk0 = k * k_k[:, cols]
        ss = _dot((kk0 * kk0).astype(BF16), ones_pair)
        kk = kk0 / jnp.maximum(jnp.sqrt(ss), 1e-12)
        kdir = k * (1.0 + (a - 1.0) * k_a[:, cols])
        b = kk * a
        cl = _exact_left(tri, logw)
        lasts = [cl[q * cs + cs - 1:q * cs + cs, :] if sign > 0 else cl[q * cs:q * cs + 1, :]
                 for q in range(c // cs)]
        last = jnp.concatenate([jnp.broadcast_to(lq, (cs, LANES)) for lq in lasts], axis=0)
        e_n = jnp.exp(-cl)
        pcr = jnp.exp(last - cl)
        pairs.append(dict(kkp=kk * jnp.exp(cl - logw), rp=r * jnp.exp(cl), ki=kdir * e_n, bi=b * e_n,
                          kipc=kdir * pcr, bipc=b * pcr, v=v, pc=[jnp.exp(lq) for lq in lasts],
                          bonus=_dot((r * kdir * r_k[:, cols]).astype(BF16), ones_pair) * v))
    return pairs, g_lo


def _rwkv_scan_kernel(xf_ref, xfp_ref, xfn_ref, xb_ref, xbp_ref, xbn_ref, mu_ref, w0_ref, a0_ref, wwa_ref,
                      gup_ref, kk_ref, ka_ref, rk_ref, yf_ref, yb_ref, bonf_ref, bonb_ref, g_ref, st_ref,
                      *, seq_len, width):
    i = pl.program_id(1)
    n_chunks = pl.num_programs(1)
    n_pairs = width // LANES

    @pl.when(i == 0)
    def _():
        st_ref[...] = jnp.zeros_like(st_ref)

    common = (mu_ref[...],)
    tail = (kk_ref[...], ka_ref[...], rk_ref[...])
    n_sub = xf_ref.shape[0] // RWKV_CHUNK
    fwd, g_lo = _rwkv_tile_prep(xf_ref, xfp_ref, xfn_ref, i, n_chunks, seq_len, width, 1, *common,
                                w0_ref[0], a0_ref[0], wwa_ref[0], *tail)
    bwd, _ = _rwkv_tile_prep(xb_ref, xbp_ref, xbn_ref, n_chunks - 1 - i, n_chunks, seq_len, width, -1, *common,
                             w0_ref[1], a0_ref[1], wwa_ref[1], *tail)
    g_ref[...] = _mm1(jax.nn.sigmoid(g_lo), gup_ref[...]).astype(g_ref.dtype)
    streams = []
    for di, (pairs, sign, bon_ref) in enumerate(((fwd, 1, bonf_ref), (bwd, -1, bonb_ref))):
        for p, s in enumerate(pairs):
            bon_ref[:, p * LANES:(p + 1) * LANES] = s.pop("bonus")
            s["st"] = st_ref[di * n_pairs + p]
            s["sign"] = sign
            streams.append(s)
    m = _rwkv_chunk_maps(streams, RWKV_CHUNK, n_sub)
    maps = [{q: m[j, q] for q in range(n_sub)} for j in range(len(streams))]
    res = _rwkv_apply_maps(streams, maps, RWKV_CHUNK, n_sub)
    for j, (y, st_new) in enumerate(res):
        di, p = divmod(j, n_pairs)
        cols = slice(p * LANES, (p + 1) * LANES)
        (yf_ref if di == 0 else yb_ref)[:, cols] = y
        st_ref[j] = st_new


def _rwkv_finish_kernel(yf_ref, yb_ref, bonf_ref, bonb_ref, g_ref, lg_ref, lb_ref, o_ref):
    y = yf_ref[...] + yb_ref[...]
    mean = _head_sums(y, exact=True) * (1.0 / HEAD_DIM)
    yc = y - mean
    var = _head_sums(yc * yc, exact=True) * (1.0 / HEAD_DIM)
    yn = yc * lax.rsqrt(var + RWKV_GN_EPS) * lg_ref[...] + lb_ref[...]
    o_ref[...] = ((yn + bonf_ref[...] + bonb_ref[...]) * g_ref[...].astype(F32)).astype(o_ref.dtype)


def rwkv_mix(rest, mu, w0, w_up, a0, a_up, g_up, k_k, k_a, r_k, lnx_g, lnx_b):
    bsz, l, n_cols = rest.shape
    width = w0.shape[1]
    rank = w_up.shape[1]
    assert n_cols == 3 * width + 2 * LANES and 2 * rank == LANES and l % RWKV_HALO == 0
    c = RWKV_CHUNK * RWKV_TILE_CHUNKS
    n_chunks = -(-l // c)
    per = c // RWKV_HALO
    n_halo = l // RWKV_HALO
    zeros = jnp.zeros((2, rank, width), F32)
    w_wa = jnp.concatenate([jnp.concatenate([w_up.astype(F32), zeros], axis=2),
                            jnp.concatenate([zeros, a_up.astype(F32)], axis=2)], axis=1)
    w_wa = w_wa.astype(BF16)

    fwd_chunk = lambda i: i
    bwd_chunk = lambda i: n_chunks - 1 - i

    def tile_specs(chunk_of):
        return [
            pl.BlockSpec((None, c, n_cols), lambda b, i: (b, chunk_of(i), 0)),
            pl.BlockSpec((None, RWKV_HALO, n_cols), lambda b, i: (b, jnp.maximum(chunk_of(i) * per - 1, 0), 0)),
            pl.BlockSpec((None, RWKV_HALO, n_cols),
                         lambda b, i: (b, jnp.minimum((chunk_of(i) + 1) * per, n_halo - 1), 0)),
        ]

    row2 = lambda a: a.astype(F32).reshape(1, -1)
    whole = lambda *shape: pl.BlockSpec(shape, lambda b, i: (0,) * len(shape))
    out_spec = lambda chunk_of: pl.BlockSpec((None, c, width), lambda b, i: (b, chunk_of(i), 0))
    act = lambda dt: jax.ShapeDtypeStruct((bsz, l, width), dt)
    y_f, y_b, bon_f, bon_b, g = pl.pallas_call(
        functools.partial(_rwkv_scan_kernel, seq_len=l, width=width),
        grid=(bsz, n_chunks),
        in_specs=tile_specs(fwd_chunk) + tile_specs(bwd_chunk) + [
            whole(1, n_cols),
            whole(2, 1, width), whole(2, 1, width),
            whole(2, LANES, 2 * width),
            whole(LANES, width),
            whole(1, width), whole(1, width), whole(1, width),
        ],
        out_specs=[out_spec(fwd_chunk), out_spec(bwd_chunk), out_spec(fwd_chunk), out_spec(bwd_chunk),
                   out_spec(fwd_chunk)],
        out_shape=[act(F32), act(F32), act(F32), act(F32), act(BF16)],
        scratch_shapes=[pltpu.VMEM((2 * (width // LANES), LANES, LANES), F32)],
        compiler_params=_cparams("parallel", "arbitrary"),
        name="rwkv_scan",
    )(rest, rest, rest, rest, rest, rest, row2(mu), w0.astype(F32).reshape(2, 1, width),
      a0.astype(F32).reshape(2, 1, width), w_wa, g_up.astype(BF16), row2(k_k), row2(k_a), row2(r_k))

    n = bsz * l
    tm = _row_tile(n, 608)
    rows = lambda: pl.BlockSpec((tm, width), lambda j: (j, 0))
    flat = lambda a: a.reshape(n, width)
    return pl.pallas_call(
        _rwkv_finish_kernel,
        grid=(n // tm,),
        in_specs=[rows(), rows(), rows(), rows(), rows(),
                  pl.BlockSpec((1, width), lambda j: (0, 0)),
                  pl.BlockSpec((1, width), lambda j: (0, 0))],
        out_specs=rows(),
        out_shape=jax.ShapeDtypeStruct((n, width), BF16),
        compiler_params=_cparams("parallel"),
        name="rwkv_finish",
    )(flat(y_f), flat(y_b), flat(bon_f), flat(bon_b), flat(g), row2(lnx_g), row2(lnx_b)).reshape(bsz, l, width)


S5_CHUNK = 16


def _cpow(n, lr, li, step):
    mag = jnp.exp(n * (lr * step))
    ang = n * (li * step)
    return mag * jnp.cos(ang), mag * jnp.sin(ang)


def _s5_param_kernel(lamr_ref, stepr_ref, bt_ref, ct_ref, kmat_ref, wst_ref, cexp_ref, alpha_ref):
    t_len = S5_CHUNK
    n_i = S5_GROUP_CH
    p2 = 2 * S5_STATE
    ti = t_len * n_i

    lr = lamr_ref[0:1, :]
    li = lamr_ref[1:2, :]
    step = jnp.exp(stepr_ref[...])
    ab_re, ab_im = _cpow(1.0, lr, li, step)
    den = lr * lr + li * li
    z_re = ((ab_re - 1.0) * lr + ab_im * li) / den
    z_im = (ab_im * lr - (ab_re - 1.0) * li) / den
    t16 = lax.broadcasted_iota(I32, (t_len, p2), 0).astype(F32)
    is_f = lax.broadcasted_iota(I32, (t_len, p2), 1) < S5_STATE

    def rows_by_t(x):
        return jnp.concatenate([jnp.broadcast_to(x[t:t + 1], (n_i, p2)) for t in range(t_len)], axis=0)

    def tiled_rows(x):
        return jnp.concatenate([x] * t_len, axis=0)

    def pow_rows(n):
        q_re, q_im = _cpow(n, lr, li, step)
        return rows_by_t(q_re), rows_by_t(q_im)

    bt_re = tiled_rows(bt_ref[0])
    bt_im = tiled_rows(bt_ref[1])
    bb_re = z_re * bt_re - z_im * bt_im
    bb_im = z_re * bt_im + z_im * bt_re
    pw_re, pw_im = pow_rows(jnp.where(is_f, (t_len - 1.0) - t16, t16))
    wst_ref[:, 0:p2] = (pw_re * bb_re - pw_im * bb_im).astype(wst_ref.dtype)
    wst_ref[:, p2:2 * p2] = (pw_re * bb_im + pw_im * bb_re).astype(wst_ref.dtype)
    al_re, al_im = _cpow(float(t_len), lr, li, step)
    alpha_ref[0:1, :] = al_re
    alpha_ref[1:2, :] = al_im

    ct_re = tiled_rows(ct_ref[0])
    ct_im = tiled_rows(ct_ref[1])

    def c_times_pow(n):
        q_re, q_im = pow_rows(n)
        return jnp.transpose(ct_re * q_re - ct_im * q_im), jnp.transpose(ct_re * q_im + ct_im * q_re)

    ca_re, ca_im = c_times_pow(jnp.where(is_f, t16, jnp.where(t16 == 0.0, 0.0, t_len - t16)))
    lane_p = lax.broadcasted_iota(I32, (n_i, p2), 1)
    bbr = bb_re[0:n_i]
    bbi = bb_im[0:n_i]
    zero = jnp.zeros_like(bbr)
    strips = []
    for sel in (lane_p < S5_STATE, lane_p >= S5_STATE):
        strips.append(_mm3(jnp.where(sel, bbr, zero), ca_re) - _mm3(jnp.where(sel, bbi, zero), ca_im))
    strip_f, strip_b = strips
    t_k = lax.broadcasted_iota(I32, (n_i, ti), 1) // n_i
    for tt in range(t_len):
        sf = strip_f if tt == 0 else pltpu.roll(strip_f, tt * n_i, 1)
        sb = strip_b if tt == 0 else pltpu.roll(strip_b, tt * n_i, 1)
        blk = jnp.where(t_k >= tt, sf, 0.0) + jnp.where(t_k <= tt, sb, 0.0)
        kmat_ref[tt * n_i:(tt + 1) * n_i, :] = blk.astype(kmat_ref.dtype)

    co_re, co_im = c_times_pow(jnp.where(is_f, t16 + 1.0, t_len - t16))
    cexp_ref[0:p2, :] = co_re.astype(cexp_ref.dtype)
    cexp_ref[p2:2 * p2, :] = (-co_im).astype(cexp_ref.dtype)


def _s5_main_kernel(u_ref, kmat_ref, wst_ref, cexp_ref, alpha_ref, y_ref, x_ref, sf_ref, sb_ref,
                    *, n_batch, n_chunks):
    p2 = 2 * S5_STATE
    n_gb = u_ref.shape[0]
    for g in range(n_gb):
        x_ref[g] = _dot(u_ref[g].astype(BF16), wst_ref[g])
    lane = lax.broadcasted_iota(I32, (1, p2), 1)
    is_f = lane < S5_STATE
    alphas = [(alpha_ref[g, 0:1, :], alpha_ref[g, 1:2, :]) for g in range(n_gb)]

    sub = S5_SCAN_ROWS
    assert n_chunks % sub == 0
    chains = [(g, b) for g in range(n_gb) for b in range(n_batch)]

    def step(k, carry):
        new = []
        for (g, b), (s_re, s_im) in zip(chains, carry):
            a_re, a_im = alphas[g]
            row_f = pl.multiple_of(b * n_chunks + sub * k, sub)
            row_b = pl.multiple_of(b * n_chunks + (n_chunks - sub) - sub * k, sub)
            xf = x_ref[g, pl.ds(row_f, sub), :]
            xb = x_ref[g, pl.ds(row_b, sub), :]
            seen = []
            for r in range(sub):
                seen.append(jnp.concatenate([s_re, s_im], axis=1))
                rb = sub - 1 - r
                x_re = jnp.where(is_f, xf[r:r + 1, 0:p2], xb[rb:rb + 1, 0:p2])
                x_im = jnp.where(is_f, xf[r:r + 1, p2:2 * p2], xb[rb:rb + 1, p2:2 * p2])
                s_re, s_im = a_re * s_re - a_im * s_im + x_re, a_re * s_im + a_im * s_re + x_im
            sf_ref[g, pl.ds(row_f, sub), :] = jnp.concatenate(seen, axis=0)
            sb_ref[g, pl.ds(row_b, sub), :] = jnp.concatenate(seen[::-1], axis=0)
            new.append((s_re, s_im))
        return tuple(new)

    zero = jnp.zeros((1, p2), F32)
    lax.fori_loop(0, n_chunks // sub, step, tuple((zero, zero) for _ in chains))
    lane2 = lax.broadcasted_iota(I32, sf_ref.shape[1:], 1) % p2
    for g in range(n_gb):
        s_in = jnp.where(lane2 < S5_STATE, sf_ref[g], sb_ref[g])
        s_hi, s_lo = _split_bf16(s_in)
        y_ref[g] = (_dot(u_ref[g].astype(BF16), kmat_ref[g]) + _dot(s_hi, cexp_ref[g]) + _dot(s_lo, cexp_ref[g]))


S5_RELAYOUT_CHUNKS = 128
S5_SCAN_ROWS = 8
S5_GROUPS_PER_STEP = 2


def _s5_group_major_kernel(h_ref, g_ref, u_ref, hn_ref, ut_ref, *, seq_len):
    n_g, mt, ti = u_ref.shape
    t_len = S5_CHUNK
    n_i = ti // t_len
    n_lt = hn_ref.shape[0]
    g_lt = LANES // n_i
    rows = h_ref.shape[0]
    valid = seq_len - pl.program_id(1) * rows
    row = lax.broadcasted_iota(I32, h_ref.shape, 0)
    hn = jnp.where(row < valid, _rms(h_ref[...], g_ref[...]), 0.0)
    for j in range(n_lt):
        hn_ref[j] = hn[:, j * LANES:(j + 1) * LANES]
    for tau in range(t_len):
        for j in range(n_lt):
            xt = jnp.transpose(hn_ref[j, pl.ds(tau, mt, stride=t_len), :])
            ut_ref[j * g_lt:(j + 1) * g_lt, tau * n_i:(tau + 1) * n_i, :] = xt.reshape(g_lt, n_i, mt)
    for g in range(n_g):
        u_ref[g] = jnp.transpose(ut_ref[g]).astype(u_ref.dtype)


def _s5_token_major_kernel(y_ref, o_ref, zt_ref, z_ref):
    n_g, mt, ti = y_ref.shape
    t_len = S5_CHUNK
    n_i = ti // t_len
    n_lt = z_ref.shape[0]
    for g in range(n_g):
        yt = jnp.transpose(y_ref[g])
        zt_ref[:, g * n_i:(g + 1) * n_i, :] = yt.reshape(t_len, n_i, mt)
    for t in range(t_len):
        for j in range(n_lt):
            z_ref[j, pl.ds(t, mt, stride=t_len), :] = jnp.transpose(zt_ref[t, j * LANES:(j + 1) * LANES, :])
    for j in range(n_lt):
        o_ref[:, j * LANES:(j + 1) * LANES] = z_ref[j]


def _gelu_tanh(x):
    return 0.5 * x * (1.0 + jnp.tanh(math.sqrt(2.0 / math.pi) * (x + 0.044715 * (x * x * x))))


def _s5_glu_kernel(h_ref, y_ref, g_ref, d_ref, w_ref, o_ref):
    h = h_ref[...]
    dm = h.shape[1]
    y = y_ref[...] + d_ref[...] * _rms(h, g_ref[...])
    gl = _gelu_tanh(y).astype(BF16)
    a = _dot(gl, w_ref[:, 0:dm])
    b = _dot(gl, w_ref[:, dm:2 * dm])
    o_ref[...] = h + a * jax.nn.sigmoid(b)


def s5_mix(h3, gain, b_re, b_im, lam_re, lam_im, log_step, c_re, c_im, d_skip, w_glu):
    bsz, l, dm = h3.shape
    n_g, n_p, n_i = b_re.shape
    t_len = S5_CHUNK
    assert l % t_len == 0 and n_g * n_i == dm and n_p == S5_STATE and n_i == S5_GROUP_CH
    n_chunks = -(-(l // t_len) // S5_SCAN_ROWS) * S5_SCAN_ROWS
    m = bsz * n_chunks
    ti = t_len * n_i
    p2 = 2 * n_p
    n = bsz * l
    tm = _row_tile(n, 608)
    h2 = h3.reshape(n, dm)
    gain2 = gain.astype(F32).reshape(1, dm)

    mt = min(S5_RELAYOUT_CHUNKS, n_chunks)
    n_tiles = -(-n_chunks // mt)
    u = pl.pallas_call(
        functools.partial(_s5_group_major_kernel, seq_len=l),
        grid=(bsz, n_tiles),
        in_specs=[pl.BlockSpec((None, mt * t_len, dm), lambda b, i: (b, i, 0)),
                  pl.BlockSpec((1, dm), lambda b, i: (0, 0))],
        out_specs=pl.BlockSpec((n_g, None, mt, ti), lambda b, i: (0, b, i, 0)),
        out_shape=jax.ShapeDtypeStruct((n_g, bsz, n_chunks, ti), BF16),
        scratch_shapes=[pltpu.VMEM((dm // LANES, mt * t_len, LANES), F32), pltpu.VMEM((n_g, ti, mt), F32)],
        compiler_params=_cparams("parallel", "parallel"),
        name="s5_group_major",
    )(h3, gain2).reshape(n_g, m, ti)

    f32 = lambda a: a.astype(F32)
    lam_r = jnp.stack([jnp.concatenate([f32(lam_re)[0], f32(lam_re)[1]], axis=-1),
                       jnp.concatenate([f32(lam_im)[0], f32(lam_im)[1]], axis=-1)], axis=1)
    step_r = jnp.repeat(jnp.transpose(f32(log_step))[:, None, :], n_p, axis=2)
    bt = jnp.stack([jnp.transpose(f32(b_re), (0, 2, 1)), jnp.transpose(f32(b_im), (0, 2, 1))], axis=1)
    bt = jnp.tile(bt, (1, 1, 1, 2))
    ct = jnp.stack([f32(c_re), f32(c_im)], axis=0)
    ct = jnp.transpose(ct, (2, 0, 3, 1, 4)).reshape(n_g, 2, n_i, p2)

    gspec = lambda *shape: pl.BlockSpec((None,) + shape, lambda g: (g,) + (0,) * len(shape))
    kmat, wst, cexp, alpha = pl.pallas_call(
        _s5_param_kernel,
        grid=(n_g,),
        in_specs=[gspec(2, p2), gspec(1, p2), gspec(2, n_i, p2), gspec(2, n_i, p2)],
        out_specs=[gspec(ti, ti), gspec(ti, 2 * p2), gspec(2 * p2, ti), gspec(2, p2)],
        out_shape=[
            jax.ShapeDtypeStruct((n_g, ti, ti), BF16),
            jax.ShapeDtypeStruct((n_g, ti, 2 * p2), BF16),
            jax.ShapeDtypeStruct((n_g, 2 * p2, ti), BF16),
            jax.ShapeDtypeStruct((n_g, 2, p2), F32),
        ],
        compiler_params=_cparams("parallel"),
        name="s5_params",
    )(lam_r, step_r, bt, ct)

    gb = S5_GROUPS_PER_STEP
    assert n_g % gb == 0
    gbspec = lambda *shape: pl.BlockSpec((gb,) + shape, lambda g: (g,) + (0,) * len(shape))
    y = pl.pallas_call(
        functools.partial(_s5_main_kernel, n_batch=bsz, n_chunks=n_chunks),
        grid=(n_g // gb,),
        in_specs=[gbspec(m, ti), gbspec(ti, ti), gbspec(ti, 2 * p2), gbspec(2 * p2, ti), gbspec(2, p2)],
        out_specs=gbspec(m, ti),
        out_shape=jax.ShapeDtypeStruct((n_g, m, ti), F32),
        scratch_shapes=[pltpu.VMEM((gb, m, 2 * p2), F32)] * 3,
        compiler_params=_cparams("parallel"),
        name="s5_main",
    )(u, kmat, wst, cexp, alpha)
    y2 = pl.pallas_call(
        _s5_token_major_kernel,
        grid=(bsz, n_tiles),
        in_specs=[pl.BlockSpec((n_g, None, mt, ti), lambda b, i: (0, b, i, 0))],
        out_specs=pl.BlockSpec((None, mt * t_len, dm), lambda b, i: (b, i, 0)),
        out_shape=jax.ShapeDtypeStruct((bsz, l, dm), F32),
        scratch_shapes=[pltpu.VMEM((t_len, dm, mt), F32), pltpu.VMEM((dm // LANES, mt * t_len, LANES), F32)],
        compiler_params=_cparams("parallel", "parallel"),
        name="s5_token_major",
    )(y.reshape(n_g, bsz, n_chunks, ti)).reshape(n, dm)

    out = pl.pallas_call(
        _s5_glu_kernel,
        grid=(n // tm,),
        in_specs=[
            pl.BlockSpec((tm, dm), lambda i: (i, 0)),
            pl.BlockSpec((tm, dm), lambda i: (i, 0)),
            pl.BlockSpec((1, dm), lambda i: (0, 0)),
            pl.BlockSpec((1, dm), lambda i: (0, 0)),
            pl.BlockSpec((dm, 2 * dm), lambda i: (0, 0)),
        ],
        out_specs=pl.BlockSpec((tm, dm), lambda i: (i, 0)),
        out_shape=jax.ShapeDtypeStruct((n, dm), F32),
        compiler_params=_cparams("parallel"),
        name="s5_glu",
    )(h2, y2, gain2, f32(d_skip).reshape(1, dm), w_glu.astype(BF16))
    return out.reshape(bsz, l, dm)


def na_rwkv_mix(h3, gain, w_in, w_out, rpb, mu, w0, w_up, a0, a_up, g_up, k_k, k_a, r_k, lnx_g, lnx_b):
    bsz, l, dm = h3.shape
    n = bsz * l
    h2 = h3.reshape(n, dm)
    n_qkv = 3 * (w_out.shape[0] // 2)
    qkv, rest = norm_inproj(h2, gain.astype(F32), w_in.astype(BF16), n_qkv)
    na = na_attention(qkv.reshape(bsz, l, n_qkv), rpb)
    rw = rwkv_mix(rest.reshape(bsz, l, -1), mu, w0, w_up, a0, a_up, g_up, k_k, k_a, r_k, lnx_g, lnx_b)
    out = outproj_residual(h2, na.reshape(n, -1), rw.reshape(n, -1), w_out.astype(BF16))
    return out.reshape(bsz, l, dm)


def kernel(x, meta_tokens, norm_mix, norm_ffn, norm_final, mix_w_in, mix_w_out, na_rpb, rwkv_mu,
           rwkv_w0, rwkv_w_up, rwkv_a0, rwkv_a_up, rwkv_g_up, rwkv_k_k, rwkv_k_a, rwkv_r_k,
           rwkv_lnx_g, rwkv_lnx_b, s5_b_re, s5_b_im, s5_lambda_re, s5_lambda_im, s5_log_step,
           s5_c_re, s5_c_im, s5_d, s5_w_glu, moe_w_group, moe_b_group, moe_w_expert, moe_b_expert,
           moe_w1, moe_w3, moe_w2):
    bsz, _, dm = x.shape
    depth = norm_mix.shape[0]
    meta = jnp.broadcast_to(meta_tokens.astype(x.dtype)[None], (bsz,) + meta_tokens.shape)
    h = jnp.concatenate([meta, x], axis=1)
    l = h.shape[1]
    for layer in range(depth):
        i = layer // 2
        if layer % 2 == 0:
            h = na_rwkv_mix(h, norm_mix[layer], mix_w_in[i], mix_w_out[i], na_rpb[i], rwkv_mu[i], rwkv_w0[i],
                            rwkv_w_up[i], rwkv_a0[i], rwkv_a_up[i], rwkv_g_up[i], rwkv_k_k[i], rwkv_k_a[i],
                            rwkv_r_k[i], rwkv_lnx_g[i], rwkv_lnx_b[i])
        else:
            h = s5_mix(h, norm_mix[layer], s5_b_re[i], s5_b_im[i], s5_lambda_re[i], s5_lambda_im[i],
                       s5_log_step[i], s5_c_re[i], s5_c_im[i], s5_d[i], s5_w_glu[i])
        h = hierarchical_moe_residual(h.reshape(bsz * l, dm), norm_ffn[layer].astype(F32), moe_w_group[layer],
                                      moe_b_group[layer], moe_w_expert[layer], moe_b_expert[layer],
                                      moe_w1, moe_w3, moe_w2, layer).reshape(bsz, l, dm)
    return final_norm(h, norm_final.astype(F32))
```

```python
import functools
import math

import jax
import jax.numpy as jnp
from jax import lax
from jax.experimental import pallas as pl
from jax.experimental.pallas import tpu as pltpu

F32 = jnp.float32
BF16 = jnp.bfloat16
I32 = jnp.int32

N_META = 16
GRID_W = 64
HEAD_DIM = 64
NA_WIN_ROWS = 8
NA_WIN_COLS = 16
S5_GROUP_CH = 16
S5_STATE = 64
MOE_GROUPS = 4
MOE_PER_GROUP = 8
MOE_EXPERTS = MOE_GROUPS * MOE_PER_GROUP
NORM_EPS = 1e-6
RWKV_GN_EPS = 64e-5
NEG_INF = -1e30

LANES = 128
SUBLANES_BF16 = 16
VMEM_LIMIT_BYTES = 56 * 1024 * 1024

MOE_TILE = 256
MOE_TOKEN_TILE = 320
MOE_STAGE_SLOTS = 3
ROUTER_LANES = 128


def _cparams(*sem):
    return pltpu.CompilerParams(dimension_semantics=sem, vmem_limit_bytes=VMEM_LIMIT_BYTES)


def _row_tile(n, target):
    best = None
    for t in range(SUBLANES_BF16, min(n, target) + 1, SUBLANES_BF16):
        if n % t == 0:
            best = t
    assert best is not None, (n, target)
    return best


def _rms(x, gain):
    ms = jnp.mean(x * x, axis=-1, keepdims=True)
    return (x * lax.rsqrt(ms + NORM_EPS)) * gain


def _split_bf16(x):
    hi = x.astype(BF16)
    lo = (x - hi.astype(F32)).astype(BF16)
    return hi, lo


def _dot(a, b):
    return jnp.dot(a, b, preferred_element_type=F32)


def _dot_nt(a, b):
    return lax.dot_general(a, b, (((1,), (1,)), ((), ())), preferred_element_type=F32)


def _norm_inproj_kernel(h_ref, g_ref, w_ref, qkv_ref, rest_ref, xn_ref, *, n_qkv, chunk):
    xn_ref[...] = _rms(h_ref[...], g_ref[...]).astype(BF16)
    n_all = w_ref.shape[1]
    for c in range(0, n_all, chunk):
        y = _dot(xn_ref[...], w_ref[:, c:c + chunk])
        if c < n_qkv:
            qkv_ref[:, c:c + chunk] = y.astype(BF16)
        else:
            rest_ref[:, c - n_qkv:c - n_qkv + chunk] = y


def norm_inproj(h2, gain, w_bf16, n_qkv):
    n, d = h2.shape
    n_all = w_bf16.shape[1]
    tm = _row_tile(n, 608)
    chunk = 256
    assert n_qkv % chunk == 0 and n_all % chunk == 0
    return pl.pallas_call(
        functools.partial(_norm_inproj_kernel, n_qkv=n_qkv, chunk=chunk),
        grid=(n // tm,),
        in_specs=[
            pl.BlockSpec((tm, d), lambda i: (i, 0)),
            pl.BlockSpec((1, d), lambda i: (0, 0)),
            pl.BlockSpec((d, n_all), lambda i: (0, 0)),
        ],
        out_specs=[
            pl.BlockSpec((tm, n_qkv), lambda i: (i, 0)),
            pl.BlockSpec((tm, n_all - n_qkv), lambda i: (i, 0)),
        ],
        out_shape=[
            jax.ShapeDtypeStruct((n, n_qkv), BF16),
            jax.ShapeDtypeStruct((n, n_all - n_qkv), F32),
        ],
        scratch_shapes=[pltpu.VMEM((tm, d), BF16)],
        compiler_params=_cparams("parallel"),
        name="norm_inproj",
    )(h2, gain.reshape(1, d), w_bf16)


def _outproj_kernel(h_ref, na_ref, rw_ref, wa_ref, wb_ref, o_ref):
    acc = _dot(na_ref[...], wa_ref[...])
    acc = acc + _dot(rw_ref[...], wb_ref[...])
    o_ref[...] = h_ref[...] + acc


def outproj_residual(h2, na, rw, w_out_bf16):
    n, d = h2.shape
    ka, kb = na.shape[1], rw.shape[1]
    tm = _row_tile(n, 608)
    return pl.pallas_call(
        _outproj_kernel,
        grid=(n // tm,),
        in_specs=[
            pl.BlockSpec((tm, d), lambda i: (i, 0)),
            pl.BlockSpec((tm, ka), lambda i: (i, 0)),
            pl.BlockSpec((tm, kb), lambda i: (i, 0)),
            pl.BlockSpec((ka, d), lambda i: (0, 0)),
            pl.BlockSpec((kb, d), lambda i: (0, 0)),
        ],
        out_specs=pl.BlockSpec((tm, d), lambda i: (i, 0)),
        out_shape=jax.ShapeDtypeStruct((n, d), F32),
        compiler_params=_cparams("parallel"),
        name="outproj_residual",
    )(h2, na, rw, w_out_bf16[:ka], w_out_bf16[ka:])


def _store_token_tiles(ref, x):
    rows = x.shape[0]
    s_n = x.shape[1] // LANES
    for s in range(s_n):
        ref[pl.ds(s, rows, stride=s_n), :] = x[:, s * LANES:(s + 1) * LANES]


def _load_token_tile_cols(ref, s, rows, s_n):
    return ref[pl.ds(s, rows, stride=s_n), :]


def _router_kernel(h_ref, g_ref, whi_ref, wlo_ref, b_ref, xn_ref, route_ref):
    xn = _rms(h_ref[...], g_ref[...])
    x_hi, x_lo = _split_bf16(xn)
    _store_token_tiles(xn_ref, xn)
    logits = (_dot(x_hi, whi_ref[...]) + _dot(x_hi, wlo_ref[...]) + _dot(x_lo, whi_ref[...])
              + b_ref[...])
    tm = logits.shape[0]
    lane = lax.broadcasted_iota(I32, (tm, ROUTER_LANES), 1)
    big = jnp.int32(ROUTER_LANES)

    is_g = lane < MOE_GROUPS
    lg = jnp.where(is_g, logits, -jnp.inf)
    eg = jnp.where(is_g, jnp.exp(lg - jnp.max(lg, axis=-1, keepdims=True)), 0.0)
    pg = eg / jnp.sum(eg, axis=-1, keepdims=True)
    p_grp = jnp.max(pg, axis=-1, keepdims=True)
    grp = jnp.min(jnp.where(is_g & (pg == p_grp), lane, big), axis=-1, keepdims=True)

    lo_lane = MOE_GROUPS + MOE_PER_GROUP * grp
    is_e = (lane >= lo_lane) & (lane < lo_lane + MOE_PER_GROUP)
    le = jnp.where(is_e, logits, -jnp.inf)
    ee = jnp.where(is_e, jnp.exp(le - jnp.max(le, axis=-1, keepdims=True)), 0.0)
    pe = jnp.where(is_e, ee / jnp.sum(ee, axis=-1, keepdims=True), -1.0)
    p1 = jnp.max(pe, axis=-1, keepdims=True)
    i1 = jnp.min(jnp.where(pe == p1, lane, big), axis=-1, keepdims=True)
    pe2 = jnp.where(lane == i1, -1.0, pe)
    p2 = jnp.max(pe2, axis=-1, keepdims=True)
    i2 = jnp.min(jnp.where(pe2 == p2, lane, big), axis=-1, keepdims=True)
    denom = p1 + p2
    g1 = p_grp * p1 / denom
    g2 = p_grp * p2 / denom
    e1 = (i1 - MOE_GROUPS).astype(F32)
    e2 = (i2 - MOE_GROUPS).astype(F32)
    route_ref[...] = jnp.where(lane == 0, e1, jnp.where(lane == 1, e2, jnp.where(lane == 2, g1, g2)))


def moe_router(h2, gain, w_group, b_group, w_expert, b_expert):
    n, d = h2.shape
    n_r = MOE_GROUPS + MOE_EXPERTS
    w_r = jnp.concatenate([w_group, jnp.transpose(w_expert, (1, 0, 2)).reshape(d, MOE_EXPERTS)], axis=1)
    w_r = jnp.pad(w_r.astype(F32), ((0, 0), (0, ROUTER_LANES - n_r)))
    w_hi, w_lo = _split_bf16(w_r)
    b_r = jnp.pad(jnp.concatenate([b_group, b_expert.reshape(-1)]).astype(F32), (0, ROUTER_LANES - n_r))
    tm = _row_tile(n, 608)
    return pl.pallas_call(
        _router_kernel,
        grid=(n // tm,),
        in_specs=[
            pl.BlockSpec((tm, d), lambda i: (i, 0)),
            pl.BlockSpec((1, d), lambda i: (0, 0)),
            pl.BlockSpec((d, ROUTER_LANES), lambda i: (0, 0)),
            pl.BlockSpec((d, ROUTER_LANES), lambda i: (0, 0)),
            pl.BlockSpec((1, ROUTER_LANES), lambda i: (0, 0)),
        ],
        out_specs=[
            pl.BlockSpec((tm * (d // LANES), LANES), lambda i: (i, 0)),
            pl.BlockSpec((tm, ROUTER_LANES), lambda i: (i, 0)),
        ],
        out_shape=[
            jax.ShapeDtypeStruct((n * (d // LANES), LANES), F32),
            jax.ShapeDtypeStruct((n, ROUTER_LANES), F32),
        ],
        compiler_params=_cparams("parallel"),
        name="moe_router",
    )(h2, gain.reshape(1, d), w_hi, w_lo, b_r.reshape(1, ROUTER_LANES))


def _moe_dispatch_kernel(tail_ref, n_used_ref, dst_ref, xn_hbm, xbuf_hbm, zero_ref, stage_ref, sem, lsem, zsem,
                         *, tm, s_n, n_blocks):
    i = pl.program_id(0)
    n_steps = pl.num_programs(0)
    tile_rows = zero_ref.shape[0]

    def zero_block(b, carry):
        pltpu.make_async_copy(zero_ref, xbuf_hbm.at[pl.ds(pl.multiple_of(b * tile_rows, tile_rows), tile_rows), :],
                              zsem).start()
        return carry

    def wait_zero_block(b, carry):
        pltpu.make_async_copy(zero_ref, xbuf_hbm.at[pl.ds(0, tile_rows), :], zsem).wait()
        return carry

    n_slots = stage_ref.shape[0]
    slot = i % n_slots
    nxt = (i + 1) % n_slots

    def load(step, s):
        return pltpu.make_async_copy(xn_hbm.at[pl.ds(pl.multiple_of(step * tm * s_n, s_n), tm * s_n), :],
                                     stage_ref.at[s], lsem.at[s])

    def wait_rows_out(s):
        for _ in range(2):
            pltpu.make_async_copy(stage_ref.at[s], xbuf_hbm.at[pl.ds(0, tm * s_n), :], sem.at[s]).wait()

    @pl.when(i == 0)
    def _():
        load(0, 0).start()
        zero_ref[...] = jnp.zeros_like(zero_ref)
        for e in range(tail_ref.shape[0]):
            @pl.when(tail_ref[e] >= 0)
            def _():
                pltpu.make_async_copy(zero_ref, xbuf_hbm.at[pl.ds(pl.multiple_of(tail_ref[e], s_n), tile_rows), :],
                                      zsem).start()
        lax.fori_loop(n_used_ref[0], n_blocks, zero_block, 0)
        for e in range(tail_ref.shape[0]):
            @pl.when(tail_ref[e] >= 0)
            def _():
                wait_zero_block(0, 0)
        lax.fori_loop(n_used_ref[0], n_blocks, wait_zero_block, 0)

    @pl.when(i >= n_slots - 1)
    def _():
        wait_rows_out(nxt)

    @pl.when(i + 1 < n_steps)
    def _():
        load(i + 1, nxt).start()

    load(i, slot).wait()
    for r in range(tm):
        src = stage_ref.at[slot, pl.ds(r * s_n, s_n), :]
        for k in range(2):
            dst = pl.multiple_of(dst_ref[k, r], s_n)
            pltpu.make_async_copy(src, xbuf_hbm.at[pl.ds(dst, s_n), :], sem.at[slot]).start(priority=k)

    @pl.when(i == n_steps - 1)
    def _():
        for back in range(n_slots - 1):
            @pl.when(i >= back)
            def _():
                wait_rows_out((i - back) % n_slots)


def moe_dispatch(xn_tiles, dst_tiles, tail_start, n_used, n_blocks, tm, s_n):
    n_steps = dst_tiles.shape[0]
    n_rows = n_blocks * MOE_TILE
    assert n_steps * tm * s_n == xn_tiles.shape[0]
    grid_spec = pltpu.PrefetchScalarGridSpec(
        num_scalar_prefetch=2,
        grid=(n_steps,),
        in_specs=[
            pl.BlockSpec((None, 2, tm), lambda i, tail, nu: (i, 0, 0), memory_space=pltpu.SMEM),
            pl.BlockSpec(memory_space=pl.ANY),
        ],
        out_specs=pl.BlockSpec(memory_space=pl.ANY),
        scratch_shapes=[
            pltpu.VMEM((MOE_TILE * s_n, LANES), F32),
            pltpu.VMEM((MOE_STAGE_SLOTS, tm * s_n, LANES), F32),
            pltpu.SemaphoreType.DMA((MOE_STAGE_SLOTS,)),
            pltpu.SemaphoreType.DMA((MOE_STAGE_SLOTS,)),
            pltpu.SemaphoreType.DMA(()),
        ],
    )
    return pl.pallas_call(
        functools.partial(_moe_dispatch_kernel, tm=tm, s_n=s_n, n_blocks=n_blocks),
        grid_spec=grid_spec,
        out_shape=jax.ShapeDtypeStruct((n_rows * s_n, LANES), F32),
        compiler_params=_cparams("arbitrary"),
        name="moe_dispatch",
    )(tail_start, n_used, dst_tiles, xn_tiles)


def _expert_kernel(blk_e_ref, n_used_ref, x_ref, w1_ref, w3_ref, w2_ref, y_ref, xb_ref, w1b_ref, w3b_ref, w2b_ref):
    i = pl.program_id(0)
    used = i < n_used_ref[0]
    tile, d = xb_ref.shape
    s_n = d // LANES
    prev_e = blk_e_ref[jnp.maximum(i - 1, 0)]
    fresh = (i == 0) | (blk_e_ref[i] != prev_e)

    @pl.when(used & fresh)
    def _():
        w1b_ref[...] = w1_ref[...].astype(BF16)
        w3b_ref[...] = w3_ref[...].astype(BF16)
        w2b_ref[...] = w2_ref[...].astype(BF16)

    @pl.when(used)
    def _():
        for s in range(s_n):
            xb_ref[:, s * LANES:(s + 1) * LANES] = _load_token_tile_cols(x_ref, s, tile, s_n).astype(BF16)
        x = xb_ref[...]
        a = _dot(x, w1b_ref[...])
        b = _dot(x, w3b_ref[...])
        hmid = (a * jax.nn.sigmoid(a) * b).astype(BF16)
        _store_token_tiles(y_ref, _dot(hmid, w2b_ref[...]))

    @pl.when(jnp.logical_not(used))
    def _():
        y_ref[...] = jnp.zeros_like(y_ref)


def moe_experts(xbuf, blk_e, n_used, w1, w3, w2, layer, n_blocks):
    d, f = w1.shape[2], w1.shape[3]
    s_n = d // LANES
    tile = MOE_TILE

    def w_map(i, blk_e_ref, n_used_ref):
        return (layer, blk_e_ref[i], 0, 0)

    def x_map(i, blk_e_ref, n_used_ref):
        return (jnp.minimum(i, jnp.maximum(n_used_ref[0] - 1, 0)), 0)

    grid_spec = pltpu.PrefetchScalarGridSpec(
        num_scalar_prefetch=2,
        grid=(n_blocks,),
        in_specs=[
            pl.BlockSpec((tile * s_n, LANES), x_map),
            pl.BlockSpec((None, None, d, f), w_map),
            pl.BlockSpec((None, None, d, f), w_map),
            pl.BlockSpec((None, None, f, d), w_map),
        ],
        out_specs=pl.BlockSpec((tile * s_n, LANES), lambda i, be, nu: (i, 0)),
        scratch_shapes=[
            pltpu.VMEM((tile, d), BF16),
            pltpu.VMEM((d, f), BF16),
            pltpu.VMEM((d, f), BF16),
            pltpu.VMEM((f, d), BF16),
        ],
    )
    return pl.pallas_call(
        _expert_kernel,
        grid_spec=grid_spec,
        out_shape=jax.ShapeDtypeStruct((n_blocks * tile * s_n, LANES), F32),
        compiler_params=_cparams("arbitrary"),
        name="moe_experts",
    )(blk_e, n_used, xbuf, w1, w3, w2)


def _moe_combine_kernel(src_ref, src_next_ref, h_ref, route_ref, y_hbm, o_ref, yg_ref, sem, *, n_steps):
    i = pl.program_id(0)
    slot = i & 1
    tm, d = h_ref.shape
    s_n = d // LANES

    def start_gather(ids_ref, dst_slot):
        for k in range(2):
            for r in range(tm):
                src = pl.multiple_of(ids_ref[k, r], s_n)
                pltpu.make_async_copy(y_hbm.at[pl.ds(src, s_n), :],
                                      yg_ref.at[dst_slot, k, pl.ds(r * s_n, s_n), :],
                                      sem.at[dst_slot]).start(priority=r % 2)

    @pl.when(i == 0)
    def _():
        start_gather(src_ref, 0)

    @pl.when(i + 1 < n_steps)
    def _():
        start_gather(src_next_ref, 1 - slot)

    for k in range(2):
        pltpu.make_async_copy(y_hbm.at[pl.ds(0, tm * s_n), :], yg_ref.at[slot, k], sem.at[slot]).wait()
    route = route_ref[...]
    g1 = route[:, 2:3]
    g2 = route[:, 3:4]
    for s in range(s_n):
        cols = slice(s * LANES, (s + 1) * LANES)
        o_ref[:, cols] = (h_ref[:, cols] + g1 * _load_token_tile_cols(yg_ref.at[slot, 0], s, tm, s_n)
                          + g2 * _load_token_tile_cols(yg_ref.at[slot, 1], s, tm, s_n))


def moe_combine(h2, route, y_tiles, src_tiles, tm):
    n, d = h2.shape
    s_n = d // LANES
    n_steps = src_tiles.shape[0]
    ids = lambda index: pl.BlockSpec((None, 2, tm), lambda i: (index(i), 0, 0), memory_space=pltpu.SMEM)
    return pl.pallas_call(
        functools.partial(_moe_combine_kernel, n_steps=n_steps),
        grid=(n_steps,),
        in_specs=[
            ids(lambda i: i),
            ids(lambda i: jnp.minimum(i + 1, n_steps - 1)),
            pl.BlockSpec((tm, d), lambda i: (i, 0)),
            pl.BlockSpec((tm, ROUTER_LANES), lambda i: (i, 0)),
            pl.BlockSpec(memory_space=pl.ANY),
        ],
        out_specs=pl.BlockSpec((tm, d), lambda i: (i, 0)),
        out_shape=jax.ShapeDtypeStruct((n, d), F32),
        scratch_shapes=[pltpu.VMEM((2, 2, tm * s_n, LANES), F32), pltpu.SemaphoreType.DMA((2,))],
        compiler_params=_cparams("arbitrary"),
        name="moe_combine",
    )(src_tiles, src_tiles, h2, route, y_tiles)


def hierarchical_moe_residual(h2, gain, w_group, b_group, w_expert, b_expert, w1, w3, w2, layer):
    n, d = h2.shape
    xn, route = moe_router(h2, gain, w_group, b_group, w_expert, b_expert)
    e_km = jnp.concatenate([route[:, 0], route[:, 1]]).astype(I32)
    n_assign = 2 * n
    onehot = (e_km[:, None] == jnp.arange(MOE_EXPERTS, dtype=I32)[None, :]).astype(I32)
    csum = jnp.cumsum(onehot, axis=0)
    counts = csum[-1]
    padded = (counts + MOE_TILE - 1) // MOE_TILE * MOE_TILE
    pad_end = jnp.cumsum(padded)
    pad_start = pad_end - padded
    dest = jnp.sum((csum - onehot + pad_start[None, :]) * onehot, axis=1)
    n_blocks = -(-n_assign // MOE_TILE) + MOE_EXPERTS
    blk_start = jnp.arange(n_blocks, dtype=I32) * MOE_TILE
    blk_e = jnp.minimum(jnp.sum((pad_end[None, :] <= blk_start[:, None]).astype(I32), axis=1),
                        MOE_EXPERTS - 1).astype(I32)
    n_used = (pad_end[-1] // MOE_TILE).astype(I32).reshape(1)
    s_n = d // LANES
    tm = _row_tile(n, MOE_TOKEN_TILE)
    dest_tiles = jnp.transpose((dest * s_n).astype(I32).reshape(2, n // tm, tm), (1, 0, 2))
    tail_start = jnp.where(counts > 0, (pad_end - MOE_TILE) * s_n, -1).astype(I32)
    xbuf = moe_dispatch(xn, dest_tiles, tail_start, n_used, n_blocks, tm, s_n)
    y = moe_experts(xbuf, blk_e, n_used, w1, w3, w2, layer, n_blocks)
    return moe_combine(h2, route, y, dest_tiles, tm)


def _final_norm_kernel(h_ref, g_ref, o_ref):
    o_ref[...] = _rms(h_ref[...], g_ref[...])


def final_norm(h3, gain):
    b, l, d = h3.shape
    t = l - N_META
    tm = _row_tile(t, 512)
    return pl.pallas_call(
        _final_norm_kernel,
        grid=(b, t // tm),
        in_specs=[
            pl.BlockSpec((None, pl.Element(tm), pl.Element(d)),
                         lambda bi, i: (bi, pl.multiple_of(N_META + i * tm, SUBLANES_BF16), 0)),
            pl.BlockSpec((1, d), lambda bi, i: (0, 0)),
        ],
        out_specs=pl.BlockSpec((None, tm, d), lambda bi, i: (bi, i, 0)),
        out_shape=jax.ShapeDtypeStruct((b, t, d), F32),
        compiler_params=_cparams("parallel", "parallel"),
        name="final_norm",
    )(h3, gain.reshape(1, d))


NA_QROWS = 8
NA_KROWS = 3 * NA_QROWS
NA_ROWS_PER_ITER = 4


def _na_kernel(q_ref, kw_ref, vw_ref, qm_ref, km_ref, vm_ref, bias_ref, o_ref, om_ref, *, rows, scale):
    blk = pl.program_id(1)
    tq = GRID_W
    n_pairs = q_ref.shape[1] // LANES
    base = jnp.clip(NA_QROWS * blk - NA_QROWS, 0, rows - NA_KROWS)
    lane = lax.broadcasted_iota(I32, (tq, LANES), 1)
    halves = [lane < HEAD_DIM, lane >= HEAD_DIM]

    pad = jnp.zeros((LANES - N_META, LANES), km_ref.dtype)
    k_meta = [jnp.concatenate([km_ref[:, p * LANES:(p + 1) * LANES], pad], axis=0) for p in range(n_pairs)]
    v_meta = [jnp.concatenate([vm_ref[:, p * LANES:(p + 1) * LANES], pad], axis=0) for p in range(n_pairs)]
    lane2 = lax.broadcasted_iota(I32, (2 * tq, LANES), 1)
    meta_bias2 = jnp.where(lane2 < N_META, 0.0, NEG_INF)

    n_win = NA_WIN_ROWS * GRID_W

    def row_body(jb, carry):
        colsl = [slice(p * LANES, (p + 1) * LANES) for p in range(n_pairs)]
        units = [(jj, p) for jj in range(NA_ROWS_PER_ITER) for p in range(n_pairs)]
        s_idx, koff, qoff = [], [], []
        for jj in range(NA_ROWS_PER_ITER):
            j = jb * NA_ROWS_PER_ITER + jj
            r = NA_QROWS * blk + j
            start = jnp.clip(r - NA_WIN_ROWS // 2, 0, rows - NA_WIN_ROWS)
            s_idx.append(start - r + (NA_WIN_ROWS - 1))
            koff.append(pl.multiple_of((start - base) * GRID_W, GRID_W))
            qoff.append(pl.multiple_of(j * tq, tq))
        q_pair = {(jj, p): q_ref[pl.ds(qoff[jj], tq), c]
                  for jj in range(NA_ROWS_PER_ITER) for p, c in enumerate(colsl)}
        k_ext = {(jj, p): jnp.concatenate([kw_ref[pl.ds(koff[jj], n_win), c], k_meta[p]], axis=0)
                 for jj in range(NA_ROWS_PER_ITER) for p, c in enumerate(colsl)}
        v_ext = {(jj, p): jnp.concatenate([vw_ref[pl.ds(koff[jj], n_win), c], v_meta[p]], axis=0)
                 for jj in range(NA_ROWS_PER_ITER) for p, c in enumerate(colsl)}
        qh = [_stack_heads(q_pair[jj, p], halves[0]) for jj, p in units]
        s = [_dot_nt(qh[u], k_ext[jj, p]) * scale + jnp.concatenate([bias_ref[p, s_idx[jj]], meta_bias2], axis=1)
             for u, (jj, p) in enumerate(units)]
        m = [jnp.max(x, axis=-1, keepdims=True) for x in s]
        e = [jnp.exp(x - mx) for x, mx in zip(s, m)]
        den = [jnp.sum(x, axis=-1, keepdims=True) for x in e]
        o = [_dot(e[u].astype(BF16), v_ext[jj, p]) / den[u] for u, (jj, p) in enumerate(units)]
        for u, (jj, p) in enumerate(units):
            o_ref[pl.ds(qoff[jj], tq), colsl[p]] = jnp.where(halves[0], o[u][0:tq], o[u][tq:2 * tq]).astype(o_ref.dtype)
        return carry

    lax.fori_loop(0, NA_QROWS // NA_ROWS_PER_ITER, row_body, 0)

    @pl.when(blk == 0)
    def _():
        lane_m = lax.broadcasted_iota(I32, (N_META, LANES), 1)
        for p in range(n_pairs):
            cols = slice(p * LANES, (p + 1) * LANES)
            q_pair = qm_ref[:, cols]
            kmp = km_ref[:, cols]
            vmp = vm_ref[:, cols]
            outs = []
            for hh in range(2):
                sel = (lane_m < HEAD_DIM) if hh == 0 else (lane_m >= HEAD_DIM)
                qp = jnp.where(sel, q_pair, jnp.zeros_like(q_pair))
                s_m = _dot_nt(qp, kmp) * scale
                p_m = jnp.exp(s_m - jnp.max(s_m, axis=-1, keepdims=True))
                den = jnp.sum(p_m, axis=-1, keepdims=True)
                outs.append(_dot(p_m.astype(BF16), vmp) / den)
            om_ref[:, cols] = jnp.where(lane_m < HEAD_DIM, outs[0], outs[1]).astype(om_ref.dtype)


def _na_bias_table(rpb):
    h = rpb.shape[0]
    c_ids = jnp.arange(GRID_W)
    c_start = jnp.clip(c_ids - NA_WIN_COLS // 2, 0, GRID_W - NA_WIN_COLS)
    in_band = (c_ids[None, :] >= c_start[:, None]) & (c_ids[None, :] < c_start[:, None] + NA_WIN_COLS)
    dc = jnp.clip(c_ids[None, :] - c_ids[:, None] + NA_WIN_COLS - 1, 0, 2 * NA_WIN_COLS - 2)
    tab = jnp.where(in_band[None, None], rpb.astype(F32)[:, :, dc], NEG_INF)
    win = jnp.stack([tab[:, s:s + NA_WIN_ROWS] for s in range(NA_WIN_ROWS)], axis=1)
    per_head = jnp.transpose(win, (0, 1, 3, 2, 4)).reshape(h // 2, 2, NA_WIN_ROWS, GRID_W, NA_WIN_ROWS * GRID_W)
    return jnp.transpose(per_head, (0, 2, 1, 3, 4)).reshape(h // 2, NA_WIN_ROWS, 2 * GRID_W, NA_WIN_ROWS * GRID_W)


def na_attention(qkv, rpb):
    b, l, w3 = qkv.shape
    w = w3 // 3
    t = l - N_META
    rows = t // GRID_W
    assert rows * GRID_W == t and rows % NA_QROWS == 0 and rows >= NA_KROWS
    tq = NA_QROWS * GRID_W
    tk = NA_KROWS * GRID_W
    bias = _na_bias_table(rpb)
    al = SUBLANES_BF16

    def q_map(bi, i):
        return (bi, pl.multiple_of(N_META + i * tq, al), 0)

    def kv_map(col):
        def f(bi, i):
            base = jnp.clip(NA_QROWS * i - NA_QROWS, 0, rows - NA_KROWS)
            return (bi, pl.multiple_of(N_META + base * GRID_W, al), col)
        return f

    def meta_map(col):
        return lambda bi, i: (bi, 0, col)

    el = pl.Element
    grid_out, meta_out = pl.pallas_call(
        functools.partial(_na_kernel, rows=rows, scale=HEAD_DIM ** -0.5),
        grid=(b, rows // NA_QROWS),
        in_specs=[
            pl.BlockSpec((None, el(tq), el(w)), q_map),
            pl.BlockSpec((None, el(tk), el(w)), kv_map(w)),
            pl.BlockSpec((None, el(tk), el(w)), kv_map(2 * w)),
            pl.BlockSpec((None, el(N_META), el(w)), meta_map(0)),
            pl.BlockSpec((None, el(N_META), el(w)), meta_map(w)),
            pl.BlockSpec((None, el(N_META), el(w)), meta_map(2 * w)),
            pl.BlockSpec(bias.shape, lambda bi, i: (0, 0, 0, 0)),
        ],
        out_specs=[
            pl.BlockSpec((None, tq, w), lambda bi, i: (bi, i, 0)),
            pl.BlockSpec((None, N_META, w), lambda bi, i: (bi, 0, 0)),
        ],
        out_shape=[
            jax.ShapeDtypeStruct((b, t, w), BF16),
            jax.ShapeDtypeStruct((b, N_META, w), BF16),
        ],
        compiler_params=_cparams("parallel", "arbitrary"),
        name="na_attention",
    )(qkv, qkv, qkv, qkv, qkv, qkv, bias)
    return jnp.concatenate([meta_out, grid_out], axis=1)


RWKV_CHUNK = 64
RWKV_TILE_CHUNKS = 2
RWKV_HALO = 8


def _split3_bf16(x):
    p1 = x.astype(BF16)
    r1 = x - p1.astype(F32)
    p2 = r1.astype(BF16)
    p3 = (r1 - p2.astype(F32)).astype(BF16)
    return p1, p2, p3


def _mm1(a, b):
    return _dot(a.astype(BF16), b.astype(BF16))


def _mm3(a, b):
    ah, al = _split_bf16(a)
    bh, bl = _split_bf16(b)
    return _dot(ah, bh) + _dot(ah, bl) + _dot(al, bh)


def _mm1_nt(a, b):
    return _dot_nt(a.astype(BF16), b.astype(BF16))


def _mm3_nt(a, b):
    ah, al = _split_bf16(a)
    bh, bl = _split_bf16(b)
    return _dot_nt(ah, bh) + _dot_nt(ah, bl) + _dot_nt(al, bh)


def _exact_left(mat_bf16, x):
    p1, p2, p3 = _split3_bf16(x)
    return _dot(mat_bf16, p1) + _dot(mat_bf16, p2) + _dot(mat_bf16, p3)


def _exact_right(x, mat_bf16):
    p1, p2, p3 = _split3_bf16(x)
    return _dot(p1, mat_bf16) + _dot(p2, mat_bf16) + _dot(p3, mat_bf16)


def _head_block_ones(width):
    ri = lax.broadcasted_iota(I32, (width, width), 0) // HEAD_DIM
    ci = lax.broadcasted_iota(I32, (width, width), 1) // HEAD_DIM
    return (ri == ci).astype(BF16)


def _head_sums(x, exact):
    ones_pair = _head_block_ones(LANES)
    tiles = []
    for p in range(x.shape[1] // LANES):
        xt = x[:, p * LANES:(p + 1) * LANES]
        tiles.append(_exact_right(xt, ones_pair) if exact else _dot(xt.astype(BF16), ones_pair))
    return jnp.concatenate(tiles, axis=1)


def _stack_heads(x, m0):
    z = jnp.zeros_like(x)
    return jnp.concatenate([jnp.where(m0, x, z), jnp.where(m0, z, x)], axis=0)


_MM_L4 = _mm1_nt
_MM_KT = _mm1_nt
_MM_SQ = _mm1
_MM_AP = _mm1
_MM_V = _mm1
_MM_Y = _mm1
_MM_UPD = _mm1


def _rwkv_chunk_maps(streams, c, n_sub):
    assert c == 64
    c2 = 2 * c
    lane = lax.broadcasted_iota(I32, (c, LANES), 1)
    m0 = lane < HEAD_DIM
    r_i = lax.broadcasted_iota(I32, (c2, c2), 0)
    c_i = lax.broadcasted_iota(I32, (c2, c2), 1)
    eye = (r_i == c_i).astype(F32)
    rel = r_i % c - c_i % c
    masks = {sg: (rel * sg > 0, rel * sg >= 0) for sg in {s["sign"] for s in streams}}
    items = [(j, q) for j in range(len(streams)) for q in range(n_sub)]

    def part(j, q, name):
        return _stack_heads(streams[j][name][q * c:(q + 1) * c], m0)

    lhs = {it: jnp.concatenate([part(*it, "kkp"), part(*it, "rp")], axis=0) for it in items}
    rhs = {it: jnp.concatenate([part(*it, "ki"), part(*it, "bi")], axis=0) for it in items}
    vs = {it: part(*it, "v") for it in items}
    kipcs = {it: part(*it, "kipc") for it in items}
    bipcs = {it: part(*it, "bipc") for it in items}
    l4 = {it: _MM_L4(lhs[it], rhs[it]) for it in items}
    m_kk, n1, m_rk, m_rb = {}, {}, {}, {}
    for it in items:
        strict, incl = masks[streams[it[0]]["sign"]]
        m = l4[it]
        m_kk[it] = jnp.where(strict, m[0:c2, 0:c2], 0.0)
        n1[it] = jnp.where(strict, m[0:c2, c2:2 * c2], 0.0)
        m_rk[it] = jnp.where(incl, m[c2:2 * c2, 0:c2], 0.0)
        m_rb[it] = jnp.where(incl, m[c2:2 * c2, c2:2 * c2], 0.0)
    n2 = {it: _MM_SQ(n1[it], n1[it]) for it in items}
    n4 = {it: _MM_SQ(n2[it], n2[it]) for it in items}
    n8 = {it: _MM_SQ(n4[it], n4[it]) for it in items}
    n16 = {it: _MM_SQ(n8[it], n8[it]) for it in items}
    n32 = {it: _MM_SQ(n16[it], n16[it]) for it in items}
    p1 = {it: (eye - n1[it]) + _MM_AP(eye - n1[it], n2[it]) for it in items}
    p2 = {it: eye + n4[it] + n8[it] + _MM_AP(n4[it], n8[it]) for it in items}
    p3 = {it: eye + n16[it] + n32[it] + _MM_AP(n16[it], n32[it]) for it in items}
    p23 = {it: _MM_AP(p2[it], p3[it]) for it in items}
    winv = {it: _MM_AP(p1[it], p23[it]) for it in items}
    mv = {it: _MM_V(m_kk[it], vs[it]) for it in items}
    mrv = {it: _MM_Y(m_rk[it], vs[it]) for it in items}
    wl = {it: _MM_AP(winv[it], lhs[it][0:c2]) for it in items}
    wmv = {it: _MM_AP(winv[it], mv[it]) for it in items}
    yl = {it: lhs[it][c2:2 * c2] - _MM_Y(m_rb[it], wl[it]) for it in items}
    y0 = {it: mrv[it] - _MM_Y(m_rb[it], wmv[it]) for it in items}
    g2 = {it: _MM_UPD(jnp.transpose(wl[it]), bipcs[it]) for it in items}
    hh = {it: _MM_UPD(jnp.transpose(jnp.concatenate([vs[it], -wmv[it]], axis=0)),
                      jnp.concatenate([kipcs[it], bipcs[it]], axis=0)) for it in items}
    return {it: (yl[it], y0[it], g2[it], hh[it]) for it in items}


def _rwkv_apply_maps(streams, maps, c, n_sub):
    c2 = 2 * c
    st = [s["st"] for s in streams]
    ys = {}
    for k in range(n_sub):
        cur = [(j, k if s["sign"] > 0 else n_sub - 1 - k) for j, s in enumerate(streams)]
        sg = [_MM_UPD(st[j], maps[j][q][2]) for j, q in cur]
        yk = [_MM_KT(maps[j][q][0], st[j]) + maps[j][q][1] for j, q in cur]
        for it, yi in zip(cur, yk):
            ys[it] = yi[0:c] + yi[c:c2]
        st = [st[j] * streams[j]["pc"][q] - sgi + maps[j][q][3] for (j, q), sgi in zip(cur, sg)]
    return [(jnp.concatenate([ys[j, q] for q in range(n_sub)], axis=0), st[j]) for j in range(len(streams))]


def _softplus(z):
    return jnp.maximum(z, 0.0) + jnp.log(1.0 + jnp.exp(-jnp.abs(z)))


def _rwkv_tile_prep(x_ref, xp_ref, xn_ref, tile, n_tiles, seq_len, width, sign, mu, w0, a0, w_wa,
                    k_k, k_a, r_k):
    cs = RWKV_CHUNK
    c = x_ref.shape[0]
    valid = jnp.minimum(c, seq_len - tile * c)
    row = lax.broadcasted_iota(I32, (c, LANES), 0)
    rowv = row < valid
    lane = lax.broadcasted_iota(I32, (c, LANES), 1)
    ones_pair = _head_block_ones(LANES)

    def shifted(lo):
        cols = slice(lo, lo + LANES)
        x = jnp.where(rowv, x_ref[:, cols], 0.0)
        prev_row = jnp.where(tile > 0, xp_ref[RWKV_HALO - 1:RWKV_HALO, cols], 0.0)
        next_row = jnp.where(tile < n_tiles - 1, xn_ref[0:1, cols], 0.0)
        x_prev = jnp.where(row == 0, prev_row, pltpu.roll(x, 1, 0))
        x_next = jnp.where(row == c - 1, next_row, pltpu.roll(x, c - 1, 0))
        xs = x + mu[:, cols] * (0.5 * (x_prev + x_next) - x)
        return jnp.where(rowv, xs, 0.0)

    wa = shifted(3 * width)
    g_lo = shifted(3 * width + LANES)
    xwa = jnp.where(lane < LANES // 2, jnp.tanh(wa), wa)
    la = _dot(xwa.astype(BF16), w_wa)

    t_i = lax.broadcasted_iota(I32, (c, c), 0)
    s_i = lax.broadcasted_iota(I32, (c, c), 1)
    tri = ((t_i // cs == s_i // cs) & ((t_i - s_i) * sign >= 0)).astype(BF16)

    pairs = []
    for p in range(width // LANES):
        lo = p * LANES
        cols = slice(lo, lo + LANES)
        r = shifted(lo)
        k = shifted(width + lo)
        v = shifted(2 * width + lo)
        w_log = -_softplus(-(w0[:, cols] + la[:, cols])) - 0.5
        logw = jnp.where(rowv, -jnp.exp(w_log), 0.0)
        a = jax.nn.sigmoid(a0[:, cols] + la[:, width + lo:width + lo + LANES])
        kk0 = k * k_k[:, cols]
        ss = _dot((kk0 * kk0).astype(BF16), ones_pair)
        kk = kk0 / jnp.maximum(jnp.sqrt(ss), 1e-12)
        kdir = k * (1.0 + (a - 1.0) * k_a[:, cols])
        b = kk * a
        cl = _exact_left(tri, logw)
        lasts = [cl[q * cs + cs - 1:q * cs + cs, :] if sign > 0 else cl[q * cs:q * cs + 1, :]
                 for q in range(c // cs)]
        last = jnp.concatenate([jnp.broadcast_to(lq, (cs, LANES)) for lq in lasts], axis=0)
        e_n = jnp.exp(-cl)
        pcr = jnp.exp(last - cl)
        pairs.append(dict(kkp=kk * jnp.exp(cl - logw), rp=r * jnp.exp(cl), ki=kdir * e_n, bi=b * e_n,
                          kipc=kdir * pcr, bipc=b * pcr, v=v, pc=[jnp.exp(lq) for lq in lasts],
                          bonus=_dot((r * kdir * r_k[:, cols]).astype(BF16), ones_pair) * v))
    return pairs, g_lo


def _rwkv_scan_kernel(xf_ref, xfp_ref, xfn_ref, xb_ref, xbp_ref, xbn_ref, mu_ref, w0_ref, a0_ref, wwa_ref,
                      gup_ref, kk_ref, ka_ref, rk_ref, yf_ref, yb_ref, bonf_ref, bonb_ref, g_ref, st_ref,
                      *, seq_len, width):
    i = pl.program_id(1)
    n_chunks = pl.num_programs(1)
    n_pairs = width // LANES

    @pl.when(i == 0)
    def _():
        st_ref[...] = jnp.zeros_like(st_ref)

    common = (mu_ref[...],)
    tail = (kk_ref[...], ka_ref[...], rk_ref[...])
    n_sub = xf_ref.shape[0] // RWKV_CHUNK
    fwd, g_lo = _rwkv_tile_prep(xf_ref, xfp_ref, xfn_ref, i, n_chunks, seq_len, width, 1, *common,
                                w0_ref[0], a0_ref[0], wwa_ref[0], *tail)
    bwd, _ = _rwkv_tile_prep(xb_ref, xbp_ref, xbn_ref, n_chunks - 1 - i, n_chunks, seq_len, width, -1, *common,
                             w0_ref[1], a0_ref[1], wwa_ref[1], *tail)
    g_ref[...] = _mm1(jax.nn.sigmoid(g_lo), gup_ref[...]).astype(g_ref.dtype)
    streams = []
    for di, (pairs, sign, bon_ref) in enumerate(((fwd, 1, bonf_ref), (bwd, -1, bonb_ref))):
        for p, s in enumerate(pairs):
            bon_ref[:, p * LANES:(p + 1) * LANES] = s.pop("bonus")
            s["st"] = st_ref[di * n_pairs + p]
            s["sign"] = sign
            streams.append(s)
    m = _rwkv_chunk_maps(streams, RWKV_CHUNK, n_sub)
    maps = [{q: m[j, q] for q in range(n_sub)} for j in range(len(streams))]
    res = _rwkv_apply_maps(streams, maps, RWKV_CHUNK, n_sub)
    for j, (y, st_new) in enumerate(res):
        di, p = divmod(j, n_pairs)
        cols = slice(p * LANES, (p + 1) * LANES)
        (yf_ref if di == 0 else yb_ref)[:, cols] = y
        st_ref[j] = st_new


def _rwkv_finish_kernel(yf_ref, yb_ref, bonf_ref, bonb_ref, g_ref, lg_ref, lb_ref, o_ref):
    y = yf_ref[...] + yb_ref[...]
    mean = _head_sums(y, exact=True) * (1.0 / HEAD_DIM)
    yc = y - mean
    var = _head_sums(yc * yc, exact=True) * (1.0 / HEAD_DIM)
    yn = yc * lax.rsqrt(var + RWKV_GN_EPS) * lg_ref[...] + lb_ref[...]
    o_ref[...] = ((yn + bonf_ref[...] + bonb_ref[...]) * g_ref[...].astype(F32)).astype(o_ref.dtype)


def rwkv_mix(rest, mu, w0, w_up, a0, a_up, g_up, k_k, k_a, r_k, lnx_g, lnx_b):
    bsz, l, n_cols = rest.shape
    width = w0.shape[1]
    rank = w_up.shape[1]
    assert n_cols == 3 * width + 2 * LANES and 2 * rank == LANES and l % RWKV_HALO == 0
    c = RWKV_CHUNK * RWKV_TILE_CHUNKS
    n_chunks = -(-l // c)
    per = c // RWKV_HALO
    n_halo = l // RWKV_HALO
    zeros = jnp.zeros((2, rank, width), F32)
    w_wa = jnp.concatenate([jnp.concatenate([w_up.astype(F32), zeros], axis=2),
                            jnp.concatenate([zeros, a_up.astype(F32)], axis=2)], axis=1)
    w_wa = w_wa.astype(BF16)

    fwd_chunk = lambda i: i
    bwd_chunk = lambda i: n_chunks - 1 - i

    def tile_specs(chunk_of):
        return [
            pl.BlockSpec((None, c, n_cols), lambda b, i: (b, chunk_of(i), 0)),
            pl.BlockSpec((None, RWKV_HALO, n_cols), lambda b, i: (b, jnp.maximum(chunk_of(i) * per - 1, 0), 0)),
            pl.BlockSpec((None, RWKV_HALO, n_cols),
                         lambda b, i: (b, jnp.minimum((chunk_of(i) + 1) * per, n_halo - 1), 0)),
        ]

    row2 = lambda a: a.astype(F32).reshape(1, -1)
    whole = lambda *shape: pl.BlockSpec(shape, lambda b, i: (0,) * len(shape))
    out_spec = lambda chunk_of: pl.BlockSpec((None, c, width), lambda b, i: (b, chunk_of(i), 0))
    act = lambda dt: jax.ShapeDtypeStruct((bsz, l, width), dt)
    y_f, y_b, bon_f, bon_b, g = pl.pallas_call(
        functools.partial(_rwkv_scan_kernel, seq_len=l, width=width),
        grid=(bsz, n_chunks),
        in_specs=tile_specs(fwd_chunk) + tile_specs(bwd_chunk) + [
            whole(1, n_cols),
            whole(2, 1, width), whole(2, 1, width),
            whole(2, LANES, 2 * width),
            whole(LANES, width),
            whole(1, width), whole(1, width), whole(1, width),
        ],
        out_specs=[out_spec(fwd_chunk), out_spec(bwd_chunk), out_spec(fwd_chunk), out_spec(bwd_chunk),
                   out_spec(fwd_chunk)],
        out_shape=[act(F32), act(F32), act(F32), act(F32), act(BF16)],
        scratch_shapes=[pltpu.VMEM((2 * (width // LANES), LANES, LANES), F32)],
        compiler_params=_cparams("parallel", "arbitrary"),
        name="rwkv_scan",
    )(rest, rest, rest, rest, rest, rest, row2(mu), w0.astype(F32).reshape(2, 1, width),
      a0.astype(F32).reshape(2, 1, width), w_wa, g_up.astype(BF16), row2(k_k), row2(k_a), row2(r_k))

    n = bsz * l
    tm = _row_tile(n, 608)
    rows = lambda: pl.BlockSpec((tm, width), lambda j: (j, 0))
    flat = lambda a: a.reshape(n, width)
    return pl.pallas_call(
        _rwkv_finish_kernel,
        grid=(n // tm,),
        in_specs=[rows(), rows(), rows(), rows(), rows(),
                  pl.BlockSpec((1, width), lambda j: (0, 0)),
                  pl.BlockSpec((1, width), lambda j: (0, 0))],
        out_specs=rows(),
        out_shape=jax.ShapeDtypeStruct((n, width), BF16),
        compiler_params=_cparams("parallel"),
        name="rwkv_finish",
    )(flat(y_f), flat(y_b), flat(bon_f), flat(bon_b), flat(g), row2(lnx_g), row2(lnx_b)).reshape(bsz, l, width)


S5_CHUNK = 16


def _cpow(n, lr, li, step):
    mag = jnp.exp(n * (lr * step))
    ang = n * (li * step)
    return mag * jnp.cos(ang), mag * jnp.sin(ang)


def _s5_param_kernel(lamr_ref, stepr_ref, bt_ref, ct_ref, kmat_ref, wst_ref, cexp_ref, alpha_ref):
    t_len = S5_CHUNK
    n_i = S5_GROUP_CH
    p2 = 2 * S5_STATE
    ti = t_len * n_i

    lr = lamr_ref[0:1, :]
    li = lamr_ref[1:2, :]
    step = jnp.exp(stepr_ref[...])
    ab_re, ab_im = _cpow(1.0, lr, li, step)
    den = lr * lr + li * li
    z_re = ((ab_re - 1.0) * lr + ab_im * li) / den
    z_im = (ab_im * lr - (ab_re - 1.0) * li) / den
    t16 = lax.broadcasted_iota(I32, (t_len, p2), 0).astype(F32)
    is_f = lax.broadcasted_iota(I32, (t_len, p2), 1) < S5_STATE

    def rows_by_t(x):
        return jnp.concatenate([jnp.broadcast_to(x[t:t + 1], (n_i, p2)) for t in range(t_len)], axis=0)

    def tiled_rows(x):
        return jnp.concatenate([x] * t_len, axis=0)

    def pow_rows(n):
        q_re, q_im = _cpow(n, lr, li, step)
        return rows_by_t(q_re), rows_by_t(q_im)

    bt_re = tiled_rows(bt_ref[0])
    bt_im = tiled_rows(bt_ref[1])
    bb_re = z_re * bt_re - z_im * bt_im
    bb_im = z_re * bt_im + z_im * bt_re
    pw_re, pw_im = pow_rows(jnp.where(is_f, (t_len - 1.0) - t16, t16))
    wst_ref[:, 0:p2] = (pw_re * bb_re - pw_im * bb_im).astype(wst_ref.dtype)
    wst_ref[:, p2:2 * p2] = (pw_re * bb_im + pw_im * bb_re).astype(wst_ref.dtype)
    al_re, al_im = _cpow(float(t_len), lr, li, step)
    alpha_ref[0:1, :] = al_re
    alpha_ref[1:2, :] = al_im

    ct_re = tiled_rows(ct_ref[0])
    ct_im = tiled_rows(ct_ref[1])

    def c_times_pow(n):
        q_re, q_im = pow_rows(n)
        return jnp.transpose(ct_re * q_re - ct_im * q_im), jnp.transpose(ct_re * q_im + ct_im * q_re)

    ca_re, ca_im = c_times_pow(jnp.where(is_f, t16, jnp.where(t16 == 0.0, 0.0, t_len - t16)))
    lane_p = lax.broadcasted_iota(I32, (n_i, p2), 1)
    bbr = bb_re[0:n_i]
    bbi = bb_im[0:n_i]
    zero = jnp.zeros_like(bbr)
    strips = []
    for sel in (lane_p < S5_STATE, lane_p >= S5_STATE):
        strips.append(_mm3(jnp.where(sel, bbr, zero), ca_re) - _mm3(jnp.where(sel, bbi, zero), ca_im))
    strip_f, strip_b = strips
    t_k = lax.broadcasted_iota(I32, (n_i, ti), 1) // n_i
    for tt in range(t_len):
        sf = strip_f if tt == 0 else pltpu.roll(strip_f, tt * n_i, 1)
        sb = strip_b if tt == 0 else pltpu.roll(strip_b, tt * n_i, 1)
        blk = jnp.where(t_k >= tt, sf, 0.0) + jnp.where(t_k <= tt, sb, 0.0)
        kmat_ref[tt * n_i:(tt + 1) * n_i, :] = blk.astype(kmat_ref.dtype)

    co_re, co_im = c_times_pow(jnp.where(is_f, t16 + 1.0, t_len - t16))
    cexp_ref[0:p2, :] = co_re.astype(cexp_ref.dtype)
    cexp_ref[p2:2 * p2, :] = (-co_im).astype(cexp_ref.dtype)


def _s5_main_kernel(u_ref, kmat_ref, wst_ref, cexp_ref, alpha_ref, y_ref, x_ref, sf_ref, sb_ref,
                    *, n_batch, n_chunks):
    p2 = 2 * S5_STATE
    n_gb = u_ref.shape[0]
    for g in range(n_gb):
        x_ref[g] = _dot(u_ref[g].astype(BF16), wst_ref[g])
    lane = lax.broadcasted_iota(I32, (1, p2), 1)
    is_f = lane < S5_STATE
    alphas = [(alpha_ref[g, 0:1, :], alpha_ref[g, 1:2, :]) for g in range(n_gb)]

    sub = S5_SCAN_ROWS
    assert n_chunks % sub == 0
    chains = [(g, b) for g in range(n_gb) for b in range(n_batch)]

    def step(k, carry):
        new = []
        for (g, b), (s_re, s_im) in zip(chains, carry):
            a_re, a_im = alphas[g]
            row_f = pl.multiple_of(b * n_chunks + sub * k, sub)
            row_b = pl.multiple_of(b * n_chunks + (n_chunks - sub) - sub * k, sub)
            xf = x_ref[g, pl.ds(row_f, sub), :]
            xb = x_ref[g, pl.ds(row_b, sub), :]
            seen = []
            for r in range(sub):
                seen.append(jnp.concatenate([s_re, s_im], axis=1))
                rb = sub - 1 - r
                x_re = jnp.where(is_f, xf[r:r + 1, 0:p2], xb[rb:rb + 1, 0:p2])
                x_im = jnp.where(is_f, xf[r:r + 1, p2:2 * p2], xb[rb:rb + 1, p2:2 * p2])
                s_re, s_im = a_re * s_re - a_im * s_im + x_re, a_re * s_im + a_im * s_re + x_im
            sf_ref[g, pl.ds(row_f, sub), :] = jnp.concatenate(seen, axis=0)
            sb_ref[g, pl.ds(row_b, sub), :] = jnp.concatenate(seen[::-1], axis=0)
            new.append((s_re, s_im))
        return tuple(new)

    zero = jnp.zeros((1, p2), F32)
    lax.fori_loop(0, n_chunks // sub, step, tuple((zero, zero) for _ in chains))
    lane2 = lax.broadcasted_iota(I32, sf_ref.shape[1:], 1) % p2
    for g in range(n_gb):
        s_in = jnp.where(lane2 < S5_STATE, sf_ref[g], sb_ref[g])
        s_hi, s_lo = _split_bf16(s_in)
        y_ref[g] = (_dot(u_ref[g].astype(BF16), kmat_ref[g]) + _dot(s_hi, cexp_ref[g]) + _dot(s_lo, cexp_ref[g]))


S5_RELAYOUT_CHUNKS = 128
S5_SCAN_ROWS = 8
S5_GROUPS_PER_STEP = 2


def _s5_group_major_kernel(h_ref, g_ref, u_ref, hn_ref, ut_ref, *, seq_len):
    n_g, mt, ti = u_ref.shape
    t_len = S5_CHUNK
    n_i = ti // t_len
    n_lt = hn_ref.shape[0]
    g_lt = LANES // n_i
    rows = h_ref.shape[0]
    valid = seq_len - pl.program_id(1) * rows
    row = lax.broadcasted_iota(I32, h_ref.shape, 0)
    hn = jnp.where(row < valid, _rms(h_ref[...], g_ref[...]), 0.0)
    for j in range(n_lt):
        hn_ref[j] = hn[:, j * LANES:(j + 1) * LANES]
    for tau in range(t_len):
        for j in range(n_lt):
            xt = jnp.transpose(hn_ref[j, pl.ds(tau, mt, stride=t_len), :])
            ut_ref[j * g_lt:(j + 1) * g_lt, tau * n_i:(tau + 1) * n_i, :] = xt.reshape(g_lt, n_i, mt)
    for g in range(n_g):
        u_ref[g] = jnp.transpose(ut_ref[g]).astype(u_ref.dtype)


def _s5_token_major_kernel(y_ref, o_ref, zt_ref, z_ref):
    n_g, mt, ti = y_ref.shape
    t_len = S5_CHUNK
    n_i = ti // t_len
    n_lt = z_ref.shape[0]
    for g in range(n_g):
        yt = jnp.transpose(y_ref[g])
        zt_ref[:, g * n_i:(g + 1) * n_i, :] = yt.reshape(t_len, n_i, mt)
    for t in range(t_len):
        for j in range(n_lt):
            z_ref[j, pl.ds(t, mt, stride=t_len), :] = jnp.transpose(zt_ref[t, j * LANES:(j + 1) * LANES, :])
    for j in range(n_lt):
        o_ref[:, j * LANES:(j + 1) * LANES] = z_ref[j]


def _gelu_tanh(x):
    return 0.5 * x * (1.0 + jnp.tanh(math.sqrt(2.0 / math.pi) * (x + 0.044715 * (x * x * x))))


def _s5_glu_kernel(h_ref, y_ref, g_ref, d_ref, w_ref, o_ref):
    h = h_ref[...]
    dm = h.shape[1]
    y = y_ref[...] + d_ref[...] * _rms(h, g_ref[...])
    gl = _gelu_tanh(y).astype(BF16)
    a = _dot(gl, w_ref[:, 0:dm])
    b = _dot(gl, w_ref[:, dm:2 * dm])
    o_ref[...] = h + a * jax.nn.sigmoid(b)


def s5_mix(h3, gain, b_re, b_im, lam_re, lam_im, log_step, c_re, c_im, d_skip, w_glu):
    bsz, l, dm = h3.shape
    n_g, n_p, n_i = b_re.shape
    t_len = S5_CHUNK
    assert l % t_len == 0 and n_g * n_i == dm and n_p == S5_STATE and n_i == S5_GROUP_CH
    n_chunks = -(-(l // t_len) // S5_SCAN_ROWS) * S5_SCAN_ROWS
    m = bsz * n_chunks
    ti = t_len * n_i
    p2 = 2 * n_p
    n = bsz * l
    tm = _row_tile(n, 608)
    h2 = h3.reshape(n, dm)
    gain2 = gain.astype(F32).reshape(1, dm)

    mt = min(S5_RELAYOUT_CHUNKS, n_chunks)
    n_tiles = -(-n_chunks // mt)
    u = pl.pallas_call(
        functools.partial(_s5_group_major_kernel, seq_len=l),
        grid=(bsz, n_tiles),
        in_specs=[pl.BlockSpec((None, mt * t_len, dm), lambda b, i: (b, i, 0)),
                  pl.BlockSpec((1, dm), lambda b, i: (0, 0))],
        out_specs=pl.BlockSpec((n_g, None, mt, ti), lambda b, i: (0, b, i, 0)),
        out_shape=jax.ShapeDtypeStruct((n_g, bsz, n_chunks, ti), BF16),
        scratch_shapes=[pltpu.VMEM((dm // LANES, mt * t_len, LANES), F32), pltpu.VMEM((n_g, ti, mt), F32)],
        compiler_params=_cparams("parallel", "parallel"),
        name="s5_group_major",
    )(h3, gain2).reshape(n_g, m, ti)

    f32 = lambda a: a.astype(F32)
    lam_r = jnp.stack([jnp.concatenate([f32(lam_re)[0], f32(lam_re)[1]], axis=-1),
                       jnp.concatenate([f32(lam_im)[0], f32(lam_im)[1]], axis=-1)], axis=1)
    step_r = jnp.repeat(jnp.transpose(f32(log_step))[:, None, :], n_p, axis=2)
    bt = jnp.stack([jnp.transpose(f32(b_re), (0, 2, 1)), jnp.transpose(f32(b_im), (0, 2, 1))], axis=1)
    bt = jnp.tile(bt, (1, 1, 1, 2))
    ct = jnp.stack([f32(c_re), f32(c_im)], axis=0)
    ct = jnp.transpose(ct, (2, 0, 3, 1, 4)).reshape(n_g, 2, n_i, p2)

    gspec = lambda *shape: pl.BlockSpec((None,) + shape, lambda g: (g,) + (0,) * len(shape))
    kmat, wst, cexp, alpha = pl.pallas_call(
        _s5_param_kernel,
        grid=(n_g,),
        in_specs=[gspec(2, p2), gspec(1, p2), gspec(2, n_i, p2), gspec(2, n_i, p2)],
        out_specs=[gspec(ti, ti), gspec(ti, 2 * p2), gspec(2 * p2, ti), gspec(2, p2)],
        out_shape=[
            jax.ShapeDtypeStruct((n_g, ti, ti), BF16),
            jax.ShapeDtypeStruct((n_g, ti, 2 * p2), BF16),
            jax.ShapeDtypeStruct((n_g, 2 * p2, ti), BF16),
            jax.ShapeDtypeStruct((n_g, 2, p2), F32),
        ],
        compiler_params=_cparams("parallel"),
        name="s5_params",
    )(lam_r, step_r, bt, ct)

    gb = S5_GROUPS_PER_STEP
    assert n_g % gb == 0
    gbspec = lambda *shape: pl.BlockSpec((gb,) + shape, lambda g: (g,) + (0,) * len(shape))
    y = pl.pallas_call(
        functools.partial(_s5_main_kernel, n_batch=bsz, n_chunks=n_chunks),
        grid=(n_g // gb,),
        in_specs=[gbspec(m, ti), gbspec(ti, ti), gbspec(ti, 2 * p2), gbspec(2 * p2, ti), gbspec(2, p2)],
        out_specs=gbspec(m, ti),
        out_shape=jax.ShapeDtypeStruct((n_g, m, ti), F32),
        scratch_shapes=[pltpu.VMEM((gb, m, 2 * p2), F32)] * 3,
        compiler_params=_cparams("parallel"),
        name="s5_main",
    )(u, kmat, wst, cexp, alpha)
    y2 = pl.pallas_call(
        _s5_token_major_kernel,
        grid=(bsz, n_tiles),
        in_specs=[pl.BlockSpec((n_g, None, mt, ti), lambda b, i: (0, b, i, 0))],
        out_specs=pl.BlockSpec((None, mt * t_len, dm), lambda b, i: (b, i, 0)),
        out_shape=jax.ShapeDtypeStruct((bsz, l, dm), F32),
        scratch_shapes=[pltpu.VMEM((t_len, dm, mt), F32), pltpu.VMEM((dm // LANES, mt * t_len, LANES), F32)],
        compiler_params=_cparams("parallel", "parallel"),
        name="s5_token_major",
    )(y.reshape(n_g, bsz, n_chunks, ti)).reshape(n, dm)

    out = pl.pallas_call(
        _s5_glu_kernel,
        grid=(n // tm,),
        in_specs=[
            pl.BlockSpec((tm, dm), lambda i: (i, 0)),
            pl.BlockSpec((tm, dm), lambda i: (i, 0)),
            pl.BlockSpec((1, dm), lambda i: (0, 0)),
            pl.BlockSpec((1, dm), lambda i: (0, 0)),
            pl.BlockSpec((dm, 2 * dm), lambda i: (0, 0)),
        ],
        out_specs=pl.BlockSpec((tm, dm), lambda i: (i, 0)),
        out_shape=jax.ShapeDtypeStruct((n, dm), F32),
        compiler_params=_cparams("parallel"),
        name="s5_glu",
    )(h2, y2, gain2, f32(d_skip).reshape(1, dm), w_glu.astype(BF16))
    return out.reshape(bsz, l, dm)


def na_rwkv_mix(h3, gain, w_in, w_out, rpb, mu, w0, w_up, a0, a_up, g_up, k_k, k_a, r_k, lnx_g, lnx_b):
    bsz, l, dm = h3.shape
    n = bsz * l
    h2 = h3.reshape(n, dm)
    n_qkv = 3 * (w_out.shape[0] // 2)
    qkv, rest = norm_inproj(h2, gain.astype(F32), w_in.astype(BF16), n_qkv)
    na = na_attention(qkv.reshape(bsz, l, n_qkv), rpb)
    rw = rwkv_mix(rest.reshape(bsz, l, -1), mu, w0, w_up, a0, a_up, g_up, k_k, k_a, r_k, lnx_g, lnx_b)
    out = outproj_residual(h2, na.reshape(n, -1), rw.reshape(n, -1), w_out.astype(BF16))
    return out.reshape(bsz, l, dm)


def kernel(x, meta_tokens, norm_mix, norm_ffn, norm_final, mix_w_in, mix_w_out, na_rpb, rwkv_mu,
           rwkv_w0, rwkv_w_up, rwkv_a0, rwkv_a_up, rwkv_g_up, rwkv_k_k, rwkv_k_a, rwkv_r_k,
           rwkv_lnx_g, rwkv_lnx_b, s5_b_re, s5_b_im, s5_lambda_re, s5_lambda_im, s5_log_step,
           s5_c_re, s5_c_im, s5_d, s5_w_glu, moe_w_group, moe_b_group, moe_w_expert, moe_b_expert,
           moe_w1, moe_w3, moe_w2):
    bsz, _, dm = x.shape
    depth = norm_mix.shape[0]
    meta = jnp.broadcast_to(meta_tokens.astype(x.dtype)[None], (bsz,) + meta_tokens.shape)
    h = jnp.concatenate([meta, x], axis=1)
    l = h.shape[1]
    for layer in range(depth):
        i = layer // 2
        if layer % 2 == 0:
            h = na_rwkv_mix(h, norm_mix[layer], mix_w_in[i], mix_w_out[i], na_rpb[i], rwkv_mu[i], rwkv_w0[i],
                            rwkv_w_up[i], rwkv_a0[i], rwkv_a_up[i], rwkv_g_up[i], rwkv_k_k[i], rwkv_k_a[i],
                            rwkv_r_k[i], rwkv_lnx_g[i], rwkv_lnx_b[i])
        else:
            h = s5_mix(h, norm_mix[layer], s5_b_re[i], s5_b_im[i], s5_lambda_re[i], s5_lambda_im[i],
                       s5_log_step[i], s5_c_re[i], s5_c_im[i], s5_d[i], s5_w_glu[i])
        h = hierarchical_moe_residual(h.reshape(bsz * l, dm), norm_ffn[layer].astype(F32), moe_w_group[layer],
                                      moe_b_group[layer], moe_w_expert[layer], moe_b_expert[layer],
                                      moe_w1, moe_w3, moe_w2, layer).reshape(bsz, l, dm)
    return final_norm(h, norm_final.astype(F32))
```

```python
import functools
import math

import jax
import jax.numpy as jnp
from jax import lax
from jax.experimental import pallas as pl
from jax.experimental.pallas import tpu as pltpu

F32 = jnp.float32
BF16 = jnp.bfloat16
I32 = jnp.int32

N_META = 16
GRID_W = 64
HEAD_DIM = 64
NA_WIN_ROWS = 8
NA_WIN_COLS = 16
S5_GROUP_CH = 16
S5_STATE = 64
MOE_GROUPS = 4
MOE_PER_GROUP = 8
MOE_EXPERTS = MOE_GROUPS * MOE_PER_GROUP
NORM_EPS = 1e-6
RWKV_GN_EPS = 64e-5
NEG_INF = -1e30

LANES = 128
SUBLANES_BF16 = 16
VMEM_LIMIT_BYTES = 56 * 1024 * 1024

MOE_TILE = 256
MOE_TOKEN_TILE = 320
MOE_STAGE_SLOTS = 3
ROUTER_LANES = 128


def _cparams(*sem):
    return pltpu.CompilerParams(dimension_semantics=sem, vmem_limit_bytes=VMEM_LIMIT_BYTES)


def _row_tile(n, target):
    best = None
    for t in range(SUBLANES_BF16, min(n, target) + 1, SUBLANES_BF16):
        if n % t == 0:
            best = t
    assert best is not None, (n, target)
    return best


def _rms(x, gain):
    ms = jnp.mean(x * x, axis=-1, keepdims=True)
    return (x * lax.rsqrt(ms + NORM_EPS)) * gain


def _split_bf16(x):
    hi = x.astype(BF16)
    lo = (x - hi.astype(F32)).astype(BF16)
    return hi, lo


def _dot(a, b):
    return jnp.dot(a, b, preferred_element_type=F32)


def _dot_nt(a, b):
    return lax.dot_general(a, b, (((1,), (1,)), ((), ())), preferred_element_type=F32)


SHIFT_HALO = SUBLANES_BF16


def _norm_inproj_kernel(h_ref, hp_ref, hn_ref, g_ref, w_ref, mu_ref, qkv_ref, rest_ref, xn_ref,
                        *, n_qkv, chunk, seq_len):
    tm = h_ref.shape[0]
    hl = SHIFT_HALO
    gain = g_ref[...]
    xn_ref[...] = jnp.concatenate([_rms(hp_ref[...], gain), _rms(h_ref[...], gain), _rms(hn_ref[...], gain)],
                                  axis=0).astype(BF16)
    t = (pl.program_id(0) * tm + lax.broadcasted_iota(I32, (tm, chunk), 0)) % seq_len
    has_prev = t > 0
    has_next = t < seq_len - 1
    n_all = w_ref.shape[1]
    for c in range(0, n_all, chunk):
        if c < n_qkv:
            qkv_ref[:, c:c + chunk] = _dot(xn_ref[hl:hl + tm, :], w_ref[:, c:c + chunk]).astype(BF16)
        else:
            y = _dot(xn_ref[...], w_ref[:, c:c + chunk])
            p = y[hl:hl + tm]
            nb = 0.5 * (jnp.where(has_prev, y[hl - 1:hl - 1 + tm], 0.0) + jnp.where(has_next, y[hl + 1:hl + 1 + tm], 0.0))
            rest_ref[:, c - n_qkv:c - n_qkv + chunk] = p + mu_ref[:, c - n_qkv:c - n_qkv + chunk] * (nb - p)


def norm_inproj(h2, gain, w_bf16, n_qkv, mu, seq_len):
    n, d = h2.shape
    n_all = w_bf16.shape[1]
    tm = _row_tile(n, 608)
    chunk = 256
    hl = SHIFT_HALO
    assert n_qkv % chunk == 0 and n_all % chunk == 0 and tm % hl == 0 and n % seq_len == 0
    per = tm // hl
    n_halo = n // hl
    return pl.pallas_call(
        functools.partial(_norm_inproj_kernel, n_qkv=n_qkv, chunk=chunk, seq_len=seq_len),
        grid=(n // tm,),
        in_specs=[
            pl.BlockSpec((tm, d), lambda i: (i, 0)),
            pl.BlockSpec((hl, d), lambda i: (jnp.maximum(i * per - 1, 0), 0)),
            pl.BlockSpec((hl, d), lambda i: (jnp.minimum((i + 1) * per, n_halo - 1), 0)),
            pl.BlockSpec((1, d), lambda i: (0, 0)),
            pl.BlockSpec((d, n_all), lambda i: (0, 0)),
            pl.BlockSpec((1, n_all - n_qkv), lambda i: (0, 0)),
        ],
        out_specs=[
            pl.BlockSpec((tm, n_qkv), lambda i: (i, 0)),
            pl.BlockSpec((tm, n_all - n_qkv), lambda i: (i, 0)),
        ],
        out_shape=[
            jax.ShapeDtypeStruct((n, n_qkv), BF16),
            jax.ShapeDtypeStruct((n, n_all - n_qkv), F32),
        ],
        scratch_shapes=[pltpu.VMEM((tm + 2 * hl, d), BF16)],
        compiler_params=_cparams("parallel"),
        name="norm_inproj",
    )(h2, h2, h2, gain.reshape(1, d), w_bf16, mu.astype(F32).reshape(1, n_all - n_qkv))


def _outproj_kernel(h_ref, na_ref, rw_ref, wa_ref, wb_ref, o_ref):
    acc = _dot(na_ref[...], wa_ref[...])
    acc = acc + _dot(rw_ref[...], wb_ref[...])
    o_ref[...] = h_ref[...] + acc


def outproj_residual(h2, na, rw, w_out_bf16):
    n, d = h2.shape
    ka, kb = na.shape[1], rw.shape[1]
    tm = _row_tile(n, 608)
    return pl.pallas_call(
        _outproj_kernel,
        grid=(n // tm,),
        in_specs=[
            pl.BlockSpec((tm, d), lambda i: (i, 0)),
            pl.BlockSpec((tm, ka), lambda i: (i, 0)),
            pl.BlockSpec((tm, kb), lambda i: (i, 0)),
            pl.BlockSpec((ka, d), lambda i: (0, 0)),
            pl.BlockSpec((kb, d), lambda i: (0, 0)),
        ],
        out_specs=pl.BlockSpec((tm, d), lambda i: (i, 0)),
        out_shape=jax.ShapeDtypeStruct((n, d), F32),
        compiler_params=_cparams("parallel"),
        name="outproj_residual",
    )(h2, na, rw, w_out_bf16[:ka], w_out_bf16[ka:])


def _store_token_tiles(ref, x):
    rows = x.shape[0]
    s_n = x.shape[1] // LANES
    for s in range(s_n):
        ref[pl.ds(s, rows, stride=s_n), :] = x[:, s * LANES:(s + 1) * LANES]


def _load_token_tile_cols(ref, s, rows, s_n):
    return ref[pl.ds(s, rows, stride=s_n), :]


def _router_kernel(h_ref, g_ref, whi_ref, wlo_ref, b_ref, xn_ref, route_ref):
    xn = _rms(h_ref[...], g_ref[...])
    x_hi, x_lo = _split_bf16(xn)
    _store_token_tiles(xn_ref, xn)
    logits = (_dot(x_hi, whi_ref[...]) + _dot(x_hi, wlo_ref[...]) + _dot(x_lo, whi_ref[...])
              + b_ref[...])
    tm = logits.shape[0]
    lane = lax.broadcasted_iota(I32, (tm, ROUTER_LANES), 1)
    big = jnp.int32(ROUTER_LANES)

    is_g = lane < MOE_GROUPS
    lg = jnp.where(is_g, logits, -jnp.inf)
    eg = jnp.where(is_g, jnp.exp(lg - jnp.max(lg, axis=-1, keepdims=True)), 0.0)
    pg = eg / jnp.sum(eg, axis=-1, keepdims=True)
    p_grp = jnp.max(pg, axis=-1, keepdims=True)
    grp = jnp.min(jnp.where(is_g & (pg == p_grp), lane, big), axis=-1, keepdims=True)

    lo_lane = MOE_GROUPS + MOE_PER_GROUP * grp
    is_e = (lane >= lo_lane) & (lane < lo_lane + MOE_PER_GROUP)
    le = jnp.where(is_e, logits, -jnp.inf)
    ee = jnp.where(is_e, jnp.exp(le - jnp.max(le, axis=-1, keepdims=True)), 0.0)
    pe = jnp.where(is_e, ee / jnp.sum(ee, axis=-1, keepdims=True), -1.0)
    p1 = jnp.max(pe, axis=-1, keepdims=True)
    i1 = jnp.min(jnp.where(pe == p1, lane, big), axis=-1, keepdims=True)
    pe2 = jnp.where(lane == i1, -1.0, pe)
    p2 = jnp.max(pe2, axis=-1, keepdims=True)
    i2 = jnp.min(jnp.where(pe2 == p2, lane, big), axis=-1, keepdims=True)
    denom = p1 + p2
    g1 = p_grp * p1 / denom
    g2 = p_grp * p2 / denom
    e1 = (i1 - MOE_GROUPS).astype(F32)
    e2 = (i2 - MOE_GROUPS).astype(F32)
    route_ref[...] = jnp.where(lane == 0, e1, jnp.where(lane == 1, e2, jnp.where(lane == 2, g1, g2)))


def moe_router(h2, gain, w_group, b_group, w_expert, b_expert):
    n, d = h2.shape
    n_r = MOE_GROUPS + MOE_EXPERTS
    w_r = jnp.concatenate([w_group, jnp.transpose(w_expert, (1, 0, 2)).reshape(d, MOE_EXPERTS)], axis=1)
    w_r = jnp.pad(w_r.astype(F32), ((0, 0), (0, ROUTER_LANES - n_r)))
    w_hi, w_lo = _split_bf16(w_r)
    b_r = jnp.pad(jnp.concatenate([b_group, b_expert.reshape(-1)]).astype(F32), (0, ROUTER_LANES - n_r))
    tm = _row_tile(n, 608)
    return pl.pallas_call(
        _router_kernel,
        grid=(n // tm,),
        in_specs=[
            pl.BlockSpec((tm, d), lambda i: (i, 0)),
            pl.BlockSpec((1, d), lambda i: (0, 0)),
            pl.BlockSpec((d, ROUTER_LANES), lambda i: (0, 0)),
            pl.BlockSpec((d, ROUTER_LANES), lambda i: (0, 0)),
            pl.BlockSpec((1, ROUTER_LANES), lambda i: (0, 0)),
        ],
        out_specs=[
            pl.BlockSpec((tm * (d // LANES), LANES), lambda i: (i, 0)),
            pl.BlockSpec((tm, ROUTER_LANES), lambda i: (i, 0)),
        ],
        out_shape=[
            jax.ShapeDtypeStruct((n * (d // LANES), LANES), F32),
            jax.ShapeDtypeStruct((n, ROUTER_LANES), F32),
        ],
        compiler_params=_cparams("parallel"),
        name="moe_router",
    )(h2, gain.reshape(1, d), w_hi, w_lo, b_r.reshape(1, ROUTER_LANES))


def _moe_dispatch_kernel(tail_ref, n_used_ref, dst_ref, xn_hbm, xbuf_hbm, zero_ref, stage_ref, sem, lsem, zsem,
                         *, tm, s_n, n_blocks):
    i = pl.program_id(0)
    n_steps = pl.num_programs(0)
    tile_rows = zero_ref.shape[0]

    def zero_block(b, carry):
        pltpu.make_async_copy(zero_ref, xbuf_hbm.at[pl.ds(pl.multiple_of(b * tile_rows, tile_rows), tile_rows), :],
                              zsem).start()
        return carry

    def wait_zero_block(b, carry):
        pltpu.make_async_copy(zero_ref, xbuf_hbm.at[pl.ds(0, tile_rows), :], zsem).wait()
        return carry

    n_slots = stage_ref.shape[0]
    slot = i % n_slots
    nxt = (i + 1) % n_slots

    def load(step, s):
        return pltpu.make_async_copy(xn_hbm.at[pl.ds(pl.multiple_of(step * tm * s_n, s_n), tm * s_n), :],
                                     stage_ref.at[s], lsem.at[s])

    def wait_rows_out(s):
        for _ in range(2):
            pltpu.make_async_copy(stage_ref.at[s], xbuf_hbm.at[pl.ds(0, tm * s_n), :], sem.at[s]).wait()

    @pl.when(i == 0)
    def _():
        load(0, 0).start()
        zero_ref[...] = jnp.zeros_like(zero_ref)
        for e in range(tail_ref.shape[0]):
            @pl.when(tail_ref[e] >= 0)
            def _():
                pltpu.make_async_copy(zero_ref, xbuf_hbm.at[pl.ds(pl.multiple_of(tail_ref[e], s_n), tile_rows), :],
                                      zsem).start()
        lax.fori_loop(n_used_ref[0], n_blocks, zero_block, 0)
        for e in range(tail_ref.shape[0]):
            @pl.when(tail_ref[e] >= 0)
            def _():
                wait_zero_block(0, 0)
        lax.fori_loop(n_used_ref[0], n_blocks, wait_zero_block, 0)

    @pl.when(i >= n_slots - 1)
    def _():
        wait_rows_out(nxt)

    @pl.when(i + 1 < n_steps)
    def _():
        load(i + 1, nxt).start()

    load(i, slot).wait()
    for r in range(tm):
        src = stage_ref.at[slot, pl.ds(r * s_n, s_n), :]
        for k in range(2):
            dst = pl.multiple_of(dst_ref[k, r], s_n)
            pltpu.make_async_copy(src, xbuf_hbm.at[pl.ds(dst, s_n), :], sem.at[slot]).start(priority=k)

    @pl.when(i == n_steps - 1)
    def _():
        for back in range(n_slots - 1):
            @pl.when(i >= back)
            def _():
                wait_rows_out((i - back) % n_slots)


def moe_dispatch(xn_tiles, dst_tiles, tail_start, n_used, n_blocks, tm, s_n):
    n_steps = dst_tiles.shape[0]
    n_rows = n_blocks * MOE_TILE
    assert n_steps * tm * s_n == xn_tiles.shape[0]
    grid_spec = pltpu.PrefetchScalarGridSpec(
        num_scalar_prefetch=2,
        grid=(n_steps,),
        in_specs=[
            pl.BlockSpec((None, 2, tm), lambda i, tail, nu: (i, 0, 0), memory_space=pltpu.SMEM),
            pl.BlockSpec(memory_space=pl.ANY),
        ],
        out_specs=pl.BlockSpec(memory_space=pl.ANY),
        scratch_shapes=[
            pltpu.VMEM((MOE_TILE * s_n, LANES), F32),
            pltpu.VMEM((MOE_STAGE_SLOTS, tm * s_n, LANES), F32),
            pltpu.SemaphoreType.DMA((MOE_STAGE_SLOTS,)),
            pltpu.SemaphoreType.DMA((MOE_STAGE_SLOTS,)),
            pltpu.SemaphoreType.DMA(()),
        ],
    )
    return pl.pallas_call(
        functools.partial(_moe_dispatch_kernel, tm=tm, s_n=s_n, n_blocks=n_blocks),
        grid_spec=grid_spec,
        out_shape=jax.ShapeDtypeStruct((n_rows * s_n, LANES), F32),
        compiler_params=_cparams("arbitrary"),
        name="moe_dispatch",
    )(tail_start, n_used, dst_tiles, xn_tiles)


def _expert_kernel(blk_e_ref, n_used_ref, x_ref, w1_ref, w3_ref, w2_ref, y_ref, xb_ref, w1b_ref, w3b_ref, w2b_ref):
    i = pl.program_id(0)
    used = i < n_used_ref[0]
    tile, d = xb_ref.shape
    s_n = d // LANES
    prev_e = blk_e_ref[jnp.maximum(i - 1, 0)]
    fresh = (i == 0) | (blk_e_ref[i] != prev_e)

    @pl.when(used & fresh)
    def _():
        w1b_ref[...] = w1_ref[...].astype(BF16)
        w3b_ref[...] = w3_ref[...].astype(BF16)
        w2b_ref[...] = w2_ref[...].astype(BF16)

    @pl.when(used)
    def _():
        for s in range(s_n):
            xb_ref[:, s * LANES:(s + 1) * LANES] = _load_token_tile_cols(x_ref, s, tile, s_n).astype(BF16)
        x = xb_ref[...]
        a = _dot(x, w1b_ref[...])
        b = _dot(x, w3b_ref[...])
        hmid = (a * jax.nn.sigmoid(a) * b).astype(BF16)
        _store_token_tiles(y_ref, _dot(hmid, w2b_ref[...]))

    @pl.when(jnp.logical_not(used))
    def _():
        y_ref[...] = jnp.zeros_like(y_ref)


def moe_experts(xbuf, blk_e, n_used, w1, w3, w2, layer, n_blocks):
    d, f = w1.shape[2], w1.shape[3]
    s_n = d // LANES
    tile = MOE_TILE

    def w_map(i, blk_e_ref, n_used_ref):
        return (layer, blk_e_ref[i], 0, 0)

    def x_map(i, blk_e_ref, n_used_ref):
        return (jnp.minimum(i, jnp.maximum(n_used_ref[0] - 1, 0)), 0)

    grid_spec = pltpu.PrefetchScalarGridSpec(
        num_scalar_prefetch=2,
        grid=(n_blocks,),
        in_specs=[
            pl.BlockSpec((tile * s_n, LANES), x_map),
            pl.BlockSpec((None, None, d, f), w_map),
            pl.BlockSpec((None, None, d, f), w_map),
            pl.BlockSpec((None, None, f, d), w_map),
        ],
        out_specs=pl.BlockSpec((tile * s_n, LANES), lambda i, be, nu: (i, 0)),
        scratch_shapes=[
            pltpu.VMEM((tile, d), BF16),
            pltpu.VMEM((d, f), BF16),
            pltpu.VMEM((d, f), BF16),
            pltpu.VMEM((f, d), BF16),
        ],
    )
    return pl.pallas_call(
        _expert_kernel,
        grid_spec=grid_spec,
        out_shape=jax.ShapeDtypeStruct((n_blocks * tile * s_n, LANES), F32),
        compiler_params=_cparams("arbitrary"),
        name="moe_experts",
    )(blk_e, n_used, xbuf, w1, w3, w2)


def _moe_combine_kernel(src_ref, src_next_ref, h_ref, route_ref, y_hbm, o_ref, yg_ref, sem, *, n_steps):
    i = pl.program_id(0)
    slot = i & 1
    tm, d = h_ref.shape
    s_n = d // LANES

    def start_gather(ids_ref, dst_slot):
        for k in range(2):
            for r in range(tm):
                src = pl.multiple_of(ids_ref[k, r], s_n)
                pltpu.make_async_copy(y_hbm.at[pl.ds(src, s_n), :],
                                      yg_ref.at[dst_slot, k, pl.ds(r * s_n, s_n), :],
                                      sem.at[dst_slot]).start(priority=r % 2)

    @pl.when(i == 0)
    def _():
        start_gather(src_ref, 0)

    @pl.when(i + 1 < n_steps)
    def _():
        start_gather(src_next_ref, 1 - slot)

    for k in range(2):
        pltpu.make_async_copy(y_hbm.at[pl.ds(0, tm * s_n), :], yg_ref.at[slot, k], sem.at[slot]).wait()
    route = route_ref[...]
    g1 = route[:, 2:3]
    g2 = route[:, 3:4]
    for s in range(s_n):
        cols = slice(s * LANES, (s + 1) * LANES)
        o_ref[:, cols] = (h_ref[:, cols] + g1 * _load_token_tile_cols(yg_ref.at[slot, 0], s, tm, s_n)
                          + g2 * _load_token_tile_cols(yg_ref.at[slot, 1], s, tm, s_n))


def moe_combine(h2, route, y_tiles, src_tiles, tm):
    n, d = h2.shape
    s_n = d // LANES
    n_steps = src_tiles.shape[0]
    ids = lambda index: pl.BlockSpec((None, 2, tm), lambda i: (index(i), 0, 0), memory_space=pltpu.SMEM)
    return pl.pallas_call(
        functools.partial(_moe_combine_kernel, n_steps=n_steps),
        grid=(n_steps,),
        in_specs=[
            ids(lambda i: i),
            ids(lambda i: jnp.minimum(i + 1, n_steps - 1)),
            pl.BlockSpec((tm, d), lambda i: (i, 0)),
            pl.BlockSpec((tm, ROUTER_LANES), lambda i: (i, 0)),
            pl.BlockSpec(memory_space=pl.ANY),
        ],
        out_specs=pl.BlockSpec((tm, d), lambda i: (i, 0)),
        out_shape=jax.ShapeDtypeStruct((n, d), F32),
        scratch_shapes=[pltpu.VMEM((2, 2, tm * s_n, LANES), F32), pltpu.SemaphoreType.DMA((2,))],
        compiler_params=_cparams("arbitrary"),
        name="moe_combine",
    )(src_tiles, src_tiles, h2, route, y_tiles)


def hierarchical_moe_residual(h2, gain, w_group, b_group, w_expert, b_expert, w1, w3, w2, layer):
    n, d = h2.shape
    xn, route = moe_router(h2, gain, w_group, b_group, w_expert, b_expert)
    e_km = jnp.concatenate([route[:, 0], route[:, 1]]).astype(I32)
    n_assign = 2 * n
    onehot = (e_km[:, None] == jnp.arange(MOE_EXPERTS, dtype=I32)[None, :]).astype(I32)
    csum = jnp.cumsum(onehot, axis=0)
    counts = csum[-1]
    padded = (counts + MOE_TILE - 1) // MOE_TILE * MOE_TILE
    pad_end = jnp.cumsum(padded)
    pad_start = pad_end - padded
    dest = jnp.sum((csum - onehot + pad_start[None, :]) * onehot, axis=1)
    n_blocks = -(-n_assign // MOE_TILE) + MOE_EXPERTS
    blk_start = jnp.arange(n_blocks, dtype=I32) * MOE_TILE
    blk_e = jnp.minimum(jnp.sum((pad_end[None, :] <= blk_start[:, None]).astype(I32), axis=1),
                        MOE_EXPERTS - 1).astype(I32)
    n_used = (pad_end[-1] // MOE_TILE).astype(I32).reshape(1)
    s_n = d // LANES
    tm = _row_tile(n, MOE_TOKEN_TILE)
    dest_tiles = jnp.transpose((dest * s_n).astype(I32).reshape(2, n // tm, tm), (1, 0, 2))
    tail_start = jnp.where(counts > 0, (pad_end - MOE_TILE) * s_n, -1).astype(I32)
    xbuf = moe_dispatch(xn, dest_tiles, tail_start, n_used, n_blocks, tm, s_n)
    y = moe_experts(xbuf, blk_e, n_used, w1, w3, w2, layer, n_blocks)
    return moe_combine(h2, route, y, dest_tiles, tm)


def _final_norm_kernel(h_ref, g_ref, o_ref):
    o_ref[...] = _rms(h_ref[...], g_ref[...])


def final_norm(h3, gain):
    b, l, d = h3.shape
    t = l - N_META
    tm = _row_tile(t, 512)
    return pl.pallas_call(
        _final_norm_kernel,
        grid=(b, t // tm),
        in_specs=[
            pl.BlockSpec((None, pl.Element(tm), pl.Element(d)),
                         lambda bi, i: (bi, pl.multiple_of(N_META + i * tm, SUBLANES_BF16), 0)),
            pl.BlockSpec((1, d), lambda bi, i: (0, 0)),
        ],
        out_specs=pl.BlockSpec((None, tm, d), lambda bi, i: (bi, i, 0)),
        out_shape=jax.ShapeDtypeStruct((b, t, d), F32),
        compiler_params=_cparams("parallel", "parallel"),
        name="final_norm",
    )(h3, gain.reshape(1, d))


NA_QROWS = 8
NA_KROWS = 3 * NA_QROWS
NA_ROWS_PER_ITER = 4


def _na_kernel(q_ref, kw_ref, vw_ref, qm_ref, km_ref, vm_ref, bias_ref, o_ref, om_ref, *, rows, scale):
    blk = pl.program_id(1)
    tq = GRID_W
    n_pairs = q_ref.shape[1] // LANES
    base = jnp.clip(NA_QROWS * blk - NA_QROWS, 0, rows - NA_KROWS)
    lane = lax.broadcasted_iota(I32, (tq, LANES), 1)
    halves = [lane < HEAD_DIM, lane >= HEAD_DIM]

    pad = jnp.zeros((LANES - N_META, LANES), km_ref.dtype)
    k_meta = [jnp.concatenate([km_ref[:, p * LANES:(p + 1) * LANES], pad], axis=0) for p in range(n_pairs)]
    v_meta = [jnp.concatenate([vm_ref[:, p * LANES:(p + 1) * LANES], pad], axis=0) for p in range(n_pairs)]
    lane2 = lax.broadcasted_iota(I32, (2 * tq, LANES), 1)
    meta_bias2 = jnp.where(lane2 < N_META, 0.0, NEG_INF)

    n_win = NA_WIN_ROWS * GRID_W

    def row_body(jb, carry):
        colsl = [slice(p * LANES, (p + 1) * LANES) for p in range(n_pairs)]
        units = [(jj, p) for jj in range(NA_ROWS_PER_ITER) for p in range(n_pairs)]
        s_idx, koff, qoff = [], [], []
        for jj in range(NA_ROWS_PER_ITER):
            j = jb * NA_ROWS_PER_ITER + jj
            r = NA_QROWS * blk + j
            start = jnp.clip(r - NA_WIN_ROWS // 2, 0, rows - NA_WIN_ROWS)
            s_idx.append(start - r + (NA_WIN_ROWS - 1))
            koff.append(pl.multiple_of((start - base) * GRID_W, GRID_W))
            qoff.append(pl.multiple_of(j * tq, tq))
        q_pair = {(jj, p): q_ref[pl.ds(qoff[jj], tq), c]
                  for jj in range(NA_ROWS_PER_ITER) for p, c in enumerate(colsl)}
        k_ext = {(jj, p): jnp.concatenate([kw_ref[pl.ds(koff[jj], n_win), c], k_meta[p]], axis=0)
                 for jj in range(NA_ROWS_PER_ITER) for p, c in enumerate(colsl)}
        v_ext = {(jj, p): jnp.concatenate([vw_ref[pl.ds(koff[jj], n_win), c], v_meta[p]], axis=0)
                 for jj in range(NA_ROWS_PER_ITER) for p, c in enumerate(colsl)}
        qh = [_stack_heads(q_pair[jj, p], halves[0]) for jj, p in units]
        s = [_dot_nt(qh[u], k_ext[jj, p]) * scale + jnp.concatenate([bias_ref[p, s_idx[jj]], meta_bias2], axis=1)
             for u, (jj, p) in enumerate(units)]
        m = [jnp.max(x, axis=-1, keepdims=True) for x in s]
        e = [jnp.exp(x - mx) for x, mx in zip(s, m)]
        den = [jnp.sum(x, axis=-1, keepdims=True) for x in e]
        o = [_dot(e[u].astype(BF16), v_ext[jj, p]) / den[u] for u, (jj, p) in enumerate(units)]
        for u, (jj, p) in enumerate(units):
            o_ref[pl.ds(qoff[jj], tq), colsl[p]] = jnp.where(halves[0], o[u][0:tq], o[u][tq:2 * tq]).astype(o_ref.dtype)
        return carry

    lax.fori_loop(0, NA_QROWS // NA_ROWS_PER_ITER, row_body, 0)

    @pl.when(blk == 0)
    def _():
        lane_m = lax.broadcasted_iota(I32, (N_META, LANES), 1)
        for p in range(n_pairs):
            cols = slice(p * LANES, (p + 1) * LANES)
            q_pair = qm_ref[:, cols]
            kmp = km_ref[:, cols]
            vmp = vm_ref[:, cols]
            outs = []
            for hh in range(2):
                sel = (lane_m < HEAD_DIM) if hh == 0 else (lane_m >= HEAD_DIM)
                qp = jnp.where(sel, q_pair, jnp.zeros_like(q_pair))
                s_m = _dot_nt(qp, kmp) * scale
                p_m = jnp.exp(s_m - jnp.max(s_m, axis=-1, keepdims=True))
                den = jnp.sum(p_m, axis=-1, keepdims=True)
                outs.append(_dot(p_m.astype(BF16), vmp) / den)
            om_ref[:, cols] = jnp.where(lane_m < HEAD_DIM, outs[0], outs[1]).astype(om_ref.dtype)


def _na_bias_table(rpb):
    h = rpb.shape[0]
    c_ids = jnp.arange(GRID_W)
    c_start = jnp.clip(c_ids - NA_WIN_COLS // 2, 0, GRID_W - NA_WIN_COLS)
    in_band = (c_ids[None, :] >= c_start[:, None]) & (c_ids[None, :] < c_start[:, None] + NA_WIN_COLS)
    dc = jnp.clip(c_ids[None, :] - c_ids[:, None] + NA_WIN_COLS - 1, 0, 2 * NA_WIN_COLS - 2)
    tab = jnp.where(in_band[None, None], rpb.astype(F32)[:, :, dc], NEG_INF)
    win = jnp.stack([tab[:, s:s + NA_WIN_ROWS] for s in range(NA_WIN_ROWS)], axis=1)
    per_head = jnp.transpose(win, (0, 1, 3, 2, 4)).reshape(h // 2, 2, NA_WIN_ROWS, GRID_W, NA_WIN_ROWS * GRID_W)
    return jnp.transpose(per_head, (0, 2, 1, 3, 4)).reshape(h // 2, NA_WIN_ROWS, 2 * GRID_W, NA_WIN_ROWS * GRID_W)


def na_attention(qkv, rpb):
    b, l, w3 = qkv.shape
    w = w3 // 3
    t = l - N_META
    rows = t // GRID_W
    assert rows * GRID_W == t and rows % NA_QROWS == 0 and rows >= NA_KROWS
    tq = NA_QROWS * GRID_W
    tk = NA_KROWS * GRID_W
    bias = _na_bias_table(rpb)
    al = SUBLANES_BF16

    def q_map(bi, i):
        return (bi, pl.multiple_of(N_META + i * tq, al), 0)

    def kv_map(col):
        def f(bi, i):
            base = jnp.clip(NA_QROWS * i - NA_QROWS, 0, rows - NA_KROWS)
            return (bi, pl.multiple_of(N_META + base * GRID_W, al), col)
        return f

    def meta_map(col):
        return lambda bi, i: (bi, 0, col)

    el = pl.Element
    grid_out, meta_out = pl.pallas_call(
        functools.partial(_na_kernel, rows=rows, scale=HEAD_DIM ** -0.5),
        grid=(b, rows // NA_QROWS),
        in_specs=[
            pl.BlockSpec((None, el(tq), el(w)), q_map),
            pl.BlockSpec((None, el(tk), el(w)), kv_map(w)),
            pl.BlockSpec((None, el(tk), el(w)), kv_map(2 * w)),
            pl.BlockSpec((None, el(N_META), el(w)), meta_map(0)),
            pl.BlockSpec((None, el(N_META), el(w)), meta_map(w)),
            pl.BlockSpec((None, el(N_META), el(w)), meta_map(2 * w)),
            pl.BlockSpec(bias.shape, lambda bi, i: (0, 0, 0, 0)),
        ],
        out_specs=[
            pl.BlockSpec((None, tq, w), lambda bi, i: (bi, i, 0)),
            pl.BlockSpec((None, N_META, w), lambda bi, i: (bi, 0, 0)),
        ],
        out_shape=[
            jax.ShapeDtypeStruct((b, t, w), BF16),
            jax.ShapeDtypeStruct((b, N_META, w), BF16),
        ],
        compiler_params=_cparams("parallel", "arbitrary"),
        name="na_attention",
    )(qkv, qkv, qkv, qkv, qkv, qkv, bias)
    return jnp.concatenate([meta_out, grid_out], axis=1)


RWKV_CHUNK = 64
RWKV_TILE_CHUNKS = 2


def _split3_bf16(x):
    p1 = x.astype(BF16)
    r1 = x - p1.astype(F32)
    p2 = r1.astype(BF16)
    p3 = (r1 - p2.astype(F32)).astype(BF16)
    return p1, p2, p3


def _mm1(a, b):
    return _dot(a.astype(BF16), b.astype(BF16))


def _mm3(a, b):
    ah, al = _split_bf16(a)
    bh, bl = _split_bf16(b)
    return _dot(ah, bh) + _dot(ah, bl) + _dot(al, bh)


def _mm1_nt(a, b):
    return _dot_nt(a.astype(BF16), b.astype(BF16))


def _mm3_nt(a, b):
    ah, al = _split_bf16(a)
    bh, bl = _split_bf16(b)
    return _dot_nt(ah, bh) + _dot_nt(ah, bl) + _dot_nt(al, bh)


def _exact_left(mat_bf16, x):
    p1, p2, p3 = _split3_bf16(x)
    return _dot(mat_bf16, p1) + _dot(mat_bf16, p2) + _dot(mat_bf16, p3)


def _exact_right(x, mat_bf16):
    p1, p2, p3 = _split3_bf16(x)
    return _dot(p1, mat_bf16) + _dot(p2, mat_bf16) + _dot(p3, mat_bf16)


def _head_block_ones(width):
    ri = lax.broadcasted_iota(I32, (width, width), 0) // HEAD_DIM
    ci = lax.broadcasted_iota(I32, (width, width), 1) // HEAD_DIM
    return (ri == ci).astype(BF16)


def _head_sums(x, exact):
    ones_pair = _head_block_ones(LANES)
    tiles = []
    for p in range(x.shape[1] // LANES):
        xt = x[:, p * LANES:(p + 1) * LANES]
        tiles.append(_exact_right(xt, ones_pair) if exact else _dot(xt.astype(BF16), ones_pair))
    return jnp.concatenate(tiles, axis=1)


def _stack_heads(x, m0):
    z = jnp.zeros_like(x)
    return jnp.concatenate([jnp.where(m0, x, z), jnp.where(m0, z, x)], axis=0)


_MM_L4 = _mm1_nt
_MM_KT = _mm1_nt
_MM_SQ = _mm1
_MM_AP = _mm1
_MM_V = _mm1
_MM_Y = _mm1
_MM_UPD = _mm1


def _rwkv_chunk_maps(streams, c, n_sub):
    assert c == 64
    c2 = 2 * c
    lane = lax.broadcasted_iota(I32, (c, LANES), 1)
    m0 = lane < HEAD_DIM
    r_i = lax.broadcasted_iota(I32, (c2, c2), 0)
    c_i = lax.broadcasted_iota(I32, (c2, c2), 1)
    eye = (r_i == c_i).astype(F32)
    rel = r_i % c - c_i % c
    masks = {sg: (rel * sg > 0, rel * sg >= 0) for sg in {s["sign"] for s in streams}}
    items = [(j, q) for j in range(len(streams)) for q in range(n_sub)]

    def part(j, q, name):
        return _stack_heads(streams[j][name][q * c:(q + 1) * c], m0)

    lhs = {it: jnp.concatenate([part(*it, "kkp"), part(*it, "rp")], axis=0) for it in items}
    rhs = {it: jnp.concatenate([part(*it, "ki"), part(*it, "bi")], axis=0) for it in items}
    vs = {it: part(*it, "v") for it in items}
    kipcs = {it: part(*it, "kipc") for it in items}
    bipcs = {it: part(*it, "bipc") for it in items}
    l4 = {it: _MM_L4(lhs[it], rhs[it]) for it in items}
    m_kk, n1, m_rk, m_rb = {}, {}, {}, {}
    for it in items:
        strict, incl = masks[streams[it[0]]["sign"]]
        m = l4[it]
        m_kk[it] = jnp.where(strict, m[0:c2, 0:c2], 0.0)
        n1[it] = jnp.where(strict, m[0:c2, c2:2 * c2], 0.0)
        m_rk[it] = jnp.where(incl, m[c2:2 * c2, 0:c2], 0.0)
        m_rb[it] = jnp.where(incl, m[c2:2 * c2, c2:2 * c2], 0.0)
    n2 = {it: _MM_SQ(n1[it], n1[it]) for it in items}
    n4 = {it: _MM_SQ(n2[it], n2[it]) for it in items}
    n8 = {it: _MM_SQ(n4[it], n4[it]) for it in items}
    n16 = {it: _MM_SQ(n8[it], n8[it]) for it in items}
    n32 = {it: _MM_SQ(n16[it], n16[it]) for it in items}
    p1 = {it: (eye - n1[it]) + _MM_AP(eye - n1[it], n2[it]) for it in items}
    p2 = {it: eye + n4[it] + n8[it] + _MM_AP(n4[it], n8[it]) for it in items}
    p3 = {it: eye + n16[it] + n32[it] + _MM_AP(n16[it], n32[it]) for it in items}
    p23 = {it: _MM_AP(p2[it], p3[it]) for it in items}
    winv = {it: _MM_AP(p1[it], p23[it]) for it in items}
    mv = {it: _MM_V(m_kk[it], vs[it]) for it in items}
    mrv = {it: _MM_Y(m_rk[it], vs[it]) for it in items}
    wl = {it: _MM_AP(winv[it], lhs[it][0:c2]) for it in items}
    wmv = {it: _MM_AP(winv[it], mv[it]) for it in items}
    yl = {it: lhs[it][c2:2 * c2] - _MM_Y(m_rb[it], wl[it]) for it in items}
    y0 = {it: mrv[it] - _MM_Y(m_rb[it], wmv[it]) for it in items}
    g2 = {it: _MM_UPD(jnp.transpose(wl[it]), bipcs[it]) for it in items}
    hh = {it: _MM_UPD(jnp.transpose(jnp.concatenate([vs[it], -wmv[it]], axis=0)),
                      jnp.concatenate([kipcs[it], bipcs[it]], axis=0)) for it in items}
    return {it: (yl[it], y0[it], g2[it], hh[it]) for it in items}


def _rwkv_apply_maps(streams, maps, c, n_sub):
    c2 = 2 * c
    st = [s["st"] for s in streams]
    ys = {}
    for k in range(n_sub):
        cur = [(j, k if s["sign"] > 0 else n_sub - 1 - k) for j, s in enumerate(streams)]
        sg = [_MM_UPD(st[j], maps[j][q][2]) for j, q in cur]
        yk = [_MM_KT(maps[j][q][0], st[j]) + maps[j][q][1] for j, q in cur]
        for it, yi in zip(cur, yk):
            ys[it] = yi[0:c] + yi[c:c2]
        st = [st[j] * streams[j]["pc"][q] - sgi + maps[j][q][3] for (j, q), sgi in zip(cur, sg)]
    return [(jnp.concatenate([ys[j, q] for q in range(n_sub)], axis=0), st[j]) for j in range(len(streams))]


def _softplus(z):
    return jnp.maximum(z, 0.0) + jnp.log(1.0 + jnp.exp(-jnp.abs(z)))


def _rwkv_tile_prep(x_ref, tile, seq_len, width, sign, w0, a0, w_wa, k_k, k_a, r_k):
    cs = RWKV_CHUNK
    c = x_ref.shape[0]
    valid = jnp.minimum(c, seq_len - tile * c)
    row = lax.broadcasted_iota(I32, (c, LANES), 0)
    rowv = row < valid
    lane = lax.broadcasted_iota(I32, (c, LANES), 1)
    ones_pair = _head_block_ones(LANES)

    def shifted(lo):
        return jnp.where(rowv, x_ref[:, lo:lo + LANES], 0.0)

    wa = shifted(3 * width)
    g_lo = shifted(3 * width + LANES)
    xwa = jnp.where(lane < LANES // 2, jnp.tanh(wa), wa)
    la = _dot(xwa.astype(BF16), w_wa)

    t_i = lax.broadcasted_iota(I32, (c, c), 0)
    s_i = lax.broadcasted_iota(I32, (c, c), 1)
    tri = ((t_i // cs == s_i // cs) & ((t_i - s_i) * sign >= 0)).astype(BF16)

    pairs = []
    for p in range(width // LANES):
        lo = p * LANES
        cols = slice(lo, lo + LANES)
        r = shifted(lo)
        k = shifted(width + lo)
        v = shifted(2 * width + lo)
        w_log = -_softplus(-(w0[:, cols] + la[:, cols])) - 0.5
        logw = jnp.where(rowv, -jnp.exp(w_log), 0.0)
        a = jax.nn.sigmoid(a0[:, cols] + la[:, width + lo:width + lo + LANES])
        kk0 = k * k_k[:, cols]
        ss = _dot((kk0 * kk0).astype(BF16), ones_pair)
        kk = kk0 / jnp.maximum(jnp.sqrt(ss), 1e-12)
        kdir = k * (1.0 + (a - 1.0) * k_a[:, cols])
        b = kk * a
        cl = _exact_left(tri, logw)
        lasts = [cl[q * cs + cs - 1:q * cs + cs, :] if sign > 0 else cl[q * cs:q * cs + 1, :]
                 for q in range(c // cs)]
        last = jnp.concatenate([jnp.broadcast_to(lq, (cs, LANES)) for lq in lasts], axis=0)
        e_n = jnp.exp(-cl)
        pcr = jnp.exp(last - cl)
        pairs.append(dict(kkp=kk * jnp.exp(cl - logw), rp=r * jnp.exp(cl), ki=kdir * e_n, bi=b * e_n,
                          kipc=kdir * pcr, bipc=b * pcr, v=v, pc=[jnp.exp(lq) for lq in lasts],
                          bonus=_dot((r * kdir * r_k[:, cols]).astype(BF16), ones_pair) * v))
    return pairs, g_lo


def _rwkv_scan_kernel(xf_ref, xb_ref, w0_ref, a0_ref, wwa_ref,
                      gup_ref, kk_ref, ka_ref, rk_ref, yf_ref, yb_ref, bonf_ref, bonb_ref, g_ref, st_ref,
                      *, seq_len, width):
    i = pl.program_id(1)
    n_chunks = pl.num_programs(1)
    n_pairs = width // LANES

    @pl.when(i == 0)
    def _():
        st_ref[...] = jnp.zeros_like(st_ref)

    tail = (kk_ref[...], ka_ref[...], rk_ref[...])
    n_sub = xf_ref.shape[0] // RWKV_CHUNK
    fwd, g_lo = _rwkv_tile_prep(xf_ref, i, seq_len, width, 1, w0_ref[0], a0_ref[0], wwa_ref[0], *tail)
    bwd, _ = _rwkv_tile_prep(xb_ref, n_chunks - 1 - i, seq_len, width, -1, w0_ref[1], a0_ref[1], wwa_ref[1], *tail)
    g_ref[...] = _mm1(jax.nn.sigmoid(g_lo), gup_ref[...]).astype(g_ref.dtype)
    streams = []
    for di, (pairs, sign, bon_ref) in enumerate(((fwd, 1, bonf_ref), (bwd, -1, bonb_ref))):
        for p, s in enumerate(pairs):
            bon_ref[:, p * LANES:(p + 1) * LANES] = s.pop("bonus")
            s["st"] = st_ref[di * n_pairs + p]
            s["sign"] = sign
            streams.append(s)
    m = _rwkv_chunk_maps(streams, RWKV_CHUNK, n_sub)
    maps = [{q: m[j, q] for q in range(n_sub)} for j in range(len(streams))]
    res = _rwkv_apply_maps(streams, maps, RWKV_CHUNK, n_sub)
    for j, (y, st_new) in enumerate(res):
        di, p = divmod(j, n_pairs)
        cols = slice(p * LANES, (p + 1) * LANES)
        (yf_ref if di == 0 else yb_ref)[:, cols] = y
        st_ref[j] = st_new


def _rwkv_finish_kernel(yf_ref, yb_ref, bonf_ref, bonb_ref, g_ref, lg_ref, lb_ref, o_ref):
    y = yf_ref[...] + yb_ref[...]
    mean = _head_sums(y, exact=True) * (1.0 / HEAD_DIM)
    yc = y - mean
    var = _head_sums(yc * yc, exact=True) * (1.0 / HEAD_DIM)
    yn = yc * lax.rsqrt(var + RWKV_GN_EPS) * lg_ref[...] + lb_ref[...]
    o_ref[...] = ((yn + bonf_ref[...] + bonb_ref[...]) * g_ref[...].astype(F32)).astype(o_ref.dtype)


def rwkv_mix(rest, w0, w_up, a0, a_up, g_up, k_k, k_a, r_k, lnx_g, lnx_b):
    bsz, l, n_cols = rest.shape
    width = w0.shape[1]
    rank = w_up.shape[1]
    assert n_cols == 3 * width + 2 * LANES and 2 * rank == LANES
    c = RWKV_CHUNK * RWKV_TILE_CHUNKS
    n_chunks = -(-l // c)
    zeros = jnp.zeros((2, rank, width), F32)
    w_wa = jnp.concatenate([jnp.concatenate([w_up.astype(F32), zeros], axis=2),
                            jnp.concatenate([zeros, a_up.astype(F32)], axis=2)], axis=1)
    w_wa = w_wa.astype(BF16)

    fwd_chunk = lambda i: i
    bwd_chunk = lambda i: n_chunks - 1 - i

    def tile_specs(chunk_of):
        return [pl.BlockSpec((None, c, n_cols), lambda b, i: (b, chunk_of(i), 0))]

    row2 = lambda a: a.astype(F32).reshape(1, -1)
    whole = lambda *shape: pl.BlockSpec(shape, lambda b, i: (0,) * len(shape))
    out_spec = lambda chunk_of: pl.BlockSpec((None, c, width), lambda b, i: (b, chunk_of(i), 0))
    act = lambda dt: jax.ShapeDtypeStruct((bsz, l, width), dt)
    y_f, y_b, bon_f, bon_b, g = pl.pallas_call(
        functools.partial(_rwkv_scan_kernel, seq_len=l, width=width),
        grid=(bsz, n_chunks),
        in_specs=tile_specs(fwd_chunk) + tile_specs(bwd_chunk) + [
            whole(2, 1, width), whole(2, 1, width),
            whole(2, LANES, 2 * width),
            whole(LANES, width),
            whole(1, width), whole(1, width), whole(1, width),
        ],
        out_specs=[out_spec(fwd_chunk), out_spec(bwd_chunk), out_spec(fwd_chunk), out_spec(bwd_chunk),
                   out_spec(fwd_chunk)],
        out_shape=[act(F32), act(F32), act(F32), act(F32), act(BF16)],
        scratch_shapes=[pltpu.VMEM((2 * (width // LANES), LANES, LANES), F32)],
        compiler_params=_cparams("parallel", "arbitrary"),
        name="rwkv_scan",
    )(rest, rest, w0.astype(F32).reshape(2, 1, width),
      a0.astype(F32).reshape(2, 1, width), w_wa, g_up.astype(BF16), row2(k_k), row2(k_a), row2(r_k))

    n = bsz * l
    tm = _row_tile(n, 608)
    rows = lambda: pl.BlockSpec((tm, width), lambda j: (j, 0))
    flat = lambda a: a.reshape(n, width)
    return pl.pallas_call(
        _rwkv_finish_kernel,
        grid=(n // tm,),
        in_specs=[rows(), rows(), rows(), rows(), rows(),
                  pl.BlockSpec((1, width), lambda j: (0, 0)),
                  pl.BlockSpec((1, width), lambda j: (0, 0))],
        out_specs=rows(),
        out_shape=jax.ShapeDtypeStruct((n, width), BF16),
        compiler_params=_cparams("parallel"),
        name="rwkv_finish",
    )(flat(y_f), flat(y_b), flat(bon_f), flat(bon_b), flat(g), row2(lnx_g), row2(lnx_b)).reshape(bsz, l, width)


S5_CHUNK = 16


def _cpow(n, lr, li, step):
    mag = jnp.exp(n * (lr * step))
    ang = n * (li * step)
    return mag * jnp.cos(ang), mag * jnp.sin(ang)


def _s5_param_kernel(lamr_ref, stepr_ref, bt_ref, ct_ref, kmat_ref, wst_ref, cexp_ref, alpha_ref):
    t_len = S5_CHUNK
    n_i = S5_GROUP_CH
    p2 = 2 * S5_STATE
    ti = t_len * n_i

    lr = lamr_ref[0:1, :]
    li = lamr_ref[1:2, :]
    step = jnp.exp(stepr_ref[...])
    ab_re, ab_im = _cpow(1.0, lr, li, step)
    den = lr * lr + li * li
    z_re = ((ab_re - 1.0) * lr + ab_im * li) / den
    z_im = (ab_im * lr - (ab_re - 1.0) * li) / den
    t16 = lax.broadcasted_iota(I32, (t_len, p2), 0).astype(F32)
    is_f = lax.broadcasted_iota(I32, (t_len, p2), 1) < S5_STATE

    def rows_by_t(x):
        return jnp.concatenate([jnp.broadcast_to(x[t:t + 1], (n_i, p2)) for t in range(t_len)], axis=0)

    def tiled_rows(x):
        return jnp.concatenate([x] * t_len, axis=0)

    def pow_rows(n):
        q_re, q_im = _cpow(n, lr, li, step)
        return rows_by_t(q_re), rows_by_t(q_im)

    bt_re = tiled_rows(bt_ref[0])
    bt_im = tiled_rows(bt_ref[1])
    bb_re = z_re * bt_re - z_im * bt_im
    bb_im = z_re * bt_im + z_im * bt_re
    pw_re, pw_im = pow_rows(jnp.where(is_f, (t_len - 1.0) - t16, t16))
    wst_ref[:, 0:p2] = (pw_re * bb_re - pw_im * bb_im).astype(wst_ref.dtype)
    wst_ref[:, p2:2 * p2] = (pw_re * bb_im + pw_im * bb_re).astype(wst_ref.dtype)
    al_re, al_im = _cpow(float(t_len), lr, li, step)
    alpha_ref[0:1, :] = al_re
    alpha_ref[1:2, :] = al_im

    ct_re = tiled_rows(ct_ref[0])
    ct_im = tiled_rows(ct_ref[1])

    def c_times_pow(n):
        q_re, q_im = pow_rows(n)
        return jnp.transpose(ct_re * q_re - ct_im * q_im), jnp.transpose(ct_re * q_im + ct_im * q_re)

    ca_re, ca_im = c_times_pow(jnp.where(is_f, t16, jnp.where(t16 == 0.0, 0.0, t_len - t16)))
    lane_p = lax.broadcasted_iota(I32, (n_i, p2), 1)
    bbr = bb_re[0:n_i]
    bbi = bb_im[0:n_i]
    zero = jnp.zeros_like(bbr)
    strips = []
    for sel in (lane_p < S5_STATE, lane_p >= S5_STATE):
        strips.append(_mm3(jnp.where(sel, bbr, zero), ca_re) - _mm3(jnp.where(sel, bbi, zero), ca_im))
    strip_f, strip_b = strips
    t_k = lax.broadcasted_iota(I32, (n_i, ti), 1) // n_i
    for tt in range(t_len):
        sf = strip_f if tt == 0 else pltpu.roll(strip_f, tt * n_i, 1)
        sb = strip_b if tt == 0 else pltpu.roll(strip_b, tt * n_i, 1)
        blk = jnp.where(t_k >= tt, sf, 0.0) + jnp.where(t_k <= tt, sb, 0.0)
        kmat_ref[tt * n_i:(tt + 1) * n_i, :] = blk.astype(kmat_ref.dtype)

    co_re, co_im = c_times_pow(jnp.where(is_f, t16 + 1.0, t_len - t16))
    cexp_ref[0:p2, :] = co_re.astype(cexp_ref.dtype)
    cexp_ref[p2:2 * p2, :] = (-co_im).astype(cexp_ref.dtype)


def _s5_main_kernel(u_ref, kmat_ref, wst_ref, cexp_ref, alpha_ref, y_ref, x_ref, sf_ref, sb_ref,
                    *, n_batch, n_chunks):
    p2 = 2 * S5_STATE
    n_gb = u_ref.shape[0]
    for g in range(n_gb):
        x_ref[g] = _dot(u_ref[g].astype(BF16), wst_ref[g])
    lane = lax.broadcasted_iota(I32, (1, p2), 1)
    is_f = lane < S5_STATE
    alphas = [(alpha_ref[g, 0:1, :], alpha_ref[g, 1:2, :]) for g in range(n_gb)]

    sub = S5_SCAN_ROWS
    assert n_chunks % sub == 0
    chains = [(g, b) for g in range(n_gb) for b in range(n_batch)]

    def step(k, carry):
        new = []
        for (g, b), (s_re, s_im) in zip(chains, carry):
            a_re, a_im = alphas[g]
            row_f = pl.multiple_of(b * n_chunks + sub * k, sub)
            row_b = pl.multiple_of(b * n_chunks + (n_chunks - sub) - sub * k, sub)
            xf = x_ref[g, pl.ds(row_f, sub), :]
            xb = x_ref[g, pl.ds(row_b, sub), :]
            seen = []
            for r in range(sub):
                seen.append(jnp.concatenate([s_re, s_im], axis=1))
                rb = sub - 1 - r
                x_re = jnp.where(is_f, xf[r:r + 1, 0:p2], xb[rb:rb + 1, 0:p2])
                x_im = jnp.where(is_f, xf[r:r + 1, p2:2 * p2], xb[rb:rb + 1, p2:2 * p2])
                s_re, s_im = a_re * s_re - a_im * s_im + x_re, a_re * s_im + a_im * s_re + x_im
            sf_ref[g, pl.ds(row_f, sub), :] = jnp.concatenate(seen, axis=0)
            sb_ref[g, pl.ds(row_b, sub), :] = jnp.concatenate(seen[::-1], axis=0)
            new.append((s_re, s_im))
        return tuple(new)

    zero = jnp.zeros((1, p2), F32)
    lax.fori_loop(0, n_chunks // sub, step, tuple((zero, zero) for _ in chains))
    lane2 = lax.broadcasted_iota(I32, sf_ref.shape[1:], 1) % p2
    for g in range(n_gb):
        s_in = jnp.where(lane2 < S5_STATE, sf_ref[g], sb_ref[g])
        s_hi, s_lo = _split_bf16(s_in)
        y_ref[g] = (_dot(u_ref[g].astype(BF16), kmat_ref[g]) + _dot(s_hi, cexp_ref[g]) + _dot(s_lo, cexp_ref[g]))


S5_RELAYOUT_CHUNKS = 128
S5_SCAN_ROWS = 8
S5_GROUPS_PER_STEP = 4


def _s5_group_major_kernel(h_ref, g_ref, u_ref, hn_ref, ut_ref, *, seq_len):
    n_g, mt, ti = u_ref.shape
    t_len = S5_CHUNK
    n_i = ti // t_len
    n_lt = hn_ref.shape[0]
    g_lt = LANES // n_i
    rows = h_ref.shape[0]
    valid = seq_len - pl.program_id(1) * rows
    row = lax.broadcasted_iota(I32, h_ref.shape, 0)
    hn = jnp.where(row < valid, _rms(h_ref[...], g_ref[...]), 0.0)
    for j in range(n_lt):
        hn_ref[j] = hn[:, j * LANES:(j + 1) * LANES]
    for tau in range(t_len):
        for j in range(n_lt):
            xt = jnp.transpose(hn_ref[j, pl.ds(tau, mt, stride=t_len), :])
            ut_ref[j * g_lt:(j + 1) * g_lt, tau * n_i:(tau + 1) * n_i, :] = xt.reshape(g_lt, n_i, mt)
    for g in range(n_g):
        u_ref[g] = jnp.transpose(ut_ref[g]).astype(u_ref.dtype)


def _s5_token_major_kernel(y_ref, o_ref, zt_ref, z_ref):
    n_g, mt, ti = y_ref.shape
    t_len = S5_CHUNK
    n_i = ti // t_len
    n_lt = z_ref.shape[0]
    for g in range(n_g):
        yt = jnp.transpose(y_ref[g])
        zt_ref[:, g * n_i:(g + 1) * n_i, :] = yt.reshape(t_len, n_i, mt)
    for t in range(t_len):
        for j in range(n_lt):
            z_ref[j, pl.ds(t, mt, stride=t_len), :] = jnp.transpose(zt_ref[t, j * LANES:(j + 1) * LANES, :])
    for j in range(n_lt):
        o_ref[:, j * LANES:(j + 1) * LANES] = z_ref[j]


def _gelu_tanh(x):
    return 0.5 * x * (1.0 + jnp.tanh(math.sqrt(2.0 / math.pi) * (x + 0.044715 * (x * x * x))))


def _s5_glu_kernel(h_ref, y_ref, g_ref, d_ref, w_ref, o_ref):
    h = h_ref[...]
    dm = h.shape[1]
    y = y_ref[...] + d_ref[...] * _rms(h, g_ref[...])
    gl = _gelu_tanh(y).astype(BF16)
    a = _dot(gl, w_ref[:, 0:dm])
    b = _dot(gl, w_ref[:, dm:2 * dm])
    o_ref[...] = h + a * jax.nn.sigmoid(b)


def s5_mix(h3, gain, b_re, b_im, lam_re, lam_im, log_step, c_re, c_im, d_skip, w_glu):
    bsz, l, dm = h3.shape
    n_g, n_p, n_i = b_re.shape
    t_len = S5_CHUNK
    assert l % t_len == 0 and n_g * n_i == dm and n_p == S5_STATE and n_i == S5_GROUP_CH
    n_chunks = -(-(l // t_len) // S5_SCAN_ROWS) * S5_SCAN_ROWS
    m = bsz * n_chunks
    ti = t_len * n_i
    p2 = 2 * n_p
    n = bsz * l
    tm = _row_tile(n, 608)
    h2 = h3.reshape(n, dm)
    gain2 = gain.astype(F32).reshape(1, dm)

    mt = min(S5_RELAYOUT_CHUNKS, n_chunks)
    n_tiles = -(-n_chunks // mt)
    u = pl.pallas_call(
        functools.partial(_s5_group_major_kernel, seq_len=l),
        grid=(bsz, n_tiles),
        in_specs=[pl.BlockSpec((None, mt * t_len, dm), lambda b, i: (b, i, 0)),
                  pl.BlockSpec((1, dm), lambda b, i: (0, 0))],
        out_specs=pl.BlockSpec((n_g, None, mt, ti), lambda b, i: (0, b, i, 0)),
        out_shape=jax.ShapeDtypeStruct((n_g, bsz, n_chunks, ti), BF16),
        scratch_shapes=[pltpu.VMEM((dm // LANES, mt * t_len, LANES), F32), pltpu.VMEM((n_g, ti, mt), F32)],
        compiler_params=_cparams("parallel", "parallel"),
        name="s5_group_major",
    )(h3, gain2).reshape(n_g, m, ti)

    f32 = lambda a: a.astype(F32)
    lam_r = jnp.stack([jnp.concatenate([f32(lam_re)[0], f32(lam_re)[1]], axis=-1),
                       jnp.concatenate([f32(lam_im)[0], f32(lam_im)[1]], axis=-1)], axis=1)
    step_r = jnp.repeat(jnp.transpose(f32(log_step))[:, None, :], n_p, axis=2)
    bt = jnp.stack([jnp.transpose(f32(b_re), (0, 2, 1)), jnp.transpose(f32(b_im), (0, 2, 1))], axis=1)
    bt = jnp.tile(bt, (1, 1, 1, 2))
    ct = jnp.stack([f32(c_re), f32(c_im)], axis=0)
    ct = jnp.transpose(ct, (2, 0, 3, 1, 4)).reshape(n_g, 2, n_i, p2)

    gspec = lambda *shape: pl.BlockSpec((None,) + shape, lambda g: (g,) + (0,) * len(shape))
    kmat, wst, cexp, alpha = pl.pallas_call(
        _s5_param_kernel,
        grid=(n_g,),
        in_specs=[gspec(2, p2), gspec(1, p2), gspec(2, n_i, p2), gspec(2, n_i, p2)],
        out_specs=[gspec(ti, ti), gspec(ti, 2 * p2), gspec(2 * p2, ti), gspec(2, p2)],
        out_shape=[
            jax.ShapeDtypeStruct((n_g, ti, ti), BF16),
            jax.ShapeDtypeStruct((n_g, ti, 2 * p2), BF16),
            jax.ShapeDtypeStruct((n_g, 2 * p2, ti), BF16),
            jax.ShapeDtypeStruct((n_g, 2, p2), F32),
        ],
        compiler_params=_cparams("parallel"),
        name="s5_params",
    )(lam_r, step_r, bt, ct)

    gb = S5_GROUPS_PER_STEP
    assert n_g % gb == 0
    gbspec = lambda *shape: pl.BlockSpec((gb,) + shape, lambda g: (g,) + (0,) * len(shape))
    y = pl.pallas_call(
        functools.partial(_s5_main_kernel, n_batch=bsz, n_chunks=n_chunks),
        grid=(n_g // gb,),
        in_specs=[gbspec(m, ti), gbspec(ti, ti), gbspec(ti, 2 * p2), gbspec(2 * p2, ti), gbspec(2, p2)],
        out_specs=gbspec(m, ti),
        out_shape=jax.ShapeDtypeStruct((n_g, m, ti), F32),
        scratch_shapes=[pltpu.VMEM((gb, m, 2 * p2), F32)] * 3,
        compiler_params=_cparams("parallel"),
        name="s5_main",
    )(u, kmat, wst, cexp, alpha)
    y2 = pl.pallas_call(
        _s5_token_major_kernel,
        grid=(bsz, n_tiles),
        in_specs=[pl.BlockSpec((n_g, None, mt, ti), lambda b, i: (0, b, i, 0))],
        out_specs=pl.BlockSpec((None, mt * t_len, dm), lambda b, i: (b, i, 0)),
        out_shape=jax.ShapeDtypeStruct((bsz, l, dm), F32),
        scratch_shapes=[pltpu.VMEM((t_len, dm, mt), F32), pltpu.VMEM((dm // LANES, mt * t_len, LANES), F32)],
        compiler_params=_cparams("parallel", "parallel"),
        name="s5_token_major",
    )(y.reshape(n_g, bsz, n_chunks, ti)).reshape(n, dm)

    out = pl.pallas_call(
        _s5_glu_kernel,
        grid=(n // tm,),
        in_specs=[
            pl.BlockSpec((tm, dm), lambda i: (i, 0)),
            pl.BlockSpec((tm, dm), lambda i: (i, 0)),
            pl.BlockSpec((1, dm), lambda i: (0, 0)),
            pl.BlockSpec((1, dm), lambda i: (0, 0)),
            pl.BlockSpec((dm, 2 * dm), lambda i: (0, 0)),
        ],
        out_specs=pl.BlockSpec((tm, dm), lambda i: (i, 0)),
        out_shape=jax.ShapeDtypeStruct((n, dm), F32),
        compiler_params=_cparams("parallel"),
        name="s5_glu",
    )(h2, y2, gain2, f32(d_skip).reshape(1, dm), w_glu.astype(BF16))
    return out.reshape(bsz, l, dm)


def na_rwkv_mix(h3, gain, w_in, w_out, rpb, mu, w0, w_up, a0, a_up, g_up, k_k, k_a, r_k, lnx_g, lnx_b):
    bsz, l, dm = h3.shape
    n = bsz * l
    h2 = h3.reshape(n, dm)
    n_qkv = 3 * (w_out.shape[0] // 2)
    qkv, rest = norm_inproj(h2, gain.astype(F32), w_in.astype(BF16), n_qkv, mu, l)
    na = na_attention(qkv.reshape(bsz, l, n_qkv), rpb)
    rw = rwkv_mix(rest.reshape(bsz, l, -1), w0, w_up, a0, a_up, g_up, k_k, k_a, r_k, lnx_g, lnx_b)
    out = outproj_residual(h2, na.reshape(n, -1), rw.reshape(n, -1), w_out.astype(BF16))
    return out.reshape(bsz, l, dm)


def kernel(x, meta_tokens, norm_mix, norm_ffn, norm_final, mix_w_in, mix_w_out, na_rpb, rwkv_mu,
           rwkv_w0, rwkv_w_up, rwkv_a0, rwkv_a_up, rwkv_g_up, rwkv_k_k, rwkv_k_a, rwkv_r_k,
           rwkv_lnx_g, rwkv_lnx_b, s5_b_re, s5_b_im, s5_lambda_re, s5_lambda_im, s5_log_step,
           s5_c_re, s5_c_im, s5_d, s5_w_glu, moe_w_group, moe_b_group, moe_w_expert, moe_b_expert,
           moe_w1, moe_w3, moe_w2):
    bsz, _, dm = x.shape
    depth = norm_mix.shape[0]
    meta = jnp.broadcast_to(meta_tokens.astype(x.dtype)[None], (bsz,) + meta_tokens.shape)
    h = jnp.concatenate([meta, x], axis=1)
    l = h.shape[1]
    for layer in range(depth):
        i = layer // 2
        if layer % 2 == 0:
            h = na_rwkv_mix(h, norm_mix[layer], mix_w_in[i], mix_w_out[i], na_rpb[i], rwkv_mu[i], rwkv_w0[i],
                            rwkv_w_up[i], rwkv_a0[i], rwkv_a_up[i], rwkv_g_up[i], rwkv_k_k[i], rwkv_k_a[i],
                            rwkv_r_k[i], rwkv_lnx_g[i], rwkv_lnx_b[i])
        else:
            h = s5_mix(h, norm_mix[layer], s5_b_re[i], s5_b_im[i], s5_lambda_re[i], s5_lambda_im[i],
                       s5_log_step[i], s5_c_re[i], s5_c_im[i], s5_d[i], s5_w_glu[i])
        h = hierarchical_moe_residual(h.reshape(bsz * l, dm), norm_ffn[layer].astype(F32), moe_w_group[layer],
                                      moe_b_group[layer], moe_w_expert[layer], moe_b_expert[layer],
                                      moe_w1, moe_w3, moe_w2, layer).reshape(bsz, l, dm)
    return final_norm(h, norm_final.astype(F32))
```

```python
import functools
import math

import jax
import jax.numpy as jnp
from jax import lax
from jax.experimental import pallas as pl
from jax.experimental.pallas import tpu as pltpu

F32 = jnp.float32
BF16 = jnp.bfloat16
I32 = jnp.int32

N_META = 16
GRID_W = 64
HEAD_DIM = 64
NA_WIN_ROWS = 8
NA_WIN_COLS = 16
S5_GROUP_CH = 16
S5_STATE = 64
MOE_GROUPS = 4
MOE_PER_GROUP = 8
MOE_EXPERTS = MOE_GROUPS * MOE_PER_GROUP
NORM_EPS = 1e-6
RWKV_GN_EPS = 64e-5
NEG_INF = -1e30

LANES = 128
SUBLANES_BF16 = 16
VMEM_LIMIT_BYTES = 56 * 1024 * 1024

MOE_TILE = 256
MOE_TOKEN_TILE = 320
MOE_STAGE_SLOTS = 3
ROUTER_LANES = 128


def _cparams(*sem):
    return pltpu.CompilerParams(dimension_semantics=sem, vmem_limit_bytes=VMEM_LIMIT_BYTES)


def _row_tile(n, target):
    best = None
    for t in range(SUBLANES_BF16, min(n, target) + 1, SUBLANES_BF16):
        if n % t == 0:
            best = t
    assert best is not None, (n, target)
    return best


def _rms(x, gain):
    ms = jnp.mean(x * x, axis=-1, keepdims=True)
    return (x * lax.rsqrt(ms + NORM_EPS)) * gain


def _split_bf16(x):
    hi = x.astype(BF16)
    lo = (x - hi.astype(F32)).astype(BF16)
    return hi, lo


def _dot(a, b):
    return jnp.dot(a, b, preferred_element_type=F32)


def _dot_nt(a, b):
    return lax.dot_general(a, b, (((1,), (1,)), ((), ())), preferred_element_type=F32)


SHIFT_HALO = SUBLANES_BF16


def _norm_inproj_kernel(h_ref, hp_ref, hn_ref, g_ref, w_ref, mu_ref, qkv_ref, rest_ref, xn_ref, y_ref,
                        *, n_qkv, chunk, seq_len):
    tm = h_ref.shape[0]
    hl = SHIFT_HALO
    gain = g_ref[...]
    xn_ref[...] = jnp.concatenate([_rms(hp_ref[...], gain), _rms(h_ref[...], gain), _rms(hn_ref[...], gain)],
                                  axis=0).astype(BF16)
    t = (pl.program_id(0) * tm + lax.broadcasted_iota(I32, (tm, chunk), 0)) % seq_len
    has_prev = t > 0
    has_next = t < seq_len - 1
    n_all = w_ref.shape[1]
    for c in range(0, n_all, chunk):
        if c < n_qkv:
            qkv_ref[:, c:c + chunk] = _dot(xn_ref[hl:hl + tm, :], w_ref[:, c:c + chunk]).astype(BF16)
        else:
            y_ref[...] = _dot(xn_ref[...], w_ref[:, c:c + chunk])
            p = y_ref[hl:hl + tm, :]
            nb = 0.5 * (jnp.where(has_prev, y_ref[hl - 1:hl - 1 + tm, :], 0.0)
                        + jnp.where(has_next, y_ref[hl + 1:hl + 1 + tm, :], 0.0))
            rest_ref[:, c - n_qkv:c - n_qkv + chunk] = p + mu_ref[:, c - n_qkv:c - n_qkv + chunk] * (nb - p)


def norm_inproj(h2, gain, w_bf16, n_qkv, mu, seq_len):
    n, d = h2.shape
    n_all = w_bf16.shape[1]
    tm = _row_tile(n, 608)
    chunk = 256
    hl = SHIFT_HALO
    assert n_qkv % chunk == 0 and n_all % chunk == 0 and tm % hl == 0 and n % seq_len == 0
    per = tm // hl
    n_halo = n // hl
    return pl.pallas_call(
        functools.partial(_norm_inproj_kernel, n_qkv=n_qkv, chunk=chunk, seq_len=seq_len),
        grid=(n // tm,),
        in_specs=[
            pl.BlockSpec((tm, d), lambda i: (i, 0)),
            pl.BlockSpec((hl, d), lambda i: (jnp.maximum(i * per - 1, 0), 0)),
            pl.BlockSpec((hl, d), lambda i: (jnp.minimum((i + 1) * per, n_halo - 1), 0)),
            pl.BlockSpec((1, d), lambda i: (0, 0)),
            pl.BlockSpec((d, n_all), lambda i: (0, 0)),
            pl.BlockSpec((1, n_all - n_qkv), lambda i: (0, 0)),
        ],
        out_specs=[
            pl.BlockSpec((tm, n_qkv), lambda i: (i, 0)),
            pl.BlockSpec((tm, n_all - n_qkv), lambda i: (i, 0)),
        ],
        out_shape=[
            jax.ShapeDtypeStruct((n, n_qkv), BF16),
            jax.ShapeDtypeStruct((n, n_all - n_qkv), F32),
        ],
        scratch_shapes=[pltpu.VMEM((tm + 2 * hl, d), BF16), pltpu.VMEM((tm + 2 * hl, chunk), F32)],
        compiler_params=_cparams("parallel"),
        name="norm_inproj",
    )(h2, h2, h2, gain.reshape(1, d), w_bf16, mu.astype(F32).reshape(1, n_all - n_qkv))


def _outproj_kernel(h_ref, na_ref, yf_ref, yb_ref, bonf_ref, bonb_ref, g_ref, lg_ref, lb_ref, wa_ref, wb_ref, o_ref):
    rw = _rwkv_readout(yf_ref[...], yb_ref[...], bonf_ref[...], bonb_ref[...], g_ref[...], lg_ref[...], lb_ref[...])
    acc = _dot(na_ref[...], wa_ref[...])
    acc = acc + _dot(rw.astype(BF16), wb_ref[...])
    o_ref[...] = h_ref[...] + acc


def outproj_residual(h2, na, rwkv_parts, lnx_g, lnx_b, w_out_bf16):
    n, d = h2.shape
    ka, kb = na.shape[1], rwkv_parts[0].shape[1]
    tm = _row_tile(n, 608)
    rows = lambda w: pl.BlockSpec((tm, w), lambda i: (i, 0))
    whole = lambda r, c: pl.BlockSpec((r, c), lambda i: (0, 0))
    return pl.pallas_call(
        _outproj_kernel,
        grid=(n // tm,),
        in_specs=[rows(d), rows(ka)] + [rows(kb)] * 5 + [whole(1, kb), whole(1, kb), whole(ka, d), whole(kb, d)],
        out_specs=rows(d),
        out_shape=jax.ShapeDtypeStruct((n, d), F32),
        compiler_params=_cparams("parallel"),
        name="outproj_residual",
    )(h2, na, *rwkv_parts, lnx_g.astype(F32).reshape(1, kb), lnx_b.astype(F32).reshape(1, kb),
      w_out_bf16[:ka], w_out_bf16[ka:])


def _store_token_tiles(ref, x):
    rows = x.shape[0]
    s_n = x.shape[1] // LANES
    for s in range(s_n):
        ref[pl.ds(s, rows, stride=s_n), :] = x[:, s * LANES:(s + 1) * LANES]


def _load_token_tile_cols(ref, s, rows, s_n):
    return ref[pl.ds(s, rows, stride=s_n), :]


def _router_kernel(h_ref, g_ref, whi_ref, wlo_ref, b_ref, xn_ref, route_ref):
    xn = _rms(h_ref[...], g_ref[...])
    x_hi, x_lo = _split_bf16(xn)
    _store_token_tiles(xn_ref, xn)
    logits = (_dot(x_hi, whi_ref[...]) + _dot(x_hi, wlo_ref[...]) + _dot(x_lo, whi_ref[...])
              + b_ref[...])
    tm = logits.shape[0]
    lane = lax.broadcasted_iota(I32, (tm, ROUTER_LANES), 1)
    big = jnp.int32(ROUTER_LANES)

    is_g = lane < MOE_GROUPS
    lg = jnp.where(is_g, logits, -jnp.inf)
    eg = jnp.where(is_g, jnp.exp(lg - jnp.max(lg, axis=-1, keepdims=True)), 0.0)
    pg = eg / jnp.sum(eg, axis=-1, keepdims=True)
    p_grp = jnp.max(pg, axis=-1, keepdims=True)
    grp = jnp.min(jnp.where(is_g & (pg == p_grp), lane, big), axis=-1, keepdims=True)

    lo_lane = MOE_GROUPS + MOE_PER_GROUP * grp
    is_e = (lane >= lo_lane) & (lane < lo_lane + MOE_PER_GROUP)
    le = jnp.where(is_e, logits, -jnp.inf)
    ee = jnp.where(is_e, jnp.exp(le - jnp.max(le, axis=-1, keepdims=True)), 0.0)
    pe = jnp.where(is_e, ee / jnp.sum(ee, axis=-1, keepdims=True), -1.0)
    p1 = jnp.max(pe, axis=-1, keepdims=True)
    i1 = jnp.min(jnp.where(pe == p1, lane, big), axis=-1, keepdims=True)
    pe2 = jnp.where(lane == i1, -1.0, pe)
    p2 = jnp.max(pe2, axis=-1, keepdims=True)
    i2 = jnp.min(jnp.where(pe2 == p2, lane, big), axis=-1, keepdims=True)
    denom = p1 + p2
    g1 = p_grp * p1 / denom
    g2 = p_grp * p2 / denom
    e1 = (i1 - MOE_GROUPS).astype(F32)
    e2 = (i2 - MOE_GROUPS).astype(F32)
    route_ref[...] = jnp.where(lane == 0, e1, jnp.where(lane == 1, e2, jnp.where(lane == 2, g1, g2)))


def moe_router(h2, gain, w_group, b_group, w_expert, b_expert):
    n, d = h2.shape
    n_r = MOE_GROUPS + MOE_EXPERTS
    w_r = jnp.concatenate([w_group, jnp.transpose(w_expert, (1, 0, 2)).reshape(d, MOE_EXPERTS)], axis=1)
    w_r = jnp.pad(w_r.astype(F32), ((0, 0), (0, ROUTER_LANES - n_r)))
    w_hi, w_lo = _split_bf16(w_r)
    b_r = jnp.pad(jnp.concatenate([b_group, b_expert.reshape(-1)]).astype(F32), (0, ROUTER_LANES - n_r))
    tm = _row_tile(n, 608)
    return pl.pallas_call(
        _router_kernel,
        grid=(n // tm,),
        in_specs=[
            pl.BlockSpec((tm, d), lambda i: (i, 0)),
            pl.BlockSpec((1, d), lambda i: (0, 0)),
            pl.BlockSpec((d, ROUTER_LANES), lambda i: (0, 0)),
            pl.BlockSpec((d, ROUTER_LANES), lambda i: (0, 0)),
            pl.BlockSpec((1, ROUTER_LANES), lambda i: (0, 0)),
        ],
        out_specs=[
            pl.BlockSpec((tm * (d // LANES), LANES), lambda i: (i, 0)),
            pl.BlockSpec((tm, ROUTER_LANES), lambda i: (i, 0)),
        ],
        out_shape=[
            jax.ShapeDtypeStruct((n * (d // LANES), LANES), F32),
            jax.ShapeDtypeStruct((n, ROUTER_LANES), F32),
        ],
        compiler_params=_cparams("parallel"),
        name="moe_router",
    )(h2, gain.reshape(1, d), w_hi, w_lo, b_r.reshape(1, ROUTER_LANES))


def _moe_dispatch_kernel(tail_ref, n_used_ref, dst_ref, xn_hbm, xbuf_hbm, zero_ref, stage_ref, sem, lsem, zsem,
                         *, tm, s_n, n_blocks):
    i = pl.program_id(0)
    n_steps = pl.num_programs(0)
    tile_rows = zero_ref.shape[0]

    def zero_block(b, carry):
        pltpu.make_async_copy(zero_ref, xbuf_hbm.at[pl.ds(pl.multiple_of(b * tile_rows, tile_rows), tile_rows), :],
                              zsem).start()
        return carry

    def wait_zero_block(b, carry):
        pltpu.make_async_copy(zero_ref, xbuf_hbm.at[pl.ds(0, tile_rows), :], zsem).wait()
        return carry

    n_slots = stage_ref.shape[0]
    slot = i % n_slots
    nxt = (i + 1) % n_slots

    def load(step, s):
        return pltpu.make_async_copy(xn_hbm.at[pl.ds(pl.multiple_of(step * tm * s_n, s_n), tm * s_n), :],
                                     stage_ref.at[s], lsem.at[s])

    def wait_rows_out(s):
        for _ in range(2):
            pltpu.make_async_copy(stage_ref.at[s], xbuf_hbm.at[pl.ds(0, tm * s_n), :], sem.at[s]).wait()

    @pl.when(i == 0)
    def _():
        load(0, 0).start()
        zero_ref[...] = jnp.zeros_like(zero_ref)
        for e in range(tail_ref.shape[0]):
            @pl.when(tail_ref[e] >= 0)
            def _():
                pltpu.make_async_copy(zero_ref, xbuf_hbm.at[pl.ds(pl.multiple_of(tail_ref[e], s_n), tile_rows), :],
                                      zsem).start()
        lax.fori_loop(n_used_ref[0], n_blocks, zero_block, 0)
        for e in range(tail_ref.shape[0]):
            @pl.when(tail_ref[e] >= 0)
            def _():
                wait_zero_block(0, 0)
        lax.fori_loop(n_used_ref[0], n_blocks, wait_zero_block, 0)

    @pl.when(i >= n_slots - 1)
    def _():
        wait_rows_out(nxt)

    @pl.when(i + 1 < n_steps)
    def _():
        load(i + 1, nxt).start()

    load(i, slot).wait()
    for r in range(tm):
        src = stage_ref.at[slot, pl.ds(r * s_n, s_n), :]
        for k in range(2):
            dst = pl.multiple_of(dst_ref[k, r], s_n)
            pltpu.make_async_copy(src, xbuf_hbm.at[pl.ds(dst, s_n), :], sem.at[slot]).start(priority=k)

    @pl.when(i == n_steps - 1)
    def _():
        for back in range(n_slots - 1):
            @pl.when(i >= back)
            def _():
                wait_rows_out((i - back) % n_slots)


def moe_dispatch(xn_tiles, dst_tiles, tail_start, n_used, n_blocks, tm, s_n):
    n_steps = dst_tiles.shape[0]
    n_rows = n_blocks * MOE_TILE
    assert n_steps * tm * s_n == xn_tiles.shape[0]
    grid_spec = pltpu.PrefetchScalarGridSpec(
        num_scalar_prefetch=2,
        grid=(n_steps,),
        in_specs=[
            pl.BlockSpec((None, 2, tm), lambda i, tail, nu: (i, 0, 0), memory_space=pltpu.SMEM),
            pl.BlockSpec(memory_space=pl.ANY),
        ],
        out_specs=pl.BlockSpec(memory_space=pl.ANY),
        scratch_shapes=[
            pltpu.VMEM((MOE_TILE * s_n, LANES), F32),
            pltpu.VMEM((MOE_STAGE_SLOTS, tm * s_n, LANES), F32),
            pltpu.SemaphoreType.DMA((MOE_STAGE_SLOTS,)),
            pltpu.SemaphoreType.DMA((MOE_STAGE_SLOTS,)),
            pltpu.SemaphoreType.DMA(()),
        ],
    )
    return pl.pallas_call(
        functools.partial(_moe_dispatch_kernel, tm=tm, s_n=s_n, n_blocks=n_blocks),
        grid_spec=grid_spec,
        out_shape=jax.ShapeDtypeStruct((n_rows * s_n, LANES), F32),
        compiler_params=_cparams("arbitrary"),
        name="moe_dispatch",
    )(tail_start, n_used, dst_tiles, xn_tiles)


def _expert_kernel(blk_e_ref, n_used_ref, x_ref, w1_ref, w3_ref, w2_ref, y_ref, xb_ref, w1b_ref, w3b_ref, w2b_ref):
    i = pl.program_id(0)
    used = i < n_used_ref[0]
    tile, d = xb_ref.shape
    s_n = d // LANES
    prev_e = blk_e_ref[jnp.maximum(i - 1, 0)]
    fresh = (i == 0) | (blk_e_ref[i] != prev_e)

    @pl.when(used & fresh)
    def _():
        w1b_ref[...] = w1_ref[...].astype(BF16)
        w3b_ref[...] = w3_ref[...].astype(BF16)
        w2b_ref[...] = w2_ref[...].astype(BF16)

    @pl.when(used)
    def _():
        for s in range(s_n):
            xb_ref[:, s * LANES:(s + 1) * LANES] = _load_token_tile_cols(x_ref, s, tile, s_n).astype(BF16)
        x = xb_ref[...]
        a = _dot(x, w1b_ref[...])
        b = _dot(x, w3b_ref[...])
        hmid = (a * jax.nn.sigmoid(a) * b).astype(BF16)
        _store_token_tiles(y_ref, _dot(hmid, w2b_ref[...]))

    @pl.when(jnp.logical_not(used))
    def _():
        y_ref[...] = jnp.zeros_like(y_ref)


def moe_experts(xbuf, blk_e, n_used, w1, w3, w2, layer, n_blocks):
    d, f = w1.shape[2], w1.shape[3]
    s_n = d // LANES
    tile = MOE_TILE

    def w_map(i, blk_e_ref, n_used_ref):
        return (layer, blk_e_ref[i], 0, 0)

    def x_map(i, blk_e_ref, n_used_ref):
        return (jnp.minimum(i, jnp.maximum(n_used_ref[0] - 1, 0)), 0)

    grid_spec = pltpu.PrefetchScalarGridSpec(
        num_scalar_prefetch=2,
        grid=(n_blocks,),
        in_specs=[
            pl.BlockSpec((tile * s_n, LANES), x_map),
            pl.BlockSpec((None, None, d, f), w_map),
            pl.BlockSpec((None, None, d, f), w_map),
            pl.BlockSpec((None, None, f, d), w_map),
        ],
        out_specs=pl.BlockSpec((tile * s_n, LANES), lambda i, be, nu: (i, 0)),
        scratch_shapes=[
            pltpu.VMEM((tile, d), BF16),
            pltpu.VMEM((d, f), BF16),
            pltpu.VMEM((d, f), BF16),
            pltpu.VMEM((f, d), BF16),
        ],
    )
    return pl.pallas_call(
        _expert_kernel,
        grid_spec=grid_spec,
        out_shape=jax.ShapeDtypeStruct((n_blocks * tile * s_n, LANES), F32),
        compiler_params=_cparams("arbitrary"),
        name="moe_experts",
    )(blk_e, n_used, xbuf, w1, w3, w2)


def _moe_combine_kernel(src_ref, src_next_ref, h_ref, route_ref, y_hbm, o_ref, yg_ref, sem, *, n_steps):
    i = pl.program_id(0)
    slot = i & 1
    tm, d = h_ref.shape
    s_n = d // LANES

    def start_gather(ids_ref, dst_slot):
        for k in range(2):
            for r in range(tm):
                src = pl.multiple_of(ids_ref[k, r], s_n)
                pltpu.make_async_copy(y_hbm.at[pl.ds(src, s_n), :],
                                      yg_ref.at[dst_slot, k, pl.ds(r * s_n, s_n), :],
                                      sem.at[dst_slot]).start(priority=r % 2)

    @pl.when(i == 0)
    def _():
        start_gather(src_ref, 0)

    @pl.when(i + 1 < n_steps)
    def _():
        start_gather(src_next_ref, 1 - slot)

    for k in range(2):
        pltpu.make_async_copy(y_hbm.at[pl.ds(0, tm * s_n), :], yg_ref.at[slot, k], sem.at[slot]).wait()
    route = route_ref[...]
    g1 = route[:, 2:3]
    g2 = route[:, 3:4]
    for s in range(s_n):
        cols = slice(s * LANES, (s + 1) * LANES)
        o_ref[:, cols] = (h_ref[:, cols] + g1 * _load_token_tile_cols(yg_ref.at[slot, 0], s, tm, s_n)
                          + g2 * _load_token_tile_cols(yg_ref.at[slot, 1], s, tm, s_n))


def moe_combine(h2, route, y_tiles, src_tiles, tm):
    n, d = h2.shape
    s_n = d // LANES
    n_steps = src_tiles.shape[0]
    ids = lambda index: pl.BlockSpec((None, 2, tm), lambda i: (index(i), 0, 0), memory_space=pltpu.SMEM)
    return pl.pallas_call(
        functools.partial(_moe_combine_kernel, n_steps=n_steps),
        grid=(n_steps,),
        in_specs=[
            ids(lambda i: i),
            ids(lambda i: jnp.minimum(i + 1, n_steps - 1)),
            pl.BlockSpec((tm, d), lambda i: (i, 0)),
            pl.BlockSpec((tm, ROUTER_LANES), lambda i: (i, 0)),
            pl.BlockSpec(memory_space=pl.ANY),
        ],
        out_specs=pl.BlockSpec((tm, d), lambda i: (i, 0)),
        out_shape=jax.ShapeDtypeStruct((n, d), F32),
        scratch_shapes=[pltpu.VMEM((2, 2, tm * s_n, LANES), F32), pltpu.SemaphoreType.DMA((2,))],
        compiler_params=_cparams("arbitrary"),
        name="moe_combine",
    )(src_tiles, src_tiles, h2, route, y_tiles)


def hierarchical_moe_residual(h2, gain, w_group, b_group, w_expert, b_expert, w1, w3, w2, layer):
    n, d = h2.shape
    xn, route = moe_router(h2, gain, w_group, b_group, w_expert, b_expert)
    e_km = jnp.concatenate([route[:, 0], route[:, 1]]).astype(I32)
    n_assign = 2 * n
    onehot = (e_km[:, None] == jnp.arange(MOE_EXPERTS, dtype=I32)[None, :]).astype(I32)
    csum = jnp.cumsum(onehot, axis=0)
    counts = csum[-1]
    padded = (counts + MOE_TILE - 1) // MOE_TILE * MOE_TILE
    pad_end = jnp.cumsum(padded)
    pad_start = pad_end - padded
    dest = jnp.sum((csum - onehot + pad_start[None, :]) * onehot, axis=1)
    n_blocks = -(-n_assign // MOE_TILE) + MOE_EXPERTS
    blk_start = jnp.arange(n_blocks, dtype=I32) * MOE_TILE
    blk_e = jnp.minimum(jnp.sum((pad_end[None, :] <= blk_start[:, None]).astype(I32), axis=1),
                        MOE_EXPERTS - 1).astype(I32)
    n_used = (pad_end[-1] // MOE_TILE).astype(I32).reshape(1)
    s_n = d // LANES
    tm = _row_tile(n, MOE_TOKEN_TILE)
    dest_tiles = jnp.transpose((dest * s_n).astype(I32).reshape(2, n // tm, tm), (1, 0, 2))
    tail_start = jnp.where(counts > 0, (pad_end - MOE_TILE) * s_n, -1).astype(I32)
    xbuf = moe_dispatch(xn, dest_tiles, tail_start, n_used, n_blocks, tm, s_n)
    y = moe_experts(xbuf, blk_e, n_used, w1, w3, w2, layer, n_blocks)
    return moe_combine(h2, route, y, dest_tiles, tm)


def _final_norm_kernel(h_ref, g_ref, o_ref):
    o_ref[...] = _rms(h_ref[...], g_ref[...])


def final_norm(h3, gain):
    b, l, d = h3.shape
    t = l - N_META
    tm = _row_tile(t, 512)
    return pl.pallas_call(
        _final_norm_kernel,
        grid=(b, t // tm),
        in_specs=[
            pl.BlockSpec((None, pl.Element(tm), pl.Element(d)),
                         lambda bi, i: (bi, pl.multiple_of(N_META + i * tm, SUBLANES_BF16), 0)),
            pl.BlockSpec((1, d), lambda bi, i: (0, 0)),
        ],
        out_specs=pl.BlockSpec((None, tm, d), lambda bi, i: (bi, i, 0)),
        out_shape=jax.ShapeDtypeStruct((b, t, d), F32),
        compiler_params=_cparams("parallel", "parallel"),
        name="final_norm",
    )(h3, gain.reshape(1, d))


NA_QROWS = 8
NA_KROWS = 3 * NA_QROWS
NA_ROWS_PER_ITER = 4


def _na_kernel(q_ref, kw_ref, vw_ref, qm_ref, km_ref, vm_ref, bias_ref, o_ref, om_ref, *, rows, scale):
    blk = pl.program_id(1)
    tq = GRID_W
    n_pairs = q_ref.shape[1] // LANES
    base = jnp.clip(NA_QROWS * blk - NA_QROWS, 0, rows - NA_KROWS)
    lane = lax.broadcasted_iota(I32, (tq, LANES), 1)
    halves = [lane < HEAD_DIM, lane >= HEAD_DIM]

    pad = jnp.zeros((LANES - N_META, LANES), km_ref.dtype)
    k_meta = [jnp.concatenate([km_ref[:, p * LANES:(p + 1) * LANES], pad], axis=0) for p in range(n_pairs)]
    v_meta = [jnp.concatenate([vm_ref[:, p * LANES:(p + 1) * LANES], pad], axis=0) for p in range(n_pairs)]
    lane2 = lax.broadcasted_iota(I32, (2 * tq, LANES), 1)
    meta_bias2 = jnp.where(lane2 < N_META, 0.0, NEG_INF)

    n_win = NA_WIN_ROWS * GRID_W

    def row_body(jb, carry):
        colsl = [slice(p * LANES, (p + 1) * LANES) for p in range(n_pairs)]
        units = [(jj, p) for jj in range(NA_ROWS_PER_ITER) for p in range(n_pairs)]
        s_idx, koff, qoff = [], [], []
        for jj in range(NA_ROWS_PER_ITER):
            j = jb * NA_ROWS_PER_ITER + jj
            r = NA_QROWS * blk + j
            start = jnp.clip(r - NA_WIN_ROWS // 2, 0, rows - NA_WIN_ROWS)
            s_idx.append(start - r + (NA_WIN_ROWS - 1))
            koff.append(pl.multiple_of((start - base) * GRID_W, GRID_W))
            qoff.append(pl.multiple_of(j * tq, tq))
        q_pair = {(jj, p): q_ref[pl.ds(qoff[jj], tq), c]
                  for jj in range(NA_ROWS_PER_ITER) for p, c in enumerate(colsl)}
        k_ext = {(jj, p): jnp.concatenate([kw_ref[pl.ds(koff[jj], n_win), c], k_meta[p]], axis=0)
                 for jj in range(NA_ROWS_PER_ITER) for p, c in enumerate(colsl)}
        v_ext = {(jj, p): jnp.concatenate([vw_ref[pl.ds(koff[jj], n_win), c], v_meta[p]], axis=0)
                 for jj in range(NA_ROWS_PER_ITER) for p, c in enumerate(colsl)}
        qh = [_stack_heads(q_pair[jj, p], halves[0]) for jj, p in units]
        s = [_dot_nt(qh[u], k_ext[jj, p]) * scale + jnp.concatenate([bias_ref[p, s_idx[jj]], meta_bias2], axis=1)
             for u, (jj, p) in enumerate(units)]
        m = [jnp.max(x, axis=-1, keepdims=True) for x in s]
        e = [jnp.exp(x - mx) for x, mx in zip(s, m)]
        den = [jnp.sum(x, axis=-1, keepdims=True) for x in e]
        o = [_dot(e[u].astype(BF16), v_ext[jj, p]) / den[u] for u, (jj, p) in enumerate(units)]
        for u, (jj, p) in enumerate(units):
            o_ref[pl.ds(qoff[jj], tq), colsl[p]] = jnp.where(halves[0], o[u][0:tq], o[u][tq:2 * tq]).astype(o_ref.dtype)
        return carry

    lax.fori_loop(0, NA_QROWS // NA_ROWS_PER_ITER, row_body, 0)

    @pl.when(blk == 0)
    def _():
        lane_m = lax.broadcasted_iota(I32, (N_META, LANES), 1)
        for p in range(n_pairs):
            cols = slice(p * LANES, (p + 1) * LANES)
            q_pair = qm_ref[:, cols]
            kmp = km_ref[:, cols]
            vmp = vm_ref[:, cols]
            outs = []
            for hh in range(2):
                sel = (lane_m < HEAD_DIM) if hh == 0 else (lane_m >= HEAD_DIM)
                qp = jnp.where(sel, q_pair, jnp.zeros_like(q_pair))
                s_m = _dot_nt(qp, kmp) * scale
                p_m = jnp.exp(s_m - jnp.max(s_m, axis=-1, keepdims=True))
                den = jnp.sum(p_m, axis=-1, keepdims=True)
                outs.append(_dot(p_m.astype(BF16), vmp) / den)
            om_ref[:, cols] = jnp.where(lane_m < HEAD_DIM, outs[0], outs[1]).astype(om_ref.dtype)


def _na_bias_table(rpb):
    h = rpb.shape[0]
    c_ids = jnp.arange(GRID_W)
    c_start = jnp.clip(c_ids - NA_WIN_COLS // 2, 0, GRID_W - NA_WIN_COLS)
    in_band = (c_ids[None, :] >= c_start[:, None]) & (c_ids[None, :] < c_start[:, None] + NA_WIN_COLS)
    dc = jnp.clip(c_ids[None, :] - c_ids[:, None] + NA_WIN_COLS - 1, 0, 2 * NA_WIN_COLS - 2)
    tab = jnp.where(in_band[None, None], rpb.astype(F32)[:, :, dc], NEG_INF)
    win = jnp.stack([tab[:, s:s + NA_WIN_ROWS] for s in range(NA_WIN_ROWS)], axis=1)
    per_head = jnp.transpose(win, (0, 1, 3, 2, 4)).reshape(h // 2, 2, NA_WIN_ROWS, GRID_W, NA_WIN_ROWS * GRID_W)
    return jnp.transpose(per_head, (0, 2, 1, 3, 4)).reshape(h // 2, NA_WIN_ROWS, 2 * GRID_W, NA_WIN_ROWS * GRID_W)


def na_attention(qkv, rpb):
    b, l, w3 = qkv.shape
    w = w3 // 3
    t = l - N_META
    rows = t // GRID_W
    assert rows * GRID_W == t and rows % NA_QROWS == 0 and rows >= NA_KROWS
    tq = NA_QROWS * GRID_W
    tk = NA_KROWS * GRID_W
    bias = _na_bias_table(rpb)
    al = SUBLANES_BF16

    def q_map(bi, i):
        return (bi, pl.multiple_of(N_META + i * tq, al), 0)

    def kv_map(col):
        def f(bi, i):
            base = jnp.clip(NA_QROWS * i - NA_QROWS, 0, rows - NA_KROWS)
            return (bi, pl.multiple_of(N_META + base * GRID_W, al), col)
        return f

    def meta_map(col):
        return lambda bi, i: (bi, 0, col)

    el = pl.Element
    grid_out, meta_out = pl.pallas_call(
        functools.partial(_na_kernel, rows=rows, scale=HEAD_DIM ** -0.5),
        grid=(b, rows // NA_QROWS),
        in_specs=[
            pl.BlockSpec((None, el(tq), el(w)), q_map),
            pl.BlockSpec((None, el(tk), el(w)), kv_map(w)),
            pl.BlockSpec((None, el(tk), el(w)), kv_map(2 * w)),
            pl.BlockSpec((None, el(N_META), el(w)), meta_map(0)),
            pl.BlockSpec((None, el(N_META), el(w)), meta_map(w)),
            pl.BlockSpec((None, el(N_META), el(w)), meta_map(2 * w)),
            pl.BlockSpec(bias.shape, lambda bi, i: (0, 0, 0, 0)),
        ],
        out_specs=[
            pl.BlockSpec((None, tq, w), lambda bi, i: (bi, i, 0)),
            pl.BlockSpec((None, N_META, w), lambda bi, i: (bi, 0, 0)),
        ],
        out_shape=[
            jax.ShapeDtypeStruct((b, t, w), BF16),
            jax.ShapeDtypeStruct((b, N_META, w), BF16),
        ],
        compiler_params=_cparams("parallel", "arbitrary"),
        name="na_attention",
    )(qkv, qkv, qkv, qkv, qkv, qkv, bias)
    return jnp.concatenate([meta_out, grid_out], axis=1)


RWKV_CHUNK = 64
RWKV_TILE_CHUNKS = 2


def _split3_bf16(x):
    p1 = x.astype(BF16)
    r1 = x - p1.astype(F32)
    p2 = r1.astype(BF16)
    p3 = (r1 - p2.astype(F32)).astype(BF16)
    return p1, p2, p3


def _mm1(a, b):
    return _dot(a.astype(BF16), b.astype(BF16))


def _mm3(a, b):
    ah, al = _split_bf16(a)
    bh, bl = _split_bf16(b)
    return _dot(ah, bh) + _dot(ah, bl) + _dot(al, bh)


def _mm1_nt(a, b):
    return _dot_nt(a.astype(BF16), b.astype(BF16))


def _mm3_nt(a, b):
    ah, al = _split_bf16(a)
    bh, bl = _split_bf16(b)
    return _dot_nt(ah, bh) + _dot_nt(ah, bl) + _dot_nt(al, bh)


def _exact_left(mat_bf16, x):
    p1, p2, p3 = _split3_bf16(x)
    return _dot(mat_bf16, p1) + _dot(mat_bf16, p2) + _dot(mat_bf16, p3)


def _exact_right(x, mat_bf16):
    p1, p2, p3 = _split3_bf16(x)
    return _dot(p1, mat_bf16) + _dot(p2, mat_bf16) + _dot(p3, mat_bf16)


def _head_block_ones(width):
    ri = lax.broadcasted_iota(I32, (width, width), 0) // HEAD_DIM
    ci = lax.broadcasted_iota(I32, (width, width), 1) // HEAD_DIM
    return (ri == ci).astype(BF16)


def _head_sums(x, exact):
    ones_pair = _head_block_ones(LANES)
    tiles = []
    for p in range(x.shape[1] // LANES):
        xt = x[:, p * LANES:(p + 1) * LANES]
        tiles.append(_exact_right(xt, ones_pair) if exact else _dot(xt.astype(BF16), ones_pair))
    return jnp.concatenate(tiles, axis=1)


def _stack_heads(x, m0):
    z = jnp.zeros_like(x)
    return jnp.concatenate([jnp.where(m0, x, z), jnp.where(m0, z, x)], axis=0)


_MM_L4 = _mm1_nt
_MM_KT = _mm1_nt
_MM_SQ = _mm1
_MM_AP = _mm1
_MM_V = _mm1
_MM_Y = _mm1
_MM_UPD = _mm1


def _rwkv_chunk_maps(streams, c, n_sub):
    assert c == 64
    c2 = 2 * c
    lane = lax.broadcasted_iota(I32, (c, LANES), 1)
    m0 = lane < HEAD_DIM
    r_i = lax.broadcasted_iota(I32, (c2, c2), 0)
    c_i = lax.broadcasted_iota(I32, (c2, c2), 1)
    eye = (r_i == c_i).astype(F32)
    rel = r_i % c - c_i % c
    masks = {sg: (rel * sg > 0, rel * sg >= 0) for sg in {s["sign"] for s in streams}}
    items = [(j, q) for j in range(len(streams)) for q in range(n_sub)]

    def part(j, q, name):
        return _stack_heads(streams[j][name][q * c:(q + 1) * c], m0)

    lhs = {it: jnp.concatenate([part(*it, "kkp"), part(*it, "rp")], axis=0) for it in items}
    rhs = {it: jnp.concatenate([part(*it, "ki"), part(*it, "bi")], axis=0) for it in items}
    vs = {it: part(*it, "v") for it in items}
    kipcs = {it: part(*it, "kipc") for it in items}
    bipcs = {it: part(*it, "bipc") for it in items}
    l4 = {it: _MM_L4(lhs[it], rhs[it]) for it in items}
    m_kk, n1, m_rk, m_rb = {}, {}, {}, {}
    for it in items:
        strict, incl = masks[streams[it[0]]["sign"]]
        m = l4[it]
        m_kk[it] = jnp.where(strict, m[0:c2, 0:c2], 0.0)
        n1[it] = jnp.where(strict, m[0:c2, c2:2 * c2], 0.0)
        m_rk[it] = jnp.where(incl, m[c2:2 * c2, 0:c2], 0.0)
        m_rb[it] = jnp.where(incl, m[c2:2 * c2, c2:2 * c2], 0.0)
    n2 = {it: _MM_SQ(n1[it], n1[it]) for it in items}
    n4 = {it: _MM_SQ(n2[it], n2[it]) for it in items}
    n8 = {it: _MM_SQ(n4[it], n4[it]) for it in items}
    n16 = {it: _MM_SQ(n8[it], n8[it]) for it in items}
    n32 = {it: _MM_SQ(n16[it], n16[it]) for it in items}
    p1 = {it: (eye - n1[it]) + _MM_AP(eye - n1[it], n2[it]) for it in items}
    p2 = {it: eye + n4[it] + n8[it] + _MM_AP(n4[it], n8[it]) for it in items}
    p3 = {it: eye + n16[it] + n32[it] + _MM_AP(n16[it], n32[it]) for it in items}
    p23 = {it: _MM_AP(p2[it], p3[it]) for it in items}
    winv = {it: _MM_AP(p1[it], p23[it]) for it in items}
    mv = {it: _MM_V(m_kk[it], vs[it]) for it in items}
    mrv = {it: _MM_Y(m_rk[it], vs[it]) for it in items}
    wl = {it: _MM_AP(winv[it], lhs[it][0:c2]) for it in items}
    wmv = {it: _MM_AP(winv[it], mv[it]) for it in items}
    yl = {it: lhs[it][c2:2 * c2] - _MM_Y(m_rb[it], wl[it]) for it in items}
    y0 = {it: mrv[it] - _MM_Y(m_rb[it], wmv[it]) for it in items}
    g2 = {it: _MM_UPD(jnp.transpose(wl[it]), bipcs[it]) for it in items}
    hh = {it: _MM_UPD(jnp.transpose(jnp.concatenate([vs[it], -wmv[it]], axis=0)),
                      jnp.concatenate([kipcs[it], bipcs[it]], axis=0)) for it in items}
    return {it: (yl[it], y0[it], g2[it], hh[it]) for it in items}


def _rwkv_apply_maps(streams, maps, c, n_sub):
    c2 = 2 * c
    st = [s["st"] for s in streams]
    ys = {}
    for k in range(n_sub):
        cur = [(j, k if s["sign"] > 0 else n_sub - 1 - k) for j, s in enumerate(streams)]
        sg = [_MM_UPD(st[j], maps[j][q][2]) for j, q in cur]
        yk = [_MM_KT(maps[j][q][0], st[j]) + maps[j][q][1] for j, q in cur]
        for it, yi in zip(cur, yk):
            ys[it] = yi[0:c] + yi[c:c2]
        st = [st[j] * streams[j]["pc"][q] - sgi + maps[j][q][3] for (j, q), sgi in zip(cur, sg)]
    return [(jnp.concatenate([ys[j, q] for q in range(n_sub)], axis=0), st[j]) for j in range(len(streams))]


def _softplus(z):
    return jnp.maximum(z, 0.0) + jnp.log(1.0 + jnp.exp(-jnp.abs(z)))


def _rwkv_tile_prep(x_ref, tile, seq_len, width, sign, w0, a0, w_wa, k_k, k_a, r_k):
    cs = RWKV_CHUNK
    c = x_ref.shape[0]
    valid = jnp.minimum(c, seq_len - tile * c)
    row = lax.broadcasted_iota(I32, (c, LANES), 0)
    rowv = row < valid
    lane = lax.broadcasted_iota(I32, (c, LANES), 1)
    ones_pair = _head_block_ones(LANES)

    def shifted(lo):
        return jnp.where(rowv, x_ref[:, lo:lo + LANES], 0.0)

    wa = shifted(3 * width)
    g_lo = shifted(3 * width + LANES)
    xwa = jnp.where(lane < LANES // 2, jnp.tanh(wa), wa)
    la = _dot(xwa.astype(BF16), w_wa)

    t_i = lax.broadcasted_iota(I32, (c, c), 0)
    s_i = lax.broadcasted_iota(I32, (c, c), 1)
    tri = ((t_i // cs == s_i // cs) & ((t_i - s_i) * sign >= 0)).astype(BF16)

    pairs = []
    for p in range(width // LANES):
        lo = p * LANES
        cols = slice(lo, lo + LANES)
        r = shifted(lo)
        k = shifted(width + lo)
        v = shifted(2 * width + lo)
        w_log = -_softplus(-(w0[:, cols] + la[:, cols])) - 0.5
        logw = jnp.where(rowv, -jnp.exp(w_log), 0.0)
        a = jax.nn.sigmoid(a0[:, cols] + la[:, width + lo:width + lo + LANES])
        kk0 = k * k_k[:, cols]
        ss = _dot((kk0 * kk0).astype(BF16), ones_pair)
        kk = kk0 / jnp.maximum(jnp.sqrt(ss), 1e-12)
        kdir = k * (1.0 + (a - 1.0) * k_a[:, cols])
        b = kk * a
        cl = _exact_left(tri, logw)
        lasts = [cl[q * cs + cs - 1:q * cs + cs, :] if sign > 0 else cl[q * cs:q * cs + 1, :]
                 for q in range(c // cs)]
        last = jnp.concatenate([jnp.broadcast_to(lq, (cs, LANES)) for lq in lasts], axis=0)
        e_n = jnp.exp(-cl)
        pcr = jnp.exp(last - cl)
        pairs.append(dict(kkp=kk * jnp.exp(cl - logw), rp=r * jnp.exp(cl), ki=kdir * e_n, bi=b * e_n,
                          kipc=kdir * pcr, bipc=b * pcr, v=v, pc=[jnp.exp(lq) for lq in lasts],
                          bonus=_dot((r * kdir * r_k[:, cols]).astype(BF16), ones_pair) * v))
    return pairs, g_lo


def _rwkv_scan_kernel(xf_ref, xb_ref, w0_ref, a0_ref, wwa_ref,
                      gup_ref, kk_ref, ka_ref, rk_ref, yf_ref, yb_ref, bonf_ref, bonb_ref, g_ref, st_ref,
                      *, seq_len, width):
    i = pl.program_id(1)
    n_chunks = pl.num_programs(1)
    n_pairs = width // LANES

    @pl.when(i == 0)
    def _():
        st_ref[...] = jnp.zeros_like(st_ref)

    tail = (kk_ref[...], ka_ref[...], rk_ref[...])
    n_sub = xf_ref.shape[0] // RWKV_CHUNK
    fwd, g_lo = _rwkv_tile_prep(xf_ref, i, seq_len, width, 1, w0_ref[0], a0_ref[0], wwa_ref[0], *tail)
    bwd, _ = _rwkv_tile_prep(xb_ref, n_chunks - 1 - i, seq_len, width, -1, w0_ref[1], a0_ref[1], wwa_ref[1], *tail)
    g_ref[...] = _mm1(jax.nn.sigmoid(g_lo), gup_ref[...]).astype(g_ref.dtype)
    streams = []
    for di, (pairs, sign, bon_ref) in enumerate(((fwd, 1, bonf_ref), (bwd, -1, bonb_ref))):
        for p, s in enumerate(pairs):
            bon_ref[:, p * LANES:(p + 1) * LANES] = s.pop("bonus")
            s["st"] = st_ref[di * n_pairs + p]
            s["sign"] = sign
            streams.append(s)
    m = _rwkv_chunk_maps(streams, RWKV_CHUNK, n_sub)
    maps = [{q: m[j, q] for q in range(n_sub)} for j in range(len(streams))]
    res = _rwkv_apply_maps(streams, maps, RWKV_CHUNK, n_sub)
    for j, (y, st_new) in enumerate(res):
        di, p = divmod(j, n_pairs)
        cols = slice(p * LANES, (p + 1) * LANES)
        (yf_ref if di == 0 else yb_ref)[:, cols] = y
        st_ref[j] = st_new


def _rwkv_readout(y_f, y_b, bon_f, bon_b, gate, lnx_g, lnx_b):
    y = y_f + y_b
    mean = _head_sums(y, exact=True) * (1.0 / HEAD_DIM)
    yc = y - mean
    var = _head_sums(yc * yc, exact=True) * (1.0 / HEAD_DIM)
    yn = yc * lax.rsqrt(var + RWKV_GN_EPS) * lnx_g + lnx_b
    return (yn + bon_f + bon_b) * gate.astype(F32)


def rwkv_mix(rest, w0, w_up, a0, a_up, g_up, k_k, k_a, r_k):
    bsz, l, n_cols = rest.shape
    width = w0.shape[1]
    rank = w_up.shape[1]
    assert n_cols == 3 * width + 2 * LANES and 2 * rank == LANES
    c = RWKV_CHUNK * RWKV_TILE_CHUNKS
    n_chunks = -(-l // c)
    zeros = jnp.zeros((2, rank, width), F32)
    w_wa = jnp.concatenate([jnp.concatenate([w_up.astype(F32), zeros], axis=2),
                            jnp.concatenate([zeros, a_up.astype(F32)], axis=2)], axis=1)
    w_wa = w_wa.astype(BF16)

    fwd_chunk = lambda i: i
    bwd_chunk = lambda i: n_chunks - 1 - i

    def tile_specs(chunk_of):
        return [pl.BlockSpec((None, c, n_cols), lambda b, i: (b, chunk_of(i), 0))]

    row2 = lambda a: a.astype(F32).reshape(1, -1)
    whole = lambda *shape: pl.BlockSpec(shape, lambda b, i: (0,) * len(shape))
    out_spec = lambda chunk_of: pl.BlockSpec((None, c, width), lambda b, i: (b, chunk_of(i), 0))
    act = lambda dt: jax.ShapeDtypeStruct((bsz, l, width), dt)
    y_f, y_b, bon_f, bon_b, g = pl.pallas_call(
        functools.partial(_rwkv_scan_kernel, seq_len=l, width=width),
        grid=(bsz, n_chunks),
        in_specs=tile_specs(fwd_chunk) + tile_specs(bwd_chunk) + [
            whole(2, 1, width), whole(2, 1, width),
            whole(2, LANES, 2 * width),
            whole(LANES, width),
            whole(1, width), whole(1, width), whole(1, width),
        ],
        out_specs=[out_spec(fwd_chunk), out_spec(bwd_chunk), out_spec(fwd_chunk), out_spec(bwd_chunk),
                   out_spec(fwd_chunk)],
        out_shape=[act(F32), act(F32), act(F32), act(F32), act(BF16)],
        scratch_shapes=[pltpu.VMEM((2 * (width // LANES), LANES, LANES), F32)],
        compiler_params=_cparams("parallel", "arbitrary"),
        name="rwkv_scan",
    )(rest, rest, w0.astype(F32).reshape(2, 1, width),
      a0.astype(F32).reshape(2, 1, width), w_wa, g_up.astype(BF16), row2(k_k), row2(k_a), row2(r_k))

    return tuple(a.reshape(bsz * l, width) for a in (y_f, y_b, bon_f, bon_b, g))


S5_CHUNK = 16


def _cpow(n, lr, li, step):
    mag = jnp.exp(n * (lr * step))
    ang = n * (li * step)
    return mag * jnp.cos(ang), mag * jnp.sin(ang)


def _s5_param_kernel(lamr_ref, stepr_ref, bt_ref, ct_ref, kmat_ref, wst_ref, cexp_ref, alpha_ref):
    t_len = S5_CHUNK
    n_i = S5_GROUP_CH
    p2 = 2 * S5_STATE
    ti = t_len * n_i

    lr = lamr_ref[0:1, :]
    li = lamr_ref[1:2, :]
    step = jnp.exp(stepr_ref[...])
    ab_re, ab_im = _cpow(1.0, lr, li, step)
    den = lr * lr + li * li
    z_re = ((ab_re - 1.0) * lr + ab_im * li) / den
    z_im = (ab_im * lr - (ab_re - 1.0) * li) / den
    t16 = lax.broadcasted_iota(I32, (t_len, p2), 0).astype(F32)
    is_f = lax.broadcasted_iota(I32, (t_len, p2), 1) < S5_STATE

    def rows_by_t(x):
        return jnp.concatenate([jnp.broadcast_to(x[t:t + 1], (n_i, p2)) for t in range(t_len)], axis=0)

    def tiled_rows(x):
        return jnp.concatenate([x] * t_len, axis=0)

    def pow_rows(n):
        q_re, q_im = _cpow(n, lr, li, step)
        return rows_by_t(q_re), rows_by_t(q_im)

    bt_re = tiled_rows(bt_ref[0])
    bt_im = tiled_rows(bt_ref[1])
    bb_re = z_re * bt_re - z_im * bt_im
    bb_im = z_re * bt_im + z_im * bt_re
    pw_re, pw_im = pow_rows(jnp.where(is_f, (t_len - 1.0) - t16, t16))
    wst_ref[:, 0:p2] = (pw_re * bb_re - pw_im * bb_im).astype(wst_ref.dtype)
    wst_ref[:, p2:2 * p2] = (pw_re * bb_im + pw_im * bb_re).astype(wst_ref.dtype)
    al_re, al_im = _cpow(float(t_len), lr, li, step)
    alpha_ref[0:1, :] = al_re
    alpha_ref[1:2, :] = al_im

    ct_re = tiled_rows(ct_ref[0])
    ct_im = tiled_rows(ct_ref[1])

    def c_times_pow(n):
        q_re, q_im = pow_rows(n)
        return jnp.transpose(ct_re * q_re - ct_im * q_im), jnp.transpose(ct_re * q_im + ct_im * q_re)

    ca_re, ca_im = c_times_pow(jnp.where(is_f, t16, jnp.where(t16 == 0.0, 0.0, t_len - t16)))
    lane_p = lax.broadcasted_iota(I32, (n_i, p2), 1)
    bbr = bb_re[0:n_i]
    bbi = bb_im[0:n_i]
    zero = jnp.zeros_like(bbr)
    strips = []
    for sel in (lane_p < S5_STATE, lane_p >= S5_STATE):
        strips.append(_mm3(jnp.where(sel, bbr, zero), ca_re) - _mm3(jnp.where(sel, bbi, zero), ca_im))
    strip_f, strip_b = strips
    t_k = lax.broadcasted_iota(I32, (n_i, ti), 1) // n_i
    for tt in range(t_len):
        sf = strip_f if tt == 0 else pltpu.roll(strip_f, tt * n_i, 1)
        sb = strip_b if tt == 0 else pltpu.roll(strip_b, tt * n_i, 1)
        blk = jnp.where(t_k >= tt, sf, 0.0) + jnp.where(t_k <= tt, sb, 0.0)
        kmat_ref[tt * n_i:(tt + 1) * n_i, :] = blk.astype(kmat_ref.dtype)

    co_re, co_im = c_times_pow(jnp.where(is_f, t16 + 1.0, t_len - t16))
    cexp_ref[0:p2, :] = co_re.astype(cexp_ref.dtype)
    cexp_ref[p2:2 * p2, :] = (-co_im).astype(cexp_ref.dtype)


def _s5_main_kernel(u_ref, kmat_ref, wst_ref, cexp_ref, alpha_ref, y_ref, x_ref, sf_ref, sb_ref,
                    *, n_batch, n_chunks):
    p2 = 2 * S5_STATE
    n_gb = u_ref.shape[0]
    for g in range(n_gb):
        x_ref[g] = _dot(u_ref[g].astype(BF16), wst_ref[g])
    lane = lax.broadcasted_iota(I32, (1, p2), 1)
    is_f = lane < S5_STATE
    alphas = [(alpha_ref[g, 0:1, :], alpha_ref[g, 1:2, :]) for g in range(n_gb)]

    sub = S5_SCAN_ROWS
    assert n_chunks % sub == 0
    chains = [(g, b) for g in range(n_gb) for b in range(n_batch)]

    def step(k, carry):
        new = []
        for (g, b), (s_re, s_im) in zip(chains, carry):
            a_re, a_im = alphas[g]
            row_f = pl.multiple_of(b * n_chunks + sub * k, sub)
            row_b = pl.multiple_of(b * n_chunks + (n_chunks - sub) - sub * k, sub)
            xf = x_ref[g, pl.ds(row_f, sub), :]
            xb = x_ref[g, pl.ds(row_b, sub), :]
            seen = []
            for r in range(sub):
                seen.append(jnp.concatenate([s_re, s_im], axis=1))
                rb = sub - 1 - r
                x_re = jnp.where(is_f, xf[r:r + 1, 0:p2], xb[rb:rb + 1, 0:p2])
                x_im = jnp.where(is_f, xf[r:r + 1, p2:2 * p2], xb[rb:rb + 1, p2:2 * p2])
                s_re, s_im = a_re * s_re - a_im * s_im + x_re, a_re * s_im + a_im * s_re + x_im
            sf_ref[g, pl.ds(row_f, sub), :] = jnp.concatenate(seen, axis=0)
            sb_ref[g, pl.ds(row_b, sub), :] = jnp.concatenate(seen[::-1], axis=0)
            new.append((s_re, s_im))
        return tuple(new)

    zero = jnp.zeros((1, p2), F32)
    lax.fori_loop(0, n_chunks // sub, step, tuple((zero, zero) for _ in chains))
    lane2 = lax.broadcasted_iota(I32, sf_ref.shape[1:], 1) % p2
    for g in range(n_gb):
        s_in = jnp.where(lane2 < S5_STATE, sf_ref[g], sb_ref[g])
        s_hi, s_lo = _split_bf16(s_in)
        y_ref[g] = (_dot(u_ref[g].astype(BF16), kmat_ref[g]) + _dot(s_hi, cexp_ref[g]) + _dot(s_lo, cexp_ref[g]))


S5_RELAYOUT_CHUNKS = 128
S5_SCAN_ROWS = 8
S5_GROUPS_PER_STEP = 4


def _s5_group_major_kernel(h_ref, g_ref, u_ref, hn_ref, ut_ref, *, seq_len):
    n_g, mt, ti = u_ref.shape
    t_len = S5_CHUNK
    n_i = ti // t_len
    n_lt = hn_ref.shape[0]
    g_lt = LANES // n_i
    rows = h_ref.shape[0]
    valid = seq_len - pl.program_id(1) * rows
    row = lax.broadcasted_iota(I32, h_ref.shape, 0)
    hn = jnp.where(row < valid, _rms(h_ref[...], g_ref[...]), 0.0)
    for j in range(n_lt):
        hn_ref[j] = hn[:, j * LANES:(j + 1) * LANES]
    for tau in range(t_len):
        for j in range(n_lt):
            xt = jnp.transpose(hn_ref[j, pl.ds(tau, mt, stride=t_len), :])
            ut_ref[j * g_lt:(j + 1) * g_lt, tau * n_i:(tau + 1) * n_i, :] = xt.reshape(g_lt, n_i, mt)
    for g in range(n_g):
        u_ref[g] = jnp.transpose(ut_ref[g]).astype(u_ref.dtype)


def _s5_token_major_kernel(y_ref, o_ref, zt_ref, z_ref):
    n_g, mt, ti = y_ref.shape
    t_len = S5_CHUNK
    n_i = ti // t_len
    n_lt = z_ref.shape[0]
    for g in range(n_g):
        yt = jnp.transpose(y_ref[g])
        zt_ref[:, g * n_i:(g + 1) * n_i, :] = yt.reshape(t_len, n_i, mt)
    for t in range(t_len):
        for j in range(n_lt):
            z_ref[j, pl.ds(t, mt, stride=t_len), :] = jnp.transpose(zt_ref[t, j * LANES:(j + 1) * LANES, :])
    for j in range(n_lt):
        o_ref[:, j * LANES:(j + 1) * LANES] = z_ref[j]


def _gelu_tanh(x):
    return 0.5 * x * (1.0 + jnp.tanh(math.sqrt(2.0 / math.pi) * (x + 0.044715 * (x * x * x))))


def _s5_glu_kernel(h_ref, y_ref, g_ref, d_ref, w_ref, o_ref):
    h = h_ref[...]
    dm = h.shape[1]
    y = y_ref[...] + d_ref[...] * _rms(h, g_ref[...])
    gl = _gelu_tanh(y).astype(BF16)
    a = _dot(gl, w_ref[:, 0:dm])
    b = _dot(gl, w_ref[:, dm:2 * dm])
    o_ref[...] = h + a * jax.nn.sigmoid(b)


def s5_mix(h3, gain, b_re, b_im, lam_re, lam_im, log_step, c_re, c_im, d_skip, w_glu):
    bsz, l, dm = h3.shape
    n_g, n_p, n_i = b_re.shape
    t_len = S5_CHUNK
    assert l % t_len == 0 and n_g * n_i == dm and n_p == S5_STATE and n_i == S5_GROUP_CH
    n_chunks = -(-(l // t_len) // S5_SCAN_ROWS) * S5_SCAN_ROWS
    m = bsz * n_chunks
    ti = t_len * n_i
    p2 = 2 * n_p
    n = bsz * l
    tm = _row_tile(n, 608)
    h2 = h3.reshape(n, dm)
    gain2 = gain.astype(F32).reshape(1, dm)

    mt = min(S5_RELAYOUT_CHUNKS, n_chunks)
    n_tiles = -(-n_chunks // mt)
    u = pl.pallas_call(
        functools.partial(_s5_group_major_kernel, seq_len=l),
        grid=(bsz, n_tiles),
        in_specs=[pl.BlockSpec((None, mt * t_len, dm), lambda b, i: (b, i, 0)),
                  pl.BlockSpec((1, dm), lambda b, i: (0, 0))],
        out_specs=pl.BlockSpec((n_g, None, mt, ti), lambda b, i: (0, b, i, 0)),
        out_shape=jax.ShapeDtypeStruct((n_g, bsz, n_chunks, ti), BF16),
        scratch_shapes=[pltpu.VMEM((dm // LANES, mt * t_len, LANES), F32), pltpu.VMEM((n_g, ti, mt), F32)],
        compiler_params=_cparams("parallel", "parallel"),
        name="s5_group_major",
    )(h3, gain2).reshape(n_g, m, ti)

    f32 = lambda a: a.astype(F32)
    lam_r = jnp.stack([jnp.concatenate([f32(lam_re)[0], f32(lam_re)[1]], axis=-1),
                       jnp.concatenate([f32(lam_im)[0], f32(lam_im)[1]], axis=-1)], axis=1)
    step_r = jnp.repeat(jnp.transpose(f32(log_step))[:, None, :], n_p, axis=2)
    bt = jnp.stack([jnp.transpose(f32(b_re), (0, 2, 1)), jnp.transpose(f32(b_im), (0, 2, 1))], axis=1)
    bt = jnp.tile(bt, (1, 1, 1, 2))
    ct = jnp.stack([f32(c_re), f32(c_im)], axis=0)
    ct = jnp.transpose(ct, (2, 0, 3, 1, 4)).reshape(n_g, 2, n_i, p2)

    gspec = lambda *shape: pl.BlockSpec((None,) + shape, lambda g: (g,) + (0,) * len(shape))
    kmat, wst, cexp, alpha = pl.pallas_call(
        _s5_param_kernel,
        grid=(n_g,),
        in_specs=[gspec(2, p2), gspec(1, p2), gspec(2, n_i, p2), gspec(2, n_i, p2)],
        out_specs=[gspec(ti, ti), gspec(ti, 2 * p2), gspec(2 * p2, ti), gspec(2, p2)],
        out_shape=[
            jax.ShapeDtypeStruct((n_g, ti, ti), BF16),
            jax.ShapeDtypeStruct((n_g, ti, 2 * p2), BF16),
            jax.ShapeDtypeStruct((n_g, 2 * p2, ti), BF16),
            jax.ShapeDtypeStruct((n_g, 2, p2), F32),
        ],
        compiler_params=_cparams("parallel"),
        name="s5_params",
    )(lam_r, step_r, bt, ct)

    gb = S5_GROUPS_PER_STEP
    assert n_g % gb == 0
    gbspec = lambda *shape: pl.BlockSpec((gb,) + shape, lambda g: (g,) + (0,) * len(shape))
    y = pl.pallas_call(
        functools.partial(_s5_main_kernel, n_batch=bsz, n_chunks=n_chunks),
        grid=(n_g // gb,),
        in_specs=[gbspec(m, ti), gbspec(ti, ti), gbspec(ti, 2 * p2), gbspec(2 * p2, ti), gbspec(2, p2)],
        out_specs=gbspec(m, ti),
        out_shape=jax.ShapeDtypeStruct((n_g, m, ti), F32),
        scratch_shapes=[pltpu.VMEM((gb, m, 2 * p2), F32)] * 3,
        compiler_params=_cparams("parallel"),
        name="s5_main",
    )(u, kmat, wst, cexp, alpha)
    y2 = pl.pallas_call(
        _s5_token_major_kernel,
        grid=(bsz, n_tiles),
        in_specs=[pl.BlockSpec((n_g, None, mt, ti), lambda b, i: (0, b, i, 0))],
        out_specs=pl.BlockSpec((None, mt * t_len, dm), lambda b, i: (b, i, 0)),
        out_shape=jax.ShapeDtypeStruct((bsz, l, dm), F32),
        scratch_shapes=[pltpu.VMEM((t_len, dm, mt), F32), pltpu.VMEM((dm // LANES, mt * t_len, LANES), F32)],
        compiler_params=_cparams("parallel", "parallel"),
        name="s5_token_major",
    )(y.reshape(n_g, bsz, n_chunks, ti)).reshape(n, dm)

    out = pl.pallas_call(
        _s5_glu_kernel,
        grid=(n // tm,),
        in_specs=[
            pl.BlockSpec((tm, dm), lambda i: (i, 0)),
            pl.BlockSpec((tm, dm), lambda i: (i, 0)),
            pl.BlockSpec((1, dm), lambda i: (0, 0)),
            pl.BlockSpec((1, dm), lambda i: (0, 0)),
            pl.BlockSpec((dm, 2 * dm), lambda i: (0, 0)),
        ],
        out_specs=pl.BlockSpec((tm, dm), lambda i: (i, 0)),
        out_shape=jax.ShapeDtypeStruct((n, dm), F32),
        compiler_params=_cparams("parallel"),
        name="s5_glu",
    )(h2, y2, gain2, f32(d_skip).reshape(1, dm), w_glu.astype(BF16))
    return out.reshape(bsz, l, dm)


def na_rwkv_mix(h3, gain, w_in, w_out, rpb, mu, w0, w_up, a0, a_up, g_up, k_k, k_a, r_k, lnx_g, lnx_b):
    bsz, l, dm = h3.shape
    n = bsz * l
    h2 = h3.reshape(n, dm)
    n_qkv = 3 * (w_out.shape[0] // 2)
    qkv, rest = norm_inproj(h2, gain.astype(F32), w_in.astype(BF16), n_qkv, mu, l)
    na = na_attention(qkv.reshape(bsz, l, n_qkv), rpb)
    rw_parts = rwkv_mix(rest.reshape(bsz, l, -1), w0, w_up, a0, a_up, g_up, k_k, k_a, r_k)
    out = outproj_residual(h2, na.reshape(n, -1), rw_parts, lnx_g, lnx_b, w_out.astype(BF16))
    return out.reshape(bsz, l, dm)


def kernel(x, meta_tokens, norm_mix, norm_ffn, norm_final, mix_w_in, mix_w_out, na_rpb, rwkv_mu,
           rwkv_w0, rwkv_w_up, rwkv_a0, rwkv_a_up, rwkv_g_up, rwkv_k_k, rwkv_k_a, rwkv_r_k,
           rwkv_lnx_g, rwkv_lnx_b, s5_b_re, s5_b_im, s5_lambda_re, s5_lambda_im, s5_log_step,
           s5_c_re, s5_c_im, s5_d, s5_w_glu, moe_w_group, moe_b_group, moe_w_expert, moe_b_expert,
           moe_w1, moe_w3, moe_w2):
    bsz, _, dm = x.shape
    depth = norm_mix.shape[0]
    meta = jnp.broadcast_to(meta_tokens.astype(x.dtype)[None], (bsz,) + meta_tokens.shape)
    h = jnp.concatenate([meta, x], axis=1)
    l = h.shape[1]
    for layer in range(depth):
        i = layer // 2
        if layer % 2 == 0:
            h = na_rwkv_mix(h, norm_mix[layer], mix_w_in[i], mix_w_out[i], na_rpb[i], rwkv_mu[i], rwkv_w0[i],
                            rwkv_w_up[i], rwkv_a0[i], rwkv_a_up[i], rwkv_g_up[i], rwkv_k_k[i], rwkv_k_a[i],
                            rwkv_r_k[i], rwkv_lnx_g[i], rwkv_lnx_b[i])
        else:
            h = s5_mix(h, norm_mix[layer], s5_b_re[i], s5_b_im[i], s5_lambda_re[i], s5_lambda_im[i],
                       s5_log_step[i], s5_c_re[i], s5_c_im[i], s5_d[i], s5_w_glu[i])
        h = hierarchical_moe_residual(h.reshape(bsz * l, dm), norm_ffn[layer].astype(F32), moe_w_group[layer],
                                      moe_b_group[layer], moe_w_expert[layer], moe_b_expert[layer],
                                      moe_w1, moe_w3, moe_w2, layer).reshape(bsz, l, dm)
    return final_norm(h, norm_final.astype(F32))
```

```python
import functools
import math

import jax
import jax.numpy as jnp
from jax import lax
from jax.experimental import pallas as pl
from jax.experimental.pallas import tpu as pltpu

F32 = jnp.float32
BF16 = jnp.bfloat16
I32 = jnp.int32

N_META = 16
GRID_W = 64
HEAD_DIM = 64
NA_WIN_ROWS = 8
NA_WIN_COLS = 16
S5_GROUP_CH = 16
S5_STATE = 64
MOE_GROUPS = 4
MOE_PER_GROUP = 8
MOE_EXPERTS = MOE_GROUPS * MOE_PER_GROUP
NORM_EPS = 1e-6
RWKV_GN_EPS = 64e-5
NEG_INF = -1e30

LANES = 128
SUBLANES_BF16 = 16
VMEM_LIMIT_BYTES = 56 * 1024 * 1024

MOE_TILE = 512
MOE_TOKEN_TILE = 320
MOE_STAGE_SLOTS = 3
ROUTER_LANES = 128


def _cparams(*sem):
    return pltpu.CompilerParams(dimension_semantics=sem, vmem_limit_bytes=VMEM_LIMIT_BYTES)


def _row_tile(n, target):
    best = None
    for t in range(SUBLANES_BF16, min(n, target) + 1, SUBLANES_BF16):
        if n % t == 0:
            best = t
    assert best is not None, (n, target)
    return best


def _rms(x, gain):
    ms = jnp.mean(x * x, axis=-1, keepdims=True)
    return (x * lax.rsqrt(ms + NORM_EPS)) * gain


def _split_bf16(x):
    hi = x.astype(BF16)
    lo = (x - hi.astype(F32)).astype(BF16)
    return hi, lo


def _dot(a, b):
    return jnp.dot(a, b, preferred_element_type=F32)


def _dot_nt(a, b):
    return lax.dot_general(a, b, (((1,), (1,)), ((), ())), preferred_element_type=F32)


SHIFT_HALO = SUBLANES_BF16


def _norm_inproj_kernel(h_ref, hp_ref, hn_ref, g_ref, w_ref, mu_ref, qkv_ref, rest_ref, xn_ref, y_ref,
                        *, n_qkv, chunk, seq_len):
    tm = h_ref.shape[0]
    hl = SHIFT_HALO
    gain = g_ref[...]
    xn_ref[...] = jnp.concatenate([_rms(hp_ref[...], gain), _rms(h_ref[...], gain), _rms(hn_ref[...], gain)],
                                  axis=0).astype(BF16)
    t = (pl.program_id(0) * tm + lax.broadcasted_iota(I32, (tm, chunk), 0)) % seq_len
    has_prev = t > 0
    has_next = t < seq_len - 1
    n_all = w_ref.shape[1]
    for c in range(0, n_all, chunk):
        if c < n_qkv:
            qkv_ref[:, c:c + chunk] = _dot(xn_ref[hl:hl + tm, :], w_ref[:, c:c + chunk]).astype(BF16)
        else:
            y_ref[...] = _dot(xn_ref[...], w_ref[:, c:c + chunk])
            p = y_ref[hl:hl + tm, :]
            nb = 0.5 * (jnp.where(has_prev, y_ref[hl - 1:hl - 1 + tm, :], 0.0)
                        + jnp.where(has_next, y_ref[hl + 1:hl + 1 + tm, :], 0.0))
            rest_ref[:, c - n_qkv:c - n_qkv + chunk] = p + mu_ref[:, c - n_qkv:c - n_qkv + chunk] * (nb - p)


def norm_inproj(h2, gain, w_bf16, n_qkv, mu, seq_len):
    n, d = h2.shape
    n_all = w_bf16.shape[1]
    tm = _row_tile(n, 608)
    chunk = 256
    hl = SHIFT_HALO
    assert n_qkv % chunk == 0 and n_all % chunk == 0 and tm % hl == 0 and n % seq_len == 0
    per = tm // hl
    n_halo = n // hl
    return pl.pallas_call(
        functools.partial(_norm_inproj_kernel, n_qkv=n_qkv, chunk=chunk, seq_len=seq_len),
        grid=(n // tm,),
        in_specs=[
            pl.BlockSpec((tm, d), lambda i: (i, 0)),
            pl.BlockSpec((hl, d), lambda i: (jnp.maximum(i * per - 1, 0), 0)),
            pl.BlockSpec((hl, d), lambda i: (jnp.minimum((i + 1) * per, n_halo - 1), 0)),
            pl.BlockSpec((1, d), lambda i: (0, 0)),
            pl.BlockSpec((d, n_all), lambda i: (0, 0)),
            pl.BlockSpec((1, n_all - n_qkv), lambda i: (0, 0)),
        ],
        out_specs=[
            pl.BlockSpec((tm, n_qkv), lambda i: (i, 0)),
            pl.BlockSpec((tm, n_all - n_qkv), lambda i: (i, 0)),
        ],
        out_shape=[
            jax.ShapeDtypeStruct((n, n_qkv), BF16),
            jax.ShapeDtypeStruct((n, n_all - n_qkv), F32),
        ],
        scratch_shapes=[pltpu.VMEM((tm + 2 * hl, d), BF16), pltpu.VMEM((tm + 2 * hl, chunk), F32)],
        compiler_params=_cparams("parallel"),
        name="norm_inproj",
    )(h2, h2, h2, gain.reshape(1, d), w_bf16, mu.astype(F32).reshape(1, n_all - n_qkv))


def _outproj_kernel(h_ref, na_ref, yf_ref, yb_ref, bonf_ref, bonb_ref, g_ref, lg_ref, lb_ref, wa_ref, wb_ref, o_ref):
    rw = _rwkv_readout(yf_ref[...], yb_ref[...], bonf_ref[...], bonb_ref[...], g_ref[...], lg_ref[...], lb_ref[...])
    acc = _dot(na_ref[...], wa_ref[...])
    acc = acc + _dot(rw.astype(BF16), wb_ref[...])
    o_ref[...] = h_ref[...] + acc


def outproj_residual(h2, na, rwkv_parts, lnx_g, lnx_b, w_out_bf16):
    n, d = h2.shape
    ka, kb = na.shape[1], rwkv_parts[0].shape[1]
    tm = _row_tile(n, 608)
    rows = lambda w: pl.BlockSpec((tm, w), lambda i: (i, 0))
    whole = lambda r, c: pl.BlockSpec((r, c), lambda i: (0, 0))
    return pl.pallas_call(
        _outproj_kernel,
        grid=(n // tm,),
        in_specs=[rows(d), rows(ka)] + [rows(kb)] * 5 + [whole(1, kb), whole(1, kb), whole(ka, d), whole(kb, d)],
        out_specs=rows(d),
        out_shape=jax.ShapeDtypeStruct((n, d), F32),
        compiler_params=_cparams("parallel"),
        name="outproj_residual",
    )(h2, na, *rwkv_parts, lnx_g.astype(F32).reshape(1, kb), lnx_b.astype(F32).reshape(1, kb),
      w_out_bf16[:ka], w_out_bf16[ka:])


def _store_token_tiles(ref, x):
    rows = x.shape[0]
    s_n = x.shape[1] // LANES
    for s in range(s_n):
        ref[pl.ds(s, rows, stride=s_n), :] = x[:, s * LANES:(s + 1) * LANES]


def _load_token_tile_cols(ref, s, rows, s_n):
    return ref[pl.ds(s, rows, stride=s_n), :]


def _router_kernel(h_ref, g_ref, whi_ref, wlo_ref, b_ref, xn_ref, route_ref):
    xn = _rms(h_ref[...], g_ref[...])
    x_hi, x_lo = _split_bf16(xn)
    _store_token_tiles(xn_ref, xn)
    logits = (_dot(x_hi, whi_ref[...]) + _dot(x_hi, wlo_ref[...]) + _dot(x_lo, whi_ref[...])
              + b_ref[...])
    tm = logits.shape[0]
    lane = lax.broadcasted_iota(I32, (tm, ROUTER_LANES), 1)
    big = jnp.int32(ROUTER_LANES)

    is_g = lane < MOE_GROUPS
    lg = jnp.where(is_g, logits, -jnp.inf)
    eg = jnp.where(is_g, jnp.exp(lg - jnp.max(lg, axis=-1, keepdims=True)), 0.0)
    pg = eg / jnp.sum(eg, axis=-1, keepdims=True)
    p_grp = jnp.max(pg, axis=-1, keepdims=True)
    grp = jnp.min(jnp.where(is_g & (pg == p_grp), lane, big), axis=-1, keepdims=True)

    lo_lane = MOE_GROUPS + MOE_PER_GROUP * grp
    is_e = (lane >= lo_lane) & (lane < lo_lane + MOE_PER_GROUP)
    le = jnp.where(is_e, logits, -jnp.inf)
    ee = jnp.where(is_e, jnp.exp(le - jnp.max(le, axis=-1, keepdims=True)), 0.0)
    pe = jnp.where(is_e, ee / jnp.sum(ee, axis=-1, keepdims=True), -1.0)
    p1 = jnp.max(pe, axis=-1, keepdims=True)
    i1 = jnp.min(jnp.where(pe == p1, lane, big), axis=-1, keepdims=True)
    pe2 = jnp.where(lane == i1, -1.0, pe)
    p2 = jnp.max(pe2, axis=-1, keepdims=True)
    i2 = jnp.min(jnp.where(pe2 == p2, lane, big), axis=-1, keepdims=True)
    denom = p1 + p2
    g1 = p_grp * p1 / denom
    g2 = p_grp * p2 / denom
    e1 = (i1 - MOE_GROUPS).astype(F32)
    e2 = (i2 - MOE_GROUPS).astype(F32)
    route_ref[...] = jnp.where(lane == 0, e1, jnp.where(lane == 1, e2, jnp.where(lane == 2, g1, g2)))


def moe_router(h2, gain, w_group, b_group, w_expert, b_expert):
    n, d = h2.shape
    n_r = MOE_GROUPS + MOE_EXPERTS
    w_r = jnp.concatenate([w_group, jnp.transpose(w_expert, (1, 0, 2)).reshape(d, MOE_EXPERTS)], axis=1)
    w_r = jnp.pad(w_r.astype(F32), ((0, 0), (0, ROUTER_LANES - n_r)))
    w_hi, w_lo = _split_bf16(w_r)
    b_r = jnp.pad(jnp.concatenate([b_group, b_expert.reshape(-1)]).astype(F32), (0, ROUTER_LANES - n_r))
    tm = _row_tile(n, 608)
    return pl.pallas_call(
        _router_kernel,
        grid=(n // tm,),
        in_specs=[
            pl.BlockSpec((tm, d), lambda i: (i, 0)),
            pl.BlockSpec((1, d), lambda i: (0, 0)),
            pl.BlockSpec((d, ROUTER_LANES), lambda i: (0, 0)),
            pl.BlockSpec((d, ROUTER_LANES), lambda i: (0, 0)),
            pl.BlockSpec((1, ROUTER_LANES), lambda i: (0, 0)),
        ],
        out_specs=[
            pl.BlockSpec((tm * (d // LANES), LANES), lambda i: (i, 0)),
            pl.BlockSpec((tm, ROUTER_LANES), lambda i: (i, 0)),
        ],
        out_shape=[
            jax.ShapeDtypeStruct((n * (d // LANES), LANES), F32),
            jax.ShapeDtypeStruct((n, ROUTER_LANES), F32),
        ],
        compiler_params=_cparams("parallel"),
        name="moe_router",
    )(h2, gain.reshape(1, d), w_hi, w_lo, b_r.reshape(1, ROUTER_LANES))


def _moe_dispatch_kernel(tail_ref, n_used_ref, dst_ref, xn_hbm, xbuf_hbm, zero_ref, stage_ref, sem, lsem, zsem,
                         *, tm, s_n, n_blocks):
    i = pl.program_id(0)
    n_steps = pl.num_programs(0)
    tile_rows = zero_ref.shape[0]

    def zero_block(b, carry):
        pltpu.make_async_copy(zero_ref, xbuf_hbm.at[pl.ds(pl.multiple_of(b * tile_rows, tile_rows), tile_rows), :],
                              zsem).start()
        return carry

    def wait_zero_block(b, carry):
        pltpu.make_async_copy(zero_ref, xbuf_hbm.at[pl.ds(0, tile_rows), :], zsem).wait()
        return carry

    n_slots = stage_ref.shape[0]
    slot = i % n_slots
    nxt = (i + 1) % n_slots

    def load(step, s):
        return pltpu.make_async_copy(xn_hbm.at[pl.ds(pl.multiple_of(step * tm * s_n, s_n), tm * s_n), :],
                                     stage_ref.at[s], lsem.at[s])

    def wait_rows_out(s):
        for _ in range(2):
            pltpu.make_async_copy(stage_ref.at[s], xbuf_hbm.at[pl.ds(0, tm * s_n), :], sem.at[s]).wait()

    @pl.when(i == 0)
    def _():
        load(0, 0).start()
        zero_ref[...] = jnp.zeros_like(zero_ref)
        for e in range(tail_ref.shape[0]):
            @pl.when(tail_ref[e] >= 0)
            def _():
                pltpu.make_async_copy(zero_ref, xbuf_hbm.at[pl.ds(pl.multiple_of(tail_ref[e], s_n), tile_rows), :],
                                      zsem).start()
        lax.fori_loop(n_used_ref[0], n_blocks, zero_block, 0)
        for e in range(tail_ref.shape[0]):
            @pl.when(tail_ref[e] >= 0)
            def _():
                wait_zero_block(0, 0)
        lax.fori_loop(n_used_ref[0], n_blocks, wait_zero_block, 0)

    @pl.when(i >= n_slots - 1)
    def _():
        wait_rows_out(nxt)

    @pl.when(i + 1 < n_steps)
    def _():
        load(i + 1, nxt).start()

    load(i, slot).wait()
    for r in range(tm):
        src = stage_ref.at[slot, pl.ds(r * s_n, s_n), :]
        for k in range(2):
            dst = pl.multiple_of(dst_ref[k, r], s_n)
            pltpu.make_async_copy(src, xbuf_hbm.at[pl.ds(dst, s_n), :], sem.at[slot]).start(priority=k)

    @pl.when(i == n_steps - 1)
    def _():
        for back in range(n_slots - 1):
            @pl.when(i >= back)
            def _():
                wait_rows_out((i - back) % n_slots)


def moe_dispatch(xn_tiles, dst_tiles, tail_start, n_used, n_blocks, tm, s_n):
    n_steps = dst_tiles.shape[0]
    n_rows = n_blocks * MOE_TILE
    assert n_steps * tm * s_n == xn_tiles.shape[0]
    grid_spec = pltpu.PrefetchScalarGridSpec(
        num_scalar_prefetch=2,
        grid=(n_steps,),
        in_specs=[
            pl.BlockSpec((None, 2, tm), lambda i, tail, nu: (i, 0, 0), memory_space=pltpu.SMEM),
            pl.BlockSpec(memory_space=pl.ANY),
        ],
        out_specs=pl.BlockSpec(memory_space=pl.ANY),
        scratch_shapes=[
            pltpu.VMEM((MOE_TILE * s_n, LANES), F32),
            pltpu.VMEM((MOE_STAGE_SLOTS, tm * s_n, LANES), F32),
            pltpu.SemaphoreType.DMA((MOE_STAGE_SLOTS,)),
            pltpu.SemaphoreType.DMA((MOE_STAGE_SLOTS,)),
            pltpu.SemaphoreType.DMA(()),
        ],
    )
    return pl.pallas_call(
        functools.partial(_moe_dispatch_kernel, tm=tm, s_n=s_n, n_blocks=n_blocks),
        grid_spec=grid_spec,
        out_shape=jax.ShapeDtypeStruct((n_rows * s_n, LANES), F32),
        compiler_params=_cparams("arbitrary"),
        name="moe_dispatch",
    )(tail_start, n_used, dst_tiles, xn_tiles)


def _expert_kernel(blk_e_ref, n_used_ref, x_ref, w1_ref, w3_ref, w2_ref, y_ref, xb_ref, w1b_ref, w3b_ref, w2b_ref):
    i = pl.program_id(0)
    used = i < n_used_ref[0]
    tile, d = xb_ref.shape
    s_n = d // LANES
    prev_e = blk_e_ref[jnp.maximum(i - 1, 0)]
    fresh = (i == 0) | (blk_e_ref[i] != prev_e)

    @pl.when(used & fresh)
    def _():
        w1b_ref[...] = w1_ref[...].astype(BF16)
        w3b_ref[...] = w3_ref[...].astype(BF16)
        w2b_ref[...] = w2_ref[...].astype(BF16)

    @pl.when(used)
    def _():
        for s in range(s_n):
            xb_ref[:, s * LANES:(s + 1) * LANES] = _load_token_tile_cols(x_ref, s, tile, s_n).astype(BF16)
        x = xb_ref[...]
        a = _dot(x, w1b_ref[...])
        b = _dot(x, w3b_ref[...])
        hmid = (a * jax.nn.sigmoid(a) * b).astype(BF16)
        _store_token_tiles(y_ref, _dot(hmid, w2b_ref[...]))

    @pl.when(jnp.logical_not(used))
    def _():
        y_ref[...] = jnp.zeros_like(y_ref)


def moe_experts(xbuf, blk_e, n_used, w1, w3, w2, layer, n_blocks):
    d, f = w1.shape[2], w1.shape[3]
    s_n = d // LANES
    tile = MOE_TILE

    def w_map(i, blk_e_ref, n_used_ref):
        return (layer, blk_e_ref[i], 0, 0)

    def x_map(i, blk_e_ref, n_used_ref):
        return (jnp.minimum(i, jnp.maximum(n_used_ref[0] - 1, 0)), 0)

    grid_spec = pltpu.PrefetchScalarGridSpec(
        num_scalar_prefetch=2,
        grid=(n_blocks,),
        in_specs=[
            pl.BlockSpec((tile * s_n, LANES), x_map),
            pl.BlockSpec((None, None, d, f), w_map),
            pl.BlockSpec((None, None, d, f), w_map),
            pl.BlockSpec((None, None, f, d), w_map),
        ],
        out_specs=pl.BlockSpec((tile * s_n, LANES), lambda i, be, nu: (i, 0)),
        scratch_shapes=[
            pltpu.VMEM((tile, d), BF16),
            pltpu.VMEM((d, f), BF16),
            pltpu.VMEM((d, f), BF16),
            pltpu.VMEM((f, d), BF16),
        ],
    )
    return pl.pallas_call(
        _expert_kernel,
        grid_spec=grid_spec,
        out_shape=jax.ShapeDtypeStruct((n_blocks * tile * s_n, LANES), F32),
        compiler_params=_cparams("arbitrary"),
        name="moe_experts",
    )(blk_e, n_used, xbuf, w1, w3, w2)


def _moe_combine_kernel(src_ref, src_next_ref, h_ref, route_ref, y_hbm, o_ref, yg_ref, sem, *, n_steps):
    i = pl.program_id(0)
    slot = i & 1
    tm, d = h_ref.shape
    s_n = d // LANES

    def start_gather(ids_ref, dst_slot):
        for k in range(2):
            for r in range(tm):
                src = pl.multiple_of(ids_ref[k, r], s_n)
                pltpu.make_async_copy(y_hbm.at[pl.ds(src, s_n), :],
                                      yg_ref.at[dst_slot, k, pl.ds(r * s_n, s_n), :],
                                      sem.at[dst_slot]).start(priority=r % 2)

    @pl.when(i == 0)
    def _():
        start_gather(src_ref, 0)

    @pl.when(i + 1 < n_steps)
    def _():
        start_gather(src_next_ref, 1 - slot)

    for k in range(2):
        pltpu.make_async_copy(y_hbm.at[pl.ds(0, tm * s_n), :], yg_ref.at[slot, k], sem.at[slot]).wait()
    route = route_ref[...]
    g1 = route[:, 2:3]
    g2 = route[:, 3:4]
    for s in range(s_n):
        cols = slice(s * LANES, (s + 1) * LANES)
        o_ref[:, cols] = (h_ref[:, cols] + g1 * _load_token_tile_cols(yg_ref.at[slot, 0], s, tm, s_n)
                          + g2 * _load_token_tile_cols(yg_ref.at[slot, 1], s, tm, s_n))


def moe_combine(h2, route, y_tiles, src_tiles, tm):
    n, d = h2.shape
    s_n = d // LANES
    n_steps = src_tiles.shape[0]
    ids = lambda index: pl.BlockSpec((None, 2, tm), lambda i: (index(i), 0, 0), memory_space=pltpu.SMEM)
    return pl.pallas_call(
        functools.partial(_moe_combine_kernel, n_steps=n_steps),
        grid=(n_steps,),
        in_specs=[
            ids(lambda i: i),
            ids(lambda i: jnp.minimum(i + 1, n_steps - 1)),
            pl.BlockSpec((tm, d), lambda i: (i, 0)),
            pl.BlockSpec((tm, ROUTER_LANES), lambda i: (i, 0)),
            pl.BlockSpec(memory_space=pl.ANY),
        ],
        out_specs=pl.BlockSpec((tm, d), lambda i: (i, 0)),
        out_shape=jax.ShapeDtypeStruct((n, d), F32),
        scratch_shapes=[pltpu.VMEM((2, 2, tm * s_n, LANES), F32), pltpu.SemaphoreType.DMA((2,))],
        compiler_params=_cparams("arbitrary"),
        name="moe_combine",
    )(src_tiles, src_tiles, h2, route, y_tiles)


def hierarchical_moe_residual(h2, gain, w_group, b_group, w_expert, b_expert, w1, w3, w2, layer):
    n, d = h2.shape
    xn, route = moe_router(h2, gain, w_group, b_group, w_expert, b_expert)
    e_km = jnp.concatenate([route[:, 0], route[:, 1]]).astype(I32)
    n_assign = 2 * n
    onehot = (e_km[:, None] == jnp.arange(MOE_EXPERTS, dtype=I32)[None, :]).astype(I32)
    csum = jnp.cumsum(onehot, axis=0)
    counts = csum[-1]
    padded = (counts + MOE_TILE - 1) // MOE_TILE * MOE_TILE
    pad_end = jnp.cumsum(padded)
    pad_start = pad_end - padded
    dest = jnp.sum((csum - onehot + pad_start[None, :]) * onehot, axis=1)
    n_blocks = -(-n_assign // MOE_TILE) + MOE_EXPERTS
    blk_start = jnp.arange(n_blocks, dtype=I32) * MOE_TILE
    blk_e = jnp.minimum(jnp.sum((pad_end[None, :] <= blk_start[:, None]).astype(I32), axis=1),
                        MOE_EXPERTS - 1).astype(I32)
    n_used = (pad_end[-1] // MOE_TILE).astype(I32).reshape(1)
    s_n = d // LANES
    tm = _row_tile(n, MOE_TOKEN_TILE)
    dest_tiles = jnp.transpose((dest * s_n).astype(I32).reshape(2, n // tm, tm), (1, 0, 2))
    tail_start = jnp.where(counts > 0, (pad_end - MOE_TILE) * s_n, -1).astype(I32)
    xbuf = moe_dispatch(xn, dest_tiles, tail_start, n_used, n_blocks, tm, s_n)
    y = moe_experts(xbuf, blk_e, n_used, w1, w3, w2, layer, n_blocks)
    return moe_combine(h2, route, y, dest_tiles, tm)


def _final_norm_kernel(h_ref, g_ref, o_ref):
    o_ref[...] = _rms(h_ref[...], g_ref[...])


def final_norm(h3, gain):
    b, l, d = h3.shape
    t = l - N_META
    tm = _row_tile(t, 512)
    return pl.pallas_call(
        _final_norm_kernel,
        grid=(b, t // tm),
        in_specs=[
            pl.BlockSpec((None, pl.Element(tm), pl.Element(d)),
                         lambda bi, i: (bi, pl.multiple_of(N_META + i * tm, SUBLANES_BF16), 0)),
            pl.BlockSpec((1, d), lambda bi, i: (0, 0)),
        ],
        out_specs=pl.BlockSpec((None, tm, d), lambda bi, i: (bi, i, 0)),
        out_shape=jax.ShapeDtypeStruct((b, t, d), F32),
        compiler_params=_cparams("parallel", "parallel"),
        name="final_norm",
    )(h3, gain.reshape(1, d))


NA_QROWS = 8
NA_KROWS = 3 * NA_QROWS
NA_ROWS_PER_ITER = 4


def _na_kernel(q_ref, kw_ref, vw_ref, qm_ref, km_ref, vm_ref, bias_ref, o_ref, om_ref, *, rows, scale):
    blk = pl.program_id(1)
    tq = GRID_W
    n_pairs = q_ref.shape[1] // LANES
    base = jnp.clip(NA_QROWS * blk - NA_QROWS, 0, rows - NA_KROWS)
    lane = lax.broadcasted_iota(I32, (tq, LANES), 1)
    halves = [lane < HEAD_DIM, lane >= HEAD_DIM]

    pad = jnp.zeros((LANES - N_META, LANES), km_ref.dtype)
    k_meta = [jnp.concatenate([km_ref[:, p * LANES:(p + 1) * LANES], pad], axis=0) for p in range(n_pairs)]
    v_meta = [jnp.concatenate([vm_ref[:, p * LANES:(p + 1) * LANES], pad], axis=0) for p in range(n_pairs)]
    lane2 = lax.broadcasted_iota(I32, (2 * tq, LANES), 1)
    meta_bias2 = jnp.where(lane2 < N_META, 0.0, NEG_INF)

    n_win = NA_WIN_ROWS * GRID_W

    def row_body(jb, carry):
        colsl = [slice(p * LANES, (p + 1) * LANES) for p in range(n_pairs)]
        units = [(jj, p) for jj in range(NA_ROWS_PER_ITER) for p in range(n_pairs)]
        s_idx, koff, qoff = [], [], []
        for jj in range(NA_ROWS_PER_ITER):
            j = jb * NA_ROWS_PER_ITER + jj
            r = NA_QROWS * blk + j
            start = jnp.clip(r - NA_WIN_ROWS // 2, 0, rows - NA_WIN_ROWS)
            s_idx.append(start - r + (NA_WIN_ROWS - 1))
            koff.append(pl.multiple_of((start - base) * GRID_W, GRID_W))
            qoff.append(pl.multiple_of(j * tq, tq))
        q_pair = {(jj, p): q_ref[pl.ds(qoff[jj], tq), c]
                  for jj in range(NA_ROWS_PER_ITER) for p, c in enumerate(colsl)}
        k_ext = {(jj, p): jnp.concatenate([kw_ref[pl.ds(koff[jj], n_win), c], k_meta[p]], axis=0)
                 for jj in range(NA_ROWS_PER_ITER) for p, c in enumerate(colsl)}
        v_ext = {(jj, p): jnp.concatenate([vw_ref[pl.ds(koff[jj], n_win), c], v_meta[p]], axis=0)
                 for jj in range(NA_ROWS_PER_ITER) for p, c in enumerate(colsl)}
        qh = [_stack_heads(q_pair[jj, p], halves[0]) for jj, p in units]
        s = [_dot_nt(qh[u], k_ext[jj, p]) * scale + jnp.concatenate([bias_ref[p, s_idx[jj]], meta_bias2], axis=1)
             for u, (jj, p) in enumerate(units)]
        m = [jnp.max(x, axis=-1, keepdims=True) for x in s]
        e = [jnp.exp(x - mx) for x, mx in zip(s, m)]
        den = [jnp.sum(x, axis=-1, keepdims=True) for x in e]
        o = [_dot(e[u].astype(BF16), v_ext[jj, p]) / den[u] for u, (jj, p) in enumerate(units)]
        for u, (jj, p) in enumerate(units):
            o_ref[pl.ds(qoff[jj], tq), colsl[p]] = jnp.where(halves[0], o[u][0:tq], o[u][tq:2 * tq]).astype(o_ref.dtype)
        return carry

    lax.fori_loop(0, NA_QROWS // NA_ROWS_PER_ITER, row_body, 0)

    @pl.when(blk == 0)
    def _():
        lane_m = lax.broadcasted_iota(I32, (N_META, LANES), 1)
        for p in range(n_pairs):
            cols = slice(p * LANES, (p + 1) * LANES)
            q_pair = qm_ref[:, cols]
            kmp = km_ref[:, cols]
            vmp = vm_ref[:, cols]
            outs = []
            for hh in range(2):
                sel = (lane_m < HEAD_DIM) if hh == 0 else (lane_m >= HEAD_DIM)
                qp = jnp.where(sel, q_pair, jnp.zeros_like(q_pair))
                s_m = _dot_nt(qp, kmp) * scale
                p_m = jnp.exp(s_m - jnp.max(s_m, axis=-1, keepdims=True))
                den = jnp.sum(p_m, axis=-1, keepdims=True)
                outs.append(_dot(p_m.astype(BF16), vmp) / den)
            om_ref[:, cols] = jnp.where(lane_m < HEAD_DIM, outs[0], outs[1]).astype(om_ref.dtype)


def _na_bias_table(rpb):
    h = rpb.shape[0]
    c_ids = jnp.arange(GRID_W)
    c_start = jnp.clip(c_ids - NA_WIN_COLS // 2, 0, GRID_W - NA_WIN_COLS)
    in_band = (c_ids[None, :] >= c_start[:, None]) & (c_ids[None, :] < c_start[:, None] + NA_WIN_COLS)
    dc = jnp.clip(c_ids[None, :] - c_ids[:, None] + NA_WIN_COLS - 1, 0, 2 * NA_WIN_COLS - 2)
    tab = jnp.where(in_band[None, None], rpb.astype(F32)[:, :, dc], NEG_INF)
    win = jnp.stack([tab[:, s:s + NA_WIN_ROWS] for s in range(NA_WIN_ROWS)], axis=1)
    per_head = jnp.transpose(win, (0, 1, 3, 2, 4)).reshape(h // 2, 2, NA_WIN_ROWS, GRID_W, NA_WIN_ROWS * GRID_W)
    return jnp.transpose(per_head, (0, 2, 1, 3, 4)).reshape(h // 2, NA_WIN_ROWS, 2 * GRID_W, NA_WIN_ROWS * GRID_W)


def na_attention(qkv, rpb):
    b, l, w3 = qkv.shape
    w = w3 // 3
    t = l - N_META
    rows = t // GRID_W
    assert rows * GRID_W == t and rows % NA_QROWS == 0 and rows >= NA_KROWS
    tq = NA_QROWS * GRID_W
    tk = NA_KROWS * GRID_W
    bias = _na_bias_table(rpb)
    al = SUBLANES_BF16

    def q_map(bi, i):
        return (bi, pl.multiple_of(N_META + i * tq, al), 0)

    def kv_map(col):
        def f(bi, i):
            base = jnp.clip(NA_QROWS * i - NA_QROWS, 0, rows - NA_KROWS)
            return (bi, pl.multiple_of(N_META + base * GRID_W, al), col)
        return f

    def meta_map(col):
        return lambda bi, i: (bi, 0, col)

    el = pl.Element
    grid_out, meta_out = pl.pallas_call(
        functools.partial(_na_kernel, rows=rows, scale=HEAD_DIM ** -0.5),
        grid=(b, rows // NA_QROWS),
        in_specs=[
            pl.BlockSpec((None, el(tq), el(w)), q_map),
            pl.BlockSpec((None, el(tk), el(w)), kv_map(w)),
            pl.BlockSpec((None, el(tk), el(w)), kv_map(2 * w)),
            pl.BlockSpec((None, el(N_META), el(w)), meta_map(0)),
            pl.BlockSpec((None, el(N_META), el(w)), meta_map(w)),
            pl.BlockSpec((None, el(N_META), el(w)), meta_map(2 * w)),
            pl.BlockSpec(bias.shape, lambda bi, i: (0, 0, 0, 0)),
        ],
        out_specs=[
            pl.BlockSpec((None, tq, w), lambda bi, i: (bi, i, 0)),
            pl.BlockSpec((None, N_META, w), lambda bi, i: (bi, 0, 0)),
        ],
        out_shape=[
            jax.ShapeDtypeStruct((b, t, w), BF16),
            jax.ShapeDtypeStruct((b, N_META, w), BF16),
        ],
        compiler_params=_cparams("parallel", "arbitrary"),
        name="na_attention",
    )(qkv, qkv, qkv, qkv, qkv, qkv, bias)
    return jnp.concatenate([meta_out, grid_out], axis=1)


RWKV_CHUNK = 64
RWKV_TILE_CHUNKS = 2


def _split3_bf16(x):
    p1 = x.astype(BF16)
    r1 = x - p1.astype(F32)
    p2 = r1.astype(BF16)
    p3 = (r1 - p2.astype(F32)).astype(BF16)
    return p1, p2, p3


def _mm1(a, b):
    return _dot(a.astype(BF16), b.astype(BF16))


def _mm3(a, b):
    ah, al = _split_bf16(a)
    bh, bl = _split_bf16(b)
    return _dot(ah, bh) + _dot(ah, bl) + _dot(al, bh)


def _mm1_nt(a, b):
    return _dot_nt(a.astype(BF16), b.astype(BF16))


def _mm3_nt(a, b):
    ah, al = _split_bf16(a)
    bh, bl = _split_bf16(b)
    return _dot_nt(ah, bh) + _dot_nt(ah, bl) + _dot_nt(al, bh)


def _exact_left(mat_bf16, x):
    p1, p2, p3 = _split3_bf16(x)
    return _dot(mat_bf16, p1) + _dot(mat_bf16, p2) + _dot(mat_bf16, p3)


def _exact_right(x, mat_bf16):
    p1, p2, p3 = _split3_bf16(x)
    return _dot(p1, mat_bf16) + _dot(p2, mat_bf16) + _dot(p3, mat_bf16)


def _head_block_ones(width):
    ri = lax.broadcasted_iota(I32, (width, width), 0) // HEAD_DIM
    ci = lax.broadcasted_iota(I32, (width, width), 1) // HEAD_DIM
    return (ri == ci).astype(BF16)


def _head_sums(x, exact):
    ones_pair = _head_block_ones(LANES)
    tiles = []
    for p in range(x.shape[1] // LANES):
        xt = x[:, p * LANES:(p + 1) * LANES]
        tiles.append(_exact_right(xt, ones_pair) if exact else _dot(xt.astype(BF16), ones_pair))
    return jnp.concatenate(tiles, axis=1)


def _stack_heads(x, m0):
    z = jnp.zeros_like(x)
    return jnp.concatenate([jnp.where(m0, x, z), jnp.where(m0, z, x)], axis=0)


_MM_L4 = _mm1_nt
_MM_KT = _mm1_nt
_MM_SQ = _mm1
_MM_AP = _mm1
_MM_V = _mm1
_MM_Y = _mm1
_MM_UPD = _mm1


def _rwkv_chunk_maps(streams, c, n_sub):
    assert c == 64
    c2 = 2 * c
    lane = lax.broadcasted_iota(I32, (c, LANES), 1)
    m0 = lane < HEAD_DIM
    r_i = lax.broadcasted_iota(I32, (c2, c2), 0)
    c_i = lax.broadcasted_iota(I32, (c2, c2), 1)
    eye = (r_i == c_i).astype(F32)
    rel = r_i % c - c_i % c
    masks = {sg: (rel * sg > 0, rel * sg >= 0) for sg in {s["sign"] for s in streams}}
    items = [(j, q) for j in range(len(streams)) for q in range(n_sub)]

    def part(j, q, name):
        return _stack_heads(streams[j][name][q * c:(q + 1) * c], m0)

    lhs = {it: jnp.concatenate([part(*it, "kkp"), part(*it, "rp")], axis=0) for it in items}
    rhs = {it: jnp.concatenate([part(*it, "ki"), part(*it, "bi")], axis=0) for it in items}
    vs = {it: part(*it, "v") for it in items}
    kipcs = {it: part(*it, "kipc") for it in items}
    bipcs = {it: part(*it, "bipc") for it in items}
    l4 = {it: _MM_L4(lhs[it], rhs[it]) for it in items}
    m_kk, n1, m_rk, m_rb = {}, {}, {}, {}
    for it in items:
        strict, incl = masks[streams[it[0]]["sign"]]
        m = l4[it]
        m_kk[it] = jnp.where(strict, m[0:c2, 0:c2], 0.0)
        n1[it] = jnp.where(strict, m[0:c2, c2:2 * c2], 0.0)
        m_rk[it] = jnp.where(incl, m[c2:2 * c2, 0:c2], 0.0)
        m_rb[it] = jnp.where(incl, m[c2:2 * c2, c2:2 * c2], 0.0)
    n2 = {it: _MM_SQ(n1[it], n1[it]) for it in items}
    n4 = {it: _MM_SQ(n2[it], n2[it]) for it in items}
    n8 = {it: _MM_SQ(n4[it], n4[it]) for it in items}
    n16 = {it: _MM_SQ(n8[it], n8[it]) for it in items}
    n32 = {it: _MM_SQ(n16[it], n16[it]) for it in items}
    p1 = {it: (eye - n1[it]) + _MM_AP(eye - n1[it], n2[it]) for it in items}
    p2 = {it: eye + n4[it] + n8[it] + _MM_AP(n4[it], n8[it]) for it in items}
    p3 = {it: eye + n16[it] + n32[it] + _MM_AP(n16[it], n32[it]) for it in items}
    p23 = {it: _MM_AP(p2[it], p3[it]) for it in items}
    winv = {it: _MM_AP(p1[it], p23[it]) for it in items}
    mv = {it: _MM_V(m_kk[it], vs[it]) for it in items}
    mrv = {it: _MM_Y(m_rk[it], vs[it]) for it in items}
    wl = {it: _MM_AP(winv[it], lhs[it][0:c2]) for it in items}
    wmv = {it: _MM_AP(winv[it], mv[it]) for it in items}
    yl = {it: lhs[it][c2:2 * c2] - _MM_Y(m_rb[it], wl[it]) for it in items}
    y0 = {it: mrv[it] - _MM_Y(m_rb[it], wmv[it]) for it in items}
    g2 = {it: _MM_UPD(jnp.transpose(wl[it]), bipcs[it]) for it in items}
    hh = {it: _MM_UPD(jnp.transpose(jnp.concatenate([vs[it], -wmv[it]], axis=0)),
                      jnp.concatenate([kipcs[it], bipcs[it]], axis=0)) for it in items}
    return {it: (yl[it], y0[it], g2[it], hh[it]) for it in items}


def _rwkv_apply_maps(streams, maps, c, n_sub):
    c2 = 2 * c
    st = [s["st"] for s in streams]
    ys = {}
    for k in range(n_sub):
        cur = [(j, k if s["sign"] > 0 else n_sub - 1 - k) for j, s in enumerate(streams)]
        sg = [_MM_UPD(st[j], maps[j][q][2]) for j, q in cur]
        yk = [_MM_KT(maps[j][q][0], st[j]) + maps[j][q][1] for j, q in cur]
        for it, yi in zip(cur, yk):
            ys[it] = yi[0:c] + yi[c:c2]
        st = [st[j] * streams[j]["pc"][q] - sgi + maps[j][q][3] for (j, q), sgi in zip(cur, sg)]
    return [(jnp.concatenate([ys[j, q] for q in range(n_sub)], axis=0), st[j]) for j in range(len(streams))]


def _softplus(z):
    return jnp.maximum(z, 0.0) + jnp.log(1.0 + jnp.exp(-jnp.abs(z)))


def _rwkv_tile_prep(x_ref, tile, seq_len, width, sign, w0, a0, w_wa, k_k, k_a, r_k):
    cs = RWKV_CHUNK
    c = x_ref.shape[0]
    valid = jnp.minimum(c, seq_len - tile * c)
    row = lax.broadcasted_iota(I32, (c, LANES), 0)
    rowv = row < valid
    lane = lax.broadcasted_iota(I32, (c, LANES), 1)
    ones_pair = _head_block_ones(LANES)

    def shifted(lo):
        return jnp.where(rowv, x_ref[:, lo:lo + LANES], 0.0)

    wa = shifted(3 * width)
    g_lo = shifted(3 * width + LANES)
    xwa = jnp.where(lane < LANES // 2, jnp.tanh(wa), wa)
    la = _dot(xwa.astype(BF16), w_wa)

    t_i = lax.broadcasted_iota(I32, (c, c), 0)
    s_i = lax.broadcasted_iota(I32, (c, c), 1)
    tri = ((t_i // cs == s_i // cs) & ((t_i - s_i) * sign >= 0)).astype(BF16)

    pairs = []
    for p in range(width // LANES):
        lo = p * LANES
        cols = slice(lo, lo + LANES)
        r = shifted(lo)
        k = shifted(width + lo)
        v = shifted(2 * width + lo)
        w_log = -_softplus(-(w0[:, cols] + la[:, cols])) - 0.5
        logw = jnp.where(rowv, -jnp.exp(w_log), 0.0)
        a = jax.nn.sigmoid(a0[:, cols] + la[:, width + lo:width + lo + LANES])
        kk0 = k * k_k[:, cols]
        ss = _dot((kk0 * kk0).astype(BF16), ones_pair)
        kk = kk0 / jnp.maximum(jnp.sqrt(ss), 1e-12)
        kdir = k * (1.0 + (a - 1.0) * k_a[:, cols])
        b = kk * a
        cl = _exact_left(tri, logw)
        lasts = [cl[q * cs + cs - 1:q * cs + cs, :] if sign > 0 else cl[q * cs:q * cs + 1, :]
                 for q in range(c // cs)]
        last = jnp.concatenate([jnp.broadcast_to(lq, (cs, LANES)) for lq in lasts], axis=0)
        e_n = jnp.exp(-cl)
        pcr = jnp.exp(last - cl)
        pairs.append(dict(kkp=kk * jnp.exp(cl - logw), rp=r * jnp.exp(cl), ki=kdir * e_n, bi=b * e_n,
                          kipc=kdir * pcr, bipc=b * pcr, v=v, pc=[jnp.exp(lq) for lq in lasts],
                          bonus=_dot((r * kdir * r_k[:, cols]).astype(BF16), ones_pair) * v))
    return pairs, g_lo


def _rwkv_scan_kernel(xf_ref, xb_ref, w0_ref, a0_ref, wwa_ref,
                      gup_ref, kk_ref, ka_ref, rk_ref, yf_ref, yb_ref, bonf_ref, bonb_ref, g_ref, st_ref,
                      *, seq_len, width):
    i = pl.program_id(1)
    n_chunks = pl.num_programs(1)
    n_pairs = width // LANES

    @pl.when(i == 0)
    def _():
        st_ref[...] = jnp.zeros_like(st_ref)

    tail = (kk_ref[...], ka_ref[...], rk_ref[...])
    n_sub = xf_ref.shape[0] // RWKV_CHUNK
    fwd, g_lo = _rwkv_tile_prep(xf_ref, i, seq_len, width, 1, w0_ref[0], a0_ref[0], wwa_ref[0], *tail)
    bwd, _ = _rwkv_tile_prep(xb_ref, n_chunks - 1 - i, seq_len, width, -1, w0_ref[1], a0_ref[1], wwa_ref[1], *tail)
    g_ref[...] = _mm1(jax.nn.sigmoid(g_lo), gup_ref[...]).astype(g_ref.dtype)
    streams = []
    for di, (pairs, sign, bon_ref) in enumerate(((fwd, 1, bonf_ref), (bwd, -1, bonb_ref))):
        for p, s in enumerate(pairs):
            bon_ref[:, p * LANES:(p + 1) * LANES] = s.pop("bonus")
            s["st"] = st_ref[di * n_pairs + p]
            s["sign"] = sign
            streams.append(s)
    m = _rwkv_chunk_maps(streams, RWKV_CHUNK, n_sub)
    maps = [{q: m[j, q] for q in range(n_sub)} for j in range(len(streams))]
    res = _rwkv_apply_maps(streams, maps, RWKV_CHUNK, n_sub)
    for j, (y, st_new) in enumerate(res):
        di, p = divmod(j, n_pairs)
        cols = slice(p * LANES, (p + 1) * LANES)
        (yf_ref if di == 0 else yb_ref)[:, cols] = y
        st_ref[j] = st_new


def _rwkv_readout(y_f, y_b, bon_f, bon_b, gate, lnx_g, lnx_b):
    y = y_f + y_b
    mean = _head_sums(y, exact=True) * (1.0 / HEAD_DIM)
    yc = y - mean
    var = _head_sums(yc * yc, exact=True) * (1.0 / HEAD_DIM)
    yn = yc * lax.rsqrt(var + RWKV_GN_EPS) * lnx_g + lnx_b
    return (yn + bon_f + bon_b) * gate.astype(F32)


def rwkv_mix(rest, w0, w_up, a0, a_up, g_up, k_k, k_a, r_k):
    bsz, l, n_cols = rest.shape
    width = w0.shape[1]
    rank = w_up.shape[1]
    assert n_cols == 3 * width + 2 * LANES and 2 * rank == LANES
    c = RWKV_CHUNK * RWKV_TILE_CHUNKS
    n_chunks = -(-l // c)
    zeros = jnp.zeros((2, rank, width), F32)
    w_wa = jnp.concatenate([jnp.concatenate([w_up.astype(F32), zeros], axis=2),
                            jnp.concatenate([zeros, a_up.astype(F32)], axis=2)], axis=1)
    w_wa = w_wa.astype(BF16)

    fwd_chunk = lambda i: i
    bwd_chunk = lambda i: n_chunks - 1 - i

    def tile_specs(chunk_of):
        return [pl.BlockSpec((None, c, n_cols), lambda b, i: (b, chunk_of(i), 0))]

    row2 = lambda a: a.astype(F32).reshape(1, -1)
    whole = lambda *shape: pl.BlockSpec(shape, lambda b, i: (0,) * len(shape))
    out_spec = lambda chunk_of: pl.BlockSpec((None, c, width), lambda b, i: (b, chunk_of(i), 0))
    act = lambda dt: jax.ShapeDtypeStruct((bsz, l, width), dt)
    y_f, y_b, bon_f, bon_b, g = pl.pallas_call(
        functools.partial(_rwkv_scan_kernel, seq_len=l, width=width),
        grid=(bsz, n_chunks),
        in_specs=tile_specs(fwd_chunk) + tile_specs(bwd_chunk) + [
            whole(2, 1, width), whole(2, 1, width),
            whole(2, LANES, 2 * width),
            whole(LANES, width),
            whole(1, width), whole(1, width), whole(1, width),
        ],
        out_specs=[out_spec(fwd_chunk), out_spec(bwd_chunk), out_spec(fwd_chunk), out_spec(bwd_chunk),
                   out_spec(fwd_chunk)],
        out_shape=[act(F32), act(F32), act(F32), act(F32), act(BF16)],
        scratch_shapes=[pltpu.VMEM((2 * (width // LANES), LANES, LANES), F32)],
        compiler_params=_cparams("parallel", "arbitrary"),
        name="rwkv_scan",
    )(rest, rest, w0.astype(F32).reshape(2, 1, width),
      a0.astype(F32).reshape(2, 1, width), w_wa, g_up.astype(BF16), row2(k_k), row2(k_a), row2(r_k))

    return tuple(a.reshape(bsz * l, width) for a in (y_f, y_b, bon_f, bon_b, g))


S5_CHUNK = 16


def _cpow(n, lr, li, step):
    mag = jnp.exp(n * (lr * step))
    ang = n * (li * step)
    return mag * jnp.cos(ang), mag * jnp.sin(ang)


def _s5_param_kernel(lamr_ref, stepr_ref, bt_ref, ct_ref, kmat_ref, wst_ref, cexp_ref, alpha_ref):
    t_len = S5_CHUNK
    n_i = S5_GROUP_CH
    p2 = 2 * S5_STATE
    ti = t_len * n_i

    lr = lamr_ref[0:1, :]
    li = lamr_ref[1:2, :]
    step = jnp.exp(stepr_ref[...])
    ab_re, ab_im = _cpow(1.0, lr, li, step)
    den = lr * lr + li * li
    z_re = ((ab_re - 1.0) * lr + ab_im * li) / den
    z_im = (ab_im * lr - (ab_re - 1.0) * li) / den
    t16 = lax.broadcasted_iota(I32, (t_len, p2), 0).astype(F32)
    is_f = lax.broadcasted_iota(I32, (t_len, p2), 1) < S5_STATE

    def rows_by_t(x):
        return jnp.concatenate([jnp.broadcast_to(x[t:t + 1], (n_i, p2)) for t in range(t_len)], axis=0)

    def tiled_rows(x):
        return jnp.concatenate([x] * t_len, axis=0)

    def pow_rows(n):
        q_re, q_im = _cpow(n, lr, li, step)
        return rows_by_t(q_re), rows_by_t(q_im)

    bt_re = tiled_rows(bt_ref[0])
    bt_im = tiled_rows(bt_ref[1])
    bb_re = z_re * bt_re - z_im * bt_im
    bb_im = z_re * bt_im + z_im * bt_re
    pw_re, pw_im = pow_rows(jnp.where(is_f, (t_len - 1.0) - t16, t16))
    wst_ref[:, 0:p2] = (pw_re * bb_re - pw_im * bb_im).astype(wst_ref.dtype)
    wst_ref[:, p2:2 * p2] = (pw_re * bb_im + pw_im * bb_re).astype(wst_ref.dtype)
    al_re, al_im = _cpow(float(t_len), lr, li, step)
    alpha_ref[0:1, :] = al_re
    alpha_ref[1:2, :] = al_im

    ct_re = tiled_rows(ct_ref[0])
    ct_im = tiled_rows(ct_ref[1])

    def c_times_pow(n):
        q_re, q_im = pow_rows(n)
        return jnp.transpose(ct_re * q_re - ct_im * q_im), jnp.transpose(ct_re * q_im + ct_im * q_re)

    ca_re, ca_im = c_times_pow(jnp.where(is_f, t16, jnp.where(t16 == 0.0, 0.0, t_len - t16)))
    lane_p = lax.broadcasted_iota(I32, (n_i, p2), 1)
    bbr = bb_re[0:n_i]
    bbi = bb_im[0:n_i]
    zero = jnp.zeros_like(bbr)
    strips = []
    for sel in (lane_p < S5_STATE, lane_p >= S5_STATE):
        strips.append(_mm3(jnp.where(sel, bbr, zero), ca_re) - _mm3(jnp.where(sel, bbi, zero), ca_im))
    strip_f, strip_b = strips
    t_k = lax.broadcasted_iota(I32, (n_i, ti), 1) // n_i
    for tt in range(t_len):
        sf = strip_f if tt == 0 else pltpu.roll(strip_f, tt * n_i, 1)
        sb = strip_b if tt == 0 else pltpu.roll(strip_b, tt * n_i, 1)
        blk = jnp.where(t_k >= tt, sf, 0.0) + jnp.where(t_k <= tt, sb, 0.0)
        kmat_ref[tt * n_i:(tt + 1) * n_i, :] = blk.astype(kmat_ref.dtype)

    co_re, co_im = c_times_pow(jnp.where(is_f, t16 + 1.0, t_len - t16))
    cexp_ref[0:p2, :] = co_re.astype(cexp_ref.dtype)
    cexp_ref[p2:2 * p2, :] = (-co_im).astype(cexp_ref.dtype)


def _s5_main_kernel(u_ref, kmat_ref, wst_ref, cexp_ref, alpha_ref, y_ref, x_ref, sf_ref, sb_ref,
                    *, n_batch, n_chunks):
    p2 = 2 * S5_STATE
    n_gb = u_ref.shape[0]
    for g in range(n_gb):
        x_ref[g] = _dot(u_ref[g].astype(BF16), wst_ref[g])
    lane = lax.broadcasted_iota(I32, (1, p2), 1)
    is_f = lane < S5_STATE
    alphas = [(alpha_ref[g, 0:1, :], alpha_ref[g, 1:2, :]) for g in range(n_gb)]

    sub = S5_SCAN_ROWS
    assert n_chunks % sub == 0
    chains = [(g, b) for g in range(n_gb) for b in range(n_batch)]

    def step(k, carry):
        new = []
        for (g, b), (s_re, s_im) in zip(chains, carry):
            a_re, a_im = alphas[g]
            row_f = pl.multiple_of(b * n_chunks + sub * k, sub)
            row_b = pl.multiple_of(b * n_chunks + (n_chunks - sub) - sub * k, sub)
            xf = x_ref[g, pl.ds(row_f, sub), :]
            xb = x_ref[g, pl.ds(row_b, sub), :]
            seen = []
            for r in range(sub):
                seen.append(jnp.concatenate([s_re, s_im], axis=1))
                rb = sub - 1 - r
                x_re = jnp.where(is_f, xf[r:r + 1, 0:p2], xb[rb:rb + 1, 0:p2])
                x_im = jnp.where(is_f, xf[r:r + 1, p2:2 * p2], xb[rb:rb + 1, p2:2 * p2])
                s_re, s_im = a_re * s_re - a_im * s_im + x_re, a_re * s_im + a_im * s_re + x_im
            sf_ref[g, pl.ds(row_f, sub), :] = jnp.concatenate(seen, axis=0)
            sb_ref[g, pl.ds(row_b, sub), :] = jnp.concatenate(seen[::-1], axis=0)
            new.append((s_re, s_im))
        return tuple(new)

    zero = jnp.zeros((1, p2), F32)
    lax.fori_loop(0, n_chunks // sub, step, tuple((zero, zero) for _ in chains))
    lane2 = lax.broadcasted_iota(I32, sf_ref.shape[1:], 1) % p2
    for g in range(n_gb):
        s_in = jnp.where(lane2 < S5_STATE, sf_ref[g], sb_ref[g])
        s_hi, s_lo = _split_bf16(s_in)
        y_ref[g] = (_dot(u_ref[g].astype(BF16), kmat_ref[g]) + _dot(s_hi, cexp_ref[g]) + _dot(s_lo, cexp_ref[g]))


S5_RELAYOUT_CHUNKS = 128
S5_SCAN_ROWS = 8
S5_GROUPS_PER_STEP = 4


def _s5_group_major_kernel(h_ref, g_ref, u_ref, hn_ref, ut_ref, *, seq_len):
    n_g, mt, ti = u_ref.shape
    t_len = S5_CHUNK
    n_i = ti // t_len
    n_lt = hn_ref.shape[0]
    g_lt = LANES // n_i
    rows = h_ref.shape[0]
    valid = seq_len - pl.program_id(1) * rows
    row = lax.broadcasted_iota(I32, h_ref.shape, 0)
    hn = jnp.where(row < valid, _rms(h_ref[...], g_ref[...]), 0.0)
    for j in range(n_lt):
        hn_ref[j] = hn[:, j * LANES:(j + 1) * LANES]
    for tau in range(t_len):
        for j in range(n_lt):
            xt = jnp.transpose(hn_ref[j, pl.ds(tau, mt, stride=t_len), :])
            ut_ref[j * g_lt:(j + 1) * g_lt, tau * n_i:(tau + 1) * n_i, :] = xt.reshape(g_lt, n_i, mt)
    for g in range(n_g):
        u_ref[g] = jnp.transpose(ut_ref[g]).astype(u_ref.dtype)


def _s5_token_major_kernel(y_ref, o_ref, zt_ref, z_ref):
    n_g, mt, ti = y_ref.shape
    t_len = S5_CHUNK
    n_i = ti // t_len
    n_lt = z_ref.shape[0]
    for g in range(n_g):
        yt = jnp.transpose(y_ref[g])
        zt_ref[:, g * n_i:(g + 1) * n_i, :] = yt.reshape(t_len, n_i, mt)
    for t in range(t_len):
        for j in range(n_lt):
            z_ref[j, pl.ds(t, mt, stride=t_len), :] = jnp.transpose(zt_ref[t, j * LANES:(j + 1) * LANES, :])
    for j in range(n_lt):
        o_ref[:, j * LANES:(j + 1) * LANES] = z_ref[j]


def _gelu_tanh(x):
    return 0.5 * x * (1.0 + jnp.tanh(math.sqrt(2.0 / math.pi) * (x + 0.044715 * (x * x * x))))


def _s5_glu_kernel(h_ref, y_ref, g_ref, d_ref, w_ref, o_ref):
    h = h_ref[...]
    dm = h.shape[1]
    y = y_ref[...] + d_ref[...] * _rms(h, g_ref[...])
    gl = _gelu_tanh(y).astype(BF16)
    a = _dot(gl, w_ref[:, 0:dm])
    b = _dot(gl, w_ref[:, dm:2 * dm])
    o_ref[...] = h + a * jax.nn.sigmoid(b)


def s5_mix(h3, gain, b_re, b_im, lam_re, lam_im, log_step, c_re, c_im, d_skip, w_glu):
    bsz, l, dm = h3.shape
    n_g, n_p, n_i = b_re.shape
    t_len = S5_CHUNK
    assert l % t_len == 0 and n_g * n_i == dm and n_p == S5_STATE and n_i == S5_GROUP_CH
    n_chunks = -(-(l // t_len) // S5_SCAN_ROWS) * S5_SCAN_ROWS
    m = bsz * n_chunks
    ti = t_len * n_i
    p2 = 2 * n_p
    n = bsz * l
    tm = _row_tile(n, 608)
    h2 = h3.reshape(n, dm)
    gain2 = gain.astype(F32).reshape(1, dm)

    mt = min(S5_RELAYOUT_CHUNKS, n_chunks)
    n_tiles = -(-n_chunks // mt)
    u = pl.pallas_call(
        functools.partial(_s5_group_major_kernel, seq_len=l),
        grid=(bsz, n_tiles),
        in_specs=[pl.BlockSpec((None, mt * t_len, dm), lambda b, i: (b, i, 0)),
                  pl.BlockSpec((1, dm), lambda b, i: (0, 0))],
        out_specs=pl.BlockSpec((n_g, None, mt, ti), lambda b, i: (0, b, i, 0)),
        out_shape=jax.ShapeDtypeStruct((n_g, bsz, n_chunks, ti), BF16),
        scratch_shapes=[pltpu.VMEM((dm // LANES, mt * t_len, LANES), F32), pltpu.VMEM((n_g, ti, mt), F32)],
        compiler_params=_cparams("parallel", "parallel"),
        name="s5_group_major",
    )(h3, gain2).reshape(n_g, m, ti)

    f32 = lambda a: a.astype(F32)
    lam_r = jnp.stack([jnp.concatenate([f32(lam_re)[0], f32(lam_re)[1]], axis=-1),
                       jnp.concatenate([f32(lam_im)[0], f32(lam_im)[1]], axis=-1)], axis=1)
    step_r = jnp.repeat(jnp.transpose(f32(log_step))[:, None, :], n_p, axis=2)
    bt = jnp.stack([jnp.transpose(f32(b_re), (0, 2, 1)), jnp.transpose(f32(b_im), (0, 2, 1))], axis=1)
    bt = jnp.tile(bt, (1, 1, 1, 2))
    ct = jnp.stack([f32(c_re), f32(c_im)], axis=0)
    ct = jnp.transpose(ct, (2, 0, 3, 1, 4)).reshape(n_g, 2, n_i, p2)

    gspec = lambda *shape: pl.BlockSpec((None,) + shape, lambda g: (g,) + (0,) * len(shape))
    kmat, wst, cexp, alpha = pl.pallas_call(
        _s5_param_kernel,
        grid=(n_g,),
        in_specs=[gspec(2, p2), gspec(1, p2), gspec(2, n_i, p2), gspec(2, n_i, p2)],
        out_specs=[gspec(ti, ti), gspec(ti, 2 * p2), gspec(2 * p2, ti), gspec(2, p2)],
        out_shape=[
            jax.ShapeDtypeStruct((n_g, ti, ti), BF16),
            jax.ShapeDtypeStruct((n_g, ti, 2 * p2), BF16),
            jax.ShapeDtypeStruct((n_g, 2 * p2, ti), BF16),
            jax.ShapeDtypeStruct((n_g, 2, p2), F32),
        ],
        compiler_params=_cparams("parallel"),
        name="s5_params",
    )(lam_r, step_r, bt, ct)

    gb = S5_GROUPS_PER_STEP
    assert n_g % gb == 0
    gbspec = lambda *shape: pl.BlockSpec((gb,) + shape, lambda g: (g,) + (0,) * len(shape))
    y = pl.pallas_call(
        functools.partial(_s5_main_kernel, n_batch=bsz, n_chunks=n_chunks),
        grid=(n_g // gb,),
        in_specs=[gbspec(m, ti), gbspec(ti, ti), gbspec(ti, 2 * p2), gbspec(2 * p2, ti), gbspec(2, p2)],
        out_specs=gbspec(m, ti),
        out_shape=jax.ShapeDtypeStruct((n_g, m, ti), F32),
        scratch_shapes=[pltpu.VMEM((gb, m, 2 * p2), F32)] * 3,
        compiler_params=_cparams("parallel"),
        name="s5_main",
    )(u, kmat, wst, cexp, alpha)
    y2 = pl.pallas_call(
        _s5_token_major_kernel,
        grid=(bsz, n_tiles),
        in_specs=[pl.BlockSpec((n_g, None, mt, ti), lambda b, i: (0, b, i, 0))],
        out_specs=pl.BlockSpec((None, mt * t_len, dm), lambda b, i: (b, i, 0)),
        out_shape=jax.ShapeDtypeStruct((bsz, l, dm), F32),
        scratch_shapes=[pltpu.VMEM((t_len, dm, mt), F32), pltpu.VMEM((dm // LANES, mt * t_len, LANES), F32)],
        compiler_params=_cparams("parallel", "parallel"),
        name="s5_token_major",
    )(y.reshape(n_g, bsz, n_chunks, ti)).reshape(n, dm)

    out = pl.pallas_call(
        _s5_glu_kernel,
        grid=(n // tm,),
        in_specs=[
            pl.BlockSpec((tm, dm), lambda i: (i, 0)),
            pl.BlockSpec((tm, dm), lambda i: (i, 0)),
            pl.BlockSpec((1, dm), lambda i: (0, 0)),
            pl.BlockSpec((1, dm), lambda i: (0, 0)),
            pl.BlockSpec((dm, 2 * dm), lambda i: (0, 0)),
        ],
        out_specs=pl.BlockSpec((tm, dm), lambda i: (i, 0)),
        out_shape=jax.ShapeDtypeStruct((n, dm), F32),
        compiler_params=_cparams("parallel"),
        name="s5_glu",
    )(h2, y2, gain2, f32(d_skip).reshape(1, dm), w_glu.astype(BF16))
    return out.reshape(bsz, l, dm)


def na_rwkv_mix(h3, gain, w_in, w_out, rpb, mu, w0, w_up, a0, a_up, g_up, k_k, k_a, r_k, lnx_g, lnx_b):
    bsz, l, dm = h3.shape
    n = bsz * l
    h2 = h3.reshape(n, dm)
    n_qkv = 3 * (w_out.shape[0] // 2)
    qkv, rest = norm_inproj(h2, gain.astype(F32), w_in.astype(BF16), n_qkv, mu, l)
    na = na_attention(qkv.reshape(bsz, l, n_qkv), rpb)
    rw_parts = rwkv_mix(rest.reshape(bsz, l, -1), w0, w_up, a0, a_up, g_up, k_k, k_a, r_k)
    out = outproj_residual(h2, na.reshape(n, -1), rw_parts, lnx_g, lnx_b, w_out.astype(BF16))
    return out.reshape(bsz, l, dm)


def kernel(x, meta_tokens, norm_mix, norm_ffn, norm_final, mix_w_in, mix_w_out, na_rpb, rwkv_mu,
           rwkv_w0, rwkv_w_up, rwkv_a0, rwkv_a_up, rwkv_g_up, rwkv_k_k, rwkv_k_a, rwkv_r_k,
           rwkv_lnx_g, rwkv_lnx_b, s5_b_re, s5_b_im, s5_lambda_re, s5_lambda_im, s5_log_step,
           s5_c_re, s5_c_im, s5_d, s5_w_glu, moe_w_group, moe_b_group, moe_w_expert, moe_b_expert,
           moe_w1, moe_w3, moe_w2):
    bsz, _, dm = x.shape
    depth = norm_mix.shape[0]
    meta = jnp.broadcast_to(meta_tokens.astype(x.dtype)[None], (bsz,) + meta_tokens.shape)
    h = jnp.concatenate([meta, x], axis=1)
    l = h.shape[1]
    for layer in range(depth):
        i = layer // 2
        if layer % 2 == 0:
            h = na_rwkv_mix(h, norm_mix[layer], mix_w_in[i], mix_w_out[i], na_rpb[i], rwkv_mu[i], rwkv_w0[i],
                            rwkv_w_up[i], rwkv_a0[i], rwkv_a_up[i], rwkv_g_up[i], rwkv_k_k[i], rwkv_k_a[i],
                            rwkv_r_k[i], rwkv_lnx_g[i], rwkv_lnx_b[i])
        else:
            h = s5_mix(h, norm_mix[layer], s5_b_re[i], s5_b_im[i], s5_lambda_re[i], s5_lambda_im[i],
                       s5_log_step[i], s5_c_re[i], s5_c_im[i], s5_d[i], s5_w_glu[i])
        h = hierarchical_moe_residual(h.reshape(bsz * l, dm), norm_ffn[layer].astype(F32), moe_w_group[layer],
                                      moe_b_group[layer], moe_w_expert[layer], moe_b_expert[layer],
                                      moe_w1, moe_w3, moe_w2, layer).reshape(bsz, l, dm)
    return final_norm(h, norm_final.astype(F32))
```

```python
import functools
import math

import jax
import jax.numpy as jnp
from jax import lax
from jax.experimental import pallas as pl
from jax.experimental.pallas import tpu as pltpu

F32 = jnp.float32
BF16 = jnp.bfloat16
I32 = jnp.int32

N_META = 16
GRID_W = 64
HEAD_DIM = 64
NA_WIN_ROWS = 8
NA_WIN_COLS = 16
S5_GROUP_CH = 16
S5_STATE = 64
MOE_GROUPS = 4
MOE_PER_GROUP = 8
MOE_EXPERTS = MOE_GROUPS * MOE_PER_GROUP
NORM_EPS = 1e-6
RWKV_GN_EPS = 64e-5
NEG_INF = -1e30

LANES = 128
SUBLANES_BF16 = 16
VMEM_LIMIT_BYTES = 56 * 1024 * 1024

MOE_TILE = 256
MOE_TOKEN_TILE = 608
MOE_STAGE_SLOTS = 3
ROUTER_LANES = 128


def _cparams(*sem):
    return pltpu.CompilerParams(dimension_semantics=sem, vmem_limit_bytes=VMEM_LIMIT_BYTES)


def _row_tile(n, target):
    best = None
    for t in range(SUBLANES_BF16, min(n, target) + 1, SUBLANES_BF16):
        if n % t == 0:
            best = t
    assert best is not None, (n, target)
    return best


def _rms(x, gain):
    ms = jnp.mean(x * x, axis=-1, keepdims=True)
    return (x * lax.rsqrt(ms + NORM_EPS)) * gain


def _split_bf16(x):
    hi = x.astype(BF16)
    lo = (x - hi.astype(F32)).astype(BF16)
    return hi, lo


def _dot(a, b):
    return jnp.dot(a, b, preferred_element_type=F32)


def _dot_nt(a, b):
    return lax.dot_general(a, b, (((1,), (1,)), ((), ())), preferred_element_type=F32)


SHIFT_HALO = SUBLANES_BF16


def _norm_inproj_kernel(h_ref, hp_ref, hn_ref, g_ref, w_ref, mu_ref, qkv_ref, rest_ref, xn_ref, y_ref,
                        *, n_qkv, chunk, seq_len):
    tm = h_ref.shape[0]
    hl = SHIFT_HALO
    gain = g_ref[...]
    xn_ref[...] = jnp.concatenate([_rms(hp_ref[...], gain), _rms(h_ref[...], gain), _rms(hn_ref[...], gain)],
                                  axis=0).astype(BF16)
    t = (pl.program_id(0) * tm + lax.broadcasted_iota(I32, (tm, chunk), 0)) % seq_len
    has_prev = t > 0
    has_next = t < seq_len - 1
    n_all = w_ref.shape[1]
    for c in range(0, n_all, chunk):
        if c < n_qkv:
            qkv_ref[:, c:c + chunk] = _dot(xn_ref[hl:hl + tm, :], w_ref[:, c:c + chunk]).astype(BF16)
        else:
            y_ref[...] = _dot(xn_ref[...], w_ref[:, c:c + chunk])
            p = y_ref[hl:hl + tm, :]
            nb = 0.5 * (jnp.where(has_prev, y_ref[hl - 1:hl - 1 + tm, :], 0.0)
                        + jnp.where(has_next, y_ref[hl + 1:hl + 1 + tm, :], 0.0))
            rest_ref[:, c - n_qkv:c - n_qkv + chunk] = p + mu_ref[:, c - n_qkv:c - n_qkv + chunk] * (nb - p)


def norm_inproj(h2, gain, w_bf16, n_qkv, mu, seq_len):
    n, d = h2.shape
    n_all = w_bf16.shape[1]
    tm = _row_tile(n, 608)
    chunk = 256
    hl = SHIFT_HALO
    assert n_qkv % chunk == 0 and n_all % chunk == 0 and tm % hl == 0 and n % seq_len == 0
    per = tm // hl
    n_halo = n // hl
    return pl.pallas_call(
        functools.partial(_norm_inproj_kernel, n_qkv=n_qkv, chunk=chunk, seq_len=seq_len),
        grid=(n // tm,),
        in_specs=[
            pl.BlockSpec((tm, d), lambda i: (i, 0)),
            pl.BlockSpec((hl, d), lambda i: (jnp.maximum(i * per - 1, 0), 0)),
            pl.BlockSpec((hl, d), lambda i: (jnp.minimum((i + 1) * per, n_halo - 1), 0)),
            pl.BlockSpec((1, d), lambda i: (0, 0)),
            pl.BlockSpec((d, n_all), lambda i: (0, 0)),
            pl.BlockSpec((1, n_all - n_qkv), lambda i: (0, 0)),
        ],
        out_specs=[
            pl.BlockSpec((tm, n_qkv), lambda i: (i, 0)),
            pl.BlockSpec((tm, n_all - n_qkv), lambda i: (i, 0)),
        ],
        out_shape=[
            jax.ShapeDtypeStruct((n, n_qkv), BF16),
            jax.ShapeDtypeStruct((n, n_all - n_qkv), F32),
        ],
        scratch_shapes=[pltpu.VMEM((tm + 2 * hl, d), BF16), pltpu.VMEM((tm + 2 * hl, chunk), F32)],
        compiler_params=_cparams("parallel"),
        name="norm_inproj",
    )(h2, h2, h2, gain.reshape(1, d), w_bf16, mu.astype(F32).reshape(1, n_all - n_qkv))


def _outproj_kernel(h_ref, na_ref, yf_ref, yb_ref, bonf_ref, bonb_ref, g_ref, lg_ref, lb_ref, wa_ref, wb_ref, o_ref):
    rw = _rwkv_readout(yf_ref[...], yb_ref[...], bonf_ref[...], bonb_ref[...], g_ref[...], lg_ref[...], lb_ref[...])
    acc = _dot(na_ref[...], wa_ref[...])
    acc = acc + _dot(rw.astype(BF16), wb_ref[...])
    o_ref[...] = h_ref[...] + acc


def outproj_residual(h2, na, rwkv_parts, lnx_g, lnx_b, w_out_bf16):
    n, d = h2.shape
    ka, kb = na.shape[1], rwkv_parts[0].shape[1]
    tm = _row_tile(n, 608)
    rows = lambda w: pl.BlockSpec((tm, w), lambda i: (i, 0))
    whole = lambda r, c: pl.BlockSpec((r, c), lambda i: (0, 0))
    return pl.pallas_call(
        _outproj_kernel,
        grid=(n // tm,),
        in_specs=[rows(d), rows(ka)] + [rows(kb)] * 5 + [whole(1, kb), whole(1, kb), whole(ka, d), whole(kb, d)],
        out_specs=rows(d),
        out_shape=jax.ShapeDtypeStruct((n, d), F32),
        compiler_params=_cparams("parallel"),
        name="outproj_residual",
    )(h2, na, *rwkv_parts, lnx_g.astype(F32).reshape(1, kb), lnx_b.astype(F32).reshape(1, kb),
      w_out_bf16[:ka], w_out_bf16[ka:])


def _store_token_tiles(ref, x):
    rows = x.shape[0]
    s_n = x.shape[1] // LANES
    for s in range(s_n):
        ref[pl.ds(s, rows, stride=s_n), :] = x[:, s * LANES:(s + 1) * LANES]


def _load_token_tile_cols(ref, s, rows, s_n):
    return ref[pl.ds(s, rows, stride=s_n), :]


def _router_kernel(h_ref, g_ref, whi_ref, wlo_ref, b_ref, xn_ref, route_ref):
    xn = _rms(h_ref[...], g_ref[...])
    x_hi, x_lo = _split_bf16(xn)
    _store_token_tiles(xn_ref, xn)
    logits = (_dot(x_hi, whi_ref[...]) + _dot(x_hi, wlo_ref[...]) + _dot(x_lo, whi_ref[...])
              + b_ref[...])
    tm = logits.shape[0]
    lane = lax.broadcasted_iota(I32, (tm, ROUTER_LANES), 1)
    big = jnp.int32(ROUTER_LANES)

    is_g = lane < MOE_GROUPS
    lg = jnp.where(is_g, logits, -jnp.inf)
    eg = jnp.where(is_g, jnp.exp(lg - jnp.max(lg, axis=-1, keepdims=True)), 0.0)
    pg = eg / jnp.sum(eg, axis=-1, keepdims=True)
    p_grp = jnp.max(pg, axis=-1, keepdims=True)
    grp = jnp.min(jnp.where(is_g & (pg == p_grp), lane, big), axis=-1, keepdims=True)

    lo_lane = MOE_GROUPS + MOE_PER_GROUP * grp
    is_e = (lane >= lo_lane) & (lane < lo_lane + MOE_PER_GROUP)
    le = jnp.where(is_e, logits, -jnp.inf)
    ee = jnp.where(is_e, jnp.exp(le - jnp.max(le, axis=-1, keepdims=True)), 0.0)
    pe = jnp.where(is_e, ee / jnp.sum(ee, axis=-1, keepdims=True), -1.0)
    p1 = jnp.max(pe, axis=-1, keepdims=True)
    i1 = jnp.min(jnp.where(pe == p1, lane, big), axis=-1, keepdims=True)
    pe2 = jnp.where(lane == i1, -1.0, pe)
    p2 = jnp.max(pe2, axis=-1, keepdims=True)
    i2 = jnp.min(jnp.where(pe2 == p2, lane, big), axis=-1, keepdims=True)
    denom = p1 + p2
    g1 = p_grp * p1 / denom
    g2 = p_grp * p2 / denom
    e1 = (i1 - MOE_GROUPS).astype(F32)
    e2 = (i2 - MOE_GROUPS).astype(F32)
    route_ref[...] = jnp.where(lane == 0, e1, jnp.where(lane == 1, e2, jnp.where(lane == 2, g1, g2)))


def moe_router(h2, gain, w_group, b_group, w_expert, b_expert):
    n, d = h2.shape
    n_r = MOE_GROUPS + MOE_EXPERTS
    w_r = jnp.concatenate([w_group, jnp.transpose(w_expert, (1, 0, 2)).reshape(d, MOE_EXPERTS)], axis=1)
    w_r = jnp.pad(w_r.astype(F32), ((0, 0), (0, ROUTER_LANES - n_r)))
    w_hi, w_lo = _split_bf16(w_r)
    b_r = jnp.pad(jnp.concatenate([b_group, b_expert.reshape(-1)]).astype(F32), (0, ROUTER_LANES - n_r))
    tm = _row_tile(n, 608)
    return pl.pallas_call(
        _router_kernel,
        grid=(n // tm,),
        in_specs=[
            pl.BlockSpec((tm, d), lambda i: (i, 0)),
            pl.BlockSpec((1, d), lambda i: (0, 0)),
            pl.BlockSpec((d, ROUTER_LANES), lambda i: (0, 0)),
            pl.BlockSpec((d, ROUTER_LANES), lambda i: (0, 0)),
            pl.BlockSpec((1, ROUTER_LANES), lambda i: (0, 0)),
        ],
        out_specs=[
            pl.BlockSpec((tm * (d // LANES), LANES), lambda i: (i, 0)),
            pl.BlockSpec((tm, ROUTER_LANES), lambda i: (i, 0)),
        ],
        out_shape=[
            jax.ShapeDtypeStruct((n * (d // LANES), LANES), F32),
            jax.ShapeDtypeStruct((n, ROUTER_LANES), F32),
        ],
        compiler_params=_cparams("parallel"),
        name="moe_router",
    )(h2, gain.reshape(1, d), w_hi, w_lo, b_r.reshape(1, ROUTER_LANES))


def _moe_dispatch_kernel(tail_ref, n_used_ref, dst_ref, xn_hbm, xbuf_hbm, zero_ref, stage_ref, sem, lsem, zsem,
                         *, tm, s_n, n_blocks):
    i = pl.program_id(0)
    n_steps = pl.num_programs(0)
    tile_rows = zero_ref.shape[0]

    def zero_block(b, carry):
        pltpu.make_async_copy(zero_ref, xbuf_hbm.at[pl.ds(pl.multiple_of(b * tile_rows, tile_rows), tile_rows), :],
                              zsem).start()
        return carry

    def wait_zero_block(b, carry):
        pltpu.make_async_copy(zero_ref, xbuf_hbm.at[pl.ds(0, tile_rows), :], zsem).wait()
        return carry

    n_slots = stage_ref.shape[0]
    slot = i % n_slots
    nxt = (i + 1) % n_slots

    def load(step, s):
        return pltpu.make_async_copy(xn_hbm.at[pl.ds(pl.multiple_of(step * tm * s_n, s_n), tm * s_n), :],
                                     stage_ref.at[s], lsem.at[s])

    def wait_rows_out(s):
        for _ in range(2):
            pltpu.make_async_copy(stage_ref.at[s], xbuf_hbm.at[pl.ds(0, tm * s_n), :], sem.at[s]).wait()

    @pl.when(i == 0)
    def _():
        load(0, 0).start()
        zero_ref[...] = jnp.zeros_like(zero_ref)
        for e in range(tail_ref.shape[0]):
            @pl.when(tail_ref[e] >= 0)
            def _():
                pltpu.make_async_copy(zero_ref, xbuf_hbm.at[pl.ds(pl.multiple_of(tail_ref[e], s_n), tile_rows), :],
                                      zsem).start()
        lax.fori_loop(n_used_ref[0], n_blocks, zero_block, 0)
        for e in range(tail_ref.shape[0]):
            @pl.when(tail_ref[e] >= 0)
            def _():
                wait_zero_block(0, 0)
        lax.fori_loop(n_used_ref[0], n_blocks, wait_zero_block, 0)

    @pl.when(i >= n_slots - 1)
    def _():
        wait_rows_out(nxt)

    @pl.when(i + 1 < n_steps)
    def _():
        load(i + 1, nxt).start()

    load(i, slot).wait()
    for r in range(tm):
        src = stage_ref.at[slot, pl.ds(r * s_n, s_n), :]
        for k in range(2):
            dst = pl.multiple_of(dst_ref[k, r], s_n)
            pltpu.make_async_copy(src, xbuf_hbm.at[pl.ds(dst, s_n), :], sem.at[slot]).start(priority=k)

    @pl.when(i == n_steps - 1)
    def _():
        for back in range(n_slots - 1):
            @pl.when(i >= back)
            def _():
                wait_rows_out((i - back) % n_slots)


def moe_dispatch(xn_tiles, dst_tiles, tail_start, n_used, n_blocks, tm, s_n):
    n_steps = dst_tiles.shape[0]
    n_rows = n_blocks * MOE_TILE
    assert n_steps * tm * s_n == xn_tiles.shape[0]
    grid_spec = pltpu.PrefetchScalarGridSpec(
        num_scalar_prefetch=2,
        grid=(n_steps,),
        in_specs=[
            pl.BlockSpec((None, 2, tm), lambda i, tail, nu: (i, 0, 0), memory_space=pltpu.SMEM),
            pl.BlockSpec(memory_space=pl.ANY),
        ],
        out_specs=pl.BlockSpec(memory_space=pl.ANY),
        scratch_shapes=[
            pltpu.VMEM((MOE_TILE * s_n, LANES), F32),
            pltpu.VMEM((MOE_STAGE_SLOTS, tm * s_n, LANES), F32),
            pltpu.SemaphoreType.DMA((MOE_STAGE_SLOTS,)),
            pltpu.SemaphoreType.DMA((MOE_STAGE_SLOTS,)),
            pltpu.SemaphoreType.DMA(()),
        ],
    )
    return pl.pallas_call(
        functools.partial(_moe_dispatch_kernel, tm=tm, s_n=s_n, n_blocks=n_blocks),
        grid_spec=grid_spec,
        out_shape=jax.ShapeDtypeStruct((n_rows * s_n, LANES), F32),
        compiler_params=_cparams("arbitrary"),
        name="moe_dispatch",
    )(tail_start, n_used, dst_tiles, xn_tiles)


def _expert_kernel(blk_e_ref, n_used_ref, x_ref, w1_ref, w3_ref, w2_ref, y_ref, xb_ref, w1b_ref, w3b_ref, w2b_ref):
    i = pl.program_id(0)
    used = i < n_used_ref[0]
    tile, d = xb_ref.shape
    s_n = d // LANES
    prev_e = blk_e_ref[jnp.maximum(i - 1, 0)]
    fresh = (i == 0) | (blk_e_ref[i] != prev_e)

    @pl.when(used & fresh)
    def _():
        w1b_ref[...] = w1_ref[...].astype(BF16)
        w3b_ref[...] = w3_ref[...].astype(BF16)
        w2b_ref[...] = w2_ref[...].astype(BF16)

    @pl.when(used)
    def _():
        for s in range(s_n):
            xb_ref[:, s * LANES:(s + 1) * LANES] = _load_token_tile_cols(x_ref, s, tile, s_n).astype(BF16)
        x = xb_ref[...]
        a = _dot(x, w1b_ref[...])
        b = _dot(x, w3b_ref[...])
        hmid = (a * jax.nn.sigmoid(a) * b).astype(BF16)
        _store_token_tiles(y_ref, _dot(hmid, w2b_ref[...]))

    @pl.when(jnp.logical_not(used))
    def _():
        y_ref[...] = jnp.zeros_like(y_ref)


def moe_experts(xbuf, blk_e, n_used, w1, w3, w2, layer, n_blocks):
    d, f = w1.shape[2], w1.shape[3]
    s_n = d // LANES
    tile = MOE_TILE

    def w_map(i, blk_e_ref, n_used_ref):
        return (layer, blk_e_ref[i], 0, 0)

    def x_map(i, blk_e_ref, n_used_ref):
        return (jnp.minimum(i, jnp.maximum(n_used_ref[0] - 1, 0)), 0)

    grid_spec = pltpu.PrefetchScalarGridSpec(
        num_scalar_prefetch=2,
        grid=(n_blocks,),
        in_specs=[
            pl.BlockSpec((tile * s_n, LANES), x_map),
            pl.BlockSpec((None, None, d, f), w_map),
            pl.BlockSpec((None, None, d, f), w_map),
            pl.BlockSpec((None, None, f, d), w_map),
        ],
        out_specs=pl.BlockSpec((tile * s_n, LANES), lambda i, be, nu: (i, 0)),
        scratch_shapes=[
            pltpu.VMEM((tile, d), BF16),
            pltpu.VMEM((d, f), BF16),
            pltpu.VMEM((d, f), BF16),
            pltpu.VMEM((f, d), BF16),
        ],
    )
    return pl.pallas_call(
        _expert_kernel,
        grid_spec=grid_spec,
        out_shape=jax.ShapeDtypeStruct((n_blocks * tile * s_n, LANES), F32),
        compiler_params=_cparams("arbitrary"),
        name="moe_experts",
    )(blk_e, n_used, xbuf, w1, w3, w2)


def _moe_combine_kernel(src_ref, src_next_ref, h_ref, route_ref, y_hbm, o_ref, yg_ref, sem, *, n_steps):
    i = pl.program_id(0)
    slot = i & 1
    tm, d = h_ref.shape
    s_n = d // LANES

    def start_gather(ids_ref, dst_slot):
        for k in range(2):
            for r in range(tm):
                src = pl.multiple_of(ids_ref[k, r], s_n)
                pltpu.make_async_copy(y_hbm.at[pl.ds(src, s_n), :],
                                      yg_ref.at[dst_slot, k, pl.ds(r * s_n, s_n), :],
                                      sem.at[dst_slot]).start(priority=r % 2)

    @pl.when(i == 0)
    def _():
        start_gather(src_ref, 0)

    @pl.when(i + 1 < n_steps)
    def _():
        start_gather(src_next_ref, 1 - slot)

    for k in range(2):
        pltpu.make_async_copy(y_hbm.at[pl.ds(0, tm * s_n), :], yg_ref.at[slot, k], sem.at[slot]).wait()
    route = route_ref[...]
    g1 = route[:, 2:3]
    g2 = route[:, 3:4]
    for s in range(s_n):
        cols = slice(s * LANES, (s + 1) * LANES)
        o_ref[:, cols] = (h_ref[:, cols] + g1 * _load_token_tile_cols(yg_ref.at[slot, 0], s, tm, s_n)
                          + g2 * _load_token_tile_cols(yg_ref.at[slot, 1], s, tm, s_n))


def moe_combine(h2, route, y_tiles, src_tiles, tm):
    n, d = h2.shape
    s_n = d // LANES
    n_steps = src_tiles.shape[0]
    ids = lambda index: pl.BlockSpec((None, 2, tm), lambda i: (index(i), 0, 0), memory_space=pltpu.SMEM)
    return pl.pallas_call(
        functools.partial(_moe_combine_kernel, n_steps=n_steps),
        grid=(n_steps,),
        in_specs=[
            ids(lambda i: i),
            ids(lambda i: jnp.minimum(i + 1, n_steps - 1)),
            pl.BlockSpec((tm, d), lambda i: (i, 0)),
            pl.BlockSpec((tm, ROUTER_LANES), lambda i: (i, 0)),
            pl.BlockSpec(memory_space=pl.ANY),
        ],
        out_specs=pl.BlockSpec((tm, d), lambda i: (i, 0)),
        out_shape=jax.ShapeDtypeStruct((n, d), F32),
        scratch_shapes=[pltpu.VMEM((2, 2, tm * s_n, LANES), F32), pltpu.SemaphoreType.DMA((2,))],
        compiler_params=_cparams("arbitrary"),
        name="moe_combine",
    )(src_tiles, src_tiles, h2, route, y_tiles)


def hierarchical_moe_residual(h2, gain, w_group, b_group, w_expert, b_expert, w1, w3, w2, layer):
    n, d = h2.shape
    xn, route = moe_router(h2, gain, w_group, b_group, w_expert, b_expert)
    e_km = jnp.concatenate([route[:, 0], route[:, 1]]).astype(I32)
    n_assign = 2 * n
    onehot = (e_km[:, None] == jnp.arange(MOE_EXPERTS, dtype=I32)[None, :]).astype(I32)
    csum = jnp.cumsum(onehot, axis=0)
    counts = csum[-1]
    padded = (counts + MOE_TILE - 1) // MOE_TILE * MOE_TILE
    pad_end = jnp.cumsum(padded)
    pad_start = pad_end - padded
    dest = jnp.sum((csum - onehot + pad_start[None, :]) * onehot, axis=1)
    n_blocks = -(-n_assign // MOE_TILE) + MOE_EXPERTS
    blk_start = jnp.arange(n_blocks, dtype=I32) * MOE_TILE
    blk_e = jnp.minimum(jnp.sum((pad_end[None, :] <= blk_start[:, None]).astype(I32), axis=1),
                        MOE_EXPERTS - 1).astype(I32)
    n_used = (pad_end[-1] // MOE_TILE).astype(I32).reshape(1)
    s_n = d // LANES
    tm = _row_tile(n, MOE_TOKEN_TILE)
    dest_tiles = jnp.transpose((dest * s_n).astype(I32).reshape(2, n // tm, tm), (1, 0, 2))
    tail_start = jnp.where(counts > 0, (pad_end - MOE_TILE) * s_n, -1).astype(I32)
    xbuf = moe_dispatch(xn, dest_tiles, tail_start, n_used, n_blocks, tm, s_n)
    y = moe_experts(xbuf, blk_e, n_used, w1, w3, w2, layer, n_blocks)
    return moe_combine(h2, route, y, dest_tiles, tm)


def _final_norm_kernel(h_ref, g_ref, o_ref):
    o_ref[...] = _rms(h_ref[...], g_ref[...])


def final_norm(h3, gain):
    b, l, d = h3.shape
    t = l - N_META
    tm = _row_tile(t, 512)
    return pl.pallas_call(
        _final_norm_kernel,
        grid=(b, t // tm),
        in_specs=[
            pl.BlockSpec((None, pl.Element(tm), pl.Element(d)),
                         lambda bi, i: (bi, pl.multiple_of(N_META + i * tm, SUBLANES_BF16), 0)),
            pl.BlockSpec((1, d), lambda bi, i: (0, 0)),
        ],
        out_specs=pl.BlockSpec((None, tm, d), lambda bi, i: (bi, i, 0)),
        out_shape=jax.ShapeDtypeStruct((b, t, d), F32),
        compiler_params=_cparams("parallel", "parallel"),
        name="final_norm",
    )(h3, gain.reshape(1, d))


NA_QROWS = 8
NA_KROWS = 3 * NA_QROWS
NA_ROWS_PER_ITER = 4


def _na_kernel(q_ref, kw_ref, vw_ref, qm_ref, km_ref, vm_ref, bias_ref, o_ref, om_ref, *, rows, scale):
    blk = pl.program_id(1)
    tq = GRID_W
    n_pairs = q_ref.shape[1] // LANES
    base = jnp.clip(NA_QROWS * blk - NA_QROWS, 0, rows - NA_KROWS)
    lane = lax.broadcasted_iota(I32, (tq, LANES), 1)
    halves = [lane < HEAD_DIM, lane >= HEAD_DIM]

    pad = jnp.zeros((LANES - N_META, LANES), km_ref.dtype)
    k_meta = [jnp.concatenate([km_ref[:, p * LANES:(p + 1) * LANES], pad], axis=0) for p in range(n_pairs)]
    v_meta = [jnp.concatenate([vm_ref[:, p * LANES:(p + 1) * LANES], pad], axis=0) for p in range(n_pairs)]
    lane2 = lax.broadcasted_iota(I32, (2 * tq, LANES), 1)
    meta_bias2 = jnp.where(lane2 < N_META, 0.0, NEG_INF)

    n_win = NA_WIN_ROWS * GRID_W

    def row_body(jb, carry):
        colsl = [slice(p * LANES, (p + 1) * LANES) for p in range(n_pairs)]
        units = [(jj, p) for jj in range(NA_ROWS_PER_ITER) for p in range(n_pairs)]
        s_idx, koff, qoff = [], [], []
        for jj in range(NA_ROWS_PER_ITER):
            j = jb * NA_ROWS_PER_ITER + jj
            r = NA_QROWS * blk + j
            start = jnp.clip(r - NA_WIN_ROWS // 2, 0, rows - NA_WIN_ROWS)
            s_idx.append(start - r + (NA_WIN_ROWS - 1))
            koff.append(pl.multiple_of((start - base) * GRID_W, GRID_W))
            qoff.append(pl.multiple_of(j * tq, tq))
        q_pair = {(jj, p): q_ref[pl.ds(qoff[jj], tq), c]
                  for jj in range(NA_ROWS_PER_ITER) for p, c in enumerate(colsl)}
        k_ext = {(jj, p): jnp.concatenate([kw_ref[pl.ds(koff[jj], n_win), c], k_meta[p]], axis=0)
                 for jj in range(NA_ROWS_PER_ITER) for p, c in enumerate(colsl)}
        v_ext = {(jj, p): jnp.concatenate([vw_ref[pl.ds(koff[jj], n_win), c], v_meta[p]], axis=0)
                 for jj in range(NA_ROWS_PER_ITER) for p, c in enumerate(colsl)}
        qh = [_stack_heads(q_pair[jj, p], halves[0]) for jj, p in units]
        s = [_dot_nt(qh[u], k_ext[jj, p]) * scale + jnp.concatenate([bias_ref[p, s_idx[jj]], meta_bias2], axis=1)
             for u, (jj, p) in enumerate(units)]
        m = [jnp.max(x, axis=-1, keepdims=True) for x in s]
        e = [jnp.exp(x - mx) for x, mx in zip(s, m)]
        den = [jnp.sum(x, axis=-1, keepdims=True) for x in e]
        o = [_dot(e[u].astype(BF16), v_ext[jj, p]) / den[u] for u, (jj, p) in enumerate(units)]
        for u, (jj, p) in enumerate(units):
            o_ref[pl.ds(qoff[jj], tq), colsl[p]] = jnp.where(halves[0], o[u][0:tq], o[u][tq:2 * tq]).astype(o_ref.dtype)
        return carry

    lax.fori_loop(0, NA_QROWS // NA_ROWS_PER_ITER, row_body, 0)

    @pl.when(blk == 0)
    def _():
        lane_m = lax.broadcasted_iota(I32, (N_META, LANES), 1)
        for p in range(n_pairs):
            cols = slice(p * LANES, (p + 1) * LANES)
            q_pair = qm_ref[:, cols]
            kmp = km_ref[:, cols]
            vmp = vm_ref[:, cols]
            outs = []
            for hh in range(2):
                sel = (lane_m < HEAD_DIM) if hh == 0 else (lane_m >= HEAD_DIM)
                qp = jnp.where(sel, q_pair, jnp.zeros_like(q_pair))
                s_m = _dot_nt(qp, kmp) * scale
                p_m = jnp.exp(s_m - jnp.max(s_m, axis=-1, keepdims=True))
                den = jnp.sum(p_m, axis=-1, keepdims=True)
                outs.append(_dot(p_m.astype(BF16), vmp) / den)
            om_ref[:, cols] = jnp.where(lane_m < HEAD_DIM, outs[0], outs[1]).astype(om_ref.dtype)


def _na_bias_table(rpb):
    h = rpb.shape[0]
    c_ids = jnp.arange(GRID_W)
    c_start = jnp.clip(c_ids - NA_WIN_COLS // 2, 0, GRID_W - NA_WIN_COLS)
    in_band = (c_ids[None, :] >= c_start[:, None]) & (c_ids[None, :] < c_start[:, None] + NA_WIN_COLS)
    dc = jnp.clip(c_ids[None, :] - c_ids[:, None] + NA_WIN_COLS - 1, 0, 2 * NA_WIN_COLS - 2)
    tab = jnp.where(in_band[None, None], rpb.astype(F32)[:, :, dc], NEG_INF)
    win = jnp.stack([tab[:, s:s + NA_WIN_ROWS] for s in range(NA_WIN_ROWS)], axis=1)
    per_head = jnp.transpose(win, (0, 1, 3, 2, 4)).reshape(h // 2, 2, NA_WIN_ROWS, GRID_W, NA_WIN_ROWS * GRID_W)
    return jnp.transpose(per_head, (0, 2, 1, 3, 4)).reshape(h // 2, NA_WIN_ROWS, 2 * GRID_W, NA_WIN_ROWS * GRID_W)


def na_attention(qkv, rpb):
    b, l, w3 = qkv.shape
    w = w3 // 3
    t = l - N_META
    rows = t // GRID_W
    assert rows * GRID_W == t and rows % NA_QROWS == 0 and rows >= NA_KROWS
    tq = NA_QROWS * GRID_W
    tk = NA_KROWS * GRID_W
    bias = _na_bias_table(rpb)
    al = SUBLANES_BF16

    def q_map(bi, i):
        return (bi, pl.multiple_of(N_META + i * tq, al), 0)

    def kv_map(col):
        def f(bi, i):
            base = jnp.clip(NA_QROWS * i - NA_QROWS, 0, rows - NA_KROWS)
            return (bi, pl.multiple_of(N_META + base * GRID_W, al), col)
        return f

    def meta_map(col):
        return lambda bi, i: (bi, 0, col)

    el = pl.Element
    grid_out, meta_out = pl.pallas_call(
        functools.partial(_na_kernel, rows=rows, scale=HEAD_DIM ** -0.5),
        grid=(b, rows // NA_QROWS),
        in_specs=[
            pl.BlockSpec((None, el(tq), el(w)), q_map),
            pl.BlockSpec((None, el(tk), el(w)), kv_map(w)),
            pl.BlockSpec((None, el(tk), el(w)), kv_map(2 * w)),
            pl.BlockSpec((None, el(N_META), el(w)), meta_map(0)),
            pl.BlockSpec((None, el(N_META), el(w)), meta_map(w)),
            pl.BlockSpec((None, el(N_META), el(w)), meta_map(2 * w)),
            pl.BlockSpec(bias.shape, lambda bi, i: (0, 0, 0, 0)),
        ],
        out_specs=[
            pl.BlockSpec((None, tq, w), lambda bi, i: (bi, i, 0)),
            pl.BlockSpec((None, N_META, w), lambda bi, i: (bi, 0, 0)),
        ],
        out_shape=[
            jax.ShapeDtypeStruct((b, t, w), BF16),
            jax.ShapeDtypeStruct((b, N_META, w), BF16),
        ],
        compiler_params=_cparams("parallel", "arbitrary"),
        name="na_attention",
    )(qkv, qkv, qkv, qkv, qkv, qkv, bias)
    return jnp.concatenate([meta_out, grid_out], axis=1)


RWKV_CHUNK = 64
RWKV_TILE_CHUNKS = 2


def _split3_bf16(x):
    p1 = x.astype(BF16)
    r1 = x - p1.astype(F32)
    p2 = r1.astype(BF16)
    p3 = (r1 - p2.astype(F32)).astype(BF16)
    return p1, p2, p3


def _mm1(a, b):
    return _dot(a.astype(BF16), b.astype(BF16))


def _mm3(a, b):
    ah, al = _split_bf16(a)
    bh, bl = _split_bf16(b)
    return _dot(ah, bh) + _dot(ah, bl) + _dot(al, bh)


def _mm1_nt(a, b):
    return _dot_nt(a.astype(BF16), b.astype(BF16))


def _mm3_nt(a, b):
    ah, al = _split_bf16(a)
    bh, bl = _split_bf16(b)
    return _dot_nt(ah, bh) + _dot_nt(ah, bl) + _dot_nt(al, bh)


def _exact_left(mat_bf16, x):
    p1, p2, p3 = _split3_bf16(x)
    return _dot(mat_bf16, p1) + _dot(mat_bf16, p2) + _dot(mat_bf16, p3)


def _exact_right(x, mat_bf16):
    p1, p2, p3 = _split3_bf16(x)
    return _dot(p1, mat_bf16) + _dot(p2, mat_bf16) + _dot(p3, mat_bf16)


def _head_block_ones(width):
    ri = lax.broadcasted_iota(I32, (width, width), 0) // HEAD_DIM
    ci = lax.broadcasted_iota(I32, (width, width), 1) // HEAD_DIM
    return (ri == ci).astype(BF16)


def _head_sums(x, exact):
    ones_pair = _head_block_ones(LANES)
    tiles = []
    for p in range(x.shape[1] // LANES):
        xt = x[:, p * LANES:(p + 1) * LANES]
        tiles.append(_exact_right(xt, ones_pair) if exact else _dot(xt.astype(BF16), ones_pair))
    return jnp.concatenate(tiles, axis=1)


def _stack_heads(x, m0):
    z = jnp.zeros_like(x)
    return jnp.concatenate([jnp.where(m0, x, z), jnp.where(m0, z, x)], axis=0)


_MM_L4 = _mm1_nt
_MM_KT = _mm1_nt
_MM_SQ = _mm1
_MM_AP = _mm1
_MM_V = _mm1
_MM_Y = _mm1
_MM_UPD = _mm1


def _rwkv_chunk_maps(streams, c, n_sub):
    assert c == 64
    c2 = 2 * c
    lane = lax.broadcasted_iota(I32, (c, LANES), 1)
    m0 = lane < HEAD_DIM
    r_i = lax.broadcasted_iota(I32, (c2, c2), 0)
    c_i = lax.broadcasted_iota(I32, (c2, c2), 1)
    eye = (r_i == c_i).astype(F32)
    rel = r_i % c - c_i % c
    masks = {sg: (rel * sg > 0, rel * sg >= 0) for sg in {s["sign"] for s in streams}}
    items = [(j, q) for j in range(len(streams)) for q in range(n_sub)]

    def part(j, q, name):
        return _stack_heads(streams[j][name][q * c:(q + 1) * c], m0)

    lhs = {it: jnp.concatenate([part(*it, "kkp"), part(*it, "rp")], axis=0) for it in items}
    rhs = {it: jnp.concatenate([part(*it, "ki"), part(*it, "bi")], axis=0) for it in items}
    vs = {it: part(*it, "v") for it in items}
    kipcs = {it: part(*it, "kipc") for it in items}
    bipcs = {it: part(*it, "bipc") for it in items}
    l4 = {it: _MM_L4(lhs[it], rhs[it]) for it in items}
    m_kk, n1, m_rk, m_rb = {}, {}, {}, {}
    for it in items:
        strict, incl = masks[streams[it[0]]["sign"]]
        m = l4[it]
        m_kk[it] = jnp.where(strict, m[0:c2, 0:c2], 0.0)
        n1[it] = jnp.where(strict, m[0:c2, c2:2 * c2], 0.0)
        m_rk[it] = jnp.where(incl, m[c2:2 * c2, 0:c2], 0.0)
        m_rb[it] = jnp.where(incl, m[c2:2 * c2, c2:2 * c2], 0.0)
    n2 = {it: _MM_SQ(n1[it], n1[it]) for it in items}
    n4 = {it: _MM_SQ(n2[it], n2[it]) for it in items}
    n8 = {it: _MM_SQ(n4[it], n4[it]) for it in items}
    n16 = {it: _MM_SQ(n8[it], n8[it]) for it in items}
    n32 = {it: _MM_SQ(n16[it], n16[it]) for it in items}
    p1 = {it: (eye - n1[it]) + _MM_AP(eye - n1[it], n2[it]) for it in items}
    p2 = {it: eye + n4[it] + n8[it] + _MM_AP(n4[it], n8[it]) for it in items}
    p3 = {it: eye + n16[it] + n32[it] + _MM_AP(n16[it], n32[it]) for it in items}
    p23 = {it: _MM_AP(p2[it], p3[it]) for it in items}
    winv = {it: _MM_AP(p1[it], p23[it]) for it in items}
    mv = {it: _MM_V(m_kk[it], vs[it]) for it in items}
    mrv = {it: _MM_Y(m_rk[it], vs[it]) for it in items}
    wl = {it: _MM_AP(winv[it], lhs[it][0:c2]) for it in items}
    wmv = {it: _MM_AP(winv[it], mv[it]) for it in items}
    yl = {it: lhs[it][c2:2 * c2] - _MM_Y(m_rb[it], wl[it]) for it in items}
    y0 = {it: mrv[it] - _MM_Y(m_rb[it], wmv[it]) for it in items}
    g2 = {it: _MM_UPD(jnp.transpose(wl[it]), bipcs[it]) for it in items}
    hh = {it: _MM_UPD(jnp.transpose(jnp.concatenate([vs[it], -wmv[it]], axis=0)),
                      jnp.concatenate([kipcs[it], bipcs[it]], axis=0)) for it in items}
    return {it: (yl[it], y0[it], g2[it], hh[it]) for it in items}


def _rwkv_apply_maps(streams, maps, c, n_sub):
    c2 = 2 * c
    st = [s["st"] for s in streams]
    ys = {}
    for k in range(n_sub):
        cur = [(j, k if s["sign"] > 0 else n_sub - 1 - k) for j, s in enumerate(streams)]
        sg = [_MM_UPD(st[j], maps[j][q][2]) for j, q in cur]
        yk = [_MM_KT(maps[j][q][0], st[j]) + maps[j][q][1] for j, q in cur]
        for it, yi in zip(cur, yk):
            ys[it] = yi[0:c] + yi[c:c2]
        st = [st[j] * streams[j]["pc"][q] - sgi + maps[j][q][3] for (j, q), sgi in zip(cur, sg)]
    return [(jnp.concatenate([ys[j, q] for q in range(n_sub)], axis=0), st[j]) for j in range(len(streams))]


def _softplus(z):
    return jnp.maximum(z, 0.0) + jnp.log(1.0 + jnp.exp(-jnp.abs(z)))


def _rwkv_tile_prep(x_ref, tile, seq_len, width, sign, w0, a0, w_wa, k_k, k_a, r_k):
    cs = RWKV_CHUNK
    c = x_ref.shape[0]
    valid = jnp.minimum(c, seq_len - tile * c)
    row = lax.broadcasted_iota(I32, (c, LANES), 0)
    rowv = row < valid
    lane = lax.broadcasted_iota(I32, (c, LANES), 1)
    ones_pair = _head_block_ones(LANES)

    def shifted(lo):
        return jnp.where(rowv, x_ref[:, lo:lo + LANES], 0.0)

    wa = shifted(3 * width)
    g_lo = shifted(3 * width + LANES)
    xwa = jnp.where(lane < LANES // 2, jnp.tanh(wa), wa)
    la = _dot(xwa.astype(BF16), w_wa)

    t_i = lax.broadcasted_iota(I32, (c, c), 0)
    s_i = lax.broadcasted_iota(I32, (c, c), 1)
    tri = ((t_i // cs == s_i // cs) & ((t_i - s_i) * sign >= 0)).astype(BF16)

    pairs = []
    for p in range(width // LANES):
        lo = p * LANES
        cols = slice(lo, lo + LANES)
        r = shifted(lo)
        k = shifted(width + lo)
        v = shifted(2 * width + lo)
        w_log = -_softplus(-(w0[:, cols] + la[:, cols])) - 0.5
        logw = jnp.where(rowv, -jnp.exp(w_log), 0.0)
        a = jax.nn.sigmoid(a0[:, cols] + la[:, width + lo:width + lo + LANES])
        kk0 = k * k_k[:, cols]
        ss = _dot((kk0 * kk0).astype(BF16), ones_pair)
        kk = kk0 / jnp.maximum(jnp.sqrt(ss), 1e-12)
        kdir = k * (1.0 + (a - 1.0) * k_a[:, cols])
        b = kk * a
        cl = _exact_left(tri, logw)
        lasts = [cl[q * cs + cs - 1:q * cs + cs, :] if sign > 0 else cl[q * cs:q * cs + 1, :]
                 for q in range(c // cs)]
        last = jnp.concatenate([jnp.broadcast_to(lq, (cs, LANES)) for lq in lasts], axis=0)
        e_n = jnp.exp(-cl)
        pcr = jnp.exp(last - cl)
        pairs.append(dict(kkp=kk * jnp.exp(cl - logw), rp=r * jnp.exp(cl), ki=kdir * e_n, bi=b * e_n,
                          kipc=kdir * pcr, bipc=b * pcr, v=v, pc=[jnp.exp(lq) for lq in lasts],
                          bonus=_dot((r * kdir * r_k[:, cols]).astype(BF16), ones_pair) * v))
    return pairs, g_lo


def _rwkv_scan_kernel(xf_ref, xb_ref, w0_ref, a0_ref, wwa_ref,
                      gup_ref, kk_ref, ka_ref, rk_ref, yf_ref, yb_ref, bonf_ref, bonb_ref, g_ref, st_ref,
                      *, seq_len, width):
    i = pl.program_id(1)
    n_chunks = pl.num_programs(1)
    n_pairs = width // LANES

    @pl.when(i == 0)
    def _():
        st_ref[...] = jnp.zeros_like(st_ref)

    tail = (kk_ref[...], ka_ref[...], rk_ref[...])
    n_sub = xf_ref.shape[0] // RWKV_CHUNK
    fwd, g_lo = _rwkv_tile_prep(xf_ref, i, seq_len, width, 1, w0_ref[0], a0_ref[0], wwa_ref[0], *tail)
    bwd, _ = _rwkv_tile_prep(xb_ref, n_chunks - 1 - i, seq_len, width, -1, w0_ref[1], a0_ref[1], wwa_ref[1], *tail)
    g_ref[...] = _mm1(jax.nn.sigmoid(g_lo), gup_ref[...]).astype(g_ref.dtype)
    streams = []
    for di, (pairs, sign, bon_ref) in enumerate(((fwd, 1, bonf_ref), (bwd, -1, bonb_ref))):
        for p, s in enumerate(pairs):
            bon_ref[:, p * LANES:(p + 1) * LANES] = s.pop("bonus")
            s["st"] = st_ref[di * n_pairs + p]
            s["sign"] = sign
            streams.append(s)
    m = _rwkv_chunk_maps(streams, RWKV_CHUNK, n_sub)
    maps = [{q: m[j, q] for q in range(n_sub)} for j in range(len(streams))]
    res = _rwkv_apply_maps(streams, maps, RWKV_CHUNK, n_sub)
    for j, (y, st_new) in enumerate(res):
        di, p = divmod(j, n_pairs)
        cols = slice(p * LANES, (p + 1) * LANES)
        (yf_ref if di == 0 else yb_ref)[:, cols] = y
        st_ref[j] = st_new


def _rwkv_readout(y_f, y_b, bon_f, bon_b, gate, lnx_g, lnx_b):
    y = y_f + y_b
    mean = _head_sums(y, exact=True) * (1.0 / HEAD_DIM)
    yc = y - mean
    var = _head_sums(yc * yc, exact=True) * (1.0 / HEAD_DIM)
    yn = yc * lax.rsqrt(var + RWKV_GN_EPS) * lnx_g + lnx_b
    return (yn + bon_f + bon_b) * gate.astype(F32)


def rwkv_mix(rest, w0, w_up, a0, a_up, g_up, k_k, k_a, r_k):
    bsz, l, n_cols = rest.shape
    width = w0.shape[1]
    rank = w_up.shape[1]
    assert n_cols == 3 * width + 2 * LANES and 2 * rank == LANES
    c = RWKV_CHUNK * RWKV_TILE_CHUNKS
    n_chunks = -(-l // c)
    zeros = jnp.zeros((2, rank, width), F32)
    w_wa = jnp.concatenate([jnp.concatenate([w_up.astype(F32), zeros], axis=2),
                            jnp.concatenate([zeros, a_up.astype(F32)], axis=2)], axis=1)
    w_wa = w_wa.astype(BF16)

    fwd_chunk = lambda i: i
    bwd_chunk = lambda i: n_chunks - 1 - i

    def tile_specs(chunk_of):
        return [pl.BlockSpec((None, c, n_cols), lambda b, i: (b, chunk_of(i), 0))]

    row2 = lambda a: a.astype(F32).reshape(1, -1)
    whole = lambda *shape: pl.BlockSpec(shape, lambda b, i: (0,) * len(shape))
    out_spec = lambda chunk_of: pl.BlockSpec((None, c, width), lambda b, i: (b, chunk_of(i), 0))
    act = lambda dt: jax.ShapeDtypeStruct((bsz, l, width), dt)
    y_f, y_b, bon_f, bon_b, g = pl.pallas_call(
        functools.partial(_rwkv_scan_kernel, seq_len=l, width=width),
        grid=(bsz, n_chunks),
        in_specs=tile_specs(fwd_chunk) + tile_specs(bwd_chunk) + [
            whole(2, 1, width), whole(2, 1, width),
            whole(2, LANES, 2 * width),
            whole(LANES, width),
            whole(1, width), whole(1, width), whole(1, width),
        ],
        out_specs=[out_spec(fwd_chunk), out_spec(bwd_chunk), out_spec(fwd_chunk), out_spec(bwd_chunk),
                   out_spec(fwd_chunk)],
        out_shape=[act(F32), act(F32), act(F32), act(F32), act(BF16)],
        scratch_shapes=[pltpu.VMEM((2 * (width // LANES), LANES, LANES), F32)],
        compiler_params=_cparams("parallel", "arbitrary"),
        name="rwkv_scan",
    )(rest, rest, w0.astype(F32).reshape(2, 1, width),
      a0.astype(F32).reshape(2, 1, width), w_wa, g_up.astype(BF16), row2(k_k), row2(k_a), row2(r_k))

    return tuple(a.reshape(bsz * l, width) for a in (y_f, y_b, bon_f, bon_b, g))


S5_CHUNK = 16


def _cpow(n, lr, li, step):
    mag = jnp.exp(n * (lr * step))
    ang = n * (li * step)
    return mag * jnp.cos(ang), mag * jnp.sin(ang)


def _s5_param_kernel(lamr_ref, stepr_ref, bt_ref, ct_ref, kmat_ref, wst_ref, cexp_ref, alpha_ref):
    t_len = S5_CHUNK
    n_i = S5_GROUP_CH
    p2 = 2 * S5_STATE
    ti = t_len * n_i

    lr = lamr_ref[0:1, :]
    li = lamr_ref[1:2, :]
    step = jnp.exp(stepr_ref[...])
    ab_re, ab_im = _cpow(1.0, lr, li, step)
    den = lr * lr + li * li
    z_re = ((ab_re - 1.0) * lr + ab_im * li) / den
    z_im = (ab_im * lr - (ab_re - 1.0) * li) / den
    t16 = lax.broadcasted_iota(I32, (t_len, p2), 0).astype(F32)
    is_f = lax.broadcasted_iota(I32, (t_len, p2), 1) < S5_STATE

    def rows_by_t(x):
        return jnp.concatenate([jnp.broadcast_to(x[t:t + 1], (n_i, p2)) for t in range(t_len)], axis=0)

    def tiled_rows(x):
        return jnp.concatenate([x] * t_len, axis=0)

    def pow_rows(n):
        q_re, q_im = _cpow(n, lr, li, step)
        return rows_by_t(q_re), rows_by_t(q_im)

    bt_re = tiled_rows(bt_ref[0])
    bt_im = tiled_rows(bt_ref[1])
    bb_re = z_re * bt_re - z_im * bt_im
    bb_im = z_re * bt_im + z_im * bt_re
    pw_re, pw_im = pow_rows(jnp.where(is_f, (t_len - 1.0) - t16, t16))
    wst_ref[:, 0:p2] = (pw_re * bb_re - pw_im * bb_im).astype(wst_ref.dtype)
    wst_ref[:, p2:2 * p2] = (pw_re * bb_im + pw_im * bb_re).astype(wst_ref.dtype)
    al_re, al_im = _cpow(float(t_len), lr, li, step)
    alpha_ref[0:1, :] = al_re
    alpha_ref[1:2, :] = al_im

    ct_re = tiled_rows(ct_ref[0])
    ct_im = tiled_rows(ct_ref[1])

    def c_times_pow(n):
        q_re, q_im = pow_rows(n)
        return jnp.transpose(ct_re * q_re - ct_im * q_im), jnp.transpose(ct_re * q_im + ct_im * q_re)

    ca_re, ca_im = c_times_pow(jnp.where(is_f, t16, jnp.where(t16 == 0.0, 0.0, t_len - t16)))
    lane_p = lax.broadcasted_iota(I32, (n_i, p2), 1)
    bbr = bb_re[0:n_i]
    bbi = bb_im[0:n_i]
    zero = jnp.zeros_like(bbr)
    strips = []
    for sel in (lane_p < S5_STATE, lane_p >= S5_STATE):
        strips.append(_mm3(jnp.where(sel, bbr, zero), ca_re) - _mm3(jnp.where(sel, bbi, zero), ca_im))
    strip_f, strip_b = strips
    t_k = lax.broadcasted_iota(I32, (n_i, ti), 1) // n_i
    for tt in range(t_len):
        sf = strip_f if tt == 0 else pltpu.roll(strip_f, tt * n_i, 1)
        sb = strip_b if tt == 0 else pltpu.roll(strip_b, tt * n_i, 1)
        blk = jnp.where(t_k >= tt, sf, 0.0) + jnp.where(t_k <= tt, sb, 0.0)
        kmat_ref[tt * n_i:(tt + 1) * n_i, :] = blk.astype(kmat_ref.dtype)

    co_re, co_im = c_times_pow(jnp.where(is_f, t16 + 1.0, t_len - t16))
    cexp_ref[0:p2, :] = co_re.astype(cexp_ref.dtype)
    cexp_ref[p2:2 * p2, :] = (-co_im).astype(cexp_ref.dtype)


def _s5_main_kernel(u_ref, kmat_ref, wst_ref, cexp_ref, alpha_ref, y_ref, x_ref, sf_ref, sb_ref,
                    *, n_batch, n_chunks):
    p2 = 2 * S5_STATE
    n_gb = u_ref.shape[0]
    for g in range(n_gb):
        x_ref[g] = _dot(u_ref[g].astype(BF16), wst_ref[g])
    lane = lax.broadcasted_iota(I32, (1, p2), 1)
    is_f = lane < S5_STATE
    alphas = [(alpha_ref[g, 0:1, :], alpha_ref[g, 1:2, :]) for g in range(n_gb)]

    sub = S5_SCAN_ROWS
    assert n_chunks % sub == 0
    chains = [(g, b) for g in range(n_gb) for b in range(n_batch)]

    def step(k, carry):
        new = []
        for (g, b), (s_re, s_im) in zip(chains, carry):
            a_re, a_im = alphas[g]
            row_f = pl.multiple_of(b * n_chunks + sub * k, sub)
            row_b = pl.multiple_of(b * n_chunks + (n_chunks - sub) - sub * k, sub)
            xf = x_ref[g, pl.ds(row_f, sub), :]
            xb = x_ref[g, pl.ds(row_b, sub), :]
            seen = []
            for r in range(sub):
                seen.append(jnp.concatenate([s_re, s_im], axis=1))
                rb = sub - 1 - r
                x_re = jnp.where(is_f, xf[r:r + 1, 0:p2], xb[rb:rb + 1, 0:p2])
                x_im = jnp.where(is_f, xf[r:r + 1, p2:2 * p2], xb[rb:rb + 1, p2:2 * p2])
                s_re, s_im = a_re * s_re - a_im * s_im + x_re, a_re * s_im + a_im * s_re + x_im
            sf_ref[g, pl.ds(row_f, sub), :] = jnp.concatenate(seen, axis=0)
            sb_ref[g, pl.ds(row_b, sub), :] = jnp.concatenate(seen[::-1], axis=0)
            new.append((s_re, s_im))
        return tuple(new)

    zero = jnp.zeros((1, p2), F32)
    lax.fori_loop(0, n_chunks // sub, step, tuple((zero, zero) for _ in chains))
    lane2 = lax.broadcasted_iota(I32, sf_ref.shape[1:], 1) % p2
    for g in range(n_gb):
        s_in = jnp.where(lane2 < S5_STATE, sf_ref[g], sb_ref[g])
        s_hi, s_lo = _split_bf16(s_in)
        y_ref[g] = (_dot(u_ref[g].astype(BF16), kmat_ref[g]) + _dot(s_hi, cexp_ref[g]) + _dot(s_lo, cexp_ref[g]))


S5_RELAYOUT_CHUNKS = 128
S5_SCAN_ROWS = 8
S5_GROUPS_PER_STEP = 4


def _s5_group_major_kernel(h_ref, g_ref, u_ref, hn_ref, ut_ref, *, seq_len):
    n_g, mt, ti = u_ref.shape
    t_len = S5_CHUNK
    n_i = ti // t_len
    n_lt = hn_ref.shape[0]
    g_lt = LANES // n_i
    rows = h_ref.shape[0]
    valid = seq_len - pl.program_id(1) * rows
    row = lax.broadcasted_iota(I32, h_ref.shape, 0)
    hn = jnp.where(row < valid, _rms(h_ref[...], g_ref[...]), 0.0)
    for j in range(n_lt):
        hn_ref[j] = hn[:, j * LANES:(j + 1) * LANES]
    for tau in range(t_len):
        for j in range(n_lt):
            xt = jnp.transpose(hn_ref[j, pl.ds(tau, mt, stride=t_len), :])
            ut_ref[j * g_lt:(j + 1) * g_lt, tau * n_i:(tau + 1) * n_i, :] = xt.reshape(g_lt, n_i, mt)
    for g in range(n_g):
        u_ref[g] = jnp.transpose(ut_ref[g]).astype(u_ref.dtype)


def _s5_token_major_kernel(y_ref, o_ref, zt_ref, z_ref):
    n_g, mt, ti = y_ref.shape
    t_len = S5_CHUNK
    n_i = ti // t_len
    n_lt = z_ref.shape[0]
    for g in range(n_g):
        yt = jnp.transpose(y_ref[g])
        zt_ref[:, g * n_i:(g + 1) * n_i, :] = yt.reshape(t_len, n_i, mt)
    for t in range(t_len):
        for j in range(n_lt):
            z_ref[j, pl.ds(t, mt, stride=t_len), :] = jnp.transpose(zt_ref[t, j * LANES:(j + 1) * LANES, :])
    for j in range(n_lt):
        o_ref[:, j * LANES:(j + 1) * LANES] = z_ref[j]


def _gelu_tanh(x):
    return 0.5 * x * (1.0 + jnp.tanh(math.sqrt(2.0 / math.pi) * (x + 0.044715 * (x * x * x))))


def _s5_glu_kernel(h_ref, y_ref, g_ref, d_ref, w_ref, o_ref):
    h = h_ref[...]
    dm = h.shape[1]
    y = y_ref[...] + d_ref[...] * _rms(h, g_ref[...])
    gl = _gelu_tanh(y).astype(BF16)
    a = _dot(gl, w_ref[:, 0:dm])
    b = _dot(gl, w_ref[:, dm:2 * dm])
    o_ref[...] = h + a * jax.nn.sigmoid(b)


def s5_mix(h3, gain, b_re, b_im, lam_re, lam_im, log_step, c_re, c_im, d_skip, w_glu):
    bsz, l, dm = h3.shape
    n_g, n_p, n_i = b_re.shape
    t_len = S5_CHUNK
    assert l % t_len == 0 and n_g * n_i == dm and n_p == S5_STATE and n_i == S5_GROUP_CH
    n_chunks = -(-(l // t_len) // S5_SCAN_ROWS) * S5_SCAN_ROWS
    m = bsz * n_chunks
    ti = t_len * n_i
    p2 = 2 * n_p
    n = bsz * l
    tm = _row_tile(n, 608)
    h2 = h3.reshape(n, dm)
    gain2 = gain.astype(F32).reshape(1, dm)

    mt = min(S5_RELAYOUT_CHUNKS, n_chunks)
    n_tiles = -(-n_chunks // mt)
    u = pl.pallas_call(
        functools.partial(_s5_group_major_kernel, seq_len=l),
        grid=(bsz, n_tiles),
        in_specs=[pl.BlockSpec((None, mt * t_len, dm), lambda b, i: (b, i, 0)),
                  pl.BlockSpec((1, dm), lambda b, i: (0, 0))],
        out_specs=pl.BlockSpec((n_g, None, mt, ti), lambda b, i: (0, b, i, 0)),
        out_shape=jax.ShapeDtypeStruct((n_g, bsz, n_chunks, ti), BF16),
        scratch_shapes=[pltpu.VMEM((dm // LANES, mt * t_len, LANES), F32), pltpu.VMEM((n_g, ti, mt), F32)],
        compiler_params=_cparams("parallel", "parallel"),
        name="s5_group_major",
    )(h3, gain2).reshape(n_g, m, ti)

    f32 = lambda a: a.astype(F32)
    lam_r = jnp.stack([jnp.concatenate([f32(lam_re)[0], f32(lam_re)[1]], axis=-1),
                       jnp.concatenate([f32(lam_im)[0], f32(lam_im)[1]], axis=-1)], axis=1)
    step_r = jnp.repeat(jnp.transpose(f32(log_step))[:, None, :], n_p, axis=2)
    bt = jnp.stack([jnp.transpose(f32(b_re), (0, 2, 1)), jnp.transpose(f32(b_im), (0, 2, 1))], axis=1)
    bt = jnp.tile(bt, (1, 1, 1, 2))
    ct = jnp.stack([f32(c_re), f32(c_im)], axis=0)
    ct = jnp.transpose(ct, (2, 0, 3, 1, 4)).reshape(n_g, 2, n_i, p2)

    gspec = lambda *shape: pl.BlockSpec((None,) + shape, lambda g: (g,) + (0,) * len(shape))
    kmat, wst, cexp, alpha = pl.pallas_call(
        _s5_param_kernel,
        grid=(n_g,),
        in_specs=[gspec(2, p2), gspec(1, p2), gspec(2, n_i, p2), gspec(2, n_i, p2)],
        out_specs=[gspec(ti, ti), gspec(ti, 2 * p2), gspec(2 * p2, ti), gspec(2, p2)],
        out_shape=[
            jax.ShapeDtypeStruct((n_g, ti, ti), BF16),
            jax.ShapeDtypeStruct((n_g, ti, 2 * p2), BF16),
            jax.ShapeDtypeStruct((n_g, 2 * p2, ti), BF16),
            jax.ShapeDtypeStruct((n_g, 2, p2), F32),
        ],
        compiler_params=_cparams("parallel"),
        name="s5_params",
    )(lam_r, step_r, bt, ct)

    gb = S5_GROUPS_PER_STEP
    assert n_g % gb == 0
    gbspec = lambda *shape: pl.BlockSpec((gb,) + shape, lambda g: (g,) + (0,) * len(shape))
    y = pl.pallas_call(
        functools.partial(_s5_main_kernel, n_batch=bsz, n_chunks=n_chunks),
        grid=(n_g // gb,),
        in_specs=[gbspec(m, ti), gbspec(ti, ti), gbspec(ti, 2 * p2), gbspec(2 * p2, ti), gbspec(2, p2)],
        out_specs=gbspec(m, ti),
        out_shape=jax.ShapeDtypeStruct((n_g, m, ti), F32),
        scratch_shapes=[pltpu.VMEM((gb, m, 2 * p2), F32)] * 3,
        compiler_params=_cparams("parallel"),
        name="s5_main",
    )(u, kmat, wst, cexp, alpha)
    y2 = pl.pallas_call(
        _s5_token_major_kernel,
        grid=(bsz, n_tiles),
        in_specs=[pl.BlockSpec((n_g, None, mt, ti), lambda b, i: (0, b, i, 0))],
        out_specs=pl.BlockSpec((None, mt * t_len, dm), lambda b, i: (b, i, 0)),
        out_shape=jax.ShapeDtypeStruct((bsz, l, dm), F32),
        scratch_shapes=[pltpu.VMEM((t_len, dm, mt), F32), pltpu.VMEM((dm // LANES, mt * t_len, LANES), F32)],
        compiler_params=_cparams("parallel", "parallel"),
        name="s5_token_major",
    )(y.reshape(n_g, bsz, n_chunks, ti)).reshape(n, dm)

    out = pl.pallas_call(
        _s5_glu_kernel,
        grid=(n // tm,),
        in_specs=[
            pl.BlockSpec((tm, dm), lambda i: (i, 0)),
            pl.BlockSpec((tm, dm), lambda i: (i, 0)),
            pl.BlockSpec((1, dm), lambda i: (0, 0)),
            pl.BlockSpec((1, dm), lambda i: (0, 0)),
            pl.BlockSpec((dm, 2 * dm), lambda i: (0, 0)),
        ],
        out_specs=pl.BlockSpec((tm, dm), lambda i: (i, 0)),
        out_shape=jax.ShapeDtypeStruct((n, dm), F32),
        compiler_params=_cparams("parallel"),
        name="s5_glu",
    )(h2, y2, gain2, f32(d_skip).reshape(1, dm), w_glu.astype(BF16))
    return out.reshape(bsz, l, dm)


def na_rwkv_mix(h3, gain, w_in, w_out, rpb, mu, w0, w_up, a0, a_up, g_up, k_k, k_a, r_k, lnx_g, lnx_b):
    bsz, l, dm = h3.shape
    n = bsz * l
    h2 = h3.reshape(n, dm)
    n_qkv = 3 * (w_out.shape[0] // 2)
    qkv, rest = norm_inproj(h2, gain.astype(F32), w_in.astype(BF16), n_qkv, mu, l)
    na = na_attention(qkv.reshape(bsz, l, n_qkv), rpb)
    rw_parts = rwkv_mix(rest.reshape(bsz, l, -1), w0, w_up, a0, a_up, g_up, k_k, k_a, r_k)
    out = outproj_residual(h2, na.reshape(n, -1), rw_parts, lnx_g, lnx_b, w_out.astype(BF16))
    return out.reshape(bsz, l, dm)


def kernel(x, meta_tokens, norm_mix, norm_ffn, norm_final, mix_w_in, mix_w_out, na_rpb, rwkv_mu,
           rwkv_w0, rwkv_w_up, rwkv_a0, rwkv_a_up, rwkv_g_up, rwkv_k_k, rwkv_k_a, rwkv_r_k,
           rwkv_lnx_g, rwkv_lnx_b, s5_b_re, s5_b_im, s5_lambda_re, s5_lambda_im, s5_log_step,
           s5_c_re, s5_c_im, s5_d, s5_w_glu, moe_w_group, moe_b_group, moe_w_expert, moe_b_expert,
           moe_w1, moe_w3, moe_w2):
    bsz, _, dm = x.shape
    depth = norm_mix.shape[0]
    meta = jnp.broadcast_to(meta_tokens.astype(x.dtype)[None], (bsz,) + meta_tokens.shape)
    h = jnp.concatenate([meta, x], axis=1)
    l = h.shape[1]
    for layer in range(depth):
        i = layer // 2
        if layer % 2 == 0:
            h = na_rwkv_mix(h, norm_mix[layer], mix_w_in[i], mix_w_out[i], na_rpb[i], rwkv_mu[i], rwkv_w0[i],
                            rwkv_w_up[i], rwkv_a0[i], rwkv_a_up[i], rwkv_g_up[i], rwkv_k_k[i], rwkv_k_a[i],
                            rwkv_r_k[i], rwkv_lnx_g[i], rwkv_lnx_b[i])
        else:
            h = s5_mix(h, norm_mix[layer], s5_b_re[i], s5_b_im[i], s5_lambda_re[i], s5_lambda_im[i],
                       s5_log_step[i], s5_c_re[i], s5_c_im[i], s5_d[i], s5_w_glu[i])
        h = hierarchical_moe_residual(h.reshape(bsz * l, dm), norm_ffn[layer].astype(F32), moe_w_group[layer],
                                      moe_b_group[layer], moe_w_expert[layer], moe_b_expert[layer],
                                      moe_w1, moe_w3, moe_w2, layer).reshape(bsz, l, dm)
    return final_norm(h, norm_final.astype(F32))
```

```python
import functools
import math

import jax
import jax.numpy as jnp
from jax import lax
from jax.experimental import pallas as pl
from jax.experimental.pallas import tpu as pltpu

F32 = jnp.float32
BF16 = jnp.bfloat16
I32 = jnp.int32

N_META = 16
GRID_W = 64
HEAD_DIM = 64
NA_WIN_ROWS = 8
NA_WIN_COLS = 16
S5_GROUP_CH = 16
S5_STATE = 64
MOE_GROUPS = 4
MOE_PER_GROUP = 8
MOE_EXPERTS = MOE_GROUPS * MOE_PER_GROUP
NORM_EPS = 1e-6
RWKV_GN_EPS = 64e-5
NEG_INF = -1e30

LANES = 128
SUBLANES_BF16 = 16
VMEM_LIMIT_BYTES = 56 * 1024 * 1024

MOE_TILE = 256
MOE_DISPATCH_TOKENS = 608
MOE_COMBINE_TOKENS = 320
MOE_STAGE_SLOTS = 3
ROUTER_LANES = 128


def _cparams(*sem):
    return pltpu.CompilerParams(dimension_semantics=sem, vmem_limit_bytes=VMEM_LIMIT_BYTES)


def _row_tile(n, target):
    best = None
    for t in range(SUBLANES_BF16, min(n, target) + 1, SUBLANES_BF16):
        if n % t == 0:
            best = t
    assert best is not None, (n, target)
    return best


def _rms(x, gain):
    ms = jnp.mean(x * x, axis=-1, keepdims=True)
    return (x * lax.rsqrt(ms + NORM_EPS)) * gain


def _split_bf16(x):
    hi = x.astype(BF16)
    lo = (x - hi.astype(F32)).astype(BF16)
    return hi, lo


def _dot(a, b):
    return jnp.dot(a, b, preferred_element_type=F32)


def _dot_nt(a, b):
    return lax.dot_general(a, b, (((1,), (1,)), ((), ())), preferred_element_type=F32)


SHIFT_HALO = SUBLANES_BF16


def _norm_inproj_kernel(h_ref, hp_ref, hn_ref, g_ref, w_ref, mu_ref, qkv_ref, rest_ref, xn_ref, y_ref,
                        *, n_qkv, chunk, seq_len):
    tm = h_ref.shape[0]
    hl = SHIFT_HALO
    gain = g_ref[...]
    xn_ref[...] = jnp.concatenate([_rms(hp_ref[...], gain), _rms(h_ref[...], gain), _rms(hn_ref[...], gain)],
                                  axis=0).astype(BF16)
    t = (pl.program_id(0) * tm + lax.broadcasted_iota(I32, (tm, chunk), 0)) % seq_len
    has_prev = t > 0
    has_next = t < seq_len - 1
    n_all = w_ref.shape[1]
    for c in range(0, n_all, chunk):
        if c < n_qkv:
            qkv_ref[:, c:c + chunk] = _dot(xn_ref[hl:hl + tm, :], w_ref[:, c:c + chunk]).astype(BF16)
        else:
            y_ref[...] = _dot(xn_ref[...], w_ref[:, c:c + chunk])
            p = y_ref[hl:hl + tm, :]
            nb = 0.5 * (jnp.where(has_prev, y_ref[hl - 1:hl - 1 + tm, :], 0.0)
                        + jnp.where(has_next, y_ref[hl + 1:hl + 1 + tm, :], 0.0))
            rest_ref[:, c - n_qkv:c - n_qkv + chunk] = p + mu_ref[:, c - n_qkv:c - n_qkv + chunk] * (nb - p)


def norm_inproj(h2, gain, w_bf16, n_qkv, mu, seq_len):
    n, d = h2.shape
    n_all = w_bf16.shape[1]
    tm = _row_tile(n, 608)
    chunk = 256
    hl = SHIFT_HALO
    assert n_qkv % chunk == 0 and n_all % chunk == 0 and tm % hl == 0 and n % seq_len == 0
    per = tm // hl
    n_halo = n // hl
    return pl.pallas_call(
        functools.partial(_norm_inproj_kernel, n_qkv=n_qkv, chunk=chunk, seq_len=seq_len),
        grid=(n // tm,),
        in_specs=[
            pl.BlockSpec((tm, d), lambda i: (i, 0)),
            pl.BlockSpec((hl, d), lambda i: (jnp.maximum(i * per - 1, 0), 0)),
            pl.BlockSpec((hl, d), lambda i: (jnp.minimum((i + 1) * per, n_halo - 1), 0)),
            pl.BlockSpec((1, d), lambda i: (0, 0)),
            pl.BlockSpec((d, n_all), lambda i: (0, 0)),
            pl.BlockSpec((1, n_all - n_qkv), lambda i: (0, 0)),
        ],
        out_specs=[
            pl.BlockSpec((tm, n_qkv), lambda i: (i, 0)),
            pl.BlockSpec((tm, n_all - n_qkv), lambda i: (i, 0)),
        ],
        out_shape=[
            jax.ShapeDtypeStruct((n, n_qkv), BF16),
            jax.ShapeDtypeStruct((n, n_all - n_qkv), F32),
        ],
        scratch_shapes=[pltpu.VMEM((tm + 2 * hl, d), BF16), pltpu.VMEM((tm + 2 * hl, chunk), F32)],
        compiler_params=_cparams("parallel"),
        name="norm_inproj",
    )(h2, h2, h2, gain.reshape(1, d), w_bf16, mu.astype(F32).reshape(1, n_all - n_qkv))


def _outproj_kernel(h_ref, na_ref, yf_ref, yb_ref, bonf_ref, bonb_ref, g_ref, lg_ref, lb_ref, wa_ref, wb_ref, o_ref):
    rw = _rwkv_readout(yf_ref[...], yb_ref[...], bonf_ref[...], bonb_ref[...], g_ref[...], lg_ref[...], lb_ref[...])
    acc = _dot(na_ref[...], wa_ref[...])
    acc = acc + _dot(rw.astype(BF16), wb_ref[...])
    o_ref[...] = h_ref[...] + acc


def outproj_residual(h2, na, rwkv_parts, lnx_g, lnx_b, w_out_bf16):
    n, d = h2.shape
    ka, kb = na.shape[1], rwkv_parts[0].shape[1]
    tm = _row_tile(n, 608)
    rows = lambda w: pl.BlockSpec((tm, w), lambda i: (i, 0))
    whole = lambda r, c: pl.BlockSpec((r, c), lambda i: (0, 0))
    return pl.pallas_call(
        _outproj_kernel,
        grid=(n // tm,),
        in_specs=[rows(d), rows(ka)] + [rows(kb)] * 5 + [whole(1, kb), whole(1, kb), whole(ka, d), whole(kb, d)],
        out_specs=rows(d),
        out_shape=jax.ShapeDtypeStruct((n, d), F32),
        compiler_params=_cparams("parallel"),
        name="outproj_residual",
    )(h2, na, *rwkv_parts, lnx_g.astype(F32).reshape(1, kb), lnx_b.astype(F32).reshape(1, kb),
      w_out_bf16[:ka], w_out_bf16[ka:])


def _store_token_tiles(ref, x):
    rows = x.shape[0]
    s_n = x.shape[1] // LANES
    for s in range(s_n):
        ref[pl.ds(s, rows, stride=s_n), :] = x[:, s * LANES:(s + 1) * LANES]


def _load_token_tile_cols(ref, s, rows, s_n):
    return ref[pl.ds(s, rows, stride=s_n), :]


def _router_kernel(h_ref, g_ref, whi_ref, wlo_ref, b_ref, xn_ref, route_ref):
    xn = _rms(h_ref[...], g_ref[...])
    x_hi, x_lo = _split_bf16(xn)
    _store_token_tiles(xn_ref, xn)
    logits = (_dot(x_hi, whi_ref[...]) + _dot(x_hi, wlo_ref[...]) + _dot(x_lo, whi_ref[...])
              + b_ref[...])
    tm = logits.shape[0]
    lane = lax.broadcasted_iota(I32, (tm, ROUTER_LANES), 1)
    big = jnp.int32(ROUTER_LANES)

    is_g = lane < MOE_GROUPS
    lg = jnp.where(is_g, logits, -jnp.inf)
    eg = jnp.where(is_g, jnp.exp(lg - jnp.max(lg, axis=-1, keepdims=True)), 0.0)
    pg = eg / jnp.sum(eg, axis=-1, keepdims=True)
    p_grp = jnp.max(pg, axis=-1, keepdims=True)
    grp = jnp.min(jnp.where(is_g & (pg == p_grp), lane, big), axis=-1, keepdims=True)

    lo_lane = MOE_GROUPS + MOE_PER_GROUP * grp
    is_e = (lane >= lo_lane) & (lane < lo_lane + MOE_PER_GROUP)
    le = jnp.where(is_e, logits, -jnp.inf)
    ee = jnp.where(is_e, jnp.exp(le - jnp.max(le, axis=-1, keepdims=True)), 0.0)
    pe = jnp.where(is_e, ee / jnp.sum(ee, axis=-1, keepdims=True), -1.0)
    p1 = jnp.max(pe, axis=-1, keepdims=True)
    i1 = jnp.min(jnp.where(pe == p1, lane, big), axis=-1, keepdims=True)
    pe2 = jnp.where(lane == i1, -1.0, pe)
    p2 = jnp.max(pe2, axis=-1, keepdims=True)
    i2 = jnp.min(jnp.where(pe2 == p2, lane, big), axis=-1, keepdims=True)
    denom = p1 + p2
    g1 = p_grp * p1 / denom
    g2 = p_grp * p2 / denom
    e1 = (i1 - MOE_GROUPS).astype(F32)
    e2 = (i2 - MOE_GROUPS).astype(F32)
    route_ref[...] = jnp.where(lane == 0, e1, jnp.where(lane == 1, e2, jnp.where(lane == 2, g1, g2)))


def moe_router(h2, gain, w_group, b_group, w_expert, b_expert):
    n, d = h2.shape
    n_r = MOE_GROUPS + MOE_EXPERTS
    w_r = jnp.concatenate([w_group, jnp.transpose(w_expert, (1, 0, 2)).reshape(d, MOE_EXPERTS)], axis=1)
    w_r = jnp.pad(w_r.astype(F32), ((0, 0), (0, ROUTER_LANES - n_r)))
    w_hi, w_lo = _split_bf16(w_r)
    b_r = jnp.pad(jnp.concatenate([b_group, b_expert.reshape(-1)]).astype(F32), (0, ROUTER_LANES - n_r))
    tm = _row_tile(n, 608)
    return pl.pallas_call(
        _router_kernel,
        grid=(n // tm,),
        in_specs=[
            pl.BlockSpec((tm, d), lambda i: (i, 0)),
            pl.BlockSpec((1, d), lambda i: (0, 0)),
            pl.BlockSpec((d, ROUTER_LANES), lambda i: (0, 0)),
            pl.BlockSpec((d, ROUTER_LANES), lambda i: (0, 0)),
            pl.BlockSpec((1, ROUTER_LANES), lambda i: (0, 0)),
        ],
        out_specs=[
            pl.BlockSpec((tm * (d // LANES), LANES), lambda i: (i, 0)),
            pl.BlockSpec((tm, ROUTER_LANES), lambda i: (i, 0)),
        ],
        out_shape=[
            jax.ShapeDtypeStruct((n * (d // LANES), LANES), F32),
            jax.ShapeDtypeStruct((n, ROUTER_LANES), F32),
        ],
        compiler_params=_cparams("parallel"),
        name="moe_router",
    )(h2, gain.reshape(1, d), w_hi, w_lo, b_r.reshape(1, ROUTER_LANES))


def _moe_dispatch_kernel(tail_ref, n_used_ref, dst_ref, xn_hbm, xbuf_hbm, zero_ref, stage_ref, sem, lsem, zsem,
                         *, tm, s_n, n_blocks):
    i = pl.program_id(0)
    n_steps = pl.num_programs(0)
    tile_rows = zero_ref.shape[0]

    def zero_block(b, carry):
        pltpu.make_async_copy(zero_ref, xbuf_hbm.at[pl.ds(pl.multiple_of(b * tile_rows, tile_rows), tile_rows), :],
                              zsem).start()
        return carry

    def wait_zero_block(b, carry):
        pltpu.make_async_copy(zero_ref, xbuf_hbm.at[pl.ds(0, tile_rows), :], zsem).wait()
        return carry

    n_slots = stage_ref.shape[0]
    slot = i % n_slots
    nxt = (i + 1) % n_slots

    def load(step, s):
        return pltpu.make_async_copy(xn_hbm.at[pl.ds(pl.multiple_of(step * tm * s_n, s_n), tm * s_n), :],
                                     stage_ref.at[s], lsem.at[s])

    def wait_rows_out(s):
        for _ in range(2):
            pltpu.make_async_copy(stage_ref.at[s], xbuf_hbm.at[pl.ds(0, tm * s_n), :], sem.at[s]).wait()

    @pl.when(i == 0)
    def _():
        load(0, 0).start()
        zero_ref[...] = jnp.zeros_like(zero_ref)
        for e in range(tail_ref.shape[0]):
            @pl.when(tail_ref[e] >= 0)
            def _():
                pltpu.make_async_copy(zero_ref, xbuf_hbm.at[pl.ds(pl.multiple_of(tail_ref[e], s_n), tile_rows), :],
                                      zsem).start()
        lax.fori_loop(n_used_ref[0], n_blocks, zero_block, 0)
        for e in range(tail_ref.shape[0]):
            @pl.when(tail_ref[e] >= 0)
            def _():
                wait_zero_block(0, 0)
        lax.fori_loop(n_used_ref[0], n_blocks, wait_zero_block, 0)

    @pl.when(i >= n_slots - 1)
    def _():
        wait_rows_out(nxt)

    @pl.when(i + 1 < n_steps)
    def _():
        load(i + 1, nxt).start()

    load(i, slot).wait()
    for r in range(tm):
        src = stage_ref.at[slot, pl.ds(r * s_n, s_n), :]
        for k in range(2):
            dst = pl.multiple_of(dst_ref[k, r], s_n)
            pltpu.make_async_copy(src, xbuf_hbm.at[pl.ds(dst, s_n), :], sem.at[slot]).start(priority=k)

    @pl.when(i == n_steps - 1)
    def _():
        for back in range(n_slots - 1):
            @pl.when(i >= back)
            def _():
                wait_rows_out((i - back) % n_slots)


def moe_dispatch(xn_tiles, dst_tiles, tail_start, n_used, n_blocks, tm, s_n):
    n_steps = dst_tiles.shape[0]
    n_rows = n_blocks * MOE_TILE
    assert n_steps * tm * s_n == xn_tiles.shape[0]
    grid_spec = pltpu.PrefetchScalarGridSpec(
        num_scalar_prefetch=2,
        grid=(n_steps,),
        in_specs=[
            pl.BlockSpec((None, 2, tm), lambda i, tail, nu: (i, 0, 0), memory_space=pltpu.SMEM),
            pl.BlockSpec(memory_space=pl.ANY),
        ],
        out_specs=pl.BlockSpec(memory_space=pl.ANY),
        scratch_shapes=[
            pltpu.VMEM((MOE_TILE * s_n, LANES), F32),
            pltpu.VMEM((MOE_STAGE_SLOTS, tm * s_n, LANES), F32),
            pltpu.SemaphoreType.DMA((MOE_STAGE_SLOTS,)),
            pltpu.SemaphoreType.DMA((MOE_STAGE_SLOTS,)),
            pltpu.SemaphoreType.DMA(()),
        ],
    )
    return pl.pallas_call(
        functools.partial(_moe_dispatch_kernel, tm=tm, s_n=s_n, n_blocks=n_blocks),
        grid_spec=grid_spec,
        out_shape=jax.ShapeDtypeStruct((n_rows * s_n, LANES), F32),
        compiler_params=_cparams("arbitrary"),
        name="moe_dispatch",
    )(tail_start, n_used, dst_tiles, xn_tiles)


def _expert_kernel(blk_e_ref, n_used_ref, x_ref, w1_ref, w3_ref, w2_ref, y_ref, xb_ref, w1b_ref, w3b_ref, w2b_ref):
    i = pl.program_id(0)
    used = i < n_used_ref[0]
    tile, d = xb_ref.shape
    s_n = d // LANES
    prev_e = blk_e_ref[jnp.maximum(i - 1, 0)]
    fresh = (i == 0) | (blk_e_ref[i] != prev_e)

    @pl.when(used & fresh)
    def _():
        w1b_ref[...] = w1_ref[...].astype(BF16)
        w3b_ref[...] = w3_ref[...].astype(BF16)
        w2b_ref[...] = w2_ref[...].astype(BF16)

    @pl.when(used)
    def _():
        for s in range(s_n):
            xb_ref[:, s * LANES:(s + 1) * LANES] = _load_token_tile_cols(x_ref, s, tile, s_n).astype(BF16)
        x = xb_ref[...]
        a = _dot(x, w1b_ref[...])
        b = _dot(x, w3b_ref[...])
        hmid = (a * jax.nn.sigmoid(a) * b).astype(BF16)
        _store_token_tiles(y_ref, _dot(hmid, w2b_ref[...]))

    @pl.when(jnp.logical_not(used))
    def _():
        y_ref[...] = jnp.zeros_like(y_ref)


def moe_experts(xbuf, blk_e, n_used, w1, w3, w2, layer, n_blocks):
    d, f = w1.shape[2], w1.shape[3]
    s_n = d // LANES
    tile = MOE_TILE

    def w_map(i, blk_e_ref, n_used_ref):
        return (layer, blk_e_ref[i], 0, 0)

    def x_map(i, blk_e_ref, n_used_ref):
        return (jnp.minimum(i, jnp.maximum(n_used_ref[0] - 1, 0)), 0)

    grid_spec = pltpu.PrefetchScalarGridSpec(
        num_scalar_prefetch=2,
        grid=(n_blocks,),
        in_specs=[
            pl.BlockSpec((tile * s_n, LANES), x_map),
            pl.BlockSpec((None, None, d, f), w_map),
            pl.BlockSpec((None, None, d, f), w_map),
            pl.BlockSpec((None, None, f, d), w_map),
        ],
        out_specs=pl.BlockSpec((tile * s_n, LANES), lambda i, be, nu: (i, 0)),
        scratch_shapes=[
            pltpu.VMEM((tile, d), BF16),
            pltpu.VMEM((d, f), BF16),
            pltpu.VMEM((d, f), BF16),
            pltpu.VMEM((f, d), BF16),
        ],
    )
    return pl.pallas_call(
        _expert_kernel,
        grid_spec=grid_spec,
        out_shape=jax.ShapeDtypeStruct((n_blocks * tile * s_n, LANES), F32),
        compiler_params=_cparams("arbitrary"),
        name="moe_experts",
    )(blk_e, n_used, xbuf, w1, w3, w2)


def _moe_combine_kernel(src_ref, src_next_ref, h_ref, route_ref, y_hbm, o_ref, yg_ref, sem, *, n_steps):
    i = pl.program_id(0)
    slot = i & 1
    tm, d = h_ref.shape
    s_n = d // LANES

    def start_gather(ids_ref, dst_slot):
        for k in range(2):
            for r in range(tm):
                src = pl.multiple_of(ids_ref[k, r], s_n)
                pltpu.make_async_copy(y_hbm.at[pl.ds(src, s_n), :],
                                      yg_ref.at[dst_slot, k, pl.ds(r * s_n, s_n), :],
                                      sem.at[dst_slot]).start(priority=r % 2)

    @pl.when(i == 0)
    def _():
        start_gather(src_ref, 0)

    @pl.when(i + 1 < n_steps)
    def _():
        start_gather(src_next_ref, 1 - slot)

    for k in range(2):
        pltpu.make_async_copy(y_hbm.at[pl.ds(0, tm * s_n), :], yg_ref.at[slot, k], sem.at[slot]).wait()
    route = route_ref[...]
    g1 = route[:, 2:3]
    g2 = route[:, 3:4]
    for s in range(s_n):
        cols = slice(s * LANES, (s + 1) * LANES)
        o_ref[:, cols] = (h_ref[:, cols] + g1 * _load_token_tile_cols(yg_ref.at[slot, 0], s, tm, s_n)
                          + g2 * _load_token_tile_cols(yg_ref.at[slot, 1], s, tm, s_n))


def moe_combine(h2, route, y_tiles, src_tiles, tm):
    n, d = h2.shape
    s_n = d // LANES
    n_steps = src_tiles.shape[0]
    ids = lambda index: pl.BlockSpec((None, 2, tm), lambda i: (index(i), 0, 0), memory_space=pltpu.SMEM)
    return pl.pallas_call(
        functools.partial(_moe_combine_kernel, n_steps=n_steps),
        grid=(n_steps,),
        in_specs=[
            ids(lambda i: i),
            ids(lambda i: jnp.minimum(i + 1, n_steps - 1)),
            pl.BlockSpec((tm, d), lambda i: (i, 0)),
            pl.BlockSpec((tm, ROUTER_LANES), lambda i: (i, 0)),
            pl.BlockSpec(memory_space=pl.ANY),
        ],
        out_specs=pl.BlockSpec((tm, d), lambda i: (i, 0)),
        out_shape=jax.ShapeDtypeStruct((n, d), F32),
        scratch_shapes=[pltpu.VMEM((2, 2, tm * s_n, LANES), F32), pltpu.SemaphoreType.DMA((2,))],
        compiler_params=_cparams("arbitrary"),
        name="moe_combine",
    )(src_tiles, src_tiles, h2, route, y_tiles)


def hierarchical_moe_residual(h2, gain, w_group, b_group, w_expert, b_expert, w1, w3, w2, layer):
    n, d = h2.shape
    xn, route = moe_router(h2, gain, w_group, b_group, w_expert, b_expert)
    e_km = jnp.concatenate([route[:, 0], route[:, 1]]).astype(I32)
    n_assign = 2 * n
    onehot = (e_km[:, None] == jnp.arange(MOE_EXPERTS, dtype=I32)[None, :]).astype(I32)
    csum = jnp.cumsum(onehot, axis=0)
    counts = csum[-1]
    padded = (counts + MOE_TILE - 1) // MOE_TILE * MOE_TILE
    pad_end = jnp.cumsum(padded)
    pad_start = pad_end - padded
    dest = jnp.sum((csum - onehot + pad_start[None, :]) * onehot, axis=1)
    n_blocks = -(-n_assign // MOE_TILE) + MOE_EXPERTS
    blk_start = jnp.arange(n_blocks, dtype=I32) * MOE_TILE
    blk_e = jnp.minimum(jnp.sum((pad_end[None, :] <= blk_start[:, None]).astype(I32), axis=1),
                        MOE_EXPERTS - 1).astype(I32)
    n_used = (pad_end[-1] // MOE_TILE).astype(I32).reshape(1)
    s_n = d // LANES
    first = (dest * s_n).astype(I32)
    tiles = lambda tm: jnp.transpose(first.reshape(2, n // tm, tm), (1, 0, 2))
    tm_d = _row_tile(n, MOE_DISPATCH_TOKENS)
    tm_c = _row_tile(n, MOE_COMBINE_TOKENS)
    tail_start = jnp.where(counts > 0, (pad_end - MOE_TILE) * s_n, -1).astype(I32)
    xbuf = moe_dispatch(xn, tiles(tm_d), tail_start, n_used, n_blocks, tm_d, s_n)
    y = moe_experts(xbuf, blk_e, n_used, w1, w3, w2, layer, n_blocks)
    return moe_combine(h2, route, y, tiles(tm_c), tm_c)


def _final_norm_kernel(h_ref, g_ref, o_ref):
    o_ref[...] = _rms(h_ref[...], g_ref[...])


def final_norm(h3, gain):
    b, l, d = h3.shape
    t = l - N_META
    tm = _row_tile(t, 512)
    return pl.pallas_call(
        _final_norm_kernel,
        grid=(b, t // tm),
        in_specs=[
            pl.BlockSpec((None, pl.Element(tm), pl.Element(d)),
                         lambda bi, i: (bi, pl.multiple_of(N_META + i * tm, SUBLANES_BF16), 0)),
            pl.BlockSpec((1, d), lambda bi, i: (0, 0)),
        ],
        out_specs=pl.BlockSpec((None, tm, d), lambda bi, i: (bi, i, 0)),
        out_shape=jax.ShapeDtypeStruct((b, t, d), F32),
        compiler_params=_cparams("parallel", "parallel"),
        name="final_norm",
    )(h3, gain.reshape(1, d))


NA_QROWS = 8
NA_KROWS = 3 * NA_QROWS
NA_ROWS_PER_ITER = 4


def _na_kernel(q_ref, kw_ref, vw_ref, qm_ref, km_ref, vm_ref, bias_ref, o_ref, om_ref, *, rows, scale):
    blk = pl.program_id(1)
    tq = GRID_W
    n_pairs = q_ref.shape[1] // LANES
    base = jnp.clip(NA_QROWS * blk - NA_QROWS, 0, rows - NA_KROWS)
    lane = lax.broadcasted_iota(I32, (tq, LANES), 1)
    halves = [lane < HEAD_DIM, lane >= HEAD_DIM]

    pad = jnp.zeros((LANES - N_META, LANES), km_ref.dtype)
    k_meta = [jnp.concatenate([km_ref[:, p * LANES:(p + 1) * LANES], pad], axis=0) for p in range(n_pairs)]
    v_meta = [jnp.concatenate([vm_ref[:, p * LANES:(p + 1) * LANES], pad], axis=0) for p in range(n_pairs)]
    lane2 = lax.broadcasted_iota(I32, (2 * tq, LANES), 1)
    meta_bias2 = jnp.where(lane2 < N_META, 0.0, NEG_INF)

    n_win = NA_WIN_ROWS * GRID_W

    def row_body(jb, carry):
        colsl = [slice(p * LANES, (p + 1) * LANES) for p in range(n_pairs)]
        units = [(jj, p) for jj in range(NA_ROWS_PER_ITER) for p in range(n_pairs)]
        s_idx, koff, qoff = [], [], []
        for jj in range(NA_ROWS_PER_ITER):
            j = jb * NA_ROWS_PER_ITER + jj
            r = NA_QROWS * blk + j
            start = jnp.clip(r - NA_WIN_ROWS // 2, 0, rows - NA_WIN_ROWS)
            s_idx.append(start - r + (NA_WIN_ROWS - 1))
            koff.append(pl.multiple_of((start - base) * GRID_W, GRID_W))
            qoff.append(pl.multiple_of(j * tq, tq))
        q_pair = {(jj, p): q_ref[pl.ds(qoff[jj], tq), c]
                  for jj in range(NA_ROWS_PER_ITER) for p, c in enumerate(colsl)}
        k_ext = {(jj, p): jnp.concatenate([kw_ref[pl.ds(koff[jj], n_win), c], k_meta[p]], axis=0)
                 for jj in range(NA_ROWS_PER_ITER) for p, c in enumerate(colsl)}
        v_ext = {(jj, p): jnp.concatenate([vw_ref[pl.ds(koff[jj], n_win), c], v_meta[p]], axis=0)
                 for jj in range(NA_ROWS_PER_ITER) for p, c in enumerate(colsl)}
        qh = [_stack_heads(q_pair[jj, p], halves[0]) for jj, p in units]
        s = [_dot_nt(qh[u], k_ext[jj, p]) * scale + jnp.concatenate([bias_ref[p, s_idx[jj]], meta_bias2], axis=1)
             for u, (jj, p) in enumerate(units)]
        m = [jnp.max(x, axis=-1, keepdims=True) for x in s]
        e = [jnp.exp(x - mx) for x, mx in zip(s, m)]
        den = [jnp.sum(x, axis=-1, keepdims=True) for x in e]
        o = [_dot(e[u].astype(BF16), v_ext[jj, p]) / den[u] for u, (jj, p) in enumerate(units)]
        for u, (jj, p) in enumerate(units):
            o_ref[pl.ds(qoff[jj], tq), colsl[p]] = jnp.where(halves[0], o[u][0:tq], o[u][tq:2 * tq]).astype(o_ref.dtype)
        return carry

    lax.fori_loop(0, NA_QROWS // NA_ROWS_PER_ITER, row_body, 0)

    @pl.when(blk == 0)
    def _():
        lane_m = lax.broadcasted_iota(I32, (N_META, LANES), 1)
        for p in range(n_pairs):
            cols = slice(p * LANES, (p + 1) * LANES)
            q_pair = qm_ref[:, cols]
            kmp = km_ref[:, cols]
            vmp = vm_ref[:, cols]
            outs = []
            for hh in range(2):
                sel = (lane_m < HEAD_DIM) if hh == 0 else (lane_m >= HEAD_DIM)
                qp = jnp.where(sel, q_pair, jnp.zeros_like(q_pair))
                s_m = _dot_nt(qp, kmp) * scale
                p_m = jnp.exp(s_m - jnp.max(s_m, axis=-1, keepdims=True))
                den = jnp.sum(p_m, axis=-1, keepdims=True)
                outs.append(_dot(p_m.astype(BF16), vmp) / den)
            om_ref[:, cols] = jnp.where(lane_m < HEAD_DIM, outs[0], outs[1]).astype(om_ref.dtype)


def _na_bias_table(rpb):
    h = rpb.shape[0]
    c_ids = jnp.arange(GRID_W)
    c_start = jnp.clip(c_ids - NA_WIN_COLS // 2, 0, GRID_W - NA_WIN_COLS)
    in_band = (c_ids[None, :] >= c_start[:, None]) & (c_ids[None, :] < c_start[:, None] + NA_WIN_COLS)
    dc = jnp.clip(c_ids[None, :] - c_ids[:, None] + NA_WIN_COLS - 1, 0, 2 * NA_WIN_COLS - 2)
    tab = jnp.where(in_band[None, None], rpb.astype(F32)[:, :, dc], NEG_INF)
    win = jnp.stack([tab[:, s:s + NA_WIN_ROWS] for s in range(NA_WIN_ROWS)], axis=1)
    per_head = jnp.transpose(win, (0, 1, 3, 2, 4)).reshape(h // 2, 2, NA_WIN_ROWS, GRID_W, NA_WIN_ROWS * GRID_W)
    return jnp.transpose(per_head, (0, 2, 1, 3, 4)).reshape(h // 2, NA_WIN_ROWS, 2 * GRID_W, NA_WIN_ROWS * GRID_W)


def na_attention(qkv, rpb):
    b, l, w3 = qkv.shape
    w = w3 // 3
    t = l - N_META
    rows = t // GRID_W
    assert rows * GRID_W == t and rows % NA_QROWS == 0 and rows >= NA_KROWS
    tq = NA_QROWS * GRID_W
    tk = NA_KROWS * GRID_W
    bias = _na_bias_table(rpb)
    al = SUBLANES_BF16

    def q_map(bi, i):
        return (bi, pl.multiple_of(N_META + i * tq, al), 0)

    def kv_map(col):
        def f(bi, i):
            base = jnp.clip(NA_QROWS * i - NA_QROWS, 0, rows - NA_KROWS)
            return (bi, pl.multiple_of(N_META + base * GRID_W, al), col)
        return f

    def meta_map(col):
        return lambda bi, i: (bi, 0, col)

    el = pl.Element
    grid_out, meta_out = pl.pallas_call(
        functools.partial(_na_kernel, rows=rows, scale=HEAD_DIM ** -0.5),
        grid=(b, rows // NA_QROWS),
        in_specs=[
            pl.BlockSpec((None, el(tq), el(w)), q_map),
            pl.BlockSpec((None, el(tk), el(w)), kv_map(w)),
            pl.BlockSpec((None, el(tk), el(w)), kv_map(2 * w)),
            pl.BlockSpec((None, el(N_META), el(w)), meta_map(0)),
            pl.BlockSpec((None, el(N_META), el(w)), meta_map(w)),
            pl.BlockSpec((None, el(N_META), el(w)), meta_map(2 * w)),
            pl.BlockSpec(bias.shape, lambda bi, i: (0, 0, 0, 0)),
        ],
        out_specs=[
            pl.BlockSpec((None, tq, w), lambda bi, i: (bi, i, 0)),
            pl.BlockSpec((None, N_META, w), lambda bi, i: (bi, 0, 0)),
        ],
        out_shape=[
            jax.ShapeDtypeStruct((b, t, w), BF16),
            jax.ShapeDtypeStruct((b, N_META, w), BF16),
        ],
        compiler_params=_cparams("parallel", "arbitrary"),
        name="na_attention",
    )(qkv, qkv, qkv, qkv, qkv, qkv, bias)
    return jnp.concatenate([meta_out, grid_out], axis=1)


RWKV_CHUNK = 64
RWKV_TILE_CHUNKS = 2


def _split3_bf16(x):
    p1 = x.astype(BF16)
    r1 = x - p1.astype(F32)
    p2 = r1.astype(BF16)
    p3 = (r1 - p2.astype(F32)).astype(BF16)
    return p1, p2, p3


def _mm1(a, b):
    return _dot(a.astype(BF16), b.astype(BF16))


def _mm3(a, b):
    ah, al = _split_bf16(a)
    bh, bl = _split_bf16(b)
    return _dot(ah, bh) + _dot(ah, bl) + _dot(al, bh)


def _mm1_nt(a, b):
    return _dot_nt(a.astype(BF16), b.astype(BF16))


def _mm3_nt(a, b):
    ah, al = _split_bf16(a)
    bh, bl = _split_bf16(b)
    return _dot_nt(ah, bh) + _dot_nt(ah, bl) + _dot_nt(al, bh)


def _exact_left(mat_bf16, x):
    p1, p2, p3 = _split3_bf16(x)
    return _dot(mat_bf16, p1) + _dot(mat_bf16, p2) + _dot(mat_bf16, p3)


def _exact_right(x, mat_bf16):
    p1, p2, p3 = _split3_bf16(x)
    return _dot(p1, mat_bf16) + _dot(p2, mat_bf16) + _dot(p3, mat_bf16)


def _head_block_ones(width):
    ri = lax.broadcasted_iota(I32, (width, width), 0) // HEAD_DIM
    ci = lax.broadcasted_iota(I32, (width, width), 1) // HEAD_DIM
    return (ri == ci).astype(BF16)


def _head_sums(x, exact):
    ones_pair = _head_block_ones(LANES)
    tiles = []
    for p in range(x.shape[1] // LANES):
        xt = x[:, p * LANES:(p + 1) * LANES]
        tiles.append(_exact_right(xt, ones_pair) if exact else _dot(xt.astype(BF16), ones_pair))
    return jnp.concatenate(tiles, axis=1)


def _stack_heads(x, m0):
    z = jnp.zeros_like(x)
    return jnp.concatenate([jnp.where(m0, x, z), jnp.where(m0, z, x)], axis=0)


_MM_L4 = _mm1_nt
_MM_KT = _mm1_nt
_MM_SQ = _mm1
_MM_AP = _mm1
_MM_V = _mm1
_MM_Y = _mm1
_MM_UPD = _mm1


def _rwkv_chunk_maps(streams, c, n_sub):
    assert c == 64
    c2 = 2 * c
    lane = lax.broadcasted_iota(I32, (c, LANES), 1)
    m0 = lane < HEAD_DIM
    r_i = lax.broadcasted_iota(I32, (c2, c2), 0)
    c_i = lax.broadcasted_iota(I32, (c2, c2), 1)
    eye = (r_i == c_i).astype(F32)
    rel = r_i % c - c_i % c
    masks = {sg: (rel * sg > 0, rel * sg >= 0) for sg in {s["sign"] for s in streams}}
    items = [(j, q) for j in range(len(streams)) for q in range(n_sub)]

    def part(j, q, name):
        return _stack_heads(streams[j][name][q * c:(q + 1) * c], m0)

    lhs = {it: jnp.concatenate([part(*it, "kkp"), part(*it, "rp")], axis=0) for it in items}
    rhs = {it: jnp.concatenate([part(*it, "ki"), part(*it, "bi")], axis=0) for it in items}
    vs = {it: part(*it, "v") for it in items}
    kipcs = {it: part(*it, "kipc") for it in items}
    bipcs = {it: part(*it, "bipc") for it in items}
    l4 = {it: _MM_L4(lhs[it], rhs[it]) for it in items}
    m_kk, n1, m_rk, m_rb = {}, {}, {}, {}
    for it in items:
        strict, incl = masks[streams[it[0]]["sign"]]
        m = l4[it]
        m_kk[it] = jnp.where(strict, m[0:c2, 0:c2], 0.0)
        n1[it] = jnp.where(strict, m[0:c2, c2:2 * c2], 0.0)
        m_rk[it] = jnp.where(incl, m[c2:2 * c2, 0:c2], 0.0)
        m_rb[it] = jnp.where(incl, m[c2:2 * c2, c2:2 * c2], 0.0)
    n2 = {it: _MM_SQ(n1[it], n1[it]) for it in items}
    n4 = {it: _MM_SQ(n2[it], n2[it]) for it in items}
    n8 = {it: _MM_SQ(n4[it], n4[it]) for it in items}
    n16 = {it: _MM_SQ(n8[it], n8[it]) for it in items}
    n32 = {it: _MM_SQ(n16[it], n16[it]) for it in items}
    p1 = {it: (eye - n1[it]) + _MM_AP(eye - n1[it], n2[it]) for it in items}
    p2 = {it: eye + n4[it] + n8[it] + _MM_AP(n4[it], n8[it]) for it in items}
    p3 = {it: eye + n16[it] + n32[it] + _MM_AP(n16[it], n32[it]) for it in items}
    p23 = {it: _MM_AP(p2[it], p3[it]) for it in items}
    winv = {it: _MM_AP(p1[it], p23[it]) for it in items}
    mv = {it: _MM_V(m_kk[it], vs[it]) for it in items}
    mrv = {it: _MM_Y(m_rk[it], vs[it]) for it in items}
    wl = {it: _MM_AP(winv[it], lhs[it][0:c2]) for it in items}
    wmv = {it: _MM_AP(winv[it], mv[it]) for it in items}
    yl = {it: lhs[it][c2:2 * c2] - _MM_Y(m_rb[it], wl[it]) for it in items}
    y0 = {it: mrv[it] - _MM_Y(m_rb[it], wmv[it]) for it in items}
    g2 = {it: _MM_UPD(jnp.transpose(wl[it]), bipcs[it]) for it in items}
    hh = {it: _MM_UPD(jnp.transpose(jnp.concatenate([vs[it], -wmv[it]], axis=0)),
                      jnp.concatenate([kipcs[it], bipcs[it]], axis=0)) for it in items}
    return {it: (yl[it], y0[it], g2[it], hh[it]) for it in items}


def _rwkv_apply_maps(streams, maps, c, n_sub):
    c2 = 2 * c
    st = [s["st"] for s in streams]
    ys = {}
    for k in range(n_sub):
        cur = [(j, k if s["sign"] > 0 else n_sub - 1 - k) for j, s in enumerate(streams)]
        sg = [_MM_UPD(st[j], maps[j][q][2]) for j, q in cur]
        yk = [_MM_KT(maps[j][q][0], st[j]) + maps[j][q][1] for j, q in cur]
        for it, yi in zip(cur, yk):
            ys[it] = yi[0:c] + yi[c:c2]
        st = [st[j] * streams[j]["pc"][q] - sgi + maps[j][q][3] for (j, q), sgi in zip(cur, sg)]
    return [(jnp.concatenate([ys[j, q] for q in range(n_sub)], axis=0), st[j]) for j in range(len(streams))]


def _softplus(z):
    return jnp.maximum(z, 0.0) + jnp.log(1.0 + jnp.exp(-jnp.abs(z)))


def _rwkv_tile_prep(x_ref, tile, seq_len, width, sign, w0, a0, w_wa, k_k, k_a, r_k):
    cs = RWKV_CHUNK
    c = x_ref.shape[0]
    valid = jnp.minimum(c, seq_len - tile * c)
    row = lax.broadcasted_iota(I32, (c, LANES), 0)
    rowv = row < valid
    lane = lax.broadcasted_iota(I32, (c, LANES), 1)
    ones_pair = _head_block_ones(LANES)

    def shifted(lo):
        return jnp.where(rowv, x_ref[:, lo:lo + LANES], 0.0)

    wa = shifted(3 * width)
    g_lo = shifted(3 * width + LANES)
    xwa = jnp.where(lane < LANES // 2, jnp.tanh(wa), wa)
    la = _dot(xwa.astype(BF16), w_wa)

    t_i = lax.broadcasted_iota(I32, (c, c), 0)
    s_i = lax.broadcasted_iota(I32, (c, c), 1)
    tri = ((t_i // cs == s_i // cs) & ((t_i - s_i) * sign >= 0)).astype(BF16)

    pairs = []
    for p in range(width // LANES):
        lo = p * LANES
        cols = slice(lo, lo + LANES)
        r = shifted(lo)
        k = shifted(width + lo)
        v = shifted(2 * width + lo)
        w_log = -_softplus(-(w0[:, cols] + la[:, cols])) - 0.5
        logw = jnp.where(rowv, -jnp.exp(w_log), 0.0)
        a = jax.nn.sigmoid(a0[:, cols] + la[:, width + lo:width + lo + LANES])
        kk0 = k * k_k[:, cols]
        ss = _dot((kk0 * kk0).astype(BF16), ones_pair)
        kk = kk0 / jnp.maximum(jnp.sqrt(ss), 1e-12)
        kdir = k * (1.0 + (a - 1.0) * k_a[:, cols])
        b = kk * a
        cl = _exact_left(tri, logw)
        lasts = [cl[q * cs + cs - 1:q * cs + cs, :] if sign > 0 else cl[q * cs:q * cs + 1, :]
                 for q in range(c // cs)]
        last = jnp.concatenate([jnp.broadcast_to(lq, (cs, LANES)) for lq in lasts], axis=0)
        e_n = jnp.exp(-cl)
        pcr = jnp.exp(last - cl)
        pairs.append(dict(kkp=kk * jnp.exp(cl - logw), rp=r * jnp.exp(cl), ki=kdir * e_n, bi=b * e_n,
                          kipc=kdir * pcr, bipc=b * pcr, v=v, pc=[jnp.exp(lq) for lq in lasts],
                          bonus=_dot((r * kdir * r_k[:, cols]).astype(BF16), ones_pair) * v))
    return pairs, g_lo


def _rwkv_scan_kernel(xf_ref, xb_ref, w0_ref, a0_ref, wwa_ref,
                      gup_ref, kk_ref, ka_ref, rk_ref, yf_ref, yb_ref, bonf_ref, bonb_ref, g_ref, st_ref,
                      *, seq_len, width):
    i = pl.program_id(1)
    n_chunks = pl.num_programs(1)
    n_pairs = width // LANES

    @pl.when(i == 0)
    def _():
        st_ref[...] = jnp.zeros_like(st_ref)

    tail = (kk_ref[...], ka_ref[...], rk_ref[...])
    n_sub = xf_ref.shape[0] // RWKV_CHUNK
    fwd, g_lo = _rwkv_tile_prep(xf_ref, i, seq_len, width, 1, w0_ref[0], a0_ref[0], wwa_ref[0], *tail)
    bwd, _ = _rwkv_tile_prep(xb_ref, n_chunks - 1 - i, seq_len, width, -1, w0_ref[1], a0_ref[1], wwa_ref[1], *tail)
    g_ref[...] = _mm1(jax.nn.sigmoid(g_lo), gup_ref[...]).astype(g_ref.dtype)
    streams = []
    for di, (pairs, sign, bon_ref) in enumerate(((fwd, 1, bonf_ref), (bwd, -1, bonb_ref))):
        for p, s in enumerate(pairs):
            bon_ref[:, p * LANES:(p + 1) * LANES] = s.pop("bonus")
            s["st"] = st_ref[di * n_pairs + p]
            s["sign"] = sign
            streams.append(s)
    m = _rwkv_chunk_maps(streams, RWKV_CHUNK, n_sub)
    maps = [{q: m[j, q] for q in range(n_sub)} for j in range(len(streams))]
    res = _rwkv_apply_maps(streams, maps, RWKV_CHUNK, n_sub)
    for j, (y, st_new) in enumerate(res):
        di, p = divmod(j, n_pairs)
        cols = slice(p * LANES, (p + 1) * LANES)
        (yf_ref if di == 0 else yb_ref)[:, cols] = y
        st_ref[j] = st_new


def _rwkv_readout(y_f, y_b, bon_f, bon_b, gate, lnx_g, lnx_b):
    y = y_f + y_b
    mean = _head_sums(y, exact=True) * (1.0 / HEAD_DIM)
    yc = y - mean
    var = _head_sums(yc * yc, exact=True) * (1.0 / HEAD_DIM)
    yn = yc * lax.rsqrt(var + RWKV_GN_EPS) * lnx_g + lnx_b
    return (yn + bon_f + bon_b) * gate.astype(F32)


def rwkv_mix(rest, w0, w_up, a0, a_up, g_up, k_k, k_a, r_k):
    bsz, l, n_cols = rest.shape
    width = w0.shape[1]
    rank = w_up.shape[1]
    assert n_cols == 3 * width + 2 * LANES and 2 * rank == LANES
    c = RWKV_CHUNK * RWKV_TILE_CHUNKS
    n_chunks = -(-l // c)
    zeros = jnp.zeros((2, rank, width), F32)
    w_wa = jnp.concatenate([jnp.concatenate([w_up.astype(F32), zeros], axis=2),
                            jnp.concatenate([zeros, a_up.astype(F32)], axis=2)], axis=1)
    w_wa = w_wa.astype(BF16)

    fwd_chunk = lambda i: i
    bwd_chunk = lambda i: n_chunks - 1 - i

    def tile_specs(chunk_of):
        return [pl.BlockSpec((None, c, n_cols), lambda b, i: (b, chunk_of(i), 0))]

    row2 = lambda a: a.astype(F32).reshape(1, -1)
    whole = lambda *shape: pl.BlockSpec(shape, lambda b, i: (0,) * len(shape))
    out_spec = lambda chunk_of: pl.BlockSpec((None, c, width), lambda b, i: (b, chunk_of(i), 0))
    act = lambda dt: jax.ShapeDtypeStruct((bsz, l, width), dt)
    y_f, y_b, bon_f, bon_b, g = pl.pallas_call(
        functools.partial(_rwkv_scan_kernel, seq_len=l, width=width),
        grid=(bsz, n_chunks),
        in_specs=tile_specs(fwd_chunk) + tile_specs(bwd_chunk) + [
            whole(2, 1, width), whole(2, 1, width),
            whole(2, LANES, 2 * width),
            whole(LANES, width),
            whole(1, width), whole(1, width), whole(1, width),
        ],
        out_specs=[out_spec(fwd_chunk), out_spec(bwd_chunk), out_spec(fwd_chunk), out_spec(bwd_chunk),
                   out_spec(fwd_chunk)],
        out_shape=[act(F32), act(F32), act(F32), act(F32), act(BF16)],
        scratch_shapes=[pltpu.VMEM((2 * (width // LANES), LANES, LANES), F32)],
        compiler_params=_cparams("parallel", "arbitrary"),
        name="rwkv_scan",
    )(rest, rest, w0.astype(F32).reshape(2, 1, width),
      a0.astype(F32).reshape(2, 1, width), w_wa, g_up.astype(BF16), row2(k_k), row2(k_a), row2(r_k))

    return tuple(a.reshape(bsz * l, width) for a in (y_f, y_b, bon_f, bon_b, g))


S5_CHUNK = 16


def _cpow(n, lr, li, step):
    mag = jnp.exp(n * (lr * step))
    ang = n * (li * step)
    return mag * jnp.cos(ang), mag * jnp.sin(ang)


def _s5_param_kernel(lamr_ref, stepr_ref, bt_ref, ct_ref, kmat_ref, wst_ref, cexp_ref, alpha_ref):
    t_len = S5_CHUNK
    n_i = S5_GROUP_CH
    p2 = 2 * S5_STATE
    ti = t_len * n_i

    lr = lamr_ref[0:1, :]
    li = lamr_ref[1:2, :]
    step = jnp.exp(stepr_ref[...])
    ab_re, ab_im = _cpow(1.0, lr, li, step)
    den = lr * lr + li * li
    z_re = ((ab_re - 1.0) * lr + ab_im * li) / den
    z_im = (ab_im * lr - (ab_re - 1.0) * li) / den
    t16 = lax.broadcasted_iota(I32, (t_len, p2), 0).astype(F32)
    is_f = lax.broadcasted_iota(I32, (t_len, p2), 1) < S5_STATE

    def rows_by_t(x):
        return jnp.concatenate([jnp.broadcast_to(x[t:t + 1], (n_i, p2)) for t in range(t_len)], axis=0)

    def tiled_rows(x):
        return jnp.concatenate([x] * t_len, axis=0)

    def pow_rows(n):
        q_re, q_im = _cpow(n, lr, li, step)
        return rows_by_t(q_re), rows_by_t(q_im)

    bt_re = tiled_rows(bt_ref[0])
    bt_im = tiled_rows(bt_ref[1])
    bb_re = z_re * bt_re - z_im * bt_im
    bb_im = z_re * bt_im + z_im * bt_re
    pw_re, pw_im = pow_rows(jnp.where(is_f, (t_len - 1.0) - t16, t16))
    wst_ref[:, 0:p2] = (pw_re * bb_re - pw_im * bb_im).astype(wst_ref.dtype)
    wst_ref[:, p2:2 * p2] = (pw_re * bb_im + pw_im * bb_re).astype(wst_ref.dtype)
    al_re, al_im = _cpow(float(t_len), lr, li, step)
    alpha_ref[0:1, :] = al_re
    alpha_ref[1:2, :] = al_im

    ct_re = tiled_rows(ct_ref[0])
    ct_im = tiled_rows(ct_ref[1])

    def c_times_pow(n):
        q_re, q_im = pow_rows(n)
        return jnp.transpose(ct_re * q_re - ct_im * q_im), jnp.transpose(ct_re * q_im + ct_im * q_re)

    ca_re, ca_im = c_times_pow(jnp.where(is_f, t16, jnp.where(t16 == 0.0, 0.0, t_len - t16)))
    lane_p = lax.broadcasted_iota(I32, (n_i, p2), 1)
    bbr = bb_re[0:n_i]
    bbi = bb_im[0:n_i]
    zero = jnp.zeros_like(bbr)
    strips = []
    for sel in (lane_p < S5_STATE, lane_p >= S5_STATE):
        strips.append(_mm3(jnp.where(sel, bbr, zero), ca_re) - _mm3(jnp.where(sel, bbi, zero), ca_im))
    strip_f, strip_b = strips
    t_k = lax.broadcasted_iota(I32, (n_i, ti), 1) // n_i
    for tt in range(t_len):
        sf = strip_f if tt == 0 else pltpu.roll(strip_f, tt * n_i, 1)
        sb = strip_b if tt == 0 else pltpu.roll(strip_b, tt * n_i, 1)
        blk = jnp.where(t_k >= tt, sf, 0.0) + jnp.where(t_k <= tt, sb, 0.0)
        kmat_ref[tt * n_i:(tt + 1) * n_i, :] = blk.astype(kmat_ref.dtype)

    co_re, co_im = c_times_pow(jnp.where(is_f, t16 + 1.0, t_len - t16))
    cexp_ref[0:p2, :] = co_re.astype(cexp_ref.dtype)
    cexp_ref[p2:2 * p2, :] = (-co_im).astype(cexp_ref.dtype)


def _s5_main_kernel(u_ref, kmat_ref, wst_ref, cexp_ref, alpha_ref, y_ref, x_ref, sf_ref, sb_ref,
                    *, n_batch, n_chunks):
    p2 = 2 * S5_STATE
    n_gb = u_ref.shape[0]
    for g in range(n_gb):
        x_ref[g] = _dot(u_ref[g].astype(BF16), wst_ref[g])
    lane = lax.broadcasted_iota(I32, (1, p2), 1)
    is_f = lane < S5_STATE
    alphas = [(alpha_ref[g, 0:1, :], alpha_ref[g, 1:2, :]) for g in range(n_gb)]

    sub = S5_SCAN_ROWS
    assert n_chunks % sub == 0
    chains = [(g, b) for g in range(n_gb) for b in range(n_batch)]

    def step(k, carry):
        new = []
        for (g, b), (s_re, s_im) in zip(chains, carry):
            a_re, a_im = alphas[g]
            row_f = pl.multiple_of(b * n_chunks + sub * k, sub)
            row_b = pl.multiple_of(b * n_chunks + (n_chunks - sub) - sub * k, sub)
            xf = x_ref[g, pl.ds(row_f, sub), :]
            xb = x_ref[g, pl.ds(row_b, sub), :]
            seen = []
            for r in range(sub):
                seen.append(jnp.concatenate([s_re, s_im], axis=1))
                rb = sub - 1 - r
                x_re = jnp.where(is_f, xf[r:r + 1, 0:p2], xb[rb:rb + 1, 0:p2])
                x_im = jnp.where(is_f, xf[r:r + 1, p2:2 * p2], xb[rb:rb + 1, p2:2 * p2])
                s_re, s_im = a_re * s_re - a_im * s_im + x_re, a_re * s_im + a_im * s_re + x_im
            sf_ref[g, pl.ds(row_f, sub), :] = jnp.concatenate(seen, axis=0)
            sb_ref[g, pl.ds(row_b, sub), :] = jnp.concatenate(seen[::-1], axis=0)
            new.append((s_re, s_im))
        return tuple(new)

    zero = jnp.zeros((1, p2), F32)
    lax.fori_loop(0, n_chunks // sub, step, tuple((zero, zero) for _ in chains))
    lane2 = lax.broadcasted_iota(I32, sf_ref.shape[1:], 1) % p2
    for g in range(n_gb):
        s_in = jnp.where(lane2 < S5_STATE, sf_ref[g], sb_ref[g])
        s_hi, s_lo = _split_bf16(s_in)
        y_ref[g] = (_dot(u_ref[g].astype(BF16), kmat_ref[g]) + _dot(s_hi, cexp_ref[g]) + _dot(s_lo, cexp_ref[g]))


S5_RELAYOUT_CHUNKS = 128
S5_SCAN_ROWS = 8
S5_GROUPS_PER_STEP = 4


def _s5_group_major_kernel(h_ref, g_ref, u_ref, hn_ref, ut_ref, *, seq_len):
    n_g, mt, ti = u_ref.shape
    t_len = S5_CHUNK
    n_i = ti // t_len
    n_lt = hn_ref.shape[0]
    g_lt = LANES // n_i
    rows = h_ref.shape[0]
    valid = seq_len - pl.program_id(1) * rows
    row = lax.broadcasted_iota(I32, h_ref.shape, 0)
    hn = jnp.where(row < valid, _rms(h_ref[...], g_ref[...]), 0.0)
    for j in range(n_lt):
        hn_ref[j] = hn[:, j * LANES:(j + 1) * LANES]
    for tau in range(t_len):
        for j in range(n_lt):
            xt = jnp.transpose(hn_ref[j, pl.ds(tau, mt, stride=t_len), :])
            ut_ref[j * g_lt:(j + 1) * g_lt, tau * n_i:(tau + 1) * n_i, :] = xt.reshape(g_lt, n_i, mt)
    for g in range(n_g):
        u_ref[g] = jnp.transpose(ut_ref[g]).astype(u_ref.dtype)


def _s5_token_major_kernel(y_ref, o_ref, zt_ref, z_ref):
    n_g, mt, ti = y_ref.shape
    t_len = S5_CHUNK
    n_i = ti // t_len
    n_lt = z_ref.shape[0]
    for g in range(n_g):
        yt = jnp.transpose(y_ref[g])
        zt_ref[:, g * n_i:(g + 1) * n_i, :] = yt.reshape(t_len, n_i, mt)
    for t in range(t_len):
        for j in range(n_lt):
            z_ref[j, pl.ds(t, mt, stride=t_len), :] = jnp.transpose(zt_ref[t, j * LANES:(j + 1) * LANES, :])
    for j in range(n_lt):
        o_ref[:, j * LANES:(j + 1) * LANES] = z_ref[j]


def _gelu_tanh(x):
    return 0.5 * x * (1.0 + jnp.tanh(math.sqrt(2.0 / math.pi) * (x + 0.044715 * (x * x * x))))


def _s5_glu_kernel(h_ref, y_ref, g_ref, d_ref, w_ref, o_ref):
    h = h_ref[...]
    dm = h.shape[1]
    y = y_ref[...] + d_ref[...] * _rms(h, g_ref[...])
    gl = _gelu_tanh(y).astype(BF16)
    a = _dot(gl, w_ref[:, 0:dm])
    b = _dot(gl, w_ref[:, dm:2 * dm])
    o_ref[...] = h + a * jax.nn.sigmoid(b)


def s5_mix(h3, gain, b_re, b_im, lam_re, lam_im, log_step, c_re, c_im, d_skip, w_glu):
    bsz, l, dm = h3.shape
    n_g, n_p, n_i = b_re.shape
    t_len = S5_CHUNK
    assert l % t_len == 0 and n_g * n_i == dm and n_p == S5_STATE and n_i == S5_GROUP_CH
    n_chunks = -(-(l // t_len) // S5_SCAN_ROWS) * S5_SCAN_ROWS
    m = bsz * n_chunks
    ti = t_len * n_i
    p2 = 2 * n_p
    n = bsz * l
    tm = _row_tile(n, 608)
    h2 = h3.reshape(n, dm)
    gain2 = gain.astype(F32).reshape(1, dm)

    mt = min(S5_RELAYOUT_CHUNKS, n_chunks)
    n_tiles = -(-n_chunks // mt)
    u = pl.pallas_call(
        functools.partial(_s5_group_major_kernel, seq_len=l),
        grid=(bsz, n_tiles),
        in_specs=[pl.BlockSpec((None, mt * t_len, dm), lambda b, i: (b, i, 0)),
                  pl.BlockSpec((1, dm), lambda b, i: (0, 0))],
        out_specs=pl.BlockSpec((n_g, None, mt, ti), lambda b, i: (0, b, i, 0)),
        out_shape=jax.ShapeDtypeStruct((n_g, bsz, n_chunks, ti), BF16),
        scratch_shapes=[pltpu.VMEM((dm // LANES, mt * t_len, LANES), F32), pltpu.VMEM((n_g, ti, mt), F32)],
        compiler_params=_cparams("parallel", "parallel"),
        name="s5_group_major",
    )(h3, gain2).reshape(n_g, m, ti)

    f32 = lambda a: a.astype(F32)
    lam_r = jnp.stack([jnp.concatenate([f32(lam_re)[0], f32(lam_re)[1]], axis=-1),
                       jnp.concatenate([f32(lam_im)[0], f32(lam_im)[1]], axis=-1)], axis=1)
    step_r = jnp.repeat(jnp.transpose(f32(log_step))[:, None, :], n_p, axis=2)
    bt = jnp.stack([jnp.transpose(f32(b_re), (0, 2, 1)), jnp.transpose(f32(b_im), (0, 2, 1))], axis=1)
    bt = jnp.tile(bt, (1, 1, 1, 2))
    ct = jnp.stack([f32(c_re), f32(c_im)], axis=0)
    ct = jnp.transpose(ct, (2, 0, 3, 1, 4)).reshape(n_g, 2, n_i, p2)

    gspec = lambda *shape: pl.BlockSpec((None,) + shape, lambda g: (g,) + (0,) * len(shape))
    kmat, wst, cexp, alpha = pl.pallas_call(
        _s5_param_kernel,
        grid=(n_g,),
        in_specs=[gspec(2, p2), gspec(1, p2), gspec(2, n_i, p2), gspec(2, n_i, p2)],
        out_specs=[gspec(ti, ti), gspec(ti, 2 * p2), gspec(2 * p2, ti), gspec(2, p2)],
        out_shape=[
            jax.ShapeDtypeStruct((n_g, ti, ti), BF16),
            jax.ShapeDtypeStruct((n_g, ti, 2 * p2), BF16),
            jax.ShapeDtypeStruct((n_g, 2 * p2, ti), BF16),
            jax.ShapeDtypeStruct((n_g, 2, p2), F32),
        ],
        compiler_params=_cparams("parallel"),
        name="s5_params",
    )(lam_r, step_r, bt, ct)

    gb = S5_GROUPS_PER_STEP
    assert n_g % gb == 0
    gbspec = lambda *shape: pl.BlockSpec((gb,) + shape, lambda g: (g,) + (0,) * len(shape))
    y = pl.pallas_call(
        functools.partial(_s5_main_kernel, n_batch=bsz, n_chunks=n_chunks),
        grid=(n_g // gb,),
        in_specs=[gbspec(m, ti), gbspec(ti, ti), gbspec(ti, 2 * p2), gbspec(2 * p2, ti), gbspec(2, p2)],
        out_specs=gbspec(m, ti),
        out_shape=jax.ShapeDtypeStruct((n_g, m, ti), F32),
        scratch_shapes=[pltpu.VMEM((gb, m, 2 * p2), F32)] * 3,
        compiler_params=_cparams("parallel"),
        name="s5_main",
    )(u, kmat, wst, cexp, alpha)
    y2 = pl.pallas_call(
        _s5_token_major_kernel,
        grid=(bsz, n_tiles),
        in_specs=[pl.BlockSpec((n_g, None, mt, ti), lambda b, i: (0, b, i, 0))],
        out_specs=pl.BlockSpec((None, mt * t_len, dm), lambda b, i: (b, i, 0)),
        out_shape=jax.ShapeDtypeStruct((bsz, l, dm), F32),
        scratch_shapes=[pltpu.VMEM((t_len, dm, mt), F32), pltpu.VMEM((dm // LANES, mt * t_len, LANES), F32)],
        compiler_params=_cparams("parallel", "parallel"),
        name="s5_token_major",
    )(y.reshape(n_g, bsz, n_chunks, ti)).reshape(n, dm)

    out = pl.pallas_call(
        _s5_glu_kernel,
        grid=(n // tm,),
        in_specs=[
            pl.BlockSpec((tm, dm), lambda i: (i, 0)),
            pl.BlockSpec((tm, dm), lambda i: (i, 0)),
            pl.BlockSpec((1, dm), lambda i: (0, 0)),
            pl.BlockSpec((1, dm), lambda i: (0, 0)),
            pl.BlockSpec((dm, 2 * dm), lambda i: (0, 0)),
        ],
        out_specs=pl.BlockSpec((tm, dm), lambda i: (i, 0)),
        out_shape=jax.ShapeDtypeStruct((n, dm), F32),
        compiler_params=_cparams("parallel"),
        name="s5_glu",
    )(h2, y2, gain2, f32(d_skip).reshape(1, dm), w_glu.astype(BF16))
    return out.reshape(bsz, l, dm)


def na_rwkv_mix(h3, gain, w_in, w_out, rpb, mu, w0, w_up, a0, a_up, g_up, k_k, k_a, r_k, lnx_g, lnx_b):
    bsz, l, dm = h3.shape
    n = bsz * l
    h2 = h3.reshape(n, dm)
    n_qkv = 3 * (w_out.shape[0] // 2)
    qkv, rest = norm_inproj(h2, gain.astype(F32), w_in.astype(BF16), n_qkv, mu, l)
    na = na_attention(qkv.reshape(bsz, l, n_qkv), rpb)
    rw_parts = rwkv_mix(rest.reshape(bsz, l, -1), w0, w_up, a0, a_up, g_up, k_k, k_a, r_k)
    out = outproj_residual(h2, na.reshape(n, -1), rw_parts, lnx_g, lnx_b, w_out.astype(BF16))
    return out.reshape(bsz, l, dm)


def kernel(x, meta_tokens, norm_mix, norm_ffn, norm_final, mix_w_in, mix_w_out, na_rpb, rwkv_mu,
           rwkv_w0, rwkv_w_up, rwkv_a0, rwkv_a_up, rwkv_g_up, rwkv_k_k, rwkv_k_a, rwkv_r_k,
           rwkv_lnx_g, rwkv_lnx_b, s5_b_re, s5_b_im, s5_lambda_re, s5_lambda_im, s5_log_step,
           s5_c_re, s5_c_im, s5_d, s5_w_glu, moe_w_group, moe_b_group, moe_w_expert, moe_b_expert,
           moe_w1, moe_w3, moe_w2):
    bsz, _, dm = x.shape
    depth = norm_mix.shape[0]
    meta = jnp.broadcast_to(meta_tokens.astype(x.dtype)[None], (bsz,) + meta_tokens.shape)
    h = jnp.concatenate([meta, x], axis=1)
    l = h.shape[1]
    for layer in range(depth):
        i = layer // 2
        if layer % 2 == 0:
            h = na_rwkv_mix(h, norm_mix[layer], mix_w_in[i], mix_w_out[i], na_rpb[i], rwkv_mu[i], rwkv_w0[i],
                            rwkv_w_up[i], rwkv_a0[i], rwkv_a_up[i], rwkv_g_up[i], rwkv_k_k[i], rwkv_k_a[i],
                            rwkv_r_k[i], rwkv_lnx_g[i], rwkv_lnx_b[i])
        else:
            h = s5_mix(h, norm_mix[layer], s5_b_re[i], s5_b_im[i], s5_lambda_re[i], s5_lambda_im[i],
                       s5_log_step[i], s5_c_re[i], s5_c_im[i], s5_d[i], s5_w_glu[i])
        h = hierarchical_moe_residual(h.reshape(bsz * l, dm), norm_ffn[layer].astype(F32), moe_w_group[layer],
                                      moe_b_group[layer], moe_w_expert[layer], moe_b_expert[layer],
                                      moe_w1, moe_w3, moe_w2, layer).reshape(bsz, l, dm)
    return final_norm(h, norm_final.astype(F32))
```
